```python
import math
import jax, jax.numpy as jnp
from jax import lax
import numpy as np

D_MODEL = 1024
BATCH = 8
SEQ = 8192
DEPTH = 4

ATT_HEADS = 8
KV_HEADS = 2
GQA_GROUP = ATT_HEADS // KV_HEADS
HEAD_DIM = 64
ATT_WIDTH = ATT_HEADS * HEAD_DIM
KV_WIDTH = KV_HEADS * HEAD_DIM
WINDOW = 128
BLOCK = 128
SSM_WIDTH = D_MODEL - ATT_WIDTH
SSM_GROUP = 16
SSM_GROUPS = SSM_WIDTH // SSM_GROUP
SSM_STATE = 64
DT_MIN = 1e-3
DT_MAX = 1e-1
MIX_WIDTH = ATT_WIDTH + SSM_WIDTH
IN_WIDTH = ATT_WIDTH + 2 * KV_WIDTH + SSM_WIDTH
D_FF = 4 * D_MODEL
EPS = 1e-6

kernel_name = "hymba_style_swa_s5_hybrid_encoder"


def rms_norm(x, gain):
    xf = x.astype(jnp.float32)
    y = xf * lax.rsqrt(jnp.mean(xf * xf, axis=-1, keepdims=True) + EPS)
    return (y * gain.astype(jnp.float32)).astype(x.dtype)


def alibi_slopes(n_heads):
    return jnp.exp2(-8.0 * jnp.arange(1, n_heads + 1, dtype=jnp.float32) / n_heads)


def windowed_gqa(q, k, v, q_gain, k_gain, sink):
    bsz, seq = q.shape[0], q.shape[1]
    nb = seq // BLOCK
    q = rms_norm(q, q_gain)
    k = rms_norm(k, k_gain)
    qb = q.reshape(bsz, nb, BLOCK, KV_HEADS, GQA_GROUP, HEAD_DIM)

    def band(t):
        tp = jnp.pad(t, ((0, 0), (BLOCK, BLOCK), (0, 0), (0, 0)))
        tb = tp.reshape(bsz, nb + 2, BLOCK, KV_HEADS, HEAD_DIM)
        return jnp.concatenate([tb[:, :-2], tb[:, 1:-1], tb[:, 2:]], axis=2)

    kb, vb = band(k), band(v)
    scores = jnp.einsum('bnqkgd,bnckd->bnkgqc', qb, kb,
                        preferred_element_type=jnp.float32) / math.sqrt(HEAD_DIM)
    q_idx = jnp.arange(BLOCK)[:, None]
    c_idx = jnp.arange(3 * BLOCK)[None, :]
    dist = jnp.abs(q_idx - c_idx + BLOCK)
    key_pos = (jnp.arange(nb)[:, None] - 1) * BLOCK + jnp.arange(3 * BLOCK)[None, :]
    valid = (dist <= WINDOW)[None] & ((key_pos >= 0) & (key_pos < seq))[:, None, :]
    slopes = alibi_slopes(ATT_HEADS).reshape(KV_HEADS, GQA_GROUP)
    bias = -slopes[:, :, None, None] * dist.astype(jnp.float32)
    neg = jnp.finfo(jnp.float32).min
    scores = jnp.where(valid[None, :, None, None], scores + bias, neg)
    sk = sink.astype(jnp.float32).reshape(1, 1, KV_HEADS, GQA_GROUP, 1, 1)
    m = jnp.maximum(jnp.max(scores, axis=-1, keepdims=True), sk)
    p = jnp.exp(scores - m)
    denom = jnp.sum(p, axis=-1, keepdims=True) + jnp.exp(sk - m)
    out = jnp.einsum('bnkgqc,bnckd->bnqkgd', (p / denom).astype(v.dtype), vb)
    return out.reshape(bsz, seq, ATT_WIDTH)


def complex_diag_scan(a_re, a_im, b_re, b_im, reverse):
    ar = jnp.broadcast_to(a_re, b_re.shape)
    ai = jnp.broadcast_to(a_im, b_re.shape)

    def combine(e1, e2):
        a1r, a1i, b1r, b1i = e1
        a2r, a2i, b2r, b2i = e2
        return (a1r * a2r - a1i * a2i,
                a1r * a2i + a1i * a2r,
                a2r * b1r - a2i * b1i + b2r,
                a2r * b1i + a2i * b1r + b2i)

    _, _, xr, xi = lax.associative_scan(combine, (ar, ai, b_re, b_im), reverse=reverse, axis=1)
    return xr, xi


def s5_mixer(u, lam_re, lam_im, log_dt, b_re, b_im, c_re, c_im, d_skip, w_glu):
    bsz, seq = u.shape[0], u.shape[1]
    uf = u.astype(jnp.float32).reshape(bsz, seq, SSM_GROUPS, SSM_GROUP)
    y = d_skip.astype(jnp.float32).reshape(SSM_GROUPS, SSM_GROUP) * uf
    br = b_re.astype(jnp.float32)
    bi = b_im.astype(jnp.float32)
    for direction, reverse in enumerate((False, True)):
        lr = lam_re[direction].astype(jnp.float32)
        li = lam_im[direction].astype(jnp.float32)
        dt = jnp.exp(log_dt[direction].astype(jnp.float32))[:, None]
        mag = jnp.exp(lr * dt)
        abr = mag * jnp.cos(li * dt)
        abi = mag * jnp.sin(li * dt)
        den = lr * lr + li * li
        zr = ((abr - 1.0) * lr + abi * li) / den
        zi = (abi * lr - (abr - 1.0) * li) / den
        bbr = zr[..., None] * br - zi[..., None] * bi
        bbi = zr[..., None] * bi + zi[..., None] * br
        bur = jnp.einsum('bsgh,gph->bsgp', uf, bbr)
        bui = jnp.einsum('bsgh,gph->bsgp', uf, bbi)
        xr, xi = complex_diag_scan(abr, abi, bur, bui, reverse)
        y = (y + jnp.einsum('bsgp,ghp->bsgh', xr, c_re[direction].astype(jnp.float32))
             - jnp.einsum('bsgp,ghp->bsgh', xi, c_im[direction].astype(jnp.float32)))
    y = jax.nn.gelu(y).reshape(bsz, seq, SSM_WIDTH).astype(u.dtype)
    g_val, g_gate = jnp.split(y @ w_glu, 2, axis=-1)
    return g_val * jax.nn.sigmoid(g_gate)


def _fwd_setup_inputs(seed: int = 0) -> dict:
    key = jax.random.key(seed)
    ks = jax.random.split(key, 20)
    nrm = jax.random.normal
    f32 = jnp.float32
    x = nrm(ks[0], (BATCH, SEQ, D_MODEL), f32)
    norm1 = 1.0 + 0.05 * nrm(ks[1], (DEPTH, D_MODEL), f32)
    w_in = nrm(ks[2], (DEPTH, D_MODEL, IN_WIDTH), f32) * D_MODEL ** -0.5
    q_gain = 1.0 + 0.05 * nrm(ks[3], (DEPTH, HEAD_DIM), f32)
    k_gain = 1.0 + 0.05 * nrm(ks[4], (DEPTH, HEAD_DIM), f32)
    sink = 0.5 * nrm(ks[5], (DEPTH, ATT_HEADS), f32)
    lam_re = -0.5 + 0.01 * nrm(ks[6], (DEPTH, 2, SSM_GROUPS, SSM_STATE), f32)
    lam_im = (math.pi * jnp.arange(SSM_STATE, dtype=f32)
              + 0.01 * nrm(ks[7], (DEPTH, 2, SSM_GROUPS, SSM_STATE), f32))
    log_dt = jax.random.uniform(ks[8], (DEPTH, 2, SSM_GROUPS), f32,
                                minval=math.log(DT_MIN), maxval=math.log(DT_MAX))
    b_re = nrm(ks[9], (DEPTH, SSM_GROUPS, SSM_STATE, SSM_GROUP), f32) * (2 * SSM_GROUP) ** -0.5
    b_im = nrm(ks[10], (DEPTH, SSM_GROUPS, SSM_STATE, SSM_GROUP), f32) * (2 * SSM_GROUP) ** -0.5
    c_re = nrm(ks[11], (DEPTH, 2, SSM_GROUPS, SSM_GROUP, SSM_STATE), f32) * SSM_STATE ** -0.5
    c_im = nrm(ks[12], (DEPTH, 2, SSM_GROUPS, SSM_GROUP, SSM_STATE), f32) * SSM_STATE ** -0.5
    d_skip = nrm(ks[13], (DEPTH, SSM_WIDTH), f32)
    w_glu = nrm(ks[14], (DEPTH, SSM_WIDTH, 2 * SSM_WIDTH), f32) * SSM_WIDTH ** -0.5
    w_out = nrm(ks[15], (DEPTH, MIX_WIDTH, D_MODEL), f32) * (0.5 * MIX_WIDTH ** -0.5)
    norm2 = 1.0 + 0.05 * nrm(ks[16], (DEPTH, D_MODEL), f32)
    w_ff1 = nrm(ks[17], (DEPTH, D_MODEL, D_FF), f32) * D_MODEL ** -0.5
    w_ff2 = nrm(ks[18], (DEPTH, D_FF, D_MODEL), f32) * (0.5 * D_FF ** -0.5)
    return {"x": x, "norm1": norm1, "w_in": w_in, "q_gain": q_gain, "k_gain": k_gain,
            "sink": sink, "lam_re": lam_re, "lam_im": lam_im, "log_dt": log_dt,
            "b_re": b_re, "b_im": b_im, "c_re": c_re, "c_im": c_im, "d_skip": d_skip,
            "w_glu": w_glu, "w_out": w_out, "norm2": norm2, "w_ff1": w_ff1, "w_ff2": w_ff2}


def _fwd_reference(x, norm1, w_in, q_gain, k_gain, sink, lam_re, lam_im, log_dt,
              b_re, b_im, c_re, c_im, d_skip, w_glu, w_out, norm2, w_ff1, w_ff2):
    bsz, seq = x.shape[0], x.shape[1]
    q_end = ATT_WIDTH
    k_end = q_end + KV_WIDTH
    v_end = k_end + KV_WIDTH
    for layer in range(DEPTH):
        h = rms_norm(x, norm1[layer])
        z = h @ w_in[layer]
        q = z[..., :q_end].reshape(bsz, seq, ATT_HEADS, HEAD_DIM)
        k = z[..., q_end:k_end].reshape(bsz, seq, KV_HEADS, HEAD_DIM)
        v = z[..., k_end:v_end].reshape(bsz, seq, KV_HEADS, HEAD_DIM)
        u = z[..., v_end:]
        att = windowed_gqa(q, k, v, q_gain[layer], k_gain[layer], sink[layer])
        ssm = s5_mixer(u, lam_re[layer], lam_im[layer], log_dt[layer], b_re[layer], b_im[layer],
                       c_re[layer], c_im[layer], d_skip[layer], w_glu[layer])
        x = x + jnp.concatenate([att, ssm], axis=-1) @ w_out[layer]
        h = rms_norm(x, norm2[layer])
        x = x + jnp.square(jax.nn.relu(h @ w_ff1[layer])) @ w_ff2[layer]
    return x


import jax as _jax
import jax.numpy as _jnp

TWIN_FORMAT = 'train_step'
FWD_PARAMS = ['x', 'norm1', 'w_in', 'q_gain', 'k_gain', 'sink', 'lam_re', 'lam_im', 'log_dt', 'b_re', 'b_im', 'c_re', 'c_im', 'd_skip', 'w_glu', 'w_out', 'norm2', 'w_ff1', 'w_ff2']
TWIN_WEIGHTS = ['norm1', 'w_in', 'q_gain', 'k_gain', 'sink', 'lam_re', 'lam_im', 'log_dt', 'b_re', 'b_im', 'c_re', 'c_im', 'd_skip', 'w_glu', 'w_out', 'norm2', 'w_ff1', 'w_ff2']
TWIN_DIFF_INPUT = 'x'
TWIN_INPUTS = ['x', 'norm1', 'w_in', 'q_gain', 'k_gain', 'sink', 'lam_re', 'lam_im', 'log_dt', 'b_re', 'b_im', 'c_re', 'c_im', 'd_skip', 'w_glu', 'w_out', 'norm2', 'w_ff1', 'w_ff2', 'loss_target', 'm_norm1', 'm_w_in', 'm_q_gain', 'm_k_gain', 'm_sink', 'm_lam_re', 'm_lam_im', 'm_log_dt', 'm_b_re', 'm_b_im', 'm_c_re', 'm_c_im', 'm_d_skip', 'm_w_glu', 'm_w_out', 'm_norm2', 'm_w_ff1', 'm_w_ff2', 'v_norm1', 'v_w_in', 'v_q_gain', 'v_k_gain', 'v_sink', 'v_lam_re', 'v_lam_im', 'v_log_dt', 'v_b_re', 'v_b_im', 'v_c_re', 'v_c_im', 'v_d_skip', 'v_w_glu', 'v_w_out', 'v_norm2', 'v_w_ff1', 'v_w_ff2']
TWIN_OUTPUTS = ['loss', 'grad_x', 'grad_norm1', 'grad_w_in', 'grad_q_gain', 'grad_k_gain', 'grad_sink', 'grad_lam_re', 'grad_lam_im', 'grad_log_dt', 'grad_b_re', 'grad_b_im', 'grad_c_re', 'grad_c_im', 'grad_d_skip', 'grad_w_glu', 'grad_w_out', 'grad_norm2', 'grad_w_ff1', 'grad_w_ff2', 'delta_norm1', 'delta_w_in', 'delta_q_gain', 'delta_k_gain', 'delta_sink', 'delta_lam_re', 'delta_lam_im', 'delta_log_dt', 'delta_b_re', 'delta_b_im', 'delta_c_re', 'delta_c_im', 'delta_d_skip', 'delta_w_glu', 'delta_w_out', 'delta_norm2', 'delta_w_ff1', 'delta_w_ff2', 'new_m_norm1', 'new_m_w_in', 'new_m_q_gain', 'new_m_k_gain', 'new_m_sink', 'new_m_lam_re', 'new_m_lam_im', 'new_m_log_dt', 'new_m_b_re', 'new_m_b_im', 'new_m_c_re', 'new_m_c_im', 'new_m_d_skip', 'new_m_w_glu', 'new_m_w_out', 'new_m_norm2', 'new_m_w_ff1', 'new_m_w_ff2', 'new_v_norm1', 'new_v_w_in', 'new_v_q_gain', 'new_v_k_gain', 'new_v_sink', 'new_v_lam_re', 'new_v_lam_im', 'new_v_log_dt', 'new_v_b_re', 'new_v_b_im', 'new_v_c_re', 'new_v_c_im', 'new_v_d_skip', 'new_v_w_glu', 'new_v_w_out', 'new_v_norm2', 'new_v_w_ff1', 'new_v_w_ff2']
TWIN_LEAF_KINDS = {'loss': 'loss', 'grad_x': 'grad_x', 'grad_norm1': 'grad_w', 'grad_w_in': 'grad_w', 'grad_q_gain': 'grad_w', 'grad_k_gain': 'grad_w', 'grad_sink': 'grad_w', 'grad_lam_re': 'grad_w', 'grad_lam_im': 'grad_w', 'grad_log_dt': 'grad_w', 'grad_b_re': 'grad_w', 'grad_b_im': 'grad_w', 'grad_c_re': 'grad_w', 'grad_c_im': 'grad_w', 'grad_d_skip': 'grad_w', 'grad_w_glu': 'grad_w', 'grad_w_out': 'grad_w', 'grad_norm2': 'grad_w', 'grad_w_ff1': 'grad_w', 'grad_w_ff2': 'grad_w', 'delta_norm1': 'delta_w', 'delta_w_in': 'delta_w', 'delta_q_gain': 'delta_w', 'delta_k_gain': 'delta_w', 'delta_sink': 'delta_w', 'delta_lam_re': 'delta_w', 'delta_lam_im': 'delta_w', 'delta_log_dt': 'delta_w', 'delta_b_re': 'delta_w', 'delta_b_im': 'delta_w', 'delta_c_re': 'delta_w', 'delta_c_im': 'delta_w', 'delta_d_skip': 'delta_w', 'delta_w_glu': 'delta_w', 'delta_w_out': 'delta_w', 'delta_norm2': 'delta_w', 'delta_w_ff1': 'delta_w', 'delta_w_ff2': 'delta_w', 'new_m_norm1': 'new_m', 'new_m_w_in': 'new_m', 'new_m_q_gain': 'new_m', 'new_m_k_gain': 'new_m', 'new_m_sink': 'new_m', 'new_m_lam_re': 'new_m', 'new_m_lam_im': 'new_m', 'new_m_log_dt': 'new_m', 'new_m_b_re': 'new_m', 'new_m_b_im': 'new_m', 'new_m_c_re': 'new_m', 'new_m_c_im': 'new_m', 'new_m_d_skip': 'new_m', 'new_m_w_glu': 'new_m', 'new_m_w_out': 'new_m', 'new_m_norm2': 'new_m', 'new_m_w_ff1': 'new_m', 'new_m_w_ff2': 'new_m', 'new_v_norm1': 'new_v', 'new_v_w_in': 'new_v', 'new_v_q_gain': 'new_v', 'new_v_k_gain': 'new_v', 'new_v_sink': 'new_v', 'new_v_lam_re': 'new_v', 'new_v_lam_im': 'new_v', 'new_v_log_dt': 'new_v', 'new_v_b_re': 'new_v', 'new_v_b_im': 'new_v', 'new_v_c_re': 'new_v', 'new_v_c_im': 'new_v', 'new_v_d_skip': 'new_v', 'new_v_w_glu': 'new_v', 'new_v_w_out': 'new_v', 'new_v_norm2': 'new_v', 'new_v_w_ff1': 'new_v', 'new_v_w_ff2': 'new_v'}


def _forward(args):
    return _fwd_reference(*[args[k] for k in FWD_PARAMS])


def _output_shape():
    def fwd():
        inp = _fwd_setup_inputs(0)
        return _fwd_reference(*[inp[k] for k in FWD_PARAMS])
    out = _jax.eval_shape(fwd)
    return out.shape, out.dtype

N_MICROBATCH = 1
ADAM_LR = 0.001
ADAM_B1 = 0.9
ADAM_B2 = 0.999
ADAM_EPS = 1e-08
ADAM_WD = 0.01
ADAM_STEP = 10
PER_EXAMPLE_BATCH_AXIS = {'x': 0, 'loss_target': 0}
SHARED_INPUTS = []
_WEIGHT_DTYPES = {'norm1': _jnp.float32, 'w_in': _jnp.float32, 'q_gain': _jnp.float32, 'k_gain': _jnp.float32, 'sink': _jnp.float32, 'lam_re': _jnp.float32, 'lam_im': _jnp.float32, 'log_dt': _jnp.float32, 'b_re': _jnp.float32, 'b_im': _jnp.float32, 'c_re': _jnp.float32, 'c_im': _jnp.float32, 'd_skip': _jnp.float32, 'w_glu': _jnp.float32, 'w_out': _jnp.float32, 'norm2': _jnp.float32, 'w_ff1': _jnp.float32, 'w_ff2': _jnp.float32}
MOMENT_SCALE = {'norm1': 4.747565e+00, 'w_in': 3.828157e+00, 'q_gain': 3.460669e+00, 'k_gain': 3.450972e+00, 'sink': 1.160553e+01, 'lam_re': 1.646643e-01, 'lam_im': 1.856934e-01, 'log_dt': 8.245192e+00, 'b_re': 1.720908e-01, 'b_im': 1.823236e-01, 'c_re': 1.868301e-01, 'c_im': 1.795893e-01, 'd_skip': 5.606354e+00, 'w_glu': 3.367603e+00, 'w_out': 1.022628e+01, 'norm2': 4.990290e+01, 'w_ff1': 3.040984e+00, 'w_ff2': 2.576102e+01}


def _to_microbatches(a, axis):
    t = _jnp.moveaxis(a, axis, 0)
    t = t.reshape((N_MICROBATCH, t.shape[0] // N_MICROBATCH) + t.shape[1:])
    return _jnp.moveaxis(t, 1, axis + 1)


def setup_inputs(seed: int = 0) -> dict:
    inp = _fwd_setup_inputs(seed)
    key = _jax.random.fold_in(_jax.random.key(seed), 7919)
    shape, _ = _output_shape()
    out = dict(inp)
    out["loss_target"] = _jax.random.normal(_jax.random.fold_in(key, 0), shape, _jnp.float32)
    for i, name in enumerate(TWIN_WEIGHTS):
        w = inp[name].astype(_jnp.float32)
        if MOMENT_SCALE is None:
            s = _jnp.sqrt(_jnp.mean(_jnp.square(w)) + 1e-30)
        else:
            s = MOMENT_SCALE[name]
        km, kv = _jax.random.split(_jax.random.fold_in(key, i + 1))
        out[name] = w
        out["m_" + name] = s * _jax.random.normal(km, w.shape, _jnp.float32)
        out["v_" + name] = (s * s) * _jax.random.uniform(kv, w.shape, _jnp.float32, 0.5, 1.5)
    if N_MICROBATCH > 1:
        for name, axis in PER_EXAMPLE_BATCH_AXIS.items():
            out[name] = _to_microbatches(out[name], axis)
    return {'x': out['x'], 'norm1': out['norm1'], 'w_in': out['w_in'], 'q_gain': out['q_gain'], 'k_gain': out['k_gain'], 'sink': out['sink'], 'lam_re': out['lam_re'], 'lam_im': out['lam_im'], 'log_dt': out['log_dt'], 'b_re': out['b_re'], 'b_im': out['b_im'], 'c_re': out['c_re'], 'c_im': out['c_im'], 'd_skip': out['d_skip'], 'w_glu': out['w_glu'], 'w_out': out['w_out'], 'norm2': out['norm2'], 'w_ff1': out['w_ff1'], 'w_ff2': out['w_ff2'], 'loss_target': out['loss_target'], 'm_norm1': out['m_norm1'], 'm_w_in': out['m_w_in'], 'm_q_gain': out['m_q_gain'], 'm_k_gain': out['m_k_gain'], 'm_sink': out['m_sink'], 'm_lam_re': out['m_lam_re'], 'm_lam_im': out['m_lam_im'], 'm_log_dt': out['m_log_dt'], 'm_b_re': out['m_b_re'], 'm_b_im': out['m_b_im'], 'm_c_re': out['m_c_re'], 'm_c_im': out['m_c_im'], 'm_d_skip': out['m_d_skip'], 'm_w_glu': out['m_w_glu'], 'm_w_out': out['m_w_out'], 'm_norm2': out['m_norm2'], 'm_w_ff1': out['m_w_ff1'], 'm_w_ff2': out['m_w_ff2'], 'v_norm1': out['v_norm1'], 'v_w_in': out['v_w_in'], 'v_q_gain': out['v_q_gain'], 'v_k_gain': out['v_k_gain'], 'v_sink': out['v_sink'], 'v_lam_re': out['v_lam_re'], 'v_lam_im': out['v_lam_im'], 'v_log_dt': out['v_log_dt'], 'v_b_re': out['v_b_re'], 'v_b_im': out['v_b_im'], 'v_c_re': out['v_c_re'], 'v_c_im': out['v_c_im'], 'v_d_skip': out['v_d_skip'], 'v_w_glu': out['v_w_glu'], 'v_w_out': out['v_w_out'], 'v_norm2': out['v_norm2'], 'v_w_ff1': out['v_w_ff1'], 'v_w_ff2': out['v_w_ff2']}


def _loss(weights, diff, rest, loss_target):
    with _jax.named_scope("forward"):
        args = {**rest, TWIN_DIFF_INPUT: diff, **{k: w.astype(_WEIGHT_DTYPES[k]) for k, w in weights.items()}}
        y = _forward(args)
    with _jax.named_scope("loss_head"):
        err = _jnp.square(y.astype(_jnp.float32) - loss_target)
        return 0.5 * _jnp.sum(_jnp.mean(err, axis=-1)) if err.ndim else 0.5 * err


def _adamw(w, g, m, v):
    m = ADAM_B1 * m + (1.0 - ADAM_B1) * g
    v = ADAM_B2 * v + (1.0 - ADAM_B2) * _jnp.square(g)
    m_hat = m / (1.0 - ADAM_B1 ** ADAM_STEP)
    v_hat = v / (1.0 - ADAM_B2 ** ADAM_STEP)
    delta = -ADAM_LR * (m_hat / (_jnp.sqrt(v_hat) + ADAM_EPS) + ADAM_WD * w)
    return delta, m, v


def reference(x, norm1, w_in, q_gain, k_gain, sink, lam_re, lam_im, log_dt, b_re, b_im, c_re, c_im, d_skip, w_glu, w_out, norm2, w_ff1, w_ff2, loss_target, m_norm1, m_w_in, m_q_gain, m_k_gain, m_sink, m_lam_re, m_lam_im, m_log_dt, m_b_re, m_b_im, m_c_re, m_c_im, m_d_skip, m_w_glu, m_w_out, m_norm2, m_w_ff1, m_w_ff2, v_norm1, v_w_in, v_q_gain, v_k_gain, v_sink, v_lam_re, v_lam_im, v_log_dt, v_b_re, v_b_im, v_c_re, v_c_im, v_d_skip, v_w_glu, v_w_out, v_norm2, v_w_ff1, v_w_ff2):
    given = dict(x=x, norm1=norm1, w_in=w_in, q_gain=q_gain, k_gain=k_gain, sink=sink, lam_re=lam_re, lam_im=lam_im, log_dt=log_dt, b_re=b_re, b_im=b_im, c_re=c_re, c_im=c_im, d_skip=d_skip, w_glu=w_glu, w_out=w_out, norm2=norm2, w_ff1=w_ff1, w_ff2=w_ff2, loss_target=loss_target, m_norm1=m_norm1, m_w_in=m_w_in, m_q_gain=m_q_gain, m_k_gain=m_k_gain, m_sink=m_sink, m_lam_re=m_lam_re, m_lam_im=m_lam_im, m_log_dt=m_log_dt, m_b_re=m_b_re, m_b_im=m_b_im, m_c_re=m_c_re, m_c_im=m_c_im, m_d_skip=m_d_skip, m_w_glu=m_w_glu, m_w_out=m_w_out, m_norm2=m_norm2, m_w_ff1=m_w_ff1, m_w_ff2=m_w_ff2, v_norm1=v_norm1, v_w_in=v_w_in, v_q_gain=v_q_gain, v_k_gain=v_k_gain, v_sink=v_sink, v_lam_re=v_lam_re, v_lam_im=v_lam_im, v_log_dt=v_log_dt, v_b_re=v_b_re, v_b_im=v_b_im, v_c_re=v_c_re, v_c_im=v_c_im, v_d_skip=v_d_skip, v_w_glu=v_w_glu, v_w_out=v_w_out, v_norm2=v_norm2, v_w_ff1=v_w_ff1, v_w_ff2=v_w_ff2)
    weights = {n: given[n] for n in TWIN_WEIGHTS}
    shared = {n: given[n] for n in SHARED_INPUTS}
    per_example = {n: given[n] for n in ['x']}
    grad_fn = _jax.value_and_grad(_loss, argnums=(0, 1))

    def one_microbatch(ex, loss_target):
        ex = dict(ex)
        diff = ex.pop(TWIN_DIFF_INPUT)
        return grad_fn(weights, diff, {**shared, **ex}, loss_target)

    if N_MICROBATCH == 1:
        loss, (grad_w, grad_x) = one_microbatch(per_example, given["loss_target"])
    else:
        def body(carry, xs):
            loss_sum, grad_sum = carry
            l_k, (gw_k, gx_k) = one_microbatch(xs[0], xs[1])
            with _jax.named_scope("update"):
                return (loss_sum + l_k, _jax.tree.map(_jnp.add, grad_sum, gw_k)), gx_k

        init = (_jnp.zeros((), _jnp.float32), _jax.tree.map(_jnp.zeros_like, weights))
        (loss, grad_w), grad_x = _jax.lax.scan(body, init, (per_example, given["loss_target"]))
    with _jax.named_scope("update"):
        delta_w, new_m, new_v = {}, {}, {}
        for n in TWIN_WEIGHTS:
            delta_w[n], new_m[n], new_v[n] = _adamw(weights[n], grad_w[n], given["m_" + n], given["v_" + n])
    return (loss, grad_x, *[grad_w[n] for n in TWIN_WEIGHTS], *[delta_w[n] for n in TWIN_WEIGHTS],
            *[new_m[n] for n in TWIN_WEIGHTS], *[new_v[n] for n in TWIN_WEIGHTS])
```

```python
import functools
import math

import jax
import jax.numpy as jnp
from jax import lax
from jax.experimental import pallas as pl
from jax.experimental.pallas import tpu as pltpu

f32 = jnp.float32
MX = jnp.bfloat16
SDS = jax.ShapeDtypeStruct

D_MODEL = 1024
DEPTH = 4
ATT_HEADS = 8
KV_HEADS = 2
GQA = ATT_HEADS // KV_HEADS
HEAD_DIM = 64
ATT_WIDTH = ATT_HEADS * HEAD_DIM
KV_WIDTH = KV_HEADS * HEAD_DIM
BLOCK = 128
SSM_WIDTH = 512
SSM_GROUP = 16
SSM_GROUPS = 32
SSM_STATE = 64
SSM_TILES = 4
TILE_CH = SSM_WIDTH // SSM_TILES
TILE_ST = SSM_GROUPS * SSM_STATE // SSM_TILES
SLAB = 256
IN_WIDTH = ATT_WIDTH + 2 * KV_WIDTH + SSM_WIDTH
U_OFF = ATT_WIDTH + 2 * KV_WIDTH
D_FF = 4096
EPS = 1e-6
NEG = float(jnp.finfo(jnp.float32).min)
SLOPES = tuple(2.0 ** (-8.0 * (h + 1) / ATT_HEADS) for h in range(ATT_HEADS))

ADAM_LR, ADAM_B1, ADAM_B2, ADAM_EPS, ADAM_WD, ADAM_STEP = 0.001, 0.9, 0.999, 1e-08, 0.01, 10

VMEM_LIMIT = 48 * 1024 * 1024
MESH = pl.DeviceIdType.MESH

NT = (((1,), (1,)), ((), ()))
TN = (((0,), (0,)), ((), ()))


def _cp(*sem):
    return pltpu.CompilerParams(dimension_semantics=sem, vmem_limit_bytes=VMEM_LIMIT)


def _dot(a, b, dims=None):
    if dims is None:
        return jnp.dot(a, b, preferred_element_type=f32)
    return lax.dot_general(a, b, dims, preferred_element_type=f32)


def _rows8(v):
    return v.reshape(v.shape[0] // 8, 8, v.shape[1]).sum(axis=0)


def rms_mm(x, gain, w, *, name, tm, tn):
    s, d = x.shape
    n = w.shape[1]

    def body(x_ref, g_ref, w_ref, h_ref, y_ref):
        @pl.when(pl.program_id(1) == 0)
        def _():
            xf = x_ref[...]
            r = lax.rsqrt(jnp.mean(xf * xf, axis=-1, keepdims=True) + EPS)
            h_ref[...] = (xf * r * g_ref[...]).astype(MX)

        y_ref[...] = _dot(h_ref[...], w_ref[...])

    return pl.pallas_call(
        body, grid=(s // tm, n // tn),
        in_specs=[pl.BlockSpec((tm, d), lambda i, j: (i, 0)), pl.BlockSpec((1, d), lambda i, j: (0, 0)),
                  pl.BlockSpec((d, tn), lambda i, j: (0, j))],
        out_specs=[pl.BlockSpec((tm, d), lambda i, j: (i, 0)), pl.BlockSpec((tm, tn), lambda i, j: (i, j))],
        out_shape=[SDS((s, d), MX), SDS((s, n), f32)],
        compiler_params=_cp("parallel", "arbitrary"), name=name)(x, gain.reshape(1, d), w)


def mm_res(a, w, res, *, relu2, name, tm, tn):
    s, k = a.shape
    n = w.shape[1]

    def body(a_ref, w_ref, r_ref, o_ref):
        av = a_ref[...]
        if relu2:
            av = jnp.maximum(av, 0.0)
            av = av * av
        o_ref[...] = r_ref[...] + _dot(av.astype(MX), w_ref[...])

    return pl.pallas_call(
        body, grid=(s // tm, n // tn),
        in_specs=[pl.BlockSpec((tm, k), lambda i, j: (i, 0)), pl.BlockSpec((k, tn), lambda i, j: (0, j)),
                  pl.BlockSpec((tm, tn), lambda i, j: (i, j))],
        out_specs=pl.BlockSpec((tm, tn), lambda i, j: (i, j)),
        out_shape=SDS((s, n), f32), compiler_params=_cp("parallel", "arbitrary"), name=name)(a, w, res)


def mm_nt(gy, w, *, name, tm, tn, f=None):
    s, n = gy.shape
    k = w.shape[0]

    def body(*refs):
        if f is None:
            g_ref, w_ref, o_ref = refs
        else:
            g_ref, w_ref, f_ref, o_ref = refs
        acc = _dot(g_ref[...].astype(MX), w_ref[...], NT)
        if f is not None:
            acc = acc * (2.0 * jnp.maximum(f_ref[...], 0.0))
        o_ref[...] = acc.astype(o_ref.dtype)

    in_specs = [pl.BlockSpec((tm, n), lambda i, j: (i, 0)), pl.BlockSpec((tn, n), lambda i, j: (j, 0))]
    args = [gy, w]
    if f is not None:
        in_specs.append(pl.BlockSpec((tm, tn), lambda i, j: (i, j)))
        args.append(f)
    return pl.pallas_call(
        body, grid=(s // tm, k // tn), in_specs=in_specs,
        out_specs=pl.BlockSpec((tm, tn), lambda i, j: (i, j)),
        out_shape=SDS((s, k), f32 if f is None else MX),
        compiler_params=_cp("parallel", "arbitrary"), name=name)(*args)


def mm_nt_norm(gy, w, x, gain, res, *, name, tm):
    s, n = gy.shape
    d = w.shape[0]

    def body(g_ref, w_ref, x_ref, gn_ref, r_ref, o_ref, gg_ref):
        @pl.when(pl.program_id(0) == 0)
        def _():
            gg_ref[...] = jnp.zeros_like(gg_ref)

        gh = _dot(g_ref[...].astype(MX), w_ref[...], NT)
        xf = x_ref[...]
        r = lax.rsqrt(jnp.mean(xf * xf, axis=-1, keepdims=True) + EPS)
        xh = xf * r
        t = gh * gn_ref[...]
        o_ref[...] = r_ref[...] + r * (t - xh * jnp.mean(t * xh, axis=-1, keepdims=True))
        gg_ref[...] += _rows8(gh * xh)

    return pl.pallas_call(
        body, grid=(s // tm,),
        in_specs=[pl.BlockSpec((tm, n), lambda i: (i, 0)), pl.BlockSpec((d, n), lambda i: (0, 0)),
                  pl.BlockSpec((tm, d), lambda i: (i, 0)), pl.BlockSpec((1, d), lambda i: (0, 0)),
                  pl.BlockSpec((tm, d), lambda i: (i, 0))],
        out_specs=[pl.BlockSpec((tm, d), lambda i: (i, 0)), pl.BlockSpec((8, d), lambda i: (0, 0))],
        out_shape=[SDS((s, d), f32), SDS((8, d), f32)],
        compiler_params=_cp("arbitrary"), name=name)(gy, w, x, gain.reshape(1, d), res)


def mm_tn(xa, gy, *, relu2, name, tk, tn, ts):
    s, k = xa.shape
    n = gy.shape[1]

    def body(x_ref, g_ref, o_ref):
        @pl.when(pl.program_id(2) == 0)
        def _():
            o_ref[...] = jnp.zeros_like(o_ref)

        xv = x_ref[...]
        if relu2:
            xv = jnp.maximum(xv, 0.0)
            xv = xv * xv
        o_ref[...] += _dot(xv.astype(MX), g_ref[...].astype(MX), TN)

    return pl.pallas_call(
        body, grid=(k // tk, n // tn, s // ts),
        in_specs=[pl.BlockSpec((ts, tk), lambda a, b, c: (c, a)), pl.BlockSpec((ts, tn), lambda a, b, c: (c, b))],
        out_specs=pl.BlockSpec((tk, tn), lambda a, b, c: (a, b)),
        out_shape=SDS((k, n), f32), compiler_params=_cp("parallel", "parallel", "arbitrary"), name=name)(xa, gy)


def _head_norm(t):
    r = lax.rsqrt(jnp.mean(t * t, axis=-1, keepdims=True) + EPS)
    return t * r, r


def _attn_mask(i, nb):
    row = lax.broadcasted_iota(jnp.int32, (BLOCK, 3 * BLOCK), 0)
    col = lax.broadcasted_iota(jnp.int32, (BLOCK, 3 * BLOCK), 1)
    dist = jnp.abs(row - col + BLOCK)
    valid = (dist <= BLOCK) & ((col >= BLOCK) | (i >= 1)) & ((col < 2 * BLOCK) | (i <= nb - 2))
    return dist.astype(f32), valid


def _attn_specs(nb):
    return [pl.BlockSpec((BLOCK, ATT_WIDTH), lambda i: (i, 0)),
            pl.BlockSpec((BLOCK, 2 * KV_WIDTH), lambda i: (jnp.maximum(i - 1, 0), 2)),
            pl.BlockSpec((BLOCK, 2 * KV_WIDTH), lambda i: (i, 2)),
            pl.BlockSpec((BLOCK, 2 * KV_WIDTH), lambda i: (jnp.minimum(i + 1, nb - 1), 2)),
            pl.BlockSpec((1, HEAD_DIM), lambda i: (0, 0)),
            pl.BlockSpec((1, HEAD_DIM), lambda i: (0, 0)),
            pl.BlockSpec(memory_space=pltpu.SMEM)]


def _attn_probs(s_g, head, distf, valid, sink_ref):
    sg = jnp.where(valid, s_g - SLOPES[head] * distf, NEG)
    sk = sink_ref[head]
    m = jnp.maximum(jnp.max(sg, axis=-1, keepdims=True), sk)
    e = jnp.exp(sg - m)
    es = jnp.exp(sk - m)
    den = jnp.sum(e, axis=-1, keepdims=True) + es
    return e / den, es / den


def attn_fwd(z, q_gain, k_gain, sink, *, name):
    s = z.shape[0]
    nb = s // BLOCK

    def body(q_ref, kp_ref, kc_ref, kn_ref, qg_ref, kg_ref, sink_ref, o_ref):
        i = pl.program_id(0)
        distf, valid = _attn_mask(i, nb)
        kv = jnp.concatenate([kp_ref[...], kc_ref[...], kn_ref[...]], axis=0)
        for kvh in range(KV_HEADS):
            kn, _ = _head_norm(kv[:, HEAD_DIM * kvh:HEAD_DIM * (kvh + 1)])
            kn = (kn * kg_ref[...]).astype(MX)
            vh = kv[:, KV_WIDTH + HEAD_DIM * kvh:KV_WIDTH + HEAD_DIM * (kvh + 1)].astype(MX)
            qs = []
            for g in range(GQA):
                h = GQA * kvh + g
                qn, _ = _head_norm(q_ref[:, HEAD_DIM * h:HEAD_DIM * (h + 1)])
                qs.append((qn * qg_ref[...]).astype(MX))
            sc = _dot(jnp.concatenate(qs, axis=0), kn, NT) * 0.125
            for g in range(GQA):
                h = GQA * kvh + g
                p, _ = _attn_probs(sc[BLOCK * g:BLOCK * (g + 1)], h, distf, valid, sink_ref)
                o_ref[:, HEAD_DIM * h:HEAD_DIM * (h + 1)] = _dot(p.astype(MX), vh).astype(o_ref.dtype)

    return pl.pallas_call(
        body, grid=(nb,), in_specs=_attn_specs(nb),
        out_specs=pl.BlockSpec((BLOCK, ATT_WIDTH), lambda i: (i, 0)),
        out_shape=SDS((s, ATT_WIDTH), MX), compiler_params=_cp("parallel"), name=name)(
            z, z, z, z, q_gain.reshape(1, HEAD_DIM), k_gain.reshape(1, HEAD_DIM), sink)


def attn_bwd(z, gmix, q_gain, k_gain, sink, *, name):
    s = z.shape[0]
    nb = s // BLOCK

    def body(q_ref, kp_ref, kc_ref, kn_ref, qg_ref, kg_ref, sink_ref, go_ref, gq_ref, dkv_ref, gqg_ref, gs_ref):
        i = pl.program_id(0)

        @pl.when(i == 0)
        def _():
            gqg_ref[...] = jnp.zeros_like(gqg_ref)
            gs_ref[...] = jnp.zeros_like(gs_ref)

        distf, valid = _attn_mask(i, nb)
        kv = jnp.concatenate([kp_ref[...], kc_ref[...], kn_ref[...]], axis=0)
        for kvh in range(KV_HEADS):
            kn, _ = _head_norm(kv[:, HEAD_DIM * kvh:HEAD_DIM * (kvh + 1)])
            kn = (kn * kg_ref[...]).astype(MX)
            vh = kv[:, KV_WIDTH + HEAD_DIM * kvh:KV_WIDTH + HEAD_DIM * (kvh + 1)].astype(MX)
            qhat, qr, qs, dos = [], [], [], []
            for g in range(GQA):
                h = GQA * kvh + g
                qn, r = _head_norm(q_ref[:, HEAD_DIM * h:HEAD_DIM * (h + 1)])
                qhat.append(qn)
                qr.append(r)
                qs.append((qn * qg_ref[...]).astype(MX))
                dos.append(go_ref[:, HEAD_DIM * h:HEAD_DIM * (h + 1)].astype(MX))
            qs = jnp.concatenate(qs, axis=0)
            dos = jnp.concatenate(dos, axis=0)
            sc = _dot(qs, kn, NT) * 0.125
            dp = _dot(dos, vh, NT)
            ps, dss = [], []
            for g in range(GQA):
                h = GQA * kvh + g
                p, psink = _attn_probs(sc[BLOCK * g:BLOCK * (g + 1)], h, distf, valid, sink_ref)
                dpg = dp[BLOCK * g:BLOCK * (g + 1)]
                delta = jnp.sum(p * dpg, axis=-1, keepdims=True)
                gs_ref[h:h + 1, :] += jnp.broadcast_to(jnp.sum(-psink * delta, axis=0, keepdims=True), (1, 128))
                ps.append(p.astype(MX))
                dss.append((p * (dpg - delta) * 0.125).astype(MX))
            ps = jnp.concatenate(ps, axis=0)
            dss = jnp.concatenate(dss, axis=0)
            gv = _dot(ps, dos, TN)
            gkn = _dot(dss, qs, TN)
            gqn = _dot(dss, kn)
            for g in range(GQA):
                h = GQA * kvh + g
                gq_h = gqn[BLOCK * g:BLOCK * (g + 1)]
                gqg_ref[h:h + 1, :] += jnp.sum(gq_h * qhat[g], axis=0, keepdims=True)
                t = gq_h * qg_ref[...]
                gq_ref[:, HEAD_DIM * h:HEAD_DIM * (h + 1)] = qr[g] * (
                    t - qhat[g] * jnp.mean(t * qhat[g], axis=-1, keepdims=True))
            for b in range(3):
                dkv_ref[b, :, HEAD_DIM * kvh:HEAD_DIM * (kvh + 1)] = gkn[BLOCK * b:BLOCK * (b + 1)]
                dkv_ref[b, :, KV_WIDTH + HEAD_DIM * kvh:KV_WIDTH + HEAD_DIM * (kvh + 1)] = gv[BLOCK * b:BLOCK * (b + 1)]

    return pl.pallas_call(
        body, grid=(nb,),
        in_specs=_attn_specs(nb) + [pl.BlockSpec((BLOCK, ATT_WIDTH), lambda i: (i, 0))],
        out_specs=[pl.BlockSpec((BLOCK, ATT_WIDTH), lambda i: (i, 0)),
                   pl.BlockSpec((3, BLOCK, 2 * KV_WIDTH), lambda i: (0, i, 0)),
                   pl.BlockSpec((ATT_HEADS, HEAD_DIM), lambda i: (0, 0)),
                   pl.BlockSpec((ATT_HEADS, 128), lambda i: (0, 0))],
        out_shape=[SDS((s, ATT_WIDTH), f32), SDS((3, s, 2 * KV_WIDTH), f32),
                   SDS((ATT_HEADS, HEAD_DIM), f32), SDS((ATT_HEADS, 128), f32)],
        compiler_params=_cp("arbitrary"), name=name)(
            z, z, z, z, q_gain.reshape(1, HEAD_DIM), k_gain.reshape(1, HEAD_DIM), sink, gmix)


def gz_assemble(gq, dkv, z, k_gain, gu_f, gu_r, gy, d_skip, *, name):
    s = z.shape[0]
    nb = s // BLOCK

    def body(gq_ref, d0_ref, d1_ref, d2_ref, z_ref, kg_ref, guf_ref, gur_ref, gy_ref, ds_ref, gz_ref, gkg_ref, gd_ref):
        i = pl.program_id(0)

        @pl.when(i == 0)
        def _():
            gkg_ref[...] = jnp.zeros_like(gkg_ref)
            gd_ref[...] = jnp.zeros_like(gd_ref)

        gkv = d1_ref[0] + jnp.where(i + 1 < nb, d0_ref[0], 0.0) + jnp.where(i >= 1, d2_ref[0], 0.0)
        gz_ref[:, 0:ATT_WIDTH] = gq_ref[...].astype(MX)
        for kvh in range(KV_HEADS):
            kh, r = _head_norm(z_ref[:, ATT_WIDTH + HEAD_DIM * kvh:ATT_WIDTH + HEAD_DIM * (kvh + 1)])
            gkn = gkv[:, HEAD_DIM * kvh:HEAD_DIM * (kvh + 1)]
            gkg_ref[kvh:kvh + 1, :] += jnp.sum(gkn * kh, axis=0, keepdims=True)
            t = gkn * kg_ref[...]
            gk = r * (t - kh * jnp.mean(t * kh, axis=-1, keepdims=True))
            gz_ref[:, ATT_WIDTH + HEAD_DIM * kvh:ATT_WIDTH + HEAD_DIM * (kvh + 1)] = gk.astype(MX)
        gz_ref[:, ATT_WIDTH + KV_WIDTH:U_OFF] = gkv[:, KV_WIDTH:].astype(MX)
        gyv = gy_ref[...]
        gz_ref[:, U_OFF:IN_WIDTH] = (guf_ref[...] + gur_ref[...] + ds_ref[...] * gyv).astype(MX)
        gd_ref[...] += _rows8(gyv * z_ref[:, U_OFF:IN_WIDTH])

    row = lambda w: pl.BlockSpec((BLOCK, w), lambda i: (i, 0))
    return pl.pallas_call(
        body, grid=(nb,),
        in_specs=[row(ATT_WIDTH),
                  pl.BlockSpec((1, BLOCK, 2 * KV_WIDTH), lambda i: (0, jnp.minimum(i + 1, nb - 1), 0)),
                  pl.BlockSpec((1, BLOCK, 2 * KV_WIDTH), lambda i: (1, i, 0)),
                  pl.BlockSpec((1, BLOCK, 2 * KV_WIDTH), lambda i: (2, jnp.maximum(i - 1, 0), 0)),
                  row(IN_WIDTH), pl.BlockSpec((1, HEAD_DIM), lambda i: (0, 0)),
                  row(SSM_WIDTH), row(SSM_WIDTH), row(SSM_WIDTH), pl.BlockSpec((1, SSM_WIDTH), lambda i: (0, 0))],
        out_specs=[row(IN_WIDTH), pl.BlockSpec((8, HEAD_DIM), lambda i: (0, 0)),
                   pl.BlockSpec((8, SSM_WIDTH), lambda i: (0, 0))],
        out_shape=[SDS((s, IN_WIDTH), MX), SDS((8, HEAD_DIM), f32), SDS((8, SSM_WIDTH), f32)],
        compiler_params=_cp("arbitrary"), name=name)(
            gq, dkv, dkv, dkv, z, k_gain.reshape(1, HEAD_DIM), gu_f, gu_r, gy, d_skip.reshape(1, SSM_WIDTH))


def _scan_chunk(b_ref, x_ref, tab_ref, carry_ref, nv, rev, acc=None):
    half = TILE_ST
    last = 0 if rev else 7
    row = lax.broadcasted_iota(jnp.int32, (8, SLAB), 0)
    for sl in range(TILE_ST // SLAB):
        re = pl.ds(SLAB * sl, SLAB)
        im = pl.ds(half + SLAB * sl, SLAB)
        tabs = [(tab_ref[8 * n:8 * n + 8, re], tab_ref[8 * n:8 * n + 8, im]) for n in range(4)]

        def step(v, carry, re=re, im=im, tabs=tabs):
            vv = (nv - 1 - v) if rev else v
            rows = pl.ds(pl.multiple_of(vv * 8, 8), 8)
            xr, xi = b_ref[rows, re], b_ref[rows, im]
            for n, k in enumerate((1, 2, 4)):
                mr, mi = tabs[n]
                sh = (8 - k) if rev else k
                rr, ri = pltpu.roll(xr, sh, 0), pltpu.roll(xi, sh, 0)
                xr, xi = xr + mr * rr - mi * ri, xi + mr * ri + mi * rr
            pr, pi = tabs[3]
            if acc is None:
                cr, ci = carry
            else:
                cr, ci, ar, ai = carry
            xr, xi = xr + pr * cr - pi * ci, xi + pr * ci + pi * cr
            x_ref[rows, re] = xr
            x_ref[rows, im] = xi
            ncr = jnp.broadcast_to(xr[last:last + 1, :], (8, SLAB))
            nci = jnp.broadcast_to(xi[last:last + 1, :], (8, SLAB))
            if acc is None:
                return ncr, nci
            sh = 7 if rev else 1
            edge = row == (7 if rev else 0)
            qr = jnp.where(edge, cr, pltpu.roll(xr, sh, 0))
            qi = jnp.where(edge, ci, pltpu.roll(xi, sh, 0))
            gr, gi = acc[0][rows, re], acc[0][rows, im]
            return ncr, nci, ar + gr * qr + gi * qi, ai + gi * qr - gr * qi

        init = (carry_ref[:, re], carry_ref[:, im])
        if acc is not None:
            init = init + (jnp.zeros((8, SLAB), f32), jnp.zeros((8, SLAB), f32))
        out = lax.fori_loop(0, nv, step, init)
        carry_ref[:, re] = out[0]
        carry_ref[:, im] = out[1]
        if acc is not None:
            acc[1][:, re] += out[2]
            acc[1][:, im] += out[3]


def ssm_fwd(z, tab, bmat, cmat, *, rev, name, chunk):
    s = z.shape[0]
    nc = s // chunk
    ci = (lambda i: nc - 1 - i) if rev else (lambda i: i)

    def body(u_ref, tab_ref, b_ref, c_ref, y_ref, xb_ref, x_scr, carry):
        @pl.when(pl.program_id(1) == 0)
        def _():
            carry[...] = jnp.zeros_like(carry)

        xb_ref[0] = carry[...]
        x_scr[...] = _dot(u_ref[...].astype(MX), b_ref[0])
        _scan_chunk(x_scr, x_scr, tab_ref.at[0], carry, chunk // 8, rev)
        y_ref[...] = _dot(x_scr[...].astype(MX), c_ref[0])

    return pl.pallas_call(
        body, grid=(SSM_TILES, nc),
        in_specs=[pl.BlockSpec((chunk, TILE_CH), lambda j, i: (ci(i), U_OFF // TILE_CH + j)),
                  pl.BlockSpec((1, 32, 2 * TILE_ST), lambda j, i: (j, 0, 0)),
                  pl.BlockSpec((1, TILE_CH, 2 * TILE_ST), lambda j, i: (j, 0, 0)),
                  pl.BlockSpec((1, 2 * TILE_ST, TILE_CH), lambda j, i: (j, 0, 0))],
        out_specs=[pl.BlockSpec((chunk, TILE_CH), lambda j, i: (ci(i), j)),
                   pl.BlockSpec((1, 8, 2 * TILE_ST), lambda j, i: (ci(i), 0, j))],
        out_shape=[SDS((s, SSM_WIDTH), f32), SDS((nc, 8, SSM_TILES * 2 * TILE_ST), f32)],
        scratch_shapes=[pltpu.VMEM((chunk, 2 * TILE_ST), f32), pltpu.VMEM((8, 2 * TILE_ST), f32)],
        compiler_params=_cp("parallel", "arbitrary"), name=name)(z, tab, bmat, cmat)


def ssm_bwd(z, gy, xb, tab_s, tab_a, bmat, cmat, *, rev, name, chunk):
    s = z.shape[0]
    nc = s // chunk
    ci = (lambda i: i) if rev else (lambda i: nc - 1 - i)

    def body(u_ref, gy_ref, xb_ref, ts_ref, ta_ref, b_ref, c_ref, gu_ref, ga_ref, gb_ref, gc_ref,
             x_scr, g_scr, gcarry, xcarry):
        @pl.when(pl.program_id(1) == 0)
        def _():
            gcarry[...] = jnp.zeros_like(gcarry)
            ga_ref[...] = jnp.zeros_like(ga_ref)
            gb_ref[...] = jnp.zeros_like(gb_ref)
            gc_ref[...] = jnp.zeros_like(gc_ref)

        ub = u_ref[...].astype(MX)
        gyb = gy_ref[...].astype(MX)
        g_scr[...] = _dot(gyb, c_ref[0], NT)
        _scan_chunk(g_scr, g_scr, ta_ref.at[0], gcarry, chunk // 8, not rev)
        x_scr[...] = _dot(ub, b_ref[0])
        xcarry[...] = xb_ref[0]
        _scan_chunk(x_scr, x_scr, ts_ref.at[0], xcarry, chunk // 8, rev, acc=(g_scr, ga_ref))
        gb16 = g_scr[...].astype(MX)
        gb_ref[0] += _dot(ub, gb16, TN)
        gc_ref[0] += _dot(x_scr[...].astype(MX), gyb, TN)
        gu_ref[...] = _dot(gb16, b_ref[0], NT)

    tile3 = lambda a, b: pl.BlockSpec((1, a, b), lambda j, i: (j, 0, 0))
    return pl.pallas_call(
        body, grid=(SSM_TILES, nc),
        in_specs=[pl.BlockSpec((chunk, TILE_CH), lambda j, i: (ci(i), U_OFF // TILE_CH + j)),
                  pl.BlockSpec((chunk, TILE_CH), lambda j, i: (ci(i), j)),
                  pl.BlockSpec((1, 8, 2 * TILE_ST), lambda j, i: (ci(i), 0, j)),
                  tile3(32, 2 * TILE_ST), tile3(32, 2 * TILE_ST),
                  tile3(TILE_CH, 2 * TILE_ST), tile3(2 * TILE_ST, TILE_CH)],
        out_specs=[pl.BlockSpec((chunk, TILE_CH), lambda j, i: (ci(i), j)),
                   pl.BlockSpec((8, 2 * TILE_ST), lambda j, i: (0, j)),
                   tile3(TILE_CH, 2 * TILE_ST), tile3(2 * TILE_ST, TILE_CH)],
        out_shape=[SDS((s, SSM_WIDTH), f32), SDS((8, SSM_TILES * 2 * TILE_ST), f32),
                   SDS((SSM_TILES, TILE_CH, 2 * TILE_ST), f32), SDS((SSM_TILES, 2 * TILE_ST, TILE_CH), f32)],
        scratch_shapes=[pltpu.VMEM((chunk, 2 * TILE_ST), f32), pltpu.VMEM((chunk, 2 * TILE_ST), f32),
                        pltpu.VMEM((8, 2 * TILE_ST), f32), pltpu.VMEM((8, 2 * TILE_ST), f32)],
        compiler_params=_cp("parallel", "arbitrary"), name=name)(z, gy, xb, tab_s, tab_a, bmat, cmat)


GELU_K = math.sqrt(2.0 / math.pi)


def _gelu(y):
    return 0.5 * y * (1.0 + jnp.tanh(GELU_K * (y + 0.044715 * (y * y * y))))


def _gelu_grad(y):
    t = jnp.tanh(GELU_K * (y + 0.044715 * (y * y * y)))
    return 0.5 * (1.0 + t) + 0.5 * y * (1.0 - t * t) * (GELU_K * (1.0 + 3.0 * 0.044715 * (y * y)))


def glu_fwd(y_f, y_r, z, att, d_skip, w_glu, *, name, tm):
    s = z.shape[0]

    def body(yf_ref, yr_ref, z_ref, att_ref, d_ref, w_ref, y_ref, gg_ref, mix_ref):
        y = d_ref[...] * z_ref[:, U_OFF:IN_WIDTH] + yf_ref[...] + yr_ref[...]
        y_ref[...] = y
        gg = _dot(_gelu(y).astype(MX), w_ref[...])
        gg_ref[...] = gg
        mix_ref[:, 0:ATT_WIDTH] = att_ref[...]
        mix_ref[:, ATT_WIDTH:] = (gg[:, :SSM_WIDTH] * jax.nn.sigmoid(gg[:, SSM_WIDTH:])).astype(MX)

    row = lambda w: pl.BlockSpec((tm, w), lambda i: (i, 0))
    return pl.pallas_call(
        body, grid=(s // tm,),
        in_specs=[row(SSM_WIDTH), row(SSM_WIDTH), row(IN_WIDTH), row(ATT_WIDTH),
                  pl.BlockSpec((1, SSM_WIDTH), lambda i: (0, 0)),
                  pl.BlockSpec((SSM_WIDTH, 2 * SSM_WIDTH), lambda i: (0, 0))],
        out_specs=[row(SSM_WIDTH), row(2 * SSM_WIDTH), row(D_MODEL)],
        out_shape=[SDS((s, SSM_WIDTH), f32), SDS((s, 2 * SSM_WIDTH), f32), SDS((s, D_MODEL), MX)],
        compiler_params=_cp("parallel"), name=name)(y_f, y_r, z, att, d_skip.reshape(1, SSM_WIDTH), w_glu)


def glu_bwd(gmix, gg, ypre, w_glu, *, name, tm):
    s = gg.shape[0]

    def body(gm_ref, gg_ref, y_ref, w_ref, ggg_ref, yg_ref, gy_ref):
        gs = gm_ref[...]
        val, gate = gg_ref[:, :SSM_WIDTH], gg_ref[:, SSM_WIDTH:]
        sg = jax.nn.sigmoid(gate)
        gval = gs * sg
        ggate = gs * val * sg * (1.0 - sg)
        ggg = jnp.concatenate([gval, ggate], axis=1).astype(MX)
        ggg_ref[...] = ggg
        y = y_ref[...]
        yg_ref[...] = _gelu(y).astype(MX)
        gy_ref[...] = _dot(ggg, w_ref[...], NT) * _gelu_grad(y)

    row = lambda w: pl.BlockSpec((tm, w), lambda i: (i, 0))
    return pl.pallas_call(
        body, grid=(s // tm,),
        in_specs=[pl.BlockSpec((tm, SSM_WIDTH), lambda i: (i, 1)), row(2 * SSM_WIDTH), row(SSM_WIDTH),
                  pl.BlockSpec((SSM_WIDTH, 2 * SSM_WIDTH), lambda i: (0, 0))],
        out_specs=[row(2 * SSM_WIDTH), row(SSM_WIDTH), row(SSM_WIDTH)],
        out_shape=[SDS((s, 2 * SSM_WIDTH), MX), SDS((s, SSM_WIDTH), MX), SDS((s, SSM_WIDTH), f32)],
        compiler_params=_cp("parallel"), name=name)(gmix, gg, ypre, w_glu)


def loss_grad(y, target, *, name, tm):
    s, d = y.shape

    def body(y_ref, t_ref, g_ref, l_ref):
        @pl.when(pl.program_id(0) == 0)
        def _():
            l_ref[...] = jnp.zeros_like(l_ref)

        e = y_ref[...] - t_ref[...]
        g_ref[...] = e * (1.0 / d)
        l_ref[...] += _rows8(e * e)

    row = pl.BlockSpec((tm, d), lambda i: (i, 0))
    return pl.pallas_call(
        body, grid=(s // tm,), in_specs=[row, row],
        out_specs=[row, pl.BlockSpec((8, d), lambda i: (0, 0))],
        out_shape=[SDS((s, d), f32), SDS((8, d), f32)],
        compiler_params=_cp("arbitrary"), name=name)(y, target)


def _row_tile(rows, cols):
    tr = rows
    while tr * cols > 256 * 1024 and tr % 16 == 0:
        tr //= 2
    return tr


def _elementwise(fn, ins, n_out, *, name):
    shape = ins[0].shape
    cols = shape[-1]
    ins2 = [a.reshape(-1, cols) for a in ins]
    rows = ins2[0].shape[0]
    tr = _row_tile(rows, cols)

    def body(*refs):
        outs = fn(*[r[...] for r in refs[:len(ins)]])
        for o_ref, o in zip(refs[len(ins):], outs):
            o_ref[...] = o

    spec = pl.BlockSpec((tr, cols), lambda i: (i, 0))
    outs = pl.pallas_call(
        body, grid=(rows // tr,), in_specs=[spec] * len(ins), out_specs=[spec] * n_out,
        out_shape=[SDS((rows, cols), f32)] * n_out, compiler_params=_cp("parallel"), name=name)(*ins2)
    return [o.reshape(shape) for o in outs]


def _adamw_math(w, g, m, v):
    m = ADAM_B1 * m + (1.0 - ADAM_B1) * g
    v = ADAM_B2 * v + (1.0 - ADAM_B2) * (g * g)
    m_hat = m / (1.0 - ADAM_B1 ** ADAM_STEP)
    v_hat = v / (1.0 - ADAM_B2 ** ADAM_STEP)
    delta = -ADAM_LR * (m_hat / (jnp.sqrt(v_hat) + ADAM_EPS) + ADAM_WD * w)
    return delta, m, v


def adamw(w, g, m, v, *, name):
    return _elementwise(_adamw_math, [w, g, m, v], 3, name=name)


def sum4(a, *, name):
    shape = a.shape[1:]
    cols = shape[-1]
    a2 = a.reshape(4, -1, cols)
    rows = a2.shape[1]
    tr = _row_tile(rows, cols)

    def body(a_ref, o_ref):
        o_ref[...] = ((a_ref[0] + a_ref[1]) + a_ref[2]) + a_ref[3]

    out = pl.pallas_call(
        body, grid=(rows // tr,), in_specs=[pl.BlockSpec((4, tr, cols), lambda i: (0, i, 0))],
        out_specs=pl.BlockSpec((tr, cols), lambda i: (i, 0)), out_shape=SDS((rows, cols), f32),
        compiler_params=_cp("parallel"), name=name)(a2)
    return out.reshape(shape)


ANY = pl.BlockSpec(memory_space=pl.ANY)


def chip_exchange(arrs, bcast, *, name):
    n = len(arrs)
    piece = [a.shape if b else a.shape[1:] for a, b in zip(arrs, bcast)]

    def body(*refs):
        ins, outs = refs[:n], refs[n:2 * n]
        send, recv, loc = refs[2 * n:]
        x, y, c = lax.axis_index("x"), lax.axis_index("y"), lax.axis_index("c")
        me = 2 * x + y
        copies = []
        for k in range(n):
            own = pltpu.make_async_copy(ins[k] if bcast[k] else ins[k].at[me], outs[k].at[me], loc.at[k])
            own.start()
            copies.append(own)
            for j, (px, py) in enumerate(((1 - x, y), (x, 1 - y), (1 - x, 1 - y))):
                cp = pltpu.make_async_remote_copy(
                    src_ref=ins[k] if bcast[k] else ins[k].at[2 * px + py], dst_ref=outs[k].at[me],
                    send_sem=send.at[3 * k + j], recv_sem=recv.at[3 * k + j],
                    device_id=(px, py, c), device_id_type=MESH)
                cp.start()
                copies.append(cp)
        for cp in copies:
            cp.wait()

    return pl.pallas_call(
        body, in_specs=[ANY] * n, out_specs=[ANY] * n,
        out_shape=[SDS((4,) + tuple(p), a.dtype) for p, a in zip(piece, arrs)],
        scratch_shapes=[pltpu.SemaphoreType.DMA((3 * n,)), pltpu.SemaphoreType.DMA((3 * n,)),
                        pltpu.SemaphoreType.DMA((n,))],
        name=name)(*arrs)


def sibling_exchange(arrs, half, *, name):
    n = len(arrs)
    piece = [a.shape[1:] if h else a.shape for a, h in zip(arrs, half)]

    def body(*refs):
        ins, outs = refs[:n], refs[n:2 * n]
        send, recv = refs[2 * n:]
        x, y, c = lax.axis_index("x"), lax.axis_index("y"), lax.axis_index("c")
        copies = []
        for k in range(n):
            cp = pltpu.make_async_remote_copy(
                src_ref=ins[k].at[1 - c] if half[k] else ins[k], dst_ref=outs[k],
                send_sem=send.at[k], recv_sem=recv.at[k], device_id=(x, y, 1 - c), device_id_type=MESH)
            cp.start()
            copies.append(cp)
        for cp in copies:
            cp.wait()

    return pl.pallas_call(
        body, in_specs=[ANY] * n, out_specs=[ANY] * n,
        out_shape=[SDS(tuple(p), a.dtype) for p, a in zip(piece, arrs)],
        scratch_shapes=[pltpu.SemaphoreType.DMA((n,)), pltpu.SemaphoreType.DMA((n,))],
        name=name)(*arrs)


def ssm_discretize(lam_re, lam_im, log_dt, b_re, b_im, c_re, c_im):
    dt = jnp.exp(log_dt)[..., None]
    mag = jnp.exp(lam_re * dt)
    abr = mag * jnp.cos(lam_im * dt)
    abi = mag * jnp.sin(lam_im * dt)
    den = lam_re * lam_re + lam_im * lam_im
    zr = ((abr - 1.0) * lam_re + abi * lam_im) / den
    zi = (abi * lam_re - (abr - 1.0) * lam_im) / den
    bbr = zr[..., None] * b_re - zi[..., None] * b_im
    bbi = zr[..., None] * b_im + zi[..., None] * b_re
    eye = jnp.eye(8, dtype=f32)
    bb = jnp.stack([bbr, bbi], axis=1).reshape(2, 2, SSM_TILES, 8, SSM_STATE, SSM_GROUP)
    bmat = jnp.einsum('dqjgph,gk->djghqkp', bb, eye).reshape(2, SSM_TILES, TILE_CH, 2 * TILE_ST)
    cc = jnp.stack([c_re, -c_im], axis=1).reshape(2, 2, SSM_TILES, 8, SSM_GROUP, SSM_STATE)
    cmat = jnp.einsum('dqjghp,gk->djqkpgh', cc, eye).reshape(2, SSM_TILES, 2 * TILE_ST, TILE_CH)
    n = SSM_GROUPS * SSM_STATE
    return abr.reshape(2, n), abi.reshape(2, n), bmat, cmat


def scan_tables(ar, ai, rev):
    pw = [(ar, ai)]
    for _ in range(7):
        pr, pi = pw[-1]
        pw.append((pr * ar - pi * ai, pr * ai + pi * ar))
    rows = jnp.arange(8)[:, None]
    parts = []
    for k in (1, 2, 4):
        cond = (rows <= 7 - k) if rev else (rows >= k)
        parts.append([jnp.where(cond, p[None, :], 0.0) for p in pw[k - 1]])
    order = [7 - r for r in range(8)] if rev else list(range(8))
    parts.append([jnp.stack([pw[o][q] for o in order]) for q in range(2)])
    tre = jnp.concatenate([p[0] for p in parts], axis=0).reshape(32, SSM_TILES, TILE_ST)
    tim = jnp.concatenate([p[1] for p in parts], axis=0).reshape(32, SSM_TILES, TILE_ST)
    return jnp.concatenate([tre, tim], axis=-1).transpose(1, 0, 2)


def _tile_a(ga):
    t = ga.sum(axis=0).reshape(SSM_TILES, 2, TILE_ST)
    return t[:, 0].reshape(-1), t[:, 1].reshape(-1)


SMALL = ('norm1', 'q_gain', 'k_gain', 'sink', 'lam_re', 'lam_im', 'log_dt', 'b_re', 'b_im', 'c_re', 'c_im',
         'd_skip', 'norm2')
BIG = ('w_in', 'w_glu', 'w_out', 'w_ff1', 'w_ff2')
WEIGHTS = ('norm1', 'w_in', 'q_gain', 'k_gain', 'sink', 'lam_re', 'lam_im', 'log_dt', 'b_re', 'b_im', 'c_re',
           'c_im', 'd_skip', 'w_glu', 'w_out', 'norm2', 'w_ff1', 'w_ff2')


def _chunk(s):
    return min(256, s)


def layer_forward(l, x, p, wb):
    s = x.shape[0]
    tm = min(512, s)
    sv = {}
    h1, z = rms_mm(x, p['norm1'], wb['w_in'], name=f"l{l}_in", tm=tm, tn=640)
    att = attn_fwd(z, p['q_gain'], p['k_gain'], p['sink'], name=f"l{l}_attn")
    (ar, ai, bmat, cmat), disc_vjp = jax.vjp(
        ssm_discretize, p['lam_re'], p['lam_im'], p['log_dt'], p['b_re'], p['b_im'], p['c_re'], p['c_im'])
    bmat16, cmat16 = bmat.astype(MX), cmat.astype(MX)
    ys, xbs, tabs = [], [], []
    for d, rev in enumerate((False, True)):
        tab = scan_tables(ar[d], ai[d], rev)
        y_d, xb_d = ssm_fwd(z, tab, bmat16[d], cmat16[d], rev=rev, name=f"l{l}_ssm{d}", chunk=_chunk(s))
        ys.append(y_d)
        xbs.append(xb_d)
        tabs.append((tab, scan_tables(ar[d], -ai[d], not rev)))
    ypre, gg, mix = glu_fwd(ys[0], ys[1], z, att, p['d_skip'], wb['w_glu'], name=f"l{l}_glu", tm=min(256, s))
    x1 = mm_res(mix, wb['w_out'], x, relu2=False, name=f"l{l}_out", tm=tm, tn=512)
    h2, f = rms_mm(x1, p['norm2'], wb['w_ff1'], name=f"l{l}_ff1", tm=tm, tn=1024)
    x2 = mm_res(f, wb['w_ff2'], x1, relu2=True, name=f"l{l}_ff2", tm=min(256, s), tn=1024)
    sv.update(x=x, h1=h1, z=z, xbs=xbs, tabs=tabs, bmat16=bmat16, cmat16=cmat16, disc_vjp=disc_vjp,
              ypre=ypre, gg=gg, mix=mix, x1=x1, h2=h2, f=f)
    return x2, sv


def layer_backward(l, gx2, p, wb, sv):
    s = gx2.shape[0]
    tm = min(512, s)
    ts = min(512, s)
    g = {}
    gf = mm_nt(gx2, wb['w_ff2'], name=f"l{l}_bff2", tm=tm, tn=1024, f=sv['f'])
    g['w_ff2'] = mm_tn(sv['f'], gx2, relu2=True, name=f"l{l}_wff2", tk=1024, tn=1024, ts=ts)
    gx1, gn2 = mm_nt_norm(gf, wb['w_ff1'], sv['x1'], p['norm2'], gx2, name=f"l{l}_bff1", tm=min(256, s))
    g['norm2'] = gn2.sum(axis=0)
    g['w_ff1'] = mm_tn(sv['h2'], gf, relu2=False, name=f"l{l}_wff1", tk=1024, tn=1024, ts=ts)
    gmix = mm_nt(gx1, wb['w_out'], name=f"l{l}_bout", tm=tm, tn=512)
    g['w_out'] = mm_tn(sv['mix'], gx1, relu2=False, name=f"l{l}_wout", tk=1024, tn=1024, ts=ts)
    ggg, yg, gy = glu_bwd(gmix, sv['gg'], sv['ypre'], wb['w_glu'], name=f"l{l}_bglu", tm=min(256, s))
    g['w_glu'] = mm_tn(yg, ggg, relu2=False, name=f"l{l}_wglu", tk=512, tn=1024, ts=ts)
    gus, gas, gbs, gcs = [], [], [], []
    for d, rev in enumerate((False, True)):
        tab_s, tab_a = sv['tabs'][d]
        gu_d, ga_d, gb_d, gc_d = ssm_bwd(sv['z'], gy, sv['xbs'][d], tab_s, tab_a, sv['bmat16'][d], sv['cmat16'][d],
                                         rev=rev, name=f"l{l}_bssm{d}", chunk=_chunk(s))
        gus.append(gu_d)
        gas.append(_tile_a(ga_d))
        gbs.append(gb_d)
        gcs.append(gc_d)
    gar = jnp.stack([gas[0][0], gas[1][0]])
    gai = jnp.stack([gas[0][1], gas[1][1]])
    (g['lam_re'], g['lam_im'], g['log_dt'], g['b_re'], g['b_im'], g['c_re'], g['c_im']) = sv['disc_vjp'](
        (gar, gai, jnp.stack(gbs), jnp.stack(gcs)))
    gq, dkv, gqg, gsk = attn_bwd(sv['z'], gmix, p['q_gain'], p['k_gain'], p['sink'], name=f"l{l}_battn")
    g['q_gain'] = gqg.sum(axis=0)
    g['sink'] = gsk[:, 0]
    gz, gkg, gd = gz_assemble(gq, dkv, sv['z'], p['k_gain'], gus[0], gus[1], gy, p['d_skip'], name=f"l{l}_gz")
    g['k_gain'] = gkg.sum(axis=0)
    g['d_skip'] = gd.sum(axis=0)
    gx, gn1 = mm_nt_norm(gz, wb['w_in'], sv['x'], p['norm1'], gx1, name=f"l{l}_bin", tm=tm)
    g['norm1'] = gn1.sum(axis=0)
    g['w_in'] = mm_tn(sv['h1'], gz, relu2=False, name=f"l{l}_win", tk=1024, tn=640, ts=ts)
    return gx, g


def local_step(x, target, small, big16):
    depth = big16['w_in'].shape[0]
    saves = []
    for l in range(depth):
        p = {k: small[k][l] for k in SMALL}
        wb = {k: big16[k][l] for k in BIG}
        x, sv = layer_forward(l, x, p, wb)
        saves.append(sv)
    gx, lparts = loss_grad(x, target, name="loss", tm=min(512, x.shape[0]))
    grads = [None] * depth
    for l in reversed(range(depth)):
        p = {k: small[k][l] for k in SMALL}
        wb = {k: big16[k][l] for k in BIG}
        gx, grads[l] = layer_backward(l, gx, p, wb, saves[l])
    return lparts, gx, grads


COL_SHARDED = ('w_in', 'w_glu', 'w_ff1')


def _gather_big(shards):
    names = list(BIG)
    got = chip_exchange([shards[k].astype(MX) for k in names], [True] * len(names), name="gather_w")
    full = {}
    for k, a in zip(names, got):
        if k in COL_SHARDED:
            full[k] = a.transpose(1, 2, 0, 3).reshape(a.shape[1], a.shape[2], 4 * a.shape[3])
        else:
            full[k] = a.transpose(1, 0, 2, 3).reshape(a.shape[1], 4 * a.shape[2], a.shape[3])
    return full


def _to_pieces(k, g):
    dp, r, c = g.shape
    if k in COL_SHARDED:
        return g.reshape(2, dp // 2, r, 4, c // 4).transpose(0, 3, 1, 2, 4)
    return g.reshape(2, dp // 2, 4, r // 4, c).transpose(0, 2, 1, 3, 4)


def _pack_small(tree):
    flat = jnp.concatenate([tree[k].reshape(-1) for k in SMALL])
    pad = (-flat.shape[0]) % 1024
    return jnp.pad(flat, (0, pad)).reshape(-1, 128)


def _unpack_small(packed, like):
    flat = packed.reshape(-1)
    out, off = {}, 0
    for k in SMALL:
        n = like[k].size
        out[k] = flat[off:off + n].reshape(like[k].shape)
        off += n
    return out


def kernel(x, norm1, w_in, q_gain, k_gain, sink, lam_re, lam_im, log_dt, b_re, b_im, c_re, c_im, d_skip, w_glu, w_out, norm2, w_ff1, w_ff2, loss_target, m_norm1, m_w_in, m_q_gain, m_k_gain, m_sink, m_lam_re, m_lam_im, m_log_dt, m_b_re, m_b_im, m_c_re, m_c_im, m_d_skip, m_w_glu, m_w_out, m_norm2, m_w_ff1, m_w_ff2, v_norm1, v_w_in, v_q_gain, v_k_gain, v_sink, v_lam_re, v_lam_im, v_log_dt, v_b_re, v_b_im, v_c_re, v_c_im, v_d_skip, v_w_glu, v_w_out, v_norm2, v_w_ff1, v_w_ff2):
    w = dict(norm1=norm1, w_in=w_in, q_gain=q_gain, k_gain=k_gain, sink=sink, lam_re=lam_re, lam_im=lam_im,
             log_dt=log_dt, b_re=b_re, b_im=b_im, c_re=c_re, c_im=c_im, d_skip=d_skip, w_glu=w_glu, w_out=w_out,
             norm2=norm2, w_ff1=w_ff1, w_ff2=w_ff2)
    m = dict(norm1=m_norm1, w_in=m_w_in, q_gain=m_q_gain, k_gain=m_k_gain, sink=m_sink, lam_re=m_lam_re,
             lam_im=m_lam_im, log_dt=m_log_dt, b_re=m_b_re, b_im=m_b_im, c_re=m_c_re, c_im=m_c_im,
             d_skip=m_d_skip, w_glu=m_w_glu, w_out=m_w_out, norm2=m_norm2, w_ff1=m_w_ff1, w_ff2=m_w_ff2)
    v = dict(norm1=v_norm1, w_in=v_w_in, q_gain=v_q_gain, k_gain=v_k_gain, sink=v_sink, lam_re=v_lam_re,
             lam_im=v_lam_im, log_dt=v_log_dt, b_re=v_b_re, b_im=v_b_im, c_re=v_c_re, c_im=v_c_im,
             d_skip=v_d_skip, w_glu=v_w_glu, w_out=v_w_out, norm2=v_norm2, w_ff1=v_w_ff1, w_ff2=v_w_ff2)
    c = lax.axis_index("c")
    depth = w_in.shape[0]

    big16 = _gather_big({k: w[k] for k in BIG})
    small = {k: w[k] for k in SMALL}
    lparts, gx, grads = local_step(x[0], loss_target[0], small, big16)
    loss = lax.psum(0.5 * jnp.sum(lparts) / D_MODEL, ("x", "y", "c"))

    gbig = {k: _to_pieces(k, jnp.stack([grads[l][k] for l in range(depth)])) for k in BIG}
    gsmall = _pack_small({k: jnp.stack([grads[l][k] for l in range(depth)]) for k in SMALL})
    names = list(BIG)
    got = sibling_exchange([gbig[k] for k in names] + [gsmall], [True] * len(names) + [False], name="reduce_sib")
    mine = [lax.dynamic_index_in_dim(gbig[k], c, 0, keepdims=False) for k in names] + [gsmall]
    halves = [_elementwise(lambda a, b: (a + b,), [a, b], 1, name=f"add_sib{i}")[0]
              for i, (a, b) in enumerate(zip(mine, got))]
    got = chip_exchange(halves, [False] * len(names) + [True], name="reduce_chips")
    sums = [sum4(a, name=f"add_chips{i}") for i, a in enumerate(got)]
    back = sibling_exchange(sums[:-1], [False] * len(names), name="reduce_back")
    gfull = {}
    for k, own, sib in zip(names, sums[:-1], back):
        lo = jnp.where(c == 0, own, sib)
        hi = jnp.where(c == 0, sib, own)
        gfull[k] = jnp.concatenate([lo, hi], axis=0)
    like = {k: w[k] for k in SMALL}
    gfull.update(_unpack_small(sums[-1], like))

    delta, new_m, new_v = {}, {}, {}
    for k in BIG:
        delta[k], new_m[k], new_v[k] = adamw(w[k], gfull[k], m[k], v[k], name=f"adamw_{k}")
    ds, ms, vs = adamw(_pack_small(like), sums[-1], _pack_small({k: m[k] for k in SMALL}),
                       _pack_small({k: v[k] for k in SMALL}), name="adamw_small")
    delta.update(_unpack_small(ds, like))
    new_m.update(_unpack_small(ms, like))
    new_v.update(_unpack_small(vs, like))

    return (loss, gx[None], *[gfull[k] for k in WEIGHTS], *[delta[k] for k in WEIGHTS],
            *[new_m[k] for k in WEIGHTS], *[new_v[k] for k in WEIGHTS])
```

```python
import functools
import math

import jax
import jax.numpy as jnp
from jax import lax
from jax.experimental import pallas as pl
from jax.experimental.pallas import tpu as pltpu

f32 = jnp.float32
MX = jnp.bfloat16
WIRE = jnp.bfloat16
SDS = jax.ShapeDtypeStruct

D_MODEL = 1024
DEPTH = 4
ATT_HEADS = 8
KV_HEADS = 2
GQA = ATT_HEADS // KV_HEADS
HEAD_DIM = 64
ATT_WIDTH = ATT_HEADS * HEAD_DIM
KV_WIDTH = KV_HEADS * HEAD_DIM
BLOCK = 128
SSM_WIDTH = 512
SSM_GROUP = 16
SSM_GROUPS = 32
SSM_STATE = 64
SSM_TILES = 4
TILE_CH = SSM_WIDTH // SSM_TILES
TILE_ST = SSM_GROUPS * SSM_STATE // SSM_TILES
SLAB = 256
IN_WIDTH = ATT_WIDTH + 2 * KV_WIDTH + SSM_WIDTH
U_OFF = ATT_WIDTH + 2 * KV_WIDTH
D_FF = 4096
EPS = 1e-6
NEG = float(jnp.finfo(jnp.float32).min)
SLOPES = tuple(2.0 ** (-8.0 * (h + 1) / ATT_HEADS) for h in range(ATT_HEADS))

ADAM_LR, ADAM_B1, ADAM_B2, ADAM_EPS, ADAM_WD, ADAM_STEP = 0.001, 0.9, 0.999, 1e-08, 0.01, 10

VMEM_LIMIT = 48 * 1024 * 1024
MESH = pl.DeviceIdType.MESH

NT = (((1,), (1,)), ((), ()))
TN = (((0,), (0,)), ((), ()))


def _cp(*sem):
    return pltpu.CompilerParams(dimension_semantics=sem, vmem_limit_bytes=VMEM_LIMIT)


def _dot(a, b, dims=None):
    if dims is None:
        return jnp.dot(a, b, preferred_element_type=f32)
    return lax.dot_general(a, b, dims, preferred_element_type=f32)


def _rows8(v):
    return v.reshape(v.shape[0] // 8, 8, v.shape[1]).sum(axis=0)


def rms_mm(x, gain, w, *, name, tm, tn):
    s, d = x.shape
    n = w.shape[1]

    def body(x_ref, g_ref, w_ref, h_ref, y_ref):
        @pl.when(pl.program_id(1) == 0)
        def _():
            xf = x_ref[...]
            r = lax.rsqrt(jnp.mean(xf * xf, axis=-1, keepdims=True) + EPS)
            h_ref[...] = (xf * r * g_ref[...]).astype(MX)

        y_ref[...] = _dot(h_ref[...], w_ref[...])

    return pl.pallas_call(
        body, grid=(s // tm, n // tn),
        in_specs=[pl.BlockSpec((tm, d), lambda i, j: (i, 0)), pl.BlockSpec((1, d), lambda i, j: (0, 0)),
                  pl.BlockSpec((d, tn), lambda i, j: (0, j))],
        out_specs=[pl.BlockSpec((tm, d), lambda i, j: (i, 0)), pl.BlockSpec((tm, tn), lambda i, j: (i, j))],
        out_shape=[SDS((s, d), MX), SDS((s, n), f32)],
        compiler_params=_cp("parallel", "arbitrary"), name=name)(x, gain.reshape(1, d), w)


def mm_res(a, w, res, *, relu2, name, tm, tn):
    s, k = a.shape
    n = w.shape[1]

    def body(a_ref, w_ref, r_ref, o_ref):
        av = a_ref[...]
        if relu2:
            av = jnp.maximum(av, 0.0)
            av = av * av
        o_ref[...] = r_ref[...] + _dot(av.astype(MX), w_ref[...])

    return pl.pallas_call(
        body, grid=(s // tm, n // tn),
        in_specs=[pl.BlockSpec((tm, k), lambda i, j: (i, 0)), pl.BlockSpec((k, tn), lambda i, j: (0, j)),
                  pl.BlockSpec((tm, tn), lambda i, j: (i, j))],
        out_specs=pl.BlockSpec((tm, tn), lambda i, j: (i, j)),
        out_shape=SDS((s, n), f32), compiler_params=_cp("parallel", "arbitrary"), name=name)(a, w, res)


def mm_nt(gy, w, *, name, tm, tn, f=None):
    s, n = gy.shape
    k = w.shape[0]

    def body(*refs):
        if f is None:
            g_ref, w_ref, o_ref = refs
        else:
            g_ref, w_ref, f_ref, o_ref = refs
        acc = _dot(g_ref[...].astype(MX), w_ref[...], NT)
        if f is not None:
            acc = acc * (2.0 * jnp.maximum(f_ref[...], 0.0))
        o_ref[...] = acc.astype(o_ref.dtype)

    in_specs = [pl.BlockSpec((tm, n), lambda i, j: (i, 0)), pl.BlockSpec((tn, n), lambda i, j: (j, 0))]
    args = [gy, w]
    if f is not None:
        in_specs.append(pl.BlockSpec((tm, tn), lambda i, j: (i, j)))
        args.append(f)
    return pl.pallas_call(
        body, grid=(s // tm, k // tn), in_specs=in_specs,
        out_specs=pl.BlockSpec((tm, tn), lambda i, j: (i, j)),
        out_shape=SDS((s, k), f32 if f is None else MX),
        compiler_params=_cp("parallel", "arbitrary"), name=name)(*args)


def mm_nt_norm(gy, w, x, gain, res, *, name, tm):
    s, n = gy.shape
    d = w.shape[0]

    def body(g_ref, w_ref, x_ref, gn_ref, r_ref, o_ref, gg_ref):
        @pl.when(pl.program_id(0) == 0)
        def _():
            gg_ref[...] = jnp.zeros_like(gg_ref)

        gh = _dot(g_ref[...].astype(MX), w_ref[...], NT)
        xf = x_ref[...]
        r = lax.rsqrt(jnp.mean(xf * xf, axis=-1, keepdims=True) + EPS)
        xh = xf * r
        t = gh * gn_ref[...]
        o_ref[...] = r_ref[...] + r * (t - xh * jnp.mean(t * xh, axis=-1, keepdims=True))
        gg_ref[...] += _rows8(gh * xh)

    return pl.pallas_call(
        body, grid=(s // tm,),
        in_specs=[pl.BlockSpec((tm, n), lambda i: (i, 0)), pl.BlockSpec((d, n), lambda i: (0, 0)),
                  pl.BlockSpec((tm, d), lambda i: (i, 0)), pl.BlockSpec((1, d), lambda i: (0, 0)),
                  pl.BlockSpec((tm, d), lambda i: (i, 0))],
        out_specs=[pl.BlockSpec((tm, d), lambda i: (i, 0)), pl.BlockSpec((8, d), lambda i: (0, 0))],
        out_shape=[SDS((s, d), f32), SDS((8, d), f32)],
        compiler_params=_cp("arbitrary"), name=name)(gy, w, x, gain.reshape(1, d), res)


def mm_tn(xa, gy, *, relu2, name, tk, tn, ts):
    s, k = xa.shape
    n = gy.shape[1]

    def body(x_ref, g_ref, o_ref):
        @pl.when(pl.program_id(2) == 0)
        def _():
            o_ref[...] = jnp.zeros_like(o_ref)

        xv = x_ref[...]
        if relu2:
            xv = jnp.maximum(xv, 0.0)
            xv = xv * xv
        o_ref[...] += _dot(xv.astype(MX), g_ref[...].astype(MX), TN)

    return pl.pallas_call(
        body, grid=(k // tk, n // tn, s // ts),
        in_specs=[pl.BlockSpec((ts, tk), lambda a, b, c: (c, a)), pl.BlockSpec((ts, tn), lambda a, b, c: (c, b))],
        out_specs=pl.BlockSpec((tk, tn), lambda a, b, c: (a, b)),
        out_shape=SDS((k, n), f32), compiler_params=_cp("parallel", "parallel", "arbitrary"), name=name)(xa, gy)


def _head_norm(t):
    r = lax.rsqrt(jnp.mean(t * t, axis=-1, keepdims=True) + EPS)
    return t * r, r


def _attn_mask(i, nb):
    row = lax.broadcasted_iota(jnp.int32, (BLOCK, 3 * BLOCK), 0)
    col = lax.broadcasted_iota(jnp.int32, (BLOCK, 3 * BLOCK), 1)
    dist = jnp.abs(row - col + BLOCK)
    valid = (dist <= BLOCK) & ((col >= BLOCK) | (i >= 1)) & ((col < 2 * BLOCK) | (i <= nb - 2))
    return dist.astype(f32), valid


def _attn_specs(nb):
    return [pl.BlockSpec((BLOCK, ATT_WIDTH), lambda i: (i, 0)),
            pl.BlockSpec((BLOCK, 2 * KV_WIDTH), lambda i: (jnp.maximum(i - 1, 0), 2)),
            pl.BlockSpec((BLOCK, 2 * KV_WIDTH), lambda i: (i, 2)),
            pl.BlockSpec((BLOCK, 2 * KV_WIDTH), lambda i: (jnp.minimum(i + 1, nb - 1), 2)),
            pl.BlockSpec((1, HEAD_DIM), lambda i: (0, 0)),
            pl.BlockSpec((1, HEAD_DIM), lambda i: (0, 0)),
            pl.BlockSpec(memory_space=pltpu.SMEM)]


def _attn_probs(s_g, head, distf, valid, sink_ref):
    sg = jnp.where(valid, s_g - SLOPES[head] * distf, NEG)
    sk = sink_ref[head]
    m = jnp.maximum(jnp.max(sg, axis=-1, keepdims=True), sk)
    e = jnp.exp(sg - m)
    es = jnp.exp(sk - m)
    den = jnp.sum(e, axis=-1, keepdims=True) + es
    return e / den, es / den


def attn_fwd(z, q_gain, k_gain, sink, *, name):
    s = z.shape[0]
    nb = s // BLOCK

    def body(q_ref, kp_ref, kc_ref, kn_ref, qg_ref, kg_ref, sink_ref, o_ref):
        i = pl.program_id(0)
        distf, valid = _attn_mask(i, nb)
        kv = jnp.concatenate([kp_ref[...], kc_ref[...], kn_ref[...]], axis=0)
        for kvh in range(KV_HEADS):
            kn, _ = _head_norm(kv[:, HEAD_DIM * kvh:HEAD_DIM * (kvh + 1)])
            kn = (kn * kg_ref[...]).astype(MX)
            vh = kv[:, KV_WIDTH + HEAD_DIM * kvh:KV_WIDTH + HEAD_DIM * (kvh + 1)].astype(MX)
            qs = []
            for g in range(GQA):
                h = GQA * kvh + g
                qn, _ = _head_norm(q_ref[:, HEAD_DIM * h:HEAD_DIM * (h + 1)])
                qs.append((qn * qg_ref[...]).astype(MX))
            sc = _dot(jnp.concatenate(qs, axis=0), kn, NT) * 0.125
            for g in range(GQA):
                h = GQA * kvh + g
                p, _ = _attn_probs(sc[BLOCK * g:BLOCK * (g + 1)], h, distf, valid, sink_ref)
                o_ref[:, HEAD_DIM * h:HEAD_DIM * (h + 1)] = _dot(p.astype(MX), vh).astype(o_ref.dtype)

    return pl.pallas_call(
        body, grid=(nb,), in_specs=_attn_specs(nb),
        out_specs=pl.BlockSpec((BLOCK, ATT_WIDTH), lambda i: (i, 0)),
        out_shape=SDS((s, ATT_WIDTH), MX), compiler_params=_cp("parallel"), name=name)(
            z, z, z, z, q_gain.reshape(1, HEAD_DIM), k_gain.reshape(1, HEAD_DIM), sink)


def attn_bwd(z, gmix, q_gain, k_gain, sink, *, name):
    s = z.shape[0]
    nb = s // BLOCK

    def body(q_ref, kp_ref, kc_ref, kn_ref, qg_ref, kg_ref, sink_ref, go_ref, gq_ref, dkv_ref, gqg_ref, gs_ref):
        i = pl.program_id(0)

        @pl.when(i == 0)
        def _():
            gqg_ref[...] = jnp.zeros_like(gqg_ref)
            gs_ref[...] = jnp.zeros_like(gs_ref)

        distf, valid = _attn_mask(i, nb)
        kv = jnp.concatenate([kp_ref[...], kc_ref[...], kn_ref[...]], axis=0)
        for kvh in range(KV_HEADS):
            kn, _ = _head_norm(kv[:, HEAD_DIM * kvh:HEAD_DIM * (kvh + 1)])
            kn = (kn * kg_ref[...]).astype(MX)
            vh = kv[:, KV_WIDTH + HEAD_DIM * kvh:KV_WIDTH + HEAD_DIM * (kvh + 1)].astype(MX)
            qhat, qr, qs, dos = [], [], [], []
            for g in range(GQA):
                h = GQA * kvh + g
                qn, r = _head_norm(q_ref[:, HEAD_DIM * h:HEAD_DIM * (h + 1)])
                qhat.append(qn)
                qr.append(r)
                qs.append((qn * qg_ref[...]).astype(MX))
                dos.append(go_ref[:, HEAD_DIM * h:HEAD_DIM * (h + 1)].astype(MX))
            qs = jnp.concatenate(qs, axis=0)
            dos = jnp.concatenate(dos, axis=0)
            sc = _dot(qs, kn, NT) * 0.125
            dp = _dot(dos, vh, NT)
            ps, dss = [], []
            for g in range(GQA):
                h = GQA * kvh + g
                p, psink = _attn_probs(sc[BLOCK * g:BLOCK * (g + 1)], h, distf, valid, sink_ref)
                dpg = dp[BLOCK * g:BLOCK * (g + 1)]
                delta = jnp.sum(p * dpg, axis=-1, keepdims=True)
                gs_ref[h:h + 1, :] += jnp.broadcast_to(jnp.sum(-psink * delta, axis=0, keepdims=True), (1, 128))
                ps.append(p.astype(MX))
                dss.append((p * (dpg - delta) * 0.125).astype(MX))
            ps = jnp.concatenate(ps, axis=0)
            dss = jnp.concatenate(dss, axis=0)
            gv = _dot(ps, dos, TN)
            gkn = _dot(dss, qs, TN)
            gqn = _dot(dss, kn)
            for g in range(GQA):
                h = GQA * kvh + g
                gq_h = gqn[BLOCK * g:BLOCK * (g + 1)]
                gqg_ref[h:h + 1, :] += jnp.sum(gq_h * qhat[g], axis=0, keepdims=True)
                t = gq_h * qg_ref[...]
                gq_ref[:, HEAD_DIM * h:HEAD_DIM * (h + 1)] = qr[g] * (
                    t - qhat[g] * jnp.mean(t * qhat[g], axis=-1, keepdims=True))
            for b in range(3):
                dkv_ref[b, :, HEAD_DIM * kvh:HEAD_DIM * (kvh + 1)] = gkn[BLOCK * b:BLOCK * (b + 1)]
                dkv_ref[b, :, KV_WIDTH + HEAD_DIM * kvh:KV_WIDTH + HEAD_DIM * (kvh + 1)] = gv[BLOCK * b:BLOCK * (b + 1)]

    return pl.pallas_call(
        body, grid=(nb,),
        in_specs=_attn_specs(nb) + [pl.BlockSpec((BLOCK, ATT_WIDTH), lambda i: (i, 0))],
        out_specs=[pl.BlockSpec((BLOCK, ATT_WIDTH), lambda i: (i, 0)),
                   pl.BlockSpec((3, BLOCK, 2 * KV_WIDTH), lambda i: (0, i, 0)),
                   pl.BlockSpec((ATT_HEADS, HEAD_DIM), lambda i: (0, 0)),
                   pl.BlockSpec((ATT_HEADS, 128), lambda i: (0, 0))],
        out_shape=[SDS((s, ATT_WIDTH), f32), SDS((3, s, 2 * KV_WIDTH), f32),
                   SDS((ATT_HEADS, HEAD_DIM), f32), SDS((ATT_HEADS, 128), f32)],
        compiler_params=_cp("arbitrary"), name=name)(
            z, z, z, z, q_gain.reshape(1, HEAD_DIM), k_gain.reshape(1, HEAD_DIM), sink, gmix)


def gz_assemble(gq, dkv, z, k_gain, gu_f, gu_r, gy, d_skip, *, name):
    s = z.shape[0]
    nb = s // BLOCK

    def body(gq_ref, d0_ref, d1_ref, d2_ref, z_ref, kg_ref, guf_ref, gur_ref, gy_ref, ds_ref, gz_ref, gkg_ref, gd_ref):
        i = pl.program_id(0)

        @pl.when(i == 0)
        def _():
            gkg_ref[...] = jnp.zeros_like(gkg_ref)
            gd_ref[...] = jnp.zeros_like(gd_ref)

        gkv = d1_ref[0] + jnp.where(i + 1 < nb, d0_ref[0], 0.0) + jnp.where(i >= 1, d2_ref[0], 0.0)
        gz_ref[:, 0:ATT_WIDTH] = gq_ref[...].astype(MX)
        for kvh in range(KV_HEADS):
            kh, r = _head_norm(z_ref[:, ATT_WIDTH + HEAD_DIM * kvh:ATT_WIDTH + HEAD_DIM * (kvh + 1)])
            gkn = gkv[:, HEAD_DIM * kvh:HEAD_DIM * (kvh + 1)]
            gkg_ref[kvh:kvh + 1, :] += jnp.sum(gkn * kh, axis=0, keepdims=True)
            t = gkn * kg_ref[...]
            gk = r * (t - kh * jnp.mean(t * kh, axis=-1, keepdims=True))
            gz_ref[:, ATT_WIDTH + HEAD_DIM * kvh:ATT_WIDTH + HEAD_DIM * (kvh + 1)] = gk.astype(MX)
        gz_ref[:, ATT_WIDTH + KV_WIDTH:U_OFF] = gkv[:, KV_WIDTH:].astype(MX)
        gyv = gy_ref[...]
        gz_ref[:, U_OFF:IN_WIDTH] = (guf_ref[...] + gur_ref[...] + ds_ref[...] * gyv).astype(MX)
        gd_ref[...] += _rows8(gyv * z_ref[:, U_OFF:IN_WIDTH])

    row = lambda w: pl.BlockSpec((BLOCK, w), lambda i: (i, 0))
    return pl.pallas_call(
        body, grid=(nb,),
        in_specs=[row(ATT_WIDTH),
                  pl.BlockSpec((1, BLOCK, 2 * KV_WIDTH), lambda i: (0, jnp.minimum(i + 1, nb - 1), 0)),
                  pl.BlockSpec((1, BLOCK, 2 * KV_WIDTH), lambda i: (1, i, 0)),
                  pl.BlockSpec((1, BLOCK, 2 * KV_WIDTH), lambda i: (2, jnp.maximum(i - 1, 0), 0)),
                  row(IN_WIDTH), pl.BlockSpec((1, HEAD_DIM), lambda i: (0, 0)),
                  row(SSM_WIDTH), row(SSM_WIDTH), row(SSM_WIDTH), pl.BlockSpec((1, SSM_WIDTH), lambda i: (0, 0))],
        out_specs=[row(IN_WIDTH), pl.BlockSpec((8, HEAD_DIM), lambda i: (0, 0)),
                   pl.BlockSpec((8, SSM_WIDTH), lambda i: (0, 0))],
        out_shape=[SDS((s, IN_WIDTH), MX), SDS((8, HEAD_DIM), f32), SDS((8, SSM_WIDTH), f32)],
        compiler_params=_cp("arbitrary"), name=name)(
            gq, dkv, dkv, dkv, z, k_gain.reshape(1, HEAD_DIM), gu_f, gu_r, gy, d_skip.reshape(1, SSM_WIDTH))


def _cmul(ar, ai, xr, xi):
    return ar * xr - ai * xi, ar * xi + ai * xr


def _permute_rows(src_ref, dst_ref, nv):
    for v in range(nv):
        dst_ref[8 * v:8 * v + 8, :] = src_ref[pl.ds(v, 8, stride=nv), :]


def _unpermute_rows(val, dst_ref, nv):
    for v in range(nv):
        dst_ref[pl.ds(v, 8, stride=nv), :] = val[8 * v:8 * v + 8, :]


def _scan_chunk(x_ref, tab_ref, carry_ref, nv, rev, acc=None):
    L = TILE_ST
    order = list(range(nv - 1, -1, -1)) if rev else list(range(nv))
    a_r, a_i = tab_ref[32:40, :L], tab_ref[32:40, L:]
    pr = pi = None
    for v in order:
        rows = slice(8 * v, 8 * v + 8)
        xr, xi = x_ref[rows, :L], x_ref[rows, L:]
        if pr is not None:
            mr, mi = _cmul(a_r, a_i, pr, pi)
            xr, xi = xr + mr, xi + mi
            x_ref[rows, :L] = xr
            x_ref[rows, L:] = xi
        pr, pi = xr, xi
    er, ei = pr, pi
    row = lax.broadcasted_iota(jnp.int32, (8, L), 0)
    edge = row == (7 if rev else 0)
    sh = 7 if rev else 1
    fr = jnp.where(edge, carry_ref[:, :L], pltpu.roll(er, sh, 0))
    fi = jnp.where(edge, carry_ref[:, L:], pltpu.roll(ei, sh, 0))
    for n, k in enumerate((1, 2, 4)):
        mr, mi = tab_ref[8 * n:8 * n + 8, :L], tab_ref[8 * n:8 * n + 8, L:]
        sh = (8 - k) if rev else k
        rr, ri = pltpu.roll(fr, sh, 0), pltpu.roll(fi, sh, 0)
        fr, fi = fr + mr * rr - mi * ri, fi + mr * ri + mi * rr
    dr, di = _cmul(tab_ref[24:32, :L], tab_ref[24:32, L:], fr, fi)
    last = 0 if rev else 7
    carry_ref[:, :L] = jnp.broadcast_to((dr + er)[last:last + 1, :], (8, L))
    carry_ref[:, L:] = jnp.broadcast_to((di + ei)[last:last + 1, :], (8, L))
    qr, qi = fr, fi
    if acc is not None:
        sr, si = jnp.zeros((8, L), f32), jnp.zeros((8, L), f32)
    for v in order:
        rows = slice(8 * v, 8 * v + 8)
        trow = slice(40 + 8 * v, 48 + 8 * v)
        mr, mi = _cmul(tab_ref[trow, :L], tab_ref[trow, L:], fr, fi)
        xr, xi = x_ref[rows, :L] + mr, x_ref[rows, L:] + mi
        x_ref[rows, :L] = xr
        x_ref[rows, L:] = xi
        if acc is not None:
            gr, gi = acc[0][rows, :L], acc[0][rows, L:]
            sr, si = sr + gr * qr + gi * qi, si + gi * qr - gr * qi
            qr, qi = xr, xi
    if acc is not None:
        acc[1][:, :L] += sr
        acc[1][:, L:] += si


def ssm_fwd(z, tab, bmat, cmat, *, rev, name, chunk):
    s = z.shape[0]
    nc = s // chunk
    nv = chunk // 8
    ci = (lambda i: nc - 1 - i) if rev else (lambda i: i)

    def body(u_ref, tab_ref, b_ref, c_ref, y_ref, xb_ref, u_scr, x_scr, carry):
        @pl.when(pl.program_id(1) == 0)
        def _():
            carry[...] = jnp.zeros_like(carry)

        xb_ref[0] = carry[...]
        _permute_rows(u_ref, u_scr, nv)
        x_scr[...] = _dot(u_scr[...].astype(MX), b_ref[0])
        _scan_chunk(x_scr, tab_ref.at[0], carry, nv, rev)
        _unpermute_rows(_dot(x_scr[...].astype(MX), c_ref[0]), y_ref, nv)

    return pl.pallas_call(
        body, grid=(SSM_TILES, nc),
        in_specs=[pl.BlockSpec((chunk, TILE_CH), lambda j, i: (ci(i), U_OFF // TILE_CH + j)),
                  pl.BlockSpec((1, 40 + chunk, 2 * TILE_ST), lambda j, i: (j, 0, 0)),
                  pl.BlockSpec((1, TILE_CH, 2 * TILE_ST), lambda j, i: (j, 0, 0)),
                  pl.BlockSpec((1, 2 * TILE_ST, TILE_CH), lambda j, i: (j, 0, 0))],
        out_specs=[pl.BlockSpec((chunk, TILE_CH), lambda j, i: (ci(i), j)),
                   pl.BlockSpec((1, 8, 2 * TILE_ST), lambda j, i: (ci(i), 0, j))],
        out_shape=[SDS((s, SSM_WIDTH), f32), SDS((nc, 8, SSM_TILES * 2 * TILE_ST), f32)],
        scratch_shapes=[pltpu.VMEM((chunk, TILE_CH), f32), pltpu.VMEM((chunk, 2 * TILE_ST), f32),
                        pltpu.VMEM((8, 2 * TILE_ST), f32)],
        compiler_params=_cp("parallel", "arbitrary"), name=name)(z, tab, bmat, cmat)


def ssm_bwd(z, gy, xb, tab_s, tab_a, bmat, cmat, *, rev, name, chunk):
    s = z.shape[0]
    nc = s // chunk
    nv = chunk // 8
    ci = (lambda i: i) if rev else (lambda i: nc - 1 - i)

    def body(u_ref, gy_ref, xb_ref, ts_ref, ta_ref, b_ref, c_ref, gu_ref, ga_ref, gb_ref, gc_ref,
             u_scr, gy_scr, x_scr, g_scr, gcarry, xcarry):
        @pl.when(pl.program_id(1) == 0)
        def _():
            gcarry[...] = jnp.zeros_like(gcarry)
            ga_ref[...] = jnp.zeros_like(ga_ref)
            gb_ref[...] = jnp.zeros_like(gb_ref)
            gc_ref[...] = jnp.zeros_like(gc_ref)

        _permute_rows(u_ref, u_scr, nv)
        _permute_rows(gy_ref, gy_scr, nv)
        ub = u_scr[...].astype(MX)
        gyb = gy_scr[...].astype(MX)
        g_scr[...] = _dot(gyb, c_ref[0], NT)
        _scan_chunk(g_scr, ta_ref.at[0], gcarry, nv, not rev)
        x_scr[...] = _dot(ub, b_ref[0])
        xcarry[...] = xb_ref[0]
        _scan_chunk(x_scr, ts_ref.at[0], xcarry, nv, rev, acc=(g_scr, ga_ref))
        gb16 = g_scr[...].astype(MX)
        gb_ref[0] += _dot(ub, gb16, TN)
        gc_ref[0] += _dot(x_scr[...].astype(MX), gyb, TN)
        _unpermute_rows(_dot(gb16, b_ref[0], NT), gu_ref, nv)

    tile3 = lambda a, b: pl.BlockSpec((1, a, b), lambda j, i: (j, 0, 0))
    return pl.pallas_call(
        body, grid=(SSM_TILES, nc),
        in_specs=[pl.BlockSpec((chunk, TILE_CH), lambda j, i: (ci(i), U_OFF // TILE_CH + j)),
                  pl.BlockSpec((chunk, TILE_CH), lambda j, i: (ci(i), j)),
                  pl.BlockSpec((1, 8, 2 * TILE_ST), lambda j, i: (ci(i), 0, j)),
                  tile3(40 + chunk, 2 * TILE_ST), tile3(40 + chunk, 2 * TILE_ST),
                  tile3(TILE_CH, 2 * TILE_ST), tile3(2 * TILE_ST, TILE_CH)],
        out_specs=[pl.BlockSpec((chunk, TILE_CH), lambda j, i: (ci(i), j)),
                   pl.BlockSpec((8, 2 * TILE_ST), lambda j, i: (0, j)),
                   tile3(TILE_CH, 2 * TILE_ST), tile3(2 * TILE_ST, TILE_CH)],
        out_shape=[SDS((s, SSM_WIDTH), f32), SDS((8, SSM_TILES * 2 * TILE_ST), f32),
                   SDS((SSM_TILES, TILE_CH, 2 * TILE_ST), f32), SDS((SSM_TILES, 2 * TILE_ST, TILE_CH), f32)],
        scratch_shapes=[pltpu.VMEM((chunk, TILE_CH), f32), pltpu.VMEM((chunk, TILE_CH), f32),
                        pltpu.VMEM((chunk, 2 * TILE_ST), f32), pltpu.VMEM((chunk, 2 * TILE_ST), f32),
                        pltpu.VMEM((8, 2 * TILE_ST), f32), pltpu.VMEM((8, 2 * TILE_ST), f32)],
        compiler_params=_cp("parallel", "arbitrary"), name=name)(z, gy, xb, tab_s, tab_a, bmat, cmat)


GELU_K = math.sqrt(2.0 / math.pi)


def _gelu(y):
    return 0.5 * y * (1.0 + jnp.tanh(GELU_K * (y + 0.044715 * (y * y * y))))


def _gelu_grad(y):
    t = jnp.tanh(GELU_K * (y + 0.044715 * (y * y * y)))
    return 0.5 * (1.0 + t) + 0.5 * y * (1.0 - t * t) * (GELU_K * (1.0 + 3.0 * 0.044715 * (y * y)))


def glu_fwd(y_f, y_r, z, att, d_skip, w_glu, *, name, tm):
    s = z.shape[0]

    def body(yf_ref, yr_ref, z_ref, att_ref, d_ref, w_ref, y_ref, gg_ref, mix_ref):
        y = d_ref[...] * z_ref[:, U_OFF:IN_WIDTH] + yf_ref[...] + yr_ref[...]
        y_ref[...] = y
        gg = _dot(_gelu(y).astype(MX), w_ref[...])
        gg_ref[...] = gg
        mix_ref[:, 0:ATT_WIDTH] = att_ref[...]
        mix_ref[:, ATT_WIDTH:] = (gg[:, :SSM_WIDTH] * jax.nn.sigmoid(gg[:, SSM_WIDTH:])).astype(MX)

    row = lambda w: pl.BlockSpec((tm, w), lambda i: (i, 0))
    return pl.pallas_call(
        body, grid=(s // tm,),
        in_specs=[row(SSM_WIDTH), row(SSM_WIDTH), row(IN_WIDTH), row(ATT_WIDTH),
                  pl.BlockSpec((1, SSM_WIDTH), lambda i: (0, 0)),
                  pl.BlockSpec((SSM_WIDTH, 2 * SSM_WIDTH), lambda i: (0, 0))],
        out_specs=[row(SSM_WIDTH), row(2 * SSM_WIDTH), row(D_MODEL)],
        out_shape=[SDS((s, SSM_WIDTH), f32), SDS((s, 2 * SSM_WIDTH), f32), SDS((s, D_MODEL), MX)],
        compiler_params=_cp("parallel"), name=name)(y_f, y_r, z, att, d_skip.reshape(1, SSM_WIDTH), w_glu)


def glu_bwd(gmix, gg, ypre, w_glu, *, name, tm):
    s = gg.shape[0]

    def body(gm_ref, gg_ref, y_ref, w_ref, ggg_ref, yg_ref, gy_ref):
        gs = gm_ref[...]
        val, gate = gg_ref[:, :SSM_WIDTH], gg_ref[:, SSM_WIDTH:]
        sg = jax.nn.sigmoid(gate)
        gval = gs * sg
        ggate = gs * val * sg * (1.0 - sg)
        ggg = jnp.concatenate([gval, ggate], axis=1).astype(MX)
        ggg_ref[...] = ggg
        y = y_ref[...]
        yg_ref[...] = _gelu(y).astype(MX)
        gy_ref[...] = _dot(ggg, w_ref[...], NT) * _gelu_grad(y)

    row = lambda w: pl.BlockSpec((tm, w), lambda i: (i, 0))
    return pl.pallas_call(
        body, grid=(s // tm,),
        in_specs=[pl.BlockSpec((tm, SSM_WIDTH), lambda i: (i, 1)), row(2 * SSM_WIDTH), row(SSM_WIDTH),
                  pl.BlockSpec((SSM_WIDTH, 2 * SSM_WIDTH), lambda i: (0, 0))],
        out_specs=[row(2 * SSM_WIDTH), row(SSM_WIDTH), row(SSM_WIDTH)],
        out_shape=[SDS((s, 2 * SSM_WIDTH), MX), SDS((s, SSM_WIDTH), MX), SDS((s, SSM_WIDTH), f32)],
        compiler_params=_cp("parallel"), name=name)(gmix, gg, ypre, w_glu)


def loss_grad(y, target, *, name, tm):
    s, d = y.shape

    def body(y_ref, t_ref, g_ref, l_ref):
        @pl.when(pl.program_id(0) == 0)
        def _():
            l_ref[...] = jnp.zeros_like(l_ref)

        e = y_ref[...] - t_ref[...]
        g_ref[...] = e * (1.0 / d)
        l_ref[...] += _rows8(e * e)

    row = pl.BlockSpec((tm, d), lambda i: (i, 0))
    return pl.pallas_call(
        body, grid=(s // tm,), in_specs=[row, row],
        out_specs=[row, pl.BlockSpec((8, d), lambda i: (0, 0))],
        out_shape=[SDS((s, d), f32), SDS((8, d), f32)],
        compiler_params=_cp("arbitrary"), name=name)(y, target)


def _row_tile(rows, cols):
    tr = rows
    while tr * cols > 256 * 1024 and tr % 16 == 0:
        tr //= 2
    return tr


def _elementwise(fn, ins, n_out, *, name, out_dtype=f32):
    shape = ins[0].shape
    cols = shape[-1]
    ins2 = [a.reshape(-1, cols) for a in ins]
    rows = ins2[0].shape[0]
    tr = _row_tile(rows, cols)

    def body(*refs):
        outs = fn(*[r[...] for r in refs[:len(ins)]])
        for o_ref, o in zip(refs[len(ins):], outs):
            o_ref[...] = o.astype(out_dtype)

    spec = pl.BlockSpec((tr, cols), lambda i: (i, 0))
    outs = pl.pallas_call(
        body, grid=(rows // tr,), in_specs=[spec] * len(ins), out_specs=[spec] * n_out,
        out_shape=[SDS((rows, cols), out_dtype)] * n_out, compiler_params=_cp("parallel"), name=name)(*ins2)
    return [o.reshape(shape) for o in outs]


def _adamw_math(w, g, m, v):
    m = ADAM_B1 * m + (1.0 - ADAM_B1) * g
    v = ADAM_B2 * v + (1.0 - ADAM_B2) * (g * g)
    m_hat = m / (1.0 - ADAM_B1 ** ADAM_STEP)
    v_hat = v / (1.0 - ADAM_B2 ** ADAM_STEP)
    delta = -ADAM_LR * (m_hat / (jnp.sqrt(v_hat) + ADAM_EPS) + ADAM_WD * w)
    return delta, m, v


def adamw(w, g, m, v, *, name):
    return _elementwise(_adamw_math, [w, g, m, v], 3, name=name)


def sum4(a, *, name):
    shape = a.shape[1:]
    cols = shape[-1]
    a2 = a.reshape(4, -1, cols)
    rows = a2.shape[1]
    tr = _row_tile(rows, cols)

    def body(a_ref, o_ref):
        o_ref[...] = ((a_ref[0].astype(f32) + a_ref[1].astype(f32)) + a_ref[2].astype(f32)) + a_ref[3].astype(f32)

    out = pl.pallas_call(
        body, grid=(rows // tr,), in_specs=[pl.BlockSpec((4, tr, cols), lambda i: (0, i, 0))],
        out_specs=pl.BlockSpec((tr, cols), lambda i: (i, 0)), out_shape=SDS((rows, cols), f32),
        compiler_params=_cp("parallel"), name=name)(a2)
    return out.reshape(shape)


ANY = pl.BlockSpec(memory_space=pl.ANY)


def chip_exchange(arrs, bcast, *, name):
    n = len(arrs)
    piece = [a.shape if b else a.shape[1:] for a, b in zip(arrs, bcast)]

    def body(*refs):
        ins, outs = refs[:n], refs[n:2 * n]
        send, recv, loc = refs[2 * n:]
        x, y, c = lax.axis_index("x"), lax.axis_index("y"), lax.axis_index("c")
        me = 2 * x + y
        copies = []
        for k in range(n):
            own = pltpu.make_async_copy(ins[k] if bcast[k] else ins[k].at[me], outs[k].at[me], loc.at[k])
            own.start()
            copies.append(own)
            for j, (px, py) in enumerate(((1 - x, y), (x, 1 - y), (1 - x, 1 - y))):
                cp = pltpu.make_async_remote_copy(
                    src_ref=ins[k] if bcast[k] else ins[k].at[2 * px + py], dst_ref=outs[k].at[me],
                    send_sem=send.at[3 * k + j], recv_sem=recv.at[3 * k + j],
                    device_id=(px, py, c), device_id_type=MESH)
                cp.start()
                copies.append(cp)
        for cp in copies:
            cp.wait()

    return pl.pallas_call(
        body, in_specs=[ANY] * n, out_specs=[ANY] * n,
        out_shape=[SDS((4,) + tuple(p), a.dtype) for p, a in zip(piece, arrs)],
        scratch_shapes=[pltpu.SemaphoreType.DMA((3 * n,)), pltpu.SemaphoreType.DMA((3 * n,)),
                        pltpu.SemaphoreType.DMA((n,))],
        name=name)(*arrs)


def gather_weights(shards, *, name):
    n = len(shards)
    hd = shards[0].shape[0] // 2

    def body(*refs):
        ins, outs = refs[:n], refs[n:2 * n]
        send, recv, loc = refs[2 * n:]
        x, y, c = lax.axis_index("x"), lax.axis_index("y"), lax.axis_index("c")
        me = 2 * x + y
        chips = ((1 - x, y), (x, 1 - y), (1 - x, 1 - y))
        mine, theirs = pl.ds(c * hd, hd), pl.ds((1 - c) * hd, hd)

        def ici(k, j, src, dst):
            px, py = chips[j]
            return pltpu.make_async_remote_copy(src_ref=src, dst_ref=dst, send_sem=send.at[6 * k + j],
                                                recv_sem=recv.at[6 * k + j], device_id=(px, py, c),
                                                device_id_type=MESH)

        def d2d(k, j, src, dst):
            return pltpu.make_async_remote_copy(src_ref=src, dst_ref=dst, send_sem=send.at[6 * k + 3 + j],
                                                recv_sem=recv.at[6 * k + 3 + j], device_id=(x, y, 1 - c),
                                                device_id_type=MESH)

        local, sent = [], []
        for k in range(n):
            local.append(pltpu.make_async_copy(ins[k], outs[k].at[me], loc.at[k]))
            local[-1].start()
            for j in range(3):
                sent.append(ici(k, j, ins[k].at[mine], outs[k].at[me, mine]))
                sent[-1].start()
        for k in range(n):
            for j, (px, py) in enumerate(chips):
                landed = outs[k].at[2 * px + py, mine]
                ici(k, j, landed, landed).wait_recv()
                sent.append(d2d(k, j, landed, landed))
                sent[-1].start()
        for k in range(n):
            for j, (px, py) in enumerate(chips):
                other = outs[k].at[2 * px + py, theirs]
                d2d(k, j, other, other).wait_recv()
        for cp in sent:
            cp.wait_send()
        for cp in local:
            cp.wait()

    return pl.pallas_call(
        body, in_specs=[ANY] * n, out_specs=[ANY] * n,
        out_shape=[SDS((4,) + tuple(a.shape), a.dtype) for a in shards],
        scratch_shapes=[pltpu.SemaphoreType.DMA((6 * n,)), pltpu.SemaphoreType.DMA((6 * n,)),
                        pltpu.SemaphoreType.DMA((n,))],
        name=name)(*shards)


def sibling_exchange(arrs, half, *, name):
    n = len(arrs)
    piece = [a.shape[1:] if h else a.shape for a, h in zip(arrs, half)]

    def body(*refs):
        ins, outs = refs[:n], refs[n:2 * n]
        send, recv = refs[2 * n:]
        x, y, c = lax.axis_index("x"), lax.axis_index("y"), lax.axis_index("c")
        copies = []
        for k in range(n):
            cp = pltpu.make_async_remote_copy(
                src_ref=ins[k].at[1 - c] if half[k] else ins[k], dst_ref=outs[k],
                send_sem=send.at[k], recv_sem=recv.at[k], device_id=(x, y, 1 - c), device_id_type=MESH)
            cp.start()
            copies.append(cp)
        for cp in copies:
            cp.wait()

    return pl.pallas_call(
        body, in_specs=[ANY] * n, out_specs=[ANY] * n,
        out_shape=[SDS(tuple(p), a.dtype) for p, a in zip(piece, arrs)],
        scratch_shapes=[pltpu.SemaphoreType.DMA((n,)), pltpu.SemaphoreType.DMA((n,))],
        name=name)(*arrs)


def ssm_discretize(lam_re, lam_im, log_dt, b_re, b_im, c_re, c_im):
    dt = jnp.exp(log_dt)[..., None]
    mag = jnp.exp(lam_re * dt)
    abr = mag * jnp.cos(lam_im * dt)
    abi = mag * jnp.sin(lam_im * dt)
    den = lam_re * lam_re + lam_im * lam_im
    zr = ((abr - 1.0) * lam_re + abi * lam_im) / den
    zi = (abi * lam_re - (abr - 1.0) * lam_im) / den
    bbr = zr[..., None] * b_re - zi[..., None] * b_im
    bbi = zr[..., None] * b_im + zi[..., None] * b_re
    eye = jnp.eye(8, dtype=f32)
    bb = jnp.stack([bbr, bbi], axis=1).reshape(2, 2, SSM_TILES, 8, SSM_STATE, SSM_GROUP)
    bmat = jnp.einsum('dqjgph,gk->djghqkp', bb, eye).reshape(2, SSM_TILES, TILE_CH, 2 * TILE_ST)
    cc = jnp.stack([c_re, -c_im], axis=1).reshape(2, 2, SSM_TILES, 8, SSM_GROUP, SSM_STATE)
    cmat = jnp.einsum('dqjghp,gk->djqkpgh', cc, eye).reshape(2, SSM_TILES, 2 * TILE_ST, TILE_CH)
    n = SSM_GROUPS * SSM_STATE
    return abr.reshape(2, n), abi.reshape(2, n), bmat, cmat


def scan_tables(ar, ai, rev, nv):
    pw = [(ar, ai)]
    for _ in range(nv - 1):
        pw.append(_cmul(ar, ai, *pw[-1]))
    big = [pw[nv - 1]]
    big.append(_cmul(*big[0], *big[0]))
    big.append(_cmul(*big[1], *big[1]))
    rows = jnp.arange(8)[:, None]
    ones = jnp.ones((8, 1), f32)
    parts = []
    for k, p in zip((1, 2, 4), big):
        cond = (rows <= 7 - k) if rev else (rows >= k)
        parts.append([jnp.where(cond, q[None, :], 0.0) for q in p])
    parts.append([ones * q[None, :] for q in big[0]])
    parts.append([ones * q[None, :] for q in pw[0]])
    for v in range(nv):
        parts.append([ones * q[None, :] for q in pw[nv - 1 - v if rev else v]])
    nrow = 40 + 8 * nv
    tre = jnp.concatenate([p[0] for p in parts], axis=0).reshape(nrow, SSM_TILES, TILE_ST)
    tim = jnp.concatenate([p[1] for p in parts], axis=0).reshape(nrow, SSM_TILES, TILE_ST)
    return jnp.concatenate([tre, tim], axis=-1).transpose(1, 0, 2)


def _tile_a(ga):
    t = ga.sum(axis=0).reshape(SSM_TILES, 2, TILE_ST)
    return t[:, 0].reshape(-1), t[:, 1].reshape(-1)


SMALL = ('norm1', 'q_gain', 'k_gain', 'sink', 'lam_re', 'lam_im', 'log_dt', 'b_re', 'b_im', 'c_re', 'c_im',
         'd_skip', 'norm2')
BIG = ('w_in', 'w_glu', 'w_out', 'w_ff1', 'w_ff2')
WEIGHTS = ('norm1', 'w_in', 'q_gain', 'k_gain', 'sink', 'lam_re', 'lam_im', 'log_dt', 'b_re', 'b_im', 'c_re',
           'c_im', 'd_skip', 'w_glu', 'w_out', 'norm2', 'w_ff1', 'w_ff2')


def _chunk(s):
    return min(256, s)


def layer_forward(l, x, p, wb):
    s = x.shape[0]
    tm = min(512, s)
    sv = {}
    h1, z = rms_mm(x, p['norm1'], wb['w_in'], name=f"l{l}_in", tm=tm, tn=640)
    att = attn_fwd(z, p['q_gain'], p['k_gain'], p['sink'], name=f"l{l}_attn")
    (ar, ai, bmat, cmat), disc_vjp = jax.vjp(
        ssm_discretize, p['lam_re'], p['lam_im'], p['log_dt'], p['b_re'], p['b_im'], p['c_re'], p['c_im'])
    bmat16, cmat16 = bmat.astype(MX), cmat.astype(MX)
    ys, xbs, tabs = [], [], []
    for d, rev in enumerate((False, True)):
        tab = scan_tables(ar[d], ai[d], rev, _chunk(s) // 8)
        y_d, xb_d = ssm_fwd(z, tab, bmat16[d], cmat16[d], rev=rev, name=f"l{l}_ssm{d}", chunk=_chunk(s))
        ys.append(y_d)
        xbs.append(xb_d)
        tabs.append((tab, scan_tables(ar[d], -ai[d], not rev, _chunk(s) // 8)))
    ypre, gg, mix = glu_fwd(ys[0], ys[1], z, att, p['d_skip'], wb['w_glu'], name=f"l{l}_glu", tm=min(256, s))
    x1 = mm_res(mix, wb['w_out'], x, relu2=False, name=f"l{l}_out", tm=tm, tn=512)
    h2, f = rms_mm(x1, p['norm2'], wb['w_ff1'], name=f"l{l}_ff1", tm=tm, tn=1024)
    x2 = mm_res(f, wb['w_ff2'], x1, relu2=True, name=f"l{l}_ff2", tm=min(256, s), tn=1024)
    sv.update(x=x, h1=h1, z=z, xbs=xbs, tabs=tabs, bmat16=bmat16, cmat16=cmat16, disc_vjp=disc_vjp,
              ypre=ypre, gg=gg, mix=mix, x1=x1, h2=h2, f=f)
    return x2, sv


def layer_backward(l, gx2, p, wb, sv):
    s = gx2.shape[0]
    tm = min(512, s)
    ts = min(512, s)
    g = {}
    gf = mm_nt(gx2, wb['w_ff2'], name=f"l{l}_bff2", tm=tm, tn=1024, f=sv['f'])
    g['w_ff2'] = mm_tn(sv['f'], gx2, relu2=True, name=f"l{l}_wff2", tk=1024, tn=1024, ts=ts)
    gx1, gn2 = mm_nt_norm(gf, wb['w_ff1'], sv['x1'], p['norm2'], gx2, name=f"l{l}_bff1", tm=min(256, s))
    g['norm2'] = gn2.sum(axis=0)
    g['w_ff1'] = mm_tn(sv['h2'], gf, relu2=False, name=f"l{l}_wff1", tk=1024, tn=1024, ts=ts)
    gmix = mm_nt(gx1, wb['w_out'], name=f"l{l}_bout", tm=tm, tn=512)
    g['w_out'] = mm_tn(sv['mix'], gx1, relu2=False, name=f"l{l}_wout", tk=1024, tn=1024, ts=ts)
    ggg, yg, gy = glu_bwd(gmix, sv['gg'], sv['ypre'], wb['w_glu'], name=f"l{l}_bglu", tm=min(256, s))
    g['w_glu'] = mm_tn(yg, ggg, relu2=False, name=f"l{l}_wglu", tk=512, tn=1024, ts=ts)
    gus, gas, gbs, gcs = [], [], [], []
    for d, rev in enumerate((False, True)):
        tab_s, tab_a = sv['tabs'][d]
        gu_d, ga_d, gb_d, gc_d = ssm_bwd(sv['z'], gy, sv['xbs'][d], tab_s, tab_a, sv['bmat16'][d], sv['cmat16'][d],
                                         rev=rev, name=f"l{l}_bssm{d}", chunk=_chunk(s))
        gus.append(gu_d)
        gas.append(_tile_a(ga_d))
        gbs.append(gb_d)
        gcs.append(gc_d)
    gar = jnp.stack([gas[0][0], gas[1][0]])
    gai = jnp.stack([gas[0][1], gas[1][1]])
    (g['lam_re'], g['lam_im'], g['log_dt'], g['b_re'], g['b_im'], g['c_re'], g['c_im']) = sv['disc_vjp'](
        (gar, gai, jnp.stack(gbs), jnp.stack(gcs)))
    gq, dkv, gqg, gsk = attn_bwd(sv['z'], gmix, p['q_gain'], p['k_gain'], p['sink'], name=f"l{l}_battn")
    g['q_gain'] = gqg.sum(axis=0)
    g['sink'] = gsk[:, 0]
    gz, gkg, gd = gz_assemble(gq, dkv, sv['z'], p['k_gain'], gus[0], gus[1], gy, p['d_skip'], name=f"l{l}_gz")
    g['k_gain'] = gkg.sum(axis=0)
    g['d_skip'] = gd.sum(axis=0)
    gx, gn1 = mm_nt_norm(gz, wb['w_in'], sv['x'], p['norm1'], gx1, name=f"l{l}_bin", tm=tm)
    g['norm1'] = gn1.sum(axis=0)
    g['w_in'] = mm_tn(sv['h1'], gz, relu2=False, name=f"l{l}_win", tk=1024, tn=640, ts=ts)
    return gx, g


def local_step(x, target, small, big16):
    depth = big16['w_in'].shape[0]
    saves = []
    for l in range(depth):
        p = {k: small[k][l] for k in SMALL}
        wb = {k: big16[k][l] for k in BIG}
        x, sv = layer_forward(l, x, p, wb)
        saves.append(sv)
    gx, lparts = loss_grad(x, target, name="loss", tm=min(512, x.shape[0]))
    grads = [None] * depth
    for l in reversed(range(depth)):
        p = {k: small[k][l] for k in SMALL}
        wb = {k: big16[k][l] for k in BIG}
        gx, grads[l] = layer_backward(l, gx, p, wb, saves[l])
    return lparts, gx, grads


COL_SHARDED = ('w_in', 'w_glu', 'w_ff1')


def _gather_big(shards):
    names = list(BIG)
    got = gather_weights([shards[k].astype(WIRE) for k in names], name="gather_w")
    full = {}
    for k, a in zip(names, got):
        if k in COL_SHARDED:
            full[k] = a.transpose(1, 2, 0, 3).reshape(a.shape[1], a.shape[2], 4 * a.shape[3])
        else:
            full[k] = a.transpose(1, 0, 2, 3).reshape(a.shape[1], 4 * a.shape[2], a.shape[3])
    return full


def _to_pieces(k, g):
    dp, r, c = g.shape
    if k in COL_SHARDED:
        return g.reshape(2, dp // 2, r, 4, c // 4).transpose(0, 3, 1, 2, 4)
    return g.reshape(2, dp // 2, 4, r // 4, c).transpose(0, 2, 1, 3, 4)


def _pack_small(tree):
    flat = jnp.concatenate([tree[k].reshape(-1) for k in SMALL])
    pad = (-flat.shape[0]) % 1024
    return jnp.pad(flat, (0, pad)).reshape(-1, 128)


def _unpack_small(packed, like):
    flat = packed.reshape(-1)
    out, off = {}, 0
    for k in SMALL:
        n = like[k].size
        out[k] = flat[off:off + n].reshape(like[k].shape)
        off += n
    return out


def kernel(x, norm1, w_in, q_gain, k_gain, sink, lam_re, lam_im, log_dt, b_re, b_im, c_re, c_im, d_skip, w_glu, w_out, norm2, w_ff1, w_ff2, loss_target, m_norm1, m_w_in, m_q_gain, m_k_gain, m_sink, m_lam_re, m_lam_im, m_log_dt, m_b_re, m_b_im, m_c_re, m_c_im, m_d_skip, m_w_glu, m_w_out, m_norm2, m_w_ff1, m_w_ff2, v_norm1, v_w_in, v_q_gain, v_k_gain, v_sink, v_lam_re, v_lam_im, v_log_dt, v_b_re, v_b_im, v_c_re, v_c_im, v_d_skip, v_w_glu, v_w_out, v_norm2, v_w_ff1, v_w_ff2):
    w = dict(norm1=norm1, w_in=w_in, q_gain=q_gain, k_gain=k_gain, sink=sink, lam_re=lam_re, lam_im=lam_im,
             log_dt=log_dt, b_re=b_re, b_im=b_im, c_re=c_re, c_im=c_im, d_skip=d_skip, w_glu=w_glu, w_out=w_out,
             norm2=norm2, w_ff1=w_ff1, w_ff2=w_ff2)
    m = dict(norm1=m_norm1, w_in=m_w_in, q_gain=m_q_gain, k_gain=m_k_gain, sink=m_sink, lam_re=m_lam_re,
             lam_im=m_lam_im, log_dt=m_log_dt, b_re=m_b_re, b_im=m_b_im, c_re=m_c_re, c_im=m_c_im,
             d_skip=m_d_skip, w_glu=m_w_glu, w_out=m_w_out, norm2=m_norm2, w_ff1=m_w_ff1, w_ff2=m_w_ff2)
    v = dict(norm1=v_norm1, w_in=v_w_in, q_gain=v_q_gain, k_gain=v_k_gain, sink=v_sink, lam_re=v_lam_re,
             lam_im=v_lam_im, log_dt=v_log_dt, b_re=v_b_re, b_im=v_b_im, c_re=v_c_re, c_im=v_c_im,
             d_skip=v_d_skip, w_glu=v_w_glu, w_out=v_w_out, norm2=v_norm2, w_ff1=v_w_ff1, w_ff2=v_w_ff2)
    c = lax.axis_index("c")
    depth = w_in.shape[0]

    big16 = _gather_big({k: w[k] for k in BIG})
    small = {k: w[k] for k in SMALL}
    lparts, gx, grads = local_step(x[0], loss_target[0], small, big16)
    loss = lax.psum(0.5 * jnp.sum(lparts) / D_MODEL, ("x", "y", "c"))

    gbig = {k: _to_pieces(k, jnp.stack([grads[l][k] for l in range(depth)])) for k in BIG}
    gsmall = _pack_small({k: jnp.stack([grads[l][k] for l in range(depth)]) for k in SMALL})
    names = list(BIG)
    got = sibling_exchange([gbig[k] for k in names] + [gsmall], [True] * len(names) + [False], name="reduce_sib")
    mine = [lax.dynamic_index_in_dim(gbig[k], c, 0, keepdims=False) for k in names] + [gsmall]
    halves = [_elementwise(lambda a, b: (a + b,), [a, b], 1, name=f"add_sib{i}",
                           out_dtype=WIRE if i < len(names) else f32)[0]
              for i, (a, b) in enumerate(zip(mine, got))]
    got = chip_exchange(halves, [False] * len(names) + [True], name="reduce_chips")
    sums = [sum4(a, name=f"add_chips{i}") for i, a in enumerate(got)]
    back = sibling_exchange(sums[:-1], [False] * len(names), name="reduce_back")
    gfull = {}
    for k, own, sib in zip(names, sums[:-1], back):
        lo = jnp.where(c == 0, own, sib)
        hi = jnp.where(c == 0, sib, own)
        gfull[k] = jnp.concatenate([lo, hi], axis=0)
    like = {k: w[k] for k in SMALL}
    gfull.update(_unpack_small(sums[-1], like))

    delta, new_m, new_v = {}, {}, {}
    for k in BIG:
        delta[k], new_m[k], new_v[k] = adamw(w[k], gfull[k], m[k], v[k], name=f"adamw_{k}")
    ds, ms, vs = adamw(_pack_small(like), sums[-1], _pack_small({k: m[k] for k in SMALL}),
                       _pack_small({k: v[k] for k in SMALL}), name="adamw_small")
    delta.update(_unpack_small(ds, like))
    new_m.update(_unpack_small(ms, like))
    new_v.update(_unpack_small(vs, like))

    return (loss, gx[None], *[gfull[k] for k in WEIGHTS], *[delta[k] for k in WEIGHTS],
            *[new_m[k] for k in WEIGHTS], *[new_v[k] for k in WEIGHTS])
```

```python
import functools
import math

import jax
import jax.numpy as jnp
from jax import lax
from jax.experimental import pallas as pl
from jax.experimental.pallas import tpu as pltpu

f32 = jnp.float32
MX = jnp.bfloat16
WIRE = jnp.bfloat16
SDS = jax.ShapeDtypeStruct

D_MODEL = 1024
DEPTH = 4
ATT_HEADS = 8
KV_HEADS = 2
GQA = ATT_HEADS // KV_HEADS
HEAD_DIM = 64
ATT_WIDTH = ATT_HEADS * HEAD_DIM
KV_WIDTH = KV_HEADS * HEAD_DIM
BLOCK = 128
SSM_WIDTH = 512
SSM_GROUP = 16
SSM_GROUPS = 32
SSM_STATE = 64
SSM_TILES = 4
TILE_CH = SSM_WIDTH // SSM_TILES
TILE_ST = SSM_GROUPS * SSM_STATE // SSM_TILES
SLAB = 256
IN_WIDTH = ATT_WIDTH + 2 * KV_WIDTH + SSM_WIDTH
U_OFF = ATT_WIDTH + 2 * KV_WIDTH
D_FF = 4096
EPS = 1e-6
NEG = float(jnp.finfo(jnp.float32).min)
SLOPES = tuple(2.0 ** (-8.0 * (h + 1) / ATT_HEADS) for h in range(ATT_HEADS))

ADAM_LR, ADAM_B1, ADAM_B2, ADAM_EPS, ADAM_WD, ADAM_STEP = 0.001, 0.9, 0.999, 1e-08, 0.01, 10

VMEM_LIMIT = 48 * 1024 * 1024
MESH = pl.DeviceIdType.MESH

NT = (((1,), (1,)), ((), ()))
TN = (((0,), (0,)), ((), ()))


def _cp(*sem):
    return pltpu.CompilerParams(dimension_semantics=sem, vmem_limit_bytes=VMEM_LIMIT)


def _dot(a, b, dims=None):
    if dims is None:
        return jnp.dot(a, b, preferred_element_type=f32)
    return lax.dot_general(a, b, dims, preferred_element_type=f32)


def _rows8(v):
    return v.reshape(v.shape[0] // 8, 8, v.shape[1]).sum(axis=0)


def _layer_spec(w, l):
    nd = w.ndim
    return pl.BlockSpec((1,) + tuple(w.shape[1:]), lambda i: (l,) + (0,) * (nd - 1))


def _row_spec(tm, width):
    return pl.BlockSpec((tm, width), lambda i: (i, 0))


def norm_mm(x, gain, w, l, *, relu2, name, tm):
    s, d = x.shape
    if relu2:
        nblk, cb = w.shape[1], w.shape[3]
        n = nblk * cb
    else:
        n = w.shape[2]

    def body(x_ref, g_ref, w_ref, h_ref, y_ref):
        xf = x_ref[...]
        r = lax.rsqrt(jnp.mean(xf * xf, axis=-1, keepdims=True) + EPS)
        h = (xf * r * g_ref[...]).astype(MX)
        h_ref[...] = h
        if relu2:
            for b in range(nblk):
                f = jnp.maximum(_dot(h, w_ref[0, b]), 0.0)
                y_ref[:, cb * b:cb * (b + 1)] = (f * f).astype(MX)
        else:
            y_ref[...] = _dot(h, w_ref[0])

    return pl.pallas_call(
        body, grid=(s // tm,),
        in_specs=[_row_spec(tm, d), pl.BlockSpec((1, d), lambda i: (0, 0)), _layer_spec(w, l)],
        out_specs=[_row_spec(tm, d), _row_spec(tm, n)],
        out_shape=[SDS((s, d), MX), SDS((s, n), MX if relu2 else f32)],
        compiler_params=_cp("parallel"), name=name)(x, gain.reshape(1, d), w)


def mm_res(a, w, l, res, *, name, tm):
    s, k = a.shape
    n = w.shape[2]

    def body(a_ref, w_ref, r_ref, o_ref):
        o_ref[...] = r_ref[...] + _dot(a_ref[...], w_ref[0])

    return pl.pallas_call(
        body, grid=(s // tm,), in_specs=[_row_spec(tm, k), _layer_spec(w, l), _row_spec(tm, n)],
        out_specs=_row_spec(tm, n), out_shape=SDS((s, n), f32), compiler_params=_cp("parallel"), name=name)(a, w, res)


def mm_nt(gy, w, l, *, name, tm, a2=None):
    s, n = gy.shape
    k = w.shape[1]
    kb = min(k, 1024)

    def body(*refs):
        if a2 is None:
            g_ref, w_ref, o_ref = refs
        else:
            g_ref, w_ref, a_ref, o_ref = refs
        g = g_ref[...]
        for b in range(k // kb):
            cols = slice(kb * b, kb * (b + 1))
            acc = _dot(g, w_ref[0, cols, :], NT)
            if a2 is not None:
                acc = acc * (2.0 * jnp.sqrt(a_ref[:, cols].astype(f32)))
            o_ref[:, cols] = acc.astype(o_ref.dtype)

    in_specs = [_row_spec(tm, n), _layer_spec(w, l)]
    args = [gy, w]
    if a2 is not None:
        in_specs.append(_row_spec(tm, k))
        args.append(a2)
    return pl.pallas_call(
        body, grid=(s // tm,), in_specs=in_specs, out_specs=_row_spec(tm, k),
        out_shape=SDS((s, k), f32 if a2 is None else MX), compiler_params=_cp("parallel"), name=name)(*args)


def mm_nt_norm(gy, w, l, x, gain, res, *, name, tm):
    s, n = gy.shape
    d = x.shape[1]

    def body(g_ref, w_ref, x_ref, gn_ref, r_ref, o_ref, o16_ref, gg_ref):
        @pl.when(pl.program_id(0) == 0)
        def _():
            gg_ref[...] = jnp.zeros_like(gg_ref)

        if w.ndim == 3:
            gh = _dot(g_ref[...], w_ref[0], NT)
        else:
            cb = w.shape[3]
            gh = _dot(g_ref[:, 0:cb], w_ref[0, 0], NT)
            for b in range(1, w.shape[1]):
                gh = gh + _dot(g_ref[:, cb * b:cb * (b + 1)], w_ref[0, b], NT)
        xf = x_ref[...]
        r = lax.rsqrt(jnp.mean(xf * xf, axis=-1, keepdims=True) + EPS)
        xh = xf * r
        t = gh * gn_ref[...]
        gx = r_ref[...] + r * (t - xh * jnp.mean(t * xh, axis=-1, keepdims=True))
        o_ref[...] = gx
        o16_ref[...] = gx.astype(MX)
        gg_ref[...] += _rows8(gh * xh)

    return pl.pallas_call(
        body, grid=(s // tm,),
        in_specs=[_row_spec(tm, n), _layer_spec(w, l), _row_spec(tm, d), pl.BlockSpec((1, d), lambda i: (0, 0)),
                  _row_spec(tm, d)],
        out_specs=[_row_spec(tm, d), _row_spec(tm, d), pl.BlockSpec((8, d), lambda i: (0, 0))],
        out_shape=[SDS((s, d), f32), SDS((s, d), MX), SDS((8, d), f32)],
        compiler_params=_cp("arbitrary"), name=name)(gy, w, x, gain.reshape(1, d), res)


def mm_tn(xa, gy, *, name, tk, tn, ts, chip_major=False):
    s, k = xa.shape
    n = gy.shape[1]

    def body(x_ref, g_ref, o_ref):
        @pl.when(pl.program_id(2) == 0)
        def _():
            o_ref[...] = jnp.zeros_like(o_ref)

        acc = _dot(x_ref[...], g_ref[...], TN)
        if chip_major:
            o_ref[0] += acc
        else:
            o_ref[...] += acc

    if chip_major:
        out_spec = pl.BlockSpec((1, tk, tn), lambda a, b, c: (b, a, 0))
        out_shape = SDS((n // tn, k, tn), f32)
    else:
        out_spec = pl.BlockSpec((tk, tn), lambda a, b, c: (a, b))
        out_shape = SDS((k, n), f32)
    return pl.pallas_call(
        body, grid=(k // tk, n // tn, s // ts),
        in_specs=[pl.BlockSpec((ts, tk), lambda a, b, c: (c, a)), pl.BlockSpec((ts, tn), lambda a, b, c: (c, b))],
        out_specs=out_spec, out_shape=out_shape,
        compiler_params=_cp("parallel", "parallel", "arbitrary"), name=name)(xa, gy)


def _head_norm(t):
    r = lax.rsqrt(jnp.mean(t * t, axis=-1, keepdims=True) + EPS)
    return t * r, r


def _attn_mask(i, nb):
    row = lax.broadcasted_iota(jnp.int32, (BLOCK, 3 * BLOCK), 0)
    col = lax.broadcasted_iota(jnp.int32, (BLOCK, 3 * BLOCK), 1)
    dist = jnp.abs(row - col + BLOCK)
    valid = (dist <= BLOCK) & ((col >= BLOCK) | (i >= 1)) & ((col < 2 * BLOCK) | (i <= nb - 2))
    return dist.astype(f32), valid


def _attn_specs(nb):
    return [pl.BlockSpec((BLOCK, ATT_WIDTH), lambda i: (i, 0)),
            pl.BlockSpec((BLOCK, 2 * KV_WIDTH), lambda i: (jnp.maximum(i - 1, 0), 2)),
            pl.BlockSpec((BLOCK, 2 * KV_WIDTH), lambda i: (i, 2)),
            pl.BlockSpec((BLOCK, 2 * KV_WIDTH), lambda i: (jnp.minimum(i + 1, nb - 1), 2)),
            pl.BlockSpec((1, HEAD_DIM), lambda i: (0, 0)),
            pl.BlockSpec((1, HEAD_DIM), lambda i: (0, 0)),
            pl.BlockSpec(memory_space=pltpu.SMEM)]


def _attn_probs(s_g, head, distf, valid, sink_ref):
    sg = jnp.where(valid, s_g - SLOPES[head] * distf, NEG)
    sk = sink_ref[head]
    m = jnp.maximum(jnp.max(sg, axis=-1, keepdims=True), sk)
    e = jnp.exp(sg - m)
    es = jnp.exp(sk - m)
    den = jnp.sum(e, axis=-1, keepdims=True) + es
    return e / den, es / den


def attn_fwd(z, q_gain, k_gain, sink, *, name):
    s = z.shape[0]
    nb = s // BLOCK

    def body(q_ref, kp_ref, kc_ref, kn_ref, qg_ref, kg_ref, sink_ref, o_ref):
        i = pl.program_id(0)
        distf, valid = _attn_mask(i, nb)
        kv = jnp.concatenate([kp_ref[...], kc_ref[...], kn_ref[...]], axis=0)
        for kvh in range(KV_HEADS):
            kn, _ = _head_norm(kv[:, HEAD_DIM * kvh:HEAD_DIM * (kvh + 1)])
            kn = (kn * kg_ref[...]).astype(MX)
            vh = kv[:, KV_WIDTH + HEAD_DIM * kvh:KV_WIDTH + HEAD_DIM * (kvh + 1)].astype(MX)
            qs = []
            for g in range(GQA):
                h = GQA * kvh + g
                qn, _ = _head_norm(q_ref[:, HEAD_DIM * h:HEAD_DIM * (h + 1)])
                qs.append((qn * qg_ref[...]).astype(MX))
            sc = _dot(jnp.concatenate(qs, axis=0), kn, NT) * 0.125
            for g in range(GQA):
                h = GQA * kvh + g
                p, _ = _attn_probs(sc[BLOCK * g:BLOCK * (g + 1)], h, distf, valid, sink_ref)
                o_ref[:, HEAD_DIM * h:HEAD_DIM * (h + 1)] = _dot(p.astype(MX), vh).astype(o_ref.dtype)

    return pl.pallas_call(
        body, grid=(nb,), in_specs=_attn_specs(nb),
        out_specs=pl.BlockSpec((BLOCK, ATT_WIDTH), lambda i: (i, 0)),
        out_shape=SDS((s, ATT_WIDTH), MX), compiler_params=_cp("parallel"), name=name)(
            z, z, z, z, q_gain.reshape(1, HEAD_DIM), k_gain.reshape(1, HEAD_DIM), sink)


def attn_bwd(z, gmix, q_gain, k_gain, sink, *, name):
    s = z.shape[0]
    nb = s // BLOCK

    def body(q_ref, kp_ref, kc_ref, kn_ref, qg_ref, kg_ref, sink_ref, go_ref, gq_ref, dkv_ref, gqg_ref, gs_ref):
        i = pl.program_id(0)

        @pl.when(i == 0)
        def _():
            gqg_ref[...] = jnp.zeros_like(gqg_ref)
            gs_ref[...] = jnp.zeros_like(gs_ref)

        distf, valid = _attn_mask(i, nb)
        kv = jnp.concatenate([kp_ref[...], kc_ref[...], kn_ref[...]], axis=0)
        for kvh in range(KV_HEADS):
            kn, _ = _head_norm(kv[:, HEAD_DIM * kvh:HEAD_DIM * (kvh + 1)])
            kn = (kn * kg_ref[...]).astype(MX)
            vh = kv[:, KV_WIDTH + HEAD_DIM * kvh:KV_WIDTH + HEAD_DIM * (kvh + 1)].astype(MX)
            qhat, qr, qs, dos = [], [], [], []
            for g in range(GQA):
                h = GQA * kvh + g
                qn, r = _head_norm(q_ref[:, HEAD_DIM * h:HEAD_DIM * (h + 1)])
                qhat.append(qn)
                qr.append(r)
                qs.append((qn * qg_ref[...]).astype(MX))
                dos.append(go_ref[:, HEAD_DIM * h:HEAD_DIM * (h + 1)].astype(MX))
            qs = jnp.concatenate(qs, axis=0)
            dos = jnp.concatenate(dos, axis=0)
            sc = _dot(qs, kn, NT) * 0.125
            dp = _dot(dos, vh, NT)
            ps, dss = [], []
            for g in range(GQA):
                h = GQA * kvh + g
                p, psink = _attn_probs(sc[BLOCK * g:BLOCK * (g + 1)], h, distf, valid, sink_ref)
                dpg = dp[BLOCK * g:BLOCK * (g + 1)]
                delta = jnp.sum(p * dpg, axis=-1, keepdims=True)
                gs_ref[h:h + 1, :] += jnp.broadcast_to(jnp.sum(-psink * delta, axis=0, keepdims=True), (1, 128))
                ps.append(p.astype(MX))
                dss.append((p * (dpg - delta) * 0.125).astype(MX))
            ps = jnp.concatenate(ps, axis=0)
            dss = jnp.concatenate(dss, axis=0)
            gv = _dot(ps, dos, TN)
            gkn = _dot(dss, qs, TN)
            gqn = _dot(dss, kn)
            for g in range(GQA):
                h = GQA * kvh + g
                gq_h = gqn[BLOCK * g:BLOCK * (g + 1)]
                gqg_ref[h:h + 1, :] += jnp.sum(gq_h * qhat[g], axis=0, keepdims=True)
                t = gq_h * qg_ref[...]
                gq_ref[:, HEAD_DIM * h:HEAD_DIM * (h + 1)] = qr[g] * (
                    t - qhat[g] * jnp.mean(t * qhat[g], axis=-1, keepdims=True))
            for b in range(3):
                dkv_ref[b, :, HEAD_DIM * kvh:HEAD_DIM * (kvh + 1)] = gkn[BLOCK * b:BLOCK * (b + 1)]
                dkv_ref[b, :, KV_WIDTH + HEAD_DIM * kvh:KV_WIDTH + HEAD_DIM * (kvh + 1)] = gv[BLOCK * b:BLOCK * (b + 1)]

    return pl.pallas_call(
        body, grid=(nb,),
        in_specs=_attn_specs(nb) + [pl.BlockSpec((BLOCK, ATT_WIDTH), lambda i: (i, 0))],
        out_specs=[pl.BlockSpec((BLOCK, ATT_WIDTH), lambda i: (i, 0)),
                   pl.BlockSpec((3, BLOCK, 2 * KV_WIDTH), lambda i: (0, i, 0)),
                   pl.BlockSpec((ATT_HEADS, HEAD_DIM), lambda i: (0, 0)),
                   pl.BlockSpec((ATT_HEADS, 128), lambda i: (0, 0))],
        out_shape=[SDS((s, ATT_WIDTH), f32), SDS((3, s, 2 * KV_WIDTH), f32),
                   SDS((ATT_HEADS, HEAD_DIM), f32), SDS((ATT_HEADS, 128), f32)],
        compiler_params=_cp("arbitrary"), name=name)(
            z, z, z, z, q_gain.reshape(1, HEAD_DIM), k_gain.reshape(1, HEAD_DIM), sink, gmix)


def gz_assemble(gq, dkv, z, k_gain, gu_f, gu_r, gy, d_skip, *, name):
    s = z.shape[0]
    nb = s // BLOCK

    def body(gq_ref, d0_ref, d1_ref, d2_ref, z_ref, kg_ref, guf_ref, gur_ref, gy_ref, ds_ref, gz_ref, gkg_ref, gd_ref):
        i = pl.program_id(0)

        @pl.when(i == 0)
        def _():
            gkg_ref[...] = jnp.zeros_like(gkg_ref)
            gd_ref[...] = jnp.zeros_like(gd_ref)

        gkv = d1_ref[0] + jnp.where(i + 1 < nb, d0_ref[0], 0.0) + jnp.where(i >= 1, d2_ref[0], 0.0)
        gz_ref[:, 0:ATT_WIDTH] = gq_ref[...].astype(MX)
        for kvh in range(KV_HEADS):
            kh, r = _head_norm(z_ref[:, ATT_WIDTH + HEAD_DIM * kvh:ATT_WIDTH + HEAD_DIM * (kvh + 1)])
            gkn = gkv[:, HEAD_DIM * kvh:HEAD_DIM * (kvh + 1)]
            gkg_ref[kvh:kvh + 1, :] += jnp.sum(gkn * kh, axis=0, keepdims=True)
            t = gkn * kg_ref[...]
            gk = r * (t - kh * jnp.mean(t * kh, axis=-1, keepdims=True))
            gz_ref[:, ATT_WIDTH + HEAD_DIM * kvh:ATT_WIDTH + HEAD_DIM * (kvh + 1)] = gk.astype(MX)
        gz_ref[:, ATT_WIDTH + KV_WIDTH:U_OFF] = gkv[:, KV_WIDTH:].astype(MX)
        gyv = gy_ref[...]
        gz_ref[:, U_OFF:IN_WIDTH] = (guf_ref[...] + gur_ref[...] + ds_ref[...] * gyv).astype(MX)
        gd_ref[...] += _rows8(gyv * z_ref[:, U_OFF:IN_WIDTH])

    row = lambda w: pl.BlockSpec((BLOCK, w), lambda i: (i, 0))
    return pl.pallas_call(
        body, grid=(nb,),
        in_specs=[row(ATT_WIDTH),
                  pl.BlockSpec((1, BLOCK, 2 * KV_WIDTH), lambda i: (0, jnp.minimum(i + 1, nb - 1), 0)),
                  pl.BlockSpec((1, BLOCK, 2 * KV_WIDTH), lambda i: (1, i, 0)),
                  pl.BlockSpec((1, BLOCK, 2 * KV_WIDTH), lambda i: (2, jnp.maximum(i - 1, 0), 0)),
                  row(IN_WIDTH), pl.BlockSpec((1, HEAD_DIM), lambda i: (0, 0)),
                  row(SSM_WIDTH), row(SSM_WIDTH), row(SSM_WIDTH), pl.BlockSpec((1, SSM_WIDTH), lambda i: (0, 0))],
        out_specs=[row(IN_WIDTH), pl.BlockSpec((8, HEAD_DIM), lambda i: (0, 0)),
                   pl.BlockSpec((8, SSM_WIDTH), lambda i: (0, 0))],
        out_shape=[SDS((s, IN_WIDTH), MX), SDS((8, HEAD_DIM), f32), SDS((8, SSM_WIDTH), f32)],
        compiler_params=_cp("arbitrary"), name=name)(
            gq, dkv, dkv, dkv, z, k_gain.reshape(1, HEAD_DIM), gu_f, gu_r, gy, d_skip.reshape(1, SSM_WIDTH))


def _cmul(ar, ai, xr, xi):
    return ar * xr - ai * xi, ar * xi + ai * xr


def _permute_rows(src_ref, dst_ref, nv):
    for v in range(nv):
        dst_ref[8 * v:8 * v + 8, :] = src_ref[pl.ds(v, 8, stride=nv), :]


def _unpermute_rows(val, dst_ref, nv):
    for v in range(nv):
        dst_ref[pl.ds(v, 8, stride=nv), :] = val[8 * v:8 * v + 8, :]


def _scan_chunk(x_ref, tab_ref, carry_ref, nv, rev, acc=None):
    L = TILE_ST
    order = list(range(nv - 1, -1, -1)) if rev else list(range(nv))
    a_r, a_i = tab_ref[32:40, :L], tab_ref[32:40, L:]
    pr = pi = None
    for v in order:
        rows = slice(8 * v, 8 * v + 8)
        xr, xi = x_ref[rows, :L], x_ref[rows, L:]
        if pr is not None:
            mr, mi = _cmul(a_r, a_i, pr, pi)
            xr, xi = xr + mr, xi + mi
            x_ref[rows, :L] = xr
            x_ref[rows, L:] = xi
        pr, pi = xr, xi
    er, ei = pr, pi
    row = lax.broadcasted_iota(jnp.int32, (8, L), 0)
    edge = row == (7 if rev else 0)
    sh = 7 if rev else 1
    fr = jnp.where(edge, carry_ref[:, :L], pltpu.roll(er, sh, 0))
    fi = jnp.where(edge, carry_ref[:, L:], pltpu.roll(ei, sh, 0))
    for n, k in enumerate((1, 2, 4)):
        mr, mi = tab_ref[8 * n:8 * n + 8, :L], tab_ref[8 * n:8 * n + 8, L:]
        sh = (8 - k) if rev else k
        rr, ri = pltpu.roll(fr, sh, 0), pltpu.roll(fi, sh, 0)
        fr, fi = fr + mr * rr - mi * ri, fi + mr * ri + mi * rr
    dr, di = _cmul(tab_ref[24:32, :L], tab_ref[24:32, L:], fr, fi)
    last = 0 if rev else 7
    carry_ref[:, :L] = jnp.broadcast_to((dr + er)[last:last + 1, :], (8, L))
    carry_ref[:, L:] = jnp.broadcast_to((di + ei)[last:last + 1, :], (8, L))
    qr, qi = fr, fi
    if acc is not None:
        sr, si = jnp.zeros((8, L), f32), jnp.zeros((8, L), f32)
    for v in order:
        rows = slice(8 * v, 8 * v + 8)
        trow = slice(40 + 8 * v, 48 + 8 * v)
        mr, mi = _cmul(tab_ref[trow, :L], tab_ref[trow, L:], fr, fi)
        xr, xi = x_ref[rows, :L] + mr, x_ref[rows, L:] + mi
        x_ref[rows, :L] = xr
        x_ref[rows, L:] = xi
        if acc is not None:
            gr, gi = acc[0][rows, :L], acc[0][rows, L:]
            sr, si = sr + gr * qr + gi * qi, si + gi * qr - gr * qi
            qr, qi = xr, xi
    if acc is not None:
        acc[1][:, :L] += sr
        acc[1][:, L:] += si


def ssm_fwd(z, tab, bmat, cmat, *, rev, name, chunk):
    s = z.shape[0]
    nc = s // chunk
    nv = chunk // 8
    ci = (lambda i: nc - 1 - i) if rev else (lambda i: i)

    def body(u_ref, tab_ref, b_ref, c_ref, y_ref, xb_ref, u_scr, x_scr, carry):
        @pl.when(pl.program_id(1) == 0)
        def _():
            carry[...] = jnp.zeros_like(carry)

        xb_ref[0] = carry[...]
        _permute_rows(u_ref, u_scr, nv)
        x_scr[...] = _dot(u_scr[...].astype(MX), b_ref[0])
        _scan_chunk(x_scr, tab_ref.at[0], carry, nv, rev)
        _unpermute_rows(_dot(x_scr[...].astype(MX), c_ref[0]), y_ref, nv)

    return pl.pallas_call(
        body, grid=(SSM_TILES, nc),
        in_specs=[pl.BlockSpec((chunk, TILE_CH), lambda j, i: (ci(i), U_OFF // TILE_CH + j)),
                  pl.BlockSpec((1, 40 + chunk, 2 * TILE_ST), lambda j, i: (j, 0, 0)),
                  pl.BlockSpec((1, TILE_CH, 2 * TILE_ST), lambda j, i: (j, 0, 0)),
                  pl.BlockSpec((1, 2 * TILE_ST, TILE_CH), lambda j, i: (j, 0, 0))],
        out_specs=[pl.BlockSpec((chunk, TILE_CH), lambda j, i: (ci(i), j)),
                   pl.BlockSpec((1, 8, 2 * TILE_ST), lambda j, i: (ci(i), 0, j))],
        out_shape=[SDS((s, SSM_WIDTH), f32), SDS((nc, 8, SSM_TILES * 2 * TILE_ST), f32)],
        scratch_shapes=[pltpu.VMEM((chunk, TILE_CH), f32), pltpu.VMEM((chunk, 2 * TILE_ST), f32),
                        pltpu.VMEM((8, 2 * TILE_ST), f32)],
        compiler_params=_cp("parallel", "arbitrary"), name=name)(z, tab, bmat, cmat)


def ssm_bwd(z, gy, xb, tab_s, tab_a, bmat, cmat, *, rev, name, chunk):
    s = z.shape[0]
    nc = s // chunk
    nv = chunk // 8
    ci = (lambda i: i) if rev else (lambda i: nc - 1 - i)

    def body(u_ref, gy_ref, xb_ref, ts_ref, ta_ref, b_ref, c_ref, gu_ref, ga_ref, gb_ref, gc_ref,
             u_scr, gy_scr, x_scr, g_scr, gcarry, xcarry):
        @pl.when(pl.program_id(1) == 0)
        def _():
            gcarry[...] = jnp.zeros_like(gcarry)
            ga_ref[...] = jnp.zeros_like(ga_ref)
            gb_ref[...] = jnp.zeros_like(gb_ref)
            gc_ref[...] = jnp.zeros_like(gc_ref)

        _permute_rows(u_ref, u_scr, nv)
        _permute_rows(gy_ref, gy_scr, nv)
        ub = u_scr[...].astype(MX)
        gyb = gy_scr[...].astype(MX)
        g_scr[...] = _dot(gyb, c_ref[0], NT)
        _scan_chunk(g_scr, ta_ref.at[0], gcarry, nv, not rev)
        x_scr[...] = _dot(ub, b_ref[0])
        xcarry[...] = xb_ref[0]
        _scan_chunk(x_scr, ts_ref.at[0], xcarry, nv, rev, acc=(g_scr, ga_ref))
        gb16 = g_scr[...].astype(MX)
        gb_ref[0] += _dot(ub, gb16, TN)
        gc_ref[0] += _dot(x_scr[...].astype(MX), gyb, TN)
        _unpermute_rows(_dot(gb16, b_ref[0], NT), gu_ref, nv)

    tile3 = lambda a, b: pl.BlockSpec((1, a, b), lambda j, i: (j, 0, 0))
    return pl.pallas_call(
        body, grid=(SSM_TILES, nc),
        in_specs=[pl.BlockSpec((chunk, TILE_CH), lambda j, i: (ci(i), U_OFF // TILE_CH + j)),
                  pl.BlockSpec((chunk, TILE_CH), lambda j, i: (ci(i), j)),
                  pl.BlockSpec((1, 8, 2 * TILE_ST), lambda j, i: (ci(i), 0, j)),
                  tile3(40 + chunk, 2 * TILE_ST), tile3(40 + chunk, 2 * TILE_ST),
                  tile3(TILE_CH, 2 * TILE_ST), tile3(2 * TILE_ST, TILE_CH)],
        out_specs=[pl.BlockSpec((chunk, TILE_CH), lambda j, i: (ci(i), j)),
                   pl.BlockSpec((8, 2 * TILE_ST), lambda j, i: (0, j)),
                   tile3(TILE_CH, 2 * TILE_ST), tile3(2 * TILE_ST, TILE_CH)],
        out_shape=[SDS((s, SSM_WIDTH), f32), SDS((8, SSM_TILES * 2 * TILE_ST), f32),
                   SDS((SSM_TILES, TILE_CH, 2 * TILE_ST), f32), SDS((SSM_TILES, 2 * TILE_ST, TILE_CH), f32)],
        scratch_shapes=[pltpu.VMEM((chunk, TILE_CH), f32), pltpu.VMEM((chunk, TILE_CH), f32),
                        pltpu.VMEM((chunk, 2 * TILE_ST), f32), pltpu.VMEM((chunk, 2 * TILE_ST), f32),
                        pltpu.VMEM((8, 2 * TILE_ST), f32), pltpu.VMEM((8, 2 * TILE_ST), f32)],
        compiler_params=_cp("parallel", "arbitrary"), name=name)(z, gy, xb, tab_s, tab_a, bmat, cmat)


GELU_K = math.sqrt(2.0 / math.pi)


def _gelu(y):
    return 0.5 * y * (1.0 + jnp.tanh(GELU_K * (y + 0.044715 * (y * y * y))))


def _gelu_grad(y):
    t = jnp.tanh(GELU_K * (y + 0.044715 * (y * y * y)))
    return 0.5 * (1.0 + t) + 0.5 * y * (1.0 - t * t) * (GELU_K * (1.0 + 3.0 * 0.044715 * (y * y)))


def glu_fwd(y_f, y_r, z, att, d_skip, w_glu, l, *, name, tm):
    s = z.shape[0]
    nblk, cb = w_glu.shape[1], w_glu.shape[3]

    def body(yf_ref, yr_ref, z_ref, att_ref, d_ref, w_ref, y_ref, gg_ref, mix_ref):
        y = d_ref[...] * z_ref[:, U_OFF:IN_WIDTH] + yf_ref[...] + yr_ref[...]
        y_ref[...] = y
        yg = _gelu(y).astype(MX)
        for b in range(nblk):
            gg_ref[:, cb * b:cb * (b + 1)] = _dot(yg, w_ref[0, b])
        mix_ref[:, 0:ATT_WIDTH] = att_ref[...]
        mix_ref[:, ATT_WIDTH:] = (gg_ref[:, :SSM_WIDTH] * jax.nn.sigmoid(gg_ref[:, SSM_WIDTH:])).astype(MX)

    return pl.pallas_call(
        body, grid=(s // tm,),
        in_specs=[_row_spec(tm, SSM_WIDTH), _row_spec(tm, SSM_WIDTH), _row_spec(tm, IN_WIDTH),
                  _row_spec(tm, ATT_WIDTH), pl.BlockSpec((1, SSM_WIDTH), lambda i: (0, 0)), _layer_spec(w_glu, l)],
        out_specs=[_row_spec(tm, SSM_WIDTH), _row_spec(tm, 2 * SSM_WIDTH), _row_spec(tm, D_MODEL)],
        out_shape=[SDS((s, SSM_WIDTH), f32), SDS((s, 2 * SSM_WIDTH), f32), SDS((s, D_MODEL), MX)],
        compiler_params=_cp("parallel"), name=name)(y_f, y_r, z, att, d_skip.reshape(1, SSM_WIDTH), w_glu)


def glu_bwd(gmix, gg, ypre, w_glu, l, *, name, tm):
    s = gg.shape[0]
    nblk, cb = w_glu.shape[1], w_glu.shape[3]

    def body(gm_ref, gg_ref, y_ref, w_ref, ggg_ref, yg_ref, gy_ref):
        gs = gm_ref[...]
        val, gate = gg_ref[:, :SSM_WIDTH], gg_ref[:, SSM_WIDTH:]
        sg = jax.nn.sigmoid(gate)
        ggg_ref[:, :SSM_WIDTH] = (gs * sg).astype(MX)
        ggg_ref[:, SSM_WIDTH:] = (gs * val * sg * (1.0 - sg)).astype(MX)
        y = y_ref[...]
        yg_ref[...] = _gelu(y).astype(MX)
        gyg = _dot(ggg_ref[:, 0:cb], w_ref[0, 0], NT)
        for b in range(1, nblk):
            gyg = gyg + _dot(ggg_ref[:, cb * b:cb * (b + 1)], w_ref[0, b], NT)
        gy_ref[...] = gyg * _gelu_grad(y)

    return pl.pallas_call(
        body, grid=(s // tm,),
        in_specs=[pl.BlockSpec((tm, SSM_WIDTH), lambda i: (i, 1)), _row_spec(tm, 2 * SSM_WIDTH),
                  _row_spec(tm, SSM_WIDTH), _layer_spec(w_glu, l)],
        out_specs=[_row_spec(tm, 2 * SSM_WIDTH), _row_spec(tm, SSM_WIDTH), _row_spec(tm, SSM_WIDTH)],
        out_shape=[SDS((s, 2 * SSM_WIDTH), MX), SDS((s, SSM_WIDTH), MX), SDS((s, SSM_WIDTH), f32)],
        compiler_params=_cp("parallel"), name=name)(gmix, gg, ypre, w_glu)


def loss_grad(y, target, *, name, tm):
    s, d = y.shape

    def body(y_ref, t_ref, g_ref, g16_ref, l_ref):
        @pl.when(pl.program_id(0) == 0)
        def _():
            l_ref[...] = jnp.zeros_like(l_ref)

        e = y_ref[...] - t_ref[...]
        g = e * (1.0 / d)
        g_ref[...] = g
        g16_ref[...] = g.astype(MX)
        l_ref[...] += _rows8(e * e)

    row = pl.BlockSpec((tm, d), lambda i: (i, 0))
    return pl.pallas_call(
        body, grid=(s // tm,), in_specs=[row, row],
        out_specs=[row, row, pl.BlockSpec((8, d), lambda i: (0, 0))],
        out_shape=[SDS((s, d), f32), SDS((s, d), MX), SDS((8, d), f32)],
        compiler_params=_cp("arbitrary"), name=name)(y, target)


def _row_tile(rows, cols):
    tr = rows
    while tr * cols > 256 * 1024 and tr % 16 == 0:
        tr //= 2
    return tr


def _elementwise(fn, ins, n_out, *, name, out_dtype=f32):
    shape = ins[0].shape
    cols = shape[-1]
    ins2 = [a.reshape(-1, cols) for a in ins]
    rows = ins2[0].shape[0]
    tr = _row_tile(rows, cols)

    def body(*refs):
        outs = fn(*[r[...] for r in refs[:len(ins)]])
        for o_ref, o in zip(refs[len(ins):], outs):
            o_ref[...] = o.astype(out_dtype)

    spec = pl.BlockSpec((tr, cols), lambda i: (i, 0))
    outs = pl.pallas_call(
        body, grid=(rows // tr,), in_specs=[spec] * len(ins), out_specs=[spec] * n_out,
        out_shape=[SDS((rows, cols), out_dtype)] * n_out, compiler_params=_cp("parallel"), name=name)(*ins2)
    return [o.reshape(shape) for o in outs]


def _adamw_math(w, g, m, v):
    m = ADAM_B1 * m + (1.0 - ADAM_B1) * g
    v = ADAM_B2 * v + (1.0 - ADAM_B2) * (g * g)
    m_hat = m / (1.0 - ADAM_B1 ** ADAM_STEP)
    v_hat = v / (1.0 - ADAM_B2 ** ADAM_STEP)
    delta = -ADAM_LR * (m_hat / (jnp.sqrt(v_hat) + ADAM_EPS) + ADAM_WD * w)
    return delta, m, v


def adamw(w, g, m, v, *, name):
    return _elementwise(_adamw_math, [w, g, m, v], 3, name=name)


def sum4(a, *, name):
    shape = a.shape[1:]
    cols = shape[-1]
    a2 = a.reshape(4, -1, cols)
    rows = a2.shape[1]
    tr = _row_tile(rows, cols)

    def body(a_ref, o_ref):
        o_ref[...] = ((a_ref[0].astype(f32) + a_ref[1].astype(f32)) + a_ref[2].astype(f32)) + a_ref[3].astype(f32)

    out = pl.pallas_call(
        body, grid=(rows // tr,), in_specs=[pl.BlockSpec((4, tr, cols), lambda i: (0, i, 0))],
        out_specs=pl.BlockSpec((tr, cols), lambda i: (i, 0)), out_shape=SDS((rows, cols), f32),
        compiler_params=_cp("parallel"), name=name)(a2)
    return out.reshape(shape)


ANY = pl.BlockSpec(memory_space=pl.ANY)


def chip_exchange(arrs, bcast, *, name):
    n = len(arrs)
    piece = [a.shape if b else a.shape[1:] for a, b in zip(arrs, bcast)]

    def body(*refs):
        ins, outs = refs[:n], refs[n:2 * n]
        send, recv, loc = refs[2 * n:]
        x, y, c = lax.axis_index("x"), lax.axis_index("y"), lax.axis_index("c")
        me = 2 * x + y
        copies = []
        for k in range(n):
            own = pltpu.make_async_copy(ins[k] if bcast[k] else ins[k].at[me], outs[k].at[me], loc.at[k])
            own.start()
            copies.append(own)
            for j, (px, py) in enumerate(((1 - x, y), (x, 1 - y), (1 - x, 1 - y))):
                cp = pltpu.make_async_remote_copy(
                    src_ref=ins[k] if bcast[k] else ins[k].at[2 * px + py], dst_ref=outs[k].at[me],
                    send_sem=send.at[3 * k + j], recv_sem=recv.at[3 * k + j],
                    device_id=(px, py, c), device_id_type=MESH)
                cp.start()
                copies.append(cp)
        for cp in copies:
            cp.wait()

    return pl.pallas_call(
        body, in_specs=[ANY] * n, out_specs=[ANY] * n,
        out_shape=[SDS((4,) + tuple(p), a.dtype) for p, a in zip(piece, arrs)],
        scratch_shapes=[pltpu.SemaphoreType.DMA((3 * n,)), pltpu.SemaphoreType.DMA((3 * n,)),
                        pltpu.SemaphoreType.DMA((n,))],
        name=name)(*arrs)


def gather_weights(shards, *, name):
    n = len(shards)
    hd = shards[0].shape[0] // 2

    def body(*refs):
        ins, outs = refs[:n], refs[n:2 * n]
        send, recv, loc = refs[2 * n:]
        x, y, c = lax.axis_index("x"), lax.axis_index("y"), lax.axis_index("c")
        me = 2 * x + y
        chips = ((1 - x, y), (x, 1 - y), (1 - x, 1 - y))
        mine, theirs = pl.ds(c * hd, hd), pl.ds((1 - c) * hd, hd)

        def ici(k, j, src, dst):
            px, py = chips[j]
            return pltpu.make_async_remote_copy(src_ref=src, dst_ref=dst, send_sem=send.at[6 * k + j],
                                                recv_sem=recv.at[6 * k + j], device_id=(px, py, c),
                                                device_id_type=MESH)

        def d2d(k, j, src, dst):
            return pltpu.make_async_remote_copy(src_ref=src, dst_ref=dst, send_sem=send.at[6 * k + 3 + j],
                                                recv_sem=recv.at[6 * k + 3 + j], device_id=(x, y, 1 - c),
                                                device_id_type=MESH)

        local, sent = [], []
        for k in range(n):
            local.append(pltpu.make_async_copy(ins[k], outs[k].at[:, me], loc.at[k]))
            local[-1].start()
            for j in range(3):
                sent.append(ici(k, j, ins[k].at[mine], outs[k].at[mine, me]))
                sent[-1].start()
        for k in range(n):
            for j, (px, py) in enumerate(chips):
                landed = outs[k].at[mine, 2 * px + py]
                ici(k, j, landed, landed).wait_recv()
                sent.append(d2d(k, j, landed, landed))
                sent[-1].start()
        for k in range(n):
            for j, (px, py) in enumerate(chips):
                other = outs[k].at[theirs, 2 * px + py]
                d2d(k, j, other, other).wait_recv()
        for cp in sent:
            cp.wait_send()
        for cp in local:
            cp.wait()

    return pl.pallas_call(
        body, in_specs=[ANY] * n, out_specs=[ANY] * n,
        out_shape=[SDS((a.shape[0], 4) + tuple(a.shape[1:]), a.dtype) for a in shards],
        scratch_shapes=[pltpu.SemaphoreType.DMA((6 * n,)), pltpu.SemaphoreType.DMA((6 * n,)),
                        pltpu.SemaphoreType.DMA((n,))],
        name=name)(*shards)


def sibling_exchange(arrs, half, *, name):
    n = len(arrs)
    nh = sum(half)
    piece = [(a.shape[0],) + a.shape[2:] if h else a.shape for a, h in zip(arrs, half)]

    def body(*refs):
        ins, outs, own = refs[:n], refs[n:2 * n], refs[2 * n:2 * n + nh]
        send, recv, loc = refs[2 * n + nh:]
        x, y, c = lax.axis_index("x"), lax.axis_index("y"), lax.axis_index("c")
        copies, q = [], 0
        for k in range(n):
            cp = pltpu.make_async_remote_copy(
                src_ref=ins[k].at[:, 1 - c] if half[k] else ins[k], dst_ref=outs[k],
                send_sem=send.at[k], recv_sem=recv.at[k], device_id=(x, y, 1 - c), device_id_type=MESH)
            cp.start()
            copies.append(cp)
            if half[k]:
                cp = pltpu.make_async_copy(ins[k].at[:, c], own[q], loc.at[q])
                cp.start()
                copies.append(cp)
                q += 1
        for cp in copies:
            cp.wait()

    shapes = [SDS(tuple(p), a.dtype) for p, a in zip(piece, arrs)]
    res = pl.pallas_call(
        body, in_specs=[ANY] * n, out_specs=[ANY] * (n + nh),
        out_shape=shapes + [sh for sh, h in zip(shapes, half) if h],
        scratch_shapes=[pltpu.SemaphoreType.DMA((n,)), pltpu.SemaphoreType.DMA((n,)),
                        pltpu.SemaphoreType.DMA((max(nh, 1),))],
        name=name)(*arrs)
    return res[:n], res[n:]


def assemble_shards(parts, *, name):
    n = len(parts)
    depth = len(parts[0])
    flat = [p for ps in parts for p in ps]

    def body(*refs):
        ins, outs = refs[:n * depth], refs[n * depth:n * depth + n]
        send, recv, loc = refs[n * depth + n:]
        x, y, c = lax.axis_index("x"), lax.axis_index("y"), lax.axis_index("c")
        copies = []
        for k in range(n):
            h = parts[k][0].shape[0]
            for l in range(depth):
                q = k * depth + l
                dst = outs[k].at[l, pl.ds(c * h, h)]
                cp = pltpu.make_async_copy(ins[q], dst, loc.at[q])
                cp.start()
                copies.append(cp)
                cp = pltpu.make_async_remote_copy(src_ref=ins[q], dst_ref=dst, send_sem=send.at[q], recv_sem=recv.at[q],
                                                  device_id=(x, y, 1 - c), device_id_type=MESH)
                cp.start()
                copies.append(cp)
        for cp in copies:
            cp.wait()

    return pl.pallas_call(
        body, in_specs=[ANY] * (n * depth), out_specs=[ANY] * n,
        out_shape=[SDS((depth, 2 * ps[0].shape[0], ps[0].shape[1]), ps[0].dtype) for ps in parts],
        scratch_shapes=[pltpu.SemaphoreType.DMA((n * depth,)), pltpu.SemaphoreType.DMA((n * depth,)),
                        pltpu.SemaphoreType.DMA((n * depth,))],
        name=name)(*flat)


def ssm_discretize(lam_re, lam_im, log_dt, b_re, b_im, c_re, c_im):
    dt = jnp.exp(log_dt)[..., None]
    mag = jnp.exp(lam_re * dt)
    abr = mag * jnp.cos(lam_im * dt)
    abi = mag * jnp.sin(lam_im * dt)
    den = lam_re * lam_re + lam_im * lam_im
    zr = ((abr - 1.0) * lam_re + abi * lam_im) / den
    zi = (abi * lam_re - (abr - 1.0) * lam_im) / den
    bbr = zr[..., None] * b_re - zi[..., None] * b_im
    bbi = zr[..., None] * b_im + zi[..., None] * b_re
    eye = jnp.eye(8, dtype=f32)
    bb = jnp.stack([bbr, bbi], axis=1).reshape(2, 2, SSM_TILES, 8, SSM_STATE, SSM_GROUP)
    bmat = jnp.einsum('dqjgph,gk->djghqkp', bb, eye).reshape(2, SSM_TILES, TILE_CH, 2 * TILE_ST)
    cc = jnp.stack([c_re, -c_im], axis=1).reshape(2, 2, SSM_TILES, 8, SSM_GROUP, SSM_STATE)
    cmat = jnp.einsum('dqjghp,gk->djqkpgh', cc, eye).reshape(2, SSM_TILES, 2 * TILE_ST, TILE_CH)
    n = SSM_GROUPS * SSM_STATE
    return abr.reshape(2, n), abi.reshape(2, n), bmat, cmat


def scan_tables(ar, ai, rev, nv):
    pw = [(ar, ai)]
    for _ in range(nv - 1):
        pw.append(_cmul(ar, ai, *pw[-1]))
    big = [pw[nv - 1]]
    big.append(_cmul(*big[0], *big[0]))
    big.append(_cmul(*big[1], *big[1]))
    rows = jnp.arange(8)[:, None]
    ones = jnp.ones((8, 1), f32)
    parts = []
    for k, p in zip((1, 2, 4), big):
        cond = (rows <= 7 - k) if rev else (rows >= k)
        parts.append([jnp.where(cond, q[None, :], 0.0) for q in p])
    parts.append([ones * q[None, :] for q in big[0]])
    parts.append([ones * q[None, :] for q in pw[0]])
    for v in range(nv):
        parts.append([ones * q[None, :] for q in pw[nv - 1 - v if rev else v]])
    nrow = 40 + 8 * nv
    tre = jnp.concatenate([p[0] for p in parts], axis=0).reshape(nrow, SSM_TILES, TILE_ST)
    tim = jnp.concatenate([p[1] for p in parts], axis=0).reshape(nrow, SSM_TILES, TILE_ST)
    return jnp.concatenate([tre, tim], axis=-1).transpose(1, 0, 2)


def _tile_a(ga):
    t = ga.sum(axis=0).reshape(SSM_TILES, 2, TILE_ST)
    return t[:, 0].reshape(-1), t[:, 1].reshape(-1)


SMALL = ('norm1', 'q_gain', 'k_gain', 'sink', 'lam_re', 'lam_im', 'log_dt', 'b_re', 'b_im', 'c_re', 'c_im',
         'd_skip', 'norm2')
BIG = ('w_in', 'w_glu', 'w_out', 'w_ff1', 'w_ff2')
WEIGHTS = ('norm1', 'w_in', 'q_gain', 'k_gain', 'sink', 'lam_re', 'lam_im', 'log_dt', 'b_re', 'b_im', 'c_re',
           'c_im', 'd_skip', 'w_glu', 'w_out', 'norm2', 'w_ff1', 'w_ff2')


def _chunk(s):
    return min(256, s)


def layer_forward(l, x, p, wb):
    s = x.shape[0]
    tm = min(512, s)
    sv = {}
    h1, z = norm_mm(x, p['norm1'], wb['w_in'], l, relu2=False, name=f"l{l}_in", tm=tm)
    att = attn_fwd(z, p['q_gain'], p['k_gain'], p['sink'], name=f"l{l}_attn")
    (ar, ai, bmat, cmat), disc_vjp = jax.vjp(
        ssm_discretize, p['lam_re'], p['lam_im'], p['log_dt'], p['b_re'], p['b_im'], p['c_re'], p['c_im'])
    bmat16, cmat16 = bmat.astype(MX), cmat.astype(MX)
    ys, xbs, tabs = [], [], []
    for d, rev in enumerate((False, True)):
        tab = scan_tables(ar[d], ai[d], rev, _chunk(s) // 8)
        y_d, xb_d = ssm_fwd(z, tab, bmat16[d], cmat16[d], rev=rev, name=f"l{l}_ssm{d}", chunk=_chunk(s))
        ys.append(y_d)
        xbs.append(xb_d)
        tabs.append((tab, scan_tables(ar[d], -ai[d], not rev, _chunk(s) // 8)))
    ypre, gg, mix = glu_fwd(ys[0], ys[1], z, att, p['d_skip'], wb['w_glu'], l, name=f"l{l}_glu", tm=min(256, s))
    x1 = mm_res(mix, wb['w_out'], l, x, name=f"l{l}_out", tm=tm)
    h2, a2 = norm_mm(x1, p['norm2'], wb['w_ff1'], l, relu2=True, name=f"l{l}_ff1", tm=tm)
    x2 = mm_res(a2, wb['w_ff2'], l, x1, name=f"l{l}_ff2", tm=tm)
    sv.update(x=x, h1=h1, z=z, xbs=xbs, tabs=tabs, bmat16=bmat16, cmat16=cmat16, disc_vjp=disc_vjp,
              ypre=ypre, gg=gg, mix=mix, x1=x1, h2=h2, a2=a2)
    return x2, sv


def layer_backward(l, gx2, gx2h, p, wb, sv):
    s = gx2.shape[0]
    tm = min(512, s)
    ts = min(1024, s)
    g = {}
    gf = mm_nt(gx2h, wb['w_ff2'], l, name=f"l{l}_bff2", tm=tm, a2=sv['a2'])
    g['w_ff2'] = mm_tn(sv['a2'], gx2h, name=f"l{l}_wff2", tk=1024, tn=1024, ts=ts).reshape(4, D_FF // 4, D_MODEL)
    gx1, gx1h, gn2 = mm_nt_norm(gf, wb['w_ff1'], l, sv['x1'], p['norm2'], gx2, name=f"l{l}_bff1", tm=min(256, s))
    g['norm2'] = gn2.sum(axis=0)
    g['w_ff1'] = mm_tn(sv['h2'], gf, name=f"l{l}_wff1", tk=1024, tn=1024, ts=ts, chip_major=True)
    gmix = mm_nt(gx1h, wb['w_out'], l, name=f"l{l}_bout", tm=tm)
    g['w_out'] = mm_tn(sv['mix'], gx1h, name=f"l{l}_wout", tk=1024, tn=1024, ts=ts).reshape(4, D_MODEL // 4, D_MODEL)
    ggg, yg, gy = glu_bwd(gmix, sv['gg'], sv['ypre'], wb['w_glu'], l, name=f"l{l}_bglu", tm=min(256, s))
    g['w_glu'] = mm_tn(yg, ggg, name=f"l{l}_wglu", tk=512, tn=256, ts=ts, chip_major=True)
    gus, gas, gbs, gcs = [], [], [], []
    for d, rev in enumerate((False, True)):
        tab_s, tab_a = sv['tabs'][d]
        gu_d, ga_d, gb_d, gc_d = ssm_bwd(sv['z'], gy, sv['xbs'][d], tab_s, tab_a, sv['bmat16'][d], sv['cmat16'][d],
                                         rev=rev, name=f"l{l}_bssm{d}", chunk=_chunk(s))
        gus.append(gu_d)
        gas.append(_tile_a(ga_d))
        gbs.append(gb_d)
        gcs.append(gc_d)
    gar = jnp.stack([gas[0][0], gas[1][0]])
    gai = jnp.stack([gas[0][1], gas[1][1]])
    (g['lam_re'], g['lam_im'], g['log_dt'], g['b_re'], g['b_im'], g['c_re'], g['c_im']) = sv['disc_vjp'](
        (gar, gai, jnp.stack(gbs), jnp.stack(gcs)))
    gq, dkv, gqg, gsk = attn_bwd(sv['z'], gmix, p['q_gain'], p['k_gain'], p['sink'], name=f"l{l}_battn")
    g['q_gain'] = gqg.sum(axis=0)
    g['sink'] = gsk[:, 0]
    gz, gkg, gd = gz_assemble(gq, dkv, sv['z'], p['k_gain'], gus[0], gus[1], gy, p['d_skip'], name=f"l{l}_gz")
    g['k_gain'] = gkg.sum(axis=0)
    g['d_skip'] = gd.sum(axis=0)
    gx, gxh, gn1 = mm_nt_norm(gz, wb['w_in'], l, sv['x'], p['norm1'], gx1, name=f"l{l}_bin", tm=tm)
    g['norm1'] = gn1.sum(axis=0)
    gw_in = mm_tn(sv['h1'], gz, name=f"l{l}_win", tk=1024, tn=640, ts=ts)
    g['w_in'] = gw_in.reshape(D_MODEL, 4, IN_WIDTH // 4).transpose(1, 0, 2)
    return gx, gxh, g


def stack_layouts(gathered):
    w_in = gathered['w_in']
    depth = w_in.shape[0]
    return dict(w_in=w_in.transpose(0, 2, 1, 3).reshape(depth, D_MODEL, IN_WIDTH),
                w_glu=gathered['w_glu'], w_ff1=gathered['w_ff1'],
                w_out=gathered['w_out'].reshape(depth, D_MODEL, D_MODEL),
                w_ff2=gathered['w_ff2'].reshape(depth, D_FF, D_MODEL))


def local_step(x, target, small, wb, after_layer=None):
    depth = wb['w_in'].shape[0]
    saves = []
    for l in range(depth):
        p = {k: small[k][l] for k in SMALL}
        x, sv = layer_forward(l, x, p, wb)
        saves.append(sv)
    gx, gxh, lparts = loss_grad(x, target, name="loss", tm=min(512, x.shape[0]))
    grads = [None] * depth
    for l in reversed(range(depth)):
        p = {k: small[k][l] for k in SMALL}
        gx, gxh, g = layer_backward(l, gx, gxh, p, wb, saves[l])
        grads[l] = g if after_layer is None else after_layer(l, g)
    return lparts, gx, grads


def reduce_layer(l, g):
    arrs = [g[k].reshape(4, 2, g[k].shape[1] // 2, g[k].shape[2]) for k in BIG]
    got, own = sibling_exchange(arrs, [True] * len(BIG), name=f"l{l}_rsib")
    sums = [_elementwise(lambda a, b: (a + b,), [a, b], 1, name=f"l{l}_radd_{k}", out_dtype=WIRE)[0]
            for k, a, b in zip(BIG, own, got)]
    got = chip_exchange(sums, [False] * len(BIG), name=f"l{l}_rchips")
    return {k: sum4(a, name=f"l{l}_rsum_{k}") for k, a in zip(BIG, got)}


def reduce_small(packed):
    got, _ = sibling_exchange([packed], [False], name="small_rsib")
    pair = _elementwise(lambda a, b: (a + b,), [packed, got[0]], 1, name="small_radd")[0]
    got = chip_exchange([pair], [True], name="small_rchips")
    return sum4(got[0], name="small_rsum")


def _pack_small(tree):
    flat = jnp.concatenate([tree[k].reshape(-1) for k in SMALL])
    pad = (-flat.shape[0]) % 1024
    return jnp.pad(flat, (0, pad)).reshape(-1, 128)


def _unpack_small(packed, like):
    flat = packed.reshape(-1)
    out, off = {}, 0
    for k in SMALL:
        n = like[k].size
        out[k] = flat[off:off + n].reshape(like[k].shape)
        off += n
    return out


def kernel(x, norm1, w_in, q_gain, k_gain, sink, lam_re, lam_im, log_dt, b_re, b_im, c_re, c_im, d_skip, w_glu, w_out, norm2, w_ff1, w_ff2, loss_target, m_norm1, m_w_in, m_q_gain, m_k_gain, m_sink, m_lam_re, m_lam_im, m_log_dt, m_b_re, m_b_im, m_c_re, m_c_im, m_d_skip, m_w_glu, m_w_out, m_norm2, m_w_ff1, m_w_ff2, v_norm1, v_w_in, v_q_gain, v_k_gain, v_sink, v_lam_re, v_lam_im, v_log_dt, v_b_re, v_b_im, v_c_re, v_c_im, v_d_skip, v_w_glu, v_w_out, v_norm2, v_w_ff1, v_w_ff2):
    w = dict(norm1=norm1, w_in=w_in, q_gain=q_gain, k_gain=k_gain, sink=sink, lam_re=lam_re, lam_im=lam_im,
             log_dt=log_dt, b_re=b_re, b_im=b_im, c_re=c_re, c_im=c_im, d_skip=d_skip, w_glu=w_glu, w_out=w_out,
             norm2=norm2, w_ff1=w_ff1, w_ff2=w_ff2)
    m = dict(norm1=m_norm1, w_in=m_w_in, q_gain=m_q_gain, k_gain=m_k_gain, sink=m_sink, lam_re=m_lam_re,
             lam_im=m_lam_im, log_dt=m_log_dt, b_re=m_b_re, b_im=m_b_im, c_re=m_c_re, c_im=m_c_im,
             d_skip=m_d_skip, w_glu=m_w_glu, w_out=m_w_out, norm2=m_norm2, w_ff1=m_w_ff1, w_ff2=m_w_ff2)
    v = dict(norm1=v_norm1, w_in=v_w_in, q_gain=v_q_gain, k_gain=v_k_gain, sink=v_sink, lam_re=v_lam_re,
             lam_im=v_lam_im, log_dt=v_log_dt, b_re=v_b_re, b_im=v_b_im, c_re=v_c_re, c_im=v_c_im,
             d_skip=v_d_skip, w_glu=v_w_glu, w_out=v_w_out, norm2=v_norm2, w_ff1=v_w_ff1, w_ff2=v_w_ff2)
    depth = w_in.shape[0]

    gathered = gather_weights([w[k].astype(WIRE) for k in BIG], name="gather_w")
    wb = stack_layouts(dict(zip(BIG, gathered)))
    small = {k: w[k] for k in SMALL}

    def after_layer(l, g):
        return reduce_layer(l, g), {k: g[k] for k in SMALL}

    lparts, gx, grads = local_step(x[0], loss_target[0], small, wb, after_layer)
    loss = lax.psum(0.5 * jnp.sum(lparts) / D_MODEL, ("x", "y", "c"))

    full = assemble_shards([[grads[l][0][k] for l in range(depth)] for k in BIG], name="assemble")
    gfull = dict(zip(BIG, full))
    gsmall = reduce_small(_pack_small({k: jnp.stack([grads[l][1][k] for l in range(depth)]) for k in SMALL}))
    like = {k: w[k] for k in SMALL}
    gfull.update(_unpack_small(gsmall, like))

    delta, new_m, new_v = {}, {}, {}
    for k in BIG:
        delta[k], new_m[k], new_v[k] = adamw(w[k], gfull[k], m[k], v[k], name=f"adamw_{k}")
    ds, ms, vs = adamw(_pack_small(like), gsmall, _pack_small({k: m[k] for k in SMALL}),
                       _pack_small({k: v[k] for k in SMALL}), name="adamw_small")
    delta.update(_unpack_small(ds, like))
    new_m.update(_unpack_small(ms, like))
    new_v.update(_unpack_small(vs, like))

    return (loss, gx[None], *[gfull[k] for k in WEIGHTS], *[delta[k] for k in WEIGHTS],
            *[new_m[k] for k in WEIGHTS], *[new_v[k] for k in WEIGHTS])
```

```python
import functools
import math

import jax
import jax.numpy as jnp
from jax import lax
from jax.experimental import pallas as pl
from jax.experimental.pallas import tpu as pltpu

f32 = jnp.float32
MX = jnp.bfloat16
WIRE = jnp.bfloat16
SDS = jax.ShapeDtypeStruct

D_MODEL = 1024
DEPTH = 4
ATT_HEADS = 8
KV_HEADS = 2
GQA = ATT_HEADS // KV_HEADS
HEAD_DIM = 64
ATT_WIDTH = ATT_HEADS * HEAD_DIM
KV_WIDTH = KV_HEADS * HEAD_DIM
BLOCK = 128
SSM_WIDTH = 512
SSM_GROUP = 16
SSM_GROUPS = 32
SSM_STATE = 64
SSM_TILES = 4
TILE_CH = SSM_WIDTH // SSM_TILES
TILE_ST = SSM_GROUPS * SSM_STATE // SSM_TILES
SLAB = 256
IN_WIDTH = ATT_WIDTH + 2 * KV_WIDTH + SSM_WIDTH
U_OFF = ATT_WIDTH + 2 * KV_WIDTH
D_FF = 4096
EPS = 1e-6
NEG = float(jnp.finfo(jnp.float32).min)
SLOPES = tuple(2.0 ** (-8.0 * (h + 1) / ATT_HEADS) for h in range(ATT_HEADS))

ADAM_LR, ADAM_B1, ADAM_B2, ADAM_EPS, ADAM_WD, ADAM_STEP = 0.001, 0.9, 0.999, 1e-08, 0.01, 10

VMEM_LIMIT = 48 * 1024 * 1024
MESH = pl.DeviceIdType.MESH

NT = (((1,), (1,)), ((), ()))
TN = (((0,), (0,)), ((), ()))


def _cp(*sem):
    return pltpu.CompilerParams(dimension_semantics=sem, vmem_limit_bytes=VMEM_LIMIT)


def _dot(a, b, dims=None):
    if dims is None:
        return jnp.dot(a, b, preferred_element_type=f32)
    return lax.dot_general(a, b, dims, preferred_element_type=f32)


def _rows8(v):
    return v.reshape(v.shape[0] // 8, 8, v.shape[1]).sum(axis=0)


def _layer_spec(w, l):
    nd = w.ndim
    return pl.BlockSpec((1,) + tuple(w.shape[1:]), lambda i: (l,) + (0,) * (nd - 1))


def _row_spec(tm, width):
    return pl.BlockSpec((tm, width), lambda i: (i, 0))


def norm_mm(x, gain, w, l, *, relu2, name, tm):
    s, d = x.shape
    if relu2:
        nblk, cb = w.shape[1], w.shape[3]
        n = nblk * cb
    else:
        n = w.shape[2]

    def body(x_ref, g_ref, w_ref, h_ref, y_ref):
        xf = x_ref[...]
        r = lax.rsqrt(jnp.mean(xf * xf, axis=-1, keepdims=True) + EPS)
        h = (xf * r * g_ref[...]).astype(MX)
        h_ref[...] = h
        if relu2:
            for b in range(nblk):
                f = jnp.maximum(_dot(h, w_ref[0, b]), 0.0)
                y_ref[:, cb * b:cb * (b + 1)] = (f * f).astype(MX)
        else:
            y_ref[...] = _dot(h, w_ref[0])

    return pl.pallas_call(
        body, grid=(s // tm,),
        in_specs=[_row_spec(tm, d), pl.BlockSpec((1, d), lambda i: (0, 0)), _layer_spec(w, l)],
        out_specs=[_row_spec(tm, d), _row_spec(tm, n)],
        out_shape=[SDS((s, d), MX), SDS((s, n), MX if relu2 else f32)],
        compiler_params=_cp("parallel"), name=name)(x, gain.reshape(1, d), w)


def mm_res(a, w, l, res, *, name, tm):
    s, k = a.shape
    n = w.shape[2]

    def body(a_ref, w_ref, r_ref, o_ref):
        o_ref[...] = r_ref[...] + _dot(a_ref[...], w_ref[0])

    return pl.pallas_call(
        body, grid=(s // tm,), in_specs=[_row_spec(tm, k), _layer_spec(w, l), _row_spec(tm, n)],
        out_specs=_row_spec(tm, n), out_shape=SDS((s, n), f32), compiler_params=_cp("parallel"), name=name)(a, w, res)


def mm_nt(gy, w, l, *, name, tm, a2=None):
    s, n = gy.shape
    k = w.shape[1]
    kb = min(k, 1024)

    def body(*refs):
        if a2 is None:
            g_ref, w_ref, o_ref = refs
        else:
            g_ref, w_ref, a_ref, o_ref = refs
        g = g_ref[...]
        for b in range(k // kb):
            cols = slice(kb * b, kb * (b + 1))
            acc = _dot(g, w_ref[0, cols, :], NT)
            if a2 is not None:
                acc = acc * (2.0 * jnp.sqrt(a_ref[:, cols].astype(f32)))
            o_ref[:, cols] = acc.astype(o_ref.dtype)

    in_specs = [_row_spec(tm, n), _layer_spec(w, l)]
    args = [gy, w]
    if a2 is not None:
        in_specs.append(_row_spec(tm, k))
        args.append(a2)
    return pl.pallas_call(
        body, grid=(s // tm,), in_specs=in_specs, out_specs=_row_spec(tm, k),
        out_shape=SDS((s, k), f32 if a2 is None else MX), compiler_params=_cp("parallel"), name=name)(*args)


def mm_nt_norm(gy, w, l, x, gain, res, *, name, tm):
    s, n = gy.shape
    d = x.shape[1]

    def body(g_ref, w_ref, x_ref, gn_ref, r_ref, o_ref, o16_ref, gg_ref):
        @pl.when(pl.program_id(0) == 0)
        def _():
            gg_ref[...] = jnp.zeros_like(gg_ref)

        if w.ndim == 3:
            gh = _dot(g_ref[...], w_ref[0], NT)
        else:
            cb = w.shape[3]
            gh = _dot(g_ref[:, 0:cb], w_ref[0, 0], NT)
            for b in range(1, w.shape[1]):
                gh = gh + _dot(g_ref[:, cb * b:cb * (b + 1)], w_ref[0, b], NT)
        xf = x_ref[...]
        r = lax.rsqrt(jnp.mean(xf * xf, axis=-1, keepdims=True) + EPS)
        xh = xf * r
        t = gh * gn_ref[...]
        gx = r_ref[...] + r * (t - xh * jnp.mean(t * xh, axis=-1, keepdims=True))
        o_ref[...] = gx
        o16_ref[...] = gx.astype(MX)
        gg_ref[...] += _rows8(gh * xh)

    return pl.pallas_call(
        body, grid=(s // tm,),
        in_specs=[_row_spec(tm, n), _layer_spec(w, l), _row_spec(tm, d), pl.BlockSpec((1, d), lambda i: (0, 0)),
                  _row_spec(tm, d)],
        out_specs=[_row_spec(tm, d), _row_spec(tm, d), pl.BlockSpec((8, d), lambda i: (0, 0))],
        out_shape=[SDS((s, d), f32), SDS((s, d), MX), SDS((8, d), f32)],
        compiler_params=_cp("arbitrary"), name=name)(gy, w, x, gain.reshape(1, d), res)


def mm_tn(xa, gy, *, name, tk, tn, ts, chip_major=False):
    s, k = xa.shape
    n = gy.shape[1]

    def body(x_ref, g_ref, o_ref):
        @pl.when(pl.program_id(2) == 0)
        def _():
            o_ref[...] = jnp.zeros_like(o_ref)

        acc = _dot(x_ref[...], g_ref[...], TN)
        if chip_major:
            o_ref[0] += acc
        else:
            o_ref[...] += acc

    if chip_major:
        out_spec = pl.BlockSpec((1, tk, tn), lambda a, b, c: (b, a, 0))
        out_shape = SDS((n // tn, k, tn), f32)
    else:
        out_spec = pl.BlockSpec((tk, tn), lambda a, b, c: (a, b))
        out_shape = SDS((k, n), f32)
    return pl.pallas_call(
        body, grid=(k // tk, n // tn, s // ts),
        in_specs=[pl.BlockSpec((ts, tk), lambda a, b, c: (c, a)), pl.BlockSpec((ts, tn), lambda a, b, c: (c, b))],
        out_specs=out_spec, out_shape=out_shape,
        compiler_params=_cp("parallel", "parallel", "arbitrary"), name=name)(xa, gy)


def _head_norm(t):
    r = lax.rsqrt(jnp.mean(t * t, axis=-1, keepdims=True) + EPS)
    return t * r, r


def _attn_mask(i, nb):
    row = lax.broadcasted_iota(jnp.int32, (BLOCK, 3 * BLOCK), 0)
    col = lax.broadcasted_iota(jnp.int32, (BLOCK, 3 * BLOCK), 1)
    dist = jnp.abs(row - col + BLOCK)
    valid = (dist <= BLOCK) & ((col >= BLOCK) | (i >= 1)) & ((col < 2 * BLOCK) | (i <= nb - 2))
    return dist.astype(f32), valid


def _attn_specs(nb):
    return [pl.BlockSpec((BLOCK, ATT_WIDTH), lambda i: (i, 0)),
            pl.BlockSpec((BLOCK, 2 * KV_WIDTH), lambda i: (jnp.maximum(i - 1, 0), 2)),
            pl.BlockSpec((BLOCK, 2 * KV_WIDTH), lambda i: (i, 2)),
            pl.BlockSpec((BLOCK, 2 * KV_WIDTH), lambda i: (jnp.minimum(i + 1, nb - 1), 2)),
            pl.BlockSpec((1, HEAD_DIM), lambda i: (0, 0)),
            pl.BlockSpec((1, HEAD_DIM), lambda i: (0, 0)),
            pl.BlockSpec(memory_space=pltpu.SMEM)]


def _attn_probs(s_g, head, distf, valid, sink_ref):
    sg = jnp.where(valid, s_g - SLOPES[head] * distf, NEG)
    sk = sink_ref[head]
    m = jnp.maximum(jnp.max(sg, axis=-1, keepdims=True), sk)
    e = jnp.exp(sg - m)
    es = jnp.exp(sk - m)
    den = jnp.sum(e, axis=-1, keepdims=True) + es
    return e / den, es / den


def attn_fwd(z, q_gain, k_gain, sink, *, name):
    s = z.shape[0]
    nb = s // BLOCK

    def body(q_ref, kp_ref, kc_ref, kn_ref, qg_ref, kg_ref, sink_ref, o_ref):
        i = pl.program_id(0)
        distf, valid = _attn_mask(i, nb)
        kv = jnp.concatenate([kp_ref[...], kc_ref[...], kn_ref[...]], axis=0)
        for kvh in range(KV_HEADS):
            kn, _ = _head_norm(kv[:, HEAD_DIM * kvh:HEAD_DIM * (kvh + 1)])
            kn = (kn * kg_ref[...]).astype(MX)
            vh = kv[:, KV_WIDTH + HEAD_DIM * kvh:KV_WIDTH + HEAD_DIM * (kvh + 1)].astype(MX)
            qs = []
            for g in range(GQA):
                h = GQA * kvh + g
                qn, _ = _head_norm(q_ref[:, HEAD_DIM * h:HEAD_DIM * (h + 1)])
                qs.append((qn * qg_ref[...]).astype(MX))
            sc = _dot(jnp.concatenate(qs, axis=0), kn, NT) * 0.125
            for g in range(GQA):
                h = GQA * kvh + g
                p, _ = _attn_probs(sc[BLOCK * g:BLOCK * (g + 1)], h, distf, valid, sink_ref)
                o_ref[:, HEAD_DIM * h:HEAD_DIM * (h + 1)] = _dot(p.astype(MX), vh).astype(o_ref.dtype)

    return pl.pallas_call(
        body, grid=(nb,), in_specs=_attn_specs(nb),
        out_specs=pl.BlockSpec((BLOCK, ATT_WIDTH), lambda i: (i, 0)),
        out_shape=SDS((s, ATT_WIDTH), MX), compiler_params=_cp("parallel"), name=name)(
            z, z, z, z, q_gain.reshape(1, HEAD_DIM), k_gain.reshape(1, HEAD_DIM), sink)


def attn_bwd(z, gmix, q_gain, k_gain, sink, *, name):
    s = z.shape[0]
    nb = s // BLOCK

    def body(q_ref, kp_ref, kc_ref, kn_ref, qg_ref, kg_ref, sink_ref, go_ref, gq_ref, dkv_ref, gqg_ref, gs_ref):
        i = pl.program_id(0)

        @pl.when(i == 0)
        def _():
            gqg_ref[...] = jnp.zeros_like(gqg_ref)
            gs_ref[...] = jnp.zeros_like(gs_ref)

        distf, valid = _attn_mask(i, nb)
        kv = jnp.concatenate([kp_ref[...], kc_ref[...], kn_ref[...]], axis=0)
        for kvh in range(KV_HEADS):
            kn, _ = _head_norm(kv[:, HEAD_DIM * kvh:HEAD_DIM * (kvh + 1)])
            kn = (kn * kg_ref[...]).astype(MX)
            vh = kv[:, KV_WIDTH + HEAD_DIM * kvh:KV_WIDTH + HEAD_DIM * (kvh + 1)].astype(MX)
            qhat, qr, qs, dos = [], [], [], []
            for g in range(GQA):
                h = GQA * kvh + g
                qn, r = _head_norm(q_ref[:, HEAD_DIM * h:HEAD_DIM * (h + 1)])
                qhat.append(qn)
                qr.append(r)
                qs.append((qn * qg_ref[...]).astype(MX))
                dos.append(go_ref[:, HEAD_DIM * h:HEAD_DIM * (h + 1)].astype(MX))
            qs = jnp.concatenate(qs, axis=0)
            dos = jnp.concatenate(dos, axis=0)
            sc = _dot(qs, kn, NT) * 0.125
            dp = _dot(dos, vh, NT)
            ps, dss = [], []
            for g in range(GQA):
                h = GQA * kvh + g
                p, psink = _attn_probs(sc[BLOCK * g:BLOCK * (g + 1)], h, distf, valid, sink_ref)
                dpg = dp[BLOCK * g:BLOCK * (g + 1)]
                delta = jnp.sum(p * dpg, axis=-1, keepdims=True)
                gs_ref[h:h + 1, :] += jnp.broadcast_to(jnp.sum(-psink * delta, axis=0, keepdims=True), (1, 128))
                ps.append(p.astype(MX))
                dss.append((p * (dpg - delta) * 0.125).astype(MX))
            ps = jnp.concatenate(ps, axis=0)
            dss = jnp.concatenate(dss, axis=0)
            gv = _dot(ps, dos, TN)
            gkn = _dot(dss, qs, TN)
            gqn = _dot(dss, kn)
            for g in range(GQA):
                h = GQA * kvh + g
                gq_h = gqn[BLOCK * g:BLOCK * (g + 1)]
                gqg_ref[h:h + 1, :] += jnp.sum(gq_h * qhat[g], axis=0, keepdims=True)
                t = gq_h * qg_ref[...]
                gq_ref[:, HEAD_DIM * h:HEAD_DIM * (h + 1)] = qr[g] * (
                    t - qhat[g] * jnp.mean(t * qhat[g], axis=-1, keepdims=True))
            for b in range(3):
                dkv_ref[b, :, HEAD_DIM * kvh:HEAD_DIM * (kvh + 1)] = gkn[BLOCK * b:BLOCK * (b + 1)]
                dkv_ref[b, :, KV_WIDTH + HEAD_DIM * kvh:KV_WIDTH + HEAD_DIM * (kvh + 1)] = gv[BLOCK * b:BLOCK * (b + 1)]

    return pl.pallas_call(
        body, grid=(nb,),
        in_specs=_attn_specs(nb) + [pl.BlockSpec((BLOCK, ATT_WIDTH), lambda i: (i, 0))],
        out_specs=[pl.BlockSpec((BLOCK, ATT_WIDTH), lambda i: (i, 0)),
                   pl.BlockSpec((3, BLOCK, 2 * KV_WIDTH), lambda i: (0, i, 0)),
                   pl.BlockSpec((ATT_HEADS, HEAD_DIM), lambda i: (0, 0)),
                   pl.BlockSpec((ATT_HEADS, 128), lambda i: (0, 0))],
        out_shape=[SDS((s, ATT_WIDTH), f32), SDS((3, s, 2 * KV_WIDTH), f32),
                   SDS((ATT_HEADS, HEAD_DIM), f32), SDS((ATT_HEADS, 128), f32)],
        compiler_params=_cp("arbitrary"), name=name)(
            z, z, z, z, q_gain.reshape(1, HEAD_DIM), k_gain.reshape(1, HEAD_DIM), sink, gmix)


def gz_assemble(gq, dkv, z, k_gain, gu_f, gu_r, gy, d_skip, *, name):
    s = z.shape[0]
    nb = s // BLOCK

    def body(gq_ref, d0_ref, d1_ref, d2_ref, z_ref, kg_ref, guf_ref, gur_ref, gy_ref, ds_ref, gz_ref, gkg_ref, gd_ref):
        i = pl.program_id(0)

        @pl.when(i == 0)
        def _():
            gkg_ref[...] = jnp.zeros_like(gkg_ref)
            gd_ref[...] = jnp.zeros_like(gd_ref)

        gkv = d1_ref[0] + jnp.where(i + 1 < nb, d0_ref[0], 0.0) + jnp.where(i >= 1, d2_ref[0], 0.0)
        gz_ref[:, 0:ATT_WIDTH] = gq_ref[...].astype(MX)
        for kvh in range(KV_HEADS):
            kh, r = _head_norm(z_ref[:, ATT_WIDTH + HEAD_DIM * kvh:ATT_WIDTH + HEAD_DIM * (kvh + 1)])
            gkn = gkv[:, HEAD_DIM * kvh:HEAD_DIM * (kvh + 1)]
            gkg_ref[kvh:kvh + 1, :] += jnp.sum(gkn * kh, axis=0, keepdims=True)
            t = gkn * kg_ref[...]
            gk = r * (t - kh * jnp.mean(t * kh, axis=-1, keepdims=True))
            gz_ref[:, ATT_WIDTH + HEAD_DIM * kvh:ATT_WIDTH + HEAD_DIM * (kvh + 1)] = gk.astype(MX)
        gz_ref[:, ATT_WIDTH + KV_WIDTH:U_OFF] = gkv[:, KV_WIDTH:].astype(MX)
        gyv = gy_ref[...]
        gz_ref[:, U_OFF:IN_WIDTH] = (guf_ref[...] + gur_ref[...] + ds_ref[...] * gyv).astype(MX)
        gd_ref[...] += _rows8(gyv * z_ref[:, U_OFF:IN_WIDTH])

    row = lambda w: pl.BlockSpec((BLOCK, w), lambda i: (i, 0))
    return pl.pallas_call(
        body, grid=(nb,),
        in_specs=[row(ATT_WIDTH),
                  pl.BlockSpec((1, BLOCK, 2 * KV_WIDTH), lambda i: (0, jnp.minimum(i + 1, nb - 1), 0)),
                  pl.BlockSpec((1, BLOCK, 2 * KV_WIDTH), lambda i: (1, i, 0)),
                  pl.BlockSpec((1, BLOCK, 2 * KV_WIDTH), lambda i: (2, jnp.maximum(i - 1, 0), 0)),
                  row(IN_WIDTH), pl.BlockSpec((1, HEAD_DIM), lambda i: (0, 0)),
                  row(SSM_WIDTH), row(SSM_WIDTH), row(SSM_WIDTH), pl.BlockSpec((1, SSM_WIDTH), lambda i: (0, 0))],
        out_specs=[row(IN_WIDTH), pl.BlockSpec((8, HEAD_DIM), lambda i: (0, 0)),
                   pl.BlockSpec((8, SSM_WIDTH), lambda i: (0, 0))],
        out_shape=[SDS((s, IN_WIDTH), MX), SDS((8, HEAD_DIM), f32), SDS((8, SSM_WIDTH), f32)],
        compiler_params=_cp("arbitrary"), name=name)(
            gq, dkv, dkv, dkv, z, k_gain.reshape(1, HEAD_DIM), gu_f, gu_r, gy, d_skip.reshape(1, SSM_WIDTH))


def _cmul(ar, ai, xr, xi):
    return ar * xr - ai * xi, ar * xi + ai * xr


def _permute_rows(src_ref, dst_ref, nv):
    for v in range(nv):
        dst_ref[8 * v:8 * v + 8, :] = src_ref[pl.ds(v, 8, stride=nv), :]


def _unpermute_rows(val, dst_ref, nv):
    for v in range(nv):
        dst_ref[pl.ds(v, 8, stride=nv), :] = val[8 * v:8 * v + 8, :]


def _scan_chunk(x_ref, tab_ref, carry_ref, nv, rev, acc=None):
    L = TILE_ST
    order = list(range(nv - 1, -1, -1)) if rev else list(range(nv))
    a_r, a_i = tab_ref[32:40, :L], tab_ref[32:40, L:]
    pr = pi = None
    for v in order:
        rows = slice(8 * v, 8 * v + 8)
        xr, xi = x_ref[rows, :L], x_ref[rows, L:]
        if pr is not None:
            mr, mi = _cmul(a_r, a_i, pr, pi)
            xr, xi = xr + mr, xi + mi
            x_ref[rows, :L] = xr
            x_ref[rows, L:] = xi
        pr, pi = xr, xi
    er, ei = pr, pi
    row = lax.broadcasted_iota(jnp.int32, (8, L), 0)
    edge = row == (7 if rev else 0)
    sh = 7 if rev else 1
    fr = jnp.where(edge, carry_ref[:, :L], pltpu.roll(er, sh, 0))
    fi = jnp.where(edge, carry_ref[:, L:], pltpu.roll(ei, sh, 0))
    for n, k in enumerate((1, 2, 4)):
        mr, mi = tab_ref[8 * n:8 * n + 8, :L], tab_ref[8 * n:8 * n + 8, L:]
        sh = (8 - k) if rev else k
        rr, ri = pltpu.roll(fr, sh, 0), pltpu.roll(fi, sh, 0)
        fr, fi = fr + mr * rr - mi * ri, fi + mr * ri + mi * rr
    dr, di = _cmul(tab_ref[24:32, :L], tab_ref[24:32, L:], fr, fi)
    last = 0 if rev else 7
    carry_ref[:, :L] = jnp.broadcast_to((dr + er)[last:last + 1, :], (8, L))
    carry_ref[:, L:] = jnp.broadcast_to((di + ei)[last:last + 1, :], (8, L))
    qr, qi = fr, fi
    if acc is not None:
        sr, si = jnp.zeros((8, L), f32), jnp.zeros((8, L), f32)
    for v in order:
        rows = slice(8 * v, 8 * v + 8)
        trow = slice(40 + 8 * v, 48 + 8 * v)
        mr, mi = _cmul(tab_ref[trow, :L], tab_ref[trow, L:], fr, fi)
        xr, xi = x_ref[rows, :L] + mr, x_ref[rows, L:] + mi
        x_ref[rows, :L] = xr
        x_ref[rows, L:] = xi
        if acc is not None:
            gr, gi = acc[0][rows, :L], acc[0][rows, L:]
            sr, si = sr + gr * qr + gi * qi, si + gi * qr - gr * qi
            qr, qi = xr, xi
    if acc is not None:
        acc[1][:, :L] += sr
        acc[1][:, L:] += si


def ssm_fwd(z, tab, bmat, cmat, *, rev, name, chunk):
    s = z.shape[0]
    nc = s // chunk
    nv = chunk // 8
    ci = (lambda i: nc - 1 - i) if rev else (lambda i: i)

    def body(u_ref, tab_ref, b_ref, c_ref, y_ref, xb_ref, u_scr, x_scr, carry):
        @pl.when(pl.program_id(1) == 0)
        def _():
            carry[...] = jnp.zeros_like(carry)

        xb_ref[0] = carry[...]
        _permute_rows(u_ref, u_scr, nv)
        x_scr[...] = _dot(u_scr[...].astype(MX), b_ref[0])
        _scan_chunk(x_scr, tab_ref.at[0], carry, nv, rev)
        _unpermute_rows(_dot(x_scr[...].astype(MX), c_ref[0]), y_ref, nv)

    return pl.pallas_call(
        body, grid=(SSM_TILES, nc),
        in_specs=[pl.BlockSpec((chunk, TILE_CH), lambda j, i: (ci(i), U_OFF // TILE_CH + j)),
                  pl.BlockSpec((1, 40 + chunk, 2 * TILE_ST), lambda j, i: (j, 0, 0)),
                  pl.BlockSpec((1, TILE_CH, 2 * TILE_ST), lambda j, i: (j, 0, 0)),
                  pl.BlockSpec((1, 2 * TILE_ST, TILE_CH), lambda j, i: (j, 0, 0))],
        out_specs=[pl.BlockSpec((chunk, TILE_CH), lambda j, i: (ci(i), j)),
                   pl.BlockSpec((1, 8, 2 * TILE_ST), lambda j, i: (ci(i), 0, j))],
        out_shape=[SDS((s, SSM_WIDTH), f32), SDS((nc, 8, SSM_TILES * 2 * TILE_ST), f32)],
        scratch_shapes=[pltpu.VMEM((chunk, TILE_CH), f32), pltpu.VMEM((chunk, 2 * TILE_ST), f32),
                        pltpu.VMEM((8, 2 * TILE_ST), f32)],
        compiler_params=_cp("parallel", "arbitrary"), name=name)(z, tab, bmat, cmat)


def ssm_bwd(z, gy, xb, tab_s, tab_a, bmat, cmat, *, rev, name, chunk):
    s = z.shape[0]
    nc = s // chunk
    nv = chunk // 8
    ci = (lambda i: i) if rev else (lambda i: nc - 1 - i)

    def body(u_ref, gy_ref, xb_ref, ts_ref, ta_ref, b_ref, c_ref, gu_ref, ga_ref, gb_ref, gc_ref,
             u_scr, gy_scr, x_scr, g_scr, gcarry, xcarry):
        @pl.when(pl.program_id(1) == 0)
        def _():
            gcarry[...] = jnp.zeros_like(gcarry)
            ga_ref[...] = jnp.zeros_like(ga_ref)
            gb_ref[...] = jnp.zeros_like(gb_ref)
            gc_ref[...] = jnp.zeros_like(gc_ref)

        _permute_rows(u_ref, u_scr, nv)
        _permute_rows(gy_ref, gy_scr, nv)
        ub = u_scr[...].astype(MX)
        gyb = gy_scr[...].astype(MX)
        g_scr[...] = _dot(gyb, c_ref[0], NT)
        _scan_chunk(g_scr, ta_ref.at[0], gcarry, nv, not rev)
        x_scr[...] = _dot(ub, b_ref[0])
        xcarry[...] = xb_ref[0]
        _scan_chunk(x_scr, ts_ref.at[0], xcarry, nv, rev, acc=(g_scr, ga_ref))
        gb16 = g_scr[...].astype(MX)
        gb_ref[0] += _dot(ub, gb16, TN)
        gc_ref[0] += _dot(x_scr[...].astype(MX), gyb, TN)
        _unpermute_rows(_dot(gb16, b_ref[0], NT), gu_ref, nv)

    tile3 = lambda a, b: pl.BlockSpec((1, a, b), lambda j, i: (j, 0, 0))
    return pl.pallas_call(
        body, grid=(SSM_TILES, nc),
        in_specs=[pl.BlockSpec((chunk, TILE_CH), lambda j, i: (ci(i), U_OFF // TILE_CH + j)),
                  pl.BlockSpec((chunk, TILE_CH), lambda j, i: (ci(i), j)),
                  pl.BlockSpec((1, 8, 2 * TILE_ST), lambda j, i: (ci(i), 0, j)),
                  tile3(40 + chunk, 2 * TILE_ST), tile3(40 + chunk, 2 * TILE_ST),
                  tile3(TILE_CH, 2 * TILE_ST), tile3(2 * TILE_ST, TILE_CH)],
        out_specs=[pl.BlockSpec((chunk, TILE_CH), lambda j, i: (ci(i), j)),
                   pl.BlockSpec((8, 2 * TILE_ST), lambda j, i: (0, j)),
                   tile3(TILE_CH, 2 * TILE_ST), tile3(2 * TILE_ST, TILE_CH)],
        out_shape=[SDS((s, SSM_WIDTH), f32), SDS((8, SSM_TILES * 2 * TILE_ST), f32),
                   SDS((SSM_TILES, TILE_CH, 2 * TILE_ST), f32), SDS((SSM_TILES, 2 * TILE_ST, TILE_CH), f32)],
        scratch_shapes=[pltpu.VMEM((chunk, TILE_CH), f32), pltpu.VMEM((chunk, TILE_CH), f32),
                        pltpu.VMEM((chunk, 2 * TILE_ST), f32), pltpu.VMEM((chunk, 2 * TILE_ST), f32),
                        pltpu.VMEM((8, 2 * TILE_ST), f32), pltpu.VMEM((8, 2 * TILE_ST), f32)],
        compiler_params=_cp("parallel", "arbitrary"), name=name)(z, gy, xb, tab_s, tab_a, bmat, cmat)


GELU_K = math.sqrt(2.0 / math.pi)


def _gelu(y):
    return 0.5 * y * (1.0 + jnp.tanh(GELU_K * (y + 0.044715 * (y * y * y))))


def _gelu_grad(y):
    t = jnp.tanh(GELU_K * (y + 0.044715 * (y * y * y)))
    return 0.5 * (1.0 + t) + 0.5 * y * (1.0 - t * t) * (GELU_K * (1.0 + 3.0 * 0.044715 * (y * y)))


def glu_fwd(y_f, y_r, z, att, d_skip, w_glu, l, *, name, tm):
    s = z.shape[0]
    nblk, cb = w_glu.shape[1], w_glu.shape[3]

    def body(yf_ref, yr_ref, z_ref, att_ref, d_ref, w_ref, y_ref, gg_ref, mix_ref):
        y = d_ref[...] * z_ref[:, U_OFF:IN_WIDTH] + yf_ref[...] + yr_ref[...]
        y_ref[...] = y
        yg = _gelu(y).astype(MX)
        for b in range(nblk):
            gg_ref[:, cb * b:cb * (b + 1)] = _dot(yg, w_ref[0, b])
        mix_ref[:, 0:ATT_WIDTH] = att_ref[...]
        mix_ref[:, ATT_WIDTH:] = (gg_ref[:, :SSM_WIDTH] * jax.nn.sigmoid(gg_ref[:, SSM_WIDTH:])).astype(MX)

    return pl.pallas_call(
        body, grid=(s // tm,),
        in_specs=[_row_spec(tm, SSM_WIDTH), _row_spec(tm, SSM_WIDTH), _row_spec(tm, IN_WIDTH),
                  _row_spec(tm, ATT_WIDTH), pl.BlockSpec((1, SSM_WIDTH), lambda i: (0, 0)), _layer_spec(w_glu, l)],
        out_specs=[_row_spec(tm, SSM_WIDTH), _row_spec(tm, 2 * SSM_WIDTH), _row_spec(tm, D_MODEL)],
        out_shape=[SDS((s, SSM_WIDTH), f32), SDS((s, 2 * SSM_WIDTH), f32), SDS((s, D_MODEL), MX)],
        compiler_params=_cp("parallel"), name=name)(y_f, y_r, z, att, d_skip.reshape(1, SSM_WIDTH), w_glu)


def glu_bwd(gmix, gg, ypre, w_glu, l, *, name, tm):
    s = gg.shape[0]
    nblk, cb = w_glu.shape[1], w_glu.shape[3]

    def body(gm_ref, gg_ref, y_ref, w_ref, ggg_ref, yg_ref, gy_ref):
        gs = gm_ref[...]
        val, gate = gg_ref[:, :SSM_WIDTH], gg_ref[:, SSM_WIDTH:]
        sg = jax.nn.sigmoid(gate)
        ggg_ref[:, :SSM_WIDTH] = (gs * sg).astype(MX)
        ggg_ref[:, SSM_WIDTH:] = (gs * val * sg * (1.0 - sg)).astype(MX)
        y = y_ref[...]
        yg_ref[...] = _gelu(y).astype(MX)
        gyg = _dot(ggg_ref[:, 0:cb], w_ref[0, 0], NT)
        for b in range(1, nblk):
            gyg = gyg + _dot(ggg_ref[:, cb * b:cb * (b + 1)], w_ref[0, b], NT)
        gy_ref[...] = gyg * _gelu_grad(y)

    return pl.pallas_call(
        body, grid=(s // tm,),
        in_specs=[pl.BlockSpec((tm, SSM_WIDTH), lambda i: (i, 1)), _row_spec(tm, 2 * SSM_WIDTH),
                  _row_spec(tm, SSM_WIDTH), _layer_spec(w_glu, l)],
        out_specs=[_row_spec(tm, 2 * SSM_WIDTH), _row_spec(tm, SSM_WIDTH), _row_spec(tm, SSM_WIDTH)],
        out_shape=[SDS((s, 2 * SSM_WIDTH), MX), SDS((s, SSM_WIDTH), MX), SDS((s, SSM_WIDTH), f32)],
        compiler_params=_cp("parallel"), name=name)(gmix, gg, ypre, w_glu)


def loss_grad(y, target, *, name, tm):
    s, d = y.shape

    def body(y_ref, t_ref, g_ref, g16_ref, l_ref):
        @pl.when(pl.program_id(0) == 0)
        def _():
            l_ref[...] = jnp.zeros_like(l_ref)

        e = y_ref[...] - t_ref[...]
        g = e * (1.0 / d)
        g_ref[...] = g
        g16_ref[...] = g.astype(MX)
        l_ref[...] += _rows8(e * e)

    row = pl.BlockSpec((tm, d), lambda i: (i, 0))
    return pl.pallas_call(
        body, grid=(s // tm,), in_specs=[row, row],
        out_specs=[row, row, pl.BlockSpec((8, d), lambda i: (0, 0))],
        out_shape=[SDS((s, d), f32), SDS((s, d), MX), SDS((8, d), f32)],
        compiler_params=_cp("arbitrary"), name=name)(y, target)


def _row_tile(rows, cols):
    tr = rows
    while tr * cols > 256 * 1024 and tr % 16 == 0:
        tr //= 2
    return tr


def _elementwise(fn, ins, n_out, *, name, out_dtype=f32):
    shape = ins[0].shape
    cols = shape[-1]
    ins2 = [a.reshape(-1, cols) for a in ins]
    rows = ins2[0].shape[0]
    tr = _row_tile(rows, cols)

    def body(*refs):
        outs = fn(*[r[...] for r in refs[:len(ins)]])
        for o_ref, o in zip(refs[len(ins):], outs):
            o_ref[...] = o.astype(out_dtype)

    spec = pl.BlockSpec((tr, cols), lambda i: (i, 0))
    outs = pl.pallas_call(
        body, grid=(rows // tr,), in_specs=[spec] * len(ins), out_specs=[spec] * n_out,
        out_shape=[SDS((rows, cols), out_dtype)] * n_out, compiler_params=_cp("parallel"), name=name)(*ins2)
    return [o.reshape(shape) for o in outs]


def _adamw_math(w, g, m, v):
    m = ADAM_B1 * m + (1.0 - ADAM_B1) * g
    v = ADAM_B2 * v + (1.0 - ADAM_B2) * (g * g)
    m_hat = m / (1.0 - ADAM_B1 ** ADAM_STEP)
    v_hat = v / (1.0 - ADAM_B2 ** ADAM_STEP)
    delta = -ADAM_LR * (m_hat / (jnp.sqrt(v_hat) + ADAM_EPS) + ADAM_WD * w)
    return delta, m, v


def adamw(w, g, m, v, *, name):
    return _elementwise(_adamw_math, [w, g, m, v], 3, name=name)


SMEM = pl.BlockSpec(memory_space=pltpu.SMEM)


def _core_index():
    return lax.axis_index("c").astype(jnp.int32).reshape(1)


def adamw_halves(w, own, sib, m, v, *, name):
    depth, r, cols = w.shape
    h = r // 2
    tr = _row_tile(h, cols)
    quad = lambda a: a.reshape(depth, 2, h, cols)

    def body(c_ref, w_ref, own_ref, sib_ref, m_ref, v_ref, g_ref, d_ref, mo_ref, vo_ref):
        g = jnp.where(pl.program_id(1) == c_ref[0], own_ref[0], sib_ref[0])
        g_ref[0, 0] = g
        d_ref[0, 0], mo_ref[0, 0], vo_ref[0, 0] = _adamw_math(w_ref[0, 0], g, m_ref[0, 0], v_ref[0, 0])

    full = pl.BlockSpec((1, 1, tr, cols), lambda l, j, i: (l, j, i, 0))
    part = pl.BlockSpec((1, tr, cols), lambda l, j, i: (l, i, 0))
    outs = pl.pallas_call(
        body, grid=(depth, 2, h // tr), in_specs=[SMEM, full, part, part, full, full], out_specs=[full] * 4,
        out_shape=[SDS((depth, 2, h, cols), f32)] * 4,
        compiler_params=_cp("parallel", "parallel", "parallel"), name=name)(
            _core_index(), quad(w), own, sib, quad(m), quad(v))
    return [o.reshape(depth, r, cols) for o in outs]


def add_own_half(g4, recv, *, name):
    _, _, h, cols = g4.shape
    tr = _row_tile(h, cols)

    def body(c_ref, g_ref, r_ref, o_ref):
        own = jnp.where(c_ref[0] == 0, g_ref[0, 0], g_ref[0, 1])
        o_ref[0] = (own + r_ref[0]).astype(WIRE)

    part = pl.BlockSpec((1, tr, cols), lambda s, i: (s, i, 0))
    return pl.pallas_call(
        body, grid=(4, h // tr),
        in_specs=[SMEM, pl.BlockSpec((1, 2, tr, cols), lambda s, i: (s, 0, i, 0)), part], out_specs=part,
        out_shape=SDS((4, h, cols), WIRE), compiler_params=_cp("parallel", "parallel"), name=name)(
            _core_index(), g4, recv)


def sum4(a, *, name, into=None, layer=0):
    shape = a.shape[1:]
    cols = shape[-1]
    a2 = a.reshape(4, -1, cols)
    rows = a2.shape[1]
    tr = _row_tile(rows, cols)

    def body(*refs):
        a_ref, o_ref = refs[0], refs[-1]
        tot = ((a_ref[0].astype(f32) + a_ref[1].astype(f32)) + a_ref[2].astype(f32)) + a_ref[3].astype(f32)
        if into is None:
            o_ref[...] = tot
        else:
            o_ref[0] = tot

    in_spec = pl.BlockSpec((4, tr, cols), lambda i: (0, i, 0))
    if into is None:
        out = pl.pallas_call(
            body, grid=(rows // tr,), in_specs=[in_spec], out_specs=pl.BlockSpec((tr, cols), lambda i: (i, 0)),
            out_shape=SDS((rows, cols), f32), compiler_params=_cp("parallel"), name=name)(a2)
        return out.reshape(shape)
    stack = into.reshape(into.shape[0], rows, cols)
    out = pl.pallas_call(
        body, grid=(rows // tr,), in_specs=[in_spec, ANY],
        out_specs=pl.BlockSpec((1, tr, cols), lambda i: (layer, i, 0)),
        out_shape=SDS(stack.shape, f32), input_output_aliases={1: 0},
        compiler_params=_cp("parallel"), name=name)(a2, stack)
    return out.reshape(into.shape)


ANY = pl.BlockSpec(memory_space=pl.ANY)


def chip_exchange(arrs, bcast, *, name):
    n = len(arrs)
    piece = [a.shape if b else a.shape[1:] for a, b in zip(arrs, bcast)]

    def body(*refs):
        ins, outs = refs[:n], refs[n:2 * n]
        send, recv, loc = refs[2 * n:]
        x, y, c = lax.axis_index("x"), lax.axis_index("y"), lax.axis_index("c")
        me = 2 * x + y
        copies = []
        for k in range(n):
            own = pltpu.make_async_copy(ins[k] if bcast[k] else ins[k].at[me], outs[k].at[me], loc.at[k])
            own.start()
            copies.append(own)
            for j, (px, py) in enumerate(((1 - x, y), (x, 1 - y), (1 - x, 1 - y))):
                cp = pltpu.make_async_remote_copy(
                    src_ref=ins[k] if bcast[k] else ins[k].at[2 * px + py], dst_ref=outs[k].at[me],
                    send_sem=send.at[3 * k + j], recv_sem=recv.at[3 * k + j],
                    device_id=(px, py, c), device_id_type=MESH)
                cp.start()
                copies.append(cp)
        for cp in copies:
            cp.wait()

    return pl.pallas_call(
        body, in_specs=[ANY] * n, out_specs=[ANY] * n,
        out_shape=[SDS((4,) + tuple(p), a.dtype) for p, a in zip(piece, arrs)],
        scratch_shapes=[pltpu.SemaphoreType.DMA((3 * n,)), pltpu.SemaphoreType.DMA((3 * n,)),
                        pltpu.SemaphoreType.DMA((n,))],
        name=name)(*arrs)


def gather_weights(shards, *, name):
    n = len(shards)
    hd = shards[0].shape[0] // 2

    def body(*refs):
        ins, outs = refs[:n], refs[n:2 * n]
        send, recv = refs[2 * n:]
        x, y, c = lax.axis_index("x"), lax.axis_index("y"), lax.axis_index("c")
        me = 2 * x + y
        chips = ((1 - x, y), (x, 1 - y), (1 - x, 1 - y))
        mine, theirs = pl.ds(c * hd, hd), pl.ds((1 - c) * hd, hd)

        def ici(k, j, src, dst):
            px, py = chips[j]
            return pltpu.make_async_remote_copy(src_ref=src, dst_ref=dst, send_sem=send.at[7 * k + j],
                                                recv_sem=recv.at[7 * k + j], device_id=(px, py, c),
                                                device_id_type=MESH)

        def d2d(k, j, src, dst):
            return pltpu.make_async_remote_copy(src_ref=src, dst_ref=dst, send_sem=send.at[7 * k + 3 + j],
                                                recv_sem=recv.at[7 * k + 3 + j], device_id=(x, y, 1 - c),
                                                device_id_type=MESH)

        own, sent = [], []
        for k in range(n):
            own.append(d2d(k, 3, ins[k], outs[k].at[:, me]))
            own[-1].start()
            for j in range(3):
                sent.append(ici(k, j, ins[k].at[mine], outs[k].at[mine, me]))
                sent[-1].start()
        for k in range(n):
            for j, (px, py) in enumerate(chips):
                landed = outs[k].at[mine, 2 * px + py]
                ici(k, j, landed, landed).wait_recv()
                sent.append(d2d(k, j, landed, landed))
                sent[-1].start()
        for k in range(n):
            for j, (px, py) in enumerate(chips):
                other = outs[k].at[theirs, 2 * px + py]
                d2d(k, j, other, other).wait_recv()
        for cp in sent:
            cp.wait_send()
        for cp in own:
            cp.wait()

    return pl.pallas_call(
        body, in_specs=[ANY] * n, out_specs=[ANY] * n,
        out_shape=[SDS((a.shape[0], 4) + tuple(a.shape[1:]), a.dtype) for a in shards],
        scratch_shapes=[pltpu.SemaphoreType.DMA((7 * n,)), pltpu.SemaphoreType.DMA((7 * n,))],
        name=name)(*shards)


def sibling_exchange(arrs, half, *, name):
    n = len(arrs)
    piece = [(a.shape[0],) + a.shape[2:] if h else a.shape for a, h in zip(arrs, half)]

    def body(*refs):
        ins, outs = refs[:n], refs[n:2 * n]
        send, recv = refs[2 * n:]
        x, y, c = lax.axis_index("x"), lax.axis_index("y"), lax.axis_index("c")
        copies = []
        for k in range(n):
            cp = pltpu.make_async_remote_copy(
                src_ref=ins[k].at[:, 1 - c] if half[k] else ins[k], dst_ref=outs[k],
                send_sem=send.at[k], recv_sem=recv.at[k], device_id=(x, y, 1 - c), device_id_type=MESH)
            cp.start()
            copies.append(cp)
        for cp in copies:
            cp.wait()

    return pl.pallas_call(
        body, in_specs=[ANY] * n, out_specs=[ANY] * n,
        out_shape=[SDS(tuple(p), a.dtype) for p, a in zip(piece, arrs)],
        scratch_shapes=[pltpu.SemaphoreType.DMA((n,)), pltpu.SemaphoreType.DMA((n,))],
        name=name)(*arrs)


def ssm_discretize(lam_re, lam_im, log_dt, b_re, b_im, c_re, c_im):
    dt = jnp.exp(log_dt)[..., None]
    mag = jnp.exp(lam_re * dt)
    abr = mag * jnp.cos(lam_im * dt)
    abi = mag * jnp.sin(lam_im * dt)
    den = lam_re * lam_re + lam_im * lam_im
    zr = ((abr - 1.0) * lam_re + abi * lam_im) / den
    zi = (abi * lam_re - (abr - 1.0) * lam_im) / den
    bbr = zr[..., None] * b_re - zi[..., None] * b_im
    bbi = zr[..., None] * b_im + zi[..., None] * b_re
    eye = jnp.eye(8, dtype=f32)
    bb = jnp.stack([bbr, bbi], axis=1).reshape(2, 2, SSM_TILES, 8, SSM_STATE, SSM_GROUP)
    bmat = jnp.einsum('dqjgph,gk->djghqkp', bb, eye).reshape(2, SSM_TILES, TILE_CH, 2 * TILE_ST)
    cc = jnp.stack([c_re, -c_im], axis=1).reshape(2, 2, SSM_TILES, 8, SSM_GROUP, SSM_STATE)
    cmat = jnp.einsum('dqjghp,gk->djqkpgh', cc, eye).reshape(2, SSM_TILES, 2 * TILE_ST, TILE_CH)
    n = SSM_GROUPS * SSM_STATE
    return abr.reshape(2, n), abi.reshape(2, n), bmat, cmat


def scan_tables(ar, ai, rev, nv):
    pw = [(ar, ai)]
    for _ in range(nv - 1):
        pw.append(_cmul(ar, ai, *pw[-1]))
    big = [pw[nv - 1]]
    big.append(_cmul(*big[0], *big[0]))
    big.append(_cmul(*big[1], *big[1]))
    rows = jnp.arange(8)[:, None]
    ones = jnp.ones((8, 1), f32)
    parts = []
    for k, p in zip((1, 2, 4), big):
        cond = (rows <= 7 - k) if rev else (rows >= k)
        parts.append([jnp.where(cond, q[None, :], 0.0) for q in p])
    parts.append([ones * q[None, :] for q in big[0]])
    parts.append([ones * q[None, :] for q in pw[0]])
    for v in range(nv):
        parts.append([ones * q[None, :] for q in pw[nv - 1 - v if rev else v]])
    nrow = 40 + 8 * nv
    tre = jnp.concatenate([p[0] for p in parts], axis=0).reshape(nrow, SSM_TILES, TILE_ST)
    tim = jnp.concatenate([p[1] for p in parts], axis=0).reshape(nrow, SSM_TILES, TILE_ST)
    return jnp.concatenate([tre, tim], axis=-1).transpose(1, 0, 2)


def _tile_a(ga):
    t = ga.sum(axis=0).reshape(SSM_TILES, 2, TILE_ST)
    return t[:, 0].reshape(-1), t[:, 1].reshape(-1)


SMALL = ('norm1', 'q_gain', 'k_gain', 'sink', 'lam_re', 'lam_im', 'log_dt', 'b_re', 'b_im', 'c_re', 'c_im',
         'd_skip', 'norm2')
BIG = ('w_in', 'w_glu', 'w_out', 'w_ff1', 'w_ff2')
WEIGHTS = ('norm1', 'w_in', 'q_gain', 'k_gain', 'sink', 'lam_re', 'lam_im', 'log_dt', 'b_re', 'b_im', 'c_re',
           'c_im', 'd_skip', 'w_glu', 'w_out', 'norm2', 'w_ff1', 'w_ff2')


def _chunk(s):
    return min(256, s)


def layer_forward(l, x, p, wb):
    s = x.shape[0]
    tm = min(512, s)
    sv = {}
    h1, z = norm_mm(x, p['norm1'], wb['w_in'], l, relu2=False, name=f"l{l}_in", tm=tm)
    att = attn_fwd(z, p['q_gain'], p['k_gain'], p['sink'], name=f"l{l}_attn")
    (ar, ai, bmat, cmat), disc_vjp = jax.vjp(
        ssm_discretize, p['lam_re'], p['lam_im'], p['log_dt'], p['b_re'], p['b_im'], p['c_re'], p['c_im'])
    bmat16, cmat16 = bmat.astype(MX), cmat.astype(MX)
    ys, xbs, tabs = [], [], []
    for d, rev in enumerate((False, True)):
        tab = scan_tables(ar[d], ai[d], rev, _chunk(s) // 8)
        y_d, xb_d = ssm_fwd(z, tab, bmat16[d], cmat16[d], rev=rev, name=f"l{l}_ssm{d}", chunk=_chunk(s))
        ys.append(y_d)
        xbs.append(xb_d)
        tabs.append((tab, scan_tables(ar[d], -ai[d], not rev, _chunk(s) // 8)))
    ypre, gg, mix = glu_fwd(ys[0], ys[1], z, att, p['d_skip'], wb['w_glu'], l, name=f"l{l}_glu", tm=min(256, s))
    x1 = mm_res(mix, wb['w_out'], l, x, name=f"l{l}_out", tm=tm)
    h2, a2 = norm_mm(x1, p['norm2'], wb['w_ff1'], l, relu2=True, name=f"l{l}_ff1", tm=tm)
    x2 = mm_res(a2, wb['w_ff2'], l, x1, name=f"l{l}_ff2", tm=tm)
    sv.update(x=x, h1=h1, z=z, xbs=xbs, tabs=tabs, bmat16=bmat16, cmat16=cmat16, disc_vjp=disc_vjp,
              ypre=ypre, gg=gg, mix=mix, x1=x1, h2=h2, a2=a2)
    return x2, sv


def layer_backward(l, gx2, gx2h, p, wb, sv):
    s = gx2.shape[0]
    tm = min(512, s)
    ts = min(1024, s)
    g = {}
    gf = mm_nt(gx2h, wb['w_ff2'], l, name=f"l{l}_bff2", tm=tm, a2=sv['a2'])
    g['w_ff2'] = mm_tn(sv['a2'], gx2h, name=f"l{l}_wff2", tk=1024, tn=1024, ts=ts).reshape(4, D_FF // 4, D_MODEL)
    gx1, gx1h, gn2 = mm_nt_norm(gf, wb['w_ff1'], l, sv['x1'], p['norm2'], gx2, name=f"l{l}_bff1", tm=min(256, s))
    g['norm2'] = gn2.sum(axis=0)
    g['w_ff1'] = mm_tn(sv['h2'], gf, name=f"l{l}_wff1", tk=1024, tn=1024, ts=ts, chip_major=True)
    gmix = mm_nt(gx1h, wb['w_out'], l, name=f"l{l}_bout", tm=tm)
    g['w_out'] = mm_tn(sv['mix'], gx1h, name=f"l{l}_wout", tk=1024, tn=1024, ts=ts).reshape(4, D_MODEL // 4, D_MODEL)
    ggg, yg, gy = glu_bwd(gmix, sv['gg'], sv['ypre'], wb['w_glu'], l, name=f"l{l}_bglu", tm=min(256, s))
    g['w_glu'] = mm_tn(yg, ggg, name=f"l{l}_wglu", tk=512, tn=256, ts=ts, chip_major=True)
    gus, gas, gbs, gcs = [], [], [], []
    for d, rev in enumerate((False, True)):
        tab_s, tab_a = sv['tabs'][d]
        gu_d, ga_d, gb_d, gc_d = ssm_bwd(sv['z'], gy, sv['xbs'][d], tab_s, tab_a, sv['bmat16'][d], sv['cmat16'][d],
                                         rev=rev, name=f"l{l}_bssm{d}", chunk=_chunk(s))
        gus.append(gu_d)
        gas.append(_tile_a(ga_d))
        gbs.append(gb_d)
        gcs.append(gc_d)
    gar = jnp.stack([gas[0][0], gas[1][0]])
    gai = jnp.stack([gas[0][1], gas[1][1]])
    (g['lam_re'], g['lam_im'], g['log_dt'], g['b_re'], g['b_im'], g['c_re'], g['c_im']) = sv['disc_vjp'](
        (gar, gai, jnp.stack(gbs), jnp.stack(gcs)))
    gq, dkv, gqg, gsk = attn_bwd(sv['z'], gmix, p['q_gain'], p['k_gain'], p['sink'], name=f"l{l}_battn")
    g['q_gain'] = gqg.sum(axis=0)
    g['sink'] = gsk[:, 0]
    gz, gkg, gd = gz_assemble(gq, dkv, sv['z'], p['k_gain'], gus[0], gus[1], gy, p['d_skip'], name=f"l{l}_gz")
    g['k_gain'] = gkg.sum(axis=0)
    g['d_skip'] = gd.sum(axis=0)
    gx, gxh, gn1 = mm_nt_norm(gz, wb['w_in'], l, sv['x'], p['norm1'], gx1, name=f"l{l}_bin", tm=tm)
    g['norm1'] = gn1.sum(axis=0)
    gw_in = mm_tn(sv['h1'], gz, name=f"l{l}_win", tk=1024, tn=640, ts=ts)
    g['w_in'] = gw_in.reshape(D_MODEL, 4, IN_WIDTH // 4).transpose(1, 0, 2)
    return gx, gxh, g


def stack_layouts(gathered):
    w_in = gathered['w_in']
    depth = w_in.shape[0]
    return dict(w_in=w_in.transpose(0, 2, 1, 3).reshape(depth, D_MODEL, IN_WIDTH),
                w_glu=gathered['w_glu'], w_ff1=gathered['w_ff1'],
                w_out=gathered['w_out'].reshape(depth, D_MODEL, D_MODEL),
                w_ff2=gathered['w_ff2'].reshape(depth, D_FF, D_MODEL))


def local_step(x, target, small, wb, after_layer=None):
    depth = wb['w_in'].shape[0]
    saves = []
    for l in range(depth):
        p = {k: small[k][l] for k in SMALL}
        x, sv = layer_forward(l, x, p, wb)
        saves.append(sv)
    gx, gxh, lparts = loss_grad(x, target, name="loss", tm=min(512, x.shape[0]))
    grads = [None] * depth
    for l in reversed(range(depth)):
        p = {k: small[k][l] for k in SMALL}
        gx, gxh, g = layer_backward(l, gx, gxh, p, wb, saves[l])
        grads[l] = g if after_layer is None else after_layer(l, g)
    return lparts, gx, grads


def reduce_layer(l, g, stacks):
    arrs = [g[k].reshape(4, 2, g[k].shape[1] // 2, g[k].shape[2]) for k in BIG]
    got = sibling_exchange(arrs, [True] * len(BIG), name=f"l{l}_rsib")
    sums = [add_own_half(a, b, name=f"l{l}_radd_{k}") for k, a, b in zip(BIG, arrs, got)]
    got = chip_exchange(sums, [False] * len(BIG), name=f"l{l}_rchips")
    return {k: sum4(a, name=f"l{l}_rsum_{k}", into=stacks[k], layer=l) for k, a in zip(BIG, got)}


def reduce_small(packed):
    got = sibling_exchange([packed], [False], name="small_rsib")
    pair = _elementwise(lambda a, b: (a + b,), [packed, got[0]], 1, name="small_radd")[0]
    got = chip_exchange([pair], [True], name="small_rchips")
    return sum4(got[0], name="small_rsum")


def _pack_small(tree):
    flat = jnp.concatenate([tree[k].reshape(-1) for k in SMALL])
    pad = (-flat.shape[0]) % 1024
    return jnp.pad(flat, (0, pad)).reshape(-1, 128)


def _unpack_small(packed, like):
    flat = packed.reshape(-1)
    out, off = {}, 0
    for k in SMALL:
        n = like[k].size
        out[k] = flat[off:off + n].reshape(like[k].shape)
        off += n
    return out


def kernel(x, norm1, w_in, q_gain, k_gain, sink, lam_re, lam_im, log_dt, b_re, b_im, c_re, c_im, d_skip, w_glu, w_out, norm2, w_ff1, w_ff2, loss_target, m_norm1, m_w_in, m_q_gain, m_k_gain, m_sink, m_lam_re, m_lam_im, m_log_dt, m_b_re, m_b_im, m_c_re, m_c_im, m_d_skip, m_w_glu, m_w_out, m_norm2, m_w_ff1, m_w_ff2, v_norm1, v_w_in, v_q_gain, v_k_gain, v_sink, v_lam_re, v_lam_im, v_log_dt, v_b_re, v_b_im, v_c_re, v_c_im, v_d_skip, v_w_glu, v_w_out, v_norm2, v_w_ff1, v_w_ff2):
    w = dict(norm1=norm1, w_in=w_in, q_gain=q_gain, k_gain=k_gain, sink=sink, lam_re=lam_re, lam_im=lam_im,
             log_dt=log_dt, b_re=b_re, b_im=b_im, c_re=c_re, c_im=c_im, d_skip=d_skip, w_glu=w_glu, w_out=w_out,
             norm2=norm2, w_ff1=w_ff1, w_ff2=w_ff2)
    m = dict(norm1=m_norm1, w_in=m_w_in, q_gain=m_q_gain, k_gain=m_k_gain, sink=m_sink, lam_re=m_lam_re,
             lam_im=m_lam_im, log_dt=m_log_dt, b_re=m_b_re, b_im=m_b_im, c_re=m_c_re, c_im=m_c_im,
             d_skip=m_d_skip, w_glu=m_w_glu, w_out=m_w_out, norm2=m_norm2, w_ff1=m_w_ff1, w_ff2=m_w_ff2)
    v = dict(norm1=v_norm1, w_in=v_w_in, q_gain=v_q_gain, k_gain=v_k_gain, sink=v_sink, lam_re=v_lam_re,
             lam_im=v_lam_im, log_dt=v_log_dt, b_re=v_b_re, b_im=v_b_im, c_re=v_c_re, c_im=v_c_im,
             d_skip=v_d_skip, w_glu=v_w_glu, w_out=v_w_out, norm2=v_norm2, w_ff1=v_w_ff1, w_ff2=v_w_ff2)
    depth = w_in.shape[0]

    gathered = gather_weights([w[k].astype(WIRE) for k in BIG], name="gather_w")
    wb = stack_layouts(dict(zip(BIG, gathered)))
    small = {k: w[k] for k in SMALL}

    stacks = [{k: jnp.zeros((depth, w[k].shape[1] // 2, w[k].shape[2]), f32) for k in BIG}]

    def after_layer(l, g):
        stacks[0] = reduce_layer(l, g, stacks[0])
        return {k: g[k] for k in SMALL}

    lparts, gx, grads = local_step(x[0], loss_target[0], small, wb, after_layer)
    loss = lax.psum(0.5 * jnp.sum(lparts) / D_MODEL, ("x", "y", "c"))

    sib = sibling_exchange([stacks[0][k] for k in BIG], [False] * len(BIG), name="reduce_back")
    gsmall = reduce_small(_pack_small({k: jnp.stack([grads[l][k] for l in range(depth)]) for k in SMALL}))
    like = {k: w[k] for k in SMALL}
    gfull = _unpack_small(gsmall, like)

    delta, new_m, new_v = {}, {}, {}
    for k, sib_k in zip(BIG, sib):
        gfull[k], delta[k], new_m[k], new_v[k] = adamw_halves(w[k], stacks[0][k], sib_k, m[k], v[k],
                                                              name=f"adamw_{k}")
    ds, ms, vs = adamw(_pack_small(like), gsmall, _pack_small({k: m[k] for k in SMALL}),
                       _pack_small({k: v[k] for k in SMALL}), name="adamw_small")
    delta.update(_unpack_small(ds, like))
    new_m.update(_unpack_small(ms, like))
    new_v.update(_unpack_small(vs, like))

    return (loss, gx[None], *[gfull[k] for k in WEIGHTS], *[delta[k] for k in WEIGHTS],
            *[new_m[k] for k in WEIGHTS], *[new_v[k] for k in WEIGHTS])
```

```python
import functools
import math

import jax
import jax.numpy as jnp
from jax import lax
from jax.experimental import pallas as pl
from jax.experimental.pallas import tpu as pltpu

f32 = jnp.float32
MX = jnp.bfloat16
WIRE = jnp.bfloat16
SDS = jax.ShapeDtypeStruct

D_MODEL = 1024
DEPTH = 4
ATT_HEADS = 8
KV_HEADS = 2
GQA = ATT_HEADS // KV_HEADS
HEAD_DIM = 64
ATT_WIDTH = ATT_HEADS * HEAD_DIM
KV_WIDTH = KV_HEADS * HEAD_DIM
BLOCK = 128
SSM_WIDTH = 512
SSM_GROUP = 16
SSM_GROUPS = 32
SSM_STATE = 64
SSM_TILES = 4
TILE_CH = SSM_WIDTH // SSM_TILES
TILE_ST = SSM_GROUPS * SSM_STATE // SSM_TILES
SLAB = 256
IN_WIDTH = ATT_WIDTH + 2 * KV_WIDTH + SSM_WIDTH
U_OFF = ATT_WIDTH + 2 * KV_WIDTH
D_FF = 4096
EPS = 1e-6
NEG = float(jnp.finfo(jnp.float32).min)
SLOPES = tuple(2.0 ** (-8.0 * (h + 1) / ATT_HEADS) for h in range(ATT_HEADS))

ADAM_LR, ADAM_B1, ADAM_B2, ADAM_EPS, ADAM_WD, ADAM_STEP = 0.001, 0.9, 0.999, 1e-08, 0.01, 10

VMEM_LIMIT = 48 * 1024 * 1024
MESH = pl.DeviceIdType.MESH

NT = (((1,), (1,)), ((), ()))
TN = (((0,), (0,)), ((), ()))


def _cp(*sem):
    return pltpu.CompilerParams(dimension_semantics=sem, vmem_limit_bytes=VMEM_LIMIT)


def _dot(a, b, dims=None):
    if dims is None:
        return jnp.dot(a, b, preferred_element_type=f32)
    return lax.dot_general(a, b, dims, preferred_element_type=f32)


def _rows8(v):
    return v.reshape(v.shape[0] // 8, 8, v.shape[1]).sum(axis=0)


def _layer_spec(w, l):
    nd = w.ndim
    return pl.BlockSpec((1,) + tuple(w.shape[1:]), lambda i: (l,) + (0,) * (nd - 1))


def _row_spec(tm, width):
    return pl.BlockSpec((tm, width), lambda i: (i, 0))


def norm_mm(x, gain, w, l, *, relu2, name, tm):
    s, d = x.shape
    if relu2:
        nblk, cb = w.shape[1], w.shape[3]
        n = nblk * cb
    else:
        n = w.shape[2]

    def body(x_ref, g_ref, w_ref, h_ref, y_ref):
        xf = x_ref[...]
        r = lax.rsqrt(jnp.mean(xf * xf, axis=-1, keepdims=True) + EPS)
        h = (xf * r * g_ref[...]).astype(MX)
        h_ref[...] = h
        if relu2:
            for b in range(nblk):
                f = jnp.maximum(_dot(h, w_ref[0, b]), 0.0)
                y_ref[:, cb * b:cb * (b + 1)] = (f * f).astype(MX)
        else:
            y_ref[...] = _dot(h, w_ref[0])

    return pl.pallas_call(
        body, grid=(s // tm,),
        in_specs=[_row_spec(tm, d), pl.BlockSpec((1, d), lambda i: (0, 0)), _layer_spec(w, l)],
        out_specs=[_row_spec(tm, d), _row_spec(tm, n)],
        out_shape=[SDS((s, d), MX), SDS((s, n), MX if relu2 else f32)],
        compiler_params=_cp("parallel"), name=name)(x, gain.reshape(1, d), w)


def mm_res(a, w, l, res, *, name, tm):
    s, k = a.shape
    n = w.shape[2]

    def body(a_ref, w_ref, r_ref, o_ref):
        o_ref[...] = r_ref[...] + _dot(a_ref[...], w_ref[0])

    return pl.pallas_call(
        body, grid=(s // tm,), in_specs=[_row_spec(tm, k), _layer_spec(w, l), _row_spec(tm, n)],
        out_specs=_row_spec(tm, n), out_shape=SDS((s, n), f32), compiler_params=_cp("parallel"), name=name)(a, w, res)


def mm_nt(gy, w, l, *, name, tm, a2=None):
    s, n = gy.shape
    k = w.shape[1]
    kb = min(k, 1024)

    def body(*refs):
        if a2 is None:
            g_ref, w_ref, o_ref = refs
        else:
            g_ref, w_ref, a_ref, o_ref = refs
        g = g_ref[...]
        for b in range(k // kb):
            cols = slice(kb * b, kb * (b + 1))
            acc = _dot(g, w_ref[0, cols, :], NT)
            if a2 is not None:
                acc = acc * (2.0 * jnp.sqrt(a_ref[:, cols].astype(f32)))
            o_ref[:, cols] = acc.astype(o_ref.dtype)

    in_specs = [_row_spec(tm, n), _layer_spec(w, l)]
    args = [gy, w]
    if a2 is not None:
        in_specs.append(_row_spec(tm, k))
        args.append(a2)
    return pl.pallas_call(
        body, grid=(s // tm,), in_specs=in_specs, out_specs=_row_spec(tm, k),
        out_shape=SDS((s, k), f32 if a2 is None else MX), compiler_params=_cp("parallel"), name=name)(*args)


def mm_nt_norm(gy, w, l, x, gain, res, *, name, tm):
    s, n = gy.shape
    d = x.shape[1]

    def body(g_ref, w_ref, x_ref, gn_ref, r_ref, o_ref, o16_ref, gg_ref):
        @pl.when(pl.program_id(0) == 0)
        def _():
            gg_ref[...] = jnp.zeros_like(gg_ref)

        if w.ndim == 3:
            gh = _dot(g_ref[...], w_ref[0], NT)
        else:
            cb = w.shape[3]
            gh = _dot(g_ref[:, 0:cb], w_ref[0, 0], NT)
            for b in range(1, w.shape[1]):
                gh = gh + _dot(g_ref[:, cb * b:cb * (b + 1)], w_ref[0, b], NT)
        xf = x_ref[...]
        r = lax.rsqrt(jnp.mean(xf * xf, axis=-1, keepdims=True) + EPS)
        xh = xf * r
        t = gh * gn_ref[...]
        gx = r_ref[...] + r * (t - xh * jnp.mean(t * xh, axis=-1, keepdims=True))
        o_ref[...] = gx
        o16_ref[...] = gx.astype(MX)
        gg_ref[...] += _rows8(gh * xh)

    return pl.pallas_call(
        body, grid=(s // tm,),
        in_specs=[_row_spec(tm, n), _layer_spec(w, l), _row_spec(tm, d), pl.BlockSpec((1, d), lambda i: (0, 0)),
                  _row_spec(tm, d)],
        out_specs=[_row_spec(tm, d), _row_spec(tm, d), pl.BlockSpec((8, d), lambda i: (0, 0))],
        out_shape=[SDS((s, d), f32), SDS((s, d), MX), SDS((8, d), f32)],
        compiler_params=_cp("arbitrary"), name=name)(gy, w, x, gain.reshape(1, d), res)


def mm_tn(xa, gy, *, name, tk, tn, ts, chip_major=False):
    s, k = xa.shape
    n = gy.shape[1]

    def body(x_ref, g_ref, o_ref):
        @pl.when(pl.program_id(2) == 0)
        def _():
            o_ref[...] = jnp.zeros_like(o_ref)

        acc = _dot(x_ref[...], g_ref[...], TN)
        if chip_major:
            o_ref[0] += acc
        else:
            o_ref[...] += acc

    if chip_major:
        out_spec = pl.BlockSpec((1, tk, tn), lambda a, b, c: (b, a, 0))
        out_shape = SDS((n // tn, k, tn), f32)
    else:
        out_spec = pl.BlockSpec((tk, tn), lambda a, b, c: (a, b))
        out_shape = SDS((k, n), f32)
    return pl.pallas_call(
        body, grid=(k // tk, n // tn, s // ts),
        in_specs=[pl.BlockSpec((ts, tk), lambda a, b, c: (c, a)), pl.BlockSpec((ts, tn), lambda a, b, c: (c, b))],
        out_specs=out_spec, out_shape=out_shape,
        compiler_params=_cp("parallel", "parallel", "arbitrary"), name=name)(xa, gy)


def head_mean_matrix(width):
    return jnp.kron(jnp.eye(width // HEAD_DIM, dtype=f32), jnp.full((HEAD_DIM, HEAD_DIM), 1.0 / HEAD_DIM, f32)).astype(MX)


def _head_mean(t, e_ref):
    hi = t.astype(MX)
    lo = (t - hi.astype(f32)).astype(MX)
    return _dot(hi, e_ref[...]) + _dot(lo, e_ref[...])


def qk_prep(z, q_gain, k_gain, eq, ek, *, name, tm):
    s = z.shape[0]

    def body(z_ref, qg_ref, kg_ref, eq_ref, ek_ref, q_ref, kv_ref):
        q = z_ref[:, 0:ATT_WIDTH]
        r = lax.rsqrt(_head_mean(q * q, eq_ref) + EPS)
        q_ref[...] = ((q * r * qg_ref[...]) * 0.125).astype(MX)
        k = z_ref[:, ATT_WIDTH:ATT_WIDTH + KV_WIDTH]
        r = lax.rsqrt(_head_mean(k * k, ek_ref) + EPS)
        kv_ref[:, 0:KV_WIDTH] = (k * r * kg_ref[...]).astype(MX)
        kv_ref[:, KV_WIDTH:] = z_ref[:, ATT_WIDTH + KV_WIDTH:U_OFF].astype(MX)

    const = lambda a: pl.BlockSpec(a.shape, lambda i: (0, 0))
    qg = jnp.tile(q_gain.reshape(1, HEAD_DIM), (1, ATT_HEADS))
    kg = jnp.tile(k_gain.reshape(1, HEAD_DIM), (1, KV_HEADS))
    return pl.pallas_call(
        body, grid=(s // tm,), in_specs=[_row_spec(tm, IN_WIDTH), const(qg), const(kg), const(eq), const(ek)],
        out_specs=[_row_spec(tm, ATT_WIDTH), _row_spec(tm, 2 * KV_WIDTH)],
        out_shape=[SDS((s, ATT_WIDTH), MX), SDS((s, 2 * KV_WIDTH), MX)],
        compiler_params=_cp("parallel"), name=name)(z, qg, kg, eq, ek)


def _attn_mask(i, nb):
    row = lax.broadcasted_iota(jnp.int32, (GQA * BLOCK, 3 * BLOCK), 0) & (BLOCK - 1)
    col = lax.broadcasted_iota(jnp.int32, (GQA * BLOCK, 3 * BLOCK), 1)
    dist = jnp.abs(row - col + BLOCK)
    valid = (dist <= BLOCK) & ((col >= BLOCK) | (i >= 1)) & ((col < 2 * BLOCK) | (i <= nb - 2))
    return dist.astype(f32), valid


def _attn_specs(nb):
    return [pl.BlockSpec((BLOCK, ATT_WIDTH), lambda i: (i, 0)),
            pl.BlockSpec((BLOCK, 2 * KV_WIDTH), lambda i: (jnp.maximum(i - 1, 0), 0)),
            pl.BlockSpec((BLOCK, 2 * KV_WIDTH), lambda i: (i, 0)),
            pl.BlockSpec((BLOCK, 2 * KV_WIDTH), lambda i: (jnp.minimum(i + 1, nb - 1), 0)),
            pl.BlockSpec(memory_space=pltpu.SMEM)]


def _attn_probs(sc, kvh, distf, valid, sink_ref):
    row = lax.broadcasted_iota(jnp.int32, (GQA * BLOCK, 1), 0)
    slope = jnp.full((GQA * BLOCK, 1), SLOPES[GQA * kvh], f32)
    sk = jnp.full((GQA * BLOCK, 1), sink_ref[GQA * kvh], f32)
    for j in range(1, GQA):
        slope = jnp.where(row >= BLOCK * j, SLOPES[GQA * kvh + j], slope)
        sk = jnp.where(row >= BLOCK * j, sink_ref[GQA * kvh + j], sk)
    sg = jnp.where(valid, sc - slope * distf, NEG)
    m = jnp.maximum(jnp.max(sg, axis=-1, keepdims=True), sk)
    e = jnp.exp(sg - m)
    es = jnp.exp(sk - m)
    inv = 1.0 / (jnp.sum(e, axis=-1, keepdims=True) + es)
    return e * inv, es * inv


def _stack_heads(ref, kvh):
    return jnp.concatenate([ref[:, HEAD_DIM * (GQA * kvh + g):HEAD_DIM * (GQA * kvh + g + 1)] for g in range(GQA)],
                           axis=0)


def attn_fwd(qn, kv, sink, *, name):
    s = qn.shape[0]
    nb = s // BLOCK

    def body(q_ref, kp_ref, kc_ref, kn_ref, sink_ref, o_ref):
        i = pl.program_id(0)
        distf, valid = _attn_mask(i, nb)
        kv3 = jnp.concatenate([kp_ref[...], kc_ref[...], kn_ref[...]], axis=0)
        for kvh in range(KV_HEADS):
            kn = kv3[:, HEAD_DIM * kvh:HEAD_DIM * (kvh + 1)]
            vh = kv3[:, KV_WIDTH + HEAD_DIM * kvh:KV_WIDTH + HEAD_DIM * (kvh + 1)]
            sc = _dot(_stack_heads(q_ref, kvh), kn, NT)
            p, _ = _attn_probs(sc, kvh, distf, valid, sink_ref)
            o = _dot(p.astype(MX), vh)
            for g in range(GQA):
                h = GQA * kvh + g
                o_ref[:, HEAD_DIM * h:HEAD_DIM * (h + 1)] = o[BLOCK * g:BLOCK * (g + 1)].astype(o_ref.dtype)

    return pl.pallas_call(
        body, grid=(nb,), in_specs=_attn_specs(nb),
        out_specs=pl.BlockSpec((BLOCK, ATT_WIDTH), lambda i: (i, 0)),
        out_shape=SDS((s, ATT_WIDTH), MX), compiler_params=_cp("parallel"), name=name)(qn, kv, kv, kv, sink)


def attn_bwd(qn, kv, gmix, sink, *, name):
    s = qn.shape[0]
    nb = s // BLOCK

    def body(q_ref, kp_ref, kc_ref, kn_ref, sink_ref, go_ref, gq_ref, dkv_ref, gs_ref):
        i = pl.program_id(0)

        @pl.when(i == 0)
        def _():
            gs_ref[...] = jnp.zeros_like(gs_ref)

        distf, valid = _attn_mask(i, nb)
        kv3 = jnp.concatenate([kp_ref[...], kc_ref[...], kn_ref[...]], axis=0)
        for kvh in range(KV_HEADS):
            kn = kv3[:, HEAD_DIM * kvh:HEAD_DIM * (kvh + 1)]
            vh = kv3[:, KV_WIDTH + HEAD_DIM * kvh:KV_WIDTH + HEAD_DIM * (kvh + 1)]
            qs = _stack_heads(q_ref, kvh)
            dos = _stack_heads(go_ref, kvh).astype(MX)
            p, psink = _attn_probs(_dot(qs, kn, NT), kvh, distf, valid, sink_ref)
            dp = _dot(dos, vh, NT)
            delta = jnp.sum(p * dp, axis=-1, keepdims=True)
            gsk = psink * delta
            for g in range(GQA):
                h = GQA * kvh + g
                gs_ref[h:h + 1, :] -= jnp.broadcast_to(
                    jnp.sum(gsk[BLOCK * g:BLOCK * (g + 1)], axis=0, keepdims=True), (1, 128))
            ds = (p * (dp - delta)).astype(MX)
            gv = _dot(p.astype(MX), dos, TN)
            gkn = _dot(ds, qs, TN)
            gqs = _dot(ds, kn)
            for g in range(GQA):
                h = GQA * kvh + g
                gq_ref[:, HEAD_DIM * h:HEAD_DIM * (h + 1)] = gqs[BLOCK * g:BLOCK * (g + 1)]
            for b in range(3):
                dkv_ref[b, :, HEAD_DIM * kvh:HEAD_DIM * (kvh + 1)] = gkn[BLOCK * b:BLOCK * (b + 1)]
                dkv_ref[b, :, KV_WIDTH + HEAD_DIM * kvh:KV_WIDTH + HEAD_DIM * (kvh + 1)] = gv[BLOCK * b:BLOCK * (b + 1)]

    return pl.pallas_call(
        body, grid=(nb,),
        in_specs=_attn_specs(nb) + [pl.BlockSpec((BLOCK, ATT_WIDTH), lambda i: (i, 0))],
        out_specs=[pl.BlockSpec((BLOCK, ATT_WIDTH), lambda i: (i, 0)),
                   pl.BlockSpec((3, BLOCK, 2 * KV_WIDTH), lambda i: (0, i, 0)),
                   pl.BlockSpec((ATT_HEADS, 128), lambda i: (0, 0))],
        out_shape=[SDS((s, ATT_WIDTH), f32), SDS((3, s, 2 * KV_WIDTH), f32), SDS((ATT_HEADS, 128), f32)],
        compiler_params=_cp("arbitrary"), name=name)(qn, kv, kv, kv, sink, gmix)


def gz_assemble(gqs, dkv, z, q_gain, k_gain, eq, ek, gu_f, gu_r, gy, d_skip, *, name):
    s = z.shape[0]
    nb = s // BLOCK

    def norm_bwd(t_in, g_out, gain_ref, e_ref):
        r = lax.rsqrt(_head_mean(t_in * t_in, e_ref) + EPS)
        hat = t_in * r
        t = g_out * gain_ref[...]
        return r * (t - hat * _head_mean(t * hat, e_ref)), g_out * hat

    def body(gq_ref, d0_ref, d1_ref, d2_ref, z_ref, qg_ref, kg_ref, eq_ref, ek_ref, guf_ref, gur_ref, gy_ref, ds_ref,
             gz_ref, gqg_ref, gkg_ref, gd_ref):
        i = pl.program_id(0)

        @pl.when(i == 0)
        def _():
            gqg_ref[...] = jnp.zeros_like(gqg_ref)
            gkg_ref[...] = jnp.zeros_like(gkg_ref)
            gd_ref[...] = jnp.zeros_like(gd_ref)

        gq, gg = norm_bwd(z_ref[:, 0:ATT_WIDTH], gq_ref[...] * 0.125, qg_ref, eq_ref)
        gz_ref[:, 0:ATT_WIDTH] = gq.astype(MX)
        gqg_ref[...] += _rows8(gg)
        gkv = d1_ref[0] + jnp.where(i + 1 < nb, d0_ref[0], 0.0) + jnp.where(i >= 1, d2_ref[0], 0.0)
        gk, gg = norm_bwd(z_ref[:, ATT_WIDTH:ATT_WIDTH + KV_WIDTH], gkv[:, 0:KV_WIDTH], kg_ref, ek_ref)
        gz_ref[:, ATT_WIDTH:ATT_WIDTH + KV_WIDTH] = gk.astype(MX)
        gkg_ref[...] += _rows8(gg)
        gz_ref[:, ATT_WIDTH + KV_WIDTH:U_OFF] = gkv[:, KV_WIDTH:].astype(MX)
        gyv = gy_ref[...]
        gz_ref[:, U_OFF:IN_WIDTH] = (guf_ref[...] + gur_ref[...] + ds_ref[...] * gyv).astype(MX)
        gd_ref[...] += _rows8(gyv * z_ref[:, U_OFF:IN_WIDTH])

    row = lambda w: pl.BlockSpec((BLOCK, w), lambda i: (i, 0))
    const = lambda a: pl.BlockSpec(a.shape, lambda i: (0, 0))
    qg = jnp.tile(q_gain.reshape(1, HEAD_DIM), (1, ATT_HEADS))
    kg = jnp.tile(k_gain.reshape(1, HEAD_DIM), (1, KV_HEADS))
    return pl.pallas_call(
        body, grid=(nb,),
        in_specs=[row(ATT_WIDTH),
                  pl.BlockSpec((1, BLOCK, 2 * KV_WIDTH), lambda i: (0, jnp.minimum(i + 1, nb - 1), 0)),
                  pl.BlockSpec((1, BLOCK, 2 * KV_WIDTH), lambda i: (1, i, 0)),
                  pl.BlockSpec((1, BLOCK, 2 * KV_WIDTH), lambda i: (2, jnp.maximum(i - 1, 0), 0)),
                  row(IN_WIDTH), const(qg), const(kg), const(eq), const(ek),
                  row(SSM_WIDTH), row(SSM_WIDTH), row(SSM_WIDTH), pl.BlockSpec((1, SSM_WIDTH), lambda i: (0, 0))],
        out_specs=[row(IN_WIDTH), pl.BlockSpec((8, ATT_WIDTH), lambda i: (0, 0)),
                   pl.BlockSpec((8, KV_WIDTH), lambda i: (0, 0)), pl.BlockSpec((8, SSM_WIDTH), lambda i: (0, 0))],
        out_shape=[SDS((s, IN_WIDTH), MX), SDS((8, ATT_WIDTH), f32), SDS((8, KV_WIDTH), f32),
                   SDS((8, SSM_WIDTH), f32)],
        compiler_params=_cp("arbitrary"), name=name)(
            gqs, dkv, dkv, dkv, z, qg, kg, eq, ek, gu_f, gu_r, gy, d_skip.reshape(1, SSM_WIDTH))


def _cmul(ar, ai, xr, xi):
    return ar * xr - ai * xi, ar * xi + ai * xr


def _permute_rows(src_ref, dst_ref, nv):
    for v in range(nv):
        dst_ref[8 * v:8 * v + 8, :] = src_ref[pl.ds(v, 8, stride=nv), :]


def _unpermute_rows(val, dst_ref, nv):
    for v in range(nv):
        dst_ref[pl.ds(v, 8, stride=nv), :] = val[8 * v:8 * v + 8, :]


def _scan_chunk(x_ref, tab_ref, carry_ref, nv, rev, acc=None):
    L = TILE_ST
    order = list(range(nv - 1, -1, -1)) if rev else list(range(nv))
    a_r, a_i = tab_ref[32:40, :L], tab_ref[32:40, L:]
    pr = pi = None
    for v in order:
        rows = slice(8 * v, 8 * v + 8)
        xr, xi = x_ref[rows, :L], x_ref[rows, L:]
        if pr is not None:
            mr, mi = _cmul(a_r, a_i, pr, pi)
            xr, xi = xr + mr, xi + mi
            x_ref[rows, :L] = xr
            x_ref[rows, L:] = xi
        pr, pi = xr, xi
    er, ei = pr, pi
    row = lax.broadcasted_iota(jnp.int32, (8, L), 0)
    edge = row == (7 if rev else 0)
    sh = 7 if rev else 1
    fr = jnp.where(edge, carry_ref[:, :L], pltpu.roll(er, sh, 0))
    fi = jnp.where(edge, carry_ref[:, L:], pltpu.roll(ei, sh, 0))
    for n, k in enumerate((1, 2, 4)):
        mr, mi = tab_ref[8 * n:8 * n + 8, :L], tab_ref[8 * n:8 * n + 8, L:]
        sh = (8 - k) if rev else k
        rr, ri = pltpu.roll(fr, sh, 0), pltpu.roll(fi, sh, 0)
        fr, fi = fr + mr * rr - mi * ri, fi + mr * ri + mi * rr
    dr, di = _cmul(tab_ref[24:32, :L], tab_ref[24:32, L:], fr, fi)
    last = 0 if rev else 7
    carry_ref[:, :L] = jnp.broadcast_to((dr + er)[last:last + 1, :], (8, L))
    carry_ref[:, L:] = jnp.broadcast_to((di + ei)[last:last + 1, :], (8, L))
    qr, qi = fr, fi
    if acc is not None:
        sr, si = jnp.zeros((8, L), f32), jnp.zeros((8, L), f32)
    for v in order:
        rows = slice(8 * v, 8 * v + 8)
        trow = slice(40 + v, 41 + v)
        mr, mi = _cmul(tab_ref[trow, :L], tab_ref[trow, L:], fr, fi)
        xr, xi = x_ref[rows, :L] + mr, x_ref[rows, L:] + mi
        x_ref[rows, :L] = xr
        x_ref[rows, L:] = xi
        if acc is not None:
            gr, gi = acc[0][rows, :L], acc[0][rows, L:]
            sr, si = sr + gr * qr + gi * qi, si + gi * qr - gr * qi
            qr, qi = xr, xi
    if acc is not None:
        acc[1][:, :L] += sr
        acc[1][:, L:] += si


def ssm_fwd(z, tab, bmat, cmat, *, rev, name, chunk):
    s = z.shape[0]
    nc = s // chunk
    nv = chunk // 8
    ci = (lambda i: nc - 1 - i) if rev else (lambda i: i)

    def body(u_ref, tab_ref, b_ref, c_ref, y_ref, xb_ref, u_scr, x_scr, carry):
        @pl.when(pl.program_id(1) == 0)
        def _():
            carry[...] = jnp.zeros_like(carry)

        xb_ref[0] = carry[...]
        _permute_rows(u_ref, u_scr, nv)
        x_scr[...] = _dot(u_scr[...].astype(MX), b_ref[0])
        _scan_chunk(x_scr, tab_ref.at[0], carry, nv, rev)
        _unpermute_rows(_dot(x_scr[...].astype(MX), c_ref[0]), y_ref, nv)

    return pl.pallas_call(
        body, grid=(SSM_TILES, nc),
        in_specs=[pl.BlockSpec((chunk, TILE_CH), lambda j, i: (ci(i), U_OFF // TILE_CH + j)),
                  pl.BlockSpec((1, 40 + nv, 2 * TILE_ST), lambda j, i: (j, 0, 0)),
                  pl.BlockSpec((1, TILE_CH, 2 * TILE_ST), lambda j, i: (j, 0, 0)),
                  pl.BlockSpec((1, 2 * TILE_ST, TILE_CH), lambda j, i: (j, 0, 0))],
        out_specs=[pl.BlockSpec((chunk, TILE_CH), lambda j, i: (ci(i), j)),
                   pl.BlockSpec((1, 8, 2 * TILE_ST), lambda j, i: (ci(i), 0, j))],
        out_shape=[SDS((s, SSM_WIDTH), f32), SDS((nc, 8, SSM_TILES * 2 * TILE_ST), f32)],
        scratch_shapes=[pltpu.VMEM((chunk, TILE_CH), f32), pltpu.VMEM((chunk, 2 * TILE_ST), f32),
                        pltpu.VMEM((8, 2 * TILE_ST), f32)],
        compiler_params=_cp("parallel", "arbitrary"), name=name)(z, tab, bmat, cmat)


def ssm_bwd(z, gy, xb, tab_s, tab_a, bmat, cmat, *, rev, name, chunk):
    s = z.shape[0]
    nc = s // chunk
    nv = chunk // 8
    ci = (lambda i: i) if rev else (lambda i: nc - 1 - i)

    def body(u_ref, gy_ref, xb_ref, ts_ref, ta_ref, b_ref, c_ref, gu_ref, ga_ref, gb_ref, gc_ref,
             u_scr, gy_scr, x_scr, g_scr, gcarry, xcarry):
        @pl.when(pl.program_id(1) == 0)
        def _():
            gcarry[...] = jnp.zeros_like(gcarry)
            ga_ref[...] = jnp.zeros_like(ga_ref)
            gb_ref[...] = jnp.zeros_like(gb_ref)
            gc_ref[...] = jnp.zeros_like(gc_ref)

        _permute_rows(u_ref, u_scr, nv)
        _permute_rows(gy_ref, gy_scr, nv)
        ub = u_scr[...].astype(MX)
        gyb = gy_scr[...].astype(MX)
        g_scr[...] = _dot(gyb, c_ref[0], NT)
        _scan_chunk(g_scr, ta_ref.at[0], gcarry, nv, not rev)
        x_scr[...] = _dot(ub, b_ref[0])
        xcarry[...] = xb_ref[0]
        _scan_chunk(x_scr, ts_ref.at[0], xcarry, nv, rev, acc=(g_scr, ga_ref))
        gb16 = g_scr[...].astype(MX)
        gb_ref[0] += _dot(ub, gb16, TN)
        gc_ref[0] += _dot(x_scr[...].astype(MX), gyb, TN)
        _unpermute_rows(_dot(gb16, b_ref[0], NT), gu_ref, nv)

    tile3 = lambda a, b: pl.BlockSpec((1, a, b), lambda j, i: (j, 0, 0))
    return pl.pallas_call(
        body, grid=(SSM_TILES, nc),
        in_specs=[pl.BlockSpec((chunk, TILE_CH), lambda j, i: (ci(i), U_OFF // TILE_CH + j)),
                  pl.BlockSpec((chunk, TILE_CH), lambda j, i: (ci(i), j)),
                  pl.BlockSpec((1, 8, 2 * TILE_ST), lambda j, i: (ci(i), 0, j)),
                  tile3(40 + nv, 2 * TILE_ST), tile3(40 + nv, 2 * TILE_ST),
                  tile3(TILE_CH, 2 * TILE_ST), tile3(2 * TILE_ST, TILE_CH)],
        out_specs=[pl.BlockSpec((chunk, TILE_CH), lambda j, i: (ci(i), j)),
                   pl.BlockSpec((8, 2 * TILE_ST), lambda j, i: (0, j)),
                   tile3(TILE_CH, 2 * TILE_ST), tile3(2 * TILE_ST, TILE_CH)],
        out_shape=[SDS((s, SSM_WIDTH), f32), SDS((8, SSM_TILES * 2 * TILE_ST), f32),
                   SDS((SSM_TILES, TILE_CH, 2 * TILE_ST), f32), SDS((SSM_TILES, 2 * TILE_ST, TILE_CH), f32)],
        scratch_shapes=[pltpu.VMEM((chunk, TILE_CH), f32), pltpu.VMEM((chunk, TILE_CH), f32),
                        pltpu.VMEM((chunk, 2 * TILE_ST), f32), pltpu.VMEM((chunk, 2 * TILE_ST), f32),
                        pltpu.VMEM((8, 2 * TILE_ST), f32), pltpu.VMEM((8, 2 * TILE_ST), f32)],
        compiler_params=_cp("parallel", "arbitrary"), name=name)(z, gy, xb, tab_s, tab_a, bmat, cmat)


GELU_K = math.sqrt(2.0 / math.pi)


def _gelu(y):
    return 0.5 * y * (1.0 + jnp.tanh(GELU_K * (y + 0.044715 * (y * y * y))))


def _gelu_grad(y):
    t = jnp.tanh(GELU_K * (y + 0.044715 * (y * y * y)))
    return 0.5 * (1.0 + t) + 0.5 * y * (1.0 - t * t) * (GELU_K * (1.0 + 3.0 * 0.044715 * (y * y)))


def glu_fwd(y_f, y_r, z, att, d_skip, w_glu, l, *, name, tm):
    s = z.shape[0]
    nblk, cb = w_glu.shape[1], w_glu.shape[3]

    def body(yf_ref, yr_ref, z_ref, att_ref, d_ref, w_ref, y_ref, gg_ref, mix_ref):
        y = d_ref[...] * z_ref[:, U_OFF:IN_WIDTH] + yf_ref[...] + yr_ref[...]
        y_ref[...] = y
        yg = _gelu(y).astype(MX)
        for b in range(nblk):
            gg_ref[:, cb * b:cb * (b + 1)] = _dot(yg, w_ref[0, b])
        mix_ref[:, 0:ATT_WIDTH] = att_ref[...]
        mix_ref[:, ATT_WIDTH:] = (gg_ref[:, :SSM_WIDTH] * jax.nn.sigmoid(gg_ref[:, SSM_WIDTH:])).astype(MX)

    return pl.pallas_call(
        body, grid=(s // tm,),
        in_specs=[_row_spec(tm, SSM_WIDTH), _row_spec(tm, SSM_WIDTH), _row_spec(tm, IN_WIDTH),
                  _row_spec(tm, ATT_WIDTH), pl.BlockSpec((1, SSM_WIDTH), lambda i: (0, 0)), _layer_spec(w_glu, l)],
        out_specs=[_row_spec(tm, SSM_WIDTH), _row_spec(tm, 2 * SSM_WIDTH), _row_spec(tm, D_MODEL)],
        out_shape=[SDS((s, SSM_WIDTH), f32), SDS((s, 2 * SSM_WIDTH), f32), SDS((s, D_MODEL), MX)],
        compiler_params=_cp("parallel"), name=name)(y_f, y_r, z, att, d_skip.reshape(1, SSM_WIDTH), w_glu)


def glu_bwd(gmix, gg, ypre, w_glu, l, *, name, tm):
    s = gg.shape[0]
    nblk, cb = w_glu.shape[1], w_glu.shape[3]

    def body(gm_ref, gg_ref, y_ref, w_ref, ggg_ref, yg_ref, gy_ref):
        gs = gm_ref[...]
        val, gate = gg_ref[:, :SSM_WIDTH], gg_ref[:, SSM_WIDTH:]
        sg = jax.nn.sigmoid(gate)
        ggg_ref[:, :SSM_WIDTH] = (gs * sg).astype(MX)
        ggg_ref[:, SSM_WIDTH:] = (gs * val * sg * (1.0 - sg)).astype(MX)
        y = y_ref[...]
        yg_ref[...] = _gelu(y).astype(MX)
        gyg = _dot(ggg_ref[:, 0:cb], w_ref[0, 0], NT)
        for b in range(1, nblk):
            gyg = gyg + _dot(ggg_ref[:, cb * b:cb * (b + 1)], w_ref[0, b], NT)
        gy_ref[...] = gyg * _gelu_grad(y)

    return pl.pallas_call(
        body, grid=(s // tm,),
        in_specs=[pl.BlockSpec((tm, SSM_WIDTH), lambda i: (i, 1)), _row_spec(tm, 2 * SSM_WIDTH),
                  _row_spec(tm, SSM_WIDTH), _layer_spec(w_glu, l)],
        out_specs=[_row_spec(tm, 2 * SSM_WIDTH), _row_spec(tm, SSM_WIDTH), _row_spec(tm, SSM_WIDTH)],
        out_shape=[SDS((s, 2 * SSM_WIDTH), MX), SDS((s, SSM_WIDTH), MX), SDS((s, SSM_WIDTH), f32)],
        compiler_params=_cp("parallel"), name=name)(gmix, gg, ypre, w_glu)


def loss_grad(y, target, *, name, tm):
    s, d = y.shape

    def body(y_ref, t_ref, g_ref, g16_ref, l_ref):
        @pl.when(pl.program_id(0) == 0)
        def _():
            l_ref[...] = jnp.zeros_like(l_ref)

        e = y_ref[...] - t_ref[...]
        g = e * (1.0 / d)
        g_ref[...] = g
        g16_ref[...] = g.astype(MX)
        l_ref[...] += _rows8(e * e)

    row = pl.BlockSpec((tm, d), lambda i: (i, 0))
    return pl.pallas_call(
        body, grid=(s // tm,), in_specs=[row, row],
        out_specs=[row, row, pl.BlockSpec((8, d), lambda i: (0, 0))],
        out_shape=[SDS((s, d), f32), SDS((s, d), MX), SDS((8, d), f32)],
        compiler_params=_cp("arbitrary"), name=name)(y, target)


def _row_tile(rows, cols):
    tr = rows
    while tr * cols > 256 * 1024 and tr % 16 == 0:
        tr //= 2
    return tr


def _elementwise(fn, ins, n_out, *, name, out_dtype=f32):
    shape = ins[0].shape
    cols = shape[-1]
    ins2 = [a.reshape(-1, cols) for a in ins]
    rows = ins2[0].shape[0]
    tr = _row_tile(rows, cols)

    def body(*refs):
        outs = fn(*[r[...] for r in refs[:len(ins)]])
        for o_ref, o in zip(refs[len(ins):], outs):
            o_ref[...] = o.astype(out_dtype)

    spec = pl.BlockSpec((tr, cols), lambda i: (i, 0))
    outs = pl.pallas_call(
        body, grid=(rows // tr,), in_specs=[spec] * len(ins), out_specs=[spec] * n_out,
        out_shape=[SDS((rows, cols), out_dtype)] * n_out, compiler_params=_cp("parallel"), name=name)(*ins2)
    return [o.reshape(shape) for o in outs]


def _adamw_math(w, g, m, v):
    m = ADAM_B1 * m + (1.0 - ADAM_B1) * g
    v = ADAM_B2 * v + (1.0 - ADAM_B2) * (g * g)
    m_hat = m / (1.0 - ADAM_B1 ** ADAM_STEP)
    v_hat = v / (1.0 - ADAM_B2 ** ADAM_STEP)
    delta = -ADAM_LR * (m_hat / (jnp.sqrt(v_hat) + ADAM_EPS) + ADAM_WD * w)
    return delta, m, v


def adamw(w, g, m, v, *, name):
    return _elementwise(_adamw_math, [w, g, m, v], 3, name=name)


SMEM = pl.BlockSpec(memory_space=pltpu.SMEM)


def _core_index():
    return lax.axis_index("c").astype(jnp.int32).reshape(1)


def adamw_halves(w, own, sib, m, v, *, name):
    depth, r, cols = w.shape
    h = r // 2
    tr = _row_tile(h, cols)
    quad = lambda a: a.reshape(depth, 2, h, cols)

    def body(c_ref, w_ref, own_ref, sib_ref, m_ref, v_ref, g_ref, d_ref, mo_ref, vo_ref):
        g = jnp.where(pl.program_id(1) == c_ref[0], own_ref[0], sib_ref[0])
        g_ref[0, 0] = g
        d_ref[0, 0], mo_ref[0, 0], vo_ref[0, 0] = _adamw_math(w_ref[0, 0], g, m_ref[0, 0], v_ref[0, 0])

    full = pl.BlockSpec((1, 1, tr, cols), lambda l, j, i: (l, j, i, 0))
    part = pl.BlockSpec((1, tr, cols), lambda l, j, i: (l, i, 0))
    outs = pl.pallas_call(
        body, grid=(depth, 2, h // tr), in_specs=[SMEM, full, part, part, full, full], out_specs=[full] * 4,
        out_shape=[SDS((depth, 2, h, cols), f32)] * 4,
        compiler_params=_cp("parallel", "parallel", "parallel"), name=name)(
            _core_index(), quad(w), own, sib, quad(m), quad(v))
    return [o.reshape(depth, r, cols) for o in outs]


def add_own_half(g4, recv, *, name):
    _, _, h, cols = g4.shape
    tr = _row_tile(h, cols)

    def body(c_ref, g_ref, r_ref, o_ref):
        own = jnp.where(c_ref[0] == 0, g_ref[0, 0], g_ref[0, 1])
        o_ref[0] = (own + r_ref[0]).astype(WIRE)

    part = pl.BlockSpec((1, tr, cols), lambda s, i: (s, i, 0))
    return pl.pallas_call(
        body, grid=(4, h // tr),
        in_specs=[SMEM, pl.BlockSpec((1, 2, tr, cols), lambda s, i: (s, 0, i, 0)), part], out_specs=part,
        out_shape=SDS((4, h, cols), WIRE), compiler_params=_cp("parallel", "parallel"), name=name)(
            _core_index(), g4, recv)


def sum4(a, *, name, into=None, layer=0):
    shape = a.shape[1:]
    cols = shape[-1]
    a2 = a.reshape(4, -1, cols)
    rows = a2.shape[1]
    tr = _row_tile(rows, cols)

    def body(*refs):
        a_ref, o_ref = refs[0], refs[-1]
        tot = ((a_ref[0].astype(f32) + a_ref[1].astype(f32)) + a_ref[2].astype(f32)) + a_ref[3].astype(f32)
        if into is None:
            o_ref[...] = tot
        else:
            o_ref[0] = tot

    in_spec = pl.BlockSpec((4, tr, cols), lambda i: (0, i, 0))
    if into is None:
        out = pl.pallas_call(
            body, grid=(rows // tr,), in_specs=[in_spec], out_specs=pl.BlockSpec((tr, cols), lambda i: (i, 0)),
            out_shape=SDS((rows, cols), f32), compiler_params=_cp("parallel"), name=name)(a2)
        return out.reshape(shape)
    stack = into.reshape(into.shape[0], rows, cols)
    out = pl.pallas_call(
        body, grid=(rows // tr,), in_specs=[in_spec, ANY],
        out_specs=pl.BlockSpec((1, tr, cols), lambda i: (layer, i, 0)),
        out_shape=SDS(stack.shape, f32), input_output_aliases={1: 0},
        compiler_params=_cp("parallel"), name=name)(a2, stack)
    return out.reshape(into.shape)


ANY = pl.BlockSpec(memory_space=pl.ANY)


def chip_exchange(arrs, bcast, *, name):
    n = len(arrs)
    piece = [a.shape if b else a.shape[1:] for a, b in zip(arrs, bcast)]

    def body(*refs):
        ins, outs = refs[:n], refs[n:2 * n]
        send, recv, loc = refs[2 * n:]
        x, y, c = lax.axis_index("x"), lax.axis_index("y"), lax.axis_index("c")
        me = 2 * x + y
        copies = []
        for k in range(n):
            own = pltpu.make_async_copy(ins[k] if bcast[k] else ins[k].at[me], outs[k].at[me], loc.at[k])
            own.start()
            copies.append(own)
            for j, (px, py) in enumerate(((1 - x, y), (x, 1 - y), (1 - x, 1 - y))):
                cp = pltpu.make_async_remote_copy(
                    src_ref=ins[k] if bcast[k] else ins[k].at[2 * px + py], dst_ref=outs[k].at[me],
                    send_sem=send.at[3 * k + j], recv_sem=recv.at[3 * k + j],
                    device_id=(px, py, c), device_id_type=MESH)
                cp.start()
                copies.append(cp)
        for cp in copies:
            cp.wait()

    return pl.pallas_call(
        body, in_specs=[ANY] * n, out_specs=[ANY] * n,
        out_shape=[SDS((4,) + tuple(p), a.dtype) for p, a in zip(piece, arrs)],
        scratch_shapes=[pltpu.SemaphoreType.DMA((3 * n,)), pltpu.SemaphoreType.DMA((3 * n,)),
                        pltpu.SemaphoreType.DMA((n,))],
        name=name)(*arrs)


def gather_weights(shards, *, name):
    n = len(shards)
    hd = shards[0].shape[0] // 2

    def body(*refs):
        ins, outs = refs[:n], refs[n:2 * n]
        send, recv = refs[2 * n:]
        x, y, c = lax.axis_index("x"), lax.axis_index("y"), lax.axis_index("c")
        me = 2 * x + y
        chips = ((1 - x, y), (x, 1 - y), (1 - x, 1 - y))
        mine, theirs = pl.ds(c * hd, hd), pl.ds((1 - c) * hd, hd)

        def ici(k, j, src, dst):
            px, py = chips[j]
            return pltpu.make_async_remote_copy(src_ref=src, dst_ref=dst, send_sem=send.at[7 * k + j],
                                                recv_sem=recv.at[7 * k + j], device_id=(px, py, c),
                                                device_id_type=MESH)

        def d2d(k, j, src, dst):
            return pltpu.make_async_remote_copy(src_ref=src, dst_ref=dst, send_sem=send.at[7 * k + 3 + j],
                                                recv_sem=recv.at[7 * k + 3 + j], device_id=(x, y, 1 - c),
                                                device_id_type=MESH)

        own, sent = [], []
        for k in range(n):
            own.append(d2d(k, 3, ins[k], outs[k].at[:, me]))
            own[-1].start()
            for j in range(3):
                sent.append(ici(k, j, ins[k].at[mine], outs[k].at[mine, me]))
                sent[-1].start()
        for k in range(n):
            for j, (px, py) in enumerate(chips):
                landed = outs[k].at[mine, 2 * px + py]
                ici(k, j, landed, landed).wait_recv()
                sent.append(d2d(k, j, landed, landed))
                sent[-1].start()
        for k in range(n):
            for j, (px, py) in enumerate(chips):
                other = outs[k].at[theirs, 2 * px + py]
                d2d(k, j, other, other).wait_recv()
        for cp in sent:
            cp.wait_send()
        for cp in own:
            cp.wait()

    return pl.pallas_call(
        body, in_specs=[ANY] * n, out_specs=[ANY] * n,
        out_shape=[SDS((a.shape[0], 4) + tuple(a.shape[1:]), a.dtype) for a in shards],
        scratch_shapes=[pltpu.SemaphoreType.DMA((7 * n,)), pltpu.SemaphoreType.DMA((7 * n,))],
        name=name)(*shards)


def sibling_exchange(arrs, half, *, name):
    n = len(arrs)
    piece = [(a.shape[0],) + a.shape[2:] if h else a.shape for a, h in zip(arrs, half)]

    def body(*refs):
        ins, outs = refs[:n], refs[n:2 * n]
        send, recv = refs[2 * n:]
        x, y, c = lax.axis_index("x"), lax.axis_index("y"), lax.axis_index("c")
        copies = []
        for k in range(n):
            cp = pltpu.make_async_remote_copy(
                src_ref=ins[k].at[:, 1 - c] if half[k] else ins[k], dst_ref=outs[k],
                send_sem=send.at[k], recv_sem=recv.at[k], device_id=(x, y, 1 - c), device_id_type=MESH)
            cp.start()
            copies.append(cp)
        for cp in copies:
            cp.wait()

    return pl.pallas_call(
        body, in_specs=[ANY] * n, out_specs=[ANY] * n,
        out_shape=[SDS(tuple(p), a.dtype) for p, a in zip(piece, arrs)],
        scratch_shapes=[pltpu.SemaphoreType.DMA((n,)), pltpu.SemaphoreType.DMA((n,))],
        name=name)(*arrs)


def ssm_discretize(lam_re, lam_im, log_dt, b_re, b_im, c_re, c_im):
    dt = jnp.exp(log_dt)[..., None]
    mag = jnp.exp(lam_re * dt)
    abr = mag * jnp.cos(lam_im * dt)
    abi = mag * jnp.sin(lam_im * dt)
    den = lam_re * lam_re + lam_im * lam_im
    zr = ((abr - 1.0) * lam_re + abi * lam_im) / den
    zi = (abi * lam_re - (abr - 1.0) * lam_im) / den
    bbr = zr[..., None] * b_re - zi[..., None] * b_im
    bbi = zr[..., None] * b_im + zi[..., None] * b_re
    eye = jnp.eye(8, dtype=f32)
    bb = jnp.stack([bbr, bbi], axis=1).reshape(2, 2, SSM_TILES, 8, SSM_STATE, SSM_GROUP)
    bmat = jnp.einsum('dqjgph,gk->djghqkp', bb, eye).reshape(2, SSM_TILES, TILE_CH, 2 * TILE_ST)
    cc = jnp.stack([c_re, -c_im], axis=1).reshape(2, 2, SSM_TILES, 8, SSM_GROUP, SSM_STATE)
    cmat = jnp.einsum('dqjghp,gk->djqkpgh', cc, eye).reshape(2, SSM_TILES, 2 * TILE_ST, TILE_CH)
    n = SSM_GROUPS * SSM_STATE
    return abr.reshape(2, n), abi.reshape(2, n), bmat, cmat


def scan_tables(ar, ai, rev, nv):
    pw = [(ar, ai)]
    for _ in range(nv - 1):
        pw.append(_cmul(ar, ai, *pw[-1]))
    big = [pw[nv - 1]]
    big.append(_cmul(*big[0], *big[0]))
    big.append(_cmul(*big[1], *big[1]))
    rows = jnp.arange(8)[:, None]
    ones = jnp.ones((8, 1), f32)
    parts = []
    for k, p in zip((1, 2, 4), big):
        cond = (rows <= 7 - k) if rev else (rows >= k)
        parts.append([jnp.where(cond, q[None, :], 0.0) for q in p])
    parts.append([ones * q[None, :] for q in big[0]])
    parts.append([ones * q[None, :] for q in pw[0]])
    for v in range(nv):
        parts.append([q[None, :] for q in pw[nv - 1 - v if rev else v]])
    nrow = 40 + nv
    tre = jnp.concatenate([p[0] for p in parts], axis=0).reshape(nrow, SSM_TILES, TILE_ST)
    tim = jnp.concatenate([p[1] for p in parts], axis=0).reshape(nrow, SSM_TILES, TILE_ST)
    return jnp.concatenate([tre, tim], axis=-1).transpose(1, 0, 2)


def _tile_a(ga):
    t = ga.sum(axis=0).reshape(SSM_TILES, 2, TILE_ST)
    return t[:, 0].reshape(-1), t[:, 1].reshape(-1)


SMALL = ('norm1', 'q_gain', 'k_gain', 'sink', 'lam_re', 'lam_im', 'log_dt', 'b_re', 'b_im', 'c_re', 'c_im',
         'd_skip', 'norm2')
BIG = ('w_in', 'w_glu', 'w_out', 'w_ff1', 'w_ff2')
WEIGHTS = ('norm1', 'w_in', 'q_gain', 'k_gain', 'sink', 'lam_re', 'lam_im', 'log_dt', 'b_re', 'b_im', 'c_re',
           'c_im', 'd_skip', 'w_glu', 'w_out', 'norm2', 'w_ff1', 'w_ff2')


def _chunk(s):
    return min(256, s)


def layer_forward(l, x, p, wb):
    s = x.shape[0]
    tm = min(512, s)
    sv = {}
    h1, z = norm_mm(x, p['norm1'], wb['w_in'], l, relu2=False, name=f"l{l}_in", tm=tm)
    eq, ek = head_mean_matrix(ATT_WIDTH), head_mean_matrix(KV_WIDTH)
    qn, kv = qk_prep(z, p['q_gain'], p['k_gain'], eq, ek, name=f"l{l}_qk", tm=tm)
    att = attn_fwd(qn, kv, p['sink'], name=f"l{l}_attn")
    sv.update(qn=qn, kv=kv, eq=eq, ek=ek)
    (ar, ai, bmat, cmat), disc_vjp = jax.vjp(
        ssm_discretize, p['lam_re'], p['lam_im'], p['log_dt'], p['b_re'], p['b_im'], p['c_re'], p['c_im'])
    bmat16, cmat16 = bmat.astype(MX), cmat.astype(MX)
    ys, xbs, tabs = [], [], []
    for d, rev in enumerate((False, True)):
        tab = scan_tables(ar[d], ai[d], rev, _chunk(s) // 8)
        y_d, xb_d = ssm_fwd(z, tab, bmat16[d], cmat16[d], rev=rev, name=f"l{l}_ssm{d}", chunk=_chunk(s))
        ys.append(y_d)
        xbs.append(xb_d)
        tabs.append((tab, scan_tables(ar[d], -ai[d], not rev, _chunk(s) // 8)))
    ypre, gg, mix = glu_fwd(ys[0], ys[1], z, att, p['d_skip'], wb['w_glu'], l, name=f"l{l}_glu", tm=min(256, s))
    x1 = mm_res(mix, wb['w_out'], l, x, name=f"l{l}_out", tm=tm)
    h2, a2 = norm_mm(x1, p['norm2'], wb['w_ff1'], l, relu2=True, name=f"l{l}_ff1", tm=tm)
    x2 = mm_res(a2, wb['w_ff2'], l, x1, name=f"l{l}_ff2", tm=tm)
    sv.update(x=x, h1=h1, z=z, xbs=xbs, tabs=tabs, bmat16=bmat16, cmat16=cmat16, disc_vjp=disc_vjp,
              ypre=ypre, gg=gg, mix=mix, x1=x1, h2=h2, a2=a2)
    return x2, sv


def layer_backward(l, gx2, gx2h, p, wb, sv):
    s = gx2.shape[0]
    tm = min(512, s)
    ts = min(1024, s)
    g = {}
    gf = mm_nt(gx2h, wb['w_ff2'], l, name=f"l{l}_bff2", tm=tm, a2=sv['a2'])
    g['w_ff2'] = mm_tn(sv['a2'], gx2h, name=f"l{l}_wff2", tk=1024, tn=1024, ts=ts).reshape(4, D_FF // 4, D_MODEL)
    gx1, gx1h, gn2 = mm_nt_norm(gf, wb['w_ff1'], l, sv['x1'], p['norm2'], gx2, name=f"l{l}_bff1", tm=min(256, s))
    g['norm2'] = gn2.sum(axis=0)
    g['w_ff1'] = mm_tn(sv['h2'], gf, name=f"l{l}_wff1", tk=1024, tn=1024, ts=ts, chip_major=True)
    gmix = mm_nt(gx1h, wb['w_out'], l, name=f"l{l}_bout", tm=tm)
    g['w_out'] = mm_tn(sv['mix'], gx1h, name=f"l{l}_wout", tk=1024, tn=1024, ts=ts).reshape(4, D_MODEL // 4, D_MODEL)
    ggg, yg, gy = glu_bwd(gmix, sv['gg'], sv['ypre'], wb['w_glu'], l, name=f"l{l}_bglu", tm=min(256, s))
    g['w_glu'] = mm_tn(yg, ggg, name=f"l{l}_wglu", tk=512, tn=256, ts=ts, chip_major=True)
    gus, gas, gbs, gcs = [], [], [], []
    for d, rev in enumerate((False, True)):
        tab_s, tab_a = sv['tabs'][d]
        gu_d, ga_d, gb_d, gc_d = ssm_bwd(sv['z'], gy, sv['xbs'][d], tab_s, tab_a, sv['bmat16'][d], sv['cmat16'][d],
                                         rev=rev, name=f"l{l}_bssm{d}", chunk=_chunk(s))
        gus.append(gu_d)
        gas.append(_tile_a(ga_d))
        gbs.append(gb_d)
        gcs.append(gc_d)
    gar = jnp.stack([gas[0][0], gas[1][0]])
    gai = jnp.stack([gas[0][1], gas[1][1]])
    (g['lam_re'], g['lam_im'], g['log_dt'], g['b_re'], g['b_im'], g['c_re'], g['c_im']) = sv['disc_vjp'](
        (gar, gai, jnp.stack(gbs), jnp.stack(gcs)))
    gqs, dkv, gsk = attn_bwd(sv['qn'], sv['kv'], gmix, p['sink'], name=f"l{l}_battn")
    g['sink'] = gsk[:, 0]
    gz, gqg, gkg, gd = gz_assemble(gqs, dkv, sv['z'], p['q_gain'], p['k_gain'], sv['eq'], sv['ek'], gus[0], gus[1],
                                   gy, p['d_skip'], name=f"l{l}_gz")
    g['q_gain'] = gqg.sum(axis=0).reshape(ATT_HEADS, HEAD_DIM).sum(axis=0)
    g['k_gain'] = gkg.sum(axis=0).reshape(KV_HEADS, HEAD_DIM).sum(axis=0)
    g['d_skip'] = gd.sum(axis=0)
    gx, gxh, gn1 = mm_nt_norm(gz, wb['w_in'], l, sv['x'], p['norm1'], gx1, name=f"l{l}_bin", tm=tm)
    g['norm1'] = gn1.sum(axis=0)
    gw_in = mm_tn(sv['h1'], gz, name=f"l{l}_win", tk=1024, tn=640, ts=ts)
    g['w_in'] = gw_in.reshape(D_MODEL, 4, IN_WIDTH // 4).transpose(1, 0, 2)
    return gx, gxh, g


def stack_layouts(gathered):
    w_in = gathered['w_in']
    depth = w_in.shape[0]
    return dict(w_in=w_in.transpose(0, 2, 1, 3).reshape(depth, D_MODEL, IN_WIDTH),
                w_glu=gathered['w_glu'], w_ff1=gathered['w_ff1'],
                w_out=gathered['w_out'].reshape(depth, D_MODEL, D_MODEL),
                w_ff2=gathered['w_ff2'].reshape(depth, D_FF, D_MODEL))


def local_step(x, target, small, wb, after_layer=None):
    depth = wb['w_in'].shape[0]
    saves = []
    for l in range(depth):
        p = {k: small[k][l] for k in SMALL}
        x, sv = layer_forward(l, x, p, wb)
        saves.append(sv)
    gx, gxh, lparts = loss_grad(x, target, name="loss", tm=min(512, x.shape[0]))
    grads = [None] * depth
    for l in reversed(range(depth)):
        p = {k: small[k][l] for k in SMALL}
        gx, gxh, g = layer_backward(l, gx, gxh, p, wb, saves[l])
        grads[l] = g if after_layer is None else after_layer(l, g)
    return lparts, gx, grads


def reduce_layer(l, g, stacks):
    arrs = [g[k].reshape(4, 2, g[k].shape[1] // 2, g[k].shape[2]) for k in BIG]
    got = sibling_exchange(arrs, [True] * len(BIG), name=f"l{l}_rsib")
    sums = [add_own_half(a, b, name=f"l{l}_radd_{k}") for k, a, b in zip(BIG, arrs, got)]
    got = chip_exchange(sums, [False] * len(BIG), name=f"l{l}_rchips")
    return {k: sum4(a, name=f"l{l}_rsum_{k}", into=stacks[k], layer=l) for k, a in zip(BIG, got)}


def reduce_small(packed):
    got = sibling_exchange([packed], [False], name="small_rsib")
    pair = _elementwise(lambda a, b: (a + b,), [packed, got[0]], 1, name="small_radd")[0]
    got = chip_exchange([pair], [True], name="small_rchips")
    return sum4(got[0], name="small_rsum")


def _pack_small(tree):
    parts = []
    for k in SMALL:
        flat = tree[k].reshape(-1)
        parts.append(jnp.pad(flat, (0, (-flat.shape[0]) % 1024)).reshape(-1, 128))
    return jnp.concatenate(parts, axis=0)


def _unpack_small(packed, like):
    out, row = {}, 0
    for k in SMALL:
        n = like[k].size
        rows = -(-n // 1024) * 8
        out[k] = packed[row:row + rows].reshape(-1)[:n].reshape(like[k].shape)
        row += rows
    return out


def kernel(x, norm1, w_in, q_gain, k_gain, sink, lam_re, lam_im, log_dt, b_re, b_im, c_re, c_im, d_skip, w_glu, w_out, norm2, w_ff1, w_ff2, loss_target, m_norm1, m_w_in, m_q_gain, m_k_gain, m_sink, m_lam_re, m_lam_im, m_log_dt, m_b_re, m_b_im, m_c_re, m_c_im, m_d_skip, m_w_glu, m_w_out, m_norm2, m_w_ff1, m_w_ff2, v_norm1, v_w_in, v_q_gain, v_k_gain, v_sink, v_lam_re, v_lam_im, v_log_dt, v_b_re, v_b_im, v_c_re, v_c_im, v_d_skip, v_w_glu, v_w_out, v_norm2, v_w_ff1, v_w_ff2):
    w = dict(norm1=norm1, w_in=w_in, q_gain=q_gain, k_gain=k_gain, sink=sink, lam_re=lam_re, lam_im=lam_im,
             log_dt=log_dt, b_re=b_re, b_im=b_im, c_re=c_re, c_im=c_im, d_skip=d_skip, w_glu=w_glu, w_out=w_out,
             norm2=norm2, w_ff1=w_ff1, w_ff2=w_ff2)
    m = dict(norm1=m_norm1, w_in=m_w_in, q_gain=m_q_gain, k_gain=m_k_gain, sink=m_sink, lam_re=m_lam_re,
             lam_im=m_lam_im, log_dt=m_log_dt, b_re=m_b_re, b_im=m_b_im, c_re=m_c_re, c_im=m_c_im,
             d_skip=m_d_skip, w_glu=m_w_glu, w_out=m_w_out, norm2=m_norm2, w_ff1=m_w_ff1, w_ff2=m_w_ff2)
    v = dict(norm1=v_norm1, w_in=v_w_in, q_gain=v_q_gain, k_gain=v_k_gain, sink=v_sink, lam_re=v_lam_re,
             lam_im=v_lam_im, log_dt=v_log_dt, b_re=v_b_re, b_im=v_b_im, c_re=v_c_re, c_im=v_c_im,
             d_skip=v_d_skip, w_glu=v_w_glu, w_out=v_w_out, norm2=v_norm2, w_ff1=v_w_ff1, w_ff2=v_w_ff2)
    depth = w_in.shape[0]

    gathered = gather_weights([w[k].astype(WIRE) for k in BIG], name="gather_w")
    wb = stack_layouts(dict(zip(BIG, gathered)))
    small = {k: w[k] for k in SMALL}

    stacks = [{k: jnp.zeros((depth, w[k].shape[1] // 2, w[k].shape[2]), f32) for k in BIG}]

    def after_layer(l, g):
        stacks[0] = reduce_layer(l, g, stacks[0])
        return {k: g[k] for k in SMALL}

    lparts, gx, grads = local_step(x[0], loss_target[0], small, wb, after_layer)
    loss = lax.psum(0.5 * jnp.sum(lparts) / D_MODEL, ("x", "y", "c"))

    sib = sibling_exchange([stacks[0][k] for k in BIG], [False] * len(BIG), name="reduce_back")
    gsmall = reduce_small(_pack_small({k: jnp.stack([grads[l][k] for l in range(depth)]) for k in SMALL}))
    like = {k: w[k] for k in SMALL}
    gfull = _unpack_small(gsmall, like)

    delta, new_m, new_v = {}, {}, {}
    for k, sib_k in zip(BIG, sib):
        gfull[k], delta[k], new_m[k], new_v[k] = adamw_halves(w[k], stacks[0][k], sib_k, m[k], v[k],
                                                              name=f"adamw_{k}")
    ds, ms, vs = adamw(_pack_small(like), gsmall, _pack_small({k: m[k] for k in SMALL}),
                       _pack_small({k: v[k] for k in SMALL}), name="adamw_small")
    delta.update(_unpack_small(ds, like))
    new_m.update(_unpack_small(ms, like))
    new_v.update(_unpack_small(vs, like))

    return (loss, gx[None], *[gfull[k] for k in WEIGHTS], *[delta[k] for k in WEIGHTS],
            *[new_m[k] for k in WEIGHTS], *[new_v[k] for k in WEIGHTS])
```

```python
import functools
import math

import jax
import jax.numpy as jnp
from jax import lax
from jax.experimental import pallas as pl
from jax.experimental.pallas import tpu as pltpu

f32 = jnp.float32
MX = jnp.bfloat16
WIRE = jnp.bfloat16
SDS = jax.ShapeDtypeStruct

D_MODEL = 1024
DEPTH = 4
ATT_HEADS = 8
KV_HEADS = 2
GQA = ATT_HEADS // KV_HEADS
HEAD_DIM = 64
ATT_WIDTH = ATT_HEADS * HEAD_DIM
KV_WIDTH = KV_HEADS * HEAD_DIM
BLOCK = 128
SSM_WIDTH = 512
SSM_GROUP = 16
SSM_GROUPS = 32
SSM_STATE = 64
SSM_TILES = 4
TILE_CH = SSM_WIDTH // SSM_TILES
TILE_ST = SSM_GROUPS * SSM_STATE // SSM_TILES
TILES_PER_STEP = 2
IN_WIDTH = ATT_WIDTH + 2 * KV_WIDTH + SSM_WIDTH
U_OFF = ATT_WIDTH + 2 * KV_WIDTH
D_FF = 4096
EPS = 1e-6
NEG = float(jnp.finfo(jnp.float32).min)
SLOPES = tuple(2.0 ** (-8.0 * (h + 1) / ATT_HEADS) for h in range(ATT_HEADS))

ADAM_LR, ADAM_B1, ADAM_B2, ADAM_EPS, ADAM_WD, ADAM_STEP = 0.001, 0.9, 0.999, 1e-08, 0.01, 10

VMEM_LIMIT = 48 * 1024 * 1024
MESH = pl.DeviceIdType.MESH

NT = (((1,), (1,)), ((), ()))
TN = (((0,), (0,)), ((), ()))


def _cp(*sem):
    return pltpu.CompilerParams(dimension_semantics=sem, vmem_limit_bytes=VMEM_LIMIT)


def _dot(a, b, dims=None):
    if dims is None:
        return jnp.dot(a, b, preferred_element_type=f32)
    return lax.dot_general(a, b, dims, preferred_element_type=f32)


def _rows8(v):
    return v.reshape(v.shape[0] // 8, 8, v.shape[1]).sum(axis=0)


def _layer_spec(w, l):
    nd = w.ndim
    return pl.BlockSpec((1,) + tuple(w.shape[1:]), lambda i: (l,) + (0,) * (nd - 1))


def _row_spec(tm, width):
    return pl.BlockSpec((tm, width), lambda i: (i, 0))


def norm_mm(x, gain, w, l, *, relu2, name, tm):
    s, d = x.shape
    if relu2:
        nblk, cb = w.shape[1], w.shape[3]
        n = nblk * cb
    else:
        n = w.shape[2]

    def body(x_ref, g_ref, w_ref, h_ref, y_ref):
        xf = x_ref[...]
        r = lax.rsqrt(jnp.mean(xf * xf, axis=-1, keepdims=True) + EPS)
        h = (xf * r * g_ref[...]).astype(MX)
        h_ref[...] = h
        if relu2:
            for b in range(nblk):
                f = jnp.maximum(_dot(h, w_ref[0, b]), 0.0)
                y_ref[:, cb * b:cb * (b + 1)] = (f * f).astype(MX)
        else:
            y_ref[...] = _dot(h, w_ref[0])

    return pl.pallas_call(
        body, grid=(s // tm,),
        in_specs=[_row_spec(tm, d), pl.BlockSpec((1, d), lambda i: (0, 0)), _layer_spec(w, l)],
        out_specs=[_row_spec(tm, d), _row_spec(tm, n)],
        out_shape=[SDS((s, d), MX), SDS((s, n), MX if relu2 else f32)],
        compiler_params=_cp("parallel"), name=name)(x, gain.reshape(1, d), w)


def mm_res(a, w, l, res, *, name, tm):
    s, k = a.shape
    n = w.shape[2]

    def body(a_ref, w_ref, r_ref, o_ref):
        o_ref[...] = r_ref[...] + _dot(a_ref[...], w_ref[0])

    return pl.pallas_call(
        body, grid=(s // tm,), in_specs=[_row_spec(tm, k), _layer_spec(w, l), _row_spec(tm, n)],
        out_specs=_row_spec(tm, n), out_shape=SDS((s, n), f32), compiler_params=_cp("parallel"), name=name)(a, w, res)


def mm_nt(gy, w, l, *, name, tm, a2=None):
    s, n = gy.shape
    k = w.shape[1]
    kb = min(k, 1024)

    def body(*refs):
        if a2 is None:
            g_ref, w_ref, o_ref = refs
        else:
            g_ref, w_ref, a_ref, o_ref = refs
        g = g_ref[...]
        for b in range(k // kb):
            cols = slice(kb * b, kb * (b + 1))
            acc = _dot(g, w_ref[0, cols, :], NT)
            if a2 is not None:
                acc = acc * (2.0 * jnp.sqrt(a_ref[:, cols].astype(f32)))
            o_ref[:, cols] = acc.astype(o_ref.dtype)

    in_specs = [_row_spec(tm, n), _layer_spec(w, l)]
    args = [gy, w]
    if a2 is not None:
        in_specs.append(_row_spec(tm, k))
        args.append(a2)
    return pl.pallas_call(
        body, grid=(s // tm,), in_specs=in_specs, out_specs=_row_spec(tm, k),
        out_shape=SDS((s, k), f32 if a2 is None else MX), compiler_params=_cp("parallel"), name=name)(*args)


def mm_nt_norm(gy, w, l, x, gain, res, *, name, tm):
    s, n = gy.shape
    d = x.shape[1]

    def body(g_ref, w_ref, x_ref, gn_ref, r_ref, o_ref, o16_ref, gg_ref):
        @pl.when(pl.program_id(0) == 0)
        def _():
            gg_ref[...] = jnp.zeros_like(gg_ref)

        if w.ndim == 3:
            gh = _dot(g_ref[...], w_ref[0], NT)
        else:
            cb = w.shape[3]
            gh = _dot(g_ref[:, 0:cb], w_ref[0, 0], NT)
            for b in range(1, w.shape[1]):
                gh = gh + _dot(g_ref[:, cb * b:cb * (b + 1)], w_ref[0, b], NT)
        xf = x_ref[...]
        r = lax.rsqrt(jnp.mean(xf * xf, axis=-1, keepdims=True) + EPS)
        xh = xf * r
        t = gh * gn_ref[...]
        gx = r_ref[...] + r * (t - xh * jnp.mean(t * xh, axis=-1, keepdims=True))
        o_ref[...] = gx
        o16_ref[...] = gx.astype(MX)
        gg_ref[...] += _rows8(gh * xh)

    return pl.pallas_call(
        body, grid=(s // tm,),
        in_specs=[_row_spec(tm, n), _layer_spec(w, l), _row_spec(tm, d), pl.BlockSpec((1, d), lambda i: (0, 0)),
                  _row_spec(tm, d)],
        out_specs=[_row_spec(tm, d), _row_spec(tm, d), pl.BlockSpec((8, d), lambda i: (0, 0))],
        out_shape=[SDS((s, d), f32), SDS((s, d), MX), SDS((8, d), f32)],
        compiler_params=_cp("arbitrary"), name=name)(gy, w, x, gain.reshape(1, d), res)


def mm_tn(xa, gy, *, name, tk, tn, ts, chip_major=False):
    s, k = xa.shape
    n = gy.shape[1]

    def body(x_ref, g_ref, o_ref):
        @pl.when(pl.program_id(2) == 0)
        def _():
            o_ref[...] = jnp.zeros_like(o_ref)

        acc = _dot(x_ref[...], g_ref[...], TN)
        if chip_major:
            o_ref[0] += acc
        else:
            o_ref[...] += acc

    if chip_major:
        out_spec = pl.BlockSpec((1, tk, tn), lambda a, b, c: (b, a, 0))
        out_shape = SDS((n // tn, k, tn), f32)
    else:
        out_spec = pl.BlockSpec((tk, tn), lambda a, b, c: (a, b))
        out_shape = SDS((k, n), f32)
    return pl.pallas_call(
        body, grid=(k // tk, n // tn, s // ts),
        in_specs=[pl.BlockSpec((ts, tk), lambda a, b, c: (c, a)), pl.BlockSpec((ts, tn), lambda a, b, c: (c, b))],
        out_specs=out_spec, out_shape=out_shape,
        compiler_params=_cp("parallel", "parallel", "arbitrary"), name=name)(xa, gy)


def head_mean_matrix(width):
    return jnp.kron(jnp.eye(width // HEAD_DIM, dtype=f32), jnp.full((HEAD_DIM, HEAD_DIM), 1.0 / HEAD_DIM, f32)).astype(MX)


def _head_mean(t, e_ref):
    hi = t.astype(MX)
    lo = (t - hi.astype(f32)).astype(MX)
    return _dot(hi, e_ref[...]) + _dot(lo, e_ref[...])


def qk_prep(z, q_gain, k_gain, eq, ek, *, name, tm):
    s = z.shape[0]

    def body(z_ref, qg_ref, kg_ref, eq_ref, ek_ref, q_ref, kv_ref):
        q = z_ref[:, 0:ATT_WIDTH]
        r = lax.rsqrt(_head_mean(q * q, eq_ref) + EPS)
        q_ref[...] = ((q * r * qg_ref[...]) * 0.125).astype(MX)
        k = z_ref[:, ATT_WIDTH:ATT_WIDTH + KV_WIDTH]
        r = lax.rsqrt(_head_mean(k * k, ek_ref) + EPS)
        kv_ref[:, 0:KV_WIDTH] = (k * r * kg_ref[...]).astype(MX)
        kv_ref[:, KV_WIDTH:] = z_ref[:, ATT_WIDTH + KV_WIDTH:U_OFF].astype(MX)

    const = lambda a: pl.BlockSpec(a.shape, lambda i: (0, 0))
    qg = jnp.tile(q_gain.reshape(1, HEAD_DIM), (1, ATT_HEADS))
    kg = jnp.tile(k_gain.reshape(1, HEAD_DIM), (1, KV_HEADS))
    return pl.pallas_call(
        body, grid=(s // tm,), in_specs=[_row_spec(tm, IN_WIDTH), const(qg), const(kg), const(eq), const(ek)],
        out_specs=[_row_spec(tm, ATT_WIDTH), _row_spec(tm, 2 * KV_WIDTH)],
        out_shape=[SDS((s, ATT_WIDTH), MX), SDS((s, 2 * KV_WIDTH), MX)],
        compiler_params=_cp("parallel"), name=name)(z, qg, kg, eq, ek)


def _attn_mask(i, nb):
    row = lax.broadcasted_iota(jnp.int32, (GQA * BLOCK, 3 * BLOCK), 0) & (BLOCK - 1)
    col = lax.broadcasted_iota(jnp.int32, (GQA * BLOCK, 3 * BLOCK), 1)
    dist = jnp.abs(row - col + BLOCK)
    valid = (dist <= BLOCK) & ((col >= BLOCK) | (i >= 1)) & ((col < 2 * BLOCK) | (i <= nb - 2))
    return dist.astype(f32), valid


def _attn_specs(nb):
    return [pl.BlockSpec((BLOCK, ATT_WIDTH), lambda i: (i, 0)),
            pl.BlockSpec((BLOCK, 2 * KV_WIDTH), lambda i: (jnp.maximum(i - 1, 0), 0)),
            pl.BlockSpec((BLOCK, 2 * KV_WIDTH), lambda i: (i, 0)),
            pl.BlockSpec((BLOCK, 2 * KV_WIDTH), lambda i: (jnp.minimum(i + 1, nb - 1), 0)),
            pl.BlockSpec(memory_space=pltpu.SMEM)]


def _attn_probs(sc, kvh, distf, valid, sink_ref):
    row = lax.broadcasted_iota(jnp.int32, (GQA * BLOCK, 1), 0)
    slope = jnp.full((GQA * BLOCK, 1), SLOPES[GQA * kvh], f32)
    sk = jnp.full((GQA * BLOCK, 1), sink_ref[GQA * kvh], f32)
    for j in range(1, GQA):
        slope = jnp.where(row >= BLOCK * j, SLOPES[GQA * kvh + j], slope)
        sk = jnp.where(row >= BLOCK * j, sink_ref[GQA * kvh + j], sk)
    sg = jnp.where(valid, sc - slope * distf, NEG)
    m = jnp.maximum(jnp.max(sg, axis=-1, keepdims=True), sk)
    e = jnp.exp(sg - m)
    es = jnp.exp(sk - m)
    inv = 1.0 / (jnp.sum(e, axis=-1, keepdims=True) + es)
    return e * inv, es * inv


def _stack_heads(ref, kvh):
    return jnp.concatenate([ref[:, HEAD_DIM * (GQA * kvh + g):HEAD_DIM * (GQA * kvh + g + 1)] for g in range(GQA)],
                           axis=0)


def attn_fwd(qn, kv, sink, *, name):
    s = qn.shape[0]
    nb = s // BLOCK

    def body(q_ref, kp_ref, kc_ref, kn_ref, sink_ref, o_ref):
        i = pl.program_id(0)
        distf, valid = _attn_mask(i, nb)
        kv3 = jnp.concatenate([kp_ref[...], kc_ref[...], kn_ref[...]], axis=0)
        for kvh in range(KV_HEADS):
            kn = kv3[:, HEAD_DIM * kvh:HEAD_DIM * (kvh + 1)]
            vh = kv3[:, KV_WIDTH + HEAD_DIM * kvh:KV_WIDTH + HEAD_DIM * (kvh + 1)]
            sc = _dot(_stack_heads(q_ref, kvh), kn, NT)
            p, _ = _attn_probs(sc, kvh, distf, valid, sink_ref)
            o = _dot(p.astype(MX), vh)
            for g in range(GQA):
                h = GQA * kvh + g
                o_ref[:, HEAD_DIM * h:HEAD_DIM * (h + 1)] = o[BLOCK * g:BLOCK * (g + 1)].astype(o_ref.dtype)

    return pl.pallas_call(
        body, grid=(nb,), in_specs=_attn_specs(nb),
        out_specs=pl.BlockSpec((BLOCK, ATT_WIDTH), lambda i: (i, 0)),
        out_shape=SDS((s, ATT_WIDTH), MX), compiler_params=_cp("parallel"), name=name)(qn, kv, kv, kv, sink)


def attn_bwd(qn, kv, gmix, sink, *, name):
    s = qn.shape[0]
    nb = s // BLOCK

    def body(q_ref, kp_ref, kc_ref, kn_ref, sink_ref, go_ref, gq_ref, dkv_ref, gs_ref):
        i = pl.program_id(0)

        @pl.when(i == 0)
        def _():
            gs_ref[...] = jnp.zeros_like(gs_ref)

        distf, valid = _attn_mask(i, nb)
        kv3 = jnp.concatenate([kp_ref[...], kc_ref[...], kn_ref[...]], axis=0)
        for kvh in range(KV_HEADS):
            kn = kv3[:, HEAD_DIM * kvh:HEAD_DIM * (kvh + 1)]
            vh = kv3[:, KV_WIDTH + HEAD_DIM * kvh:KV_WIDTH + HEAD_DIM * (kvh + 1)]
            qs = _stack_heads(q_ref, kvh)
            dos = _stack_heads(go_ref, kvh).astype(MX)
            p, psink = _attn_probs(_dot(qs, kn, NT), kvh, distf, valid, sink_ref)
            dp = _dot(dos, vh, NT)
            delta = jnp.sum(p * dp, axis=-1, keepdims=True)
            gsk = psink * delta
            for g in range(GQA):
                h = GQA * kvh + g
                gs_ref[h:h + 1, :] -= jnp.broadcast_to(
                    jnp.sum(gsk[BLOCK * g:BLOCK * (g + 1)], axis=0, keepdims=True), (1, 128))
            ds = (p * (dp - delta)).astype(MX)
            gv = _dot(p.astype(MX), dos, TN)
            gkn = _dot(ds, qs, TN)
            gqs = _dot(ds, kn)
            for g in range(GQA):
                h = GQA * kvh + g
                gq_ref[:, HEAD_DIM * h:HEAD_DIM * (h + 1)] = gqs[BLOCK * g:BLOCK * (g + 1)]
            for b in range(3):
                dkv_ref[b, :, HEAD_DIM * kvh:HEAD_DIM * (kvh + 1)] = gkn[BLOCK * b:BLOCK * (b + 1)]
                dkv_ref[b, :, KV_WIDTH + HEAD_DIM * kvh:KV_WIDTH + HEAD_DIM * (kvh + 1)] = gv[BLOCK * b:BLOCK * (b + 1)]

    return pl.pallas_call(
        body, grid=(nb,),
        in_specs=_attn_specs(nb) + [pl.BlockSpec((BLOCK, ATT_WIDTH), lambda i: (i, 0))],
        out_specs=[pl.BlockSpec((BLOCK, ATT_WIDTH), lambda i: (i, 0)),
                   pl.BlockSpec((3, BLOCK, 2 * KV_WIDTH), lambda i: (0, i, 0)),
                   pl.BlockSpec((ATT_HEADS, 128), lambda i: (0, 0))],
        out_shape=[SDS((s, ATT_WIDTH), f32), SDS((3, s, 2 * KV_WIDTH), f32), SDS((ATT_HEADS, 128), f32)],
        compiler_params=_cp("arbitrary"), name=name)(qn, kv, kv, kv, sink, gmix)


def gz_assemble(gqs, dkv, z, q_gain, k_gain, eq, ek, gu_f, gu_r, gy, d_skip, *, name):
    s = z.shape[0]
    nb = s // BLOCK

    def norm_bwd(t_in, g_out, gain_ref, e_ref):
        r = lax.rsqrt(_head_mean(t_in * t_in, e_ref) + EPS)
        hat = t_in * r
        t = g_out * gain_ref[...]
        return r * (t - hat * _head_mean(t * hat, e_ref)), g_out * hat

    def body(gq_ref, d0_ref, d1_ref, d2_ref, z_ref, qg_ref, kg_ref, eq_ref, ek_ref, guf_ref, gur_ref, gy_ref, ds_ref,
             gz_ref, gqg_ref, gkg_ref, gd_ref):
        i = pl.program_id(0)

        @pl.when(i == 0)
        def _():
            gqg_ref[...] = jnp.zeros_like(gqg_ref)
            gkg_ref[...] = jnp.zeros_like(gkg_ref)
            gd_ref[...] = jnp.zeros_like(gd_ref)

        gq, gg = norm_bwd(z_ref[:, 0:ATT_WIDTH], gq_ref[...] * 0.125, qg_ref, eq_ref)
        gz_ref[:, 0:ATT_WIDTH] = gq.astype(MX)
        gqg_ref[...] += _rows8(gg)
        gkv = d1_ref[0] + jnp.where(i + 1 < nb, d0_ref[0], 0.0) + jnp.where(i >= 1, d2_ref[0], 0.0)
        gk, gg = norm_bwd(z_ref[:, ATT_WIDTH:ATT_WIDTH + KV_WIDTH], gkv[:, 0:KV_WIDTH], kg_ref, ek_ref)
        gz_ref[:, ATT_WIDTH:ATT_WIDTH + KV_WIDTH] = gk.astype(MX)
        gkg_ref[...] += _rows8(gg)
        gz_ref[:, ATT_WIDTH + KV_WIDTH:U_OFF] = gkv[:, KV_WIDTH:].astype(MX)
        gyv = gy_ref[...]
        gz_ref[:, U_OFF:IN_WIDTH] = (guf_ref[...] + gur_ref[...] + ds_ref[...] * gyv).astype(MX)
        gd_ref[...] += _rows8(gyv * z_ref[:, U_OFF:IN_WIDTH])

    row = lambda w: pl.BlockSpec((BLOCK, w), lambda i: (i, 0))
    const = lambda a: pl.BlockSpec(a.shape, lambda i: (0, 0))
    qg = jnp.tile(q_gain.reshape(1, HEAD_DIM), (1, ATT_HEADS))
    kg = jnp.tile(k_gain.reshape(1, HEAD_DIM), (1, KV_HEADS))
    return pl.pallas_call(
        body, grid=(nb,),
        in_specs=[row(ATT_WIDTH),
                  pl.BlockSpec((1, BLOCK, 2 * KV_WIDTH), lambda i: (0, jnp.minimum(i + 1, nb - 1), 0)),
                  pl.BlockSpec((1, BLOCK, 2 * KV_WIDTH), lambda i: (1, i, 0)),
                  pl.BlockSpec((1, BLOCK, 2 * KV_WIDTH), lambda i: (2, jnp.maximum(i - 1, 0), 0)),
                  row(IN_WIDTH), const(qg), const(kg), const(eq), const(ek),
                  row(SSM_WIDTH), row(SSM_WIDTH), row(SSM_WIDTH), pl.BlockSpec((1, SSM_WIDTH), lambda i: (0, 0))],
        out_specs=[row(IN_WIDTH), pl.BlockSpec((8, ATT_WIDTH), lambda i: (0, 0)),
                   pl.BlockSpec((8, KV_WIDTH), lambda i: (0, 0)), pl.BlockSpec((8, SSM_WIDTH), lambda i: (0, 0))],
        out_shape=[SDS((s, IN_WIDTH), MX), SDS((8, ATT_WIDTH), f32), SDS((8, KV_WIDTH), f32),
                   SDS((8, SSM_WIDTH), f32)],
        compiler_params=_cp("arbitrary"), name=name)(
            gqs, dkv, dkv, dkv, z, qg, kg, eq, ek, gu_f, gu_r, gy, d_skip.reshape(1, SSM_WIDTH))


def _cmul(ar, ai, xr, xi):
    return ar * xr - ai * xi, ar * xi + ai * xr


def _permute_rows(src_ref, dst_ref, nv):
    for v in range(nv):
        dst_ref[8 * v:8 * v + 8, :] = src_ref[pl.ds(v, 8, stride=nv), :]


def _unpermute_rows(val, dst_ref, nv):
    for v in range(nv):
        dst_ref[pl.ds(v, 8, stride=nv), :] = val[8 * v:8 * v + 8, :]


def _scan_chunk(x_ref, tab_ref, carry_ref, nv, rev, acc=None):
    L = TILE_ST
    order = list(range(nv - 1, -1, -1)) if rev else list(range(nv))
    a_r, a_i = tab_ref[32:40, :L], tab_ref[32:40, L:]
    pr = pi = None
    for v in order:
        rows = slice(8 * v, 8 * v + 8)
        xr, xi = x_ref[rows, :L], x_ref[rows, L:]
        if pr is not None:
            mr, mi = _cmul(a_r, a_i, pr, pi)
            xr, xi = xr + mr, xi + mi
            x_ref[rows, :L] = xr
            x_ref[rows, L:] = xi
        pr, pi = xr, xi
    er, ei = pr, pi
    row = lax.broadcasted_iota(jnp.int32, (8, L), 0)
    edge = row == (7 if rev else 0)
    sh = 7 if rev else 1
    fr = jnp.where(edge, carry_ref[:, :L], pltpu.roll(er, sh, 0))
    fi = jnp.where(edge, carry_ref[:, L:], pltpu.roll(ei, sh, 0))
    for n, k in enumerate((1, 2, 4)):
        mr, mi = tab_ref[8 * n:8 * n + 8, :L], tab_ref[8 * n:8 * n + 8, L:]
        sh = (8 - k) if rev else k
        rr, ri = pltpu.roll(fr, sh, 0), pltpu.roll(fi, sh, 0)
        fr, fi = fr + mr * rr - mi * ri, fi + mr * ri + mi * rr
    dr, di = _cmul(tab_ref[24:32, :L], tab_ref[24:32, L:], fr, fi)
    last = 0 if rev else 7
    carry_ref[:, :L] = jnp.broadcast_to((dr + er)[last:last + 1, :], (8, L))
    carry_ref[:, L:] = jnp.broadcast_to((di + ei)[last:last + 1, :], (8, L))
    qr, qi = fr, fi
    if acc is not None:
        sr, si = jnp.zeros((8, L), f32), jnp.zeros((8, L), f32)
    for v in order:
        rows = slice(8 * v, 8 * v + 8)
        trow = slice(40 + v, 41 + v)
        mr, mi = _cmul(tab_ref[trow, :L], tab_ref[trow, L:], fr, fi)
        xr, xi = x_ref[rows, :L] + mr, x_ref[rows, L:] + mi
        x_ref[rows, :L] = xr
        x_ref[rows, L:] = xi
        if acc is not None:
            gr, gi = acc[0][rows, :L], acc[0][rows, L:]
            sr, si = sr + gr * qr + gi * qi, si + gi * qr - gr * qi
            qr, qi = xr, xi
    if acc is not None:
        acc[1][:, :L] += sr
        acc[1][:, L:] += si


def ssm_fwd(z, tab, bmat, cmat, *, rev, name, chunk):
    s = z.shape[0]
    nc = s // chunk
    nv = chunk // 8
    ci = (lambda i: nc - 1 - i) if rev else (lambda i: i)

    tp = TILES_PER_STEP

    def body(*refs):
        u_refs = refs[:tp]
        tab_ref, b_ref, c_ref, y_ref, xb_ref, u_scr, x_scr, carry = refs[tp:]

        @pl.when(pl.program_id(1) == 0)
        def _():
            carry[...] = jnp.zeros_like(carry)

        for t in range(tp):
            xb_ref[0, :, 2 * TILE_ST * t:2 * TILE_ST * (t + 1)] = carry[t]
            _permute_rows(u_refs[t], u_scr.at[t], nv)
            x_scr[t] = _dot(u_scr[t].astype(MX), b_ref[t])
        for t in range(tp):
            _scan_chunk(x_scr.at[t], tab_ref.at[t], carry.at[t], nv, rev)
        for t in range(tp):
            _unpermute_rows(_dot(x_scr[t].astype(MX), c_ref[t]), u_scr.at[t], nv)
            y_ref[:, TILE_CH * t:TILE_CH * (t + 1)] = u_scr[t]

    u_specs = [pl.BlockSpec((chunk, TILE_CH), lambda j, i, t=t: (ci(i), U_OFF // TILE_CH + tp * j + t))
               for t in range(tp)]
    return pl.pallas_call(
        body, grid=(SSM_TILES // tp, nc),
        in_specs=u_specs + [pl.BlockSpec((tp, 40 + nv, 2 * TILE_ST), lambda j, i: (j, 0, 0)),
                            pl.BlockSpec((tp, TILE_CH, 2 * TILE_ST), lambda j, i: (j, 0, 0)),
                            pl.BlockSpec((tp, 2 * TILE_ST, TILE_CH), lambda j, i: (j, 0, 0))],
        out_specs=[pl.BlockSpec((chunk, tp * TILE_CH), lambda j, i: (ci(i), j)),
                   pl.BlockSpec((1, 8, tp * 2 * TILE_ST), lambda j, i: (ci(i), 0, j))],
        out_shape=[SDS((s, SSM_WIDTH), f32), SDS((nc, 8, SSM_TILES * 2 * TILE_ST), f32)],
        scratch_shapes=[pltpu.VMEM((tp, chunk, TILE_CH), f32), pltpu.VMEM((tp, chunk, 2 * TILE_ST), f32),
                        pltpu.VMEM((tp, 8, 2 * TILE_ST), f32)],
        compiler_params=_cp("parallel", "arbitrary"), name=name)(*([z] * tp), tab, bmat, cmat)


def ssm_bwd(z, gy, xb, tab_s, tab_a, bmat, cmat, *, rev, name, chunk):
    s = z.shape[0]
    nc = s // chunk
    nv = chunk // 8
    ci = (lambda i: i) if rev else (lambda i: nc - 1 - i)

    tp = TILES_PER_STEP
    w2 = 2 * TILE_ST

    def body(*refs):
        u_refs, gy_refs = refs[:tp], refs[tp:2 * tp]
        (xb_ref, ts_ref, ta_ref, b_ref, c_ref, gu_ref, ga_ref, gb_ref, gc_ref,
         u_scr, gy_scr, x_scr, g_scr, gcarry, xcarry) = refs[2 * tp:]

        @pl.when(pl.program_id(1) == 0)
        def _():
            gcarry[...] = jnp.zeros_like(gcarry)
            ga_ref[...] = jnp.zeros_like(ga_ref)
            gb_ref[...] = jnp.zeros_like(gb_ref)
            gc_ref[...] = jnp.zeros_like(gc_ref)

        ub, gyb = [], []
        for t in range(tp):
            _permute_rows(u_refs[t], u_scr.at[t], nv)
            _permute_rows(gy_refs[t], gy_scr.at[t], nv)
            ub.append(u_scr[t].astype(MX))
            gyb.append(gy_scr[t].astype(MX))
        for t in range(tp):
            g_scr[t] = _dot(gyb[t], c_ref[t], NT)
            x_scr[t] = _dot(ub[t], b_ref[t])
            xcarry[t] = xb_ref[0, :, w2 * t:w2 * (t + 1)]
        for t in range(tp):
            _scan_chunk(g_scr.at[t], ta_ref.at[t], gcarry.at[t], nv, not rev)
        for t in range(tp):
            _scan_chunk(x_scr.at[t], ts_ref.at[t], xcarry.at[t], nv, rev,
                        acc=(g_scr.at[t], ga_ref.at[:, pl.ds(w2 * t, w2)]))
            gb16 = g_scr[t].astype(MX)
            gb_ref[t] += _dot(ub[t], gb16, TN)
            gc_ref[t] += _dot(x_scr[t].astype(MX), gyb[t], TN)
            _unpermute_rows(_dot(gb16, b_ref[t], NT), u_scr.at[t], nv)
            gu_ref[:, TILE_CH * t:TILE_CH * (t + 1)] = u_scr[t]

    tile3 = lambda a, b: pl.BlockSpec((tp, a, b), lambda j, i: (j, 0, 0))
    u_specs = [pl.BlockSpec((chunk, TILE_CH), lambda j, i, t=t: (ci(i), U_OFF // TILE_CH + tp * j + t))
               for t in range(tp)]
    gy_specs = [pl.BlockSpec((chunk, TILE_CH), lambda j, i, t=t: (ci(i), tp * j + t)) for t in range(tp)]
    return pl.pallas_call(
        body, grid=(SSM_TILES // tp, nc),
        in_specs=u_specs + gy_specs + [
                  pl.BlockSpec((1, 8, tp * w2), lambda j, i: (ci(i), 0, j)),
                  tile3(40 + nv, w2), tile3(40 + nv, w2), tile3(TILE_CH, w2), tile3(w2, TILE_CH)],
        out_specs=[pl.BlockSpec((chunk, tp * TILE_CH), lambda j, i: (ci(i), j)),
                   pl.BlockSpec((8, tp * w2), lambda j, i: (0, j)),
                   tile3(TILE_CH, w2), tile3(w2, TILE_CH)],
        out_shape=[SDS((s, SSM_WIDTH), f32), SDS((8, SSM_TILES * w2), f32),
                   SDS((SSM_TILES, TILE_CH, w2), f32), SDS((SSM_TILES, w2, TILE_CH), f32)],
        scratch_shapes=[pltpu.VMEM((tp, chunk, TILE_CH), f32), pltpu.VMEM((tp, chunk, TILE_CH), f32),
                        pltpu.VMEM((tp, chunk, w2), f32), pltpu.VMEM((tp, chunk, w2), f32),
                        pltpu.VMEM((tp, 8, w2), f32), pltpu.VMEM((tp, 8, w2), f32)],
        compiler_params=_cp("parallel", "arbitrary"), name=name)(
            *([z] * tp), *([gy] * tp), xb, tab_s, tab_a, bmat, cmat)


GELU_K = math.sqrt(2.0 / math.pi)


def _gelu(y):
    return 0.5 * y * (1.0 + jnp.tanh(GELU_K * (y + 0.044715 * (y * y * y))))


def _gelu_grad(y):
    t = jnp.tanh(GELU_K * (y + 0.044715 * (y * y * y)))
    return 0.5 * (1.0 + t) + 0.5 * y * (1.0 - t * t) * (GELU_K * (1.0 + 3.0 * 0.044715 * (y * y)))


def glu_fwd(y_f, y_r, z, att, d_skip, w_glu, l, *, name, tm):
    s = z.shape[0]
    nblk, cb = w_glu.shape[1], w_glu.shape[3]

    def body(yf_ref, yr_ref, ua_ref, ub_ref, att_ref, d_ref, w_ref, y_ref, gg_ref, mix_ref):
        u = jnp.concatenate([ua_ref[...], ub_ref[...]], axis=1)
        y = d_ref[...] * u + yf_ref[...] + yr_ref[...]
        y_ref[...] = y
        yg = _gelu(y).astype(MX)
        for b in range(nblk):
            gg_ref[:, cb * b:cb * (b + 1)] = _dot(yg, w_ref[0, b])
        mix_ref[:, 0:ATT_WIDTH] = att_ref[...]
        mix_ref[:, ATT_WIDTH:] = (gg_ref[:, :SSM_WIDTH] * jax.nn.sigmoid(gg_ref[:, SSM_WIDTH:])).astype(MX)

    return pl.pallas_call(
        body, grid=(s // tm,),
        in_specs=[_row_spec(tm, SSM_WIDTH), _row_spec(tm, SSM_WIDTH),
                  pl.BlockSpec((tm, SSM_WIDTH // 2), lambda i: (i, U_OFF // (SSM_WIDTH // 2))),
                  pl.BlockSpec((tm, SSM_WIDTH // 2), lambda i: (i, U_OFF // (SSM_WIDTH // 2) + 1)),
                  _row_spec(tm, ATT_WIDTH), pl.BlockSpec((1, SSM_WIDTH), lambda i: (0, 0)), _layer_spec(w_glu, l)],
        out_specs=[_row_spec(tm, SSM_WIDTH), _row_spec(tm, 2 * SSM_WIDTH), _row_spec(tm, D_MODEL)],
        out_shape=[SDS((s, SSM_WIDTH), f32), SDS((s, 2 * SSM_WIDTH), f32), SDS((s, D_MODEL), MX)],
        compiler_params=_cp("parallel"), name=name)(y_f, y_r, z, z, att, d_skip.reshape(1, SSM_WIDTH), w_glu)


def glu_bwd(gmix, gg, ypre, w_glu, l, *, name, tm):
    s = gg.shape[0]
    nblk, cb = w_glu.shape[1], w_glu.shape[3]

    def body(gm_ref, gg_ref, y_ref, w_ref, ggg_ref, yg_ref, gy_ref):
        gs = gm_ref[...]
        val, gate = gg_ref[:, :SSM_WIDTH], gg_ref[:, SSM_WIDTH:]
        sg = jax.nn.sigmoid(gate)
        ggg_ref[:, :SSM_WIDTH] = (gs * sg).astype(MX)
        ggg_ref[:, SSM_WIDTH:] = (gs * val * sg * (1.0 - sg)).astype(MX)
        y = y_ref[...]
        yg_ref[...] = _gelu(y).astype(MX)
        gyg = _dot(ggg_ref[:, 0:cb], w_ref[0, 0], NT)
        for b in range(1, nblk):
            gyg = gyg + _dot(ggg_ref[:, cb * b:cb * (b + 1)], w_ref[0, b], NT)
        gy_ref[...] = gyg * _gelu_grad(y)

    return pl.pallas_call(
        body, grid=(s // tm,),
        in_specs=[pl.BlockSpec((tm, SSM_WIDTH), lambda i: (i, 1)), _row_spec(tm, 2 * SSM_WIDTH),
                  _row_spec(tm, SSM_WIDTH), _layer_spec(w_glu, l)],
        out_specs=[_row_spec(tm, 2 * SSM_WIDTH), _row_spec(tm, SSM_WIDTH), _row_spec(tm, SSM_WIDTH)],
        out_shape=[SDS((s, 2 * SSM_WIDTH), MX), SDS((s, SSM_WIDTH), MX), SDS((s, SSM_WIDTH), f32)],
        compiler_params=_cp("parallel"), name=name)(gmix, gg, ypre, w_glu)


def loss_grad(y, target, *, name, tm):
    s, d = y.shape

    def body(y_ref, t_ref, g_ref, g16_ref, l_ref):
        @pl.when(pl.program_id(0) == 0)
        def _():
            l_ref[...] = jnp.zeros_like(l_ref)

        e = y_ref[...] - t_ref[...]
        g = e * (1.0 / d)
        g_ref[...] = g
        g16_ref[...] = g.astype(MX)
        l_ref[...] += _rows8(e * e)

    row = pl.BlockSpec((tm, d), lambda i: (i, 0))
    return pl.pallas_call(
        body, grid=(s // tm,), in_specs=[row, row],
        out_specs=[row, row, pl.BlockSpec((8, d), lambda i: (0, 0))],
        out_shape=[SDS((s, d), f32), SDS((s, d), MX), SDS((8, d), f32)],
        compiler_params=_cp("arbitrary"), name=name)(y, target)


def _row_tile(rows, cols):
    tr = rows
    while tr * cols > 256 * 1024 and tr % 16 == 0:
        tr //= 2
    return tr


def _elementwise(fn, ins, n_out, *, name, out_dtype=f32):
    shape = ins[0].shape
    cols = shape[-1]
    ins2 = [a.reshape(-1, cols) for a in ins]
    rows = ins2[0].shape[0]
    tr = _row_tile(rows, cols)

    def body(*refs):
        outs = fn(*[r[...] for r in refs[:len(ins)]])
        for o_ref, o in zip(refs[len(ins):], outs):
            o_ref[...] = o.astype(out_dtype)

    spec = pl.BlockSpec((tr, cols), lambda i: (i, 0))
    outs = pl.pallas_call(
        body, grid=(rows // tr,), in_specs=[spec] * len(ins), out_specs=[spec] * n_out,
        out_shape=[SDS((rows, cols), out_dtype)] * n_out, compiler_params=_cp("parallel"), name=name)(*ins2)
    return [o.reshape(shape) for o in outs]


def _adamw_math(w, g, m, v):
    m = ADAM_B1 * m + (1.0 - ADAM_B1) * g
    v = ADAM_B2 * v + (1.0 - ADAM_B2) * (g * g)
    m_hat = m / (1.0 - ADAM_B1 ** ADAM_STEP)
    v_hat = v / (1.0 - ADAM_B2 ** ADAM_STEP)
    delta = -ADAM_LR * (m_hat / (jnp.sqrt(v_hat) + ADAM_EPS) + ADAM_WD * w)
    return delta, m, v


def adamw(w, g, m, v, *, name):
    return _elementwise(_adamw_math, [w, g, m, v], 3, name=name)


SMEM = pl.BlockSpec(memory_space=pltpu.SMEM)


def _core_index():
    return lax.axis_index("c").astype(jnp.int32).reshape(1)


def adamw_halves(w, own, sib, m, v, *, name):
    depth, r, cols = w.shape
    h = r // 2
    tr = _row_tile(h, cols)
    quad = lambda a: a.reshape(depth, 2, h, cols)

    def body(c_ref, w_ref, own_ref, sib_ref, m_ref, v_ref, g_ref, d_ref, mo_ref, vo_ref):
        g = jnp.where(pl.program_id(1) == c_ref[0], own_ref[0], sib_ref[0])
        g_ref[0, 0] = g
        d_ref[0, 0], mo_ref[0, 0], vo_ref[0, 0] = _adamw_math(w_ref[0, 0], g, m_ref[0, 0], v_ref[0, 0])

    full = pl.BlockSpec((1, 1, tr, cols), lambda l, j, i: (l, j, i, 0))
    part = pl.BlockSpec((1, tr, cols), lambda l, j, i: (l, i, 0))
    outs = pl.pallas_call(
        body, grid=(depth, 2, h // tr), in_specs=[SMEM, full, part, part, full, full], out_specs=[full] * 4,
        out_shape=[SDS((depth, 2, h, cols), f32)] * 4,
        compiler_params=_cp("parallel", "parallel", "parallel"), name=name)(
            _core_index(), quad(w), own, sib, quad(m), quad(v))
    return [o.reshape(depth, r, cols) for o in outs]


def add_own_half(g4, recv, *, name):
    _, _, h, cols = g4.shape
    tr = _row_tile(h, cols)

    def body(c_ref, g_ref, r_ref, o_ref):
        own = jnp.where(c_ref[0] == 0, g_ref[0, 0], g_ref[0, 1])
        o_ref[0] = (own + r_ref[0]).astype(WIRE)

    part = pl.BlockSpec((1, tr, cols), lambda s, i: (s, i, 0))
    return pl.pallas_call(
        body, grid=(4, h // tr),
        in_specs=[SMEM, pl.BlockSpec((1, 2, tr, cols), lambda s, i: (s, 0, i, 0)), part], out_specs=part,
        out_shape=SDS((4, h, cols), WIRE), compiler_params=_cp("parallel", "parallel"), name=name)(
            _core_index(), g4, recv)


def _chip_index():
    return (2 * lax.axis_index("x") + lax.axis_index("y")).astype(jnp.int32).reshape(1)


def sum_pieces(sums, got, *, name, into, layer):
    _, h, cols = sums.shape
    tr = _row_tile(h, cols)

    def body(me_ref, s_ref, g_ref, stack_ref, o_ref):
        del stack_ref
        own = s_ref[0]
        for s in range(1, 4):
            own = jnp.where(me_ref[0] == s, s_ref[s], own)
        o_ref[0] = ((own.astype(f32) + g_ref[0].astype(f32)) + g_ref[1].astype(f32)) + g_ref[2].astype(f32)

    return pl.pallas_call(
        body, grid=(h // tr,),
        in_specs=[SMEM, pl.BlockSpec((4, tr, cols), lambda i: (0, i, 0)),
                  pl.BlockSpec((3, tr, cols), lambda i: (0, i, 0)), ANY],
        out_specs=pl.BlockSpec((1, tr, cols), lambda i: (layer, i, 0)),
        out_shape=SDS(into.shape, f32), input_output_aliases={3: 0},
        compiler_params=_cp("parallel"), name=name)(_chip_index(), sums, got, into)


def sum4(a, *, name, into=None, layer=0):
    shape = a.shape[1:]
    cols = shape[-1]
    a2 = a.reshape(4, -1, cols)
    rows = a2.shape[1]
    tr = _row_tile(rows, cols)

    def body(*refs):
        a_ref, o_ref = refs[0], refs[-1]
        tot = ((a_ref[0].astype(f32) + a_ref[1].astype(f32)) + a_ref[2].astype(f32)) + a_ref[3].astype(f32)
        if into is None:
            o_ref[...] = tot
        else:
            o_ref[0] = tot

    in_spec = pl.BlockSpec((4, tr, cols), lambda i: (0, i, 0))
    if into is None:
        out = pl.pallas_call(
            body, grid=(rows // tr,), in_specs=[in_spec], out_specs=pl.BlockSpec((tr, cols), lambda i: (i, 0)),
            out_shape=SDS((rows, cols), f32), compiler_params=_cp("parallel"), name=name)(a2)
        return out.reshape(shape)
    stack = into.reshape(into.shape[0], rows, cols)
    out = pl.pallas_call(
        body, grid=(rows // tr,), in_specs=[in_spec, ANY],
        out_specs=pl.BlockSpec((1, tr, cols), lambda i: (layer, i, 0)),
        out_shape=SDS(stack.shape, f32), input_output_aliases={1: 0},
        compiler_params=_cp("parallel"), name=name)(a2, stack)
    return out.reshape(into.shape)


ANY = pl.BlockSpec(memory_space=pl.ANY)


def chip_exchange(arrs, bcast, *, name):
    n = len(arrs)

    def body(*refs):
        ins, outs = refs[:n], refs[n:2 * n]
        send, recv = refs[2 * n:]
        x, y, c = lax.axis_index("x"), lax.axis_index("y"), lax.axis_index("c")
        me = 2 * x + y
        copies = []
        for k in range(n):
            for j, (px, py) in enumerate(((1 - x, y), (x, 1 - y), (1 - x, 1 - y))):
                cp = pltpu.make_async_remote_copy(
                    src_ref=ins[k] if bcast[k] else ins[k].at[2 * px + py],
                    dst_ref=outs[k].at[me] if bcast[k] else outs[k].at[j],
                    send_sem=send.at[4 * k + j], recv_sem=recv.at[4 * k + j],
                    device_id=(px, py, c), device_id_type=MESH)
                cp.start()
                copies.append(cp)
            if bcast[k]:
                cp = pltpu.make_async_remote_copy(
                    src_ref=ins[k], dst_ref=outs[k].at[me], send_sem=send.at[4 * k + 3], recv_sem=recv.at[4 * k + 3],
                    device_id=(x, y, 1 - c), device_id_type=MESH)
                cp.start()
                copies.append(cp)
        for cp in copies:
            cp.wait()

    return pl.pallas_call(
        body, in_specs=[ANY] * n, out_specs=[ANY] * n,
        out_shape=[SDS((4,) + tuple(a.shape) if b else (3,) + tuple(a.shape[1:]), a.dtype)
                   for a, b in zip(arrs, bcast)],
        scratch_shapes=[pltpu.SemaphoreType.DMA((4 * n,)), pltpu.SemaphoreType.DMA((4 * n,))],
        name=name)(*arrs)


def gather_weights(shards, *, name):
    n = len(shards)
    hd = shards[0].shape[0] // 2

    def body(*refs):
        ins, outs = refs[:n], refs[n:2 * n]
        send, recv = refs[2 * n:]
        x, y, c = lax.axis_index("x"), lax.axis_index("y"), lax.axis_index("c")
        me = 2 * x + y
        chips = ((1 - x, y), (x, 1 - y), (1 - x, 1 - y))
        mine, theirs = pl.ds(c * hd, hd), pl.ds((1 - c) * hd, hd)

        def ici(k, j, src, dst):
            px, py = chips[j]
            return pltpu.make_async_remote_copy(src_ref=src, dst_ref=dst, send_sem=send.at[7 * k + j],
                                                recv_sem=recv.at[7 * k + j], device_id=(px, py, c),
                                                device_id_type=MESH)

        def d2d(k, j, src, dst):
            return pltpu.make_async_remote_copy(src_ref=src, dst_ref=dst, send_sem=send.at[7 * k + 3 + j],
                                                recv_sem=recv.at[7 * k + 3 + j], device_id=(x, y, 1 - c),
                                                device_id_type=MESH)

        own, sent = [], []
        for k in range(n):
            own.append(d2d(k, 3, ins[k], outs[k].at[:, me]))
            own[-1].start()
            for j in range(3):
                sent.append(ici(k, j, ins[k].at[mine], outs[k].at[mine, me]))
                sent[-1].start()
        for k in range(n):
            for j, (px, py) in enumerate(chips):
                landed = outs[k].at[mine, 2 * px + py]
                ici(k, j, landed, landed).wait_recv()
                sent.append(d2d(k, j, landed, landed))
                sent[-1].start()
        for k in range(n):
            for j, (px, py) in enumerate(chips):
                other = outs[k].at[theirs, 2 * px + py]
                d2d(k, j, other, other).wait_recv()
        for cp in sent:
            cp.wait_send()
        for cp in own:
            cp.wait()

    return pl.pallas_call(
        body, in_specs=[ANY] * n, out_specs=[ANY] * n,
        out_shape=[SDS((a.shape[0], 4) + tuple(a.shape[1:]), a.dtype) for a in shards],
        scratch_shapes=[pltpu.SemaphoreType.DMA((7 * n,)), pltpu.SemaphoreType.DMA((7 * n,))],
        name=name)(*shards)


def sibling_exchange(arrs, half, *, name):
    n = len(arrs)
    piece = [(a.shape[0],) + a.shape[2:] if h else a.shape for a, h in zip(arrs, half)]

    def body(*refs):
        ins, outs = refs[:n], refs[n:2 * n]
        send, recv = refs[2 * n:]
        x, y, c = lax.axis_index("x"), lax.axis_index("y"), lax.axis_index("c")
        copies = []
        for k in range(n):
            cp = pltpu.make_async_remote_copy(
                src_ref=ins[k].at[:, 1 - c] if half[k] else ins[k], dst_ref=outs[k],
                send_sem=send.at[k], recv_sem=recv.at[k], device_id=(x, y, 1 - c), device_id_type=MESH)
            cp.start()
            copies.append(cp)
        for cp in copies:
            cp.wait()

    return pl.pallas_call(
        body, in_specs=[ANY] * n, out_specs=[ANY] * n,
        out_shape=[SDS(tuple(p), a.dtype) for p, a in zip(piece, arrs)],
        scratch_shapes=[pltpu.SemaphoreType.DMA((n,)), pltpu.SemaphoreType.DMA((n,))],
        name=name)(*arrs)


def ssm_discretize(lam_re, lam_im, log_dt, b_re, b_im, c_re, c_im):
    dt = jnp.exp(log_dt)[..., None]
    mag = jnp.exp(lam_re * dt)
    abr = mag * jnp.cos(lam_im * dt)
    abi = mag * jnp.sin(lam_im * dt)
    den = lam_re * lam_re + lam_im * lam_im
    zr = ((abr - 1.0) * lam_re + abi * lam_im) / den
    zi = (abi * lam_re - (abr - 1.0) * lam_im) / den
    bbr = zr[..., None] * b_re - zi[..., None] * b_im
    bbi = zr[..., None] * b_im + zi[..., None] * b_re
    eye = jnp.eye(8, dtype=f32)
    bb = jnp.stack([bbr, bbi], axis=1).reshape(2, 2, SSM_TILES, 8, SSM_STATE, SSM_GROUP)
    bmat = jnp.einsum('dqjgph,gk->djghqkp', bb, eye).reshape(2, SSM_TILES, TILE_CH, 2 * TILE_ST)
    cc = jnp.stack([c_re, -c_im], axis=1).reshape(2, 2, SSM_TILES, 8, SSM_GROUP, SSM_STATE)
    cmat = jnp.einsum('dqjghp,gk->djqkpgh', cc, eye).reshape(2, SSM_TILES, 2 * TILE_ST, TILE_CH)
    n = SSM_GROUPS * SSM_STATE
    return abr.reshape(2, n), abi.reshape(2, n), bmat, cmat


def scan_tables(ar, ai, rev, nv):
    pw = [(ar, ai)]
    for _ in range(nv - 1):
        pw.append(_cmul(ar, ai, *pw[-1]))
    big = [pw[nv - 1]]
    big.append(_cmul(*big[0], *big[0]))
    big.append(_cmul(*big[1], *big[1]))
    rows = jnp.arange(8)[:, None]
    ones = jnp.ones((8, 1), f32)
    parts = []
    for k, p in zip((1, 2, 4), big):
        cond = (rows <= 7 - k) if rev else (rows >= k)
        parts.append([jnp.where(cond, q[None, :], 0.0) for q in p])
    parts.append([ones * q[None, :] for q in big[0]])
    parts.append([ones * q[None, :] for q in pw[0]])
    for v in range(nv):
        parts.append([q[None, :] for q in pw[nv - 1 - v if rev else v]])
    nrow = 40 + nv
    tre = jnp.concatenate([p[0] for p in parts], axis=0).reshape(nrow, SSM_TILES, TILE_ST)
    tim = jnp.concatenate([p[1] for p in parts], axis=0).reshape(nrow, SSM_TILES, TILE_ST)
    return jnp.concatenate([tre, tim], axis=-1).transpose(1, 0, 2)


def _tile_a(ga):
    t = ga.sum(axis=0).reshape(SSM_TILES, 2, TILE_ST)
    return t[:, 0].reshape(-1), t[:, 1].reshape(-1)


SMALL = ('norm1', 'q_gain', 'k_gain', 'sink', 'lam_re', 'lam_im', 'log_dt', 'b_re', 'b_im', 'c_re', 'c_im',
         'd_skip', 'norm2')
BIG = ('w_in', 'w_glu', 'w_out', 'w_ff1', 'w_ff2')
WEIGHTS = ('norm1', 'w_in', 'q_gain', 'k_gain', 'sink', 'lam_re', 'lam_im', 'log_dt', 'b_re', 'b_im', 'c_re',
           'c_im', 'd_skip', 'w_glu', 'w_out', 'norm2', 'w_ff1', 'w_ff2')


def _chunk(s):
    return min(512, s)


def layer_forward(l, x, p, wb):
    s = x.shape[0]
    tm = min(512, s)
    sv = {}
    h1, z = norm_mm(x, p['norm1'], wb['w_in'], l, relu2=False, name=f"l{l}_in", tm=tm)
    eq, ek = head_mean_matrix(ATT_WIDTH), head_mean_matrix(KV_WIDTH)
    qn, kv = qk_prep(z, p['q_gain'], p['k_gain'], eq, ek, name=f"l{l}_qk", tm=tm)
    att = attn_fwd(qn, kv, p['sink'], name=f"l{l}_attn")
    sv.update(qn=qn, kv=kv, eq=eq, ek=ek)
    (ar, ai, bmat, cmat), disc_vjp = jax.vjp(
        ssm_discretize, p['lam_re'], p['lam_im'], p['log_dt'], p['b_re'], p['b_im'], p['c_re'], p['c_im'])
    bmat16, cmat16 = bmat.astype(MX), cmat.astype(MX)
    ys, xbs, tabs = [], [], []
    for d, rev in enumerate((False, True)):
        tab = scan_tables(ar[d], ai[d], rev, _chunk(s) // 8)
        y_d, xb_d = ssm_fwd(z, tab, bmat16[d], cmat16[d], rev=rev, name=f"l{l}_ssm{d}", chunk=_chunk(s))
        ys.append(y_d)
        xbs.append(xb_d)
        tabs.append((tab, scan_tables(ar[d], -ai[d], not rev, _chunk(s) // 8)))
    ypre, gg, mix = glu_fwd(ys[0], ys[1], z, att, p['d_skip'], wb['w_glu'], l, name=f"l{l}_glu", tm=min(256, s))
    x1 = mm_res(mix, wb['w_out'], l, x, name=f"l{l}_out", tm=tm)
    h2, a2 = norm_mm(x1, p['norm2'], wb['w_ff1'], l, relu2=True, name=f"l{l}_ff1", tm=tm)
    x2 = mm_res(a2, wb['w_ff2'], l, x1, name=f"l{l}_ff2", tm=tm)
    sv.update(x=x, h1=h1, z=z, xbs=xbs, tabs=tabs, bmat16=bmat16, cmat16=cmat16, disc_vjp=disc_vjp,
              ypre=ypre, gg=gg, mix=mix, x1=x1, h2=h2, a2=a2)
    return x2, sv


def layer_backward(l, gx2, gx2h, p, wb, sv):
    s = gx2.shape[0]
    tm = min(512, s)
    ts = min(1024, s)
    g = {}
    gf = mm_nt(gx2h, wb['w_ff2'], l, name=f"l{l}_bff2", tm=tm, a2=sv['a2'])
    g['w_ff2'] = mm_tn(sv['a2'], gx2h, name=f"l{l}_wff2", tk=1024, tn=1024, ts=ts).reshape(4, D_FF // 4, D_MODEL)
    gx1, gx1h, gn2 = mm_nt_norm(gf, wb['w_ff1'], l, sv['x1'], p['norm2'], gx2, name=f"l{l}_bff1", tm=min(256, s))
    g['norm2'] = gn2.sum(axis=0)
    g['w_ff1'] = mm_tn(sv['h2'], gf, name=f"l{l}_wff1", tk=1024, tn=1024, ts=ts, chip_major=True)
    gmix = mm_nt(gx1h, wb['w_out'], l, name=f"l{l}_bout", tm=tm)
    g['w_out'] = mm_tn(sv['mix'], gx1h, name=f"l{l}_wout", tk=1024, tn=1024, ts=ts).reshape(4, D_MODEL // 4, D_MODEL)
    ggg, yg, gy = glu_bwd(gmix, sv['gg'], sv['ypre'], wb['w_glu'], l, name=f"l{l}_bglu", tm=min(256, s))
    g['w_glu'] = mm_tn(yg, ggg, name=f"l{l}_wglu", tk=512, tn=256, ts=ts, chip_major=True)
    gus, gas, gbs, gcs = [], [], [], []
    for d, rev in enumerate((False, True)):
        tab_s, tab_a = sv['tabs'][d]
        gu_d, ga_d, gb_d, gc_d = ssm_bwd(sv['z'], gy, sv['xbs'][d], tab_s, tab_a, sv['bmat16'][d], sv['cmat16'][d],
                                         rev=rev, name=f"l{l}_bssm{d}", chunk=_chunk(s))
        gus.append(gu_d)
        gas.append(_tile_a(ga_d))
        gbs.append(gb_d)
        gcs.append(gc_d)
    gar = jnp.stack([gas[0][0], gas[1][0]])
    gai = jnp.stack([gas[0][1], gas[1][1]])
    (g['lam_re'], g['lam_im'], g['log_dt'], g['b_re'], g['b_im'], g['c_re'], g['c_im']) = sv['disc_vjp'](
        (gar, gai, jnp.stack(gbs), jnp.stack(gcs)))
    gqs, dkv, gsk = attn_bwd(sv['qn'], sv['kv'], gmix, p['sink'], name=f"l{l}_battn")
    g['sink'] = gsk[:, 0]
    gz, gqg, gkg, gd = gz_assemble(gqs, dkv, sv['z'], p['q_gain'], p['k_gain'], sv['eq'], sv['ek'], gus[0], gus[1],
                                   gy, p['d_skip'], name=f"l{l}_gz")
    g['q_gain'] = gqg.sum(axis=0).reshape(ATT_HEADS, HEAD_DIM).sum(axis=0)
    g['k_gain'] = gkg.sum(axis=0).reshape(KV_HEADS, HEAD_DIM).sum(axis=0)
    g['d_skip'] = gd.sum(axis=0)
    gx, gxh, gn1 = mm_nt_norm(gz, wb['w_in'], l, sv['x'], p['norm1'], gx1, name=f"l{l}_bin", tm=tm)
    g['norm1'] = gn1.sum(axis=0)
    gw_in = mm_tn(sv['h1'], gz, name=f"l{l}_win", tk=1024, tn=640, ts=ts)
    g['w_in'] = gw_in.reshape(D_MODEL, 4, IN_WIDTH // 4).transpose(1, 0, 2)
    return gx, gxh, g


def stack_layouts(gathered):
    w_in = gathered['w_in']
    depth = w_in.shape[0]
    return dict(w_in=w_in.transpose(0, 2, 1, 3).reshape(depth, D_MODEL, IN_WIDTH),
                w_glu=gathered['w_glu'], w_ff1=gathered['w_ff1'],
                w_out=gathered['w_out'].reshape(depth, D_MODEL, D_MODEL),
                w_ff2=gathered['w_ff2'].reshape(depth, D_FF, D_MODEL))


def local_step(x, target, small, wb, after_layer=None):
    depth = wb['w_in'].shape[0]
    saves = []
    for l in range(depth):
        p = {k: small[k][l] for k in SMALL}
        x, sv = layer_forward(l, x, p, wb)
        saves.append(sv)
    gx, gxh, lparts = loss_grad(x, target, name="loss", tm=min(512, x.shape[0]))
    grads = [None] * depth
    for l in reversed(range(depth)):
        p = {k: small[k][l] for k in SMALL}
        gx, gxh, g = layer_backward(l, gx, gxh, p, wb, saves[l])
        grads[l] = g if after_layer is None else after_layer(l, g)
    return lparts, gx, grads


def reduce_layer(l, g, stacks):
    arrs = [g[k].reshape(4, 2, g[k].shape[1] // 2, g[k].shape[2]) for k in BIG]
    got = sibling_exchange(arrs, [True] * len(BIG), name=f"l{l}_rsib")
    sums = [add_own_half(a, b, name=f"l{l}_radd_{k}") for k, a, b in zip(BIG, arrs, got)]
    got = chip_exchange(sums, [False] * len(BIG), name=f"l{l}_rchips")
    return {k: sum_pieces(a, b, name=f"l{l}_rsum_{k}", into=stacks[k], layer=l) for k, a, b in zip(BIG, sums, got)}


def reduce_small(packed):
    got = sibling_exchange([packed], [False], name="small_rsib")
    pair = _elementwise(lambda a, b: (a + b,), [packed, got[0]], 1, name="small_radd")[0]
    got = chip_exchange([pair], [True], name="small_rchips")
    return sum4(got[0], name="small_rsum")


def _pack_small(tree):
    parts = []
    for k in SMALL:
        flat = tree[k].reshape(-1)
        parts.append(jnp.pad(flat, (0, (-flat.shape[0]) % 1024)).reshape(-1, 128))
    return jnp.concatenate(parts, axis=0)


def _unpack_small(packed, like):
    out, row = {}, 0
    for k in SMALL:
        n = like[k].size
        rows = -(-n // 1024) * 8
        out[k] = packed[row:row + rows].reshape(-1)[:n].reshape(like[k].shape)
        row += rows
    return out


def kernel(x, norm1, w_in, q_gain, k_gain, sink, lam_re, lam_im, log_dt, b_re, b_im, c_re, c_im, d_skip, w_glu, w_out, norm2, w_ff1, w_ff2, loss_target, m_norm1, m_w_in, m_q_gain, m_k_gain, m_sink, m_lam_re, m_lam_im, m_log_dt, m_b_re, m_b_im, m_c_re, m_c_im, m_d_skip, m_w_glu, m_w_out, m_norm2, m_w_ff1, m_w_ff2, v_norm1, v_w_in, v_q_gain, v_k_gain, v_sink, v_lam_re, v_lam_im, v_log_dt, v_b_re, v_b_im, v_c_re, v_c_im, v_d_skip, v_w_glu, v_w_out, v_norm2, v_w_ff1, v_w_ff2):
    w = dict(norm1=norm1, w_in=w_in, q_gain=q_gain, k_gain=k_gain, sink=sink, lam_re=lam_re, lam_im=lam_im,
             log_dt=log_dt, b_re=b_re, b_im=b_im, c_re=c_re, c_im=c_im, d_skip=d_skip, w_glu=w_glu, w_out=w_out,
             norm2=norm2, w_ff1=w_ff1, w_ff2=w_ff2)
    m = dict(norm1=m_norm1, w_in=m_w_in, q_gain=m_q_gain, k_gain=m_k_gain, sink=m_sink, lam_re=m_lam_re,
             lam_im=m_lam_im, log_dt=m_log_dt, b_re=m_b_re, b_im=m_b_im, c_re=m_c_re, c_im=m_c_im,
             d_skip=m_d_skip, w_glu=m_w_glu, w_out=m_w_out, norm2=m_norm2, w_ff1=m_w_ff1, w_ff2=m_w_ff2)
    v = dict(norm1=v_norm1, w_in=v_w_in, q_gain=v_q_gain, k_gain=v_k_gain, sink=v_sink, lam_re=v_lam_re,
             lam_im=v_lam_im, log_dt=v_log_dt, b_re=v_b_re, b_im=v_b_im, c_re=v_c_re, c_im=v_c_im,
             d_skip=v_d_skip, w_glu=v_w_glu, w_out=v_w_out, norm2=v_norm2, w_ff1=v_w_ff1, w_ff2=v_w_ff2)
    depth = w_in.shape[0]

    gathered = gather_weights([w[k].astype(WIRE) for k in BIG], name="gather_w")
    wb = stack_layouts(dict(zip(BIG, gathered)))
    small = {k: w[k] for k in SMALL}

    stacks = [{k: jnp.zeros((depth, w[k].shape[1] // 2, w[k].shape[2]), f32) for k in BIG}]

    def after_layer(l, g):
        stacks[0] = reduce_layer(l, g, stacks[0])
        return {k: g[k] for k in SMALL}

    lparts, gx, grads = local_step(x[0], loss_target[0], small, wb, after_layer)
    loss = lax.psum(0.5 * jnp.sum(lparts) / D_MODEL, ("x", "y", "c"))

    sib = sibling_exchange([stacks[0][k] for k in BIG], [False] * len(BIG), name="reduce_back")
    gsmall = reduce_small(_pack_small({k: jnp.stack([grads[l][k] for l in range(depth)]) for k in SMALL}))
    like = {k: w[k] for k in SMALL}
    gfull = _unpack_small(gsmall, like)

    delta, new_m, new_v = {}, {}, {}
    for k, sib_k in zip(BIG, sib):
        gfull[k], delta[k], new_m[k], new_v[k] = adamw_halves(w[k], stacks[0][k], sib_k, m[k], v[k],
                                                              name=f"adamw_{k}")
    ds, ms, vs = adamw(_pack_small(like), gsmall, _pack_small({k: m[k] for k in SMALL}),
                       _pack_small({k: v[k] for k in SMALL}), name="adamw_small")
    delta.update(_unpack_small(ds, like))
    new_m.update(_unpack_small(ms, like))
    new_v.update(_unpack_small(vs, like))

    return (loss, gx[None], *[gfull[k] for k in WEIGHTS], *[delta[k] for k in WEIGHTS],
            *[new_m[k] for k in WEIGHTS], *[new_v[k] for k in WEIGHTS])
```

```python
import functools
import math

import jax
import jax.numpy as jnp
from jax import lax
from jax.experimental import pallas as pl
from jax.experimental.pallas import tpu as pltpu

f32 = jnp.float32
MX = jnp.bfloat16
WIRE = jnp.bfloat16
SDS = jax.ShapeDtypeStruct

D_MODEL = 1024
DEPTH = 4
ATT_HEADS = 8
KV_HEADS = 2
GQA = ATT_HEADS // KV_HEADS
HEAD_DIM = 64
ATT_WIDTH = ATT_HEADS * HEAD_DIM
KV_WIDTH = KV_HEADS * HEAD_DIM
BLOCK = 128
SSM_WIDTH = 512
SSM_GROUP = 16
SSM_GROUPS = 32
SSM_STATE = 64
SSM_TILES = 4
TILE_CH = SSM_WIDTH // SSM_TILES
TILE_ST = SSM_GROUPS * SSM_STATE // SSM_TILES
TILES_PER_STEP = 2
IN_WIDTH = ATT_WIDTH + 2 * KV_WIDTH + SSM_WIDTH
U_OFF = ATT_WIDTH + 2 * KV_WIDTH
D_FF = 4096
EPS = 1e-6
NEG = float(jnp.finfo(jnp.float32).min)
SLOPES = tuple(2.0 ** (-8.0 * (h + 1) / ATT_HEADS) for h in range(ATT_HEADS))

ADAM_LR, ADAM_B1, ADAM_B2, ADAM_EPS, ADAM_WD, ADAM_STEP = 0.001, 0.9, 0.999, 1e-08, 0.01, 10

VMEM_LIMIT = 48 * 1024 * 1024
MESH = pl.DeviceIdType.MESH

NT = (((1,), (1,)), ((), ()))
TN = (((0,), (0,)), ((), ()))


def _cp(*sem):
    return pltpu.CompilerParams(dimension_semantics=sem, vmem_limit_bytes=VMEM_LIMIT)


def _dot(a, b, dims=None):
    if dims is None:
        return jnp.dot(a, b, preferred_element_type=f32)
    return lax.dot_general(a, b, dims, preferred_element_type=f32)


def _rows8(v):
    return v.reshape(v.shape[0] // 8, 8, v.shape[1]).sum(axis=0)


def _layer_spec(w, l):
    nd = w.ndim
    return pl.BlockSpec((1,) + tuple(w.shape[1:]), lambda i: (l,) + (0,) * (nd - 1))


def _row_spec(tm, width):
    return pl.BlockSpec((tm, width), lambda i: (i, 0))


def norm_mm(x, gain, w, l, *, relu2, name, tm):
    s, d = x.shape
    if relu2:
        nblk, cb = w.shape[1], w.shape[3]
        n = nblk * cb
    else:
        n = w.shape[2]

    def body(x_ref, g_ref, w_ref, h_ref, y_ref):
        xf = x_ref[...]
        r = lax.rsqrt(jnp.mean(xf * xf, axis=-1, keepdims=True) + EPS)
        h = (xf * r * g_ref[...]).astype(MX)
        h_ref[...] = h
        if relu2:
            for b in range(nblk):
                f = jnp.maximum(_dot(h, w_ref[0, b]), 0.0)
                y_ref[:, cb * b:cb * (b + 1)] = (f * f).astype(MX)
        else:
            y_ref[...] = _dot(h, w_ref[0])

    return pl.pallas_call(
        body, grid=(s // tm,),
        in_specs=[_row_spec(tm, d), pl.BlockSpec((1, d), lambda i: (0, 0)), _layer_spec(w, l)],
        out_specs=[_row_spec(tm, d), _row_spec(tm, n)],
        out_shape=[SDS((s, d), MX), SDS((s, n), MX if relu2 else f32)],
        compiler_params=_cp("parallel"), name=name)(x, gain.reshape(1, d), w)


def mm_res(a, w, l, res, *, name, tm):
    s, k = a.shape
    n = w.shape[2]

    def body(a_ref, w_ref, r_ref, o_ref):
        o_ref[...] = r_ref[...] + _dot(a_ref[...], w_ref[0])

    return pl.pallas_call(
        body, grid=(s // tm,), in_specs=[_row_spec(tm, k), _layer_spec(w, l), _row_spec(tm, n)],
        out_specs=_row_spec(tm, n), out_shape=SDS((s, n), f32), compiler_params=_cp("parallel"), name=name)(a, w, res)


def mm_nt(gy, w, l, *, name, tm, a2=None, exchange=()):
    s, n = gy.shape
    k = w.shape[1]
    kb = min(k, 1024)
    nx = len(exchange)
    n_in = 2 if a2 is None else 3
    steps = s // tm

    def body(*refs):
        g_ref, w_ref = refs[0], refs[1]
        o_ref = refs[n_in + nx]
        if nx:
            copies = _chip_copies(refs[n_in:n_in + nx], refs[n_in + nx + 1:n_in + 2 * nx + 1], refs[-2], refs[-1],
                                  [False] * nx)

            @pl.when(pl.program_id(0) == 0)
            def _():
                for cp in copies:
                    cp.start()

        g = g_ref[...]
        for b in range(k // kb):
            cols = slice(kb * b, kb * (b + 1))
            acc = _dot(g, w_ref[0, cols, :], NT)
            if a2 is not None:
                acc = acc * (2.0 * jnp.sqrt(refs[2][:, cols].astype(f32)))
            o_ref[:, cols] = acc.astype(o_ref.dtype)

        if nx:
            @pl.when(pl.program_id(0) == steps - 1)
            def _():
                for cp in copies:
                    cp.wait()

    in_specs = [_row_spec(tm, n), _layer_spec(w, l)]
    args = [gy, w]
    if a2 is not None:
        in_specs.append(_row_spec(tm, k))
        args.append(a2)
    res = pl.pallas_call(
        body, grid=(steps,), in_specs=in_specs + [ANY] * nx, out_specs=[_row_spec(tm, k)] + [ANY] * nx,
        out_shape=[SDS((s, k), f32 if a2 is None else MX)] + [SDS((3,) + tuple(a.shape[1:]), a.dtype) for a in exchange],
        scratch_shapes=[pltpu.SemaphoreType.DMA((4 * nx,)), pltpu.SemaphoreType.DMA((4 * nx,))] if nx else [],
        compiler_params=_cp("arbitrary" if nx else "parallel"), name=name)(*args, *exchange)
    return (res[0], list(res[1:])) if nx else res[0]


def mm_nt_norm(gy, w, l, x, gain, res, *, name, tm):
    s, n = gy.shape
    d = x.shape[1]

    def body(g_ref, w_ref, x_ref, gn_ref, r_ref, o_ref, o16_ref, gg_ref):
        @pl.when(pl.program_id(0) == 0)
        def _():
            gg_ref[...] = jnp.zeros_like(gg_ref)

        if w.ndim == 3:
            gh = _dot(g_ref[...], w_ref[0], NT)
        else:
            cb = w.shape[3]
            gh = _dot(g_ref[:, 0:cb], w_ref[0, 0], NT)
            for b in range(1, w.shape[1]):
                gh = gh + _dot(g_ref[:, cb * b:cb * (b + 1)], w_ref[0, b], NT)
        xf = x_ref[...]
        r = lax.rsqrt(jnp.mean(xf * xf, axis=-1, keepdims=True) + EPS)
        xh = xf * r
        t = gh * gn_ref[...]
        gx = r_ref[...] + r * (t - xh * jnp.mean(t * xh, axis=-1, keepdims=True))
        o_ref[...] = gx
        o16_ref[...] = gx.astype(MX)
        gg_ref[...] += _rows8(gh * xh)

    return pl.pallas_call(
        body, grid=(s // tm,),
        in_specs=[_row_spec(tm, n), _layer_spec(w, l), _row_spec(tm, d), pl.BlockSpec((1, d), lambda i: (0, 0)),
                  _row_spec(tm, d)],
        out_specs=[_row_spec(tm, d), _row_spec(tm, d), pl.BlockSpec((8, d), lambda i: (0, 0))],
        out_shape=[SDS((s, d), f32), SDS((s, d), MX), SDS((8, d), f32)],
        compiler_params=_cp("arbitrary"), name=name)(gy, w, x, gain.reshape(1, d), res)


def mm_tn(xa, gy, *, name, tk, tn, ts, chip_major=False):
    s, k = xa.shape
    n = gy.shape[1]

    def body(x_ref, g_ref, o_ref):
        @pl.when(pl.program_id(2) == 0)
        def _():
            o_ref[...] = jnp.zeros_like(o_ref)

        acc = _dot(x_ref[...], g_ref[...], TN)
        if chip_major:
            o_ref[0] += acc
        else:
            o_ref[...] += acc

    if chip_major:
        out_spec = pl.BlockSpec((1, tk, tn), lambda a, b, c: (b, a, 0))
        out_shape = SDS((n // tn, k, tn), f32)
    else:
        out_spec = pl.BlockSpec((tk, tn), lambda a, b, c: (a, b))
        out_shape = SDS((k, n), f32)
    return pl.pallas_call(
        body, grid=(k // tk, n // tn, s // ts),
        in_specs=[pl.BlockSpec((ts, tk), lambda a, b, c: (c, a)), pl.BlockSpec((ts, tn), lambda a, b, c: (c, b))],
        out_specs=out_spec, out_shape=out_shape,
        compiler_params=_cp("parallel", "parallel", "arbitrary"), name=name)(xa, gy)


def head_mean_matrix(width):
    return jnp.kron(jnp.eye(width // HEAD_DIM, dtype=f32), jnp.full((HEAD_DIM, HEAD_DIM), 1.0 / HEAD_DIM, f32)).astype(MX)


def _head_mean(t, e_ref):
    hi = t.astype(MX)
    lo = (t - hi.astype(f32)).astype(MX)
    return _dot(hi, e_ref[...]) + _dot(lo, e_ref[...])


def qk_prep(z, q_gain, k_gain, eq, ek, *, name, tm):
    s = z.shape[0]

    def body(z_ref, qg_ref, kg_ref, eq_ref, ek_ref, q_ref, kv_ref):
        q = z_ref[:, 0:ATT_WIDTH]
        r = lax.rsqrt(_head_mean(q * q, eq_ref) + EPS)
        q_ref[...] = ((q * r * qg_ref[...]) * 0.125).astype(MX)
        k = z_ref[:, ATT_WIDTH:ATT_WIDTH + KV_WIDTH]
        r = lax.rsqrt(_head_mean(k * k, ek_ref) + EPS)
        kv_ref[:, 0:KV_WIDTH] = (k * r * kg_ref[...]).astype(MX)
        kv_ref[:, KV_WIDTH:] = z_ref[:, ATT_WIDTH + KV_WIDTH:U_OFF].astype(MX)

    const = lambda a: pl.BlockSpec(a.shape, lambda i: (0, 0))
    qg = jnp.tile(q_gain.reshape(1, HEAD_DIM), (1, ATT_HEADS))
    kg = jnp.tile(k_gain.reshape(1, HEAD_DIM), (1, KV_HEADS))
    return pl.pallas_call(
        body, grid=(s // tm,), in_specs=[_row_spec(tm, IN_WIDTH), const(qg), const(kg), const(eq), const(ek)],
        out_specs=[_row_spec(tm, ATT_WIDTH), _row_spec(tm, 2 * KV_WIDTH)],
        out_shape=[SDS((s, ATT_WIDTH), MX), SDS((s, 2 * KV_WIDTH), MX)],
        compiler_params=_cp("parallel"), name=name)(z, qg, kg, eq, ek)


def _attn_mask(i, nb):
    row = lax.broadcasted_iota(jnp.int32, (GQA * BLOCK, 3 * BLOCK), 0) & (BLOCK - 1)
    col = lax.broadcasted_iota(jnp.int32, (GQA * BLOCK, 3 * BLOCK), 1)
    dist = jnp.abs(row - col + BLOCK)
    valid = (dist <= BLOCK) & ((col >= BLOCK) | (i >= 1)) & ((col < 2 * BLOCK) | (i <= nb - 2))
    return dist.astype(f32), valid


def _attn_specs(nb):
    return [pl.BlockSpec((BLOCK, ATT_WIDTH), lambda i: (i, 0)),
            pl.BlockSpec((BLOCK, 2 * KV_WIDTH), lambda i: (jnp.maximum(i - 1, 0), 0)),
            pl.BlockSpec((BLOCK, 2 * KV_WIDTH), lambda i: (i, 0)),
            pl.BlockSpec((BLOCK, 2 * KV_WIDTH), lambda i: (jnp.minimum(i + 1, nb - 1), 0)),
            pl.BlockSpec(memory_space=pltpu.SMEM)]


def _attn_probs(sc, kvh, distf, valid, sink_ref):
    row = lax.broadcasted_iota(jnp.int32, (GQA * BLOCK, 1), 0)
    slope = jnp.full((GQA * BLOCK, 1), SLOPES[GQA * kvh], f32)
    sk = jnp.full((GQA * BLOCK, 1), sink_ref[GQA * kvh], f32)
    for j in range(1, GQA):
        slope = jnp.where(row >= BLOCK * j, SLOPES[GQA * kvh + j], slope)
        sk = jnp.where(row >= BLOCK * j, sink_ref[GQA * kvh + j], sk)
    sg = jnp.where(valid, sc - slope * distf, NEG)
    m = jnp.maximum(jnp.max(sg, axis=-1, keepdims=True), sk)
    e = jnp.exp(sg - m)
    es = jnp.exp(sk - m)
    inv = 1.0 / (jnp.sum(e, axis=-1, keepdims=True) + es)
    return e * inv, es * inv


def _stack_heads(ref, kvh):
    return jnp.concatenate([ref[:, HEAD_DIM * (GQA * kvh + g):HEAD_DIM * (GQA * kvh + g + 1)] for g in range(GQA)],
                           axis=0)


def attn_fwd(qn, kv, sink, *, name):
    s = qn.shape[0]
    nb = s // BLOCK

    def body(q_ref, kp_ref, kc_ref, kn_ref, sink_ref, o_ref):
        i = pl.program_id(0)
        distf, valid = _attn_mask(i, nb)
        kv3 = jnp.concatenate([kp_ref[...], kc_ref[...], kn_ref[...]], axis=0)
        for kvh in range(KV_HEADS):
            kn = kv3[:, HEAD_DIM * kvh:HEAD_DIM * (kvh + 1)]
            vh = kv3[:, KV_WIDTH + HEAD_DIM * kvh:KV_WIDTH + HEAD_DIM * (kvh + 1)]
            sc = _dot(_stack_heads(q_ref, kvh), kn, NT)
            p, _ = _attn_probs(sc, kvh, distf, valid, sink_ref)
            o = _dot(p.astype(MX), vh)
            for g in range(GQA):
                h = GQA * kvh + g
                o_ref[:, HEAD_DIM * h:HEAD_DIM * (h + 1)] = o[BLOCK * g:BLOCK * (g + 1)].astype(o_ref.dtype)

    return pl.pallas_call(
        body, grid=(nb,), in_specs=_attn_specs(nb),
        out_specs=pl.BlockSpec((BLOCK, ATT_WIDTH), lambda i: (i, 0)),
        out_shape=SDS((s, ATT_WIDTH), MX), compiler_params=_cp("parallel"), name=name)(qn, kv, kv, kv, sink)


def attn_bwd(qn, kv, gmix, sink, *, name):
    s = qn.shape[0]
    nb = s // BLOCK

    def body(q_ref, kp_ref, kc_ref, kn_ref, sink_ref, go_ref, gq_ref, dkv_ref, gs_ref):
        i = pl.program_id(0)

        @pl.when(i == 0)
        def _():
            gs_ref[...] = jnp.zeros_like(gs_ref)

        distf, valid = _attn_mask(i, nb)
        kv3 = jnp.concatenate([kp_ref[...], kc_ref[...], kn_ref[...]], axis=0)
        for kvh in range(KV_HEADS):
            kn = kv3[:, HEAD_DIM * kvh:HEAD_DIM * (kvh + 1)]
            vh = kv3[:, KV_WIDTH + HEAD_DIM * kvh:KV_WIDTH + HEAD_DIM * (kvh + 1)]
            qs = _stack_heads(q_ref, kvh)
            dos = _stack_heads(go_ref, kvh).astype(MX)
            p, psink = _attn_probs(_dot(qs, kn, NT), kvh, distf, valid, sink_ref)
            dp = _dot(dos, vh, NT)
            delta = jnp.sum(p * dp, axis=-1, keepdims=True)
            gsk = psink * delta
            for g in range(GQA):
                h = GQA * kvh + g
                gs_ref[h:h + 1, :] -= jnp.broadcast_to(
                    jnp.sum(gsk[BLOCK * g:BLOCK * (g + 1)], axis=0, keepdims=True), (1, 128))
            ds = (p * (dp - delta)).astype(MX)
            gv = _dot(p.astype(MX), dos, TN)
            gkn = _dot(ds, qs, TN)
            gqs = _dot(ds, kn)
            for g in range(GQA):
                h = GQA * kvh + g
                gq_ref[:, HEAD_DIM * h:HEAD_DIM * (h + 1)] = gqs[BLOCK * g:BLOCK * (g + 1)]
            for b in range(3):
                dkv_ref[b, :, HEAD_DIM * kvh:HEAD_DIM * (kvh + 1)] = gkn[BLOCK * b:BLOCK * (b + 1)]
                dkv_ref[b, :, KV_WIDTH + HEAD_DIM * kvh:KV_WIDTH + HEAD_DIM * (kvh + 1)] = gv[BLOCK * b:BLOCK * (b + 1)]

    return pl.pallas_call(
        body, grid=(nb,),
        in_specs=_attn_specs(nb) + [pl.BlockSpec((BLOCK, ATT_WIDTH), lambda i: (i, 0))],
        out_specs=[pl.BlockSpec((BLOCK, ATT_WIDTH), lambda i: (i, 0)),
                   pl.BlockSpec((3, BLOCK, 2 * KV_WIDTH), lambda i: (0, i, 0)),
                   pl.BlockSpec((ATT_HEADS, 128), lambda i: (0, 0))],
        out_shape=[SDS((s, ATT_WIDTH), f32), SDS((3, s, 2 * KV_WIDTH), f32), SDS((ATT_HEADS, 128), f32)],
        compiler_params=_cp("arbitrary"), name=name)(qn, kv, kv, kv, sink, gmix)


def gz_assemble(gqs, dkv, z, q_gain, k_gain, eq, ek, gu_f, gu_r, gy, d_skip, *, name):
    s = z.shape[0]
    nb = s // BLOCK

    def norm_bwd(t_in, g_out, gain_ref, e_ref):
        r = lax.rsqrt(_head_mean(t_in * t_in, e_ref) + EPS)
        hat = t_in * r
        t = g_out * gain_ref[...]
        return r * (t - hat * _head_mean(t * hat, e_ref)), g_out * hat

    def body(gq_ref, d0_ref, d1_ref, d2_ref, z_ref, qg_ref, kg_ref, eq_ref, ek_ref, guf_ref, gur_ref, gy_ref, ds_ref,
             gz_ref, gqg_ref, gkg_ref, gd_ref):
        i = pl.program_id(0)

        @pl.when(i == 0)
        def _():
            gqg_ref[...] = jnp.zeros_like(gqg_ref)
            gkg_ref[...] = jnp.zeros_like(gkg_ref)
            gd_ref[...] = jnp.zeros_like(gd_ref)

        gq, gg = norm_bwd(z_ref[:, 0:ATT_WIDTH], gq_ref[...] * 0.125, qg_ref, eq_ref)
        gz_ref[:, 0:ATT_WIDTH] = gq.astype(MX)
        gqg_ref[...] += _rows8(gg)
        gkv = d1_ref[0] + jnp.where(i + 1 < nb, d0_ref[0], 0.0) + jnp.where(i >= 1, d2_ref[0], 0.0)
        gk, gg = norm_bwd(z_ref[:, ATT_WIDTH:ATT_WIDTH + KV_WIDTH], gkv[:, 0:KV_WIDTH], kg_ref, ek_ref)
        gz_ref[:, ATT_WIDTH:ATT_WIDTH + KV_WIDTH] = gk.astype(MX)
        gkg_ref[...] += _rows8(gg)
        gz_ref[:, ATT_WIDTH + KV_WIDTH:U_OFF] = gkv[:, KV_WIDTH:].astype(MX)
        gyv = gy_ref[...]
        gz_ref[:, U_OFF:IN_WIDTH] = (guf_ref[...] + gur_ref[...] + ds_ref[...] * gyv).astype(MX)
        gd_ref[...] += _rows8(gyv * z_ref[:, U_OFF:IN_WIDTH])

    row = lambda w: pl.BlockSpec((BLOCK, w), lambda i: (i, 0))
    const = lambda a: pl.BlockSpec(a.shape, lambda i: (0, 0))
    qg = jnp.tile(q_gain.reshape(1, HEAD_DIM), (1, ATT_HEADS))
    kg = jnp.tile(k_gain.reshape(1, HEAD_DIM), (1, KV_HEADS))
    return pl.pallas_call(
        body, grid=(nb,),
        in_specs=[row(ATT_WIDTH),
                  pl.BlockSpec((1, BLOCK, 2 * KV_WIDTH), lambda i: (0, jnp.minimum(i + 1, nb - 1), 0)),
                  pl.BlockSpec((1, BLOCK, 2 * KV_WIDTH), lambda i: (1, i, 0)),
                  pl.BlockSpec((1, BLOCK, 2 * KV_WIDTH), lambda i: (2, jnp.maximum(i - 1, 0), 0)),
                  row(IN_WIDTH), const(qg), const(kg), const(eq), const(ek),
                  row(SSM_WIDTH), row(SSM_WIDTH), row(SSM_WIDTH), pl.BlockSpec((1, SSM_WIDTH), lambda i: (0, 0))],
        out_specs=[row(IN_WIDTH), pl.BlockSpec((8, ATT_WIDTH), lambda i: (0, 0)),
                   pl.BlockSpec((8, KV_WIDTH), lambda i: (0, 0)), pl.BlockSpec((8, SSM_WIDTH), lambda i: (0, 0))],
        out_shape=[SDS((s, IN_WIDTH), MX), SDS((8, ATT_WIDTH), f32), SDS((8, KV_WIDTH), f32),
                   SDS((8, SSM_WIDTH), f32)],
        compiler_params=_cp("arbitrary"), name=name)(
            gqs, dkv, dkv, dkv, z, qg, kg, eq, ek, gu_f, gu_r, gy, d_skip.reshape(1, SSM_WIDTH))


def _cmul(ar, ai, xr, xi):
    return ar * xr - ai * xi, ar * xi + ai * xr


def _permute_rows(src_ref, dst_ref, nv):
    for v in range(nv):
        dst_ref[8 * v:8 * v + 8, :] = src_ref[pl.ds(v, 8, stride=nv), :]


def _unpermute_rows(val, dst_ref, nv):
    for v in range(nv):
        dst_ref[pl.ds(v, 8, stride=nv), :] = val[8 * v:8 * v + 8, :]


def _scan_chunk(x_ref, tab_ref, carry_ref, nv, rev, acc=None):
    L = TILE_ST
    order = list(range(nv - 1, -1, -1)) if rev else list(range(nv))
    a_r, a_i = tab_ref[32:40, :L], tab_ref[32:40, L:]
    pr = pi = None
    for v in order:
        rows = slice(8 * v, 8 * v + 8)
        xr, xi = x_ref[rows, :L], x_ref[rows, L:]
        if pr is not None:
            mr, mi = _cmul(a_r, a_i, pr, pi)
            xr, xi = xr + mr, xi + mi
            x_ref[rows, :L] = xr
            x_ref[rows, L:] = xi
        pr, pi = xr, xi
    er, ei = pr, pi
    row = lax.broadcasted_iota(jnp.int32, (8, L), 0)
    edge = row == (7 if rev else 0)
    sh = 7 if rev else 1
    fr = jnp.where(edge, carry_ref[:, :L], pltpu.roll(er, sh, 0))
    fi = jnp.where(edge, carry_ref[:, L:], pltpu.roll(ei, sh, 0))
    for n, k in enumerate((1, 2, 4)):
        mr, mi = tab_ref[8 * n:8 * n + 8, :L], tab_ref[8 * n:8 * n + 8, L:]
        sh = (8 - k) if rev else k
        rr, ri = pltpu.roll(fr, sh, 0), pltpu.roll(fi, sh, 0)
        fr, fi = fr + mr * rr - mi * ri, fi + mr * ri + mi * rr
    dr, di = _cmul(tab_ref[24:32, :L], tab_ref[24:32, L:], fr, fi)
    last = 0 if rev else 7
    carry_ref[:, :L] = jnp.broadcast_to((dr + er)[last:last + 1, :], (8, L))
    carry_ref[:, L:] = jnp.broadcast_to((di + ei)[last:last + 1, :], (8, L))
    qr, qi = fr, fi
    if acc is not None:
        sr, si = jnp.zeros((8, L), f32), jnp.zeros((8, L), f32)
    for v in order:
        rows = slice(8 * v, 8 * v + 8)
        trow = slice(40 + v, 41 + v)
        mr, mi = _cmul(tab_ref[trow, :L], tab_ref[trow, L:], fr, fi)
        xr, xi = x_ref[rows, :L] + mr, x_ref[rows, L:] + mi
        x_ref[rows, :L] = xr
        x_ref[rows, L:] = xi
        if acc is not None:
            gr, gi = acc[0][rows, :L], acc[0][rows, L:]
            sr, si = sr + gr * qr + gi * qi, si + gi * qr - gr * qi
            qr, qi = xr, xi
    if acc is not None:
        acc[1][:, :L] += sr
        acc[1][:, L:] += si


def ssm_fwd(z, tab, bmat, cmat, *, rev, name, chunk):
    s = z.shape[0]
    nc = s // chunk
    nv = chunk // 8
    ci = (lambda i: nc - 1 - i) if rev else (lambda i: i)

    tp = TILES_PER_STEP

    def body(*refs):
        u_refs = refs[:tp]
        tab_ref, b_ref, c_ref, y_ref, xb_ref, u_scr, x_scr, carry = refs[tp:]

        @pl.when(pl.program_id(1) == 0)
        def _():
            carry[...] = jnp.zeros_like(carry)

        for t in range(tp):
            xb_ref[0, :, 2 * TILE_ST * t:2 * TILE_ST * (t + 1)] = carry[t]
            _permute_rows(u_refs[t], u_scr.at[t], nv)
            x_scr[t] = _dot(u_scr[t].astype(MX), b_ref[t])
        for t in range(tp):
            _scan_chunk(x_scr.at[t], tab_ref.at[t], carry.at[t], nv, rev)
        for t in range(tp):
            _unpermute_rows(_dot(x_scr[t].astype(MX), c_ref[t]), u_scr.at[t], nv)
            y_ref[:, TILE_CH * t:TILE_CH * (t + 1)] = u_scr[t]

    u_specs = [pl.BlockSpec((chunk, TILE_CH), lambda j, i, t=t: (ci(i), U_OFF // TILE_CH + tp * j + t))
               for t in range(tp)]
    return pl.pallas_call(
        body, grid=(SSM_TILES // tp, nc),
        in_specs=u_specs + [pl.BlockSpec((tp, 40 + nv, 2 * TILE_ST), lambda j, i: (j, 0, 0)),
                            pl.BlockSpec((tp, TILE_CH, 2 * TILE_ST), lambda j, i: (j, 0, 0)),
                            pl.BlockSpec((tp, 2 * TILE_ST, TILE_CH), lambda j, i: (j, 0, 0))],
        out_specs=[pl.BlockSpec((chunk, tp * TILE_CH), lambda j, i: (ci(i), j)),
                   pl.BlockSpec((1, 8, tp * 2 * TILE_ST), lambda j, i: (ci(i), 0, j))],
        out_shape=[SDS((s, SSM_WIDTH), f32), SDS((nc, 8, SSM_TILES * 2 * TILE_ST), f32)],
        scratch_shapes=[pltpu.VMEM((tp, chunk, TILE_CH), f32), pltpu.VMEM((tp, chunk, 2 * TILE_ST), f32),
                        pltpu.VMEM((tp, 8, 2 * TILE_ST), f32)],
        compiler_params=_cp("parallel", "arbitrary"), name=name)(*([z] * tp), tab, bmat, cmat)


def ssm_bwd(z, gy, xb, tab_s, tab_a, bmat, cmat, *, rev, name, chunk):
    s = z.shape[0]
    nc = s // chunk
    nv = chunk // 8
    ci = (lambda i: i) if rev else (lambda i: nc - 1 - i)

    tp = TILES_PER_STEP
    w2 = 2 * TILE_ST

    def body(*refs):
        u_refs, gy_refs = refs[:tp], refs[tp:2 * tp]
        (xb_ref, ts_ref, ta_ref, b_ref, c_ref, gu_ref, ga_ref, gb_ref, gc_ref,
         u_scr, gy_scr, x_scr, g_scr, gcarry, xcarry) = refs[2 * tp:]

        @pl.when(pl.program_id(1) == 0)
        def _():
            gcarry[...] = jnp.zeros_like(gcarry)
            ga_ref[...] = jnp.zeros_like(ga_ref)
            gb_ref[...] = jnp.zeros_like(gb_ref)
            gc_ref[...] = jnp.zeros_like(gc_ref)

        ub, gyb = [], []
        for t in range(tp):
            _permute_rows(u_refs[t], u_scr.at[t], nv)
            _permute_rows(gy_refs[t], gy_scr.at[t], nv)
            ub.append(u_scr[t].astype(MX))
            gyb.append(gy_scr[t].astype(MX))
        for t in range(tp):
            g_scr[t] = _dot(gyb[t], c_ref[t], NT)
            x_scr[t] = _dot(ub[t], b_ref[t])
            xcarry[t] = xb_ref[0, :, w2 * t:w2 * (t + 1)]
        for t in range(tp):
            _scan_chunk(g_scr.at[t], ta_ref.at[t], gcarry.at[t], nv, not rev)
        for t in range(tp):
            _scan_chunk(x_scr.at[t], ts_ref.at[t], xcarry.at[t], nv, rev,
                        acc=(g_scr.at[t], ga_ref.at[:, pl.ds(w2 * t, w2)]))
            gb16 = g_scr[t].astype(MX)
            gb_ref[t] += _dot(ub[t], gb16, TN)
            gc_ref[t] += _dot(x_scr[t].astype(MX), gyb[t], TN)
            _unpermute_rows(_dot(gb16, b_ref[t], NT), u_scr.at[t], nv)
            gu_ref[:, TILE_CH * t:TILE_CH * (t + 1)] = u_scr[t]

    tile3 = lambda a, b: pl.BlockSpec((tp, a, b), lambda j, i: (j, 0, 0))
    u_specs = [pl.BlockSpec((chunk, TILE_CH), lambda j, i, t=t: (ci(i), U_OFF // TILE_CH + tp * j + t))
               for t in range(tp)]
    gy_specs = [pl.BlockSpec((chunk, TILE_CH), lambda j, i, t=t: (ci(i), tp * j + t)) for t in range(tp)]
    return pl.pallas_call(
        body, grid=(SSM_TILES // tp, nc),
        in_specs=u_specs + gy_specs + [
                  pl.BlockSpec((1, 8, tp * w2), lambda j, i: (ci(i), 0, j)),
                  tile3(40 + nv, w2), tile3(40 + nv, w2), tile3(TILE_CH, w2), tile3(w2, TILE_CH)],
        out_specs=[pl.BlockSpec((chunk, tp * TILE_CH), lambda j, i: (ci(i), j)),
                   pl.BlockSpec((8, tp * w2), lambda j, i: (0, j)),
                   tile3(TILE_CH, w2), tile3(w2, TILE_CH)],
        out_shape=[SDS((s, SSM_WIDTH), f32), SDS((8, SSM_TILES * w2), f32),
                   SDS((SSM_TILES, TILE_CH, w2), f32), SDS((SSM_TILES, w2, TILE_CH), f32)],
        scratch_shapes=[pltpu.VMEM((tp, chunk, TILE_CH), f32), pltpu.VMEM((tp, chunk, TILE_CH), f32),
                        pltpu.VMEM((tp, chunk, w2), f32), pltpu.VMEM((tp, chunk, w2), f32),
                        pltpu.VMEM((tp, 8, w2), f32), pltpu.VMEM((tp, 8, w2), f32)],
        compiler_params=_cp("parallel", "arbitrary"), name=name)(
            *([z] * tp), *([gy] * tp), xb, tab_s, tab_a, bmat, cmat)


GELU_K = math.sqrt(2.0 / math.pi)


def _gelu(y):
    return 0.5 * y * (1.0 + jnp.tanh(GELU_K * (y + 0.044715 * (y * y * y))))


def _gelu_grad(y):
    t = jnp.tanh(GELU_K * (y + 0.044715 * (y * y * y)))
    return 0.5 * (1.0 + t) + 0.5 * y * (1.0 - t * t) * (GELU_K * (1.0 + 3.0 * 0.044715 * (y * y)))


def glu_fwd(y_f, y_r, z, att, d_skip, w_glu, l, *, name, tm):
    s = z.shape[0]
    nblk, cb = w_glu.shape[1], w_glu.shape[3]

    def body(yf_ref, yr_ref, ua_ref, ub_ref, att_ref, d_ref, w_ref, y_ref, gg_ref, mix_ref):
        u = jnp.concatenate([ua_ref[...], ub_ref[...]], axis=1)
        y = d_ref[...] * u + yf_ref[...] + yr_ref[...]
        y_ref[...] = y
        yg = _gelu(y).astype(MX)
        for b in range(nblk):
            gg_ref[:, cb * b:cb * (b + 1)] = _dot(yg, w_ref[0, b])
        mix_ref[:, 0:ATT_WIDTH] = att_ref[...]
        mix_ref[:, ATT_WIDTH:] = (gg_ref[:, :SSM_WIDTH] * jax.nn.sigmoid(gg_ref[:, SSM_WIDTH:])).astype(MX)

    return pl.pallas_call(
        body, grid=(s // tm,),
        in_specs=[_row_spec(tm, SSM_WIDTH), _row_spec(tm, SSM_WIDTH),
                  pl.BlockSpec((tm, SSM_WIDTH // 2), lambda i: (i, U_OFF // (SSM_WIDTH // 2))),
                  pl.BlockSpec((tm, SSM_WIDTH // 2), lambda i: (i, U_OFF // (SSM_WIDTH // 2) + 1)),
                  _row_spec(tm, ATT_WIDTH), pl.BlockSpec((1, SSM_WIDTH), lambda i: (0, 0)), _layer_spec(w_glu, l)],
        out_specs=[_row_spec(tm, SSM_WIDTH), _row_spec(tm, 2 * SSM_WIDTH), _row_spec(tm, D_MODEL)],
        out_shape=[SDS((s, SSM_WIDTH), f32), SDS((s, 2 * SSM_WIDTH), f32), SDS((s, D_MODEL), MX)],
        compiler_params=_cp("parallel"), name=name)(y_f, y_r, z, z, att, d_skip.reshape(1, SSM_WIDTH), w_glu)


def glu_bwd(gmix, gg, ypre, w_glu, l, *, name, tm):
    s = gg.shape[0]
    nblk, cb = w_glu.shape[1], w_glu.shape[3]

    def body(gm_ref, gg_ref, y_ref, w_ref, ggg_ref, yg_ref, gy_ref):
        gs = gm_ref[...]
        val, gate = gg_ref[:, :SSM_WIDTH], gg_ref[:, SSM_WIDTH:]
        sg = jax.nn.sigmoid(gate)
        ggg_ref[:, :SSM_WIDTH] = (gs * sg).astype(MX)
        ggg_ref[:, SSM_WIDTH:] = (gs * val * sg * (1.0 - sg)).astype(MX)
        y = y_ref[...]
        yg_ref[...] = _gelu(y).astype(MX)
        gyg = _dot(ggg_ref[:, 0:cb], w_ref[0, 0], NT)
        for b in range(1, nblk):
            gyg = gyg + _dot(ggg_ref[:, cb * b:cb * (b + 1)], w_ref[0, b], NT)
        gy_ref[...] = gyg * _gelu_grad(y)

    return pl.pallas_call(
        body, grid=(s // tm,),
        in_specs=[pl.BlockSpec((tm, SSM_WIDTH), lambda i: (i, 1)), _row_spec(tm, 2 * SSM_WIDTH),
                  _row_spec(tm, SSM_WIDTH), _layer_spec(w_glu, l)],
        out_specs=[_row_spec(tm, 2 * SSM_WIDTH), _row_spec(tm, SSM_WIDTH), _row_spec(tm, SSM_WIDTH)],
        out_shape=[SDS((s, 2 * SSM_WIDTH), MX), SDS((s, SSM_WIDTH), MX), SDS((s, SSM_WIDTH), f32)],
        compiler_params=_cp("parallel"), name=name)(gmix, gg, ypre, w_glu)


def loss_grad(y, target, *, name, tm):
    s, d = y.shape

    def body(y_ref, t_ref, g_ref, g16_ref, l_ref):
        @pl.when(pl.program_id(0) == 0)
        def _():
            l_ref[...] = jnp.zeros_like(l_ref)

        e = y_ref[...] - t_ref[...]
        g = e * (1.0 / d)
        g_ref[...] = g
        g16_ref[...] = g.astype(MX)
        l_ref[...] += _rows8(e * e)

    row = pl.BlockSpec((tm, d), lambda i: (i, 0))
    return pl.pallas_call(
        body, grid=(s // tm,), in_specs=[row, row],
        out_specs=[row, row, pl.BlockSpec((8, d), lambda i: (0, 0))],
        out_shape=[SDS((s, d), f32), SDS((s, d), MX), SDS((8, d), f32)],
        compiler_params=_cp("arbitrary"), name=name)(y, target)


def _row_tile(rows, cols):
    tr = rows
    while tr * cols > 256 * 1024 and tr % 16 == 0:
        tr //= 2
    return tr


def _elementwise(fn, ins, n_out, *, name, out_dtype=f32):
    shape = ins[0].shape
    cols = shape[-1]
    ins2 = [a.reshape(-1, cols) for a in ins]
    rows = ins2[0].shape[0]
    tr = _row_tile(rows, cols)

    def body(*refs):
        outs = fn(*[r[...] for r in refs[:len(ins)]])
        for o_ref, o in zip(refs[len(ins):], outs):
            o_ref[...] = o.astype(out_dtype)

    spec = pl.BlockSpec((tr, cols), lambda i: (i, 0))
    outs = pl.pallas_call(
        body, grid=(rows // tr,), in_specs=[spec] * len(ins), out_specs=[spec] * n_out,
        out_shape=[SDS((rows, cols), out_dtype)] * n_out, compiler_params=_cp("parallel"), name=name)(*ins2)
    return [o.reshape(shape) for o in outs]


def _adamw_math(w, g, m, v):
    m = ADAM_B1 * m + (1.0 - ADAM_B1) * g
    v = ADAM_B2 * v + (1.0 - ADAM_B2) * (g * g)
    m_hat = m / (1.0 - ADAM_B1 ** ADAM_STEP)
    v_hat = v / (1.0 - ADAM_B2 ** ADAM_STEP)
    delta = -ADAM_LR * (m_hat / (jnp.sqrt(v_hat) + ADAM_EPS) + ADAM_WD * w)
    return delta, m, v


def adamw(w, g, m, v, *, name):
    return _elementwise(_adamw_math, [w, g, m, v], 3, name=name)


SMEM = pl.BlockSpec(memory_space=pltpu.SMEM)


def _core_index():
    return lax.axis_index("c").astype(jnp.int32).reshape(1)


def adamw_halves(w, own, sib, m, v, *, name):
    depth, r, cols = w.shape
    h = r // 2
    tr = _row_tile(h, cols)
    quad = lambda a: a.reshape(depth, 2, h, cols)

    def body(c_ref, w_ref, own_ref, sib_ref, m_ref, v_ref, g_ref, d_ref, mo_ref, vo_ref):
        g = jnp.where(pl.program_id(1) == c_ref[0], own_ref[0], sib_ref[0])
        g_ref[0, 0] = g
        d_ref[0, 0], mo_ref[0, 0], vo_ref[0, 0] = _adamw_math(w_ref[0, 0], g, m_ref[0, 0], v_ref[0, 0])

    full = pl.BlockSpec((1, 1, tr, cols), lambda l, j, i: (l, j, i, 0))
    part = pl.BlockSpec((1, tr, cols), lambda l, j, i: (l, i, 0))
    outs = pl.pallas_call(
        body, grid=(depth, 2, h // tr), in_specs=[SMEM, full, part, part, full, full], out_specs=[full] * 4,
        out_shape=[SDS((depth, 2, h, cols), f32)] * 4,
        compiler_params=_cp("parallel", "parallel", "parallel"), name=name)(
            _core_index(), quad(w), own, sib, quad(m), quad(v))
    return [o.reshape(depth, r, cols) for o in outs]


def add_own_half(g4, recv, *, name):
    _, _, h, cols = g4.shape
    tr = _row_tile(h, cols)

    def body(c_ref, g_ref, r_ref, o_ref):
        own = jnp.where(c_ref[0] == 0, g_ref[0, 0], g_ref[0, 1])
        o_ref[0] = (own + r_ref[0]).astype(WIRE)

    part = pl.BlockSpec((1, tr, cols), lambda s, i: (s, i, 0))
    return pl.pallas_call(
        body, grid=(4, h // tr),
        in_specs=[SMEM, pl.BlockSpec((1, 2, tr, cols), lambda s, i: (s, 0, i, 0)), part], out_specs=part,
        out_shape=SDS((4, h, cols), WIRE), compiler_params=_cp("parallel", "parallel"), name=name)(
            _core_index(), g4, recv)


def _chip_index():
    return (2 * lax.axis_index("x") + lax.axis_index("y")).astype(jnp.int32).reshape(1)


def sum_pieces(sums, got, *, name, into, layer):
    _, h, cols = sums.shape
    tr = _row_tile(h, cols)

    def body(me_ref, s_ref, g_ref, stack_ref, o_ref):
        del stack_ref
        own = s_ref[0]
        for s in range(1, 4):
            own = jnp.where(me_ref[0] == s, s_ref[s], own)
        o_ref[0] = ((own.astype(f32) + g_ref[0].astype(f32)) + g_ref[1].astype(f32)) + g_ref[2].astype(f32)

    return pl.pallas_call(
        body, grid=(h // tr,),
        in_specs=[SMEM, pl.BlockSpec((4, tr, cols), lambda i: (0, i, 0)),
                  pl.BlockSpec((3, tr, cols), lambda i: (0, i, 0)), ANY],
        out_specs=pl.BlockSpec((1, tr, cols), lambda i: (layer, i, 0)),
        out_shape=SDS(into.shape, f32), input_output_aliases={3: 0},
        compiler_params=_cp("parallel"), name=name)(_chip_index(), sums, got, into)


def sum4(a, *, name, into=None, layer=0):
    shape = a.shape[1:]
    cols = shape[-1]
    a2 = a.reshape(4, -1, cols)
    rows = a2.shape[1]
    tr = _row_tile(rows, cols)

    def body(*refs):
        a_ref, o_ref = refs[0], refs[-1]
        tot = ((a_ref[0].astype(f32) + a_ref[1].astype(f32)) + a_ref[2].astype(f32)) + a_ref[3].astype(f32)
        if into is None:
            o_ref[...] = tot
        else:
            o_ref[0] = tot

    in_spec = pl.BlockSpec((4, tr, cols), lambda i: (0, i, 0))
    if into is None:
        out = pl.pallas_call(
            body, grid=(rows // tr,), in_specs=[in_spec], out_specs=pl.BlockSpec((tr, cols), lambda i: (i, 0)),
            out_shape=SDS((rows, cols), f32), compiler_params=_cp("parallel"), name=name)(a2)
        return out.reshape(shape)
    stack = into.reshape(into.shape[0], rows, cols)
    out = pl.pallas_call(
        body, grid=(rows // tr,), in_specs=[in_spec, ANY],
        out_specs=pl.BlockSpec((1, tr, cols), lambda i: (layer, i, 0)),
        out_shape=SDS(stack.shape, f32), input_output_aliases={1: 0},
        compiler_params=_cp("parallel"), name=name)(a2, stack)
    return out.reshape(into.shape)


ANY = pl.BlockSpec(memory_space=pl.ANY)


def _chip_copies(ins, outs, send, recv, bcast):
    x, y, c = lax.axis_index("x"), lax.axis_index("y"), lax.axis_index("c")
    me = 2 * x + y
    copies = []
    for k in range(len(ins)):
        for j, (px, py) in enumerate(((1 - x, y), (x, 1 - y), (1 - x, 1 - y))):
            copies.append(pltpu.make_async_remote_copy(
                src_ref=ins[k] if bcast[k] else ins[k].at[2 * px + py],
                dst_ref=outs[k].at[me] if bcast[k] else outs[k].at[j],
                send_sem=send.at[4 * k + j], recv_sem=recv.at[4 * k + j],
                device_id=(px, py, c), device_id_type=MESH))
        if bcast[k]:
            copies.append(pltpu.make_async_remote_copy(
                src_ref=ins[k], dst_ref=outs[k].at[me], send_sem=send.at[4 * k + 3], recv_sem=recv.at[4 * k + 3],
                device_id=(x, y, 1 - c), device_id_type=MESH))
    return copies


def chip_exchange(arrs, bcast, *, name):
    n = len(arrs)

    def body(*refs):
        copies = _chip_copies(refs[:n], refs[n:2 * n], refs[2 * n], refs[2 * n + 1], bcast)
        for cp in copies:
            cp.start()
        for cp in copies:
            cp.wait()

    return pl.pallas_call(
        body, in_specs=[ANY] * n, out_specs=[ANY] * n,
        out_shape=[SDS((4,) + tuple(a.shape) if b else (3,) + tuple(a.shape[1:]), a.dtype)
                   for a, b in zip(arrs, bcast)],
        scratch_shapes=[pltpu.SemaphoreType.DMA((4 * n,)), pltpu.SemaphoreType.DMA((4 * n,))],
        name=name)(*arrs)


def gather_weights(shards, *, name):
    n = len(shards)
    hd = shards[0].shape[0] // 2

    def body(*refs):
        ins, outs = refs[:n], refs[n:2 * n]
        send, recv = refs[2 * n:]
        x, y, c = lax.axis_index("x"), lax.axis_index("y"), lax.axis_index("c")
        me = 2 * x + y
        chips = ((1 - x, y), (x, 1 - y), (1 - x, 1 - y))
        mine, theirs = pl.ds(c * hd, hd), pl.ds((1 - c) * hd, hd)

        def ici(k, j, src, dst):
            px, py = chips[j]
            return pltpu.make_async_remote_copy(src_ref=src, dst_ref=dst, send_sem=send.at[7 * k + j],
                                                recv_sem=recv.at[7 * k + j], device_id=(px, py, c),
                                                device_id_type=MESH)

        def d2d(k, j, src, dst):
            return pltpu.make_async_remote_copy(src_ref=src, dst_ref=dst, send_sem=send.at[7 * k + 3 + j],
                                                recv_sem=recv.at[7 * k + 3 + j], device_id=(x, y, 1 - c),
                                                device_id_type=MESH)

        own, sent = [], []
        for k in range(n):
            own.append(d2d(k, 3, ins[k], outs[k].at[:, me]))
            own[-1].start()
            for j in range(3):
                sent.append(ici(k, j, ins[k].at[mine], outs[k].at[mine, me]))
                sent[-1].start()
        for k in range(n):
            for j, (px, py) in enumerate(chips):
                landed = outs[k].at[mine, 2 * px + py]
                ici(k, j, landed, landed).wait_recv()
                sent.append(d2d(k, j, landed, landed))
                sent[-1].start()
        for k in range(n):
            for j, (px, py) in enumerate(chips):
                other = outs[k].at[theirs, 2 * px + py]
                d2d(k, j, other, other).wait_recv()
        for cp in sent:
            cp.wait_send()
        for cp in own:
            cp.wait()

    return pl.pallas_call(
        body, in_specs=[ANY] * n, out_specs=[ANY] * n,
        out_shape=[SDS((a.shape[0], 4) + tuple(a.shape[1:]), a.dtype) for a in shards],
        scratch_shapes=[pltpu.SemaphoreType.DMA((7 * n,)), pltpu.SemaphoreType.DMA((7 * n,))],
        name=name)(*shards)


def sibling_exchange(arrs, half, *, name):
    n = len(arrs)
    piece = [(a.shape[0],) + a.shape[2:] if h else a.shape for a, h in zip(arrs, half)]

    def body(*refs):
        ins, outs = refs[:n], refs[n:2 * n]
        send, recv = refs[2 * n:]
        x, y, c = lax.axis_index("x"), lax.axis_index("y"), lax.axis_index("c")
        copies = []
        for k in range(n):
            cp = pltpu.make_async_remote_copy(
                src_ref=ins[k].at[:, 1 - c] if half[k] else ins[k], dst_ref=outs[k],
                send_sem=send.at[k], recv_sem=recv.at[k], device_id=(x, y, 1 - c), device_id_type=MESH)
            cp.start()
            copies.append(cp)
        for cp in copies:
            cp.wait()

    return pl.pallas_call(
        body, in_specs=[ANY] * n, out_specs=[ANY] * n,
        out_shape=[SDS(tuple(p), a.dtype) for p, a in zip(piece, arrs)],
        scratch_shapes=[pltpu.SemaphoreType.DMA((n,)), pltpu.SemaphoreType.DMA((n,))],
        name=name)(*arrs)


def ssm_discretize(lam_re, lam_im, log_dt, b_re, b_im, c_re, c_im):
    dt = jnp.exp(log_dt)[..., None]
    mag = jnp.exp(lam_re * dt)
    abr = mag * jnp.cos(lam_im * dt)
    abi = mag * jnp.sin(lam_im * dt)
    den = lam_re * lam_re + lam_im * lam_im
    zr = ((abr - 1.0) * lam_re + abi * lam_im) / den
    zi = (abi * lam_re - (abr - 1.0) * lam_im) / den
    bbr = zr[..., None] * b_re - zi[..., None] * b_im
    bbi = zr[..., None] * b_im + zi[..., None] * b_re
    eye = jnp.eye(8, dtype=f32)
    bb = jnp.stack([bbr, bbi], axis=1).reshape(2, 2, SSM_TILES, 8, SSM_STATE, SSM_GROUP)
    bmat = jnp.einsum('dqjgph,gk->djghqkp', bb, eye).reshape(2, SSM_TILES, TILE_CH, 2 * TILE_ST)
    cc = jnp.stack([c_re, -c_im], axis=1).reshape(2, 2, SSM_TILES, 8, SSM_GROUP, SSM_STATE)
    cmat = jnp.einsum('dqjghp,gk->djqkpgh', cc, eye).reshape(2, SSM_TILES, 2 * TILE_ST, TILE_CH)
    n = SSM_GROUPS * SSM_STATE
    return abr.reshape(2, n), abi.reshape(2, n), bmat, cmat


def scan_tables(ar, ai, rev, nv):
    pw = [(ar, ai)]
    for _ in range(nv - 1):
        pw.append(_cmul(ar, ai, *pw[-1]))
    big = [pw[nv - 1]]
    big.append(_cmul(*big[0], *big[0]))
    big.append(_cmul(*big[1], *big[1]))
    rows = jnp.arange(8)[:, None]
    ones = jnp.ones((8, 1), f32)
    parts = []
    for k, p in zip((1, 2, 4), big):
        cond = (rows <= 7 - k) if rev else (rows >= k)
        parts.append([jnp.where(cond, q[None, :], 0.0) for q in p])
    parts.append([ones * q[None, :] for q in big[0]])
    parts.append([ones * q[None, :] for q in pw[0]])
    for v in range(nv):
        parts.append([q[None, :] for q in pw[nv - 1 - v if rev else v]])
    nrow = 40 + nv
    tre = jnp.concatenate([p[0] for p in parts], axis=0).reshape(nrow, SSM_TILES, TILE_ST)
    tim = jnp.concatenate([p[1] for p in parts], axis=0).reshape(nrow, SSM_TILES, TILE_ST)
    return jnp.concatenate([tre, tim], axis=-1).transpose(1, 0, 2)


def _tile_a(ga):
    t = ga.sum(axis=0).reshape(SSM_TILES, 2, TILE_ST)
    return t[:, 0].reshape(-1), t[:, 1].reshape(-1)


SMALL = ('norm1', 'q_gain', 'k_gain', 'sink', 'lam_re', 'lam_im', 'log_dt', 'b_re', 'b_im', 'c_re', 'c_im',
         'd_skip', 'norm2')
BIG = ('w_in', 'w_glu', 'w_out', 'w_ff1', 'w_ff2')
WEIGHTS = ('norm1', 'w_in', 'q_gain', 'k_gain', 'sink', 'lam_re', 'lam_im', 'log_dt', 'b_re', 'b_im', 'c_re',
           'c_im', 'd_skip', 'w_glu', 'w_out', 'norm2', 'w_ff1', 'w_ff2')


def _chunk(s):
    return min(512, s)


def layer_forward(l, x, p, wb):
    s = x.shape[0]
    tm = min(512, s)
    sv = {}
    h1, z = norm_mm(x, p['norm1'], wb['w_in'], l, relu2=False, name=f"l{l}_in", tm=tm)
    eq, ek = head_mean_matrix(ATT_WIDTH), head_mean_matrix(KV_WIDTH)
    qn, kv = qk_prep(z, p['q_gain'], p['k_gain'], eq, ek, name=f"l{l}_qk", tm=tm)
    att = attn_fwd(qn, kv, p['sink'], name=f"l{l}_attn")
    sv.update(qn=qn, kv=kv, eq=eq, ek=ek)
    (ar, ai, bmat, cmat), disc_vjp = jax.vjp(
        ssm_discretize, p['lam_re'], p['lam_im'], p['log_dt'], p['b_re'], p['b_im'], p['c_re'], p['c_im'])
    bmat16, cmat16 = bmat.astype(MX), cmat.astype(MX)
    ys, xbs, tabs = [], [], []
    for d, rev in enumerate((False, True)):
        tab = scan_tables(ar[d], ai[d], rev, _chunk(s) // 8)
        y_d, xb_d = ssm_fwd(z, tab, bmat16[d], cmat16[d], rev=rev, name=f"l{l}_ssm{d}", chunk=_chunk(s))
        ys.append(y_d)
        xbs.append(xb_d)
        tabs.append((tab, scan_tables(ar[d], -ai[d], not rev, _chunk(s) // 8)))
    ypre, gg, mix = glu_fwd(ys[0], ys[1], z, att, p['d_skip'], wb['w_glu'], l, name=f"l{l}_glu", tm=min(256, s))
    x1 = mm_res(mix, wb['w_out'], l, x, name=f"l{l}_out", tm=tm)
    h2, a2 = norm_mm(x1, p['norm2'], wb['w_ff1'], l, relu2=True, name=f"l{l}_ff1", tm=tm)
    x2 = mm_res(a2, wb['w_ff2'], l, x1, name=f"l{l}_ff2", tm=tm)
    sv.update(x=x, h1=h1, z=z, xbs=xbs, tabs=tabs, bmat16=bmat16, cmat16=cmat16, disc_vjp=disc_vjp,
              ypre=ypre, gg=gg, mix=mix, x1=x1, h2=h2, a2=a2)
    return x2, sv


def layer_backward(l, gx2, gx2h, p, wb, sv, exchange=()):
    s = gx2.shape[0]
    tm = min(512, s)
    ts = min(1024, s)
    g = {}
    gf = mm_nt(gx2h, wb['w_ff2'], l, name=f"l{l}_bff2", tm=tm, a2=sv['a2'], exchange=exchange)
    got = None
    if exchange:
        gf, got = gf
    g['w_ff2'] = mm_tn(sv['a2'], gx2h, name=f"l{l}_wff2", tk=1024, tn=1024, ts=ts).reshape(4, D_FF // 4, D_MODEL)
    gx1, gx1h, gn2 = mm_nt_norm(gf, wb['w_ff1'], l, sv['x1'], p['norm2'], gx2, name=f"l{l}_bff1", tm=min(256, s))
    g['norm2'] = gn2.sum(axis=0)
    g['w_ff1'] = mm_tn(sv['h2'], gf, name=f"l{l}_wff1", tk=1024, tn=1024, ts=ts, chip_major=True)
    gmix = mm_nt(gx1h, wb['w_out'], l, name=f"l{l}_bout", tm=tm)
    g['w_out'] = mm_tn(sv['mix'], gx1h, name=f"l{l}_wout", tk=1024, tn=1024, ts=ts).reshape(4, D_MODEL // 4, D_MODEL)
    ggg, yg, gy = glu_bwd(gmix, sv['gg'], sv['ypre'], wb['w_glu'], l, name=f"l{l}_bglu", tm=min(256, s))
    g['w_glu'] = mm_tn(yg, ggg, name=f"l{l}_wglu", tk=512, tn=256, ts=ts, chip_major=True)
    gus, gas, gbs, gcs = [], [], [], []
    for d, rev in enumerate((False, True)):
        tab_s, tab_a = sv['tabs'][d]
        gu_d, ga_d, gb_d, gc_d = ssm_bwd(sv['z'], gy, sv['xbs'][d], tab_s, tab_a, sv['bmat16'][d], sv['cmat16'][d],
                                         rev=rev, name=f"l{l}_bssm{d}", chunk=_chunk(s))
        gus.append(gu_d)
        gas.append(_tile_a(ga_d))
        gbs.append(gb_d)
        gcs.append(gc_d)
    gar = jnp.stack([gas[0][0], gas[1][0]])
    gai = jnp.stack([gas[0][1], gas[1][1]])
    (g['lam_re'], g['lam_im'], g['log_dt'], g['b_re'], g['b_im'], g['c_re'], g['c_im']) = sv['disc_vjp'](
        (gar, gai, jnp.stack(gbs), jnp.stack(gcs)))
    gqs, dkv, gsk = attn_bwd(sv['qn'], sv['kv'], gmix, p['sink'], name=f"l{l}_battn")
    g['sink'] = gsk[:, 0]
    gz, gqg, gkg, gd = gz_assemble(gqs, dkv, sv['z'], p['q_gain'], p['k_gain'], sv['eq'], sv['ek'], gus[0], gus[1],
                                   gy, p['d_skip'], name=f"l{l}_gz")
    g['q_gain'] = gqg.sum(axis=0).reshape(ATT_HEADS, HEAD_DIM).sum(axis=0)
    g['k_gain'] = gkg.sum(axis=0).reshape(KV_HEADS, HEAD_DIM).sum(axis=0)
    g['d_skip'] = gd.sum(axis=0)
    gx, gxh, gn1 = mm_nt_norm(gz, wb['w_in'], l, sv['x'], p['norm1'], gx1, name=f"l{l}_bin", tm=tm)
    g['norm1'] = gn1.sum(axis=0)
    gw_in = mm_tn(sv['h1'], gz, name=f"l{l}_win", tk=1024, tn=640, ts=ts)
    g['w_in'] = gw_in.reshape(D_MODEL, 4, IN_WIDTH // 4).transpose(1, 0, 2)
    return gx, gxh, g, got


def stack_layouts(gathered):
    w_in = gathered['w_in']
    depth = w_in.shape[0]
    return dict(w_in=w_in.transpose(0, 2, 1, 3).reshape(depth, D_MODEL, IN_WIDTH),
                w_glu=gathered['w_glu'], w_ff1=gathered['w_ff1'],
                w_out=gathered['w_out'].reshape(depth, D_MODEL, D_MODEL),
                w_ff2=gathered['w_ff2'].reshape(depth, D_FF, D_MODEL))


def local_step(x, target, small, wb, reduce_start=None, reduce_finish=None):
    depth = wb['w_in'].shape[0]
    saves = []
    for l in range(depth):
        p = {k: small[k][l] for k in SMALL}
        x, sv = layer_forward(l, x, p, wb)
        saves.append(sv)
    gx, gxh, lparts = loss_grad(x, target, name="loss", tm=min(512, x.shape[0]))
    grads = [None] * depth
    pending = None
    for l in reversed(range(depth)):
        p = {k: small[k][l] for k in SMALL}
        gx, gxh, g, got = layer_backward(l, gx, gxh, p, wb, saves[l], exchange=pending[1] if pending else ())
        if reduce_start is None:
            grads[l] = g
            continue
        if pending:
            reduce_finish(pending[0], pending[1], got)
        pending = (l, reduce_start(l, g))
        grads[l] = {k: g[k] for k in SMALL}
    if pending:
        reduce_finish(pending[0], pending[1],
                      chip_exchange(pending[1], [False] * len(pending[1]), name="last_rchips"))
    return lparts, gx, grads


def reduce_pairs(l, g):
    arrs = [g[k].reshape(4, 2, g[k].shape[1] // 2, g[k].shape[2]) for k in BIG]
    got = sibling_exchange(arrs, [True] * len(BIG), name=f"l{l}_rsib")
    return [add_own_half(a, b, name=f"l{l}_radd_{k}") for k, a, b in zip(BIG, arrs, got)]


def reduce_chips(l, sums, got, stacks):
    return {k: sum_pieces(a, b, name=f"l{l}_rsum_{k}", into=stacks[k], layer=l) for k, a, b in zip(BIG, sums, got)}


def reduce_small(packed):
    got = sibling_exchange([packed], [False], name="small_rsib")
    pair = _elementwise(lambda a, b: (a + b,), [packed, got[0]], 1, name="small_radd")[0]
    got = chip_exchange([pair], [True], name="small_rchips")
    return sum4(got[0], name="small_rsum")


def _pack_small(tree):
    parts = []
    for k in SMALL:
        flat = tree[k].reshape(-1)
        parts.append(jnp.pad(flat, (0, (-flat.shape[0]) % 1024)).reshape(-1, 128))
    return jnp.concatenate(parts, axis=0)


def _unpack_small(packed, like):
    out, row = {}, 0
    for k in SMALL:
        n = like[k].size
        rows = -(-n // 1024) * 8
        out[k] = packed[row:row + rows].reshape(-1)[:n].reshape(like[k].shape)
        row += rows
    return out


def kernel(x, norm1, w_in, q_gain, k_gain, sink, lam_re, lam_im, log_dt, b_re, b_im, c_re, c_im, d_skip, w_glu, w_out, norm2, w_ff1, w_ff2, loss_target, m_norm1, m_w_in, m_q_gain, m_k_gain, m_sink, m_lam_re, m_lam_im, m_log_dt, m_b_re, m_b_im, m_c_re, m_c_im, m_d_skip, m_w_glu, m_w_out, m_norm2, m_w_ff1, m_w_ff2, v_norm1, v_w_in, v_q_gain, v_k_gain, v_sink, v_lam_re, v_lam_im, v_log_dt, v_b_re, v_b_im, v_c_re, v_c_im, v_d_skip, v_w_glu, v_w_out, v_norm2, v_w_ff1, v_w_ff2):
    w = dict(norm1=norm1, w_in=w_in, q_gain=q_gain, k_gain=k_gain, sink=sink, lam_re=lam_re, lam_im=lam_im,
             log_dt=log_dt, b_re=b_re, b_im=b_im, c_re=c_re, c_im=c_im, d_skip=d_skip, w_glu=w_glu, w_out=w_out,
             norm2=norm2, w_ff1=w_ff1, w_ff2=w_ff2)
    m = dict(norm1=m_norm1, w_in=m_w_in, q_gain=m_q_gain, k_gain=m_k_gain, sink=m_sink, lam_re=m_lam_re,
             lam_im=m_lam_im, log_dt=m_log_dt, b_re=m_b_re, b_im=m_b_im, c_re=m_c_re, c_im=m_c_im,
             d_skip=m_d_skip, w_glu=m_w_glu, w_out=m_w_out, norm2=m_norm2, w_ff1=m_w_ff1, w_ff2=m_w_ff2)
    v = dict(norm1=v_norm1, w_in=v_w_in, q_gain=v_q_gain, k_gain=v_k_gain, sink=v_sink, lam_re=v_lam_re,
             lam_im=v_lam_im, log_dt=v_log_dt, b_re=v_b_re, b_im=v_b_im, c_re=v_c_re, c_im=v_c_im,
             d_skip=v_d_skip, w_glu=v_w_glu, w_out=v_w_out, norm2=v_norm2, w_ff1=v_w_ff1, w_ff2=v_w_ff2)
    depth = w_in.shape[0]

    gathered = gather_weights([w[k].astype(WIRE) for k in BIG], name="gather_w")
    wb = stack_layouts(dict(zip(BIG, gathered)))
    small = {k: w[k] for k in SMALL}

    stacks = [{k: jnp.zeros((depth, w[k].shape[1] // 2, w[k].shape[2]), f32) for k in BIG}]

    def reduce_finish(l, sums, got):
        stacks[0] = reduce_chips(l, sums, got, stacks[0])

    lparts, gx, grads = local_step(x[0], loss_target[0], small, wb, reduce_pairs, reduce_finish)
    loss = lax.psum(0.5 * jnp.sum(lparts) / D_MODEL, ("x", "y", "c"))

    sib = sibling_exchange([stacks[0][k] for k in BIG], [False] * len(BIG), name="reduce_back")
    gsmall = reduce_small(_pack_small({k: jnp.stack([grads[l][k] for l in range(depth)]) for k in SMALL}))
    like = {k: w[k] for k in SMALL}
    gfull = _unpack_small(gsmall, like)

    delta, new_m, new_v = {}, {}, {}
    for k, sib_k in zip(BIG, sib):
        gfull[k], delta[k], new_m[k], new_v[k] = adamw_halves(w[k], stacks[0][k], sib_k, m[k], v[k],
                                                              name=f"adamw_{k}")
    ds, ms, vs = adamw(_pack_small(like), gsmall, _pack_small({k: m[k] for k in SMALL}),
                       _pack_small({k: v[k] for k in SMALL}), name="adamw_small")
    delta.update(_unpack_small(ds, like))
    new_m.update(_unpack_small(ms, like))
    new_v.update(_unpack_small(vs, like))

    return (loss, gx[None], *[gfull[k] for k in WEIGHTS], *[delta[k] for k in WEIGHTS],
            *[new_m[k] for k in WEIGHTS], *[new_v[k] for k in WEIGHTS])
```

```python
import functools
import math

import jax
import jax.numpy as jnp
from jax import lax
from jax.experimental import pallas as pl
from jax.experimental.pallas import tpu as pltpu

f32 = jnp.float32
MX = jnp.bfloat16
WIRE = jnp.bfloat16
SDS = jax.ShapeDtypeStruct

D_MODEL = 1024
DEPTH = 4
ATT_HEADS = 8
KV_HEADS = 2
GQA = ATT_HEADS // KV_HEADS
HEAD_DIM = 64
ATT_WIDTH = ATT_HEADS * HEAD_DIM
KV_WIDTH = KV_HEADS * HEAD_DIM
BLOCK = 128
SSM_WIDTH = 512
SSM_GROUP = 16
SSM_GROUPS = 32
SSM_STATE = 64
SSM_TILES = 4
TILE_CH = SSM_WIDTH // SSM_TILES
TILE_ST = SSM_GROUPS * SSM_STATE // SSM_TILES
TILES_PER_STEP = 2
IN_WIDTH = ATT_WIDTH + 2 * KV_WIDTH + SSM_WIDTH
U_OFF = ATT_WIDTH + 2 * KV_WIDTH
D_FF = 4096
EPS = 1e-6
NEG = float(jnp.finfo(jnp.float32).min)
SLOPES = tuple(2.0 ** (-8.0 * (h + 1) / ATT_HEADS) for h in range(ATT_HEADS))

ADAM_LR, ADAM_B1, ADAM_B2, ADAM_EPS, ADAM_WD, ADAM_STEP = 0.001, 0.9, 0.999, 1e-08, 0.01, 10

VMEM_LIMIT = 48 * 1024 * 1024
MESH = pl.DeviceIdType.MESH

NT = (((1,), (1,)), ((), ()))
TN = (((0,), (0,)), ((), ()))


def _cp(*sem):
    return pltpu.CompilerParams(dimension_semantics=sem, vmem_limit_bytes=VMEM_LIMIT)


def _dot(a, b, dims=None):
    if dims is None:
        return jnp.dot(a, b, preferred_element_type=f32)
    return lax.dot_general(a, b, dims, preferred_element_type=f32)


def _rows8(v):
    return v.reshape(v.shape[0] // 8, 8, v.shape[1]).sum(axis=0)


def _layer_spec(w, l):
    nd = w.ndim
    return pl.BlockSpec((1,) + tuple(w.shape[1:]), lambda i: (l,) + (0,) * (nd - 1))


def _row_spec(tm, width):
    return pl.BlockSpec((tm, width), lambda i: (i, 0))


def _call(body, *, grid, in_specs, out_specs, out_shape, args, sem, name, scratch=(), exchange=None):
    n_in, n_out, n_scr = len(in_specs), len(out_specs), len(scratch)
    if exchange is None:
        res = pl.pallas_call(body, grid=grid, in_specs=in_specs, out_specs=out_specs, out_shape=out_shape,
                             scratch_shapes=list(scratch), compiler_params=_cp(*sem), name=name)(*args)
        return list(res), []
    kind, arrs, flags = exchange
    nx = len(arrs)
    if kind == "chips":
        make, nsem = _chip_copies, 4 * nx
        got = [SDS((4,) + tuple(a.shape) if b else (3,) + tuple(a.shape[1:]), a.dtype) for a, b in zip(arrs, flags)]
    else:
        make, nsem = _sibling_copies, nx
        got = [SDS((a.shape[0],) + tuple(a.shape[2:]) if h else tuple(a.shape), a.dtype)
               for a, h in zip(arrs, flags)]

    def hosted(*refs):
        ins, xin = refs[:n_in], refs[n_in:n_in + nx]
        outs = refs[n_in + nx:n_in + nx + n_out]
        xout = refs[n_in + nx + n_out:n_in + 2 * nx + n_out]
        scr = refs[n_in + 2 * nx + n_out:]
        copies = make(xin, xout, scr[n_scr], scr[n_scr + 1], flags)
        first = functools.reduce(jnp.logical_and, [pl.program_id(d) == 0 for d in range(len(grid))])
        last = functools.reduce(jnp.logical_and, [pl.program_id(d) == grid[d] - 1 for d in range(len(grid))])

        @pl.when(first)
        def _():
            for cp in copies:
                cp.start()

        body(*ins, *outs, *scr[:n_scr])

        @pl.when(last)
        def _():
            for cp in copies:
                cp.wait()

    res = pl.pallas_call(
        hosted, grid=grid, in_specs=list(in_specs) + [ANY] * nx, out_specs=list(out_specs) + [ANY] * nx,
        out_shape=list(out_shape) + got,
        scratch_shapes=list(scratch) + [pltpu.SemaphoreType.DMA((nsem,)), pltpu.SemaphoreType.DMA((nsem,))],
        compiler_params=_cp(*["arbitrary"] * len(grid)), name=name)(*args, *arrs)
    return list(res[:n_out]), list(res[n_out:])


def norm_mm(x, gain, w, l, *, relu2, name, tm):
    s, d = x.shape
    if relu2:
        nblk, cb = w.shape[1], w.shape[3]
        n = nblk * cb
    else:
        n = w.shape[2]

    def body(x_ref, g_ref, w_ref, h_ref, y_ref):
        xf = x_ref[...]
        r = lax.rsqrt(jnp.mean(xf * xf, axis=-1, keepdims=True) + EPS)
        h = (xf * r * g_ref[...]).astype(MX)
        h_ref[...] = h
        if relu2:
            for b in range(nblk):
                f = jnp.maximum(_dot(h, w_ref[0, b]), 0.0)
                y_ref[:, cb * b:cb * (b + 1)] = (f * f).astype(MX)
        else:
            y_ref[...] = _dot(h, w_ref[0])

    return pl.pallas_call(
        body, grid=(s // tm,),
        in_specs=[_row_spec(tm, d), pl.BlockSpec((1, d), lambda i: (0, 0)), _layer_spec(w, l)],
        out_specs=[_row_spec(tm, d), _row_spec(tm, n)],
        out_shape=[SDS((s, d), MX), SDS((s, n), MX if relu2 else f32)],
        compiler_params=_cp("parallel"), name=name)(x, gain.reshape(1, d), w)


def mm_res(a, w, l, res, *, name, tm):
    s, k = a.shape
    n = w.shape[2]

    def body(a_ref, w_ref, r_ref, o_ref):
        o_ref[...] = r_ref[...] + _dot(a_ref[...], w_ref[0])

    return pl.pallas_call(
        body, grid=(s // tm,), in_specs=[_row_spec(tm, k), _layer_spec(w, l), _row_spec(tm, n)],
        out_specs=_row_spec(tm, n), out_shape=SDS((s, n), f32), compiler_params=_cp("parallel"), name=name)(a, w, res)


def mm_nt(gy, w, l, *, name, tm, a2=None, exchange=None):
    s, n = gy.shape
    k = w.shape[1]
    kb = min(k, 1024)

    def body(*refs):
        g_ref, w_ref, o_ref = refs[0], refs[1], refs[-1]
        g = g_ref[...]
        for b in range(k // kb):
            cols = slice(kb * b, kb * (b + 1))
            acc = _dot(g, w_ref[0, cols, :], NT)
            if a2 is not None:
                acc = acc * (2.0 * jnp.sqrt(refs[2][:, cols].astype(f32)))
            o_ref[:, cols] = acc.astype(o_ref.dtype)

    in_specs = [_row_spec(tm, n), _layer_spec(w, l)]
    args = [gy, w]
    if a2 is not None:
        in_specs.append(_row_spec(tm, k))
        args.append(a2)
    (out,), got = _call(body, grid=(s // tm,), in_specs=in_specs, out_specs=[_row_spec(tm, k)],
                        out_shape=[SDS((s, k), f32 if a2 is None else MX)], args=args, sem=("parallel",),
                        name=name, exchange=exchange)
    return out, got


def mm_nt_norm(gy, w, l, x, gain, res, *, name, tm):
    s, n = gy.shape
    d = x.shape[1]

    def body(g_ref, w_ref, x_ref, gn_ref, r_ref, o_ref, o16_ref, gg_ref):
        @pl.when(pl.program_id(0) == 0)
        def _():
            gg_ref[...] = jnp.zeros_like(gg_ref)

        if w.ndim == 3:
            gh = _dot(g_ref[...], w_ref[0], NT)
        else:
            cb = w.shape[3]
            gh = _dot(g_ref[:, 0:cb], w_ref[0, 0], NT)
            for b in range(1, w.shape[1]):
                gh = gh + _dot(g_ref[:, cb * b:cb * (b + 1)], w_ref[0, b], NT)
        xf = x_ref[...]
        r = lax.rsqrt(jnp.mean(xf * xf, axis=-1, keepdims=True) + EPS)
        xh = xf * r
        t = gh * gn_ref[...]
        gx = r_ref[...] + r * (t - xh * jnp.mean(t * xh, axis=-1, keepdims=True))
        o_ref[...] = gx
        o16_ref[...] = gx.astype(MX)
        gg_ref[...] += _rows8(gh * xh)

    return pl.pallas_call(
        body, grid=(s // tm,),
        in_specs=[_row_spec(tm, n), _layer_spec(w, l), _row_spec(tm, d), pl.BlockSpec((1, d), lambda i: (0, 0)),
                  _row_spec(tm, d)],
        out_specs=[_row_spec(tm, d), _row_spec(tm, d), pl.BlockSpec((8, d), lambda i: (0, 0))],
        out_shape=[SDS((s, d), f32), SDS((s, d), MX), SDS((8, d), f32)],
        compiler_params=_cp("arbitrary"), name=name)(gy, w, x, gain.reshape(1, d), res)


def mm_tn(xa, gy, *, name, tk, tn, ts, chip_major=False):
    s, k = xa.shape
    n = gy.shape[1]

    def body(x_ref, g_ref, o_ref):
        @pl.when(pl.program_id(2) == 0)
        def _():
            o_ref[...] = jnp.zeros_like(o_ref)

        acc = _dot(x_ref[...], g_ref[...], TN)
        if chip_major:
            o_ref[0] += acc
        else:
            o_ref[...] += acc

    if chip_major:
        out_spec = pl.BlockSpec((1, tk, tn), lambda a, b, c: (b, a, 0))
        out_shape = SDS((n // tn, k, tn), f32)
    else:
        out_spec = pl.BlockSpec((tk, tn), lambda a, b, c: (a, b))
        out_shape = SDS((k, n), f32)
    return pl.pallas_call(
        body, grid=(k // tk, n // tn, s // ts),
        in_specs=[pl.BlockSpec((ts, tk), lambda a, b, c: (c, a)), pl.BlockSpec((ts, tn), lambda a, b, c: (c, b))],
        out_specs=out_spec, out_shape=out_shape,
        compiler_params=_cp("parallel", "parallel", "arbitrary"), name=name)(xa, gy)


def head_mean_matrix(width):
    return jnp.kron(jnp.eye(width // HEAD_DIM, dtype=f32), jnp.full((HEAD_DIM, HEAD_DIM), 1.0 / HEAD_DIM, f32)).astype(MX)


def _head_mean(t, e_ref):
    hi = t.astype(MX)
    lo = (t - hi.astype(f32)).astype(MX)
    return _dot(hi, e_ref[...]) + _dot(lo, e_ref[...])


def qk_prep(z, q_gain, k_gain, eq, ek, *, name, tm):
    s = z.shape[0]

    def body(z_ref, qg_ref, kg_ref, eq_ref, ek_ref, q_ref, kv_ref):
        q = z_ref[:, 0:ATT_WIDTH]
        r = lax.rsqrt(_head_mean(q * q, eq_ref) + EPS)
        q_ref[...] = ((q * r * qg_ref[...]) * 0.125).astype(MX)
        k = z_ref[:, ATT_WIDTH:ATT_WIDTH + KV_WIDTH]
        r = lax.rsqrt(_head_mean(k * k, ek_ref) + EPS)
        kv_ref[:, 0:KV_WIDTH] = (k * r * kg_ref[...]).astype(MX)
        kv_ref[:, KV_WIDTH:] = z_ref[:, ATT_WIDTH + KV_WIDTH:U_OFF].astype(MX)

    const = lambda a: pl.BlockSpec(a.shape, lambda i: (0, 0))
    qg = jnp.tile(q_gain.reshape(1, HEAD_DIM), (1, ATT_HEADS))
    kg = jnp.tile(k_gain.reshape(1, HEAD_DIM), (1, KV_HEADS))
    return pl.pallas_call(
        body, grid=(s // tm,), in_specs=[_row_spec(tm, IN_WIDTH), const(qg), const(kg), const(eq), const(ek)],
        out_specs=[_row_spec(tm, ATT_WIDTH), _row_spec(tm, 2 * KV_WIDTH)],
        out_shape=[SDS((s, ATT_WIDTH), MX), SDS((s, 2 * KV_WIDTH), MX)],
        compiler_params=_cp("parallel"), name=name)(z, qg, kg, eq, ek)


def _attn_mask(i, nb):
    row = lax.broadcasted_iota(jnp.int32, (GQA * BLOCK, 3 * BLOCK), 0) & (BLOCK - 1)
    col = lax.broadcasted_iota(jnp.int32, (GQA * BLOCK, 3 * BLOCK), 1)
    dist = jnp.abs(row - col + BLOCK)
    valid = (dist <= BLOCK) & ((col >= BLOCK) | (i >= 1)) & ((col < 2 * BLOCK) | (i <= nb - 2))
    return dist.astype(f32), valid


def _attn_specs(nb):
    return [pl.BlockSpec((BLOCK, ATT_WIDTH), lambda i: (i, 0)),
            pl.BlockSpec((BLOCK, 2 * KV_WIDTH), lambda i: (jnp.maximum(i - 1, 0), 0)),
            pl.BlockSpec((BLOCK, 2 * KV_WIDTH), lambda i: (i, 0)),
            pl.BlockSpec((BLOCK, 2 * KV_WIDTH), lambda i: (jnp.minimum(i + 1, nb - 1), 0)),
            pl.BlockSpec(memory_space=pltpu.SMEM)]


def _attn_probs(sc, kvh, distf, valid, sink_ref):
    row = lax.broadcasted_iota(jnp.int32, (GQA * BLOCK, 1), 0)
    slope = jnp.full((GQA * BLOCK, 1), SLOPES[GQA * kvh], f32)
    sk = jnp.full((GQA * BLOCK, 1), sink_ref[GQA * kvh], f32)
    for j in range(1, GQA):
        slope = jnp.where(row >= BLOCK * j, SLOPES[GQA * kvh + j], slope)
        sk = jnp.where(row >= BLOCK * j, sink_ref[GQA * kvh + j], sk)
    sg = jnp.where(valid, sc - slope * distf, NEG)
    m = jnp.maximum(jnp.max(sg, axis=-1, keepdims=True), sk)
    e = jnp.exp(sg - m)
    es = jnp.exp(sk - m)
    inv = 1.0 / (jnp.sum(e, axis=-1, keepdims=True) + es)
    return e * inv, es * inv


def _stack_heads(ref, kvh):
    return jnp.concatenate([ref[:, HEAD_DIM * (GQA * kvh + g):HEAD_DIM * (GQA * kvh + g + 1)] for g in range(GQA)],
                           axis=0)


def attn_fwd(qn, kv, sink, *, name, exchange=None):
    s = qn.shape[0]
    nb = s // BLOCK

    def body(q_ref, kp_ref, kc_ref, kn_ref, sink_ref, o_ref):
        i = pl.program_id(0)
        distf, valid = _attn_mask(i, nb)
        kv3 = jnp.concatenate([kp_ref[...], kc_ref[...], kn_ref[...]], axis=0)
        for kvh in range(KV_HEADS):
            kn = kv3[:, HEAD_DIM * kvh:HEAD_DIM * (kvh + 1)]
            vh = kv3[:, KV_WIDTH + HEAD_DIM * kvh:KV_WIDTH + HEAD_DIM * (kvh + 1)]
            sc = _dot(_stack_heads(q_ref, kvh), kn, NT)
            p, _ = _attn_probs(sc, kvh, distf, valid, sink_ref)
            o = _dot(p.astype(MX), vh)
            for g in range(GQA):
                h = GQA * kvh + g
                o_ref[:, HEAD_DIM * h:HEAD_DIM * (h + 1)] = o[BLOCK * g:BLOCK * (g + 1)].astype(o_ref.dtype)

    (out,), got = _call(body, grid=(nb,), in_specs=_attn_specs(nb),
                        out_specs=[pl.BlockSpec((BLOCK, ATT_WIDTH), lambda i: (i, 0))],
                        out_shape=[SDS((s, ATT_WIDTH), MX)], args=(qn, kv, kv, kv, sink), sem=("parallel",),
                        name=name, exchange=exchange)
    return out, got


def attn_bwd(qn, kv, gmix, sink, *, name):
    s = qn.shape[0]
    nb = s // BLOCK

    def body(q_ref, kp_ref, kc_ref, kn_ref, sink_ref, go_ref, gq_ref, dkv_ref, gs_ref):
        i = pl.program_id(0)

        @pl.when(i == 0)
        def _():
            gs_ref[...] = jnp.zeros_like(gs_ref)

        distf, valid = _attn_mask(i, nb)
        kv3 = jnp.concatenate([kp_ref[...], kc_ref[...], kn_ref[...]], axis=0)
        for kvh in range(KV_HEADS):
            kn = kv3[:, HEAD_DIM * kvh:HEAD_DIM * (kvh + 1)]
            vh = kv3[:, KV_WIDTH + HEAD_DIM * kvh:KV_WIDTH + HEAD_DIM * (kvh + 1)]
            qs = _stack_heads(q_ref, kvh)
            dos = _stack_heads(go_ref, kvh).astype(MX)
            p, psink = _attn_probs(_dot(qs, kn, NT), kvh, distf, valid, sink_ref)
            dp = _dot(dos, vh, NT)
            delta = jnp.sum(p * dp, axis=-1, keepdims=True)
            gsk = psink * delta
            for g in range(GQA):
                h = GQA * kvh + g
                gs_ref[h:h + 1, :] -= jnp.broadcast_to(
                    jnp.sum(gsk[BLOCK * g:BLOCK * (g + 1)], axis=0, keepdims=True), (1, 128))
            ds = (p * (dp - delta)).astype(MX)
            gv = _dot(p.astype(MX), dos, TN)
            gkn = _dot(ds, qs, TN)
            gqs = _dot(ds, kn)
            for g in range(GQA):
                h = GQA * kvh + g
                gq_ref[:, HEAD_DIM * h:HEAD_DIM * (h + 1)] = gqs[BLOCK * g:BLOCK * (g + 1)]
            for b in range(3):
                dkv_ref[b, :, HEAD_DIM * kvh:HEAD_DIM * (kvh + 1)] = gkn[BLOCK * b:BLOCK * (b + 1)]
                dkv_ref[b, :, KV_WIDTH + HEAD_DIM * kvh:KV_WIDTH + HEAD_DIM * (kvh + 1)] = gv[BLOCK * b:BLOCK * (b + 1)]

    return pl.pallas_call(
        body, grid=(nb,),
        in_specs=_attn_specs(nb) + [pl.BlockSpec((BLOCK, ATT_WIDTH), lambda i: (i, 0))],
        out_specs=[pl.BlockSpec((BLOCK, ATT_WIDTH), lambda i: (i, 0)),
                   pl.BlockSpec((3, BLOCK, 2 * KV_WIDTH), lambda i: (0, i, 0)),
                   pl.BlockSpec((ATT_HEADS, 128), lambda i: (0, 0))],
        out_shape=[SDS((s, ATT_WIDTH), f32), SDS((3, s, 2 * KV_WIDTH), f32), SDS((ATT_HEADS, 128), f32)],
        compiler_params=_cp("arbitrary"), name=name)(qn, kv, kv, kv, sink, gmix)


def gz_assemble(gqs, dkv, z, q_gain, k_gain, eq, ek, gu_f, gu_r, gy, d_skip, *, name):
    s = z.shape[0]
    nb = s // BLOCK

    def norm_bwd(t_in, g_out, gain_ref, e_ref):
        r = lax.rsqrt(_head_mean(t_in * t_in, e_ref) + EPS)
        hat = t_in * r
        t = g_out * gain_ref[...]
        return r * (t - hat * _head_mean(t * hat, e_ref)), g_out * hat

    def body(gq_ref, d0_ref, d1_ref, d2_ref, z_ref, qg_ref, kg_ref, eq_ref, ek_ref, guf_ref, gur_ref, gy_ref, ds_ref,
             gz_ref, gqg_ref, gkg_ref, gd_ref):
        i = pl.program_id(0)

        @pl.when(i == 0)
        def _():
            gqg_ref[...] = jnp.zeros_like(gqg_ref)
            gkg_ref[...] = jnp.zeros_like(gkg_ref)
            gd_ref[...] = jnp.zeros_like(gd_ref)

        gq, gg = norm_bwd(z_ref[:, 0:ATT_WIDTH], gq_ref[...] * 0.125, qg_ref, eq_ref)
        gz_ref[:, 0:ATT_WIDTH] = gq.astype(MX)
        gqg_ref[...] += _rows8(gg)
        gkv = d1_ref[0] + jnp.where(i + 1 < nb, d0_ref[0], 0.0) + jnp.where(i >= 1, d2_ref[0], 0.0)
        gk, gg = norm_bwd(z_ref[:, ATT_WIDTH:ATT_WIDTH + KV_WIDTH], gkv[:, 0:KV_WIDTH], kg_ref, ek_ref)
        gz_ref[:, ATT_WIDTH:ATT_WIDTH + KV_WIDTH] = gk.astype(MX)
        gkg_ref[...] += _rows8(gg)
        gz_ref[:, ATT_WIDTH + KV_WIDTH:U_OFF] = gkv[:, KV_WIDTH:].astype(MX)
        gyv = gy_ref[...]
        gz_ref[:, U_OFF:IN_WIDTH] = (guf_ref[...] + gur_ref[...] + ds_ref[...] * gyv).astype(MX)
        gd_ref[...] += _rows8(gyv * z_ref[:, U_OFF:IN_WIDTH])

    row = lambda w: pl.BlockSpec((BLOCK, w), lambda i: (i, 0))
    const = lambda a: pl.BlockSpec(a.shape, lambda i: (0, 0))
    qg = jnp.tile(q_gain.reshape(1, HEAD_DIM), (1, ATT_HEADS))
    kg = jnp.tile(k_gain.reshape(1, HEAD_DIM), (1, KV_HEADS))
    return pl.pallas_call(
        body, grid=(nb,),
        in_specs=[row(ATT_WIDTH),
                  pl.BlockSpec((1, BLOCK, 2 * KV_WIDTH), lambda i: (0, jnp.minimum(i + 1, nb - 1), 0)),
                  pl.BlockSpec((1, BLOCK, 2 * KV_WIDTH), lambda i: (1, i, 0)),
                  pl.BlockSpec((1, BLOCK, 2 * KV_WIDTH), lambda i: (2, jnp.maximum(i - 1, 0), 0)),
                  row(IN_WIDTH), const(qg), const(kg), const(eq), const(ek),
                  row(SSM_WIDTH), row(SSM_WIDTH), row(SSM_WIDTH), pl.BlockSpec((1, SSM_WIDTH), lambda i: (0, 0))],
        out_specs=[row(IN_WIDTH), pl.BlockSpec((8, ATT_WIDTH), lambda i: (0, 0)),
                   pl.BlockSpec((8, KV_WIDTH), lambda i: (0, 0)), pl.BlockSpec((8, SSM_WIDTH), lambda i: (0, 0))],
        out_shape=[SDS((s, IN_WIDTH), MX), SDS((8, ATT_WIDTH), f32), SDS((8, KV_WIDTH), f32),
                   SDS((8, SSM_WIDTH), f32)],
        compiler_params=_cp("arbitrary"), name=name)(
            gqs, dkv, dkv, dkv, z, qg, kg, eq, ek, gu_f, gu_r, gy, d_skip.reshape(1, SSM_WIDTH))


def _cmul(ar, ai, xr, xi):
    return ar * xr - ai * xi, ar * xi + ai * xr


def _permute_rows(src_ref, dst_ref, nv):
    for v in range(nv):
        dst_ref[8 * v:8 * v + 8, :] = src_ref[pl.ds(v, 8, stride=nv), :]


def _unpermute_rows(val, dst_ref, nv):
    for v in range(nv):
        dst_ref[pl.ds(v, 8, stride=nv), :] = val[8 * v:8 * v + 8, :]


def _scan_chunk(x_ref, tab_ref, carry_ref, nv, rev, acc=None):
    L = TILE_ST
    order = list(range(nv - 1, -1, -1)) if rev else list(range(nv))
    a_r, a_i = tab_ref[32:40, :L], tab_ref[32:40, L:]
    pr = pi = None
    for v in order:
        rows = slice(8 * v, 8 * v + 8)
        xr, xi = x_ref[rows, :L], x_ref[rows, L:]
        if pr is not None:
            mr, mi = _cmul(a_r, a_i, pr, pi)
            xr, xi = xr + mr, xi + mi
            x_ref[rows, :L] = xr
            x_ref[rows, L:] = xi
        pr, pi = xr, xi
    er, ei = pr, pi
    row = lax.broadcasted_iota(jnp.int32, (8, L), 0)
    edge = row == (7 if rev else 0)
    sh = 7 if rev else 1
    fr = jnp.where(edge, carry_ref[:, :L], pltpu.roll(er, sh, 0))
    fi = jnp.where(edge, carry_ref[:, L:], pltpu.roll(ei, sh, 0))
    for n, k in enumerate((1, 2, 4)):
        mr, mi = tab_ref[8 * n:8 * n + 8, :L], tab_ref[8 * n:8 * n + 8, L:]
        sh = (8 - k) if rev else k
        rr, ri = pltpu.roll(fr, sh, 0), pltpu.roll(fi, sh, 0)
        fr, fi = fr + mr * rr - mi * ri, fi + mr * ri + mi * rr
    dr, di = _cmul(tab_ref[24:32, :L], tab_ref[24:32, L:], fr, fi)
    last = 0 if rev else 7
    carry_ref[:, :L] = jnp.broadcast_to((dr + er)[last:last + 1, :], (8, L))
    carry_ref[:, L:] = jnp.broadcast_to((di + ei)[last:last + 1, :], (8, L))
    qr, qi = fr, fi
    if acc is not None:
        sr, si = jnp.zeros((8, L), f32), jnp.zeros((8, L), f32)
    for v in order:
        rows = slice(8 * v, 8 * v + 8)
        trow = slice(40 + v, 41 + v)
        mr, mi = _cmul(tab_ref[trow, :L], tab_ref[trow, L:], fr, fi)
        xr, xi = x_ref[rows, :L] + mr, x_ref[rows, L:] + mi
        x_ref[rows, :L] = xr
        x_ref[rows, L:] = xi
        if acc is not None:
            gr, gi = acc[0][rows, :L], acc[0][rows, L:]
            sr, si = sr + gr * qr + gi * qi, si + gi * qr - gr * qi
            qr, qi = xr, xi
    if acc is not None:
        acc[1][:, :L] += sr
        acc[1][:, L:] += si


def ssm_fwd(z, tab, bmat, cmat, *, rev, name, chunk, exchange=None):
    s = z.shape[0]
    nc = s // chunk
    nv = chunk // 8
    ci = (lambda i: nc - 1 - i) if rev else (lambda i: i)

    tp = TILES_PER_STEP

    def body(*refs):
        u_refs = refs[:tp]
        tab_ref, b_ref, c_ref, y_ref, xb_ref, u_scr, x_scr, carry = refs[tp:]

        @pl.when(pl.program_id(1) == 0)
        def _():
            carry[...] = jnp.zeros_like(carry)

        for t in range(tp):
            xb_ref[0, :, 2 * TILE_ST * t:2 * TILE_ST * (t + 1)] = carry[t]
            _permute_rows(u_refs[t], u_scr.at[t], nv)
            x_scr[t] = _dot(u_scr[t].astype(MX), b_ref[t])
        for t in range(tp):
            _scan_chunk(x_scr.at[t], tab_ref.at[t], carry.at[t], nv, rev)
        for t in range(tp):
            _unpermute_rows(_dot(x_scr[t].astype(MX), c_ref[t]), u_scr.at[t], nv)
            y_ref[:, TILE_CH * t:TILE_CH * (t + 1)] = u_scr[t]

    u_specs = [pl.BlockSpec((chunk, TILE_CH), lambda j, i, t=t: (ci(i), U_OFF // TILE_CH + tp * j + t))
               for t in range(tp)]
    (y, xb), got = _call(
        body, grid=(SSM_TILES // tp, nc),
        in_specs=u_specs + [pl.BlockSpec((tp, 40 + nv, 2 * TILE_ST), lambda j, i: (j, 0, 0)),
                            pl.BlockSpec((tp, TILE_CH, 2 * TILE_ST), lambda j, i: (j, 0, 0)),
                            pl.BlockSpec((tp, 2 * TILE_ST, TILE_CH), lambda j, i: (j, 0, 0))],
        out_specs=[pl.BlockSpec((chunk, tp * TILE_CH), lambda j, i: (ci(i), j)),
                   pl.BlockSpec((1, 8, tp * 2 * TILE_ST), lambda j, i: (ci(i), 0, j))],
        out_shape=[SDS((s, SSM_WIDTH), f32), SDS((nc, 8, SSM_TILES * 2 * TILE_ST), f32)],
        scratch=[pltpu.VMEM((tp, chunk, TILE_CH), f32), pltpu.VMEM((tp, chunk, 2 * TILE_ST), f32),
                 pltpu.VMEM((tp, 8, 2 * TILE_ST), f32)],
        args=(*([z] * tp), tab, bmat, cmat), sem=("parallel", "arbitrary"), name=name, exchange=exchange)
    return y, xb, got


def ssm_bwd(z, gy, xb, tab_s, tab_a, bmat, cmat, *, rev, name, chunk, exchange=None):
    s = z.shape[0]
    nc = s // chunk
    nv = chunk // 8
    ci = (lambda i: i) if rev else (lambda i: nc - 1 - i)

    tp = TILES_PER_STEP
    w2 = 2 * TILE_ST

    def body(*refs):
        u_refs, gy_refs = refs[:tp], refs[tp:2 * tp]
        (xb_ref, ts_ref, ta_ref, b_ref, c_ref, gu_ref, ga_ref, gb_ref, gc_ref,
         u_scr, gy_scr, x_scr, g_scr, gcarry, xcarry) = refs[2 * tp:]

        @pl.when(pl.program_id(1) == 0)
        def _():
            gcarry[...] = jnp.zeros_like(gcarry)
            ga_ref[...] = jnp.zeros_like(ga_ref)
            gb_ref[...] = jnp.zeros_like(gb_ref)
            gc_ref[...] = jnp.zeros_like(gc_ref)

        ub, gyb = [], []
        for t in range(tp):
            _permute_rows(u_refs[t], u_scr.at[t], nv)
            _permute_rows(gy_refs[t], gy_scr.at[t], nv)
            ub.append(u_scr[t].astype(MX))
            gyb.append(gy_scr[t].astype(MX))
        for t in range(tp):
            g_scr[t] = _dot(gyb[t], c_ref[t], NT)
            x_scr[t] = _dot(ub[t], b_ref[t])
            xcarry[t] = xb_ref[0, :, w2 * t:w2 * (t + 1)]
        for t in range(tp):
            _scan_chunk(g_scr.at[t], ta_ref.at[t], gcarry.at[t], nv, not rev)
        for t in range(tp):
            _scan_chunk(x_scr.at[t], ts_ref.at[t], xcarry.at[t], nv, rev,
                        acc=(g_scr.at[t], ga_ref.at[:, pl.ds(w2 * t, w2)]))
            gb16 = g_scr[t].astype(MX)
            gb_ref[t] += _dot(ub[t], gb16, TN)
            gc_ref[t] += _dot(x_scr[t].astype(MX), gyb[t], TN)
            _unpermute_rows(_dot(gb16, b_ref[t], NT), u_scr.at[t], nv)
            gu_ref[:, TILE_CH * t:TILE_CH * (t + 1)] = u_scr[t]

    tile3 = lambda a, b: pl.BlockSpec((tp, a, b), lambda j, i: (j, 0, 0))
    u_specs = [pl.BlockSpec((chunk, TILE_CH), lambda j, i, t=t: (ci(i), U_OFF // TILE_CH + tp * j + t))
               for t in range(tp)]
    gy_specs = [pl.BlockSpec((chunk, TILE_CH), lambda j, i, t=t: (ci(i), tp * j + t)) for t in range(tp)]
    outs, got = _call(
        body, grid=(SSM_TILES // tp, nc),
        in_specs=u_specs + gy_specs + [
                  pl.BlockSpec((1, 8, tp * w2), lambda j, i: (ci(i), 0, j)),
                  tile3(40 + nv, w2), tile3(40 + nv, w2), tile3(TILE_CH, w2), tile3(w2, TILE_CH)],
        out_specs=[pl.BlockSpec((chunk, tp * TILE_CH), lambda j, i: (ci(i), j)),
                   pl.BlockSpec((8, tp * w2), lambda j, i: (0, j)),
                   tile3(TILE_CH, w2), tile3(w2, TILE_CH)],
        out_shape=[SDS((s, SSM_WIDTH), f32), SDS((8, SSM_TILES * w2), f32),
                   SDS((SSM_TILES, TILE_CH, w2), f32), SDS((SSM_TILES, w2, TILE_CH), f32)],
        scratch=[pltpu.VMEM((tp, chunk, TILE_CH), f32), pltpu.VMEM((tp, chunk, TILE_CH), f32),
                 pltpu.VMEM((tp, chunk, w2), f32), pltpu.VMEM((tp, chunk, w2), f32),
                 pltpu.VMEM((tp, 8, w2), f32), pltpu.VMEM((tp, 8, w2), f32)],
        args=(*([z] * tp), *([gy] * tp), xb, tab_s, tab_a, bmat, cmat), sem=("parallel", "arbitrary"),
        name=name, exchange=exchange)
    return (*outs, got)


GELU_K = math.sqrt(2.0 / math.pi)


def _gelu(y):
    return 0.5 * y * (1.0 + jnp.tanh(GELU_K * (y + 0.044715 * (y * y * y))))


def _gelu_grad(y):
    t = jnp.tanh(GELU_K * (y + 0.044715 * (y * y * y)))
    return 0.5 * (1.0 + t) + 0.5 * y * (1.0 - t * t) * (GELU_K * (1.0 + 3.0 * 0.044715 * (y * y)))


def glu_fwd(y_f, y_r, z, att, d_skip, w_glu, l, *, name, tm):
    s = z.shape[0]
    nblk, cb = w_glu.shape[1], w_glu.shape[3]

    def body(yf_ref, yr_ref, ua_ref, ub_ref, att_ref, d_ref, w_ref, y_ref, gg_ref, mix_ref):
        u = jnp.concatenate([ua_ref[...], ub_ref[...]], axis=1)
        y = d_ref[...] * u + yf_ref[...] + yr_ref[...]
        y_ref[...] = y
        yg = _gelu(y).astype(MX)
        for b in range(nblk):
            gg_ref[:, cb * b:cb * (b + 1)] = _dot(yg, w_ref[0, b])
        mix_ref[:, 0:ATT_WIDTH] = att_ref[...]
        mix_ref[:, ATT_WIDTH:] = (gg_ref[:, :SSM_WIDTH] * jax.nn.sigmoid(gg_ref[:, SSM_WIDTH:])).astype(MX)

    return pl.pallas_call(
        body, grid=(s // tm,),
        in_specs=[_row_spec(tm, SSM_WIDTH), _row_spec(tm, SSM_WIDTH),
                  pl.BlockSpec((tm, SSM_WIDTH // 2), lambda i: (i, U_OFF // (SSM_WIDTH // 2))),
                  pl.BlockSpec((tm, SSM_WIDTH // 2), lambda i: (i, U_OFF // (SSM_WIDTH // 2) + 1)),
                  _row_spec(tm, ATT_WIDTH), pl.BlockSpec((1, SSM_WIDTH), lambda i: (0, 0)), _layer_spec(w_glu, l)],
        out_specs=[_row_spec(tm, SSM_WIDTH), _row_spec(tm, 2 * SSM_WIDTH), _row_spec(tm, D_MODEL)],
        out_shape=[SDS((s, SSM_WIDTH), f32), SDS((s, 2 * SSM_WIDTH), f32), SDS((s, D_MODEL), MX)],
        compiler_params=_cp("parallel"), name=name)(y_f, y_r, z, z, att, d_skip.reshape(1, SSM_WIDTH), w_glu)


def glu_bwd(gmix, gg, ypre, w_glu, l, *, name, tm):
    s = gg.shape[0]
    nblk, cb = w_glu.shape[1], w_glu.shape[3]

    def body(gm_ref, gg_ref, y_ref, w_ref, ggg_ref, yg_ref, gy_ref):
        gs = gm_ref[...]
        val, gate = gg_ref[:, :SSM_WIDTH], gg_ref[:, SSM_WIDTH:]
        sg = jax.nn.sigmoid(gate)
        ggg_ref[:, :SSM_WIDTH] = (gs * sg).astype(MX)
        ggg_ref[:, SSM_WIDTH:] = (gs * val * sg * (1.0 - sg)).astype(MX)
        y = y_ref[...]
        yg_ref[...] = _gelu(y).astype(MX)
        gyg = _dot(ggg_ref[:, 0:cb], w_ref[0, 0], NT)
        for b in range(1, nblk):
            gyg = gyg + _dot(ggg_ref[:, cb * b:cb * (b + 1)], w_ref[0, b], NT)
        gy_ref[...] = gyg * _gelu_grad(y)

    return pl.pallas_call(
        body, grid=(s // tm,),
        in_specs=[pl.BlockSpec((tm, SSM_WIDTH), lambda i: (i, 1)), _row_spec(tm, 2 * SSM_WIDTH),
                  _row_spec(tm, SSM_WIDTH), _layer_spec(w_glu, l)],
        out_specs=[_row_spec(tm, 2 * SSM_WIDTH), _row_spec(tm, SSM_WIDTH), _row_spec(tm, SSM_WIDTH)],
        out_shape=[SDS((s, 2 * SSM_WIDTH), MX), SDS((s, SSM_WIDTH), MX), SDS((s, SSM_WIDTH), f32)],
        compiler_params=_cp("parallel"), name=name)(gmix, gg, ypre, w_glu)


def loss_grad(y, target, *, name, tm):
    s, d = y.shape

    def body(y_ref, t_ref, g_ref, g16_ref, l_ref):
        @pl.when(pl.program_id(0) == 0)
        def _():
            l_ref[...] = jnp.zeros_like(l_ref)

        e = y_ref[...] - t_ref[...]
        g = e * (1.0 / d)
        g_ref[...] = g
        g16_ref[...] = g.astype(MX)
        l_ref[...] += _rows8(e * e)

    row = pl.BlockSpec((tm, d), lambda i: (i, 0))
    return pl.pallas_call(
        body, grid=(s // tm,), in_specs=[row, row],
        out_specs=[row, row, pl.BlockSpec((8, d), lambda i: (0, 0))],
        out_shape=[SDS((s, d), f32), SDS((s, d), MX), SDS((8, d), f32)],
        compiler_params=_cp("arbitrary"), name=name)(y, target)


def _row_tile(rows, cols):
    tr = rows
    while tr * cols > 256 * 1024 and tr % 16 == 0:
        tr //= 2
    return tr


def _elementwise(fn, ins, n_out, *, name, out_dtype=f32):
    shape = ins[0].shape
    cols = shape[-1]
    ins2 = [a.reshape(-1, cols) for a in ins]
    rows = ins2[0].shape[0]
    tr = _row_tile(rows, cols)

    def body(*refs):
        outs = fn(*[r[...] for r in refs[:len(ins)]])
        for o_ref, o in zip(refs[len(ins):], outs):
            o_ref[...] = o.astype(out_dtype)

    spec = pl.BlockSpec((tr, cols), lambda i: (i, 0))
    outs = pl.pallas_call(
        body, grid=(rows // tr,), in_specs=[spec] * len(ins), out_specs=[spec] * n_out,
        out_shape=[SDS((rows, cols), out_dtype)] * n_out, compiler_params=_cp("parallel"), name=name)(*ins2)
    return [o.reshape(shape) for o in outs]


def _adamw_math(w, g, m, v):
    m = ADAM_B1 * m + (1.0 - ADAM_B1) * g
    v = ADAM_B2 * v + (1.0 - ADAM_B2) * (g * g)
    m_hat = m / (1.0 - ADAM_B1 ** ADAM_STEP)
    v_hat = v / (1.0 - ADAM_B2 ** ADAM_STEP)
    delta = -ADAM_LR * (m_hat / (jnp.sqrt(v_hat) + ADAM_EPS) + ADAM_WD * w)
    return delta, m, v


def adamw(w, g, m, v, *, name):
    return _elementwise(_adamw_math, [w, g, m, v], 3, name=name)


SMEM = pl.BlockSpec(memory_space=pltpu.SMEM)


def _core_index():
    return lax.axis_index("c").astype(jnp.int32).reshape(1)


def adamw_halves(w, own, sib, m, v, *, name):
    depth, r, cols = w.shape
    h = r // 2
    tr = _row_tile(h, cols)
    quad = lambda a: a.reshape(depth, 2, h, cols)

    def body(c_ref, w_ref, own_ref, sib_ref, m_ref, v_ref, g_ref, d_ref, mo_ref, vo_ref):
        g = jnp.where(pl.program_id(1) == c_ref[0], own_ref[0], sib_ref[0])
        g_ref[0, 0] = g
        d_ref[0, 0], mo_ref[0, 0], vo_ref[0, 0] = _adamw_math(w_ref[0, 0], g, m_ref[0, 0], v_ref[0, 0])

    full = pl.BlockSpec((1, 1, tr, cols), lambda l, j, i: (l, j, i, 0))
    part = pl.BlockSpec((1, tr, cols), lambda l, j, i: (l, i, 0))
    outs = pl.pallas_call(
        body, grid=(depth, 2, h // tr), in_specs=[SMEM, full, part, part, full, full], out_specs=[full] * 4,
        out_shape=[SDS((depth, 2, h, cols), f32)] * 4,
        compiler_params=_cp("parallel", "parallel", "parallel"), name=name)(
            _core_index(), quad(w), own, sib, quad(m), quad(v))
    return [o.reshape(depth, r, cols) for o in outs]


def add_own_half(g4, recv, *, name):
    _, _, h, cols = g4.shape
    tr = _row_tile(h, cols)

    def body(c_ref, g_ref, r_ref, o_ref):
        own = jnp.where(c_ref[0] == 0, g_ref[0, 0], g_ref[0, 1])
        o_ref[0] = (own + r_ref[0]).astype(WIRE)

    part = pl.BlockSpec((1, tr, cols), lambda s, i: (s, i, 0))
    return pl.pallas_call(
        body, grid=(4, h // tr),
        in_specs=[SMEM, pl.BlockSpec((1, 2, tr, cols), lambda s, i: (s, 0, i, 0)), part], out_specs=part,
        out_shape=SDS((4, h, cols), WIRE), compiler_params=_cp("parallel", "parallel"), name=name)(
            _core_index(), g4, recv)


def _chip_index():
    return (2 * lax.axis_index("x") + lax.axis_index("y")).astype(jnp.int32).reshape(1)


def sum_pieces(sums, got, *, name, into, layer):
    _, h, cols = sums.shape
    tr = _row_tile(h, cols)

    def body(me_ref, s_ref, g_ref, stack_ref, o_ref):
        del stack_ref
        own = s_ref[0]
        for s in range(1, 4):
            own = jnp.where(me_ref[0] == s, s_ref[s], own)
        o_ref[0] = ((own.astype(f32) + g_ref[0].astype(f32)) + g_ref[1].astype(f32)) + g_ref[2].astype(f32)

    return pl.pallas_call(
        body, grid=(h // tr,),
        in_specs=[SMEM, pl.BlockSpec((4, tr, cols), lambda i: (0, i, 0)),
                  pl.BlockSpec((3, tr, cols), lambda i: (0, i, 0)), ANY],
        out_specs=pl.BlockSpec((1, tr, cols), lambda i: (layer, i, 0)),
        out_shape=SDS(into.shape, f32), input_output_aliases={3: 0},
        compiler_params=_cp("parallel"), name=name)(_chip_index(), sums, got, into)


def sum4(a, *, name, into=None, layer=0):
    shape = a.shape[1:]
    cols = shape[-1]
    a2 = a.reshape(4, -1, cols)
    rows = a2.shape[1]
    tr = _row_tile(rows, cols)

    def body(*refs):
        a_ref, o_ref = refs[0], refs[-1]
        tot = ((a_ref[0].astype(f32) + a_ref[1].astype(f32)) + a_ref[2].astype(f32)) + a_ref[3].astype(f32)
        if into is None:
            o_ref[...] = tot
        else:
            o_ref[0] = tot

    in_spec = pl.BlockSpec((4, tr, cols), lambda i: (0, i, 0))
    if into is None:
        out = pl.pallas_call(
            body, grid=(rows // tr,), in_specs=[in_spec], out_specs=pl.BlockSpec((tr, cols), lambda i: (i, 0)),
            out_shape=SDS((rows, cols), f32), compiler_params=_cp("parallel"), name=name)(a2)
        return out.reshape(shape)
    stack = into.reshape(into.shape[0], rows, cols)
    out = pl.pallas_call(
        body, grid=(rows // tr,), in_specs=[in_spec, ANY],
        out_specs=pl.BlockSpec((1, tr, cols), lambda i: (layer, i, 0)),
        out_shape=SDS(stack.shape, f32), input_output_aliases={1: 0},
        compiler_params=_cp("parallel"), name=name)(a2, stack)
    return out.reshape(into.shape)


ANY = pl.BlockSpec(memory_space=pl.ANY)


def _chip_copies(ins, outs, send, recv, bcast):
    x, y, c = lax.axis_index("x"), lax.axis_index("y"), lax.axis_index("c")
    me = 2 * x + y
    copies = []
    for k in range(len(ins)):
        for j, (px, py) in enumerate(((1 - x, y), (x, 1 - y), (1 - x, 1 - y))):
            copies.append(pltpu.make_async_remote_copy(
                src_ref=ins[k] if bcast[k] else ins[k].at[2 * px + py],
                dst_ref=outs[k].at[me] if bcast[k] else outs[k].at[j],
                send_sem=send.at[4 * k + j], recv_sem=recv.at[4 * k + j],
                device_id=(px, py, c), device_id_type=MESH))
        if bcast[k]:
            copies.append(pltpu.make_async_remote_copy(
                src_ref=ins[k], dst_ref=outs[k].at[me], send_sem=send.at[4 * k + 3], recv_sem=recv.at[4 * k + 3],
                device_id=(x, y, 1 - c), device_id_type=MESH))
    return copies


def chip_exchange(arrs, bcast, *, name):
    n = len(arrs)

    def body(*refs):
        copies = _chip_copies(refs[:n], refs[n:2 * n], refs[2 * n], refs[2 * n + 1], bcast)
        for cp in copies:
            cp.start()
        for cp in copies:
            cp.wait()

    return pl.pallas_call(
        body, in_specs=[ANY] * n, out_specs=[ANY] * n,
        out_shape=[SDS((4,) + tuple(a.shape) if b else (3,) + tuple(a.shape[1:]), a.dtype)
                   for a, b in zip(arrs, bcast)],
        scratch_shapes=[pltpu.SemaphoreType.DMA((4 * n,)), pltpu.SemaphoreType.DMA((4 * n,))],
        name=name)(*arrs)


def gather_weights(shards, *, name):
    n = len(shards)
    hd = shards[0].shape[0] // 2

    def body(*refs):
        ins, outs = refs[:n], refs[n:2 * n]
        send, recv = refs[2 * n:]
        x, y, c = lax.axis_index("x"), lax.axis_index("y"), lax.axis_index("c")
        me = 2 * x + y
        chips = ((1 - x, y), (x, 1 - y), (1 - x, 1 - y))
        mine, theirs = pl.ds(c * hd, hd), pl.ds((1 - c) * hd, hd)

        def ici(k, j, src, dst):
            px, py = chips[j]
            return pltpu.make_async_remote_copy(src_ref=src, dst_ref=dst, send_sem=send.at[7 * k + j],
                                                recv_sem=recv.at[7 * k + j], device_id=(px, py, c),
                                                device_id_type=MESH)

        def d2d(k, j, src, dst):
            return pltpu.make_async_remote_copy(src_ref=src, dst_ref=dst, send_sem=send.at[7 * k + 3 + j],
                                                recv_sem=recv.at[7 * k + 3 + j], device_id=(x, y, 1 - c),
                                                device_id_type=MESH)

        own, sent = [], []
        for k in range(n):
            own.append(d2d(k, 3, ins[k], outs[k].at[:, me]))
            own[-1].start()
            for j in range(3):
                sent.append(ici(k, j, ins[k].at[mine], outs[k].at[mine, me]))
                sent[-1].start()
        for k in range(n):
            for j, (px, py) in enumerate(chips):
                landed = outs[k].at[mine, 2 * px + py]
                ici(k, j, landed, landed).wait_recv()
                sent.append(d2d(k, j, landed, landed))
                sent[-1].start()
        for k in range(n):
            for j, (px, py) in enumerate(chips):
                other = outs[k].at[theirs, 2 * px + py]
                d2d(k, j, other, other).wait_recv()
        for cp in sent:
            cp.wait_send()
        for cp in own:
            cp.wait()

    return pl.pallas_call(
        body, in_specs=[ANY] * n, out_specs=[ANY] * n,
        out_shape=[SDS((a.shape[0], 4) + tuple(a.shape[1:]), a.dtype) for a in shards],
        scratch_shapes=[pltpu.SemaphoreType.DMA((7 * n,)), pltpu.SemaphoreType.DMA((7 * n,))],
        name=name)(*shards)


def _sibling_copies(ins, outs, send, recv, half):
    x, y, c = lax.axis_index("x"), lax.axis_index("y"), lax.axis_index("c")
    return [pltpu.make_async_remote_copy(
        src_ref=ins[k].at[:, 1 - c] if half[k] else ins[k], dst_ref=outs[k], send_sem=send.at[k],
        recv_sem=recv.at[k], device_id=(x, y, 1 - c), device_id_type=MESH) for k in range(len(ins))]


def sibling_exchange(arrs, half, *, name):
    n = len(arrs)
    piece = [(a.shape[0],) + a.shape[2:] if h else a.shape for a, h in zip(arrs, half)]

    def body(*refs):
        copies = _sibling_copies(refs[:n], refs[n:2 * n], refs[2 * n], refs[2 * n + 1], half)
        for cp in copies:
            cp.start()
        for cp in copies:
            cp.wait()

    return pl.pallas_call(
        body, in_specs=[ANY] * n, out_specs=[ANY] * n,
        out_shape=[SDS(tuple(p), a.dtype) for p, a in zip(piece, arrs)],
        scratch_shapes=[pltpu.SemaphoreType.DMA((n,)), pltpu.SemaphoreType.DMA((n,))],
        name=name)(*arrs)


def ssm_discretize(lam_re, lam_im, log_dt, b_re, b_im, c_re, c_im):
    dt = jnp.exp(log_dt)[..., None]
    mag = jnp.exp(lam_re * dt)
    abr = mag * jnp.cos(lam_im * dt)
    abi = mag * jnp.sin(lam_im * dt)
    den = lam_re * lam_re + lam_im * lam_im
    zr = ((abr - 1.0) * lam_re + abi * lam_im) / den
    zi = (abi * lam_re - (abr - 1.0) * lam_im) / den
    bbr = zr[..., None] * b_re - zi[..., None] * b_im
    bbi = zr[..., None] * b_im + zi[..., None] * b_re
    eye = jnp.eye(8, dtype=f32)
    bb = jnp.stack([bbr, bbi], axis=1).reshape(2, 2, SSM_TILES, 8, SSM_STATE, SSM_GROUP)
    bmat = jnp.einsum('dqjgph,gk->djghqkp', bb, eye).reshape(2, SSM_TILES, TILE_CH, 2 * TILE_ST)
    cc = jnp.stack([c_re, -c_im], axis=1).reshape(2, 2, SSM_TILES, 8, SSM_GROUP, SSM_STATE)
    cmat = jnp.einsum('dqjghp,gk->djqkpgh', cc, eye).reshape(2, SSM_TILES, 2 * TILE_ST, TILE_CH)
    n = SSM_GROUPS * SSM_STATE
    return abr.reshape(2, n), abi.reshape(2, n), bmat, cmat


def scan_tables(ar, ai, rev, nv):
    pw = [(ar, ai)]
    for _ in range(nv - 1):
        pw.append(_cmul(ar, ai, *pw[-1]))
    big = [pw[nv - 1]]
    big.append(_cmul(*big[0], *big[0]))
    big.append(_cmul(*big[1], *big[1]))
    rows = jnp.arange(8)[:, None]
    ones = jnp.ones((8, 1), f32)
    parts = []
    for k, p in zip((1, 2, 4), big):
        cond = (rows <= 7 - k) if rev else (rows >= k)
        parts.append([jnp.where(cond, q[None, :], 0.0) for q in p])
    parts.append([ones * q[None, :] for q in big[0]])
    parts.append([ones * q[None, :] for q in pw[0]])
    for v in range(nv):
        parts.append([q[None, :] for q in pw[nv - 1 - v if rev else v]])
    nrow = 40 + nv
    tre = jnp.concatenate([p[0] for p in parts], axis=0).reshape(nrow, SSM_TILES, TILE_ST)
    tim = jnp.concatenate([p[1] for p in parts], axis=0).reshape(nrow, SSM_TILES, TILE_ST)
    return jnp.concatenate([tre, tim], axis=-1).transpose(1, 0, 2)


def _tile_a(ga):
    t = ga.sum(axis=0).reshape(SSM_TILES, 2, TILE_ST)
    return t[:, 0].reshape(-1), t[:, 1].reshape(-1)


SMALL = ('norm1', 'q_gain', 'k_gain', 'sink', 'lam_re', 'lam_im', 'log_dt', 'b_re', 'b_im', 'c_re', 'c_im',
         'd_skip', 'norm2')
BIG = ('w_in', 'w_glu', 'w_out', 'w_ff1', 'w_ff2')
WEIGHTS = ('norm1', 'w_in', 'q_gain', 'k_gain', 'sink', 'lam_re', 'lam_im', 'log_dt', 'b_re', 'b_im', 'c_re',
           'c_im', 'd_skip', 'w_glu', 'w_out', 'norm2', 'w_ff1', 'w_ff2')


def _chunk(s):
    return min(512, s)


FETCH_ATTN = ('w_in', 'w_glu', 'w_out')
FETCH_SSM = ('w_ff1', 'w_ff2')


def layer_forward(l, x, p, wb, li, fetch=None):
    s = x.shape[0]
    tm = min(512, s)
    sv = {}
    h1, z = norm_mm(x, p['norm1'], wb['w_in'], li, relu2=False, name=f"l{l}_in", tm=tm)
    eq, ek = head_mean_matrix(ATT_WIDTH), head_mean_matrix(KV_WIDTH)
    qn, kv = qk_prep(z, p['q_gain'], p['k_gain'], eq, ek, name=f"l{l}_qk", tm=tm)
    att, got = attn_fwd(qn, kv, p['sink'], name=f"l{l}_attn",
                        exchange=fetch and ("chips", [fetch[k] for k in FETCH_ATTN], [True] * len(FETCH_ATTN)))
    fetched = dict(zip(FETCH_ATTN, got)) if fetch else None
    sv.update(qn=qn, kv=kv, eq=eq, ek=ek)
    (ar, ai, bmat, cmat), disc_vjp = jax.vjp(
        ssm_discretize, p['lam_re'], p['lam_im'], p['log_dt'], p['b_re'], p['b_im'], p['c_re'], p['c_im'])
    bmat16, cmat16 = bmat.astype(MX), cmat.astype(MX)
    ys, xbs, tabs = [], [], []
    for d, rev in enumerate((False, True)):
        tab = scan_tables(ar[d], ai[d], rev, _chunk(s) // 8)
        y_d, xb_d, got = ssm_fwd(z, tab, bmat16[d], cmat16[d], rev=rev, name=f"l{l}_ssm{d}", chunk=_chunk(s),
                                 exchange=fetch and ("chips", [fetch[FETCH_SSM[d]]], [True]))
        if fetch:
            fetched[FETCH_SSM[d]] = got[0]
        ys.append(y_d)
        xbs.append(xb_d)
        tabs.append((tab, scan_tables(ar[d], -ai[d], not rev, _chunk(s) // 8)))
    ypre, gg, mix = glu_fwd(ys[0], ys[1], z, att, p['d_skip'], wb['w_glu'], li, name=f"l{l}_glu", tm=min(256, s))
    x1 = mm_res(mix, wb['w_out'], li, x, name=f"l{l}_out", tm=tm)
    h2, a2 = norm_mm(x1, p['norm2'], wb['w_ff1'], li, relu2=True, name=f"l{l}_ff1", tm=tm)
    x2 = mm_res(a2, wb['w_ff2'], li, x1, name=f"l{l}_ff2", tm=tm)
    sv.update(x=x, h1=h1, z=z, xbs=xbs, tabs=tabs, bmat16=bmat16, cmat16=cmat16, disc_vjp=disc_vjp,
              ypre=ypre, gg=gg, mix=mix, x1=x1, h2=h2, a2=a2)
    return x2, sv, fetched


def layer_backward(l, gx2, gx2h, p, wb, li, sv, pend=None):
    s = gx2.shape[0]
    tm = min(512, s)
    ts = min(1024, s)
    g = {}
    gf, got = mm_nt(gx2h, wb['w_ff2'], li, name=f"l{l}_bff2", tm=tm, a2=sv['a2'],
                    exchange=pend and ("sibling", pend[1], [True] * len(pend[1])))
    sums = pend and [add_own_half(a, b, name=f"l{l}_radd_{k}") for k, a, b in zip(BIG, pend[1], got)]
    g['w_ff2'] = mm_tn(sv['a2'], gx2h, name=f"l{l}_wff2", tk=1024, tn=1024, ts=ts).reshape(4, D_FF // 4, D_MODEL)
    gx1, gx1h, gn2 = mm_nt_norm(gf, wb['w_ff1'], li, sv['x1'], p['norm2'], gx2, name=f"l{l}_bff1", tm=min(256, s))
    g['norm2'] = gn2.sum(axis=0)
    g['w_ff1'] = mm_tn(sv['h2'], gf, name=f"l{l}_wff1", tk=1024, tn=1024, ts=ts, chip_major=True)
    gmix, _ = mm_nt(gx1h, wb['w_out'], li, name=f"l{l}_bout", tm=tm)
    g['w_out'] = mm_tn(sv['mix'], gx1h, name=f"l{l}_wout", tk=1024, tn=1024, ts=ts).reshape(4, D_MODEL // 4, D_MODEL)
    ggg, yg, gy = glu_bwd(gmix, sv['gg'], sv['ypre'], wb['w_glu'], li, name=f"l{l}_bglu", tm=min(256, s))
    g['w_glu'] = mm_tn(yg, ggg, name=f"l{l}_wglu", tk=512, tn=256, ts=ts, chip_major=True)
    gus, gas, gbs, gcs = [], [], [], []
    for d, rev in enumerate((False, True)):
        tab_s, tab_a = sv['tabs'][d]
        gu_d, ga_d, gb_d, gc_d, got = ssm_bwd(
            sv['z'], gy, sv['xbs'][d], tab_s, tab_a, sv['bmat16'][d], sv['cmat16'][d], rev=rev,
            name=f"l{l}_bssm{d}", chunk=_chunk(s),
            exchange=(pend and d == 0) and ("chips", sums, [False] * len(sums)) or None)
        if pend and d == 0:
            pend[0](sums, got)
        gus.append(gu_d)
        gas.append(_tile_a(ga_d))
        gbs.append(gb_d)
        gcs.append(gc_d)
    gar = jnp.stack([gas[0][0], gas[1][0]])
    gai = jnp.stack([gas[0][1], gas[1][1]])
    (g['lam_re'], g['lam_im'], g['log_dt'], g['b_re'], g['b_im'], g['c_re'], g['c_im']) = sv['disc_vjp'](
        (gar, gai, jnp.stack(gbs), jnp.stack(gcs)))
    gqs, dkv, gsk = attn_bwd(sv['qn'], sv['kv'], gmix, p['sink'], name=f"l{l}_battn")
    g['sink'] = gsk[:, 0]
    gz, gqg, gkg, gd = gz_assemble(gqs, dkv, sv['z'], p['q_gain'], p['k_gain'], sv['eq'], sv['ek'], gus[0], gus[1],
                                   gy, p['d_skip'], name=f"l{l}_gz")
    g['q_gain'] = gqg.sum(axis=0).reshape(ATT_HEADS, HEAD_DIM).sum(axis=0)
    g['k_gain'] = gkg.sum(axis=0).reshape(KV_HEADS, HEAD_DIM).sum(axis=0)
    g['d_skip'] = gd.sum(axis=0)
    gx, gxh, gn1 = mm_nt_norm(gz, wb['w_in'], li, sv['x'], p['norm1'], gx1, name=f"l{l}_bin", tm=tm)
    g['norm1'] = gn1.sum(axis=0)
    gw_in = mm_tn(sv['h1'], gz, name=f"l{l}_win", tk=1024, tn=640, ts=ts)
    g['w_in'] = gw_in.reshape(D_MODEL, 4, IN_WIDTH // 4).transpose(1, 0, 2)
    return gx, gxh, g


def stack_layouts(gathered):
    w_in = gathered['w_in']
    depth = w_in.shape[0]
    return dict(w_in=w_in.transpose(0, 2, 1, 3).reshape(depth, D_MODEL, IN_WIDTH),
                w_glu=gathered['w_glu'], w_ff1=gathered['w_ff1'],
                w_out=gathered['w_out'].reshape(depth, D_MODEL, D_MODEL),
                w_ff2=gathered['w_ff2'].reshape(depth, D_FF, D_MODEL))


def local_step(x, target, small, wb):
    depth = wb['w_in'].shape[0]
    saves = []
    for l in range(depth):
        x, sv, _ = layer_forward(l, x, {k: small[k][l] for k in SMALL}, wb, l)
        saves.append(sv)
    gx, gxh, lparts = loss_grad(x, target, name="loss", tm=min(512, x.shape[0]))
    grads = [None] * depth
    for l in reversed(range(depth)):
        gx, gxh, grads[l] = layer_backward(l, gx, gxh, {k: small[k][l] for k in SMALL}, wb, l, saves[l])
    return lparts, gx, grads


def reduce_pieces(g):
    return [g[k].reshape(4, 2, g[k].shape[1] // 2, g[k].shape[2]) for k in BIG]


def reduce_chips(l, sums, got, stacks):
    return {k: sum_pieces(a, b, name=f"l{l}_rsum_{k}", into=stacks[k], layer=l) for k, a, b in zip(BIG, sums, got)}


def gather_first(shards):
    halves = [a.reshape(2, a.shape[0] // 2, a.shape[1]) for a in shards]
    got = gather_weights(halves, name="gather_first")
    return [a.transpose(1, 0, 2, 3).reshape(4, 2 * a.shape[2], a.shape[3]) for a in got]


def reduce_small(packed):
    got = sibling_exchange([packed], [False], name="small_rsib")
    pair = _elementwise(lambda a, b: (a + b,), [packed, got[0]], 1, name="small_radd")[0]
    got = chip_exchange([pair], [True], name="small_rchips")
    return sum4(got[0], name="small_rsum")


def _pack_small(tree):
    parts = []
    for k in SMALL:
        flat = tree[k].reshape(-1)
        parts.append(jnp.pad(flat, (0, (-flat.shape[0]) % 1024)).reshape(-1, 128))
    return jnp.concatenate(parts, axis=0)


def _unpack_small(packed, like):
    out, row = {}, 0
    for k in SMALL:
        n = like[k].size
        rows = -(-n // 1024) * 8
        out[k] = packed[row:row + rows].reshape(-1)[:n].reshape(like[k].shape)
        row += rows
    return out


def kernel(x, norm1, w_in, q_gain, k_gain, sink, lam_re, lam_im, log_dt, b_re, b_im, c_re, c_im, d_skip, w_glu, w_out, norm2, w_ff1, w_ff2, loss_target, m_norm1, m_w_in, m_q_gain, m_k_gain, m_sink, m_lam_re, m_lam_im, m_log_dt, m_b_re, m_b_im, m_c_re, m_c_im, m_d_skip, m_w_glu, m_w_out, m_norm2, m_w_ff1, m_w_ff2, v_norm1, v_w_in, v_q_gain, v_k_gain, v_sink, v_lam_re, v_lam_im, v_log_dt, v_b_re, v_b_im, v_c_re, v_c_im, v_d_skip, v_w_glu, v_w_out, v_norm2, v_w_ff1, v_w_ff2):
    w = dict(norm1=norm1, w_in=w_in, q_gain=q_gain, k_gain=k_gain, sink=sink, lam_re=lam_re, lam_im=lam_im,
             log_dt=log_dt, b_re=b_re, b_im=b_im, c_re=c_re, c_im=c_im, d_skip=d_skip, w_glu=w_glu, w_out=w_out,
             norm2=norm2, w_ff1=w_ff1, w_ff2=w_ff2)
    m = dict(norm1=m_norm1, w_in=m_w_in, q_gain=m_q_gain, k_gain=m_k_gain, sink=m_sink, lam_re=m_lam_re,
             lam_im=m_lam_im, log_dt=m_log_dt, b_re=m_b_re, b_im=m_b_im, c_re=m_c_re, c_im=m_c_im,
             d_skip=m_d_skip, w_glu=m_w_glu, w_out=m_w_out, norm2=m_norm2, w_ff1=m_w_ff1, w_ff2=m_w_ff2)
    v = dict(norm1=v_norm1, w_in=v_w_in, q_gain=v_q_gain, k_gain=v_k_gain, sink=v_sink, lam_re=v_lam_re,
             lam_im=v_lam_im, log_dt=v_log_dt, b_re=v_b_re, b_im=v_b_im, c_re=v_c_re, c_im=v_c_im,
             d_skip=v_d_skip, w_glu=v_w_glu, w_out=v_w_out, norm2=v_norm2, w_ff1=v_w_ff1, w_ff2=v_w_ff2)
    depth = w_in.shape[0]

    shards = {k: w[k].astype(WIRE) for k in BIG}
    small = {k: w[k] for k in SMALL}
    stacks = [{k: jnp.zeros((depth, w[k].shape[1] // 2, w[k].shape[2]), f32) for k in BIG}]

    xs = x[0]
    gathered = dict(zip(BIG, gather_first([shards[k][0] for k in BIG])))
    saves, wbs = [], []
    for l in range(depth):
        wbs.append(stack_layouts({k: gathered[k][None] for k in BIG}))
        fetch = {k: shards[k][l + 1] for k in BIG} if l + 1 < depth else None
        xs, sv, gathered = layer_forward(l, xs, {k: small[k][l] for k in SMALL}, wbs[l], 0, fetch)
        saves.append(sv)
    gx, gxh, lparts = loss_grad(xs, loss_target[0], name="loss", tm=min(512, xs.shape[0]))
    loss = lax.psum(0.5 * jnp.sum(lparts) / D_MODEL, ("x", "y", "c"))

    def finisher(l):
        def finish(sums, got):
            stacks[0] = reduce_chips(l, sums, got, stacks[0])
        return finish

    grads, pend = [None] * depth, None
    for l in reversed(range(depth)):
        gx, gxh, g = layer_backward(l, gx, gxh, {k: small[k][l] for k in SMALL}, wbs[l], 0, saves[l], pend)
        pend = (finisher(l), reduce_pieces(g))
        grads[l] = {k: g[k] for k in SMALL}
    got = sibling_exchange(pend[1], [True] * len(BIG), name="last_rsib")
    sums = [add_own_half(a, b, name=f"last_radd_{k}") for k, a, b in zip(BIG, pend[1], got)]
    pend[0](sums, chip_exchange(sums, [False] * len(BIG), name="last_rchips"))

    sib = sibling_exchange([stacks[0][k] for k in BIG], [False] * len(BIG), name="reduce_back")
    gsmall = reduce_small(_pack_small({k: jnp.stack([grads[l][k] for l in range(depth)]) for k in SMALL}))
    like = {k: w[k] for k in SMALL}
    gfull = _unpack_small(gsmall, like)

    delta, new_m, new_v = {}, {}, {}
    for k, sib_k in zip(BIG, sib):
        gfull[k], delta[k], new_m[k], new_v[k] = adamw_halves(w[k], stacks[0][k], sib_k, m[k], v[k],
                                                              name=f"adamw_{k}")
    ds, ms, vs = adamw(_pack_small(like), gsmall, _pack_small({k: m[k] for k in SMALL}),
                       _pack_small({k: v[k] for k in SMALL}), name="adamw_small")
    delta.update(_unpack_small(ds, like))
    new_m.update(_unpack_small(ms, like))
    new_v.update(_unpack_small(vs, like))

    return (loss, gx[None], *[gfull[k] for k in WEIGHTS], *[delta[k] for k in WEIGHTS],
            *[new_m[k] for k in WEIGHTS], *[new_v[k] for k in WEIGHTS])
```

```python
import functools
import math

import jax
import jax.numpy as jnp
from jax import lax
from jax.experimental import pallas as pl
from jax.experimental.pallas import tpu as pltpu

f32 = jnp.float32
MX = jnp.bfloat16
WIRE = jnp.bfloat16
SDS = jax.ShapeDtypeStruct

D_MODEL = 1024
DEPTH = 4
ATT_HEADS = 8
KV_HEADS = 2
GQA = ATT_HEADS // KV_HEADS
HEAD_DIM = 64
ATT_WIDTH = ATT_HEADS * HEAD_DIM
KV_WIDTH = KV_HEADS * HEAD_DIM
BLOCK = 128
SSM_WIDTH = 512
SSM_GROUP = 16
SSM_GROUPS = 32
SSM_STATE = 64
SSM_TILES = 4
TILE_CH = SSM_WIDTH // SSM_TILES
TILE_ST = SSM_GROUPS * SSM_STATE // SSM_TILES
TILES_PER_STEP = 2
IN_WIDTH = ATT_WIDTH + 2 * KV_WIDTH + SSM_WIDTH
U_OFF = ATT_WIDTH + 2 * KV_WIDTH
D_FF = 4096
EPS = 1e-6
NEG = float(jnp.finfo(jnp.float32).min)
SLOPES = tuple(2.0 ** (-8.0 * (h + 1) / ATT_HEADS) for h in range(ATT_HEADS))

ADAM_LR, ADAM_B1, ADAM_B2, ADAM_EPS, ADAM_WD, ADAM_STEP = 0.001, 0.9, 0.999, 1e-08, 0.01, 10

VMEM_LIMIT = 48 * 1024 * 1024
MESH = pl.DeviceIdType.MESH

NT = (((1,), (1,)), ((), ()))
TN = (((0,), (0,)), ((), ()))


def _cp(*sem):
    return pltpu.CompilerParams(dimension_semantics=sem, vmem_limit_bytes=VMEM_LIMIT)


def _dot(a, b, dims=None):
    if dims is None:
        return jnp.dot(a, b, preferred_element_type=f32)
    return lax.dot_general(a, b, dims, preferred_element_type=f32)


def _rows8(v):
    return v.reshape(v.shape[0] // 8, 8, v.shape[1]).sum(axis=0)


def _layer_spec(w, l):
    nd = w.ndim
    return pl.BlockSpec((1,) + tuple(w.shape[1:]), lambda i: (l,) + (0,) * (nd - 1))


def _row_spec(tm, width):
    return pl.BlockSpec((tm, width), lambda i: (i, 0))


def _call(body, *, grid, in_specs, out_specs, out_shape, args, sem, name, scratch=(), exchange=None):
    n_in, n_out, n_scr = len(in_specs), len(out_specs), len(scratch)
    if exchange is None:
        res = pl.pallas_call(body, grid=grid, in_specs=in_specs, out_specs=out_specs, out_shape=out_shape,
                             scratch_shapes=list(scratch), compiler_params=_cp(*sem), name=name)(*args)
        return list(res), []
    kind, arrs, flags = exchange
    nx = len(arrs)
    if kind == "chips":
        make, nsem = _chip_copies, 4 * nx
        got = [SDS((4,) + tuple(a.shape) if b else (3,) + tuple(a.shape[1:]), a.dtype) for a, b in zip(arrs, flags)]
    else:
        make, nsem = _sibling_copies, nx
        got = [SDS((a.shape[0],) + tuple(a.shape[2:]) if h else tuple(a.shape), a.dtype)
               for a, h in zip(arrs, flags)]

    def hosted(*refs):
        ins, xin = refs[:n_in], refs[n_in:n_in + nx]
        outs = refs[n_in + nx:n_in + nx + n_out]
        xout = refs[n_in + nx + n_out:n_in + 2 * nx + n_out]
        scr = refs[n_in + 2 * nx + n_out:]
        copies = make(xin, xout, scr[n_scr], scr[n_scr + 1], flags)
        first = functools.reduce(jnp.logical_and, [pl.program_id(d) == 0 for d in range(len(grid))])
        last = functools.reduce(jnp.logical_and, [pl.program_id(d) == grid[d] - 1 for d in range(len(grid))])

        @pl.when(first)
        def _():
            for cp in copies:
                cp.start()

        body(*ins, *outs, *scr[:n_scr])

        @pl.when(last)
        def _():
            for cp in copies:
                cp.wait()

    res = pl.pallas_call(
        hosted, grid=grid, in_specs=list(in_specs) + [ANY] * nx, out_specs=list(out_specs) + [ANY] * nx,
        out_shape=list(out_shape) + got,
        scratch_shapes=list(scratch) + [pltpu.SemaphoreType.DMA((nsem,)), pltpu.SemaphoreType.DMA((nsem,))],
        compiler_params=_cp(*["arbitrary"] * len(grid)), name=name)(*args, *arrs)
    return list(res[:n_out]), list(res[n_out:])


def norm_mm(x, gain, w, l, *, relu2, name, tm):
    s, d = x.shape
    if relu2:
        nblk, cb = w.shape[1], w.shape[3]
        n = nblk * cb
    else:
        n = w.shape[2]

    def body(x_ref, g_ref, w_ref, h_ref, y_ref):
        xf = x_ref[...]
        r = lax.rsqrt(jnp.mean(xf * xf, axis=-1, keepdims=True) + EPS)
        h = (xf * r * g_ref[...]).astype(MX)
        h_ref[...] = h
        if relu2:
            for b in range(nblk):
                f = jnp.maximum(_dot(h, w_ref[0, b]), 0.0)
                y_ref[:, cb * b:cb * (b + 1)] = (f * f).astype(MX)
        else:
            y_ref[...] = _dot(h, w_ref[0])

    return pl.pallas_call(
        body, grid=(s // tm,),
        in_specs=[_row_spec(tm, d), pl.BlockSpec((1, d), lambda i: (0, 0)), _layer_spec(w, l)],
        out_specs=[_row_spec(tm, d), _row_spec(tm, n)],
        out_shape=[SDS((s, d), MX), SDS((s, n), MX if relu2 else f32)],
        compiler_params=_cp("parallel"), name=name)(x, gain.reshape(1, d), w)


def mm_res(a, w, l, res, *, name, tm):
    s, k = a.shape
    n = w.shape[2]

    def body(a_ref, w_ref, r_ref, o_ref):
        o_ref[...] = r_ref[...] + _dot(a_ref[...], w_ref[0])

    return pl.pallas_call(
        body, grid=(s // tm,), in_specs=[_row_spec(tm, k), _layer_spec(w, l), _row_spec(tm, n)],
        out_specs=_row_spec(tm, n), out_shape=SDS((s, n), f32), compiler_params=_cp("parallel"), name=name)(a, w, res)


def mm_nt(gy, w, l, *, name, tm, a2=None, exchange=None):
    s, n = gy.shape
    k = w.shape[1]
    kb = min(k, 1024)

    def body(*refs):
        g_ref, w_ref, o_ref = refs[0], refs[1], refs[-1]
        g = g_ref[...]
        for b in range(k // kb):
            cols = slice(kb * b, kb * (b + 1))
            acc = _dot(g, w_ref[0, cols, :], NT)
            if a2 is not None:
                acc = acc * (2.0 * jnp.sqrt(refs[2][:, cols].astype(f32)))
            o_ref[:, cols] = acc.astype(o_ref.dtype)

    in_specs = [_row_spec(tm, n), _layer_spec(w, l)]
    args = [gy, w]
    if a2 is not None:
        in_specs.append(_row_spec(tm, k))
        args.append(a2)
    (out,), got = _call(body, grid=(s // tm,), in_specs=in_specs, out_specs=[_row_spec(tm, k)],
                        out_shape=[SDS((s, k), f32 if a2 is None else MX)], args=args, sem=("parallel",),
                        name=name, exchange=exchange)
    return out, got


def mm_nt_norm(gy, w, l, x, gain, res, *, name, tm):
    s, n = gy.shape
    d = x.shape[1]

    def body(g_ref, w_ref, x_ref, gn_ref, r_ref, o_ref, o16_ref, gg_ref):
        @pl.when(pl.program_id(0) == 0)
        def _():
            gg_ref[...] = jnp.zeros_like(gg_ref)

        if w.ndim == 3:
            gh = _dot(g_ref[...], w_ref[0], NT)
        else:
            cb = w.shape[3]
            gh = _dot(g_ref[:, 0:cb], w_ref[0, 0], NT)
            for b in range(1, w.shape[1]):
                gh = gh + _dot(g_ref[:, cb * b:cb * (b + 1)], w_ref[0, b], NT)
        xf = x_ref[...]
        r = lax.rsqrt(jnp.mean(xf * xf, axis=-1, keepdims=True) + EPS)
        xh = xf * r
        t = gh * gn_ref[...]
        gx = r_ref[...] + r * (t - xh * jnp.mean(t * xh, axis=-1, keepdims=True))
        o_ref[...] = gx
        o16_ref[...] = gx.astype(MX)
        gg_ref[...] += _rows8(gh * xh)

    return pl.pallas_call(
        body, grid=(s // tm,),
        in_specs=[_row_spec(tm, n), _layer_spec(w, l), _row_spec(tm, d), pl.BlockSpec((1, d), lambda i: (0, 0)),
                  _row_spec(tm, d)],
        out_specs=[_row_spec(tm, d), _row_spec(tm, d), pl.BlockSpec((8, d), lambda i: (0, 0))],
        out_shape=[SDS((s, d), f32), SDS((s, d), MX), SDS((8, d), f32)],
        compiler_params=_cp("arbitrary"), name=name)(gy, w, x, gain.reshape(1, d), res)


def mm_tn(xa, gy, *, name, tk, tn, ts, chip_major=False):
    s, k = xa.shape
    n = gy.shape[1]

    def body(x_ref, g_ref, o_ref):
        @pl.when(pl.program_id(2) == 0)
        def _():
            o_ref[...] = jnp.zeros_like(o_ref)

        acc = _dot(x_ref[...], g_ref[...], TN)
        if chip_major:
            o_ref[0] += acc
        else:
            o_ref[...] += acc

    if chip_major:
        out_spec = pl.BlockSpec((1, tk, tn), lambda a, b, c: (b, a, 0))
        out_shape = SDS((n // tn, k, tn), f32)
    else:
        out_spec = pl.BlockSpec((tk, tn), lambda a, b, c: (a, b))
        out_shape = SDS((k, n), f32)
    return pl.pallas_call(
        body, grid=(k // tk, n // tn, s // ts),
        in_specs=[pl.BlockSpec((ts, tk), lambda a, b, c: (c, a)), pl.BlockSpec((ts, tn), lambda a, b, c: (c, b))],
        out_specs=out_spec, out_shape=out_shape,
        compiler_params=_cp("parallel", "parallel", "arbitrary"), name=name)(xa, gy)


def head_mean_matrix(width):
    return jnp.kron(jnp.eye(width // HEAD_DIM, dtype=f32), jnp.full((HEAD_DIM, HEAD_DIM), 1.0 / HEAD_DIM, f32)).astype(MX)


def _head_mean(t, e_ref):
    hi = t.astype(MX)
    lo = (t - hi.astype(f32)).astype(MX)
    return _dot(hi, e_ref[...]) + _dot(lo, e_ref[...])


def qk_prep(z, q_gain, k_gain, eq, ek, *, name, tm):
    s = z.shape[0]

    def body(z_ref, qg_ref, kg_ref, eq_ref, ek_ref, q_ref, kv_ref):
        q = z_ref[:, 0:ATT_WIDTH]
        r = lax.rsqrt(_head_mean(q * q, eq_ref) + EPS)
        q_ref[...] = ((q * r * qg_ref[...]) * 0.125).astype(MX)
        k = z_ref[:, ATT_WIDTH:ATT_WIDTH + KV_WIDTH]
        r = lax.rsqrt(_head_mean(k * k, ek_ref) + EPS)
        kv_ref[:, 0:KV_WIDTH] = (k * r * kg_ref[...]).astype(MX)
        kv_ref[:, KV_WIDTH:] = z_ref[:, ATT_WIDTH + KV_WIDTH:U_OFF].astype(MX)

    const = lambda a: pl.BlockSpec(a.shape, lambda i: (0, 0))
    qg = jnp.tile(q_gain.reshape(1, HEAD_DIM), (1, ATT_HEADS))
    kg = jnp.tile(k_gain.reshape(1, HEAD_DIM), (1, KV_HEADS))
    return pl.pallas_call(
        body, grid=(s // tm,), in_specs=[_row_spec(tm, IN_WIDTH), const(qg), const(kg), const(eq), const(ek)],
        out_specs=[_row_spec(tm, ATT_WIDTH), _row_spec(tm, 2 * KV_WIDTH)],
        out_shape=[SDS((s, ATT_WIDTH), MX), SDS((s, 2 * KV_WIDTH), MX)],
        compiler_params=_cp("parallel"), name=name)(z, qg, kg, eq, ek)


def _attn_mask(i, nb):
    row = lax.broadcasted_iota(jnp.int32, (GQA * BLOCK, 3 * BLOCK), 0) & (BLOCK - 1)
    col = lax.broadcasted_iota(jnp.int32, (GQA * BLOCK, 3 * BLOCK), 1)
    dist = jnp.abs(row - col + BLOCK)
    valid = (dist <= BLOCK) & ((col >= BLOCK) | (i >= 1)) & ((col < 2 * BLOCK) | (i <= nb - 2))
    return dist.astype(f32), valid


def _attn_specs(nb):
    return [pl.BlockSpec((BLOCK, ATT_WIDTH), lambda i: (i, 0)),
            pl.BlockSpec((BLOCK, 2 * KV_WIDTH), lambda i: (jnp.maximum(i - 1, 0), 0)),
            pl.BlockSpec((BLOCK, 2 * KV_WIDTH), lambda i: (i, 0)),
            pl.BlockSpec((BLOCK, 2 * KV_WIDTH), lambda i: (jnp.minimum(i + 1, nb - 1), 0)),
            pl.BlockSpec(memory_space=pltpu.SMEM)]


def _attn_probs(sc, kvh, distf, valid, sink_ref):
    row = lax.broadcasted_iota(jnp.int32, (GQA * BLOCK, 1), 0)
    slope = jnp.full((GQA * BLOCK, 1), SLOPES[GQA * kvh], f32)
    sk = jnp.full((GQA * BLOCK, 1), sink_ref[GQA * kvh], f32)
    for j in range(1, GQA):
        slope = jnp.where(row >= BLOCK * j, SLOPES[GQA * kvh + j], slope)
        sk = jnp.where(row >= BLOCK * j, sink_ref[GQA * kvh + j], sk)
    sg = jnp.where(valid, sc - slope * distf, NEG)
    m = jnp.maximum(jnp.max(sg, axis=-1, keepdims=True), sk)
    e = jnp.exp(sg - m)
    es = jnp.exp(sk - m)
    inv = 1.0 / (jnp.sum(e, axis=-1, keepdims=True) + es)
    return e * inv, es * inv


def _stack_heads(ref, kvh):
    return jnp.concatenate([ref[:, HEAD_DIM * (GQA * kvh + g):HEAD_DIM * (GQA * kvh + g + 1)] for g in range(GQA)],
                           axis=0)


def attn_fwd(qn, kv, sink, *, name, exchange=None):
    s = qn.shape[0]
    nb = s // BLOCK

    def body(q_ref, kp_ref, kc_ref, kn_ref, sink_ref, o_ref):
        i = pl.program_id(0)
        distf, valid = _attn_mask(i, nb)
        kv3 = jnp.concatenate([kp_ref[...], kc_ref[...], kn_ref[...]], axis=0)
        for kvh in range(KV_HEADS):
            kn = kv3[:, HEAD_DIM * kvh:HEAD_DIM * (kvh + 1)]
            vh = kv3[:, KV_WIDTH + HEAD_DIM * kvh:KV_WIDTH + HEAD_DIM * (kvh + 1)]
            sc = _dot(_stack_heads(q_ref, kvh), kn, NT)
            p, _ = _attn_probs(sc, kvh, distf, valid, sink_ref)
            o = _dot(p.astype(MX), vh)
            for g in range(GQA):
                h = GQA * kvh + g
                o_ref[:, HEAD_DIM * h:HEAD_DIM * (h + 1)] = o[BLOCK * g:BLOCK * (g + 1)].astype(o_ref.dtype)

    (out,), got = _call(body, grid=(nb,), in_specs=_attn_specs(nb),
                        out_specs=[pl.BlockSpec((BLOCK, ATT_WIDTH), lambda i: (i, 0))],
                        out_shape=[SDS((s, ATT_WIDTH), MX)], args=(qn, kv, kv, kv, sink), sem=("parallel",),
                        name=name, exchange=exchange)
    return out, got


def attn_bwd(qn, kv, gmix, sink, *, name):
    s = qn.shape[0]
    nb = s // BLOCK

    def body(q_ref, kp_ref, kc_ref, kn_ref, sink_ref, go_ref, gq_ref, dkv_ref, gs_ref):
        i = pl.program_id(0)

        @pl.when(i == 0)
        def _():
            gs_ref[...] = jnp.zeros_like(gs_ref)

        distf, valid = _attn_mask(i, nb)
        kv3 = jnp.concatenate([kp_ref[...], kc_ref[...], kn_ref[...]], axis=0)
        for kvh in range(KV_HEADS):
            kn = kv3[:, HEAD_DIM * kvh:HEAD_DIM * (kvh + 1)]
            vh = kv3[:, KV_WIDTH + HEAD_DIM * kvh:KV_WIDTH + HEAD_DIM * (kvh + 1)]
            qs = _stack_heads(q_ref, kvh)
            dos = _stack_heads(go_ref, kvh).astype(MX)
            p, psink = _attn_probs(_dot(qs, kn, NT), kvh, distf, valid, sink_ref)
            dp = _dot(dos, vh, NT)
            delta = jnp.sum(p * dp, axis=-1, keepdims=True)
            gsk = psink * delta
            for g in range(GQA):
                h = GQA * kvh + g
                gs_ref[h:h + 1, :] -= jnp.broadcast_to(
                    jnp.sum(gsk[BLOCK * g:BLOCK * (g + 1)], axis=0, keepdims=True), (1, 128))
            ds = (p * (dp - delta)).astype(MX)
            gv = _dot(p.astype(MX), dos, TN)
            gkn = _dot(ds, qs, TN)
            gqs = _dot(ds, kn)
            for g in range(GQA):
                h = GQA * kvh + g
                gq_ref[:, HEAD_DIM * h:HEAD_DIM * (h + 1)] = gqs[BLOCK * g:BLOCK * (g + 1)]
            for b in range(3):
                dkv_ref[b, :, HEAD_DIM * kvh:HEAD_DIM * (kvh + 1)] = gkn[BLOCK * b:BLOCK * (b + 1)]
                dkv_ref[b, :, KV_WIDTH + HEAD_DIM * kvh:KV_WIDTH + HEAD_DIM * (kvh + 1)] = gv[BLOCK * b:BLOCK * (b + 1)]

    return pl.pallas_call(
        body, grid=(nb,),
        in_specs=_attn_specs(nb) + [pl.BlockSpec((BLOCK, ATT_WIDTH), lambda i: (i, 0))],
        out_specs=[pl.BlockSpec((BLOCK, ATT_WIDTH), lambda i: (i, 0)),
                   pl.BlockSpec((3, BLOCK, 2 * KV_WIDTH), lambda i: (0, i, 0)),
                   pl.BlockSpec((ATT_HEADS, 128), lambda i: (0, 0))],
        out_shape=[SDS((s, ATT_WIDTH), f32), SDS((3, s, 2 * KV_WIDTH), f32), SDS((ATT_HEADS, 128), f32)],
        compiler_params=_cp("arbitrary"), name=name)(qn, kv, kv, kv, sink, gmix)


def gz_assemble(gqs, dkv, z, q_gain, k_gain, eq, ek, gu_f, gu_r, gy, d_skip, *, name):
    s = z.shape[0]
    nb = s // BLOCK

    def norm_bwd(t_in, g_out, gain_ref, e_ref):
        r = lax.rsqrt(_head_mean(t_in * t_in, e_ref) + EPS)
        hat = t_in * r
        t = g_out * gain_ref[...]
        return r * (t - hat * _head_mean(t * hat, e_ref)), g_out * hat

    def body(gq_ref, d0_ref, d1_ref, d2_ref, z_ref, qg_ref, kg_ref, eq_ref, ek_ref, guf_ref, gur_ref, gy_ref, ds_ref,
             gz_ref, gqg_ref, gkg_ref, gd_ref):
        i = pl.program_id(0)

        @pl.when(i == 0)
        def _():
            gqg_ref[...] = jnp.zeros_like(gqg_ref)
            gkg_ref[...] = jnp.zeros_like(gkg_ref)
            gd_ref[...] = jnp.zeros_like(gd_ref)

        gq, gg = norm_bwd(z_ref[:, 0:ATT_WIDTH], gq_ref[...] * 0.125, qg_ref, eq_ref)
        gz_ref[:, 0:ATT_WIDTH] = gq.astype(MX)
        gqg_ref[...] += _rows8(gg)
        gkv = d1_ref[0] + jnp.where(i + 1 < nb, d0_ref[0], 0.0) + jnp.where(i >= 1, d2_ref[0], 0.0)
        gk, gg = norm_bwd(z_ref[:, ATT_WIDTH:ATT_WIDTH + KV_WIDTH], gkv[:, 0:KV_WIDTH], kg_ref, ek_ref)
        gz_ref[:, ATT_WIDTH:ATT_WIDTH + KV_WIDTH] = gk.astype(MX)
        gkg_ref[...] += _rows8(gg)
        gz_ref[:, ATT_WIDTH + KV_WIDTH:U_OFF] = gkv[:, KV_WIDTH:].astype(MX)
        gyv = gy_ref[...]
        gz_ref[:, U_OFF:IN_WIDTH] = (guf_ref[...] + gur_ref[...] + ds_ref[...] * gyv).astype(MX)
        gd_ref[...] += _rows8(gyv * z_ref[:, U_OFF:IN_WIDTH])

    row = lambda w: pl.BlockSpec((BLOCK, w), lambda i: (i, 0))
    const = lambda a: pl.BlockSpec(a.shape, lambda i: (0, 0))
    qg = jnp.tile(q_gain.reshape(1, HEAD_DIM), (1, ATT_HEADS))
    kg = jnp.tile(k_gain.reshape(1, HEAD_DIM), (1, KV_HEADS))
    return pl.pallas_call(
        body, grid=(nb,),
        in_specs=[row(ATT_WIDTH),
                  pl.BlockSpec((1, BLOCK, 2 * KV_WIDTH), lambda i: (0, jnp.minimum(i + 1, nb - 1), 0)),
                  pl.BlockSpec((1, BLOCK, 2 * KV_WIDTH), lambda i: (1, i, 0)),
                  pl.BlockSpec((1, BLOCK, 2 * KV_WIDTH), lambda i: (2, jnp.maximum(i - 1, 0), 0)),
                  row(IN_WIDTH), const(qg), const(kg), const(eq), const(ek),
                  row(SSM_WIDTH), row(SSM_WIDTH), row(SSM_WIDTH), pl.BlockSpec((1, SSM_WIDTH), lambda i: (0, 0))],
        out_specs=[row(IN_WIDTH), pl.BlockSpec((8, ATT_WIDTH), lambda i: (0, 0)),
                   pl.BlockSpec((8, KV_WIDTH), lambda i: (0, 0)), pl.BlockSpec((8, SSM_WIDTH), lambda i: (0, 0))],
        out_shape=[SDS((s, IN_WIDTH), MX), SDS((8, ATT_WIDTH), f32), SDS((8, KV_WIDTH), f32),
                   SDS((8, SSM_WIDTH), f32)],
        compiler_params=_cp("arbitrary"), name=name)(
            gqs, dkv, dkv, dkv, z, qg, kg, eq, ek, gu_f, gu_r, gy, d_skip.reshape(1, SSM_WIDTH))


def _cmul(ar, ai, xr, xi):
    return ar * xr - ai * xi, ar * xi + ai * xr


def _permute_rows(src_ref, dst_ref, nv):
    for v in range(nv):
        dst_ref[8 * v:8 * v + 8, :] = src_ref[pl.ds(v, 8, stride=nv), :]


def _unpermute_rows(val, dst_ref, nv):
    for v in range(nv):
        dst_ref[pl.ds(v, 8, stride=nv), :] = val[8 * v:8 * v + 8, :]


def _scan_chunk(x_ref, tab_ref, carry_ref, nv, rev, acc=None):
    L = TILE_ST
    order = list(range(nv - 1, -1, -1)) if rev else list(range(nv))
    a_r, a_i = tab_ref[32:40, :L], tab_ref[32:40, L:]
    pr = pi = None
    for v in order:
        rows = slice(8 * v, 8 * v + 8)
        xr, xi = x_ref[rows, :L], x_ref[rows, L:]
        if pr is not None:
            mr, mi = _cmul(a_r, a_i, pr, pi)
            xr, xi = xr + mr, xi + mi
            x_ref[rows, :L] = xr
            x_ref[rows, L:] = xi
        pr, pi = xr, xi
    er, ei = pr, pi
    row = lax.broadcasted_iota(jnp.int32, (8, L), 0)
    edge = row == (7 if rev else 0)
    sh = 7 if rev else 1
    fr = jnp.where(edge, carry_ref[:, :L], pltpu.roll(er, sh, 0))
    fi = jnp.where(edge, carry_ref[:, L:], pltpu.roll(ei, sh, 0))
    for n, k in enumerate((1, 2, 4)):
        mr, mi = tab_ref[8 * n:8 * n + 8, :L], tab_ref[8 * n:8 * n + 8, L:]
        sh = (8 - k) if rev else k
        rr, ri = pltpu.roll(fr, sh, 0), pltpu.roll(fi, sh, 0)
        fr, fi = fr + mr * rr - mi * ri, fi + mr * ri + mi * rr
    dr, di = _cmul(tab_ref[24:32, :L], tab_ref[24:32, L:], fr, fi)
    last = 0 if rev else 7
    carry_ref[:, :L] = jnp.broadcast_to((dr + er)[last:last + 1, :], (8, L))
    carry_ref[:, L:] = jnp.broadcast_to((di + ei)[last:last + 1, :], (8, L))
    qr, qi = fr, fi
    if acc is not None:
        sr, si = jnp.zeros((8, L), f32), jnp.zeros((8, L), f32)
    for v in order:
        rows = slice(8 * v, 8 * v + 8)
        trow = slice(40 + v, 41 + v)
        mr, mi = _cmul(tab_ref[trow, :L], tab_ref[trow, L:], fr, fi)
        xr, xi = x_ref[rows, :L] + mr, x_ref[rows, L:] + mi
        x_ref[rows, :L] = xr
        x_ref[rows, L:] = xi
        if acc is not None:
            gr, gi = acc[0][rows, :L], acc[0][rows, L:]
            sr, si = sr + gr * qr + gi * qi, si + gi * qr - gr * qi
            qr, qi = xr, xi
    if acc is not None:
        acc[1][:, :L] += sr
        acc[1][:, L:] += si


def ssm_fwd(z, tab, bmat, cmat, *, rev, name, chunk, exchange=None):
    s = z.shape[0]
    nc = s // chunk
    nv = chunk // 8
    ci = (lambda i: nc - 1 - i) if rev else (lambda i: i)

    tp = TILES_PER_STEP

    def body(*refs):
        u_refs = refs[:tp]
        tab_ref, b_ref, c_ref, y_ref, xb_ref, u_scr, x_scr, carry = refs[tp:]

        @pl.when(pl.program_id(1) == 0)
        def _():
            carry[...] = jnp.zeros_like(carry)

        for t in range(tp):
            xb_ref[0, :, 2 * TILE_ST * t:2 * TILE_ST * (t + 1)] = carry[t]
            _permute_rows(u_refs[t], u_scr.at[t], nv)
            x_scr[t] = _dot(u_scr[t].astype(MX), b_ref[t])
        for t in range(tp):
            _scan_chunk(x_scr.at[t], tab_ref.at[t], carry.at[t], nv, rev)
        for t in range(tp):
            _unpermute_rows(_dot(x_scr[t].astype(MX), c_ref[t]), u_scr.at[t], nv)
            y_ref[:, TILE_CH * t:TILE_CH * (t + 1)] = u_scr[t]

    u_specs = [pl.BlockSpec((chunk, TILE_CH), lambda j, i, t=t: (ci(i), U_OFF // TILE_CH + tp * j + t))
               for t in range(tp)]
    (y, xb), got = _call(
        body, grid=(SSM_TILES // tp, nc),
        in_specs=u_specs + [pl.BlockSpec((tp, 40 + nv, 2 * TILE_ST), lambda j, i: (j, 0, 0)),
                            pl.BlockSpec((tp, TILE_CH, 2 * TILE_ST), lambda j, i: (j, 0, 0)),
                            pl.BlockSpec((tp, 2 * TILE_ST, TILE_CH), lambda j, i: (j, 0, 0))],
        out_specs=[pl.BlockSpec((chunk, tp * TILE_CH), lambda j, i: (ci(i), j)),
                   pl.BlockSpec((1, 8, tp * 2 * TILE_ST), lambda j, i: (ci(i), 0, j))],
        out_shape=[SDS((s, SSM_WIDTH), f32), SDS((nc, 8, SSM_TILES * 2 * TILE_ST), f32)],
        scratch=[pltpu.VMEM((tp, chunk, TILE_CH), f32), pltpu.VMEM((tp, chunk, 2 * TILE_ST), f32),
                 pltpu.VMEM((tp, 8, 2 * TILE_ST), f32)],
        args=(*([z] * tp), tab, bmat, cmat), sem=("parallel", "arbitrary"), name=name, exchange=exchange)
    return y, xb, got


def ssm_bwd(z, gy, xb, tab_s, tab_a, bmat, cmat, *, rev, name, chunk, exchange=None):
    s = z.shape[0]
    nc = s // chunk
    nv = chunk // 8
    ci = (lambda i: i) if rev else (lambda i: nc - 1 - i)

    tp = TILES_PER_STEP
    w2 = 2 * TILE_ST

    def body(*refs):
        u_refs, gy_refs = refs[:tp], refs[tp:2 * tp]
        (xb_ref, ts_ref, ta_ref, b_ref, c_ref, gu_ref, ga_ref, gb_ref, gc_ref,
         u_scr, gy_scr, x_scr, g_scr, gcarry, xcarry) = refs[2 * tp:]

        @pl.when(pl.program_id(1) == 0)
        def _():
            gcarry[...] = jnp.zeros_like(gcarry)
            ga_ref[...] = jnp.zeros_like(ga_ref)
            gb_ref[...] = jnp.zeros_like(gb_ref)
            gc_ref[...] = jnp.zeros_like(gc_ref)

        ub, gyb = [], []
        for t in range(tp):
            _permute_rows(u_refs[t], u_scr.at[t], nv)
            _permute_rows(gy_refs[t], gy_scr.at[t], nv)
            ub.append(u_scr[t].astype(MX))
            gyb.append(gy_scr[t].astype(MX))
        for t in range(tp):
            g_scr[t] = _dot(gyb[t], c_ref[t], NT)
            x_scr[t] = _dot(ub[t], b_ref[t])
            xcarry[t] = xb_ref[0, :, w2 * t:w2 * (t + 1)]
        for t in range(tp):
            _scan_chunk(g_scr.at[t], ta_ref.at[t], gcarry.at[t], nv, not rev)
        for t in range(tp):
            _scan_chunk(x_scr.at[t], ts_ref.at[t], xcarry.at[t], nv, rev,
                        acc=(g_scr.at[t], ga_ref.at[:, pl.ds(w2 * t, w2)]))
            gb16 = g_scr[t].astype(MX)
            gb_ref[t] += _dot(ub[t], gb16, TN)
            gc_ref[t] += _dot(x_scr[t].astype(MX), gyb[t], TN)
            _unpermute_rows(_dot(gb16, b_ref[t], NT), u_scr.at[t], nv)
            gu_ref[:, TILE_CH * t:TILE_CH * (t + 1)] = u_scr[t]

    tile3 = lambda a, b: pl.BlockSpec((tp, a, b), lambda j, i: (j, 0, 0))
    u_specs = [pl.BlockSpec((chunk, TILE_CH), lambda j, i, t=t: (ci(i), U_OFF // TILE_CH + tp * j + t))
               for t in range(tp)]
    gy_specs = [pl.BlockSpec((chunk, TILE_CH), lambda j, i, t=t: (ci(i), tp * j + t)) for t in range(tp)]
    outs, got = _call(
        body, grid=(SSM_TILES // tp, nc),
        in_specs=u_specs + gy_specs + [
                  pl.BlockSpec((1, 8, tp * w2), lambda j, i: (ci(i), 0, j)),
                  tile3(40 + nv, w2), tile3(40 + nv, w2), tile3(TILE_CH, w2), tile3(w2, TILE_CH)],
        out_specs=[pl.BlockSpec((chunk, tp * TILE_CH), lambda j, i: (ci(i), j)),
                   pl.BlockSpec((8, tp * w2), lambda j, i: (0, j)),
                   tile3(TILE_CH, w2), tile3(w2, TILE_CH)],
        out_shape=[SDS((s, SSM_WIDTH), f32), SDS((8, SSM_TILES * w2), f32),
                   SDS((SSM_TILES, TILE_CH, w2), f32), SDS((SSM_TILES, w2, TILE_CH), f32)],
        scratch=[pltpu.VMEM((tp, chunk, TILE_CH), f32), pltpu.VMEM((tp, chunk, TILE_CH), f32),
                 pltpu.VMEM((tp, chunk, w2), f32), pltpu.VMEM((tp, chunk, w2), f32),
                 pltpu.VMEM((tp, 8, w2), f32), pltpu.VMEM((tp, 8, w2), f32)],
        args=(*([z] * tp), *([gy] * tp), xb, tab_s, tab_a, bmat, cmat), sem=("parallel", "arbitrary"),
        name=name, exchange=exchange)
    return (*outs, got)


GELU_K = math.sqrt(2.0 / math.pi)


def _gelu(y):
    return 0.5 * y * (1.0 + jnp.tanh(GELU_K * (y + 0.044715 * (y * y * y))))


def _gelu_grad(y):
    t = jnp.tanh(GELU_K * (y + 0.044715 * (y * y * y)))
    return 0.5 * (1.0 + t) + 0.5 * y * (1.0 - t * t) * (GELU_K * (1.0 + 3.0 * 0.044715 * (y * y)))


def glu_fwd(y_f, y_r, z, att, d_skip, w_glu, l, *, name, tm):
    s = z.shape[0]
    nblk, cb = w_glu.shape[1], w_glu.shape[3]

    def body(yf_ref, yr_ref, ua_ref, ub_ref, att_ref, d_ref, w_ref, y_ref, gg_ref, mix_ref):
        u = jnp.concatenate([ua_ref[...], ub_ref[...]], axis=1)
        y = d_ref[...] * u + yf_ref[...] + yr_ref[...]
        y_ref[...] = y
        yg = _gelu(y).astype(MX)
        for b in range(nblk):
            gg_ref[:, cb * b:cb * (b + 1)] = _dot(yg, w_ref[0, b])
        mix_ref[:, 0:ATT_WIDTH] = att_ref[...]
        mix_ref[:, ATT_WIDTH:] = (gg_ref[:, :SSM_WIDTH] * jax.nn.sigmoid(gg_ref[:, SSM_WIDTH:])).astype(MX)

    return pl.pallas_call(
        body, grid=(s // tm,),
        in_specs=[_row_spec(tm, SSM_WIDTH), _row_spec(tm, SSM_WIDTH),
                  pl.BlockSpec((tm, SSM_WIDTH // 2), lambda i: (i, U_OFF // (SSM_WIDTH // 2))),
                  pl.BlockSpec((tm, SSM_WIDTH // 2), lambda i: (i, U_OFF // (SSM_WIDTH // 2) + 1)),
                  _row_spec(tm, ATT_WIDTH), pl.BlockSpec((1, SSM_WIDTH), lambda i: (0, 0)), _layer_spec(w_glu, l)],
        out_specs=[_row_spec(tm, SSM_WIDTH), _row_spec(tm, 2 * SSM_WIDTH), _row_spec(tm, D_MODEL)],
        out_shape=[SDS((s, SSM_WIDTH), f32), SDS((s, 2 * SSM_WIDTH), f32), SDS((s, D_MODEL), MX)],
        compiler_params=_cp("parallel"), name=name)(y_f, y_r, z, z, att, d_skip.reshape(1, SSM_WIDTH), w_glu)


def glu_bwd(gmix, gg, ypre, w_glu, l, *, name, tm):
    s = gg.shape[0]
    nblk, cb = w_glu.shape[1], w_glu.shape[3]

    def body(gm_ref, gg_ref, y_ref, w_ref, ggg_ref, yg_ref, gy_ref):
        gs = gm_ref[...]
        val, gate = gg_ref[:, :SSM_WIDTH], gg_ref[:, SSM_WIDTH:]
        sg = jax.nn.sigmoid(gate)
        ggg_ref[:, :SSM_WIDTH] = (gs * sg).astype(MX)
        ggg_ref[:, SSM_WIDTH:] = (gs * val * sg * (1.0 - sg)).astype(MX)
        y = y_ref[...]
        yg_ref[...] = _gelu(y).astype(MX)
        gyg = _dot(ggg_ref[:, 0:cb], w_ref[0, 0], NT)
        for b in range(1, nblk):
            gyg = gyg + _dot(ggg_ref[:, cb * b:cb * (b + 1)], w_ref[0, b], NT)
        gy_ref[...] = gyg * _gelu_grad(y)

    return pl.pallas_call(
        body, grid=(s // tm,),
        in_specs=[pl.BlockSpec((tm, SSM_WIDTH), lambda i: (i, 1)), _row_spec(tm, 2 * SSM_WIDTH),
                  _row_spec(tm, SSM_WIDTH), _layer_spec(w_glu, l)],
        out_specs=[_row_spec(tm, 2 * SSM_WIDTH), _row_spec(tm, SSM_WIDTH), _row_spec(tm, SSM_WIDTH)],
        out_shape=[SDS((s, 2 * SSM_WIDTH), MX), SDS((s, SSM_WIDTH), MX), SDS((s, SSM_WIDTH), f32)],
        compiler_params=_cp("parallel"), name=name)(gmix, gg, ypre, w_glu)


def loss_grad(y, target, *, name, tm):
    s, d = y.shape

    def body(y_ref, t_ref, g_ref, g16_ref, l_ref):
        @pl.when(pl.program_id(0) == 0)
        def _():
            l_ref[...] = jnp.zeros_like(l_ref)

        e = y_ref[...] - t_ref[...]
        g = e * (1.0 / d)
        g_ref[...] = g
        g16_ref[...] = g.astype(MX)
        l_ref[...] += _rows8(e * e)

    row = pl.BlockSpec((tm, d), lambda i: (i, 0))
    return pl.pallas_call(
        body, grid=(s // tm,), in_specs=[row, row],
        out_specs=[row, row, pl.BlockSpec((8, d), lambda i: (0, 0))],
        out_shape=[SDS((s, d), f32), SDS((s, d), MX), SDS((8, d), f32)],
        compiler_params=_cp("arbitrary"), name=name)(y, target)


def _row_tile(rows, cols):
    tr = rows
    while tr * cols > 256 * 1024 and tr % 16 == 0:
        tr //= 2
    return tr


def _elementwise(fn, ins, n_out, *, name, out_dtype=f32):
    shape = ins[0].shape
    cols = shape[-1]
    ins2 = [a.reshape(-1, cols) for a in ins]
    rows = ins2[0].shape[0]
    tr = _row_tile(rows, cols)

    def body(*refs):
        outs = fn(*[r[...] for r in refs[:len(ins)]])
        for o_ref, o in zip(refs[len(ins):], outs):
            o_ref[...] = o.astype(out_dtype)

    spec = pl.BlockSpec((tr, cols), lambda i: (i, 0))
    outs = pl.pallas_call(
        body, grid=(rows // tr,), in_specs=[spec] * len(ins), out_specs=[spec] * n_out,
        out_shape=[SDS((rows, cols), out_dtype)] * n_out, compiler_params=_cp("parallel"), name=name)(*ins2)
    return [o.reshape(shape) for o in outs]


def _adamw_math(w, g, m, v):
    m = ADAM_B1 * m + (1.0 - ADAM_B1) * g
    v = ADAM_B2 * v + (1.0 - ADAM_B2) * (g * g)
    m_hat = m / (1.0 - ADAM_B1 ** ADAM_STEP)
    v_hat = v / (1.0 - ADAM_B2 ** ADAM_STEP)
    delta = -ADAM_LR * (m_hat / (jnp.sqrt(v_hat) + ADAM_EPS) + ADAM_WD * w)
    return delta, m, v


def adamw(w, g, m, v, *, name):
    return _elementwise(_adamw_math, [w, g, m, v], 3, name=name)


SMEM = pl.BlockSpec(memory_space=pltpu.SMEM)


def _core_index():
    return lax.axis_index("c").astype(jnp.int32).reshape(1)


def adamw_halves(w, own, sib, m, v, *, name):
    depth, r, cols = w.shape
    h = r // 2
    tr = _row_tile(h, cols)
    quad = lambda a: a.reshape(depth, 2, h, cols)

    def body(c_ref, w_ref, own_ref, sib_ref, m_ref, v_ref, g_ref, d_ref, mo_ref, vo_ref):
        g = jnp.where(pl.program_id(1) == c_ref[0], own_ref[0], sib_ref[0])
        g_ref[0, 0] = g
        d_ref[0, 0], mo_ref[0, 0], vo_ref[0, 0] = _adamw_math(w_ref[0, 0], g, m_ref[0, 0], v_ref[0, 0])

    full = pl.BlockSpec((1, 1, tr, cols), lambda l, j, i: (l, j, i, 0))
    part = pl.BlockSpec((1, tr, cols), lambda l, j, i: (l, i, 0))
    outs = pl.pallas_call(
        body, grid=(depth, 2, h // tr), in_specs=[SMEM, full, part, part, full, full], out_specs=[full] * 4,
        out_shape=[SDS((depth, 2, h, cols), f32)] * 4,
        compiler_params=_cp("parallel", "parallel", "parallel"), name=name)(
            _core_index(), quad(w), own, sib, quad(m), quad(v))
    return [o.reshape(depth, r, cols) for o in outs]


def add_own_half(g4, recv, *, name):
    _, _, h, cols = g4.shape
    tr = _row_tile(h, cols)

    def body(c_ref, g_ref, r_ref, o_ref):
        own = jnp.where(c_ref[0] == 0, g_ref[0, 0], g_ref[0, 1])
        o_ref[0] = (own + r_ref[0]).astype(WIRE)

    part = pl.BlockSpec((1, tr, cols), lambda s, i: (s, i, 0))
    return pl.pallas_call(
        body, grid=(4, h // tr),
        in_specs=[SMEM, pl.BlockSpec((1, 2, tr, cols), lambda s, i: (s, 0, i, 0)), part], out_specs=part,
        out_shape=SDS((4, h, cols), WIRE), compiler_params=_cp("parallel", "parallel"), name=name)(
            _core_index(), g4, recv)


def _chip_index():
    return (2 * lax.axis_index("x") + lax.axis_index("y")).astype(jnp.int32).reshape(1)


def sum_pieces(sums, got, *, name, into, layer):
    _, h, cols = sums.shape
    tr = _row_tile(h, cols)

    def body(me_ref, s_ref, g_ref, stack_ref, o_ref):
        del stack_ref
        own = s_ref[0]
        for s in range(1, 4):
            own = jnp.where(me_ref[0] == s, s_ref[s], own)
        o_ref[0] = ((own.astype(f32) + g_ref[0].astype(f32)) + g_ref[1].astype(f32)) + g_ref[2].astype(f32)

    return pl.pallas_call(
        body, grid=(h // tr,),
        in_specs=[SMEM, pl.BlockSpec((4, tr, cols), lambda i: (0, i, 0)),
                  pl.BlockSpec((3, tr, cols), lambda i: (0, i, 0)), ANY],
        out_specs=pl.BlockSpec((1, tr, cols), lambda i: (layer, i, 0)),
        out_shape=SDS(into.shape, f32), input_output_aliases={3: 0},
        compiler_params=_cp("parallel"), name=name)(_chip_index(), sums, got, into)


def sum4(a, *, name, into=None, layer=0):
    shape = a.shape[1:]
    cols = shape[-1]
    a2 = a.reshape(4, -1, cols)
    rows = a2.shape[1]
    tr = _row_tile(rows, cols)

    def body(*refs):
        a_ref, o_ref = refs[0], refs[-1]
        tot = ((a_ref[0].astype(f32) + a_ref[1].astype(f32)) + a_ref[2].astype(f32)) + a_ref[3].astype(f32)
        if into is None:
            o_ref[...] = tot
        else:
            o_ref[0] = tot

    in_spec = pl.BlockSpec((4, tr, cols), lambda i: (0, i, 0))
    if into is None:
        out = pl.pallas_call(
            body, grid=(rows // tr,), in_specs=[in_spec], out_specs=pl.BlockSpec((tr, cols), lambda i: (i, 0)),
            out_shape=SDS((rows, cols), f32), compiler_params=_cp("parallel"), name=name)(a2)
        return out.reshape(shape)
    stack = into.reshape(into.shape[0], rows, cols)
    out = pl.pallas_call(
        body, grid=(rows // tr,), in_specs=[in_spec, ANY],
        out_specs=pl.BlockSpec((1, tr, cols), lambda i: (layer, i, 0)),
        out_shape=SDS(stack.shape, f32), input_output_aliases={1: 0},
        compiler_params=_cp("parallel"), name=name)(a2, stack)
    return out.reshape(into.shape)


ANY = pl.BlockSpec(memory_space=pl.ANY)


def _chip_copies(ins, outs, send, recv, bcast):
    x, y, c = lax.axis_index("x"), lax.axis_index("y"), lax.axis_index("c")
    me = 2 * x + y
    copies = []
    for k in range(len(ins)):
        for j, (px, py) in enumerate(((1 - x, y), (x, 1 - y), (1 - x, 1 - y))):
            copies.append(pltpu.make_async_remote_copy(
                src_ref=ins[k] if bcast[k] else ins[k].at[2 * px + py],
                dst_ref=outs[k].at[me] if bcast[k] else outs[k].at[j],
                send_sem=send.at[4 * k + j], recv_sem=recv.at[4 * k + j],
                device_id=(px, py, c), device_id_type=MESH))
        if bcast[k]:
            copies.append(pltpu.make_async_remote_copy(
                src_ref=ins[k], dst_ref=outs[k].at[me], send_sem=send.at[4 * k + 3], recv_sem=recv.at[4 * k + 3],
                device_id=(x, y, 1 - c), device_id_type=MESH))
    return copies


def chip_exchange(arrs, bcast, *, name):
    n = len(arrs)

    def body(*refs):
        copies = _chip_copies(refs[:n], refs[n:2 * n], refs[2 * n], refs[2 * n + 1], bcast)
        for cp in copies:
            cp.start()
        for cp in copies:
            cp.wait()

    return pl.pallas_call(
        body, in_specs=[ANY] * n, out_specs=[ANY] * n,
        out_shape=[SDS((4,) + tuple(a.shape) if b else (3,) + tuple(a.shape[1:]), a.dtype)
                   for a, b in zip(arrs, bcast)],
        scratch_shapes=[pltpu.SemaphoreType.DMA((4 * n,)), pltpu.SemaphoreType.DMA((4 * n,))],
        name=name)(*arrs)


def gather_weights(shards, *, name):
    n = len(shards)
    hd = shards[0].shape[0] // 2

    def body(*refs):
        ins, outs = refs[:n], refs[n:2 * n]
        send, recv = refs[2 * n:]
        x, y, c = lax.axis_index("x"), lax.axis_index("y"), lax.axis_index("c")
        me = 2 * x + y
        chips = ((1 - x, y), (x, 1 - y), (1 - x, 1 - y))
        mine, theirs = pl.ds(c * hd, hd), pl.ds((1 - c) * hd, hd)

        def ici(k, j, src, dst):
            px, py = chips[j]
            return pltpu.make_async_remote_copy(src_ref=src, dst_ref=dst, send_sem=send.at[7 * k + j],
                                                recv_sem=recv.at[7 * k + j], device_id=(px, py, c),
                                                device_id_type=MESH)

        def d2d(k, j, src, dst):
            return pltpu.make_async_remote_copy(src_ref=src, dst_ref=dst, send_sem=send.at[7 * k + 3 + j],
                                                recv_sem=recv.at[7 * k + 3 + j], device_id=(x, y, 1 - c),
                                                device_id_type=MESH)

        own, sent = [], []
        for k in range(n):
            own.append(d2d(k, 3, ins[k], outs[k].at[:, me]))
            own[-1].start()
            for j in range(3):
                sent.append(ici(k, j, ins[k].at[mine], outs[k].at[mine, me]))
                sent[-1].start()
        for k in range(n):
            for j, (px, py) in enumerate(chips):
                landed = outs[k].at[mine, 2 * px + py]
                ici(k, j, landed, landed).wait_recv()
                sent.append(d2d(k, j, landed, landed))
                sent[-1].start()
        for k in range(n):
            for j, (px, py) in enumerate(chips):
                other = outs[k].at[theirs, 2 * px + py]
                d2d(k, j, other, other).wait_recv()
        for cp in sent:
            cp.wait_send()
        for cp in own:
            cp.wait()

    return pl.pallas_call(
        body, in_specs=[ANY] * n, out_specs=[ANY] * n,
        out_shape=[SDS((a.shape[0], 4) + tuple(a.shape[1:]), a.dtype) for a in shards],
        scratch_shapes=[pltpu.SemaphoreType.DMA((7 * n,)), pltpu.SemaphoreType.DMA((7 * n,))],
        name=name)(*shards)


def _sibling_copies(ins, outs, send, recv, half):
    x, y, c = lax.axis_index("x"), lax.axis_index("y"), lax.axis_index("c")
    return [pltpu.make_async_remote_copy(
        src_ref=ins[k].at[:, 1 - c] if half[k] else ins[k], dst_ref=outs[k], send_sem=send.at[k],
        recv_sem=recv.at[k], device_id=(x, y, 1 - c), device_id_type=MESH) for k in range(len(ins))]


def sibling_exchange(arrs, half, *, name):
    n = len(arrs)
    piece = [(a.shape[0],) + a.shape[2:] if h else a.shape for a, h in zip(arrs, half)]

    def body(*refs):
        copies = _sibling_copies(refs[:n], refs[n:2 * n], refs[2 * n], refs[2 * n + 1], half)
        for cp in copies:
            cp.start()
        for cp in copies:
            cp.wait()

    return pl.pallas_call(
        body, in_specs=[ANY] * n, out_specs=[ANY] * n,
        out_shape=[SDS(tuple(p), a.dtype) for p, a in zip(piece, arrs)],
        scratch_shapes=[pltpu.SemaphoreType.DMA((n,)), pltpu.SemaphoreType.DMA((n,))],
        name=name)(*arrs)


def ssm_discretize(lam_re, lam_im, log_dt, b_re, b_im, c_re, c_im):
    dt = jnp.exp(log_dt)[..., None]
    mag = jnp.exp(lam_re * dt)
    abr = mag * jnp.cos(lam_im * dt)
    abi = mag * jnp.sin(lam_im * dt)
    den = lam_re * lam_re + lam_im * lam_im
    zr = ((abr - 1.0) * lam_re + abi * lam_im) / den
    zi = (abi * lam_re - (abr - 1.0) * lam_im) / den
    bbr = zr[..., None] * b_re - zi[..., None] * b_im
    bbi = zr[..., None] * b_im + zi[..., None] * b_re
    eye = jnp.eye(8, dtype=f32)
    bb = jnp.stack([bbr, bbi], axis=1).reshape(2, 2, SSM_TILES, 8, SSM_STATE, SSM_GROUP)
    bmat = jnp.einsum('dqjgph,gk->djghqkp', bb, eye).reshape(2, SSM_TILES, TILE_CH, 2 * TILE_ST)
    cc = jnp.stack([c_re, -c_im], axis=1).reshape(2, 2, SSM_TILES, 8, SSM_GROUP, SSM_STATE)
    cmat = jnp.einsum('dqjghp,gk->djqkpgh', cc, eye).reshape(2, SSM_TILES, 2 * TILE_ST, TILE_CH)
    n = SSM_GROUPS * SSM_STATE
    return abr.reshape(2, n), abi.reshape(2, n), bmat, cmat


def scan_tables(ar, ai, rev, nv):
    pr, pi = ar[None], ai[None]
    sr, si = ar, ai
    while pr.shape[0] < nv:
        qr, qi = _cmul(sr[None], si[None], pr, pi)
        pr, pi = jnp.concatenate([pr, qr], axis=0), jnp.concatenate([pi, qi], axis=0)
        sr, si = _cmul(sr, si, sr, si)
    big = [(pr[nv - 1], pi[nv - 1])]
    big.append(_cmul(*big[0], *big[0]))
    big.append(_cmul(*big[1], *big[1]))
    rows = jnp.arange(8)[:, None]
    ones = jnp.ones((8, 1), f32)
    parts = []
    for k, p in zip((1, 2, 4), big):
        cond = (rows <= 7 - k) if rev else (rows >= k)
        parts.append([jnp.where(cond, q[None, :], 0.0) for q in p])
    parts.append([ones * q[None, :] for q in big[0]])
    parts.append([ones * q[None, :] for q in (ar, ai)])
    parts.append([q[::-1] if rev else q for q in (pr[:nv], pi[:nv])])
    nrow = 40 + nv
    tre = jnp.concatenate([p[0] for p in parts], axis=0).reshape(nrow, SSM_TILES, TILE_ST)
    tim = jnp.concatenate([p[1] for p in parts], axis=0).reshape(nrow, SSM_TILES, TILE_ST)
    return jnp.concatenate([tre, tim], axis=-1).transpose(1, 0, 2)


def _tile_a(ga):
    t = ga.sum(axis=0).reshape(SSM_TILES, 2, TILE_ST)
    return t[:, 0].reshape(-1), t[:, 1].reshape(-1)


SMALL = ('norm1', 'q_gain', 'k_gain', 'sink', 'lam_re', 'lam_im', 'log_dt', 'b_re', 'b_im', 'c_re', 'c_im',
         'd_skip', 'norm2')
BIG = ('w_in', 'w_glu', 'w_out', 'w_ff1', 'w_ff2')
WEIGHTS = ('norm1', 'w_in', 'q_gain', 'k_gain', 'sink', 'lam_re', 'lam_im', 'log_dt', 'b_re', 'b_im', 'c_re',
           'c_im', 'd_skip', 'w_glu', 'w_out', 'norm2', 'w_ff1', 'w_ff2')


def _chunk(s):
    return min(512, s)


FETCH_ATTN = ('w_in', 'w_glu', 'w_out')
FETCH_SSM = ('w_ff1', 'w_ff2')


def layer_forward(l, x, p, wb, li, fetch=None):
    s = x.shape[0]
    tm = min(512, s)
    sv = {}
    h1, z = norm_mm(x, p['norm1'], wb['w_in'], li, relu2=False, name=f"l{l}_in", tm=tm)
    eq, ek = head_mean_matrix(ATT_WIDTH), head_mean_matrix(KV_WIDTH)
    qn, kv = qk_prep(z, p['q_gain'], p['k_gain'], eq, ek, name=f"l{l}_qk", tm=tm)
    att, got = attn_fwd(qn, kv, p['sink'], name=f"l{l}_attn",
                        exchange=fetch and ("chips", [fetch[k] for k in FETCH_ATTN], [True] * len(FETCH_ATTN)))
    fetched = dict(zip(FETCH_ATTN, got)) if fetch else None
    sv.update(qn=qn, kv=kv, eq=eq, ek=ek)
    (ar, ai, bmat, cmat), disc_vjp = jax.vjp(
        ssm_discretize, p['lam_re'], p['lam_im'], p['log_dt'], p['b_re'], p['b_im'], p['c_re'], p['c_im'])
    bmat16, cmat16 = bmat.astype(MX), cmat.astype(MX)
    ys, xbs, tabs = [], [], []
    for d, rev in enumerate((False, True)):
        tab = scan_tables(ar[d], ai[d], rev, _chunk(s) // 8)
        y_d, xb_d, got = ssm_fwd(z, tab, bmat16[d], cmat16[d], rev=rev, name=f"l{l}_ssm{d}", chunk=_chunk(s),
                                 exchange=fetch and ("chips", [fetch[FETCH_SSM[d]]], [True]))
        if fetch:
            fetched[FETCH_SSM[d]] = got[0]
        ys.append(y_d)
        xbs.append(xb_d)
        tabs.append((tab, scan_tables(ar[d], -ai[d], not rev, _chunk(s) // 8)))
    ypre, gg, mix = glu_fwd(ys[0], ys[1], z, att, p['d_skip'], wb['w_glu'], li, name=f"l{l}_glu", tm=min(256, s))
    x1 = mm_res(mix, wb['w_out'], li, x, name=f"l{l}_out", tm=tm)
    h2, a2 = norm_mm(x1, p['norm2'], wb['w_ff1'], li, relu2=True, name=f"l{l}_ff1", tm=tm)
    x2 = mm_res(a2, wb['w_ff2'], li, x1, name=f"l{l}_ff2", tm=tm)
    sv.update(x=x, h1=h1, z=z, xbs=xbs, tabs=tabs, bmat16=bmat16, cmat16=cmat16, disc_vjp=disc_vjp,
              ypre=ypre, gg=gg, mix=mix, x1=x1, h2=h2, a2=a2)
    return x2, sv, fetched


def layer_backward(l, gx2, gx2h, p, wb, li, sv, pend=None):
    s = gx2.shape[0]
    tm = min(512, s)
    ts = min(1024, s)
    g = {}
    gf, got = mm_nt(gx2h, wb['w_ff2'], li, name=f"l{l}_bff2", tm=tm, a2=sv['a2'],
                    exchange=pend and ("sibling", pend[1], [True] * len(pend[1])))
    sums = pend and [add_own_half(a, b, name=f"l{l}_radd_{k}") for k, a, b in zip(BIG, pend[1], got)]
    g['w_ff2'] = mm_tn(sv['a2'], gx2h, name=f"l{l}_wff2", tk=1024, tn=1024, ts=ts).reshape(4, D_FF // 4, D_MODEL)
    gx1, gx1h, gn2 = mm_nt_norm(gf, wb['w_ff1'], li, sv['x1'], p['norm2'], gx2, name=f"l{l}_bff1", tm=min(256, s))
    g['norm2'] = gn2.sum(axis=0)
    g['w_ff1'] = mm_tn(sv['h2'], gf, name=f"l{l}_wff1", tk=1024, tn=1024, ts=ts, chip_major=True)
    gmix, _ = mm_nt(gx1h, wb['w_out'], li, name=f"l{l}_bout", tm=tm)
    g['w_out'] = mm_tn(sv['mix'], gx1h, name=f"l{l}_wout", tk=1024, tn=1024, ts=ts).reshape(4, D_MODEL // 4, D_MODEL)
    ggg, yg, gy = glu_bwd(gmix, sv['gg'], sv['ypre'], wb['w_glu'], li, name=f"l{l}_bglu", tm=min(256, s))
    g['w_glu'] = mm_tn(yg, ggg, name=f"l{l}_wglu", tk=512, tn=256, ts=ts, chip_major=True)
    gus, gas, gbs, gcs = [], [], [], []
    for d, rev in enumerate((False, True)):
        tab_s, tab_a = sv['tabs'][d]
        gu_d, ga_d, gb_d, gc_d, got = ssm_bwd(
            sv['z'], gy, sv['xbs'][d], tab_s, tab_a, sv['bmat16'][d], sv['cmat16'][d], rev=rev,
            name=f"l{l}_bssm{d}", chunk=_chunk(s),
            exchange=(pend and d == 0) and ("chips", sums, [False] * len(sums)) or None)
        if pend and d == 0:
            pend[0](sums, got)
        gus.append(gu_d)
        gas.append(_tile_a(ga_d))
        gbs.append(gb_d)
        gcs.append(gc_d)
    gar = jnp.stack([gas[0][0], gas[1][0]])
    gai = jnp.stack([gas[0][1], gas[1][1]])
    (g['lam_re'], g['lam_im'], g['log_dt'], g['b_re'], g['b_im'], g['c_re'], g['c_im']) = sv['disc_vjp'](
        (gar, gai, jnp.stack(gbs), jnp.stack(gcs)))
    gqs, dkv, gsk = attn_bwd(sv['qn'], sv['kv'], gmix, p['sink'], name=f"l{l}_battn")
    g['sink'] = gsk[:, 0]
    gz, gqg, gkg, gd = gz_assemble(gqs, dkv, sv['z'], p['q_gain'], p['k_gain'], sv['eq'], sv['ek'], gus[0], gus[1],
                                   gy, p['d_skip'], name=f"l{l}_gz")
    g['q_gain'] = gqg.sum(axis=0).reshape(ATT_HEADS, HEAD_DIM).sum(axis=0)
    g['k_gain'] = gkg.sum(axis=0).reshape(KV_HEADS, HEAD_DIM).sum(axis=0)
    g['d_skip'] = gd.sum(axis=0)
    gx, gxh, gn1 = mm_nt_norm(gz, wb['w_in'], li, sv['x'], p['norm1'], gx1, name=f"l{l}_bin", tm=tm)
    g['norm1'] = gn1.sum(axis=0)
    gw_in = mm_tn(sv['h1'], gz, name=f"l{l}_win", tk=1024, tn=640, ts=ts)
    g['w_in'] = gw_in.reshape(D_MODEL, 4, IN_WIDTH // 4).transpose(1, 0, 2)
    return gx, gxh, g


def stack_layouts(gathered):
    w_in = gathered['w_in']
    depth = w_in.shape[0]
    return dict(w_in=w_in.transpose(0, 2, 1, 3).reshape(depth, D_MODEL, IN_WIDTH),
                w_glu=gathered['w_glu'], w_ff1=gathered['w_ff1'],
                w_out=gathered['w_out'].reshape(depth, D_MODEL, D_MODEL),
                w_ff2=gathered['w_ff2'].reshape(depth, D_FF, D_MODEL))


def local_step(x, target, small, wb):
    depth = wb['w_in'].shape[0]
    saves = []
    for l in range(depth):
        x, sv, _ = layer_forward(l, x, {k: small[k][l] for k in SMALL}, wb, l)
        saves.append(sv)
    gx, gxh, lparts = loss_grad(x, target, name="loss", tm=min(512, x.shape[0]))
    grads = [None] * depth
    for l in reversed(range(depth)):
        gx, gxh, grads[l] = layer_backward(l, gx, gxh, {k: small[k][l] for k in SMALL}, wb, l, saves[l])
    return lparts, gx, grads


def reduce_pieces(g):
    return [g[k].reshape(4, 2, g[k].shape[1] // 2, g[k].shape[2]) for k in BIG]


def reduce_chips(l, sums, got, stacks):
    return {k: sum_pieces(a, b, name=f"l{l}_rsum_{k}", into=stacks[k], layer=l) for k, a, b in zip(BIG, sums, got)}


def gather_first(shards):
    halves = [a.reshape(2, a.shape[0] // 2, a.shape[1]) for a in shards]
    got = gather_weights(halves, name="gather_first")
    return [a.transpose(1, 0, 2, 3).reshape(4, 2 * a.shape[2], a.shape[3]) for a in got]


def reduce_small(packed):
    got = sibling_exchange([packed], [False], name="small_rsib")
    pair = _elementwise(lambda a, b: (a + b,), [packed, got[0]], 1, name="small_radd")[0]
    got = chip_exchange([pair], [True], name="small_rchips")
    return sum4(got[0], name="small_rsum")


def _pack_small(tree):
    parts = []
    for k in SMALL:
        flat = tree[k].reshape(-1)
        parts.append(jnp.pad(flat, (0, (-flat.shape[0]) % 1024)).reshape(-1, 128))
    return jnp.concatenate(parts, axis=0)


def _unpack_small(packed, like):
    out, row = {}, 0
    for k in SMALL:
        n = like[k].size
        rows = -(-n // 1024) * 8
        out[k] = packed[row:row + rows].reshape(-1)[:n].reshape(like[k].shape)
        row += rows
    return out


def kernel(x, norm1, w_in, q_gain, k_gain, sink, lam_re, lam_im, log_dt, b_re, b_im, c_re, c_im, d_skip, w_glu, w_out, norm2, w_ff1, w_ff2, loss_target, m_norm1, m_w_in, m_q_gain, m_k_gain, m_sink, m_lam_re, m_lam_im, m_log_dt, m_b_re, m_b_im, m_c_re, m_c_im, m_d_skip, m_w_glu, m_w_out, m_norm2, m_w_ff1, m_w_ff2, v_norm1, v_w_in, v_q_gain, v_k_gain, v_sink, v_lam_re, v_lam_im, v_log_dt, v_b_re, v_b_im, v_c_re, v_c_im, v_d_skip, v_w_glu, v_w_out, v_norm2, v_w_ff1, v_w_ff2):
    w = dict(norm1=norm1, w_in=w_in, q_gain=q_gain, k_gain=k_gain, sink=sink, lam_re=lam_re, lam_im=lam_im,
             log_dt=log_dt, b_re=b_re, b_im=b_im, c_re=c_re, c_im=c_im, d_skip=d_skip, w_glu=w_glu, w_out=w_out,
             norm2=norm2, w_ff1=w_ff1, w_ff2=w_ff2)
    m = dict(norm1=m_norm1, w_in=m_w_in, q_gain=m_q_gain, k_gain=m_k_gain, sink=m_sink, lam_re=m_lam_re,
             lam_im=m_lam_im, log_dt=m_log_dt, b_re=m_b_re, b_im=m_b_im, c_re=m_c_re, c_im=m_c_im,
             d_skip=m_d_skip, w_glu=m_w_glu, w_out=m_w_out, norm2=m_norm2, w_ff1=m_w_ff1, w_ff2=m_w_ff2)
    v = dict(norm1=v_norm1, w_in=v_w_in, q_gain=v_q_gain, k_gain=v_k_gain, sink=v_sink, lam_re=v_lam_re,
             lam_im=v_lam_im, log_dt=v_log_dt, b_re=v_b_re, b_im=v_b_im, c_re=v_c_re, c_im=v_c_im,
             d_skip=v_d_skip, w_glu=v_w_glu, w_out=v_w_out, norm2=v_norm2, w_ff1=v_w_ff1, w_ff2=v_w_ff2)
    depth = w_in.shape[0]

    shards = {k: w[k].astype(WIRE) for k in BIG}
    small = {k: w[k] for k in SMALL}
    stacks = [{k: jnp.zeros((depth, w[k].shape[1] // 2, w[k].shape[2]), f32) for k in BIG}]

    xs = x[0]
    gathered = dict(zip(BIG, gather_first([shards[k][0] for k in BIG])))
    saves, wbs = [], []
    for l in range(depth):
        wbs.append(stack_layouts({k: gathered[k][None] for k in BIG}))
        fetch = {k: shards[k][l + 1] for k in BIG} if l + 1 < depth else None
        xs, sv, gathered = layer_forward(l, xs, {k: small[k][l] for k in SMALL}, wbs[l], 0, fetch)
        saves.append(sv)
    gx, gxh, lparts = loss_grad(xs, loss_target[0], name="loss", tm=min(512, xs.shape[0]))
    loss = lax.psum(0.5 * jnp.sum(lparts) / D_MODEL, ("x", "y", "c"))

    def finisher(l):
        def finish(sums, got):
            stacks[0] = reduce_chips(l, sums, got, stacks[0])
        return finish

    grads, pend = [None] * depth, None
    for l in reversed(range(depth)):
        gx, gxh, g = layer_backward(l, gx, gxh, {k: small[k][l] for k in SMALL}, wbs[l], 0, saves[l], pend)
        pend = (finisher(l), reduce_pieces(g))
        grads[l] = {k: g[k] for k in SMALL}
    got = sibling_exchange(pend[1], [True] * len(BIG), name="last_rsib")
    sums = [add_own_half(a, b, name=f"last_radd_{k}") for k, a, b in zip(BIG, pend[1], got)]
    pend[0](sums, chip_exchange(sums, [False] * len(BIG), name="last_rchips"))

    sib = sibling_exchange([stacks[0][k] for k in BIG], [False] * len(BIG), name="reduce_back")
    gsmall = reduce_small(_pack_small({k: jnp.stack([grads[l][k] for l in range(depth)]) for k in SMALL}))
    like = {k: w[k] for k in SMALL}
    gfull = _unpack_small(gsmall, like)

    delta, new_m, new_v = {}, {}, {}
    for k, sib_k in zip(BIG, sib):
        gfull[k], delta[k], new_m[k], new_v[k] = adamw_halves(w[k], stacks[0][k], sib_k, m[k], v[k],
                                                              name=f"adamw_{k}")
    ds, ms, vs = adamw(_pack_small(like), gsmall, _pack_small({k: m[k] for k in SMALL}),
                       _pack_small({k: v[k] for k in SMALL}), name="adamw_small")
    delta.update(_unpack_small(ds, like))
    new_m.update(_unpack_small(ms, like))
    new_v.update(_unpack_small(vs, like))

    return (loss, gx[None], *[gfull[k] for k in WEIGHTS], *[delta[k] for k in WEIGHTS],
            *[new_m[k] for k in WEIGHTS], *[new_v[k] for k in WEIGHTS])
```

```python
import functools
import math

import jax
import jax.numpy as jnp
from jax import lax
from jax.experimental import pallas as pl
from jax.experimental.pallas import tpu as pltpu

f32 = jnp.float32
MX = jnp.bfloat16
WIRE = jnp.bfloat16
SDS = jax.ShapeDtypeStruct

D_MODEL = 1024
DEPTH = 4
ATT_HEADS = 8
KV_HEADS = 2
GQA = ATT_HEADS // KV_HEADS
HEAD_DIM = 64
ATT_WIDTH = ATT_HEADS * HEAD_DIM
KV_WIDTH = KV_HEADS * HEAD_DIM
BLOCK = 128
SSM_WIDTH = 512
SSM_GROUP = 16
SSM_GROUPS = 32
SSM_STATE = 64
SSM_TILES = 4
TILE_CH = SSM_WIDTH // SSM_TILES
TILE_ST = SSM_GROUPS * SSM_STATE // SSM_TILES
TILES_PER_STEP = 2
IN_WIDTH = ATT_WIDTH + 2 * KV_WIDTH + SSM_WIDTH
U_OFF = ATT_WIDTH + 2 * KV_WIDTH
D_FF = 4096
EPS = 1e-6
NEG = float(jnp.finfo(jnp.float32).min)
SLOPES = tuple(2.0 ** (-8.0 * (h + 1) / ATT_HEADS) for h in range(ATT_HEADS))

ADAM_LR, ADAM_B1, ADAM_B2, ADAM_EPS, ADAM_WD, ADAM_STEP = 0.001, 0.9, 0.999, 1e-08, 0.01, 10

VMEM_LIMIT = 48 * 1024 * 1024
MESH = pl.DeviceIdType.MESH

NT = (((1,), (1,)), ((), ()))
TN = (((0,), (0,)), ((), ()))


def _cp(*sem):
    return pltpu.CompilerParams(dimension_semantics=sem, vmem_limit_bytes=VMEM_LIMIT)


def _dot(a, b, dims=None):
    if dims is None:
        return jnp.dot(a, b, preferred_element_type=f32)
    return lax.dot_general(a, b, dims, preferred_element_type=f32)


def _rows8(v):
    return v.reshape(v.shape[0] // 8, 8, v.shape[1]).sum(axis=0)


def _layer_spec(w, l):
    nd = w.ndim
    return pl.BlockSpec((1,) + tuple(w.shape[1:]), lambda i: (l,) + (0,) * (nd - 1))


def _row_spec(tm, width):
    return pl.BlockSpec((tm, width), lambda i: (i, 0))


def _call(body, *, grid, in_specs, out_specs, out_shape, args, sem, name, scratch=(), exchange=None):
    n_in, n_out, n_scr = len(in_specs), len(out_specs), len(scratch)
    if exchange is None:
        res = pl.pallas_call(body, grid=grid, in_specs=in_specs, out_specs=out_specs, out_shape=out_shape,
                             scratch_shapes=list(scratch), compiler_params=_cp(*sem), name=name)(*args)
        return list(res), []
    kind, arrs, flags = exchange
    nx = len(arrs)
    if kind == "chips":
        make, nsem = _chip_copies, 4 * nx
        got = [SDS((4,) + tuple(a.shape) if b else (3,) + tuple(a.shape[1:]), a.dtype) for a, b in zip(arrs, flags)]
    else:
        make, nsem = _sibling_copies, nx
        got = [SDS((a.shape[0],) + tuple(a.shape[2:]) if h else tuple(a.shape), a.dtype)
               for a, h in zip(arrs, flags)]

    def hosted(*refs):
        ins, xin = refs[:n_in], refs[n_in:n_in + nx]
        outs = refs[n_in + nx:n_in + nx + n_out]
        xout = refs[n_in + nx + n_out:n_in + 2 * nx + n_out]
        scr = refs[n_in + 2 * nx + n_out:]
        copies = make(xin, xout, scr[n_scr], scr[n_scr + 1], flags)
        first = functools.reduce(jnp.logical_and, [pl.program_id(d) == 0 for d in range(len(grid))])
        last = functools.reduce(jnp.logical_and, [pl.program_id(d) == grid[d] - 1 for d in range(len(grid))])

        @pl.when(first)
        def _():
            for cp in copies:
                cp.start()

        body(*ins, *outs, *scr[:n_scr])

        @pl.when(last)
        def _():
            for cp in copies:
                cp.wait()

    res = pl.pallas_call(
        hosted, grid=grid, in_specs=list(in_specs) + [ANY] * nx, out_specs=list(out_specs) + [ANY] * nx,
        out_shape=list(out_shape) + got,
        scratch_shapes=list(scratch) + [pltpu.SemaphoreType.DMA((nsem,)), pltpu.SemaphoreType.DMA((nsem,))],
        compiler_params=_cp(*["arbitrary"] * len(grid)), name=name)(*args, *arrs)
    return list(res[:n_out]), list(res[n_out:])


def norm_mm(x, gain, w, l, *, relu2, name, tm):
    s, d = x.shape
    if relu2:
        nblk, cb = w.shape[1], w.shape[3]
        n = nblk * cb
    else:
        n = w.shape[2]

    def body(x_ref, g_ref, w_ref, h_ref, y_ref):
        xf = x_ref[...]
        r = lax.rsqrt(jnp.mean(xf * xf, axis=-1, keepdims=True) + EPS)
        h = (xf * r * g_ref[...]).astype(MX)
        h_ref[...] = h
        if relu2:
            for b in range(nblk):
                f = jnp.maximum(_dot(h, w_ref[0, b]), 0.0)
                y_ref[:, cb * b:cb * (b + 1)] = (f * f).astype(MX)
        else:
            y_ref[...] = _dot(h, w_ref[0])

    return pl.pallas_call(
        body, grid=(s // tm,),
        in_specs=[_row_spec(tm, d), pl.BlockSpec((1, d), lambda i: (0, 0)), _layer_spec(w, l)],
        out_specs=[_row_spec(tm, d), _row_spec(tm, n)],
        out_shape=[SDS((s, d), MX), SDS((s, n), MX if relu2 else f32)],
        compiler_params=_cp("parallel"), name=name)(x, gain.reshape(1, d), w)


def mm_res(a, w, l, res, *, name, tm):
    s, k = a.shape
    n = w.shape[2]

    def body(a_ref, w_ref, r_ref, o_ref):
        o_ref[...] = r_ref[...] + _dot(a_ref[...], w_ref[0])

    return pl.pallas_call(
        body, grid=(s // tm,), in_specs=[_row_spec(tm, k), _layer_spec(w, l), _row_spec(tm, n)],
        out_specs=_row_spec(tm, n), out_shape=SDS((s, n), f32), compiler_params=_cp("parallel"), name=name)(a, w, res)


def mm_nt(gy, w, l, *, name, tm, a2=None, exchange=None):
    s, n = gy.shape
    k = w.shape[1]
    kb = min(k, 1024)

    def body(*refs):
        g_ref, w_ref, o_ref = refs[0], refs[1], refs[-1]
        g = g_ref[...]
        for b in range(k // kb):
            cols = slice(kb * b, kb * (b + 1))
            acc = _dot(g, w_ref[0, cols, :], NT)
            if a2 is not None:
                acc = acc * (2.0 * jnp.sqrt(refs[2][:, cols].astype(f32)))
            o_ref[:, cols] = acc.astype(o_ref.dtype)

    in_specs = [_row_spec(tm, n), _layer_spec(w, l)]
    args = [gy, w]
    if a2 is not None:
        in_specs.append(_row_spec(tm, k))
        args.append(a2)
    (out,), got = _call(body, grid=(s // tm,), in_specs=in_specs, out_specs=[_row_spec(tm, k)],
                        out_shape=[SDS((s, k), f32 if a2 is None else MX)], args=args, sem=("parallel",),
                        name=name, exchange=exchange)
    return out, got


def mm_nt_norm(gy, w, l, x, gain, res, *, name, tm):
    s, n = gy.shape
    d = x.shape[1]

    def body(g_ref, w_ref, x_ref, gn_ref, r_ref, o_ref, o16_ref, gg_ref):
        @pl.when(pl.program_id(0) == 0)
        def _():
            gg_ref[...] = jnp.zeros_like(gg_ref)

        if w.ndim == 3:
            gh = _dot(g_ref[...], w_ref[0], NT)
        else:
            cb = w.shape[3]
            gh = _dot(g_ref[:, 0:cb], w_ref[0, 0], NT)
            for b in range(1, w.shape[1]):
                gh = gh + _dot(g_ref[:, cb * b:cb * (b + 1)], w_ref[0, b], NT)
        xf = x_ref[...]
        r = lax.rsqrt(jnp.mean(xf * xf, axis=-1, keepdims=True) + EPS)
        xh = xf * r
        t = gh * gn_ref[...]
        gx = r_ref[...] + r * (t - xh * jnp.mean(t * xh, axis=-1, keepdims=True))
        o_ref[...] = gx
        o16_ref[...] = gx.astype(MX)
        gg_ref[...] += _rows8(gh * xh)

    return pl.pallas_call(
        body, grid=(s // tm,),
        in_specs=[_row_spec(tm, n), _layer_spec(w, l), _row_spec(tm, d), pl.BlockSpec((1, d), lambda i: (0, 0)),
                  _row_spec(tm, d)],
        out_specs=[_row_spec(tm, d), _row_spec(tm, d), pl.BlockSpec((8, d), lambda i: (0, 0))],
        out_shape=[SDS((s, d), f32), SDS((s, d), MX), SDS((8, d), f32)],
        compiler_params=_cp("arbitrary"), name=name)(gy, w, x, gain.reshape(1, d), res)


def mm_tn(xa, gy, *, name, tk, tn, ts, chip_major=False):
    s, k = xa.shape
    n = gy.shape[1]

    def body(x_ref, g_ref, o_ref):
        @pl.when(pl.program_id(2) == 0)
        def _():
            o_ref[...] = jnp.zeros_like(o_ref)

        acc = _dot(x_ref[...], g_ref[...], TN)
        if chip_major:
            o_ref[0] += acc
        else:
            o_ref[...] += acc

    if chip_major:
        out_spec = pl.BlockSpec((1, tk, tn), lambda a, b, c: (b, a, 0))
        out_shape = SDS((n // tn, k, tn), f32)
    else:
        out_spec = pl.BlockSpec((tk, tn), lambda a, b, c: (a, b))
        out_shape = SDS((k, n), f32)
    return pl.pallas_call(
        body, grid=(k // tk, n // tn, s // ts),
        in_specs=[pl.BlockSpec((ts, tk), lambda a, b, c: (c, a)), pl.BlockSpec((ts, tn), lambda a, b, c: (c, b))],
        out_specs=out_spec, out_shape=out_shape,
        compiler_params=_cp("parallel", "parallel", "arbitrary"), name=name)(xa, gy)


def head_mean_matrix(width):
    return jnp.kron(jnp.eye(width // HEAD_DIM, dtype=f32), jnp.full((HEAD_DIM, HEAD_DIM), 1.0 / HEAD_DIM, f32)).astype(MX)


def _head_mean(t, e_ref):
    hi = t.astype(MX)
    lo = (t - hi.astype(f32)).astype(MX)
    return _dot(hi, e_ref[...]) + _dot(lo, e_ref[...])


def qk_prep(z, q_gain, k_gain, eq, ek, *, name, tm):
    s = z.shape[0]

    def body(z_ref, qg_ref, kg_ref, eq_ref, ek_ref, q_ref, kv_ref):
        q = z_ref[:, 0:ATT_WIDTH]
        r = lax.rsqrt(_head_mean(q * q, eq_ref) + EPS)
        q_ref[...] = ((q * r * qg_ref[...]) * 0.125).astype(MX)
        k = z_ref[:, ATT_WIDTH:ATT_WIDTH + KV_WIDTH]
        r = lax.rsqrt(_head_mean(k * k, ek_ref) + EPS)
        kv_ref[:, 0:KV_WIDTH] = (k * r * kg_ref[...]).astype(MX)
        kv_ref[:, KV_WIDTH:] = z_ref[:, ATT_WIDTH + KV_WIDTH:U_OFF].astype(MX)

    const = lambda a: pl.BlockSpec(a.shape, lambda i: (0, 0))
    qg = jnp.tile(q_gain.reshape(1, HEAD_DIM), (1, ATT_HEADS))
    kg = jnp.tile(k_gain.reshape(1, HEAD_DIM), (1, KV_HEADS))
    return pl.pallas_call(
        body, grid=(s // tm,), in_specs=[_row_spec(tm, IN_WIDTH), const(qg), const(kg), const(eq), const(ek)],
        out_specs=[_row_spec(tm, ATT_WIDTH), _row_spec(tm, 2 * KV_WIDTH)],
        out_shape=[SDS((s, ATT_WIDTH), MX), SDS((s, 2 * KV_WIDTH), MX)],
        compiler_params=_cp("parallel"), name=name)(z, qg, kg, eq, ek)


def _attn_mask(i, nb):
    row = lax.broadcasted_iota(jnp.int32, (GQA * BLOCK, 3 * BLOCK), 0) & (BLOCK - 1)
    col = lax.broadcasted_iota(jnp.int32, (GQA * BLOCK, 3 * BLOCK), 1)
    dist = jnp.abs(row - col + BLOCK)
    valid = (dist <= BLOCK) & ((col >= BLOCK) | (i >= 1)) & ((col < 2 * BLOCK) | (i <= nb - 2))
    return dist.astype(f32), valid


def _attn_specs(nb):
    return [pl.BlockSpec((BLOCK, ATT_WIDTH), lambda i: (i, 0)),
            pl.BlockSpec((BLOCK, 2 * KV_WIDTH), lambda i: (jnp.maximum(i - 1, 0), 0)),
            pl.BlockSpec((BLOCK, 2 * KV_WIDTH), lambda i: (i, 0)),
            pl.BlockSpec((BLOCK, 2 * KV_WIDTH), lambda i: (jnp.minimum(i + 1, nb - 1), 0)),
            pl.BlockSpec(memory_space=pltpu.SMEM)]


def _attn_probs(sc, kvh, distf, valid, sink_ref):
    row = lax.broadcasted_iota(jnp.int32, (GQA * BLOCK, 1), 0)
    slope = jnp.full((GQA * BLOCK, 1), SLOPES[GQA * kvh], f32)
    sk = jnp.full((GQA * BLOCK, 1), sink_ref[GQA * kvh], f32)
    for j in range(1, GQA):
        slope = jnp.where(row >= BLOCK * j, SLOPES[GQA * kvh + j], slope)
        sk = jnp.where(row >= BLOCK * j, sink_ref[GQA * kvh + j], sk)
    sg = jnp.where(valid, sc - slope * distf, NEG)
    m = jnp.maximum(jnp.max(sg, axis=-1, keepdims=True), sk)
    e = jnp.exp(sg - m)
    es = jnp.exp(sk - m)
    inv = 1.0 / (jnp.sum(e, axis=-1, keepdims=True) + es)
    return e * inv, es * inv


def _stack_heads(ref, kvh):
    return jnp.concatenate([ref[:, HEAD_DIM * (GQA * kvh + g):HEAD_DIM * (GQA * kvh + g + 1)] for g in range(GQA)],
                           axis=0)


def attn_fwd(qn, kv, sink, *, name, exchange=None):
    s = qn.shape[0]
    nb = s // BLOCK

    def body(q_ref, kp_ref, kc_ref, kn_ref, sink_ref, o_ref):
        i = pl.program_id(0)
        distf, valid = _attn_mask(i, nb)
        kv3 = jnp.concatenate([kp_ref[...], kc_ref[...], kn_ref[...]], axis=0)
        for kvh in range(KV_HEADS):
            kn = kv3[:, HEAD_DIM * kvh:HEAD_DIM * (kvh + 1)]
            vh = kv3[:, KV_WIDTH + HEAD_DIM * kvh:KV_WIDTH + HEAD_DIM * (kvh + 1)]
            sc = _dot(_stack_heads(q_ref, kvh), kn, NT)
            p, _ = _attn_probs(sc, kvh, distf, valid, sink_ref)
            o = _dot(p.astype(MX), vh)
            for g in range(GQA):
                h = GQA * kvh + g
                o_ref[:, HEAD_DIM * h:HEAD_DIM * (h + 1)] = o[BLOCK * g:BLOCK * (g + 1)].astype(o_ref.dtype)

    (out,), got = _call(body, grid=(nb,), in_specs=_attn_specs(nb),
                        out_specs=[pl.BlockSpec((BLOCK, ATT_WIDTH), lambda i: (i, 0))],
                        out_shape=[SDS((s, ATT_WIDTH), MX)], args=(qn, kv, kv, kv, sink), sem=("parallel",),
                        name=name, exchange=exchange)
    return out, got


def attn_bwd(qn, kv, gmix, sink, *, name):
    s = qn.shape[0]
    nb = s // BLOCK

    def body(q_ref, kp_ref, kc_ref, kn_ref, sink_ref, go_ref, gq_ref, dkv_ref, gs_ref):
        i = pl.program_id(0)

        @pl.when(i == 0)
        def _():
            gs_ref[...] = jnp.zeros_like(gs_ref)

        distf, valid = _attn_mask(i, nb)
        kv3 = jnp.concatenate([kp_ref[...], kc_ref[...], kn_ref[...]], axis=0)
        for kvh in range(KV_HEADS):
            kn = kv3[:, HEAD_DIM * kvh:HEAD_DIM * (kvh + 1)]
            vh = kv3[:, KV_WIDTH + HEAD_DIM * kvh:KV_WIDTH + HEAD_DIM * (kvh + 1)]
            qs = _stack_heads(q_ref, kvh)
            dos = _stack_heads(go_ref, kvh).astype(MX)
            p, psink = _attn_probs(_dot(qs, kn, NT), kvh, distf, valid, sink_ref)
            dp = _dot(dos, vh, NT)
            delta = jnp.sum(p * dp, axis=-1, keepdims=True)
            gsk = psink * delta
            for g in range(GQA):
                h = GQA * kvh + g
                gs_ref[h:h + 1, :] -= jnp.broadcast_to(
                    jnp.sum(gsk[BLOCK * g:BLOCK * (g + 1)], axis=0, keepdims=True), (1, 128))
            ds = (p * (dp - delta)).astype(MX)
            gv = _dot(p.astype(MX), dos, TN)
            gkn = _dot(ds, qs, TN)
            gqs = _dot(ds, kn)
            for g in range(GQA):
                h = GQA * kvh + g
                gq_ref[:, HEAD_DIM * h:HEAD_DIM * (h + 1)] = gqs[BLOCK * g:BLOCK * (g + 1)]
            for b in range(3):
                dkv_ref[b, :, HEAD_DIM * kvh:HEAD_DIM * (kvh + 1)] = gkn[BLOCK * b:BLOCK * (b + 1)]
                dkv_ref[b, :, KV_WIDTH + HEAD_DIM * kvh:KV_WIDTH + HEAD_DIM * (kvh + 1)] = gv[BLOCK * b:BLOCK * (b + 1)]

    return pl.pallas_call(
        body, grid=(nb,),
        in_specs=_attn_specs(nb) + [pl.BlockSpec((BLOCK, ATT_WIDTH), lambda i: (i, 0))],
        out_specs=[pl.BlockSpec((BLOCK, ATT_WIDTH), lambda i: (i, 0)),
                   pl.BlockSpec((3, BLOCK, 2 * KV_WIDTH), lambda i: (0, i, 0)),
                   pl.BlockSpec((ATT_HEADS, 128), lambda i: (0, 0))],
        out_shape=[SDS((s, ATT_WIDTH), f32), SDS((3, s, 2 * KV_WIDTH), f32), SDS((ATT_HEADS, 128), f32)],
        compiler_params=_cp("arbitrary"), name=name)(qn, kv, kv, kv, sink, gmix)


def gz_assemble(gqs, dkv, z, q_gain, k_gain, eq, ek, gu_f, gu_r, gy, d_skip, *, name):
    s = z.shape[0]
    nb = s // BLOCK

    def norm_bwd(t_in, g_out, gain_ref, e_ref):
        r = lax.rsqrt(_head_mean(t_in * t_in, e_ref) + EPS)
        hat = t_in * r
        t = g_out * gain_ref[...]
        return r * (t - hat * _head_mean(t * hat, e_ref)), g_out * hat

    def body(gq_ref, d0_ref, d1_ref, d2_ref, z_ref, qg_ref, kg_ref, eq_ref, ek_ref, guf_ref, gur_ref, gy_ref, ds_ref,
             gz_ref, gqg_ref, gkg_ref, gd_ref):
        i = pl.program_id(0)

        @pl.when(i == 0)
        def _():
            gqg_ref[...] = jnp.zeros_like(gqg_ref)
            gkg_ref[...] = jnp.zeros_like(gkg_ref)
            gd_ref[...] = jnp.zeros_like(gd_ref)

        gq, gg = norm_bwd(z_ref[:, 0:ATT_WIDTH], gq_ref[...] * 0.125, qg_ref, eq_ref)
        gz_ref[:, 0:ATT_WIDTH] = gq.astype(MX)
        gqg_ref[...] += _rows8(gg)
        gkv = d1_ref[0] + jnp.where(i + 1 < nb, d0_ref[0], 0.0) + jnp.where(i >= 1, d2_ref[0], 0.0)
        gk, gg = norm_bwd(z_ref[:, ATT_WIDTH:ATT_WIDTH + KV_WIDTH], gkv[:, 0:KV_WIDTH], kg_ref, ek_ref)
        gz_ref[:, ATT_WIDTH:ATT_WIDTH + KV_WIDTH] = gk.astype(MX)
        gkg_ref[...] += _rows8(gg)
        gz_ref[:, ATT_WIDTH + KV_WIDTH:U_OFF] = gkv[:, KV_WIDTH:].astype(MX)
        gyv = gy_ref[...]
        gz_ref[:, U_OFF:IN_WIDTH] = (guf_ref[...] + gur_ref[...] + ds_ref[...] * gyv).astype(MX)
        gd_ref[...] += _rows8(gyv * z_ref[:, U_OFF:IN_WIDTH])

    row = lambda w: pl.BlockSpec((BLOCK, w), lambda i: (i, 0))
    const = lambda a: pl.BlockSpec(a.shape, lambda i: (0, 0))
    qg = jnp.tile(q_gain.reshape(1, HEAD_DIM), (1, ATT_HEADS))
    kg = jnp.tile(k_gain.reshape(1, HEAD_DIM), (1, KV_HEADS))
    return pl.pallas_call(
        body, grid=(nb,),
        in_specs=[row(ATT_WIDTH),
                  pl.BlockSpec((1, BLOCK, 2 * KV_WIDTH), lambda i: (0, jnp.minimum(i + 1, nb - 1), 0)),
                  pl.BlockSpec((1, BLOCK, 2 * KV_WIDTH), lambda i: (1, i, 0)),
                  pl.BlockSpec((1, BLOCK, 2 * KV_WIDTH), lambda i: (2, jnp.maximum(i - 1, 0), 0)),
                  row(IN_WIDTH), const(qg), const(kg), const(eq), const(ek),
                  row(SSM_WIDTH), row(SSM_WIDTH), row(SSM_WIDTH), pl.BlockSpec((1, SSM_WIDTH), lambda i: (0, 0))],
        out_specs=[row(IN_WIDTH), pl.BlockSpec((8, ATT_WIDTH), lambda i: (0, 0)),
                   pl.BlockSpec((8, KV_WIDTH), lambda i: (0, 0)), pl.BlockSpec((8, SSM_WIDTH), lambda i: (0, 0))],
        out_shape=[SDS((s, IN_WIDTH), MX), SDS((8, ATT_WIDTH), f32), SDS((8, KV_WIDTH), f32),
                   SDS((8, SSM_WIDTH), f32)],
        compiler_params=_cp("arbitrary"), name=name)(
            gqs, dkv, dkv, dkv, z, qg, kg, eq, ek, gu_f, gu_r, gy, d_skip.reshape(1, SSM_WIDTH))


def _cmul(ar, ai, xr, xi):
    return ar * xr - ai * xi, ar * xi + ai * xr


def _permute_rows(src_ref, dst_ref, nv):
    for v in range(nv):
        dst_ref[8 * v:8 * v + 8, :] = src_ref[pl.ds(v, 8, stride=nv), :]


def _unpermute_rows(val, dst_ref, nv):
    for v in range(nv):
        dst_ref[pl.ds(v, 8, stride=nv), :] = val[8 * v:8 * v + 8, :]


def _scan_chunk(x_ref, tab_ref, carry_ref, nv, rev, acc=None):
    L = TILE_ST
    order = list(range(nv - 1, -1, -1)) if rev else list(range(nv))
    a_r, a_i = tab_ref[32:40, :L], tab_ref[32:40, L:]
    pr = pi = None
    for v in order:
        rows = slice(8 * v, 8 * v + 8)
        xr, xi = x_ref[rows, :L], x_ref[rows, L:]
        if pr is not None:
            mr, mi = _cmul(a_r, a_i, pr, pi)
            xr, xi = xr + mr, xi + mi
            x_ref[rows, :L] = xr
            x_ref[rows, L:] = xi
        pr, pi = xr, xi
    er, ei = pr, pi
    row = lax.broadcasted_iota(jnp.int32, (8, L), 0)
    edge = row == (7 if rev else 0)
    sh = 7 if rev else 1
    fr = jnp.where(edge, carry_ref[:, :L], pltpu.roll(er, sh, 0))
    fi = jnp.where(edge, carry_ref[:, L:], pltpu.roll(ei, sh, 0))
    for n, k in enumerate((1, 2, 4)):
        mr, mi = tab_ref[8 * n:8 * n + 8, :L], tab_ref[8 * n:8 * n + 8, L:]
        sh = (8 - k) if rev else k
        rr, ri = pltpu.roll(fr, sh, 0), pltpu.roll(fi, sh, 0)
        fr, fi = fr + mr * rr - mi * ri, fi + mr * ri + mi * rr
    dr, di = _cmul(tab_ref[24:32, :L], tab_ref[24:32, L:], fr, fi)
    last = 0 if rev else 7
    carry_ref[:, :L] = jnp.broadcast_to((dr + er)[last:last + 1, :], (8, L))
    carry_ref[:, L:] = jnp.broadcast_to((di + ei)[last:last + 1, :], (8, L))
    qr, qi = fr, fi
    if acc is not None:
        sr, si = jnp.zeros((8, L), f32), jnp.zeros((8, L), f32)
    for v in order:
        rows = slice(8 * v, 8 * v + 8)
        trow = slice(40 + v, 41 + v)
        mr, mi = _cmul(tab_ref[trow, :L], tab_ref[trow, L:], fr, fi)
        xr, xi = x_ref[rows, :L] + mr, x_ref[rows, L:] + mi
        x_ref[rows, :L] = xr
        x_ref[rows, L:] = xi
        if acc is not None:
            gr, gi = acc[0][rows, :L], acc[0][rows, L:]
            sr, si = sr + gr * qr + gi * qi, si + gi * qr - gr * qi
            qr, qi = xr, xi
    if acc is not None:
        acc[1][:, :L] += sr
        acc[1][:, L:] += si


def ssm_fwd(z, tabs, bmat, cmat, *, rev, name, chunk, exchange=None):
    var = 2 if rev else 0
    s = z.shape[0]
    nc = s // chunk
    nv = chunk // 8
    ci = (lambda i: nc - 1 - i) if rev else (lambda i: i)

    tp = TILES_PER_STEP

    def body(*refs):
        u_refs = refs[:tp]
        tab_ref, b_ref, c_ref, y_ref, xb_ref, u_scr, x_scr, carry = refs[tp:]

        @pl.when(pl.program_id(1) == 0)
        def _():
            carry[...] = jnp.zeros_like(carry)

        for t in range(tp):
            xb_ref[0, :, 2 * TILE_ST * t:2 * TILE_ST * (t + 1)] = carry[t]
            _permute_rows(u_refs[t], u_scr.at[t], nv)
            x_scr[t] = _dot(u_scr[t].astype(MX), b_ref[t])
        for t in range(tp):
            _scan_chunk(x_scr.at[t], tab_ref.at[0, t], carry.at[t], nv, rev)
        for t in range(tp):
            _unpermute_rows(_dot(x_scr[t].astype(MX), c_ref[t]), u_scr.at[t], nv)
            y_ref[:, TILE_CH * t:TILE_CH * (t + 1)] = u_scr[t]

    u_specs = [pl.BlockSpec((chunk, TILE_CH), lambda j, i, t=t: (ci(i), U_OFF // TILE_CH + tp * j + t))
               for t in range(tp)]
    (y, xb), got = _call(
        body, grid=(SSM_TILES // tp, nc),
        in_specs=u_specs + [pl.BlockSpec((1, tp, 40 + nv, 2 * TILE_ST), lambda j, i: (var, j, 0, 0)),
                            pl.BlockSpec((tp, TILE_CH, 2 * TILE_ST), lambda j, i: (j, 0, 0)),
                            pl.BlockSpec((tp, 2 * TILE_ST, TILE_CH), lambda j, i: (j, 0, 0))],
        out_specs=[pl.BlockSpec((chunk, tp * TILE_CH), lambda j, i: (ci(i), j)),
                   pl.BlockSpec((1, 8, tp * 2 * TILE_ST), lambda j, i: (ci(i), 0, j))],
        out_shape=[SDS((s, SSM_WIDTH), f32), SDS((nc, 8, SSM_TILES * 2 * TILE_ST), f32)],
        scratch=[pltpu.VMEM((tp, chunk, TILE_CH), f32), pltpu.VMEM((tp, chunk, 2 * TILE_ST), f32),
                 pltpu.VMEM((tp, 8, 2 * TILE_ST), f32)],
        args=(*([z] * tp), tabs, bmat, cmat), sem=("parallel", "arbitrary"), name=name, exchange=exchange)
    return y, xb, got


def ssm_bwd(z, gy, xb, tabs, bmat, cmat, *, rev, name, chunk, exchange=None):
    var = 2 if rev else 0
    s = z.shape[0]
    nc = s // chunk
    nv = chunk // 8
    ci = (lambda i: i) if rev else (lambda i: nc - 1 - i)

    tp = TILES_PER_STEP
    w2 = 2 * TILE_ST

    def body(*refs):
        u_refs, gy_refs = refs[:tp], refs[tp:2 * tp]
        (xb_ref, ts_ref, ta_ref, b_ref, c_ref, gu_ref, ga_ref, gb_ref, gc_ref,
         u_scr, gy_scr, x_scr, g_scr, gcarry, xcarry) = refs[2 * tp:]

        @pl.when(pl.program_id(1) == 0)
        def _():
            gcarry[...] = jnp.zeros_like(gcarry)
            ga_ref[...] = jnp.zeros_like(ga_ref)
            gb_ref[...] = jnp.zeros_like(gb_ref)
            gc_ref[...] = jnp.zeros_like(gc_ref)

        ub, gyb = [], []
        for t in range(tp):
            _permute_rows(u_refs[t], u_scr.at[t], nv)
            _permute_rows(gy_refs[t], gy_scr.at[t], nv)
            ub.append(u_scr[t].astype(MX))
            gyb.append(gy_scr[t].astype(MX))
        for t in range(tp):
            g_scr[t] = _dot(gyb[t], c_ref[t], NT)
            x_scr[t] = _dot(ub[t], b_ref[t])
            xcarry[t] = xb_ref[0, :, w2 * t:w2 * (t + 1)]
        for t in range(tp):
            _scan_chunk(g_scr.at[t], ta_ref.at[0, t], gcarry.at[t], nv, not rev)
        for t in range(tp):
            _scan_chunk(x_scr.at[t], ts_ref.at[0, t], xcarry.at[t], nv, rev,
                        acc=(g_scr.at[t], ga_ref.at[:, pl.ds(w2 * t, w2)]))
            gb16 = g_scr[t].astype(MX)
            gb_ref[t] += _dot(ub[t], gb16, TN)
            gc_ref[t] += _dot(x_scr[t].astype(MX), gyb[t], TN)
            _unpermute_rows(_dot(gb16, b_ref[t], NT), u_scr.at[t], nv)
            gu_ref[:, TILE_CH * t:TILE_CH * (t + 1)] = u_scr[t]

    tile3 = lambda a, b: pl.BlockSpec((tp, a, b), lambda j, i: (j, 0, 0))
    u_specs = [pl.BlockSpec((chunk, TILE_CH), lambda j, i, t=t: (ci(i), U_OFF // TILE_CH + tp * j + t))
               for t in range(tp)]
    gy_specs = [pl.BlockSpec((chunk, TILE_CH), lambda j, i, t=t: (ci(i), tp * j + t)) for t in range(tp)]
    outs, got = _call(
        body, grid=(SSM_TILES // tp, nc),
        in_specs=u_specs + gy_specs + [
                  pl.BlockSpec((1, 8, tp * w2), lambda j, i: (ci(i), 0, j)),
                  pl.BlockSpec((1, tp, 40 + nv, w2), lambda j, i: (var, j, 0, 0)),
                  pl.BlockSpec((1, tp, 40 + nv, w2), lambda j, i: (var + 1, j, 0, 0)),
                  tile3(TILE_CH, w2), tile3(w2, TILE_CH)],
        out_specs=[pl.BlockSpec((chunk, tp * TILE_CH), lambda j, i: (ci(i), j)),
                   pl.BlockSpec((8, tp * w2), lambda j, i: (0, j)),
                   tile3(TILE_CH, w2), tile3(w2, TILE_CH)],
        out_shape=[SDS((s, SSM_WIDTH), f32), SDS((8, SSM_TILES * w2), f32),
                   SDS((SSM_TILES, TILE_CH, w2), f32), SDS((SSM_TILES, w2, TILE_CH), f32)],
        scratch=[pltpu.VMEM((tp, chunk, TILE_CH), f32), pltpu.VMEM((tp, chunk, TILE_CH), f32),
                 pltpu.VMEM((tp, chunk, w2), f32), pltpu.VMEM((tp, chunk, w2), f32),
                 pltpu.VMEM((tp, 8, w2), f32), pltpu.VMEM((tp, 8, w2), f32)],
        args=(*([z] * tp), *([gy] * tp), xb, tabs, tabs, bmat, cmat), sem=("parallel", "arbitrary"),
        name=name, exchange=exchange)
    return (*outs, got)


GELU_K = math.sqrt(2.0 / math.pi)


def _gelu(y):
    return 0.5 * y * (1.0 + jnp.tanh(GELU_K * (y + 0.044715 * (y * y * y))))


def _gelu_grad(y):
    t = jnp.tanh(GELU_K * (y + 0.044715 * (y * y * y)))
    return 0.5 * (1.0 + t) + 0.5 * y * (1.0 - t * t) * (GELU_K * (1.0 + 3.0 * 0.044715 * (y * y)))


def glu_fwd(y_f, y_r, z, att, d_skip, w_glu, l, *, name, tm):
    s = z.shape[0]
    nblk, cb = w_glu.shape[1], w_glu.shape[3]

    def body(yf_ref, yr_ref, ua_ref, ub_ref, att_ref, d_ref, w_ref, y_ref, gg_ref, mix_ref):
        u = jnp.concatenate([ua_ref[...], ub_ref[...]], axis=1)
        y = d_ref[...] * u + yf_ref[...] + yr_ref[...]
        y_ref[...] = y
        yg = _gelu(y).astype(MX)
        for b in range(nblk):
            gg_ref[:, cb * b:cb * (b + 1)] = _dot(yg, w_ref[0, b])
        mix_ref[:, 0:ATT_WIDTH] = att_ref[...]
        mix_ref[:, ATT_WIDTH:] = (gg_ref[:, :SSM_WIDTH] * jax.nn.sigmoid(gg_ref[:, SSM_WIDTH:])).astype(MX)

    return pl.pallas_call(
        body, grid=(s // tm,),
        in_specs=[_row_spec(tm, SSM_WIDTH), _row_spec(tm, SSM_WIDTH),
                  pl.BlockSpec((tm, SSM_WIDTH // 2), lambda i: (i, U_OFF // (SSM_WIDTH // 2))),
                  pl.BlockSpec((tm, SSM_WIDTH // 2), lambda i: (i, U_OFF // (SSM_WIDTH // 2) + 1)),
                  _row_spec(tm, ATT_WIDTH), pl.BlockSpec((1, SSM_WIDTH), lambda i: (0, 0)), _layer_spec(w_glu, l)],
        out_specs=[_row_spec(tm, SSM_WIDTH), _row_spec(tm, 2 * SSM_WIDTH), _row_spec(tm, D_MODEL)],
        out_shape=[SDS((s, SSM_WIDTH), f32), SDS((s, 2 * SSM_WIDTH), f32), SDS((s, D_MODEL), MX)],
        compiler_params=_cp("parallel"), name=name)(y_f, y_r, z, z, att, d_skip.reshape(1, SSM_WIDTH), w_glu)


def glu_bwd(gmix, gg, ypre, w_glu, l, *, name, tm):
    s = gg.shape[0]
    nblk, cb = w_glu.shape[1], w_glu.shape[3]

    def body(gm_ref, gg_ref, y_ref, w_ref, ggg_ref, yg_ref, gy_ref):
        gs = gm_ref[...]
        val, gate = gg_ref[:, :SSM_WIDTH], gg_ref[:, SSM_WIDTH:]
        sg = jax.nn.sigmoid(gate)
        ggg_ref[:, :SSM_WIDTH] = (gs * sg).astype(MX)
        ggg_ref[:, SSM_WIDTH:] = (gs * val * sg * (1.0 - sg)).astype(MX)
        y = y_ref[...]
        yg_ref[...] = _gelu(y).astype(MX)
        gyg = _dot(ggg_ref[:, 0:cb], w_ref[0, 0], NT)
        for b in range(1, nblk):
            gyg = gyg + _dot(ggg_ref[:, cb * b:cb * (b + 1)], w_ref[0, b], NT)
        gy_ref[...] = gyg * _gelu_grad(y)

    return pl.pallas_call(
        body, grid=(s // tm,),
        in_specs=[pl.BlockSpec((tm, SSM_WIDTH), lambda i: (i, 1)), _row_spec(tm, 2 * SSM_WIDTH),
                  _row_spec(tm, SSM_WIDTH), _layer_spec(w_glu, l)],
        out_specs=[_row_spec(tm, 2 * SSM_WIDTH), _row_spec(tm, SSM_WIDTH), _row_spec(tm, SSM_WIDTH)],
        out_shape=[SDS((s, 2 * SSM_WIDTH), MX), SDS((s, SSM_WIDTH), MX), SDS((s, SSM_WIDTH), f32)],
        compiler_params=_cp("parallel"), name=name)(gmix, gg, ypre, w_glu)


def loss_grad(y, target, *, name, tm):
    s, d = y.shape

    def body(y_ref, t_ref, g_ref, g16_ref, l_ref):
        @pl.when(pl.program_id(0) == 0)
        def _():
            l_ref[...] = jnp.zeros_like(l_ref)

        e = y_ref[...] - t_ref[...]
        g = e * (1.0 / d)
        g_ref[...] = g
        g16_ref[...] = g.astype(MX)
        l_ref[...] += _rows8(e * e)

    row = pl.BlockSpec((tm, d), lambda i: (i, 0))
    return pl.pallas_call(
        body, grid=(s // tm,), in_specs=[row, row],
        out_specs=[row, row, pl.BlockSpec((8, d), lambda i: (0, 0))],
        out_shape=[SDS((s, d), f32), SDS((s, d), MX), SDS((8, d), f32)],
        compiler_params=_cp("arbitrary"), name=name)(y, target)


def _row_tile(rows, cols):
    tr = rows
    while tr * cols > 256 * 1024 and tr % 16 == 0:
        tr //= 2
    return tr


def _elementwise(fn, ins, n_out, *, name, out_dtype=f32):
    shape = ins[0].shape
    cols = shape[-1]
    ins2 = [a.reshape(-1, cols) for a in ins]
    rows = ins2[0].shape[0]
    tr = _row_tile(rows, cols)

    def body(*refs):
        outs = fn(*[r[...] for r in refs[:len(ins)]])
        for o_ref, o in zip(refs[len(ins):], outs):
            o_ref[...] = o.astype(out_dtype)

    spec = pl.BlockSpec((tr, cols), lambda i: (i, 0))
    outs = pl.pallas_call(
        body, grid=(rows // tr,), in_specs=[spec] * len(ins), out_specs=[spec] * n_out,
        out_shape=[SDS((rows, cols), out_dtype)] * n_out, compiler_params=_cp("parallel"), name=name)(*ins2)
    return [o.reshape(shape) for o in outs]


def _adamw_math(w, g, m, v):
    m = ADAM_B1 * m + (1.0 - ADAM_B1) * g
    v = ADAM_B2 * v + (1.0 - ADAM_B2) * (g * g)
    m_hat = m / (1.0 - ADAM_B1 ** ADAM_STEP)
    v_hat = v / (1.0 - ADAM_B2 ** ADAM_STEP)
    delta = -ADAM_LR * (m_hat / (jnp.sqrt(v_hat) + ADAM_EPS) + ADAM_WD * w)
    return delta, m, v


def adamw(w, g, m, v, *, name):
    return _elementwise(_adamw_math, [w, g, m, v], 3, name=name)


SMEM = pl.BlockSpec(memory_space=pltpu.SMEM)


def _core_index():
    return lax.axis_index("c").astype(jnp.int32).reshape(1)


def adamw_halves(w, own, sib, m, v, *, name):
    depth, r, cols = w.shape
    h = r // 2
    tr = _row_tile(h, cols)
    quad = lambda a: a.reshape(depth, 2, h, cols)

    def body(c_ref, w_ref, own_ref, sib_ref, m_ref, v_ref, g_ref, d_ref, mo_ref, vo_ref):
        g = jnp.where(pl.program_id(1) == c_ref[0], own_ref[0], sib_ref[0])
        g_ref[0, 0] = g
        d_ref[0, 0], mo_ref[0, 0], vo_ref[0, 0] = _adamw_math(w_ref[0, 0], g, m_ref[0, 0], v_ref[0, 0])

    full = pl.BlockSpec((1, 1, tr, cols), lambda l, j, i: (l, j, i, 0))
    part = pl.BlockSpec((1, tr, cols), lambda l, j, i: (l, i, 0))
    outs = pl.pallas_call(
        body, grid=(depth, 2, h // tr), in_specs=[SMEM, full, part, part, full, full], out_specs=[full] * 4,
        out_shape=[SDS((depth, 2, h, cols), f32)] * 4,
        compiler_params=_cp("parallel", "parallel", "parallel"), name=name)(
            _core_index(), quad(w), own, sib, quad(m), quad(v))
    return [o.reshape(depth, r, cols) for o in outs]


def add_own_half(g4, recv, *, name):
    _, _, h, cols = g4.shape
    tr = _row_tile(h, cols)

    def body(c_ref, g_ref, r_ref, o_ref):
        own = jnp.where(c_ref[0] == 0, g_ref[0, 0], g_ref[0, 1])
        o_ref[0] = (own + r_ref[0]).astype(WIRE)

    part = pl.BlockSpec((1, tr, cols), lambda s, i: (s, i, 0))
    return pl.pallas_call(
        body, grid=(4, h // tr),
        in_specs=[SMEM, pl.BlockSpec((1, 2, tr, cols), lambda s, i: (s, 0, i, 0)), part], out_specs=part,
        out_shape=SDS((4, h, cols), WIRE), compiler_params=_cp("parallel", "parallel"), name=name)(
            _core_index(), g4, recv)


def _chip_index():
    return (2 * lax.axis_index("x") + lax.axis_index("y")).astype(jnp.int32).reshape(1)


def sum_pieces(sums, got, *, name, into, layer):
    _, h, cols = sums.shape
    tr = _row_tile(h, cols)

    def body(me_ref, s_ref, g_ref, stack_ref, o_ref):
        del stack_ref
        own = s_ref[0]
        for s in range(1, 4):
            own = jnp.where(me_ref[0] == s, s_ref[s], own)
        o_ref[0] = ((own.astype(f32) + g_ref[0].astype(f32)) + g_ref[1].astype(f32)) + g_ref[2].astype(f32)

    return pl.pallas_call(
        body, grid=(h // tr,),
        in_specs=[SMEM, pl.BlockSpec((4, tr, cols), lambda i: (0, i, 0)),
                  pl.BlockSpec((3, tr, cols), lambda i: (0, i, 0)), ANY],
        out_specs=pl.BlockSpec((1, tr, cols), lambda i: (layer, i, 0)),
        out_shape=SDS(into.shape, f32), input_output_aliases={3: 0},
        compiler_params=_cp("parallel"), name=name)(_chip_index(), sums, got, into)


def sum4(a, *, name, into=None, layer=0):
    shape = a.shape[1:]
    cols = shape[-1]
    a2 = a.reshape(4, -1, cols)
    rows = a2.shape[1]
    tr = _row_tile(rows, cols)

    def body(*refs):
        a_ref, o_ref = refs[0], refs[-1]
        tot = ((a_ref[0].astype(f32) + a_ref[1].astype(f32)) + a_ref[2].astype(f32)) + a_ref[3].astype(f32)
        if into is None:
            o_ref[...] = tot
        else:
            o_ref[0] = tot

    in_spec = pl.BlockSpec((4, tr, cols), lambda i: (0, i, 0))
    if into is None:
        out = pl.pallas_call(
            body, grid=(rows // tr,), in_specs=[in_spec], out_specs=pl.BlockSpec((tr, cols), lambda i: (i, 0)),
            out_shape=SDS((rows, cols), f32), compiler_params=_cp("parallel"), name=name)(a2)
        return out.reshape(shape)
    stack = into.reshape(into.shape[0], rows, cols)
    out = pl.pallas_call(
        body, grid=(rows // tr,), in_specs=[in_spec, ANY],
        out_specs=pl.BlockSpec((1, tr, cols), lambda i: (layer, i, 0)),
        out_shape=SDS(stack.shape, f32), input_output_aliases={1: 0},
        compiler_params=_cp("parallel"), name=name)(a2, stack)
    return out.reshape(into.shape)


ANY = pl.BlockSpec(memory_space=pl.ANY)


def _chip_copies(ins, outs, send, recv, bcast):
    x, y, c = lax.axis_index("x"), lax.axis_index("y"), lax.axis_index("c")
    me = 2 * x + y
    copies = []
    for k in range(len(ins)):
        for j, (px, py) in enumerate(((1 - x, y), (x, 1 - y), (1 - x, 1 - y))):
            copies.append(pltpu.make_async_remote_copy(
                src_ref=ins[k] if bcast[k] else ins[k].at[2 * px + py],
                dst_ref=outs[k].at[me] if bcast[k] else outs[k].at[j],
                send_sem=send.at[4 * k + j], recv_sem=recv.at[4 * k + j],
                device_id=(px, py, c), device_id_type=MESH))
        if bcast[k]:
            copies.append(pltpu.make_async_remote_copy(
                src_ref=ins[k], dst_ref=outs[k].at[me], send_sem=send.at[4 * k + 3], recv_sem=recv.at[4 * k + 3],
                device_id=(x, y, 1 - c), device_id_type=MESH))
    return copies


def chip_exchange(arrs, bcast, *, name):
    n = len(arrs)

    def body(*refs):
        copies = _chip_copies(refs[:n], refs[n:2 * n], refs[2 * n], refs[2 * n + 1], bcast)
        for cp in copies:
            cp.start()
        for cp in copies:
            cp.wait()

    return pl.pallas_call(
        body, in_specs=[ANY] * n, out_specs=[ANY] * n,
        out_shape=[SDS((4,) + tuple(a.shape) if b else (3,) + tuple(a.shape[1:]), a.dtype)
                   for a, b in zip(arrs, bcast)],
        scratch_shapes=[pltpu.SemaphoreType.DMA((4 * n,)), pltpu.SemaphoreType.DMA((4 * n,))],
        name=name)(*arrs)


def gather_weights(shards, *, name):
    n = len(shards)
    hd = shards[0].shape[0] // 2

    def body(*refs):
        ins, outs = refs[:n], refs[n:2 * n]
        send, recv = refs[2 * n:]
        x, y, c = lax.axis_index("x"), lax.axis_index("y"), lax.axis_index("c")
        me = 2 * x + y
        chips = ((1 - x, y), (x, 1 - y), (1 - x, 1 - y))
        mine, theirs = pl.ds(c * hd, hd), pl.ds((1 - c) * hd, hd)

        def ici(k, j, src, dst):
            px, py = chips[j]
            return pltpu.make_async_remote_copy(src_ref=src, dst_ref=dst, send_sem=send.at[7 * k + j],
                                                recv_sem=recv.at[7 * k + j], device_id=(px, py, c),
                                                device_id_type=MESH)

        def d2d(k, j, src, dst):
            return pltpu.make_async_remote_copy(src_ref=src, dst_ref=dst, send_sem=send.at[7 * k + 3 + j],
                                                recv_sem=recv.at[7 * k + 3 + j], device_id=(x, y, 1 - c),
                                                device_id_type=MESH)

        own, sent = [], []
        for k in range(n):
            own.append(d2d(k, 3, ins[k], outs[k].at[:, me]))
            own[-1].start()
            for j in range(3):
                sent.append(ici(k, j, ins[k].at[mine], outs[k].at[mine, me]))
                sent[-1].start()
        for k in range(n):
            for j, (px, py) in enumerate(chips):
                landed = outs[k].at[mine, 2 * px + py]
                ici(k, j, landed, landed).wait_recv()
                sent.append(d2d(k, j, landed, landed))
                sent[-1].start()
        for k in range(n):
            for j, (px, py) in enumerate(chips):
                other = outs[k].at[theirs, 2 * px + py]
                d2d(k, j, other, other).wait_recv()
        for cp in sent:
            cp.wait_send()
        for cp in own:
            cp.wait()

    return pl.pallas_call(
        body, in_specs=[ANY] * n, out_specs=[ANY] * n,
        out_shape=[SDS((a.shape[0], 4) + tuple(a.shape[1:]), a.dtype) for a in shards],
        scratch_shapes=[pltpu.SemaphoreType.DMA((7 * n,)), pltpu.SemaphoreType.DMA((7 * n,))],
        name=name)(*shards)


def _sibling_copies(ins, outs, send, recv, half):
    x, y, c = lax.axis_index("x"), lax.axis_index("y"), lax.axis_index("c")
    return [pltpu.make_async_remote_copy(
        src_ref=ins[k].at[:, 1 - c] if half[k] else ins[k], dst_ref=outs[k], send_sem=send.at[k],
        recv_sem=recv.at[k], device_id=(x, y, 1 - c), device_id_type=MESH) for k in range(len(ins))]


def sibling_exchange(arrs, half, *, name):
    n = len(arrs)
    piece = [(a.shape[0],) + a.shape[2:] if h else a.shape for a, h in zip(arrs, half)]

    def body(*refs):
        copies = _sibling_copies(refs[:n], refs[n:2 * n], refs[2 * n], refs[2 * n + 1], half)
        for cp in copies:
            cp.start()
        for cp in copies:
            cp.wait()

    return pl.pallas_call(
        body, in_specs=[ANY] * n, out_specs=[ANY] * n,
        out_shape=[SDS(tuple(p), a.dtype) for p, a in zip(piece, arrs)],
        scratch_shapes=[pltpu.SemaphoreType.DMA((n,)), pltpu.SemaphoreType.DMA((n,))],
        name=name)(*arrs)


def ssm_discretize(lam_re, lam_im, log_dt, b_re, b_im, c_re, c_im):
    dt = jnp.exp(log_dt)[..., None]
    mag = jnp.exp(lam_re * dt)
    abr = mag * jnp.cos(lam_im * dt)
    abi = mag * jnp.sin(lam_im * dt)
    den = lam_re * lam_re + lam_im * lam_im
    zr = ((abr - 1.0) * lam_re + abi * lam_im) / den
    zi = (abi * lam_re - (abr - 1.0) * lam_im) / den
    bbr = zr[..., None] * b_re - zi[..., None] * b_im
    bbi = zr[..., None] * b_im + zi[..., None] * b_re
    eye = jnp.eye(8, dtype=f32)
    bb = jnp.stack([bbr, bbi], axis=1).reshape(2, 2, SSM_TILES, 8, SSM_STATE, SSM_GROUP)
    bmat = jnp.einsum('dqjgph,gk->djghqkp', bb, eye).reshape(2, SSM_TILES, TILE_CH, 2 * TILE_ST)
    cc = jnp.stack([c_re, -c_im], axis=1).reshape(2, 2, SSM_TILES, 8, SSM_GROUP, SSM_STATE)
    cmat = jnp.einsum('dqjghp,gk->djqkpgh', cc, eye).reshape(2, SSM_TILES, 2 * TILE_ST, TILE_CH)
    n = SSM_GROUPS * SSM_STATE
    return abr.reshape(2, n), abi.reshape(2, n), bmat, cmat


SCAN_REV = (False, True, True, False)


def scan_tables_all(ar, ai, *, nv, name):
    a8 = jnp.stack([ar[0], ai[0], ar[0], -ai[0], ar[1], ai[1], ar[1], -ai[1]])
    n_state = SSM_TILES * TILE_ST

    def body(a_ref, o_ref):
        row = lax.broadcasted_iota(jnp.int32, (8, n_state), 0)

        def put(i, rows, re, im):
            for t in range(SSM_TILES):
                o_ref[i, t, rows, 0:TILE_ST] = re[:, TILE_ST * t:TILE_ST * (t + 1)]
                o_ref[i, t, rows, TILE_ST:] = im[:, TILE_ST * t:TILE_ST * (t + 1)]

        for i, rev in enumerate(SCAN_REV):
            a_r, a_i = a_ref[2 * i:2 * i + 1, :], a_ref[2 * i + 1:2 * i + 2, :]
            pr, pi = a_r, a_i
            for v in range(nv):
                dst = 40 + (nv - 1 - v if rev else v)
                put(i, slice(dst, dst + 1), pr, pi)
                if v + 1 < nv:
                    pr, pi = _cmul(a_r, a_i, pr, pi)
            big = (pr, pi)
            put(i, slice(24, 32), jnp.broadcast_to(big[0], (8, n_state)), jnp.broadcast_to(big[1], (8, n_state)))
            put(i, slice(32, 40), jnp.broadcast_to(a_r, (8, n_state)), jnp.broadcast_to(a_i, (8, n_state)))
            for n, k in enumerate((1, 2, 4)):
                cond = (row <= 7 - k) if rev else (row >= k)
                put(i, slice(8 * n, 8 * n + 8), jnp.where(cond, big[0], 0.0), jnp.where(cond, big[1], 0.0))
                big = _cmul(*big, *big)

    return pl.pallas_call(
        body, out_shape=SDS((4, SSM_TILES, 40 + nv, 2 * TILE_ST), f32),
        compiler_params=pltpu.CompilerParams(vmem_limit_bytes=VMEM_LIMIT), name=name)(a8)


def _tile_a(ga):
    t = ga.sum(axis=0).reshape(SSM_TILES, 2, TILE_ST)
    return t[:, 0].reshape(-1), t[:, 1].reshape(-1)


SMALL = ('norm1', 'q_gain', 'k_gain', 'sink', 'lam_re', 'lam_im', 'log_dt', 'b_re', 'b_im', 'c_re', 'c_im',
         'd_skip', 'norm2')
BIG = ('w_in', 'w_glu', 'w_out', 'w_ff1', 'w_ff2')
WEIGHTS = ('norm1', 'w_in', 'q_gain', 'k_gain', 'sink', 'lam_re', 'lam_im', 'log_dt', 'b_re', 'b_im', 'c_re',
           'c_im', 'd_skip', 'w_glu', 'w_out', 'norm2', 'w_ff1', 'w_ff2')


def _chunk(s):
    return min(512, s)


FETCH_ATTN = ('w_in', 'w_glu', 'w_out')
FETCH_SSM = ('w_ff1', 'w_ff2')


def layer_forward(l, x, p, wb, li, fetch=None):
    s = x.shape[0]
    tm = min(512, s)
    sv = {}
    h1, z = norm_mm(x, p['norm1'], wb['w_in'], li, relu2=False, name=f"l{l}_in", tm=tm)
    eq, ek = head_mean_matrix(ATT_WIDTH), head_mean_matrix(KV_WIDTH)
    qn, kv = qk_prep(z, p['q_gain'], p['k_gain'], eq, ek, name=f"l{l}_qk", tm=tm)
    att, got = attn_fwd(qn, kv, p['sink'], name=f"l{l}_attn",
                        exchange=fetch and ("chips", [fetch[k] for k in FETCH_ATTN], [True] * len(FETCH_ATTN)))
    fetched = dict(zip(FETCH_ATTN, got)) if fetch else None
    sv.update(qn=qn, kv=kv, eq=eq, ek=ek)
    (ar, ai, bmat, cmat), disc_vjp = jax.vjp(
        ssm_discretize, p['lam_re'], p['lam_im'], p['log_dt'], p['b_re'], p['b_im'], p['c_re'], p['c_im'])
    bmat16, cmat16 = bmat.astype(MX), cmat.astype(MX)
    ys, xbs = [], []
    tabs = scan_tables_all(ar, ai, nv=_chunk(s) // 8, name=f"l{l}_tabs")
    for d, rev in enumerate((False, True)):
        y_d, xb_d, got = ssm_fwd(z, tabs, bmat16[d], cmat16[d], rev=rev, name=f"l{l}_ssm{d}", chunk=_chunk(s),
                                 exchange=fetch and ("chips", [fetch[FETCH_SSM[d]]], [True]))
        if fetch:
            fetched[FETCH_SSM[d]] = got[0]
        ys.append(y_d)
        xbs.append(xb_d)
    ypre, gg, mix = glu_fwd(ys[0], ys[1], z, att, p['d_skip'], wb['w_glu'], li, name=f"l{l}_glu", tm=min(256, s))
    x1 = mm_res(mix, wb['w_out'], li, x, name=f"l{l}_out", tm=tm)
    h2, a2 = norm_mm(x1, p['norm2'], wb['w_ff1'], li, relu2=True, name=f"l{l}_ff1", tm=tm)
    x2 = mm_res(a2, wb['w_ff2'], li, x1, name=f"l{l}_ff2", tm=tm)
    sv.update(x=x, h1=h1, z=z, xbs=xbs, tabs=tabs, bmat16=bmat16, cmat16=cmat16, disc_vjp=disc_vjp,
              ypre=ypre, gg=gg, mix=mix, x1=x1, h2=h2, a2=a2)
    return x2, sv, fetched


def layer_backward(l, gx2, gx2h, p, wb, li, sv, pend=None):
    s = gx2.shape[0]
    tm = min(512, s)
    ts = min(1024, s)
    g = {}
    gf, got = mm_nt(gx2h, wb['w_ff2'], li, name=f"l{l}_bff2", tm=tm, a2=sv['a2'],
                    exchange=pend and ("sibling", pend[1], [True] * len(pend[1])))
    sums = pend and [add_own_half(a, b, name=f"l{l}_radd_{k}") for k, a, b in zip(BIG, pend[1], got)]
    g['w_ff2'] = mm_tn(sv['a2'], gx2h, name=f"l{l}_wff2", tk=1024, tn=1024, ts=ts).reshape(4, D_FF // 4, D_MODEL)
    gx1, gx1h, gn2 = mm_nt_norm(gf, wb['w_ff1'], li, sv['x1'], p['norm2'], gx2, name=f"l{l}_bff1", tm=min(256, s))
    g['norm2'] = gn2.sum(axis=0)
    g['w_ff1'] = mm_tn(sv['h2'], gf, name=f"l{l}_wff1", tk=1024, tn=1024, ts=ts, chip_major=True)
    gmix, _ = mm_nt(gx1h, wb['w_out'], li, name=f"l{l}_bout", tm=tm)
    g['w_out'] = mm_tn(sv['mix'], gx1h, name=f"l{l}_wout", tk=1024, tn=1024, ts=ts).reshape(4, D_MODEL // 4, D_MODEL)
    ggg, yg, gy = glu_bwd(gmix, sv['gg'], sv['ypre'], wb['w_glu'], li, name=f"l{l}_bglu", tm=min(256, s))
    g['w_glu'] = mm_tn(yg, ggg, name=f"l{l}_wglu", tk=512, tn=256, ts=ts, chip_major=True)
    gus, gas, gbs, gcs = [], [], [], []
    for d, rev in enumerate((False, True)):
        gu_d, ga_d, gb_d, gc_d, got = ssm_bwd(
            sv['z'], gy, sv['xbs'][d], sv['tabs'], sv['bmat16'][d], sv['cmat16'][d], rev=rev,
            name=f"l{l}_bssm{d}", chunk=_chunk(s),
            exchange=(pend and d == 0) and ("chips", sums, [False] * len(sums)) or None)
        if pend and d == 0:
            pend[0](sums, got)
        gus.append(gu_d)
        gas.append(_tile_a(ga_d))
        gbs.append(gb_d)
        gcs.append(gc_d)
    gar = jnp.stack([gas[0][0], gas[1][0]])
    gai = jnp.stack([gas[0][1], gas[1][1]])
    (g['lam_re'], g['lam_im'], g['log_dt'], g['b_re'], g['b_im'], g['c_re'], g['c_im']) = sv['disc_vjp'](
        (gar, gai, jnp.stack(gbs), jnp.stack(gcs)))
    gqs, dkv, gsk = attn_bwd(sv['qn'], sv['kv'], gmix, p['sink'], name=f"l{l}_battn")
    g['sink'] = gsk[:, 0]
    gz, gqg, gkg, gd = gz_assemble(gqs, dkv, sv['z'], p['q_gain'], p['k_gain'], sv['eq'], sv['ek'], gus[0], gus[1],
                                   gy, p['d_skip'], name=f"l{l}_gz")
    g['q_gain'] = gqg.sum(axis=0).reshape(ATT_HEADS, HEAD_DIM).sum(axis=0)
    g['k_gain'] = gkg.sum(axis=0).reshape(KV_HEADS, HEAD_DIM).sum(axis=0)
    g['d_skip'] = gd.sum(axis=0)
    gx, gxh, gn1 = mm_nt_norm(gz, wb['w_in'], li, sv['x'], p['norm1'], gx1, name=f"l{l}_bin", tm=tm)
    g['norm1'] = gn1.sum(axis=0)
    gw_in = mm_tn(sv['h1'], gz, name=f"l{l}_win", tk=1024, tn=640, ts=ts)
    g['w_in'] = gw_in.reshape(D_MODEL, 4, IN_WIDTH // 4).transpose(1, 0, 2)
    return gx, gxh, g


def stack_layouts(gathered):
    w_in = gathered['w_in']
    depth = w_in.shape[0]
    return dict(w_in=w_in.transpose(0, 2, 1, 3).reshape(depth, D_MODEL, IN_WIDTH),
                w_glu=gathered['w_glu'], w_ff1=gathered['w_ff1'],
                w_out=gathered['w_out'].reshape(depth, D_MODEL, D_MODEL),
                w_ff2=gathered['w_ff2'].reshape(depth, D_FF, D_MODEL))


def local_step(x, target, small, wb):
    depth = wb['w_in'].shape[0]
    saves = []
    for l in range(depth):
        x, sv, _ = layer_forward(l, x, {k: small[k][l] for k in SMALL}, wb, l)
        saves.append(sv)
    gx, gxh, lparts = loss_grad(x, target, name="loss", tm=min(512, x.shape[0]))
    grads = [None] * depth
    for l in reversed(range(depth)):
        gx, gxh, grads[l] = layer_backward(l, gx, gxh, {k: small[k][l] for k in SMALL}, wb, l, saves[l])
    return lparts, gx, grads


def reduce_pieces(g):
    return [g[k].reshape(4, 2, g[k].shape[1] // 2, g[k].shape[2]) for k in BIG]


def reduce_chips(l, sums, got, stacks):
    return {k: sum_pieces(a, b, name=f"l{l}_rsum_{k}", into=stacks[k], layer=l) for k, a, b in zip(BIG, sums, got)}


def gather_first(shards):
    halves = [a.reshape(2, a.shape[0] // 2, a.shape[1]) for a in shards]
    got = gather_weights(halves, name="gather_first")
    return [a.transpose(1, 0, 2, 3).reshape(4, 2 * a.shape[2], a.shape[3]) for a in got]


def reduce_small(packed):
    got = sibling_exchange([packed], [False], name="small_rsib")
    pair = _elementwise(lambda a, b: (a + b,), [packed, got[0]], 1, name="small_radd")[0]
    got = chip_exchange([pair], [True], name="small_rchips")
    return sum4(got[0], name="small_rsum")


def _pack_small(tree):
    parts = []
    for k in SMALL:
        flat = tree[k].reshape(-1)
        parts.append(jnp.pad(flat, (0, (-flat.shape[0]) % 1024)).reshape(-1, 128))
    return jnp.concatenate(parts, axis=0)


def _unpack_small(packed, like):
    out, row = {}, 0
    for k in SMALL:
        n = like[k].size
        rows = -(-n // 1024) * 8
        out[k] = packed[row:row + rows].reshape(-1)[:n].reshape(like[k].shape)
        row += rows
    return out


def kernel(x, norm1, w_in, q_gain, k_gain, sink, lam_re, lam_im, log_dt, b_re, b_im, c_re, c_im, d_skip, w_glu, w_out, norm2, w_ff1, w_ff2, loss_target, m_norm1, m_w_in, m_q_gain, m_k_gain, m_sink, m_lam_re, m_lam_im, m_log_dt, m_b_re, m_b_im, m_c_re, m_c_im, m_d_skip, m_w_glu, m_w_out, m_norm2, m_w_ff1, m_w_ff2, v_norm1, v_w_in, v_q_gain, v_k_gain, v_sink, v_lam_re, v_lam_im, v_log_dt, v_b_re, v_b_im, v_c_re, v_c_im, v_d_skip, v_w_glu, v_w_out, v_norm2, v_w_ff1, v_w_ff2):
    w = dict(norm1=norm1, w_in=w_in, q_gain=q_gain, k_gain=k_gain, sink=sink, lam_re=lam_re, lam_im=lam_im,
             log_dt=log_dt, b_re=b_re, b_im=b_im, c_re=c_re, c_im=c_im, d_skip=d_skip, w_glu=w_glu, w_out=w_out,
             norm2=norm2, w_ff1=w_ff1, w_ff2=w_ff2)
    m = dict(norm1=m_norm1, w_in=m_w_in, q_gain=m_q_gain, k_gain=m_k_gain, sink=m_sink, lam_re=m_lam_re,
             lam_im=m_lam_im, log_dt=m_log_dt, b_re=m_b_re, b_im=m_b_im, c_re=m_c_re, c_im=m_c_im,
             d_skip=m_d_skip, w_glu=m_w_glu, w_out=m_w_out, norm2=m_norm2, w_ff1=m_w_ff1, w_ff2=m_w_ff2)
    v = dict(norm1=v_norm1, w_in=v_w_in, q_gain=v_q_gain, k_gain=v_k_gain, sink=v_sink, lam_re=v_lam_re,
             lam_im=v_lam_im, log_dt=v_log_dt, b_re=v_b_re, b_im=v_b_im, c_re=v_c_re, c_im=v_c_im,
             d_skip=v_d_skip, w_glu=v_w_glu, w_out=v_w_out, norm2=v_norm2, w_ff1=v_w_ff1, w_ff2=v_w_ff2)
    depth = w_in.shape[0]

    shards = {k: w[k].astype(WIRE) for k in BIG}
    small = {k: w[k] for k in SMALL}
    stacks = [{k: jnp.zeros((depth, w[k].shape[1] // 2, w[k].shape[2]), f32) for k in BIG}]

    xs = x[0]
    gathered = dict(zip(BIG, gather_first([shards[k][0] for k in BIG])))
    saves, wbs = [], []
    for l in range(depth):
        wbs.append(stack_layouts({k: gathered[k][None] for k in BIG}))
        fetch = {k: shards[k][l + 1] for k in BIG} if l + 1 < depth else None
        xs, sv, gathered = layer_forward(l, xs, {k: small[k][l] for k in SMALL}, wbs[l], 0, fetch)
        saves.append(sv)
    gx, gxh, lparts = loss_grad(xs, loss_target[0], name="loss", tm=min(512, xs.shape[0]))
    loss = lax.psum(0.5 * jnp.sum(lparts) / D_MODEL, ("x", "y", "c"))

    def finisher(l):
        def finish(sums, got):
            stacks[0] = reduce_chips(l, sums, got, stacks[0])
        return finish

    grads, pend = [None] * depth, None
    for l in reversed(range(depth)):
        gx, gxh, g = layer_backward(l, gx, gxh, {k: small[k][l] for k in SMALL}, wbs[l], 0, saves[l], pend)
        pend = (finisher(l), reduce_pieces(g))
        grads[l] = {k: g[k] for k in SMALL}
    got = sibling_exchange(pend[1], [True] * len(BIG), name="last_rsib")
    sums = [add_own_half(a, b, name=f"last_radd_{k}") for k, a, b in zip(BIG, pend[1], got)]
    pend[0](sums, chip_exchange(sums, [False] * len(BIG), name="last_rchips"))

    sib = sibling_exchange([stacks[0][k] for k in BIG], [False] * len(BIG), name="reduce_back")
    gsmall = reduce_small(_pack_small({k: jnp.stack([grads[l][k] for l in range(depth)]) for k in SMALL}))
    like = {k: w[k] for k in SMALL}
    gfull = _unpack_small(gsmall, like)

    delta, new_m, new_v = {}, {}, {}
    for k, sib_k in zip(BIG, sib):
        gfull[k], delta[k], new_m[k], new_v[k] = adamw_halves(w[k], stacks[0][k], sib_k, m[k], v[k],
                                                              name=f"adamw_{k}")
    ds, ms, vs = adamw(_pack_small(like), gsmall, _pack_small({k: m[k] for k in SMALL}),
                       _pack_small({k: v[k] for k in SMALL}), name="adamw_small")
    delta.update(_unpack_small(ds, like))
    new_m.update(_unpack_small(ms, like))
    new_v.update(_unpack_small(vs, like))

    return (loss, gx[None], *[gfull[k] for k in WEIGHTS], *[delta[k] for k in WEIGHTS],
            *[new_m[k] for k in WEIGHTS], *[new_v[k] for k in WEIGHTS])
```

```python
import functools
import math

import jax
import jax.numpy as jnp
from jax import lax
from jax.experimental import pallas as pl
from jax.experimental.pallas import tpu as pltpu

f32 = jnp.float32
MX = jnp.bfloat16
WIRE = jnp.bfloat16
SDS = jax.ShapeDtypeStruct

D_MODEL = 1024
DEPTH = 4
ATT_HEADS = 8
KV_HEADS = 2
GQA = ATT_HEADS // KV_HEADS
HEAD_DIM = 64
ATT_WIDTH = ATT_HEADS * HEAD_DIM
KV_WIDTH = KV_HEADS * HEAD_DIM
BLOCK = 128
SSM_WIDTH = 512
SSM_GROUP = 16
SSM_GROUPS = 32
SSM_STATE = 64
SSM_TILES = 4
TILE_CH = SSM_WIDTH // SSM_TILES
TILE_ST = SSM_GROUPS * SSM_STATE // SSM_TILES
TILES_PER_STEP = 2
IN_WIDTH = ATT_WIDTH + 2 * KV_WIDTH + SSM_WIDTH
U_OFF = ATT_WIDTH + 2 * KV_WIDTH
D_FF = 4096
EPS = 1e-6
NEG = float(jnp.finfo(jnp.float32).min)
SLOPES = tuple(2.0 ** (-8.0 * (h + 1) / ATT_HEADS) for h in range(ATT_HEADS))

ADAM_LR, ADAM_B1, ADAM_B2, ADAM_EPS, ADAM_WD, ADAM_STEP = 0.001, 0.9, 0.999, 1e-08, 0.01, 10

VMEM_LIMIT = 48 * 1024 * 1024
MESH = pl.DeviceIdType.MESH

NT = (((1,), (1,)), ((), ()))
TN = (((0,), (0,)), ((), ()))


def _cp(*sem):
    return pltpu.CompilerParams(dimension_semantics=sem, vmem_limit_bytes=VMEM_LIMIT)


def _dot(a, b, dims=None):
    if dims is None:
        return jnp.dot(a, b, preferred_element_type=f32)
    return lax.dot_general(a, b, dims, preferred_element_type=f32)


def _rows8(v):
    return v.reshape(v.shape[0] // 8, 8, v.shape[1]).sum(axis=0)


def _layer_spec(w, l):
    nd = w.ndim
    return pl.BlockSpec((1,) + tuple(w.shape[1:]), lambda i: (l,) + (0,) * (nd - 1))


def _row_spec(tm, width):
    return pl.BlockSpec((tm, width), lambda i: (i, 0))


def _call(body, *, grid, in_specs, out_specs, out_shape, args, sem, name, scratch=(), exchange=None):
    n_in, n_out, n_scr = len(in_specs), len(out_specs), len(scratch)
    if exchange is None:
        res = pl.pallas_call(body, grid=grid, in_specs=in_specs, out_specs=out_specs, out_shape=out_shape,
                             scratch_shapes=list(scratch), compiler_params=_cp(*sem), name=name)(*args)
        return list(res), []
    kind, arrs, flags = exchange
    nx = len(arrs)
    if kind == "chips":
        make, nsem = _chip_copies, 4 * nx
        got = [SDS((4,) + tuple(a.shape) if b else (3,) + tuple(a.shape[1:]), a.dtype) for a, b in zip(arrs, flags)]
    else:
        make, nsem = _sibling_copies, nx
        got = [SDS((a.shape[0],) + tuple(a.shape[2:]) if h else tuple(a.shape), a.dtype)
               for a, h in zip(arrs, flags)]

    def hosted(*refs):
        ins, xin = refs[:n_in], refs[n_in:n_in + nx]
        outs = refs[n_in + nx:n_in + nx + n_out]
        xout = refs[n_in + nx + n_out:n_in + 2 * nx + n_out]
        scr = refs[n_in + 2 * nx + n_out:]
        copies = make(xin, xout, scr[n_scr], scr[n_scr + 1], flags)
        first = functools.reduce(jnp.logical_and, [pl.program_id(d) == 0 for d in range(len(grid))])
        last = functools.reduce(jnp.logical_and, [pl.program_id(d) == grid[d] - 1 for d in range(len(grid))])

        @pl.when(first)
        def _():
            for cp in copies:
                cp.start()

        body(*ins, *outs, *scr[:n_scr])

        @pl.when(last)
        def _():
            for cp in copies:
                cp.wait()

    res = pl.pallas_call(
        hosted, grid=grid, in_specs=list(in_specs) + [ANY] * nx, out_specs=list(out_specs) + [ANY] * nx,
        out_shape=list(out_shape) + got,
        scratch_shapes=list(scratch) + [pltpu.SemaphoreType.DMA((nsem,)), pltpu.SemaphoreType.DMA((nsem,))],
        compiler_params=_cp(*["arbitrary"] * len(grid)), name=name)(*args, *arrs)
    return list(res[:n_out]), list(res[n_out:])


def norm_mm(x, gain, w, l, *, relu2, name, tm, exchange=None):
    s, d = x.shape
    if relu2:
        nblk, cb = w.shape[1], w.shape[3]
        n = nblk * cb
    else:
        n = w.shape[2]

    def body(x_ref, g_ref, w_ref, h_ref, y_ref):
        xf = x_ref[...]
        r = lax.rsqrt(jnp.mean(xf * xf, axis=-1, keepdims=True) + EPS)
        h = (xf * r * g_ref[...]).astype(MX)
        h_ref[...] = h
        if relu2:
            for b in range(nblk):
                f = jnp.maximum(_dot(h, w_ref[0, b]), 0.0)
                y_ref[:, cb * b:cb * (b + 1)] = (f * f).astype(MX)
        else:
            y_ref[...] = _dot(h, w_ref[0])

    (h, y), got = _call(
        body, grid=(s // tm,),
        in_specs=[_row_spec(tm, d), pl.BlockSpec((1, d), lambda i: (0, 0)), _layer_spec(w, l)],
        out_specs=[_row_spec(tm, d), _row_spec(tm, n)],
        out_shape=[SDS((s, d), MX), SDS((s, n), MX if relu2 else f32)],
        args=(x, gain.reshape(1, d), w), sem=("parallel",), name=name, exchange=exchange)
    return h, y, got


def mm_res(a, w, l, res, *, name, tm, exchange=None):
    s, k = a.shape
    n = w.shape[2]

    def body(a_ref, w_ref, r_ref, o_ref):
        o_ref[...] = r_ref[...] + _dot(a_ref[...], w_ref[0])

    (out,), got = _call(
        body, grid=(s // tm,), in_specs=[_row_spec(tm, k), _layer_spec(w, l), _row_spec(tm, n)],
        out_specs=[_row_spec(tm, n)], out_shape=[SDS((s, n), f32)], args=(a, w, res), sem=("parallel",),
        name=name, exchange=exchange)
    return out, got


def mm_nt(gy, w, l, *, name, tm, a2=None, exchange=None):
    s, n = gy.shape
    k = w.shape[1]
    kb = min(k, 1024)

    def body(*refs):
        g_ref, w_ref, o_ref = refs[0], refs[1], refs[-1]
        g = g_ref[...]
        for b in range(k // kb):
            cols = slice(kb * b, kb * (b + 1))
            acc = _dot(g, w_ref[0, cols, :], NT)
            if a2 is not None:
                acc = acc * (2.0 * jnp.sqrt(refs[2][:, cols].astype(f32)))
            o_ref[:, cols] = acc.astype(o_ref.dtype)

    in_specs = [_row_spec(tm, n), _layer_spec(w, l)]
    args = [gy, w]
    if a2 is not None:
        in_specs.append(_row_spec(tm, k))
        args.append(a2)
    (out,), got = _call(body, grid=(s // tm,), in_specs=in_specs, out_specs=[_row_spec(tm, k)],
                        out_shape=[SDS((s, k), f32 if a2 is None else MX)], args=args, sem=("parallel",),
                        name=name, exchange=exchange)
    return out, got


def mm_nt_norm(gy, w, l, x, gain, res, *, name, tm):
    s, n = gy.shape
    d = x.shape[1]

    def body(g_ref, w_ref, x_ref, gn_ref, r_ref, o_ref, o16_ref, gg_ref):
        @pl.when(pl.program_id(0) == 0)
        def _():
            gg_ref[...] = jnp.zeros_like(gg_ref)

        if w.ndim == 3:
            gh = _dot(g_ref[...], w_ref[0], NT)
        else:
            cb = w.shape[3]
            gh = _dot(g_ref[:, 0:cb], w_ref[0, 0], NT)
            for b in range(1, w.shape[1]):
                gh = gh + _dot(g_ref[:, cb * b:cb * (b + 1)], w_ref[0, b], NT)
        xf = x_ref[...]
        r = lax.rsqrt(jnp.mean(xf * xf, axis=-1, keepdims=True) + EPS)
        xh = xf * r
        t = gh * gn_ref[...]
        gx = r_ref[...] + r * (t - xh * jnp.mean(t * xh, axis=-1, keepdims=True))
        o_ref[...] = gx
        o16_ref[...] = gx.astype(MX)
        gg_ref[...] += _rows8(gh * xh)

    return pl.pallas_call(
        body, grid=(s // tm,),
        in_specs=[_row_spec(tm, n), _layer_spec(w, l), _row_spec(tm, d), pl.BlockSpec((1, d), lambda i: (0, 0)),
                  _row_spec(tm, d)],
        out_specs=[_row_spec(tm, d), _row_spec(tm, d), pl.BlockSpec((8, d), lambda i: (0, 0))],
        out_shape=[SDS((s, d), f32), SDS((s, d), MX), SDS((8, d), f32)],
        compiler_params=_cp("arbitrary"), name=name)(gy, w, x, gain.reshape(1, d), res)


def mm_tn(xa, gy, *, name, tk, tn, ts, chip_major=False):
    s, k = xa.shape
    n = gy.shape[1]

    def body(x_ref, g_ref, o_ref):
        @pl.when(pl.program_id(2) == 0)
        def _():
            o_ref[...] = jnp.zeros_like(o_ref)

        acc = _dot(x_ref[...], g_ref[...], TN)
        if chip_major:
            o_ref[0] += acc
        else:
            o_ref[...] += acc

    if chip_major:
        out_spec = pl.BlockSpec((1, tk, tn), lambda a, b, c: (b, a, 0))
        out_shape = SDS((n // tn, k, tn), f32)
    else:
        out_spec = pl.BlockSpec((tk, tn), lambda a, b, c: (a, b))
        out_shape = SDS((k, n), f32)
    return pl.pallas_call(
        body, grid=(k // tk, n // tn, s // ts),
        in_specs=[pl.BlockSpec((ts, tk), lambda a, b, c: (c, a)), pl.BlockSpec((ts, tn), lambda a, b, c: (c, b))],
        out_specs=out_spec, out_shape=out_shape,
        compiler_params=_cp("parallel", "parallel", "arbitrary"), name=name)(xa, gy)


def head_mean_matrix(width):
    return jnp.kron(jnp.eye(width // HEAD_DIM, dtype=f32), jnp.full((HEAD_DIM, HEAD_DIM), 1.0 / HEAD_DIM, f32)).astype(MX)


def _head_mean(t, e_ref):
    hi = t.astype(MX)
    lo = (t - hi.astype(f32)).astype(MX)
    return _dot(hi, e_ref[...]) + _dot(lo, e_ref[...])


def qk_prep(z, q_gain, k_gain, eq, ek, *, name, tm):
    s = z.shape[0]

    def body(z_ref, qg_ref, kg_ref, eq_ref, ek_ref, q_ref, kv_ref):
        q = z_ref[:, 0:ATT_WIDTH]
        r = lax.rsqrt(_head_mean(q * q, eq_ref) + EPS)
        q_ref[...] = ((q * r * qg_ref[...]) * 0.125).astype(MX)
        k = z_ref[:, ATT_WIDTH:ATT_WIDTH + KV_WIDTH]
        r = lax.rsqrt(_head_mean(k * k, ek_ref) + EPS)
        kv_ref[:, 0:KV_WIDTH] = (k * r * kg_ref[...]).astype(MX)
        kv_ref[:, KV_WIDTH:] = z_ref[:, ATT_WIDTH + KV_WIDTH:U_OFF].astype(MX)

    const = lambda a: pl.BlockSpec(a.shape, lambda i: (0, 0))
    qg = jnp.tile(q_gain.reshape(1, HEAD_DIM), (1, ATT_HEADS))
    kg = jnp.tile(k_gain.reshape(1, HEAD_DIM), (1, KV_HEADS))
    return pl.pallas_call(
        body, grid=(s // tm,), in_specs=[_row_spec(tm, IN_WIDTH), const(qg), const(kg), const(eq), const(ek)],
        out_specs=[_row_spec(tm, ATT_WIDTH), _row_spec(tm, 2 * KV_WIDTH)],
        out_shape=[SDS((s, ATT_WIDTH), MX), SDS((s, 2 * KV_WIDTH), MX)],
        compiler_params=_cp("parallel"), name=name)(z, qg, kg, eq, ek)


def _attn_mask(i, nb):
    row = lax.broadcasted_iota(jnp.int32, (GQA * BLOCK, 3 * BLOCK), 0) & (BLOCK - 1)
    col = lax.broadcasted_iota(jnp.int32, (GQA * BLOCK, 3 * BLOCK), 1)
    dist = jnp.abs(row - col + BLOCK)
    valid = (dist <= BLOCK) & ((col >= BLOCK) | (i >= 1)) & ((col < 2 * BLOCK) | (i <= nb - 2))
    return dist.astype(f32), valid


def _attn_specs(nb):
    return [pl.BlockSpec((BLOCK, ATT_WIDTH), lambda i: (i, 0)),
            pl.BlockSpec((BLOCK, 2 * KV_WIDTH), lambda i: (jnp.maximum(i - 1, 0), 0)),
            pl.BlockSpec((BLOCK, 2 * KV_WIDTH), lambda i: (i, 0)),
            pl.BlockSpec((BLOCK, 2 * KV_WIDTH), lambda i: (jnp.minimum(i + 1, nb - 1), 0)),
            pl.BlockSpec(memory_space=pltpu.SMEM)]


def _attn_probs(sc, kvh, distf, valid, sink_ref):
    row = lax.broadcasted_iota(jnp.int32, (GQA * BLOCK, 1), 0)
    slope = jnp.full((GQA * BLOCK, 1), SLOPES[GQA * kvh], f32)
    sk = jnp.full((GQA * BLOCK, 1), sink_ref[GQA * kvh], f32)
    for j in range(1, GQA):
        slope = jnp.where(row >= BLOCK * j, SLOPES[GQA * kvh + j], slope)
        sk = jnp.where(row >= BLOCK * j, sink_ref[GQA * kvh + j], sk)
    sg = jnp.where(valid, sc - slope * distf, NEG)
    m = jnp.maximum(jnp.max(sg, axis=-1, keepdims=True), sk)
    e = jnp.exp(sg - m)
    es = jnp.exp(sk - m)
    inv = 1.0 / (jnp.sum(e, axis=-1, keepdims=True) + es)
    return e * inv, es * inv


def _stack_heads(ref, kvh):
    return jnp.concatenate([ref[:, HEAD_DIM * (GQA * kvh + g):HEAD_DIM * (GQA * kvh + g + 1)] for g in range(GQA)],
                           axis=0)


def attn_fwd(qn, kv, sink, *, name, exchange=None):
    s = qn.shape[0]
    nb = s // BLOCK

    def body(q_ref, kp_ref, kc_ref, kn_ref, sink_ref, o_ref):
        i = pl.program_id(0)
        distf, valid = _attn_mask(i, nb)
        kv3 = jnp.concatenate([kp_ref[...], kc_ref[...], kn_ref[...]], axis=0)
        for kvh in range(KV_HEADS):
            kn = kv3[:, HEAD_DIM * kvh:HEAD_DIM * (kvh + 1)]
            vh = kv3[:, KV_WIDTH + HEAD_DIM * kvh:KV_WIDTH + HEAD_DIM * (kvh + 1)]
            sc = _dot(_stack_heads(q_ref, kvh), kn, NT)
            p, _ = _attn_probs(sc, kvh, distf, valid, sink_ref)
            o = _dot(p.astype(MX), vh)
            for g in range(GQA):
                h = GQA * kvh + g
                o_ref[:, HEAD_DIM * h:HEAD_DIM * (h + 1)] = o[BLOCK * g:BLOCK * (g + 1)].astype(o_ref.dtype)

    (out,), got = _call(body, grid=(nb,), in_specs=_attn_specs(nb),
                        out_specs=[pl.BlockSpec((BLOCK, ATT_WIDTH), lambda i: (i, 0))],
                        out_shape=[SDS((s, ATT_WIDTH), MX)], args=(qn, kv, kv, kv, sink), sem=("parallel",),
                        name=name, exchange=exchange)
    return out, got


def attn_bwd(qn, kv, gmix, sink, *, name):
    s = qn.shape[0]
    nb = s // BLOCK

    def body(q_ref, kp_ref, kc_ref, kn_ref, sink_ref, go_ref, gq_ref, dkv_ref, gs_ref):
        i = pl.program_id(0)

        @pl.when(i == 0)
        def _():
            gs_ref[...] = jnp.zeros_like(gs_ref)

        distf, valid = _attn_mask(i, nb)
        kv3 = jnp.concatenate([kp_ref[...], kc_ref[...], kn_ref[...]], axis=0)
        for kvh in range(KV_HEADS):
            kn = kv3[:, HEAD_DIM * kvh:HEAD_DIM * (kvh + 1)]
            vh = kv3[:, KV_WIDTH + HEAD_DIM * kvh:KV_WIDTH + HEAD_DIM * (kvh + 1)]
            qs = _stack_heads(q_ref, kvh)
            dos = _stack_heads(go_ref, kvh).astype(MX)
            p, psink = _attn_probs(_dot(qs, kn, NT), kvh, distf, valid, sink_ref)
            dp = _dot(dos, vh, NT)
            delta = jnp.sum(p * dp, axis=-1, keepdims=True)
            gsk = psink * delta
            for g in range(GQA):
                h = GQA * kvh + g
                gs_ref[h:h + 1, :] -= jnp.broadcast_to(
                    jnp.sum(gsk[BLOCK * g:BLOCK * (g + 1)], axis=0, keepdims=True), (1, 128))
            ds = (p * (dp - delta)).astype(MX)
            gv = _dot(p.astype(MX), dos, TN)
            gkn = _dot(ds, qs, TN)
            gqs = _dot(ds, kn)
            for g in range(GQA):
                h = GQA * kvh + g
                gq_ref[:, HEAD_DIM * h:HEAD_DIM * (h + 1)] = gqs[BLOCK * g:BLOCK * (g + 1)]
            for b in range(3):
                dkv_ref[b, :, HEAD_DIM * kvh:HEAD_DIM * (kvh + 1)] = gkn[BLOCK * b:BLOCK * (b + 1)]
                dkv_ref[b, :, KV_WIDTH + HEAD_DIM * kvh:KV_WIDTH + HEAD_DIM * (kvh + 1)] = gv[BLOCK * b:BLOCK * (b + 1)]

    return pl.pallas_call(
        body, grid=(nb,),
        in_specs=_attn_specs(nb) + [pl.BlockSpec((BLOCK, ATT_WIDTH), lambda i: (i, 0))],
        out_specs=[pl.BlockSpec((BLOCK, ATT_WIDTH), lambda i: (i, 0)),
                   pl.BlockSpec((3, BLOCK, 2 * KV_WIDTH), lambda i: (0, i, 0)),
                   pl.BlockSpec((ATT_HEADS, 128), lambda i: (0, 0))],
        out_shape=[SDS((s, ATT_WIDTH), f32), SDS((3, s, 2 * KV_WIDTH), f32), SDS((ATT_HEADS, 128), f32)],
        compiler_params=_cp("arbitrary"), name=name)(qn, kv, kv, kv, sink, gmix)


def gz_assemble(gqs, dkv, z, q_gain, k_gain, eq, ek, gu_f, gu_r, gy, d_skip, *, name):
    s = z.shape[0]
    nb = s // BLOCK

    def norm_bwd(t_in, g_out, gain_ref, e_ref):
        r = lax.rsqrt(_head_mean(t_in * t_in, e_ref) + EPS)
        hat = t_in * r
        t = g_out * gain_ref[...]
        return r * (t - hat * _head_mean(t * hat, e_ref)), g_out * hat

    def body(gq_ref, d0_ref, d1_ref, d2_ref, z_ref, qg_ref, kg_ref, eq_ref, ek_ref, guf_ref, gur_ref, gy_ref, ds_ref,
             gz_ref, gqg_ref, gkg_ref, gd_ref):
        i = pl.program_id(0)

        @pl.when(i == 0)
        def _():
            gqg_ref[...] = jnp.zeros_like(gqg_ref)
            gkg_ref[...] = jnp.zeros_like(gkg_ref)
            gd_ref[...] = jnp.zeros_like(gd_ref)

        gq, gg = norm_bwd(z_ref[:, 0:ATT_WIDTH], gq_ref[...] * 0.125, qg_ref, eq_ref)
        gz_ref[:, 0:ATT_WIDTH] = gq.astype(MX)
        gqg_ref[...] += _rows8(gg)
        gkv = d1_ref[0] + jnp.where(i + 1 < nb, d0_ref[0], 0.0) + jnp.where(i >= 1, d2_ref[0], 0.0)
        gk, gg = norm_bwd(z_ref[:, ATT_WIDTH:ATT_WIDTH + KV_WIDTH], gkv[:, 0:KV_WIDTH], kg_ref, ek_ref)
        gz_ref[:, ATT_WIDTH:ATT_WIDTH + KV_WIDTH] = gk.astype(MX)
        gkg_ref[...] += _rows8(gg)
        gz_ref[:, ATT_WIDTH + KV_WIDTH:U_OFF] = gkv[:, KV_WIDTH:].astype(MX)
        gyv = gy_ref[...]
        gz_ref[:, U_OFF:IN_WIDTH] = (guf_ref[...] + gur_ref[...] + ds_ref[...] * gyv).astype(MX)
        gd_ref[...] += _rows8(gyv * z_ref[:, U_OFF:IN_WIDTH])

    row = lambda w: pl.BlockSpec((BLOCK, w), lambda i: (i, 0))
    const = lambda a: pl.BlockSpec(a.shape, lambda i: (0, 0))
    qg = jnp.tile(q_gain.reshape(1, HEAD_DIM), (1, ATT_HEADS))
    kg = jnp.tile(k_gain.reshape(1, HEAD_DIM), (1, KV_HEADS))
    return pl.pallas_call(
        body, grid=(nb,),
        in_specs=[row(ATT_WIDTH),
                  pl.BlockSpec((1, BLOCK, 2 * KV_WIDTH), lambda i: (0, jnp.minimum(i + 1, nb - 1), 0)),
                  pl.BlockSpec((1, BLOCK, 2 * KV_WIDTH), lambda i: (1, i, 0)),
                  pl.BlockSpec((1, BLOCK, 2 * KV_WIDTH), lambda i: (2, jnp.maximum(i - 1, 0), 0)),
                  row(IN_WIDTH), const(qg), const(kg), const(eq), const(ek),
                  row(SSM_WIDTH), row(SSM_WIDTH), row(SSM_WIDTH), pl.BlockSpec((1, SSM_WIDTH), lambda i: (0, 0))],
        out_specs=[row(IN_WIDTH), pl.BlockSpec((8, ATT_WIDTH), lambda i: (0, 0)),
                   pl.BlockSpec((8, KV_WIDTH), lambda i: (0, 0)), pl.BlockSpec((8, SSM_WIDTH), lambda i: (0, 0))],
        out_shape=[SDS((s, IN_WIDTH), MX), SDS((8, ATT_WIDTH), f32), SDS((8, KV_WIDTH), f32),
                   SDS((8, SSM_WIDTH), f32)],
        compiler_params=_cp("arbitrary"), name=name)(
            gqs, dkv, dkv, dkv, z, qg, kg, eq, ek, gu_f, gu_r, gy, d_skip.reshape(1, SSM_WIDTH))


def _cmul(ar, ai, xr, xi):
    return ar * xr - ai * xi, ar * xi + ai * xr


def _permute_rows(src_ref, dst_ref, nv):
    for v in range(nv):
        dst_ref[8 * v:8 * v + 8, :] = src_ref[pl.ds(v, 8, stride=nv), :]


def _unpermute_rows(val, dst_ref, nv):
    for v in range(nv):
        dst_ref[pl.ds(v, 8, stride=nv), :] = val[8 * v:8 * v + 8, :]


def _scan_chunk(x_ref, tab_ref, carry_ref, nv, rev, acc=None):
    L = TILE_ST
    order = list(range(nv - 1, -1, -1)) if rev else list(range(nv))
    a_r, a_i = tab_ref[32:40, :L], tab_ref[32:40, L:]
    pr = pi = None
    for v in order:
        rows = slice(8 * v, 8 * v + 8)
        xr, xi = x_ref[rows, :L], x_ref[rows, L:]
        if pr is not None:
            mr, mi = _cmul(a_r, a_i, pr, pi)
            xr, xi = xr + mr, xi + mi
            x_ref[rows, :L] = xr
            x_ref[rows, L:] = xi
        pr, pi = xr, xi
    er, ei = pr, pi
    row = lax.broadcasted_iota(jnp.int32, (8, L), 0)
    edge = row == (7 if rev else 0)
    sh = 7 if rev else 1
    fr = jnp.where(edge, carry_ref[:, :L], pltpu.roll(er, sh, 0))
    fi = jnp.where(edge, carry_ref[:, L:], pltpu.roll(ei, sh, 0))
    for n, k in enumerate((1, 2, 4)):
        mr, mi = tab_ref[8 * n:8 * n + 8, :L], tab_ref[8 * n:8 * n + 8, L:]
        sh = (8 - k) if rev else k
        rr, ri = pltpu.roll(fr, sh, 0), pltpu.roll(fi, sh, 0)
        fr, fi = fr + mr * rr - mi * ri, fi + mr * ri + mi * rr
    dr, di = _cmul(tab_ref[24:32, :L], tab_ref[24:32, L:], fr, fi)
    last = 0 if rev else 7
    carry_ref[:, :L] = jnp.broadcast_to((dr + er)[last:last + 1, :], (8, L))
    carry_ref[:, L:] = jnp.broadcast_to((di + ei)[last:last + 1, :], (8, L))
    qr, qi = fr, fi
    if acc is not None:
        sr, si = jnp.zeros((8, L), f32), jnp.zeros((8, L), f32)
    for v in order:
        rows = slice(8 * v, 8 * v + 8)
        trow = slice(40 + v, 41 + v)
        mr, mi = _cmul(tab_ref[trow, :L], tab_ref[trow, L:], fr, fi)
        xr, xi = x_ref[rows, :L] + mr, x_ref[rows, L:] + mi
        x_ref[rows, :L] = xr
        x_ref[rows, L:] = xi
        if acc is not None:
            gr, gi = acc[0][rows, :L], acc[0][rows, L:]
            sr, si = sr + gr * qr + gi * qi, si + gi * qr - gr * qi
            qr, qi = xr, xi
    if acc is not None:
        acc[1][:, :L] += sr
        acc[1][:, L:] += si


def ssm_fwd(z, tabs, bmat, cmat, *, rev, name, chunk, exchange=None):
    var = 2 if rev else 0
    s = z.shape[0]
    nc = s // chunk
    nv = chunk // 8
    ci = (lambda i: nc - 1 - i) if rev else (lambda i: i)

    tp = TILES_PER_STEP

    def body(*refs):
        u_refs = refs[:tp]
        tab_ref, b_ref, c_ref, y_ref, xb_ref, u_scr, x_scr, carry = refs[tp:]

        @pl.when(pl.program_id(1) == 0)
        def _():
            carry[...] = jnp.zeros_like(carry)

        for t in range(tp):
            xb_ref[0, :, 2 * TILE_ST * t:2 * TILE_ST * (t + 1)] = carry[t]
            _permute_rows(u_refs[t], u_scr.at[t], nv)
            x_scr[t] = _dot(u_scr[t].astype(MX), b_ref[t])
        for t in range(tp):
            _scan_chunk(x_scr.at[t], tab_ref.at[0, t], carry.at[t], nv, rev)
        for t in range(tp):
            _unpermute_rows(_dot(x_scr[t].astype(MX), c_ref[t]), u_scr.at[t], nv)
            y_ref[:, TILE_CH * t:TILE_CH * (t + 1)] = u_scr[t]

    u_specs = [pl.BlockSpec((chunk, TILE_CH), lambda j, i, t=t: (ci(i), U_OFF // TILE_CH + tp * j + t))
               for t in range(tp)]
    (y, xb), got = _call(
        body, grid=(SSM_TILES // tp, nc),
        in_specs=u_specs + [pl.BlockSpec((1, tp, 40 + nv, 2 * TILE_ST), lambda j, i: (var, j, 0, 0)),
                            pl.BlockSpec((tp, TILE_CH, 2 * TILE_ST), lambda j, i: (j, 0, 0)),
                            pl.BlockSpec((tp, 2 * TILE_ST, TILE_CH), lambda j, i: (j, 0, 0))],
        out_specs=[pl.BlockSpec((chunk, tp * TILE_CH), lambda j, i: (ci(i), j)),
                   pl.BlockSpec((1, 8, tp * 2 * TILE_ST), lambda j, i: (ci(i), 0, j))],
        out_shape=[SDS((s, SSM_WIDTH), f32), SDS((nc, 8, SSM_TILES * 2 * TILE_ST), f32)],
        scratch=[pltpu.VMEM((tp, chunk, TILE_CH), f32), pltpu.VMEM((tp, chunk, 2 * TILE_ST), f32),
                 pltpu.VMEM((tp, 8, 2 * TILE_ST), f32)],
        args=(*([z] * tp), tabs, bmat, cmat), sem=("parallel", "arbitrary"), name=name, exchange=exchange)
    return y, xb, got


def ssm_bwd(z, gy, xb, tabs, bmat, cmat, *, rev, name, chunk, exchange=None):
    var = 2 if rev else 0
    s = z.shape[0]
    nc = s // chunk
    nv = chunk // 8
    ci = (lambda i: i) if rev else (lambda i: nc - 1 - i)

    tp = TILES_PER_STEP
    w2 = 2 * TILE_ST

    def body(*refs):
        u_refs, gy_refs = refs[:tp], refs[tp:2 * tp]
        (xb_ref, ts_ref, ta_ref, b_ref, c_ref, gu_ref, ga_ref, gb_ref, gc_ref,
         u_scr, gy_scr, x_scr, g_scr, gcarry, xcarry) = refs[2 * tp:]

        @pl.when(pl.program_id(1) == 0)
        def _():
            gcarry[...] = jnp.zeros_like(gcarry)
            ga_ref[...] = jnp.zeros_like(ga_ref)
            gb_ref[...] = jnp.zeros_like(gb_ref)
            gc_ref[...] = jnp.zeros_like(gc_ref)

        ub, gyb = [], []
        for t in range(tp):
            _permute_rows(u_refs[t], u_scr.at[t], nv)
            _permute_rows(gy_refs[t], gy_scr.at[t], nv)
            ub.append(u_scr[t].astype(MX))
            gyb.append(gy_scr[t].astype(MX))
        for t in range(tp):
            g_scr[t] = _dot(gyb[t], c_ref[t], NT)
            x_scr[t] = _dot(ub[t], b_ref[t])
            xcarry[t] = xb_ref[0, :, w2 * t:w2 * (t + 1)]
        for t in range(tp):
            _scan_chunk(g_scr.at[t], ta_ref.at[0, t], gcarry.at[t], nv, not rev)
        for t in range(tp):
            _scan_chunk(x_scr.at[t], ts_ref.at[0, t], xcarry.at[t], nv, rev,
                        acc=(g_scr.at[t], ga_ref.at[:, pl.ds(w2 * t, w2)]))
            gb16 = g_scr[t].astype(MX)
            gb_ref[t] += _dot(ub[t], gb16, TN)
            gc_ref[t] += _dot(x_scr[t].astype(MX), gyb[t], TN)
            _unpermute_rows(_dot(gb16, b_ref[t], NT), u_scr.at[t], nv)
            gu_ref[:, TILE_CH * t:TILE_CH * (t + 1)] = u_scr[t]

    tile3 = lambda a, b: pl.BlockSpec((tp, a, b), lambda j, i: (j, 0, 0))
    u_specs = [pl.BlockSpec((chunk, TILE_CH), lambda j, i, t=t: (ci(i), U_OFF // TILE_CH + tp * j + t))
               for t in range(tp)]
    gy_specs = [pl.BlockSpec((chunk, TILE_CH), lambda j, i, t=t: (ci(i), tp * j + t)) for t in range(tp)]
    outs, got = _call(
        body, grid=(SSM_TILES // tp, nc),
        in_specs=u_specs + gy_specs + [
                  pl.BlockSpec((1, 8, tp * w2), lambda j, i: (ci(i), 0, j)),
                  pl.BlockSpec((1, tp, 40 + nv, w2), lambda j, i: (var, j, 0, 0)),
                  pl.BlockSpec((1, tp, 40 + nv, w2), lambda j, i: (var + 1, j, 0, 0)),
                  tile3(TILE_CH, w2), tile3(w2, TILE_CH)],
        out_specs=[pl.BlockSpec((chunk, tp * TILE_CH), lambda j, i: (ci(i), j)),
                   pl.BlockSpec((8, tp * w2), lambda j, i: (0, j)),
                   tile3(TILE_CH, w2), tile3(w2, TILE_CH)],
        out_shape=[SDS((s, SSM_WIDTH), f32), SDS((8, SSM_TILES * w2), f32),
                   SDS((SSM_TILES, TILE_CH, w2), f32), SDS((SSM_TILES, w2, TILE_CH), f32)],
        scratch=[pltpu.VMEM((tp, chunk, TILE_CH), f32), pltpu.VMEM((tp, chunk, TILE_CH), f32),
                 pltpu.VMEM((tp, chunk, w2), f32), pltpu.VMEM((tp, chunk, w2), f32),
                 pltpu.VMEM((tp, 8, w2), f32), pltpu.VMEM((tp, 8, w2), f32)],
        args=(*([z] * tp), *([gy] * tp), xb, tabs, tabs, bmat, cmat), sem=("parallel", "arbitrary"),
        name=name, exchange=exchange)
    return (*outs, got)


GELU_K = math.sqrt(2.0 / math.pi)


def _gelu(y):
    return 0.5 * y * (1.0 + jnp.tanh(GELU_K * (y + 0.044715 * (y * y * y))))


def _gelu_grad(y):
    t = jnp.tanh(GELU_K * (y + 0.044715 * (y * y * y)))
    return 0.5 * (1.0 + t) + 0.5 * y * (1.0 - t * t) * (GELU_K * (1.0 + 3.0 * 0.044715 * (y * y)))


def glu_fwd(y_f, y_r, z, att, d_skip, w_glu, l, *, name, tm):
    s = z.shape[0]
    nblk, cb = w_glu.shape[1], w_glu.shape[3]

    def body(yf_ref, yr_ref, ua_ref, ub_ref, att_ref, d_ref, w_ref, y_ref, gg_ref, mix_ref):
        u = jnp.concatenate([ua_ref[...], ub_ref[...]], axis=1)
        y = d_ref[...] * u + yf_ref[...] + yr_ref[...]
        y_ref[...] = y
        yg = _gelu(y).astype(MX)
        for b in range(nblk):
            gg_ref[:, cb * b:cb * (b + 1)] = _dot(yg, w_ref[0, b])
        mix_ref[:, 0:ATT_WIDTH] = att_ref[...]
        mix_ref[:, ATT_WIDTH:] = (gg_ref[:, :SSM_WIDTH] * jax.nn.sigmoid(gg_ref[:, SSM_WIDTH:])).astype(MX)

    return pl.pallas_call(
        body, grid=(s // tm,),
        in_specs=[_row_spec(tm, SSM_WIDTH), _row_spec(tm, SSM_WIDTH),
                  pl.BlockSpec((tm, SSM_WIDTH // 2), lambda i: (i, U_OFF // (SSM_WIDTH // 2))),
                  pl.BlockSpec((tm, SSM_WIDTH // 2), lambda i: (i, U_OFF // (SSM_WIDTH // 2) + 1)),
                  _row_spec(tm, ATT_WIDTH), pl.BlockSpec((1, SSM_WIDTH), lambda i: (0, 0)), _layer_spec(w_glu, l)],
        out_specs=[_row_spec(tm, SSM_WIDTH), _row_spec(tm, 2 * SSM_WIDTH), _row_spec(tm, D_MODEL)],
        out_shape=[SDS((s, SSM_WIDTH), f32), SDS((s, 2 * SSM_WIDTH), f32), SDS((s, D_MODEL), MX)],
        compiler_params=_cp("parallel"), name=name)(y_f, y_r, z, z, att, d_skip.reshape(1, SSM_WIDTH), w_glu)


def glu_bwd(gmix, gg, ypre, w_glu, l, *, name, tm):
    s = gg.shape[0]
    nblk, cb = w_glu.shape[1], w_glu.shape[3]

    def body(gm_ref, gg_ref, y_ref, w_ref, ggg_ref, yg_ref, gy_ref):
        gs = gm_ref[...]
        val, gate = gg_ref[:, :SSM_WIDTH], gg_ref[:, SSM_WIDTH:]
        sg = jax.nn.sigmoid(gate)
        ggg_ref[:, :SSM_WIDTH] = (gs * sg).astype(MX)
        ggg_ref[:, SSM_WIDTH:] = (gs * val * sg * (1.0 - sg)).astype(MX)
        y = y_ref[...]
        yg_ref[...] = _gelu(y).astype(MX)
        gyg = _dot(ggg_ref[:, 0:cb], w_ref[0, 0], NT)
        for b in range(1, nblk):
            gyg = gyg + _dot(ggg_ref[:, cb * b:cb * (b + 1)], w_ref[0, b], NT)
        gy_ref[...] = gyg * _gelu_grad(y)

    return pl.pallas_call(
        body, grid=(s // tm,),
        in_specs=[pl.BlockSpec((tm, SSM_WIDTH), lambda i: (i, 1)), _row_spec(tm, 2 * SSM_WIDTH),
                  _row_spec(tm, SSM_WIDTH), _layer_spec(w_glu, l)],
        out_specs=[_row_spec(tm, 2 * SSM_WIDTH), _row_spec(tm, SSM_WIDTH), _row_spec(tm, SSM_WIDTH)],
        out_shape=[SDS((s, 2 * SSM_WIDTH), MX), SDS((s, SSM_WIDTH), MX), SDS((s, SSM_WIDTH), f32)],
        compiler_params=_cp("parallel"), name=name)(gmix, gg, ypre, w_glu)


def loss_grad(y, target, *, name, tm):
    s, d = y.shape

    def body(y_ref, t_ref, g_ref, g16_ref, l_ref):
        @pl.when(pl.program_id(0) == 0)
        def _():
            l_ref[...] = jnp.zeros_like(l_ref)

        e = y_ref[...] - t_ref[...]
        g = e * (1.0 / d)
        g_ref[...] = g
        g16_ref[...] = g.astype(MX)
        l_ref[...] += _rows8(e * e)

    row = pl.BlockSpec((tm, d), lambda i: (i, 0))
    return pl.pallas_call(
        body, grid=(s // tm,), in_specs=[row, row],
        out_specs=[row, row, pl.BlockSpec((8, d), lambda i: (0, 0))],
        out_shape=[SDS((s, d), f32), SDS((s, d), MX), SDS((8, d), f32)],
        compiler_params=_cp("arbitrary"), name=name)(y, target)


def _row_tile(rows, cols):
    tr = rows
    while tr * cols > 256 * 1024 and tr % 16 == 0:
        tr //= 2
    return tr


def _elementwise(fn, ins, n_out, *, name, out_dtype=f32):
    shape = ins[0].shape
    cols = shape[-1]
    ins2 = [a.reshape(-1, cols) for a in ins]
    rows = ins2[0].shape[0]
    tr = _row_tile(rows, cols)

    def body(*refs):
        outs = fn(*[r[...] for r in refs[:len(ins)]])
        for o_ref, o in zip(refs[len(ins):], outs):
            o_ref[...] = o.astype(out_dtype)

    spec = pl.BlockSpec((tr, cols), lambda i: (i, 0))
    outs = pl.pallas_call(
        body, grid=(rows // tr,), in_specs=[spec] * len(ins), out_specs=[spec] * n_out,
        out_shape=[SDS((rows, cols), out_dtype)] * n_out, compiler_params=_cp("parallel"), name=name)(*ins2)
    return [o.reshape(shape) for o in outs]


def _adamw_math(w, g, m, v):
    m = ADAM_B1 * m + (1.0 - ADAM_B1) * g
    v = ADAM_B2 * v + (1.0 - ADAM_B2) * (g * g)
    m_hat = m / (1.0 - ADAM_B1 ** ADAM_STEP)
    v_hat = v / (1.0 - ADAM_B2 ** ADAM_STEP)
    delta = -ADAM_LR * (m_hat / (jnp.sqrt(v_hat) + ADAM_EPS) + ADAM_WD * w)
    return delta, m, v


def adamw(w, g, m, v, *, name):
    return _elementwise(_adamw_math, [w, g, m, v], 3, name=name)


SMEM = pl.BlockSpec(memory_space=pltpu.SMEM)


def _core_index():
    return lax.axis_index("c").astype(jnp.int32).reshape(1)


def adamw_halves(w, own, sib, m, v, *, name):
    depth, r, cols = w.shape
    h = r // 2
    tr = _row_tile(h, cols)
    quad = lambda a: a.reshape(depth, 2, h, cols)

    def body(c_ref, w_ref, own_ref, sib_ref, m_ref, v_ref, g_ref, d_ref, mo_ref, vo_ref):
        g = jnp.where(pl.program_id(1) == c_ref[0], own_ref[0], sib_ref[0])
        g_ref[0, 0] = g
        d_ref[0, 0], mo_ref[0, 0], vo_ref[0, 0] = _adamw_math(w_ref[0, 0], g, m_ref[0, 0], v_ref[0, 0])

    full = pl.BlockSpec((1, 1, tr, cols), lambda l, j, i: (l, j, i, 0))
    part = pl.BlockSpec((1, tr, cols), lambda l, j, i: (l, i, 0))
    outs = pl.pallas_call(
        body, grid=(depth, 2, h // tr), in_specs=[SMEM, full, part, part, full, full], out_specs=[full] * 4,
        out_shape=[SDS((depth, 2, h, cols), f32)] * 4,
        compiler_params=_cp("parallel", "parallel", "parallel"), name=name)(
            _core_index(), quad(w), own, sib, quad(m), quad(v))
    return [o.reshape(depth, r, cols) for o in outs]


def add_own_half(g4, recv, *, name):
    _, _, h, cols = g4.shape
    tr = _row_tile(h, cols)

    def body(c_ref, g_ref, r_ref, o_ref):
        own = jnp.where(c_ref[0] == 0, g_ref[0, 0], g_ref[0, 1])
        o_ref[0] = (own + r_ref[0]).astype(WIRE)

    part = pl.BlockSpec((1, tr, cols), lambda s, i: (s, i, 0))
    return pl.pallas_call(
        body, grid=(4, h // tr),
        in_specs=[SMEM, pl.BlockSpec((1, 2, tr, cols), lambda s, i: (s, 0, i, 0)), part], out_specs=part,
        out_shape=SDS((4, h, cols), WIRE), compiler_params=_cp("parallel", "parallel"), name=name)(
            _core_index(), g4, recv)


def _chip_index():
    return (2 * lax.axis_index("x") + lax.axis_index("y")).astype(jnp.int32).reshape(1)


def sum_pieces(sums, got, *, name, into, layer):
    _, h, cols = sums.shape
    tr = _row_tile(h, cols)

    def body(me_ref, s_ref, g_ref, stack_ref, o_ref):
        del stack_ref
        own = s_ref[0]
        for s in range(1, 4):
            own = jnp.where(me_ref[0] == s, s_ref[s], own)
        o_ref[0] = ((own.astype(f32) + g_ref[0].astype(f32)) + g_ref[1].astype(f32)) + g_ref[2].astype(f32)

    return pl.pallas_call(
        body, grid=(h // tr,),
        in_specs=[SMEM, pl.BlockSpec((4, tr, cols), lambda i: (0, i, 0)),
                  pl.BlockSpec((3, tr, cols), lambda i: (0, i, 0)), ANY],
        out_specs=pl.BlockSpec((1, tr, cols), lambda i: (layer, i, 0)),
        out_shape=SDS(into.shape, f32), input_output_aliases={3: 0},
        compiler_params=_cp("parallel"), name=name)(_chip_index(), sums, got, into)


def sum4(a, *, name, into=None, layer=0):
    shape = a.shape[1:]
    cols = shape[-1]
    a2 = a.reshape(4, -1, cols)
    rows = a2.shape[1]
    tr = _row_tile(rows, cols)

    def body(*refs):
        a_ref, o_ref = refs[0], refs[-1]
        tot = ((a_ref[0].astype(f32) + a_ref[1].astype(f32)) + a_ref[2].astype(f32)) + a_ref[3].astype(f32)
        if into is None:
            o_ref[...] = tot
        else:
            o_ref[0] = tot

    in_spec = pl.BlockSpec((4, tr, cols), lambda i: (0, i, 0))
    if into is None:
        out = pl.pallas_call(
            body, grid=(rows // tr,), in_specs=[in_spec], out_specs=pl.BlockSpec((tr, cols), lambda i: (i, 0)),
            out_shape=SDS((rows, cols), f32), compiler_params=_cp("parallel"), name=name)(a2)
        return out.reshape(shape)
    stack = into.reshape(into.shape[0], rows, cols)
    out = pl.pallas_call(
        body, grid=(rows // tr,), in_specs=[in_spec, ANY],
        out_specs=pl.BlockSpec((1, tr, cols), lambda i: (layer, i, 0)),
        out_shape=SDS(stack.shape, f32), input_output_aliases={1: 0},
        compiler_params=_cp("parallel"), name=name)(a2, stack)
    return out.reshape(into.shape)


ANY = pl.BlockSpec(memory_space=pl.ANY)


def _chip_copies(ins, outs, send, recv, bcast):
    x, y, c = lax.axis_index("x"), lax.axis_index("y"), lax.axis_index("c")
    me = 2 * x + y
    copies = []
    for k in range(len(ins)):
        for j, (px, py) in enumerate(((1 - x, y), (x, 1 - y), (1 - x, 1 - y))):
            copies.append(pltpu.make_async_remote_copy(
                src_ref=ins[k] if bcast[k] else ins[k].at[2 * px + py],
                dst_ref=outs[k].at[me] if bcast[k] else outs[k].at[j],
                send_sem=send.at[4 * k + j], recv_sem=recv.at[4 * k + j],
                device_id=(px, py, c), device_id_type=MESH))
        if bcast[k]:
            copies.append(pltpu.make_async_remote_copy(
                src_ref=ins[k], dst_ref=outs[k].at[me], send_sem=send.at[4 * k + 3], recv_sem=recv.at[4 * k + 3],
                device_id=(x, y, 1 - c), device_id_type=MESH))
    return copies


def chip_exchange(arrs, bcast, *, name):
    n = len(arrs)

    def body(*refs):
        copies = _chip_copies(refs[:n], refs[n:2 * n], refs[2 * n], refs[2 * n + 1], bcast)
        for cp in copies:
            cp.start()
        for cp in copies:
            cp.wait()

    return pl.pallas_call(
        body, in_specs=[ANY] * n, out_specs=[ANY] * n,
        out_shape=[SDS((4,) + tuple(a.shape) if b else (3,) + tuple(a.shape[1:]), a.dtype)
                   for a, b in zip(arrs, bcast)],
        scratch_shapes=[pltpu.SemaphoreType.DMA((4 * n,)), pltpu.SemaphoreType.DMA((4 * n,))],
        name=name)(*arrs)


def gather_weights(shards, *, name):
    n = len(shards)
    hd = shards[0].shape[0] // 2

    def body(*refs):
        ins, outs = refs[:n], refs[n:2 * n]
        send, recv = refs[2 * n:]
        x, y, c = lax.axis_index("x"), lax.axis_index("y"), lax.axis_index("c")
        me = 2 * x + y
        chips = ((1 - x, y), (x, 1 - y), (1 - x, 1 - y))
        mine, theirs = pl.ds(c * hd, hd), pl.ds((1 - c) * hd, hd)

        def ici(k, j, src, dst):
            px, py = chips[j]
            return pltpu.make_async_remote_copy(src_ref=src, dst_ref=dst, send_sem=send.at[7 * k + j],
                                                recv_sem=recv.at[7 * k + j], device_id=(px, py, c),
                                                device_id_type=MESH)

        def d2d(k, j, src, dst):
            return pltpu.make_async_remote_copy(src_ref=src, dst_ref=dst, send_sem=send.at[7 * k + 3 + j],
                                                recv_sem=recv.at[7 * k + 3 + j], device_id=(x, y, 1 - c),
                                                device_id_type=MESH)

        own, sent = [], []
        for k in range(n):
            own.append(d2d(k, 3, ins[k], outs[k].at[:, me]))
            own[-1].start()
            for j in range(3):
                sent.append(ici(k, j, ins[k].at[mine], outs[k].at[mine, me]))
                sent[-1].start()
        for k in range(n):
            for j, (px, py) in enumerate(chips):
                landed = outs[k].at[mine, 2 * px + py]
                ici(k, j, landed, landed).wait_recv()
                sent.append(d2d(k, j, landed, landed))
                sent[-1].start()
        for k in range(n):
            for j, (px, py) in enumerate(chips):
                other = outs[k].at[theirs, 2 * px + py]
                d2d(k, j, other, other).wait_recv()
        for cp in sent:
            cp.wait_send()
        for cp in own:
            cp.wait()

    return pl.pallas_call(
        body, in_specs=[ANY] * n, out_specs=[ANY] * n,
        out_shape=[SDS((a.shape[0], 4) + tuple(a.shape[1:]), a.dtype) for a in shards],
        scratch_shapes=[pltpu.SemaphoreType.DMA((7 * n,)), pltpu.SemaphoreType.DMA((7 * n,))],
        name=name)(*shards)


def _sibling_copies(ins, outs, send, recv, half):
    x, y, c = lax.axis_index("x"), lax.axis_index("y"), lax.axis_index("c")
    return [pltpu.make_async_remote_copy(
        src_ref=ins[k].at[:, 1 - c] if half[k] else ins[k], dst_ref=outs[k], send_sem=send.at[k],
        recv_sem=recv.at[k], device_id=(x, y, 1 - c), device_id_type=MESH) for k in range(len(ins))]


def sibling_exchange(arrs, half, *, name):
    n = len(arrs)
    piece = [(a.shape[0],) + a.shape[2:] if h else a.shape for a, h in zip(arrs, half)]

    def body(*refs):
        copies = _sibling_copies(refs[:n], refs[n:2 * n], refs[2 * n], refs[2 * n + 1], half)
        for cp in copies:
            cp.start()
        for cp in copies:
            cp.wait()

    return pl.pallas_call(
        body, in_specs=[ANY] * n, out_specs=[ANY] * n,
        out_shape=[SDS(tuple(p), a.dtype) for p, a in zip(piece, arrs)],
        scratch_shapes=[pltpu.SemaphoreType.DMA((n,)), pltpu.SemaphoreType.DMA((n,))],
        name=name)(*arrs)


def ssm_discretize(lam_re, lam_im, log_dt, b_re, b_im, c_re, c_im):
    dt = jnp.exp(log_dt)[..., None]
    mag = jnp.exp(lam_re * dt)
    abr = mag * jnp.cos(lam_im * dt)
    abi = mag * jnp.sin(lam_im * dt)
    den = lam_re * lam_re + lam_im * lam_im
    zr = ((abr - 1.0) * lam_re + abi * lam_im) / den
    zi = (abi * lam_re - (abr - 1.0) * lam_im) / den
    bbr = zr[..., None] * b_re - zi[..., None] * b_im
    bbi = zr[..., None] * b_im + zi[..., None] * b_re
    eye = jnp.eye(8, dtype=f32)
    bb = jnp.stack([bbr, bbi], axis=1).reshape(2, 2, SSM_TILES, 8, SSM_STATE, SSM_GROUP)
    bmat = jnp.einsum('dqjgph,gk->djghqkp', bb, eye).reshape(2, SSM_TILES, TILE_CH, 2 * TILE_ST)
    cc = jnp.stack([c_re, -c_im], axis=1).reshape(2, 2, SSM_TILES, 8, SSM_GROUP, SSM_STATE)
    cmat = jnp.einsum('dqjghp,gk->djqkpgh', cc, eye).reshape(2, SSM_TILES, 2 * TILE_ST, TILE_CH)
    n = SSM_GROUPS * SSM_STATE
    return abr.reshape(2, n), abi.reshape(2, n), bmat, cmat


SCAN_REV = (False, True, True, False)


def scan_tables_all(ar, ai, *, nv, name):
    a8 = jnp.stack([ar[0], ai[0], ar[0], -ai[0], ar[1], ai[1], ar[1], -ai[1]])
    n_state = SSM_TILES * TILE_ST

    def body(a_ref, o_ref):
        row = lax.broadcasted_iota(jnp.int32, (8, n_state), 0)

        def put(i, rows, re, im):
            for t in range(SSM_TILES):
                o_ref[i, t, rows, 0:TILE_ST] = re[:, TILE_ST * t:TILE_ST * (t + 1)]
                o_ref[i, t, rows, TILE_ST:] = im[:, TILE_ST * t:TILE_ST * (t + 1)]

        for i, rev in enumerate(SCAN_REV):
            a_r, a_i = a_ref[2 * i:2 * i + 1, :], a_ref[2 * i + 1:2 * i + 2, :]
            pr, pi = a_r, a_i
            for v in range(nv):
                dst = 40 + (nv - 1 - v if rev else v)
                put(i, slice(dst, dst + 1), pr, pi)
                if v + 1 < nv:
                    pr, pi = _cmul(a_r, a_i, pr, pi)
            big = (pr, pi)
            put(i, slice(24, 32), jnp.broadcast_to(big[0], (8, n_state)), jnp.broadcast_to(big[1], (8, n_state)))
            put(i, slice(32, 40), jnp.broadcast_to(a_r, (8, n_state)), jnp.broadcast_to(a_i, (8, n_state)))
            for n, k in enumerate((1, 2, 4)):
                cond = (row <= 7 - k) if rev else (row >= k)
                put(i, slice(8 * n, 8 * n + 8), jnp.where(cond, big[0], 0.0), jnp.where(cond, big[1], 0.0))
                big = _cmul(*big, *big)

    return pl.pallas_call(
        body, out_shape=SDS((4, SSM_TILES, 40 + nv, 2 * TILE_ST), f32),
        compiler_params=pltpu.CompilerParams(vmem_limit_bytes=VMEM_LIMIT), name=name)(a8)


def _tile_a(ga):
    t = ga.sum(axis=0).reshape(SSM_TILES, 2, TILE_ST)
    return t[:, 0].reshape(-1), t[:, 1].reshape(-1)


SMALL = ('norm1', 'q_gain', 'k_gain', 'sink', 'lam_re', 'lam_im', 'log_dt', 'b_re', 'b_im', 'c_re', 'c_im',
         'd_skip', 'norm2')
BIG = ('w_in', 'w_glu', 'w_out', 'w_ff1', 'w_ff2')
WEIGHTS = ('norm1', 'w_in', 'q_gain', 'k_gain', 'sink', 'lam_re', 'lam_im', 'log_dt', 'b_re', 'b_im', 'c_re',
           'c_im', 'd_skip', 'w_glu', 'w_out', 'norm2', 'w_ff1', 'w_ff2')


def _chunk(s):
    return min(512, s)


HOSTS_FIRST = {'attn': (("own", 'w_glu'), ("own", 'w_out'), ("next", 'w_in'), ("next", 'w_glu'), ("next", 'w_out')),
               'ssm0': (("own", 'w_ff1'),), 'ssm1': (("own", 'w_ff2'),),
               'ff1': (("next", 'w_ff1'),), 'ff2': (("next", 'w_ff2'),)}
HOSTS_LATER = {'attn': (("next", 'w_in'), ("next", 'w_glu'), ("next", 'w_out')),
               'ssm0': (("next", 'w_ff1'),), 'ssm1': (("next", 'w_ff2'),), 'ff1': (), 'ff2': ()}


def layer_forward(l, x, p, wb, li, own=None, nxt=None):
    s = x.shape[0]
    tm = min(512, s)
    sv = {}
    wb = dict(wb)
    src = {"own": own or {}, "next": nxt or {}}
    plan = {h: [e for e in es if e[1] in src[e[0]]] for h, es in (HOSTS_FIRST if own else HOSTS_LATER).items()}
    fetched = {}

    def hosted(host):
        es = plan[host]
        return ("chips", [src[w][k] for w, k in es], [True] * len(es)) if es else None

    def landed(host, got):
        for (w, k), g4 in zip(plan[host], got):
            if w == "own":
                wb[k] = layout_one(k, g4[None])
            else:
                fetched[k] = g4

    h1, z, _ = norm_mm(x, p['norm1'], wb['w_in'], li, relu2=False, name=f"l{l}_in", tm=tm)
    eq, ek = head_mean_matrix(ATT_WIDTH), head_mean_matrix(KV_WIDTH)
    qn, kv = qk_prep(z, p['q_gain'], p['k_gain'], eq, ek, name=f"l{l}_qk", tm=tm)
    att, got = attn_fwd(qn, kv, p['sink'], name=f"l{l}_attn", exchange=hosted('attn'))
    landed('attn', got)
    sv.update(qn=qn, kv=kv, eq=eq, ek=ek)
    (ar, ai, bmat, cmat), disc_vjp = jax.vjp(
        ssm_discretize, p['lam_re'], p['lam_im'], p['log_dt'], p['b_re'], p['b_im'], p['c_re'], p['c_im'])
    bmat16, cmat16 = bmat.astype(MX), cmat.astype(MX)
    ys, xbs = [], []
    tabs = scan_tables_all(ar, ai, nv=_chunk(s) // 8, name=f"l{l}_tabs")
    for d, rev in enumerate((False, True)):
        y_d, xb_d, got = ssm_fwd(z, tabs, bmat16[d], cmat16[d], rev=rev, name=f"l{l}_ssm{d}", chunk=_chunk(s),
                                 exchange=hosted(f'ssm{d}'))
        landed(f'ssm{d}', got)
        ys.append(y_d)
        xbs.append(xb_d)
    ypre, gg, mix = glu_fwd(ys[0], ys[1], z, att, p['d_skip'], wb['w_glu'], li, name=f"l{l}_glu", tm=min(256, s))
    x1, _ = mm_res(mix, wb['w_out'], li, x, name=f"l{l}_out", tm=tm)
    h2, a2, got = norm_mm(x1, p['norm2'], wb['w_ff1'], li, relu2=True, name=f"l{l}_ff1", tm=tm, exchange=hosted('ff1'))
    landed('ff1', got)
    x2, got = mm_res(a2, wb['w_ff2'], li, x1, name=f"l{l}_ff2", tm=tm, exchange=hosted('ff2'))
    landed('ff2', got)
    sv.update(x=x, h1=h1, z=z, xbs=xbs, tabs=tabs, bmat16=bmat16, cmat16=cmat16, disc_vjp=disc_vjp,
              ypre=ypre, gg=gg, mix=mix, x1=x1, h2=h2, a2=a2, wb=wb)
    return x2, sv, fetched


def layer_backward(l, gx2, gx2h, p, wb, li, sv, pend=None):
    s = gx2.shape[0]
    tm = min(512, s)
    ts = min(1024, s)
    g = {}
    gf, got = mm_nt(gx2h, wb['w_ff2'], li, name=f"l{l}_bff2", tm=tm, a2=sv['a2'],
                    exchange=pend and ("sibling", pend[1], [True] * len(pend[1])))
    sums = pend and [add_own_half(a, b, name=f"l{l}_radd_{k}") for k, a, b in zip(BIG, pend[1], got)]
    g['w_ff2'] = mm_tn(sv['a2'], gx2h, name=f"l{l}_wff2", tk=1024, tn=1024, ts=ts).reshape(4, D_FF // 4, D_MODEL)
    gx1, gx1h, gn2 = mm_nt_norm(gf, wb['w_ff1'], li, sv['x1'], p['norm2'], gx2, name=f"l{l}_bff1", tm=min(256, s))
    g['norm2'] = gn2.sum(axis=0)
    g['w_ff1'] = mm_tn(sv['h2'], gf, name=f"l{l}_wff1", tk=1024, tn=1024, ts=ts, chip_major=True)
    gmix, _ = mm_nt(gx1h, wb['w_out'], li, name=f"l{l}_bout", tm=tm)
    g['w_out'] = mm_tn(sv['mix'], gx1h, name=f"l{l}_wout", tk=1024, tn=1024, ts=ts).reshape(4, D_MODEL // 4, D_MODEL)
    ggg, yg, gy = glu_bwd(gmix, sv['gg'], sv['ypre'], wb['w_glu'], li, name=f"l{l}_bglu", tm=min(256, s))
    g['w_glu'] = mm_tn(yg, ggg, name=f"l{l}_wglu", tk=512, tn=256, ts=ts, chip_major=True)
    gus, gas, gbs, gcs = [], [], [], []
    for d, rev in enumerate((False, True)):
        gu_d, ga_d, gb_d, gc_d, got = ssm_bwd(
            sv['z'], gy, sv['xbs'][d], sv['tabs'], sv['bmat16'][d], sv['cmat16'][d], rev=rev,
            name=f"l{l}_bssm{d}", chunk=_chunk(s),
            exchange=(pend and d == 0) and ("chips", sums, [False] * len(sums)) or None)
        if pend and d == 0:
            pend[0](sums, got)
        gus.append(gu_d)
        gas.append(_tile_a(ga_d))
        gbs.append(gb_d)
        gcs.append(gc_d)
    gar = jnp.stack([gas[0][0], gas[1][0]])
    gai = jnp.stack([gas[0][1], gas[1][1]])
    (g['lam_re'], g['lam_im'], g['log_dt'], g['b_re'], g['b_im'], g['c_re'], g['c_im']) = sv['disc_vjp'](
        (gar, gai, jnp.stack(gbs), jnp.stack(gcs)))
    gqs, dkv, gsk = attn_bwd(sv['qn'], sv['kv'], gmix, p['sink'], name=f"l{l}_battn")
    g['sink'] = gsk[:, 0]
    gz, gqg, gkg, gd = gz_assemble(gqs, dkv, sv['z'], p['q_gain'], p['k_gain'], sv['eq'], sv['ek'], gus[0], gus[1],
                                   gy, p['d_skip'], name=f"l{l}_gz")
    g['q_gain'] = gqg.sum(axis=0).reshape(ATT_HEADS, HEAD_DIM).sum(axis=0)
    g['k_gain'] = gkg.sum(axis=0).reshape(KV_HEADS, HEAD_DIM).sum(axis=0)
    g['d_skip'] = gd.sum(axis=0)
    gx, gxh, gn1 = mm_nt_norm(gz, wb['w_in'], li, sv['x'], p['norm1'], gx1, name=f"l{l}_bin", tm=tm)
    g['norm1'] = gn1.sum(axis=0)
    gw_in = mm_tn(sv['h1'], gz, name=f"l{l}_win", tk=1024, tn=640, ts=ts)
    g['w_in'] = gw_in.reshape(D_MODEL, 4, IN_WIDTH // 4).transpose(1, 0, 2)
    return gx, gxh, g


def layout_one(k, g):
    depth = g.shape[0]
    if k == 'w_in':
        return g.transpose(0, 2, 1, 3).reshape(depth, D_MODEL, IN_WIDTH)
    if k == 'w_out':
        return g.reshape(depth, D_MODEL, D_MODEL)
    if k == 'w_ff2':
        return g.reshape(depth, D_FF, D_MODEL)
    return g


def stack_layouts(gathered):
    return {k: layout_one(k, g) for k, g in gathered.items()}


def local_step(x, target, small, wb):
    depth = wb['w_in'].shape[0]
    saves = []
    for l in range(depth):
        x, sv, _ = layer_forward(l, x, {k: small[k][l] for k in SMALL}, wb, l)
        saves.append(sv)
    gx, gxh, lparts = loss_grad(x, target, name="loss", tm=min(512, x.shape[0]))
    grads = [None] * depth
    for l in reversed(range(depth)):
        gx, gxh, grads[l] = layer_backward(l, gx, gxh, {k: small[k][l] for k in SMALL}, wb, l, saves[l])
    return lparts, gx, grads


def reduce_pieces(g):
    return [g[k].reshape(4, 2, g[k].shape[1] // 2, g[k].shape[2]) for k in BIG]


def reduce_chips(l, sums, got, stacks):
    return {k: sum_pieces(a, b, name=f"l{l}_rsum_{k}", into=stacks[k], layer=l) for k, a, b in zip(BIG, sums, got)}


def gather_first(shards):
    halves = [a.reshape(2, a.shape[0] // 2, a.shape[1]) for a in shards]
    got = gather_weights(halves, name="gather_first")
    return [a.transpose(1, 0, 2, 3).reshape(4, 2 * a.shape[2], a.shape[3]) for a in got]


def reduce_small(packed):
    got = sibling_exchange([packed], [False], name="small_rsib")
    pair = _elementwise(lambda a, b: (a + b,), [packed, got[0]], 1, name="small_radd")[0]
    got = chip_exchange([pair], [True], name="small_rchips")
    return sum4(got[0], name="small_rsum")


def _pack_small(tree):
    parts = []
    for k in SMALL:
        flat = tree[k].reshape(-1)
        parts.append(jnp.pad(flat, (0, (-flat.shape[0]) % 1024)).reshape(-1, 128))
    return jnp.concatenate(parts, axis=0)


def _unpack_small(packed, like):
    out, row = {}, 0
    for k in SMALL:
        n = like[k].size
        rows = -(-n // 1024) * 8
        out[k] = packed[row:row + rows].reshape(-1)[:n].reshape(like[k].shape)
        row += rows
    return out


def kernel(x, norm1, w_in, q_gain, k_gain, sink, lam_re, lam_im, log_dt, b_re, b_im, c_re, c_im, d_skip, w_glu, w_out, norm2, w_ff1, w_ff2, loss_target, m_norm1, m_w_in, m_q_gain, m_k_gain, m_sink, m_lam_re, m_lam_im, m_log_dt, m_b_re, m_b_im, m_c_re, m_c_im, m_d_skip, m_w_glu, m_w_out, m_norm2, m_w_ff1, m_w_ff2, v_norm1, v_w_in, v_q_gain, v_k_gain, v_sink, v_lam_re, v_lam_im, v_log_dt, v_b_re, v_b_im, v_c_re, v_c_im, v_d_skip, v_w_glu, v_w_out, v_norm2, v_w_ff1, v_w_ff2):
    w = dict(norm1=norm1, w_in=w_in, q_gain=q_gain, k_gain=k_gain, sink=sink, lam_re=lam_re, lam_im=lam_im,
             log_dt=log_dt, b_re=b_re, b_im=b_im, c_re=c_re, c_im=c_im, d_skip=d_skip, w_glu=w_glu, w_out=w_out,
             norm2=norm2, w_ff1=w_ff1, w_ff2=w_ff2)
    m = dict(norm1=m_norm1, w_in=m_w_in, q_gain=m_q_gain, k_gain=m_k_gain, sink=m_sink, lam_re=m_lam_re,
             lam_im=m_lam_im, log_dt=m_log_dt, b_re=m_b_re, b_im=m_b_im, c_re=m_c_re, c_im=m_c_im,
             d_skip=m_d_skip, w_glu=m_w_glu, w_out=m_w_out, norm2=m_norm2, w_ff1=m_w_ff1, w_ff2=m_w_ff2)
    v = dict(norm1=v_norm1, w_in=v_w_in, q_gain=v_q_gain, k_gain=v_k_gain, sink=v_sink, lam_re=v_lam_re,
             lam_im=v_lam_im, log_dt=v_log_dt, b_re=v_b_re, b_im=v_b_im, c_re=v_c_re, c_im=v_c_im,
             d_skip=v_d_skip, w_glu=v_w_glu, w_out=v_w_out, norm2=v_norm2, w_ff1=v_w_ff1, w_ff2=v_w_ff2)
    depth = w_in.shape[0]

    shards = {k: w[k].astype(WIRE) for k in BIG}
    small = {k: w[k] for k in SMALL}
    stacks = [{k: jnp.zeros((depth, w[k].shape[1] // 2, w[k].shape[2]), f32) for k in BIG}]

    xs = x[0]
    gathered = {'w_in': gather_first([shards['w_in'][0]])[0]}
    saves = []
    for l in range(depth):
        wb = stack_layouts({k: g[None] for k, g in gathered.items()})
        own = {k: shards[k][l] for k in BIG if k not in gathered}
        nxt = {k: shards[k][l + 1] for k in BIG} if l + 1 < depth else None
        xs, sv, gathered = layer_forward(l, xs, {k: small[k][l] for k in SMALL}, wb, 0, own, nxt)
        saves.append(sv)
    gx, gxh, lparts = loss_grad(xs, loss_target[0], name="loss", tm=min(512, xs.shape[0]))
    loss = lax.psum(0.5 * jnp.sum(lparts) / D_MODEL, ("x", "y", "c"))

    def finisher(l):
        def finish(sums, got):
            stacks[0] = reduce_chips(l, sums, got, stacks[0])
        return finish

    grads, pend = [None] * depth, None
    for l in reversed(range(depth)):
        gx, gxh, g = layer_backward(l, gx, gxh, {k: small[k][l] for k in SMALL}, saves[l]['wb'], 0, saves[l], pend)
        pend = (finisher(l), reduce_pieces(g))
        grads[l] = {k: g[k] for k in SMALL}
    got = sibling_exchange(pend[1], [True] * len(BIG), name="last_rsib")
    sums = [add_own_half(a, b, name=f"last_radd_{k}") for k, a, b in zip(BIG, pend[1], got)]
    pend[0](sums, chip_exchange(sums, [False] * len(BIG), name="last_rchips"))

    sib = sibling_exchange([stacks[0][k] for k in BIG], [False] * len(BIG), name="reduce_back")
    gsmall = reduce_small(_pack_small({k: jnp.stack([grads[l][k] for l in range(depth)]) for k in SMALL}))
    like = {k: w[k] for k in SMALL}
    gfull = _unpack_small(gsmall, like)

    delta, new_m, new_v = {}, {}, {}
    for k, sib_k in zip(BIG, sib):
        gfull[k], delta[k], new_m[k], new_v[k] = adamw_halves(w[k], stacks[0][k], sib_k, m[k], v[k],
                                                              name=f"adamw_{k}")
    ds, ms, vs = adamw(_pack_small(like), gsmall, _pack_small({k: m[k] for k in SMALL}),
                       _pack_small({k: v[k] for k in SMALL}), name="adamw_small")
    delta.update(_unpack_small(ds, like))
    new_m.update(_unpack_small(ms, like))
    new_v.update(_unpack_small(vs, like))

    return (loss, gx[None], *[gfull[k] for k in WEIGHTS], *[delta[k] for k in WEIGHTS],
            *[new_m[k] for k in WEIGHTS], *[new_v[k] for k in WEIGHTS])
```

```python
import functools
import math

import jax
import jax.numpy as jnp
from jax import lax
from jax.experimental import pallas as pl
from jax.experimental.pallas import tpu as pltpu

f32 = jnp.float32
MX = jnp.bfloat16
WIRE = jnp.bfloat16
SDS = jax.ShapeDtypeStruct

D_MODEL = 1024
DEPTH = 4
ATT_HEADS = 8
KV_HEADS = 2
GQA = ATT_HEADS // KV_HEADS
HEAD_DIM = 64
ATT_WIDTH = ATT_HEADS * HEAD_DIM
KV_WIDTH = KV_HEADS * HEAD_DIM
BLOCK = 128
SSM_WIDTH = 512
SSM_GROUP = 16
SSM_GROUPS = 32
SSM_STATE = 64
SSM_TILES = 4
TILE_CH = SSM_WIDTH // SSM_TILES
TILE_ST = SSM_GROUPS * SSM_STATE // SSM_TILES
TILES_PER_STEP = 2
IN_WIDTH = ATT_WIDTH + 2 * KV_WIDTH + SSM_WIDTH
U_OFF = ATT_WIDTH + 2 * KV_WIDTH
D_FF = 4096
EPS = 1e-6
NEG = float(jnp.finfo(jnp.float32).min)
SLOPES = tuple(2.0 ** (-8.0 * (h + 1) / ATT_HEADS) for h in range(ATT_HEADS))

ADAM_LR, ADAM_B1, ADAM_B2, ADAM_EPS, ADAM_WD, ADAM_STEP = 0.001, 0.9, 0.999, 1e-08, 0.01, 10

VMEM_LIMIT = 48 * 1024 * 1024
MESH = pl.DeviceIdType.MESH

NT = (((1,), (1,)), ((), ()))
TN = (((0,), (0,)), ((), ()))


def _cp(*sem):
    return pltpu.CompilerParams(dimension_semantics=sem, vmem_limit_bytes=VMEM_LIMIT)


def _dot(a, b, dims=None):
    if dims is None:
        return jnp.dot(a, b, preferred_element_type=f32)
    return lax.dot_general(a, b, dims, preferred_element_type=f32)


def _rows8(v):
    return v.reshape(v.shape[0] // 8, 8, v.shape[1]).sum(axis=0)


def _layer_spec(w, l):
    nd = w.ndim
    return pl.BlockSpec((1,) + tuple(w.shape[1:]), lambda i: (l,) + (0,) * (nd - 1))


def _row_spec(tm, width):
    return pl.BlockSpec((tm, width), lambda i: (i, 0))


def _call(body, *, grid, in_specs, out_specs, out_shape, args, sem, name, scratch=(), exchange=None):
    n_in, n_out, n_scr = len(in_specs), len(out_specs), len(scratch)
    if exchange is None:
        res = pl.pallas_call(body, grid=grid, in_specs=in_specs, out_specs=out_specs, out_shape=out_shape,
                             scratch_shapes=list(scratch), compiler_params=_cp(*sem), name=name)(*args)
        return list(res), []
    kind, arrs, flags = exchange
    nx = len(arrs)
    if kind == "chips":
        make, nsem = _chip_copies, 4 * nx
        got = [SDS((4,) + tuple(a.shape) if b else (3,) + tuple(a.shape[1:]), a.dtype) for a, b in zip(arrs, flags)]
    else:
        make, nsem = _sibling_copies, nx
        got = [SDS((a.shape[0],) + tuple(a.shape[2:]) if h else tuple(a.shape), a.dtype)
               for a, h in zip(arrs, flags)]

    def hosted(*refs):
        ins, xin = refs[:n_in], refs[n_in:n_in + nx]
        outs = refs[n_in + nx:n_in + nx + n_out]
        xout = refs[n_in + nx + n_out:n_in + 2 * nx + n_out]
        scr = refs[n_in + 2 * nx + n_out:]
        copies = make(xin, xout, scr[n_scr], scr[n_scr + 1], flags)
        first = functools.reduce(jnp.logical_and, [pl.program_id(d) == 0 for d in range(len(grid))])
        last = functools.reduce(jnp.logical_and, [pl.program_id(d) == grid[d] - 1 for d in range(len(grid))])

        @pl.when(first)
        def _():
            for cp in copies:
                cp.start()

        body(*ins, *outs, *scr[:n_scr])

        @pl.when(last)
        def _():
            for cp in copies:
                cp.wait()

    res = pl.pallas_call(
        hosted, grid=grid, in_specs=list(in_specs) + [ANY] * nx, out_specs=list(out_specs) + [ANY] * nx,
        out_shape=list(out_shape) + got,
        scratch_shapes=list(scratch) + [pltpu.SemaphoreType.DMA((nsem,)), pltpu.SemaphoreType.DMA((nsem,))],
        compiler_params=_cp(*["arbitrary"] * len(grid)), name=name)(*args, *arrs)
    return list(res[:n_out]), list(res[n_out:])


def norm_mm(x, gain, w, l, *, relu2, name, tm, exchange=None):
    s, d = x.shape
    if relu2:
        nblk, cb = w.shape[1], w.shape[3]
        n = nblk * cb
    else:
        n = w.shape[2]

    def body(x_ref, g_ref, w_ref, h_ref, y_ref):
        xf = x_ref[...]
        r = lax.rsqrt(jnp.mean(xf * xf, axis=-1, keepdims=True) + EPS)
        h = (xf * r * g_ref[...]).astype(MX)
        h_ref[...] = h
        if relu2:
            for b in range(nblk):
                f = jnp.maximum(_dot(h, w_ref[0, b]), 0.0)
                y_ref[:, cb * b:cb * (b + 1)] = (f * f).astype(MX)
        else:
            y_ref[...] = _dot(h, w_ref[0])

    (h, y), got = _call(
        body, grid=(s // tm,),
        in_specs=[_row_spec(tm, d), pl.BlockSpec((1, d), lambda i: (0, 0)), _layer_spec(w, l)],
        out_specs=[_row_spec(tm, d), _row_spec(tm, n)],
        out_shape=[SDS((s, d), MX), SDS((s, n), MX if relu2 else f32)],
        args=(x, gain.reshape(1, d), w), sem=("parallel",), name=name, exchange=exchange)
    return h, y, got


def mm_res(a, w, l, res, *, name, tm, exchange=None):
    s, k = a.shape
    n = w.shape[2]

    def body(a_ref, w_ref, r_ref, o_ref):
        o_ref[...] = r_ref[...] + _dot(a_ref[...], w_ref[0])

    (out,), got = _call(
        body, grid=(s // tm,), in_specs=[_row_spec(tm, k), _layer_spec(w, l), _row_spec(tm, n)],
        out_specs=[_row_spec(tm, n)], out_shape=[SDS((s, n), f32)], args=(a, w, res), sem=("parallel",),
        name=name, exchange=exchange)
    return out, got


def mm_nt(gy, w, l, *, name, tm, a2=None, exchange=None):
    s, n = gy.shape
    k = w.shape[1]
    kb = min(k, 1024)

    def body(*refs):
        g_ref, w_ref, o_ref = refs[0], refs[1], refs[-1]
        g = g_ref[...]
        for b in range(k // kb):
            cols = slice(kb * b, kb * (b + 1))
            acc = _dot(g, w_ref[0, cols, :], NT)
            if a2 is not None:
                acc = acc * (2.0 * jnp.sqrt(refs[2][:, cols].astype(f32)))
            o_ref[:, cols] = acc.astype(o_ref.dtype)

    in_specs = [_row_spec(tm, n), _layer_spec(w, l)]
    args = [gy, w]
    if a2 is not None:
        in_specs.append(_row_spec(tm, k))
        args.append(a2)
    (out,), got = _call(body, grid=(s // tm,), in_specs=in_specs, out_specs=[_row_spec(tm, k)],
                        out_shape=[SDS((s, k), f32 if a2 is None else MX)], args=args, sem=("parallel",),
                        name=name, exchange=exchange)
    return out, got


def mm_nt_norm(gy, w, l, x, gain, res, *, name, tm):
    s, n = gy.shape
    d = x.shape[1]

    def body(g_ref, w_ref, x_ref, gn_ref, r_ref, o_ref, o16_ref, gg_ref):
        @pl.when(pl.program_id(0) == 0)
        def _():
            gg_ref[...] = jnp.zeros_like(gg_ref)

        if w.ndim == 3:
            gh = _dot(g_ref[...], w_ref[0], NT)
        else:
            cb = w.shape[3]
            gh = _dot(g_ref[:, 0:cb], w_ref[0, 0], NT)
            for b in range(1, w.shape[1]):
                gh = gh + _dot(g_ref[:, cb * b:cb * (b + 1)], w_ref[0, b], NT)
        xf = x_ref[...]
        r = lax.rsqrt(jnp.mean(xf * xf, axis=-1, keepdims=True) + EPS)
        xh = xf * r
        t = gh * gn_ref[...]
        gx = r_ref[...] + r * (t - xh * jnp.mean(t * xh, axis=-1, keepdims=True))
        o_ref[...] = gx
        o16_ref[...] = gx.astype(MX)
        gg_ref[...] += _rows8(gh * xh)

    return pl.pallas_call(
        body, grid=(s // tm,),
        in_specs=[_row_spec(tm, n), _layer_spec(w, l), _row_spec(tm, d), pl.BlockSpec((1, d), lambda i: (0, 0)),
                  _row_spec(tm, d)],
        out_specs=[_row_spec(tm, d), _row_spec(tm, d), pl.BlockSpec((8, d), lambda i: (0, 0))],
        out_shape=[SDS((s, d), f32), SDS((s, d), MX), SDS((8, d), f32)],
        compiler_params=_cp("arbitrary"), name=name)(gy, w, x, gain.reshape(1, d), res)


def mm_tn(xa, gy, *, name, tk, tn, ts, chip_major=False):
    s, k = xa.shape
    n = gy.shape[1]

    def body(x_ref, g_ref, o_ref):
        @pl.when(pl.program_id(2) == 0)
        def _():
            o_ref[...] = jnp.zeros_like(o_ref)

        acc = _dot(x_ref[...], g_ref[...], TN)
        if chip_major:
            o_ref[0] += acc
        else:
            o_ref[...] += acc

    if chip_major:
        out_spec = pl.BlockSpec((1, tk, tn), lambda a, b, c: (b, a, 0))
        out_shape = SDS((n // tn, k, tn), f32)
    else:
        out_spec = pl.BlockSpec((tk, tn), lambda a, b, c: (a, b))
        out_shape = SDS((k, n), f32)
    return pl.pallas_call(
        body, grid=(k // tk, n // tn, s // ts),
        in_specs=[pl.BlockSpec((ts, tk), lambda a, b, c: (c, a)), pl.BlockSpec((ts, tn), lambda a, b, c: (c, b))],
        out_specs=out_spec, out_shape=out_shape,
        compiler_params=_cp("parallel", "parallel", "arbitrary"), name=name)(xa, gy)


def head_mean_matrix(width):
    return jnp.kron(jnp.eye(width // HEAD_DIM, dtype=f32), jnp.full((HEAD_DIM, HEAD_DIM), 1.0 / HEAD_DIM, f32)).astype(MX)


def _head_mean(t, e_ref):
    hi = t.astype(MX)
    lo = (t - hi.astype(f32)).astype(MX)
    return _dot(hi, e_ref[...]) + _dot(lo, e_ref[...])


def qk_prep(z, q_gain, k_gain, eq, ek, *, name, tm):
    s = z.shape[0]

    def body(z_ref, qg_ref, kg_ref, eq_ref, ek_ref, q_ref, kv_ref):
        q = z_ref[:, 0:ATT_WIDTH]
        r = lax.rsqrt(_head_mean(q * q, eq_ref) + EPS)
        q_ref[...] = ((q * r * qg_ref[...]) * 0.125).astype(MX)
        k = z_ref[:, ATT_WIDTH:ATT_WIDTH + KV_WIDTH]
        r = lax.rsqrt(_head_mean(k * k, ek_ref) + EPS)
        kv_ref[:, 0:KV_WIDTH] = (k * r * kg_ref[...]).astype(MX)
        kv_ref[:, KV_WIDTH:] = z_ref[:, ATT_WIDTH + KV_WIDTH:U_OFF].astype(MX)

    const = lambda a: pl.BlockSpec(a.shape, lambda i: (0, 0))
    qg = jnp.tile(q_gain.reshape(1, HEAD_DIM), (1, ATT_HEADS))
    kg = jnp.tile(k_gain.reshape(1, HEAD_DIM), (1, KV_HEADS))
    return pl.pallas_call(
        body, grid=(s // tm,), in_specs=[_row_spec(tm, IN_WIDTH), const(qg), const(kg), const(eq), const(ek)],
        out_specs=[_row_spec(tm, ATT_WIDTH), _row_spec(tm, 2 * KV_WIDTH)],
        out_shape=[SDS((s, ATT_WIDTH), MX), SDS((s, 2 * KV_WIDTH), MX)],
        compiler_params=_cp("parallel"), name=name)(z, qg, kg, eq, ek)


def _attn_mask(i, nb):
    row = lax.broadcasted_iota(jnp.int32, (GQA * BLOCK, 3 * BLOCK), 0) & (BLOCK - 1)
    col = lax.broadcasted_iota(jnp.int32, (GQA * BLOCK, 3 * BLOCK), 1)
    dist = jnp.abs(row - col + BLOCK)
    valid = (dist <= BLOCK) & ((col >= BLOCK) | (i >= 1)) & ((col < 2 * BLOCK) | (i <= nb - 2))
    return dist.astype(f32), valid


def _attn_specs(nb):
    return [pl.BlockSpec((BLOCK, ATT_WIDTH), lambda i: (i, 0)),
            pl.BlockSpec((BLOCK, 2 * KV_WIDTH), lambda i: (jnp.maximum(i - 1, 0), 0)),
            pl.BlockSpec((BLOCK, 2 * KV_WIDTH), lambda i: (i, 0)),
            pl.BlockSpec((BLOCK, 2 * KV_WIDTH), lambda i: (jnp.minimum(i + 1, nb - 1), 0)),
            pl.BlockSpec(memory_space=pltpu.SMEM)]


def _attn_probs(sc, kvh, distf, valid, sink_ref):
    row = lax.broadcasted_iota(jnp.int32, (GQA * BLOCK, 1), 0)
    slope = jnp.full((GQA * BLOCK, 1), SLOPES[GQA * kvh], f32)
    sk = jnp.full((GQA * BLOCK, 1), sink_ref[GQA * kvh], f32)
    for j in range(1, GQA):
        slope = jnp.where(row >= BLOCK * j, SLOPES[GQA * kvh + j], slope)
        sk = jnp.where(row >= BLOCK * j, sink_ref[GQA * kvh + j], sk)
    sg = jnp.where(valid, sc - slope * distf, NEG)
    m = jnp.maximum(jnp.max(sg, axis=-1, keepdims=True), sk)
    e = jnp.exp(sg - m)
    es = jnp.exp(sk - m)
    inv = 1.0 / (jnp.sum(e, axis=-1, keepdims=True) + es)
    return e * inv, es * inv


def _stack_heads(ref, kvh):
    return jnp.concatenate([ref[:, HEAD_DIM * (GQA * kvh + g):HEAD_DIM * (GQA * kvh + g + 1)] for g in range(GQA)],
                           axis=0)


def attn_fwd(qn, kv, sink, *, name, exchange=None):
    s = qn.shape[0]
    nb = s // BLOCK

    def body(q_ref, kp_ref, kc_ref, kn_ref, sink_ref, o_ref):
        i = pl.program_id(0)
        distf, valid = _attn_mask(i, nb)
        kv3 = jnp.concatenate([kp_ref[...], kc_ref[...], kn_ref[...]], axis=0)
        for kvh in range(KV_HEADS):
            kn = kv3[:, HEAD_DIM * kvh:HEAD_DIM * (kvh + 1)]
            vh = kv3[:, KV_WIDTH + HEAD_DIM * kvh:KV_WIDTH + HEAD_DIM * (kvh + 1)]
            sc = _dot(_stack_heads(q_ref, kvh), kn, NT)
            p, _ = _attn_probs(sc, kvh, distf, valid, sink_ref)
            o = _dot(p.astype(MX), vh)
            for g in range(GQA):
                h = GQA * kvh + g
                o_ref[:, HEAD_DIM * h:HEAD_DIM * (h + 1)] = o[BLOCK * g:BLOCK * (g + 1)].astype(o_ref.dtype)

    (out,), got = _call(body, grid=(nb,), in_specs=_attn_specs(nb),
                        out_specs=[pl.BlockSpec((BLOCK, ATT_WIDTH), lambda i: (i, 0))],
                        out_shape=[SDS((s, ATT_WIDTH), MX)], args=(qn, kv, kv, kv, sink), sem=("parallel",),
                        name=name, exchange=exchange)
    return out, got


def attn_bwd(qn, kv, gmix, sink, *, name):
    s = qn.shape[0]
    nb = s // BLOCK

    def body(q_ref, kp_ref, kc_ref, kn_ref, sink_ref, go_ref, gq_ref, dkv_ref, gs_ref):
        i = pl.program_id(0)

        @pl.when(i == 0)
        def _():
            gs_ref[...] = jnp.zeros_like(gs_ref)

        distf, valid = _attn_mask(i, nb)
        kv3 = jnp.concatenate([kp_ref[...], kc_ref[...], kn_ref[...]], axis=0)
        for kvh in range(KV_HEADS):
            kn = kv3[:, HEAD_DIM * kvh:HEAD_DIM * (kvh + 1)]
            vh = kv3[:, KV_WIDTH + HEAD_DIM * kvh:KV_WIDTH + HEAD_DIM * (kvh + 1)]
            qs = _stack_heads(q_ref, kvh)
            dos = _stack_heads(go_ref, kvh).astype(MX)
            p, psink = _attn_probs(_dot(qs, kn, NT), kvh, distf, valid, sink_ref)
            dp = _dot(dos, vh, NT)
            delta = jnp.sum(p * dp, axis=-1, keepdims=True)
            gsk = psink * delta
            for g in range(GQA):
                h = GQA * kvh + g
                gs_ref[h:h + 1, :] -= jnp.broadcast_to(
                    jnp.sum(gsk[BLOCK * g:BLOCK * (g + 1)], axis=0, keepdims=True), (1, 128))
            ds = (p * (dp - delta)).astype(MX)
            gv = _dot(p.astype(MX), dos, TN)
            gkn = _dot(ds, qs, TN)
            gqs = _dot(ds, kn)
            for g in range(GQA):
                h = GQA * kvh + g
                gq_ref[:, HEAD_DIM * h:HEAD_DIM * (h + 1)] = gqs[BLOCK * g:BLOCK * (g + 1)]
            for b in range(3):
                dkv_ref[b, :, HEAD_DIM * kvh:HEAD_DIM * (kvh + 1)] = gkn[BLOCK * b:BLOCK * (b + 1)]
                dkv_ref[b, :, KV_WIDTH + HEAD_DIM * kvh:KV_WIDTH + HEAD_DIM * (kvh + 1)] = gv[BLOCK * b:BLOCK * (b + 1)]

    return pl.pallas_call(
        body, grid=(nb,),
        in_specs=_attn_specs(nb) + [pl.BlockSpec((BLOCK, ATT_WIDTH), lambda i: (i, 0))],
        out_specs=[pl.BlockSpec((BLOCK, ATT_WIDTH), lambda i: (i, 0)),
                   pl.BlockSpec((3, BLOCK, 2 * KV_WIDTH), lambda i: (0, i, 0)),
                   pl.BlockSpec((ATT_HEADS, 128), lambda i: (0, 0))],
        out_shape=[SDS((s, ATT_WIDTH), f32), SDS((3, s, 2 * KV_WIDTH), f32), SDS((ATT_HEADS, 128), f32)],
        compiler_params=_cp("arbitrary"), name=name)(qn, kv, kv, kv, sink, gmix)


def gz_assemble(gqs, dkv, z, q_gain, k_gain, eq, ek, gu_f, gu_r, gy, d_skip, *, name):
    s = z.shape[0]
    nb = s // BLOCK

    def norm_bwd(t_in, g_out, gain_ref, e_ref):
        r = lax.rsqrt(_head_mean(t_in * t_in, e_ref) + EPS)
        hat = t_in * r
        t = g_out * gain_ref[...]
        return r * (t - hat * _head_mean(t * hat, e_ref)), g_out * hat

    def body(gq_ref, d0_ref, d1_ref, d2_ref, z_ref, qg_ref, kg_ref, eq_ref, ek_ref, guf_ref, gur_ref, gy_ref, ds_ref,
             gz_ref, gqg_ref, gkg_ref, gd_ref):
        i = pl.program_id(0)

        @pl.when(i == 0)
        def _():
            gqg_ref[...] = jnp.zeros_like(gqg_ref)
            gkg_ref[...] = jnp.zeros_like(gkg_ref)
            gd_ref[...] = jnp.zeros_like(gd_ref)

        gq, gg = norm_bwd(z_ref[:, 0:ATT_WIDTH], gq_ref[...] * 0.125, qg_ref, eq_ref)
        gz_ref[:, 0:ATT_WIDTH] = gq.astype(MX)
        gqg_ref[...] += _rows8(gg)
        gkv = d1_ref[0] + jnp.where(i + 1 < nb, d0_ref[0], 0.0) + jnp.where(i >= 1, d2_ref[0], 0.0)
        gk, gg = norm_bwd(z_ref[:, ATT_WIDTH:ATT_WIDTH + KV_WIDTH], gkv[:, 0:KV_WIDTH], kg_ref, ek_ref)
        gz_ref[:, ATT_WIDTH:ATT_WIDTH + KV_WIDTH] = gk.astype(MX)
        gkg_ref[...] += _rows8(gg)
        gz_ref[:, ATT_WIDTH + KV_WIDTH:U_OFF] = gkv[:, KV_WIDTH:].astype(MX)
        gyv = gy_ref[...]
        gz_ref[:, U_OFF:IN_WIDTH] = (guf_ref[...] + gur_ref[...] + ds_ref[...] * gyv).astype(MX)
        gd_ref[...] += _rows8(gyv * z_ref[:, U_OFF:IN_WIDTH])

    row = lambda w: pl.BlockSpec((BLOCK, w), lambda i: (i, 0))
    const = lambda a: pl.BlockSpec(a.shape, lambda i: (0, 0))
    qg = jnp.tile(q_gain.reshape(1, HEAD_DIM), (1, ATT_HEADS))
    kg = jnp.tile(k_gain.reshape(1, HEAD_DIM), (1, KV_HEADS))
    return pl.pallas_call(
        body, grid=(nb,),
        in_specs=[row(ATT_WIDTH),
                  pl.BlockSpec((1, BLOCK, 2 * KV_WIDTH), lambda i: (0, jnp.minimum(i + 1, nb - 1), 0)),
                  pl.BlockSpec((1, BLOCK, 2 * KV_WIDTH), lambda i: (1, i, 0)),
                  pl.BlockSpec((1, BLOCK, 2 * KV_WIDTH), lambda i: (2, jnp.maximum(i - 1, 0), 0)),
                  row(IN_WIDTH), const(qg), const(kg), const(eq), const(ek),
                  row(SSM_WIDTH), row(SSM_WIDTH), row(SSM_WIDTH), pl.BlockSpec((1, SSM_WIDTH), lambda i: (0, 0))],
        out_specs=[row(IN_WIDTH), pl.BlockSpec((8, ATT_WIDTH), lambda i: (0, 0)),
                   pl.BlockSpec((8, KV_WIDTH), lambda i: (0, 0)), pl.BlockSpec((8, SSM_WIDTH), lambda i: (0, 0))],
        out_shape=[SDS((s, IN_WIDTH), MX), SDS((8, ATT_WIDTH), f32), SDS((8, KV_WIDTH), f32),
                   SDS((8, SSM_WIDTH), f32)],
        compiler_params=_cp("arbitrary"), name=name)(
            gqs, dkv, dkv, dkv, z, qg, kg, eq, ek, gu_f, gu_r, gy, d_skip.reshape(1, SSM_WIDTH))


def _cmul(ar, ai, xr, xi):
    return ar * xr - ai * xi, ar * xi + ai * xr


def _permute_rows(src_ref, dst_ref, nv):
    for v in range(nv):
        dst_ref[8 * v:8 * v + 8, :] = src_ref[pl.ds(v, 8, stride=nv), :]


def _unpermute_rows(val, dst_ref, nv):
    for v in range(nv):
        dst_ref[pl.ds(v, 8, stride=nv), :] = val[8 * v:8 * v + 8, :]


def _scan_chunk(x_ref, tab_ref, carry_ref, nv, rev, acc=None):
    L = TILE_ST
    order = list(range(nv - 1, -1, -1)) if rev else list(range(nv))
    a_r, a_i = tab_ref[32:40, :L], tab_ref[32:40, L:]
    pr = pi = None
    for v in order:
        rows = slice(8 * v, 8 * v + 8)
        xr, xi = x_ref[rows, :L], x_ref[rows, L:]
        if pr is not None:
            mr, mi = _cmul(a_r, a_i, pr, pi)
            xr, xi = xr + mr, xi + mi
            x_ref[rows, :L] = xr
            x_ref[rows, L:] = xi
        pr, pi = xr, xi
    er, ei = pr, pi
    row = lax.broadcasted_iota(jnp.int32, (8, L), 0)
    edge = row == (7 if rev else 0)
    sh = 7 if rev else 1
    fr = jnp.where(edge, carry_ref[:, :L], pltpu.roll(er, sh, 0))
    fi = jnp.where(edge, carry_ref[:, L:], pltpu.roll(ei, sh, 0))
    for n, k in enumerate((1, 2, 4)):
        mr, mi = tab_ref[8 * n:8 * n + 8, :L], tab_ref[8 * n:8 * n + 8, L:]
        sh = (8 - k) if rev else k
        rr, ri = pltpu.roll(fr, sh, 0), pltpu.roll(fi, sh, 0)
        fr, fi = fr + mr * rr - mi * ri, fi + mr * ri + mi * rr
    dr, di = _cmul(tab_ref[24:32, :L], tab_ref[24:32, L:], fr, fi)
    last = 0 if rev else 7
    carry_ref[:, :L] = jnp.broadcast_to((dr + er)[last:last + 1, :], (8, L))
    carry_ref[:, L:] = jnp.broadcast_to((di + ei)[last:last + 1, :], (8, L))
    qr, qi = fr, fi
    if acc is not None:
        sr, si = jnp.zeros((8, L), f32), jnp.zeros((8, L), f32)
    for v in order:
        rows = slice(8 * v, 8 * v + 8)
        trow = slice(40 + v, 41 + v)
        mr, mi = _cmul(tab_ref[trow, :L], tab_ref[trow, L:], fr, fi)
        xr, xi = x_ref[rows, :L] + mr, x_ref[rows, L:] + mi
        x_ref[rows, :L] = xr
        x_ref[rows, L:] = xi
        if acc is not None:
            gr, gi = acc[0][rows, :L], acc[0][rows, L:]
            sr, si = sr + gr * qr + gi * qi, si + gi * qr - gr * qi
            qr, qi = xr, xi
    if acc is not None:
        acc[1][:, :L] += sr
        acc[1][:, L:] += si


def ssm_fwd(z, tabs, bmat, cmat, *, rev, name, chunk, exchange=None):
    var = 2 if rev else 0
    s = z.shape[0]
    nc = s // chunk
    nv = chunk // 8
    ci = (lambda i: nc - 1 - i) if rev else (lambda i: i)

    tp = TILES_PER_STEP

    def body(*refs):
        u_refs = refs[:tp]
        tab_ref, b_ref, c_ref, y_ref, xb_ref, u_scr, x_scr, carry = refs[tp:]

        @pl.when(pl.program_id(1) == 0)
        def _():
            carry[...] = jnp.zeros_like(carry)

        for t in range(tp):
            xb_ref[0, :, 2 * TILE_ST * t:2 * TILE_ST * (t + 1)] = carry[t]
            _permute_rows(u_refs[t], u_scr.at[t], nv)
            x_scr[t] = _dot(u_scr[t].astype(MX), b_ref[t])
        for t in range(tp):
            _scan_chunk(x_scr.at[t], tab_ref.at[0, t], carry.at[t], nv, rev)
        for t in range(tp):
            _unpermute_rows(_dot(x_scr[t].astype(MX), c_ref[t]), u_scr.at[t], nv)
            y_ref[:, TILE_CH * t:TILE_CH * (t + 1)] = u_scr[t]

    u_specs = [pl.BlockSpec((chunk, TILE_CH), lambda j, i, t=t: (ci(i), U_OFF // TILE_CH + tp * j + t))
               for t in range(tp)]
    (y, xb), got = _call(
        body, grid=(SSM_TILES // tp, nc),
        in_specs=u_specs + [pl.BlockSpec((1, tp, 40 + nv, 2 * TILE_ST), lambda j, i: (var, j, 0, 0)),
                            pl.BlockSpec((tp, TILE_CH, 2 * TILE_ST), lambda j, i: (j, 0, 0)),
                            pl.BlockSpec((tp, 2 * TILE_ST, TILE_CH), lambda j, i: (j, 0, 0))],
        out_specs=[pl.BlockSpec((chunk, tp * TILE_CH), lambda j, i: (ci(i), j)),
                   pl.BlockSpec((1, 8, tp * 2 * TILE_ST), lambda j, i: (ci(i), 0, j))],
        out_shape=[SDS((s, SSM_WIDTH), f32), SDS((nc, 8, SSM_TILES * 2 * TILE_ST), f32)],
        scratch=[pltpu.VMEM((tp, chunk, TILE_CH), f32), pltpu.VMEM((tp, chunk, 2 * TILE_ST), f32),
                 pltpu.VMEM((tp, 8, 2 * TILE_ST), f32)],
        args=(*([z] * tp), tabs, bmat, cmat), sem=("parallel", "arbitrary"), name=name, exchange=exchange)
    return y, xb, got


def ssm_bwd(z, gy, xb, tabs, bmat, cmat, *, rev, name, chunk, exchange=None):
    var = 2 if rev else 0
    s = z.shape[0]
    nc = s // chunk
    nv = chunk // 8
    ci = (lambda i: i) if rev else (lambda i: nc - 1 - i)

    tp = TILES_PER_STEP
    w2 = 2 * TILE_ST

    def body(*refs):
        u_refs, gy_refs = refs[:tp], refs[tp:2 * tp]
        (xb_ref, ts_ref, ta_ref, b_ref, c_ref, gu_ref, ga_ref, gb_ref, gc_ref,
         u_scr, gy_scr, x_scr, g_scr, gcarry, xcarry) = refs[2 * tp:]

        @pl.when(pl.program_id(1) == 0)
        def _():
            gcarry[...] = jnp.zeros_like(gcarry)
            ga_ref[...] = jnp.zeros_like(ga_ref)
            gb_ref[...] = jnp.zeros_like(gb_ref)
            gc_ref[...] = jnp.zeros_like(gc_ref)

        ub, gyb = [], []
        for t in range(tp):
            _permute_rows(u_refs[t], u_scr.at[t], nv)
            _permute_rows(gy_refs[t], gy_scr.at[t], nv)
            ub.append(u_scr[t].astype(MX))
            gyb.append(gy_scr[t].astype(MX))
        for t in range(tp):
            g_scr[t] = _dot(gyb[t], c_ref[t], NT)
            x_scr[t] = _dot(ub[t], b_ref[t])
            xcarry[t] = xb_ref[0, :, w2 * t:w2 * (t + 1)]
        for t in range(tp):
            _scan_chunk(g_scr.at[t], ta_ref.at[0, t], gcarry.at[t], nv, not rev)
        for t in range(tp):
            _scan_chunk(x_scr.at[t], ts_ref.at[0, t], xcarry.at[t], nv, rev,
                        acc=(g_scr.at[t], ga_ref.at[:, pl.ds(w2 * t, w2)]))
            gb16 = g_scr[t].astype(MX)
            gb_ref[t] += _dot(ub[t], gb16, TN)
            gc_ref[t] += _dot(x_scr[t].astype(MX), gyb[t], TN)
            _unpermute_rows(_dot(gb16, b_ref[t], NT), u_scr.at[t], nv)
            gu_ref[:, TILE_CH * t:TILE_CH * (t + 1)] = u_scr[t]

    tile3 = lambda a, b: pl.BlockSpec((tp, a, b), lambda j, i: (j, 0, 0))
    u_specs = [pl.BlockSpec((chunk, TILE_CH), lambda j, i, t=t: (ci(i), U_OFF // TILE_CH + tp * j + t))
               for t in range(tp)]
    gy_specs = [pl.BlockSpec((chunk, TILE_CH), lambda j, i, t=t: (ci(i), tp * j + t)) for t in range(tp)]
    outs, got = _call(
        body, grid=(SSM_TILES // tp, nc),
        in_specs=u_specs + gy_specs + [
                  pl.BlockSpec((1, 8, tp * w2), lambda j, i: (ci(i), 0, j)),
                  pl.BlockSpec((1, tp, 40 + nv, w2), lambda j, i: (var, j, 0, 0)),
                  pl.BlockSpec((1, tp, 40 + nv, w2), lambda j, i: (var + 1, j, 0, 0)),
                  tile3(TILE_CH, w2), tile3(w2, TILE_CH)],
        out_specs=[pl.BlockSpec((chunk, tp * TILE_CH), lambda j, i: (ci(i), j)),
                   pl.BlockSpec((8, tp * w2), lambda j, i: (0, j)),
                   tile3(TILE_CH, w2), tile3(w2, TILE_CH)],
        out_shape=[SDS((s, SSM_WIDTH), f32), SDS((8, SSM_TILES * w2), f32),
                   SDS((SSM_TILES, TILE_CH, w2), f32), SDS((SSM_TILES, w2, TILE_CH), f32)],
        scratch=[pltpu.VMEM((tp, chunk, TILE_CH), f32), pltpu.VMEM((tp, chunk, TILE_CH), f32),
                 pltpu.VMEM((tp, chunk, w2), f32), pltpu.VMEM((tp, chunk, w2), f32),
                 pltpu.VMEM((tp, 8, w2), f32), pltpu.VMEM((tp, 8, w2), f32)],
        args=(*([z] * tp), *([gy] * tp), xb, tabs, tabs, bmat, cmat), sem=("parallel", "arbitrary"),
        name=name, exchange=exchange)
    return (*outs, got)


GELU_K = math.sqrt(2.0 / math.pi)


def _gelu(y):
    return 0.5 * y * (1.0 + jnp.tanh(GELU_K * (y + 0.044715 * (y * y * y))))


def _gelu_grad(y):
    t = jnp.tanh(GELU_K * (y + 0.044715 * (y * y * y)))
    return 0.5 * (1.0 + t) + 0.5 * y * (1.0 - t * t) * (GELU_K * (1.0 + 3.0 * 0.044715 * (y * y)))


def glu_fwd(y_f, y_r, z, att, d_skip, w_glu, l, *, name, tm):
    s = z.shape[0]
    nblk, cb = w_glu.shape[1], w_glu.shape[3]

    def body(yf_ref, yr_ref, ua_ref, ub_ref, att_ref, d_ref, w_ref, y_ref, gg_ref, mix_ref):
        u = jnp.concatenate([ua_ref[...], ub_ref[...]], axis=1)
        y = d_ref[...] * u + yf_ref[...] + yr_ref[...]
        y_ref[...] = y
        yg = _gelu(y).astype(MX)
        for b in range(nblk):
            gg_ref[:, cb * b:cb * (b + 1)] = _dot(yg, w_ref[0, b])
        mix_ref[:, 0:ATT_WIDTH] = att_ref[...]
        mix_ref[:, ATT_WIDTH:] = (gg_ref[:, :SSM_WIDTH] * jax.nn.sigmoid(gg_ref[:, SSM_WIDTH:])).astype(MX)

    return pl.pallas_call(
        body, grid=(s // tm,),
        in_specs=[_row_spec(tm, SSM_WIDTH), _row_spec(tm, SSM_WIDTH),
                  pl.BlockSpec((tm, SSM_WIDTH // 2), lambda i: (i, U_OFF // (SSM_WIDTH // 2))),
                  pl.BlockSpec((tm, SSM_WIDTH // 2), lambda i: (i, U_OFF // (SSM_WIDTH // 2) + 1)),
                  _row_spec(tm, ATT_WIDTH), pl.BlockSpec((1, SSM_WIDTH), lambda i: (0, 0)), _layer_spec(w_glu, l)],
        out_specs=[_row_spec(tm, SSM_WIDTH), _row_spec(tm, 2 * SSM_WIDTH), _row_spec(tm, D_MODEL)],
        out_shape=[SDS((s, SSM_WIDTH), f32), SDS((s, 2 * SSM_WIDTH), f32), SDS((s, D_MODEL), MX)],
        compiler_params=_cp("parallel"), name=name)(y_f, y_r, z, z, att, d_skip.reshape(1, SSM_WIDTH), w_glu)


def glu_bwd(gmix, gg, ypre, w_glu, l, *, name, tm):
    s = gg.shape[0]
    nblk, cb = w_glu.shape[1], w_glu.shape[3]

    def body(gm_ref, gg_ref, y_ref, w_ref, ggg_ref, yg_ref, gy_ref):
        gs = gm_ref[...]
        val, gate = gg_ref[:, :SSM_WIDTH], gg_ref[:, SSM_WIDTH:]
        sg = jax.nn.sigmoid(gate)
        ggg_ref[:, :SSM_WIDTH] = (gs * sg).astype(MX)
        ggg_ref[:, SSM_WIDTH:] = (gs * val * sg * (1.0 - sg)).astype(MX)
        y = y_ref[...]
        yg_ref[...] = _gelu(y).astype(MX)
        gyg = _dot(ggg_ref[:, 0:cb], w_ref[0, 0], NT)
        for b in range(1, nblk):
            gyg = gyg + _dot(ggg_ref[:, cb * b:cb * (b + 1)], w_ref[0, b], NT)
        gy_ref[...] = gyg * _gelu_grad(y)

    return pl.pallas_call(
        body, grid=(s // tm,),
        in_specs=[pl.BlockSpec((tm, SSM_WIDTH), lambda i: (i, 1)), _row_spec(tm, 2 * SSM_WIDTH),
                  _row_spec(tm, SSM_WIDTH), _layer_spec(w_glu, l)],
        out_specs=[_row_spec(tm, 2 * SSM_WIDTH), _row_spec(tm, SSM_WIDTH), _row_spec(tm, SSM_WIDTH)],
        out_shape=[SDS((s, 2 * SSM_WIDTH), MX), SDS((s, SSM_WIDTH), MX), SDS((s, SSM_WIDTH), f32)],
        compiler_params=_cp("parallel"), name=name)(gmix, gg, ypre, w_glu)


def loss_grad(y, target, *, name, tm):
    s, d = y.shape

    def body(y_ref, t_ref, g_ref, g16_ref, l_ref):
        @pl.when(pl.program_id(0) == 0)
        def _():
            l_ref[...] = jnp.zeros_like(l_ref)

        e = y_ref[...] - t_ref[...]
        g = e * (1.0 / d)
        g_ref[...] = g
        g16_ref[...] = g.astype(MX)
        l_ref[...] += _rows8(e * e)

    row = pl.BlockSpec((tm, d), lambda i: (i, 0))
    return pl.pallas_call(
        body, grid=(s // tm,), in_specs=[row, row],
        out_specs=[row, row, pl.BlockSpec((8, d), lambda i: (0, 0))],
        out_shape=[SDS((s, d), f32), SDS((s, d), MX), SDS((8, d), f32)],
        compiler_params=_cp("arbitrary"), name=name)(y, target)


def _row_tile(rows, cols):
    tr = rows
    while tr * cols > 256 * 1024 and tr % 16 == 0:
        tr //= 2
    return tr


def _elementwise(fn, ins, n_out, *, name, out_dtype=f32):
    shape = ins[0].shape
    cols = shape[-1]
    ins2 = [a.reshape(-1, cols) for a in ins]
    rows = ins2[0].shape[0]
    tr = _row_tile(rows, cols)

    def body(*refs):
        outs = fn(*[r[...] for r in refs[:len(ins)]])
        for o_ref, o in zip(refs[len(ins):], outs):
            o_ref[...] = o.astype(out_dtype)

    spec = pl.BlockSpec((tr, cols), lambda i: (i, 0))
    outs = pl.pallas_call(
        body, grid=(rows // tr,), in_specs=[spec] * len(ins), out_specs=[spec] * n_out,
        out_shape=[SDS((rows, cols), out_dtype)] * n_out, compiler_params=_cp("parallel"), name=name)(*ins2)
    return [o.reshape(shape) for o in outs]


def _adamw_math(w, g, m, v):
    m = ADAM_B1 * m + (1.0 - ADAM_B1) * g
    v = ADAM_B2 * v + (1.0 - ADAM_B2) * (g * g)
    m_hat = m / (1.0 - ADAM_B1 ** ADAM_STEP)
    v_hat = v / (1.0 - ADAM_B2 ** ADAM_STEP)
    delta = -ADAM_LR * (m_hat / (jnp.sqrt(v_hat) + ADAM_EPS) + ADAM_WD * w)
    return delta, m, v


def adamw(w, g, m, v, *, name):
    return _elementwise(_adamw_math, [w, g, m, v], 3, name=name)


SMEM = pl.BlockSpec(memory_space=pltpu.SMEM)


def _core_index():
    return lax.axis_index("c").astype(jnp.int32).reshape(1)


def adamw_halves(w, own, sib, m, v, *, name):
    depth, r, cols = w.shape
    h = r // 2
    tr = _row_tile(h, cols)
    quad = lambda a: a.reshape(depth, 2, h, cols)

    def body(c_ref, w_ref, own_ref, sib_ref, m_ref, v_ref, g_ref, d_ref, mo_ref, vo_ref):
        g = jnp.where(pl.program_id(1) == c_ref[0], own_ref[0], sib_ref[0])
        g_ref[0, 0] = g
        d_ref[0, 0], mo_ref[0, 0], vo_ref[0, 0] = _adamw_math(w_ref[0, 0], g, m_ref[0, 0], v_ref[0, 0])

    full = pl.BlockSpec((1, 1, tr, cols), lambda l, j, i: (l, j, i, 0))
    part = pl.BlockSpec((1, tr, cols), lambda l, j, i: (l, i, 0))
    outs = pl.pallas_call(
        body, grid=(depth, 2, h // tr), in_specs=[SMEM, full, part, part, full, full], out_specs=[full] * 4,
        out_shape=[SDS((depth, 2, h, cols), f32)] * 4,
        compiler_params=_cp("parallel", "parallel", "parallel"), name=name)(
            _core_index(), quad(w), own, sib, quad(m), quad(v))
    return [o.reshape(depth, r, cols) for o in outs]


def add_own_half(g4, recv, *, name):
    _, _, h, cols = g4.shape
    tr = _row_tile(h, cols)

    def body(c_ref, g_ref, r_ref, o_ref):
        own = jnp.where(c_ref[0] == 0, g_ref[0, 0], g_ref[0, 1])
        o_ref[0] = (own + r_ref[0]).astype(WIRE)

    part = pl.BlockSpec((1, tr, cols), lambda s, i: (s, i, 0))
    return pl.pallas_call(
        body, grid=(4, h // tr),
        in_specs=[SMEM, pl.BlockSpec((1, 2, tr, cols), lambda s, i: (s, 0, i, 0)), part], out_specs=part,
        out_shape=SDS((4, h, cols), WIRE), compiler_params=_cp("parallel", "parallel"), name=name)(
            _core_index(), g4, recv)


def _chip_index():
    return (2 * lax.axis_index("x") + lax.axis_index("y")).astype(jnp.int32).reshape(1)


def sum_pieces(sums, got, *, name, into, layer):
    _, h, cols = sums.shape
    tr = _row_tile(h, cols)

    def body(me_ref, s_ref, g_ref, stack_ref, o_ref):
        del stack_ref
        own = s_ref[0]
        for s in range(1, 4):
            own = jnp.where(me_ref[0] == s, s_ref[s], own)
        o_ref[0] = ((own.astype(f32) + g_ref[0].astype(f32)) + g_ref[1].astype(f32)) + g_ref[2].astype(f32)

    return pl.pallas_call(
        body, grid=(h // tr,),
        in_specs=[SMEM, pl.BlockSpec((4, tr, cols), lambda i: (0, i, 0)),
                  pl.BlockSpec((3, tr, cols), lambda i: (0, i, 0)), ANY],
        out_specs=pl.BlockSpec((1, tr, cols), lambda i: (layer, i, 0)),
        out_shape=SDS(into.shape, f32), input_output_aliases={3: 0},
        compiler_params=_cp("parallel"), name=name)(_chip_index(), sums, got, into)


def sum4(a, *, name, into=None, layer=0):
    shape = a.shape[1:]
    cols = shape[-1]
    a2 = a.reshape(4, -1, cols)
    rows = a2.shape[1]
    tr = _row_tile(rows, cols)

    def body(*refs):
        a_ref, o_ref = refs[0], refs[-1]
        tot = ((a_ref[0].astype(f32) + a_ref[1].astype(f32)) + a_ref[2].astype(f32)) + a_ref[3].astype(f32)
        if into is None:
            o_ref[...] = tot
        else:
            o_ref[0] = tot

    in_spec = pl.BlockSpec((4, tr, cols), lambda i: (0, i, 0))
    if into is None:
        out = pl.pallas_call(
            body, grid=(rows // tr,), in_specs=[in_spec], out_specs=pl.BlockSpec((tr, cols), lambda i: (i, 0)),
            out_shape=SDS((rows, cols), f32), compiler_params=_cp("parallel"), name=name)(a2)
        return out.reshape(shape)
    stack = into.reshape(into.shape[0], rows, cols)
    out = pl.pallas_call(
        body, grid=(rows // tr,), in_specs=[in_spec, ANY],
        out_specs=pl.BlockSpec((1, tr, cols), lambda i: (layer, i, 0)),
        out_shape=SDS(stack.shape, f32), input_output_aliases={1: 0},
        compiler_params=_cp("parallel"), name=name)(a2, stack)
    return out.reshape(into.shape)


ANY = pl.BlockSpec(memory_space=pl.ANY)


def _chip_copies(ins, outs, send, recv, bcast):
    x, y, c = lax.axis_index("x"), lax.axis_index("y"), lax.axis_index("c")
    me = 2 * x + y
    copies = []
    for k in range(len(ins)):
        for j, (px, py) in enumerate(((1 - x, y), (x, 1 - y), (1 - x, 1 - y))):
            copies.append(pltpu.make_async_remote_copy(
                src_ref=ins[k] if bcast[k] else ins[k].at[2 * px + py],
                dst_ref=outs[k].at[me] if bcast[k] else outs[k].at[j],
                send_sem=send.at[4 * k + j], recv_sem=recv.at[4 * k + j],
                device_id=(px, py, c), device_id_type=MESH))
        if bcast[k]:
            copies.append(pltpu.make_async_remote_copy(
                src_ref=ins[k], dst_ref=outs[k].at[me], send_sem=send.at[4 * k + 3], recv_sem=recv.at[4 * k + 3],
                device_id=(x, y, 1 - c), device_id_type=MESH))
    return copies


def chip_exchange(arrs, bcast, *, name):
    n = len(arrs)

    def body(*refs):
        copies = _chip_copies(refs[:n], refs[n:2 * n], refs[2 * n], refs[2 * n + 1], bcast)
        for cp in copies:
            cp.start()
        for cp in copies:
            cp.wait()

    return pl.pallas_call(
        body, in_specs=[ANY] * n, out_specs=[ANY] * n,
        out_shape=[SDS((4,) + tuple(a.shape) if b else (3,) + tuple(a.shape[1:]), a.dtype)
                   for a, b in zip(arrs, bcast)],
        scratch_shapes=[pltpu.SemaphoreType.DMA((4 * n,)), pltpu.SemaphoreType.DMA((4 * n,))],
        name=name)(*arrs)


def gather_weights(shards, *, name):
    n = len(shards)
    hd = shards[0].shape[0] // 2

    def body(*refs):
        ins, outs = refs[:n], refs[n:2 * n]
        send, recv = refs[2 * n:]
        x, y, c = lax.axis_index("x"), lax.axis_index("y"), lax.axis_index("c")
        me = 2 * x + y
        chips = ((1 - x, y), (x, 1 - y), (1 - x, 1 - y))
        mine, theirs = pl.ds(c * hd, hd), pl.ds((1 - c) * hd, hd)

        def ici(k, j, src, dst):
            px, py = chips[j]
            return pltpu.make_async_remote_copy(src_ref=src, dst_ref=dst, send_sem=send.at[7 * k + j],
                                                recv_sem=recv.at[7 * k + j], device_id=(px, py, c),
                                                device_id_type=MESH)

        def d2d(k, j, src, dst):
            return pltpu.make_async_remote_copy(src_ref=src, dst_ref=dst, send_sem=send.at[7 * k + 3 + j],
                                                recv_sem=recv.at[7 * k + 3 + j], device_id=(x, y, 1 - c),
                                                device_id_type=MESH)

        own, sent = [], []
        for k in range(n):
            own.append(d2d(k, 3, ins[k], outs[k].at[:, me]))
            own[-1].start()
            for j in range(3):
                sent.append(ici(k, j, ins[k].at[mine], outs[k].at[mine, me]))
                sent[-1].start()
        for k in range(n):
            for j, (px, py) in enumerate(chips):
                landed = outs[k].at[mine, 2 * px + py]
                ici(k, j, landed, landed).wait_recv()
                sent.append(d2d(k, j, landed, landed))
                sent[-1].start()
        for k in range(n):
            for j, (px, py) in enumerate(chips):
                other = outs[k].at[theirs, 2 * px + py]
                d2d(k, j, other, other).wait_recv()
        for cp in sent:
            cp.wait_send()
        for cp in own:
            cp.wait()

    return pl.pallas_call(
        body, in_specs=[ANY] * n, out_specs=[ANY] * n,
        out_shape=[SDS((a.shape[0], 4) + tuple(a.shape[1:]), a.dtype) for a in shards],
        scratch_shapes=[pltpu.SemaphoreType.DMA((7 * n,)), pltpu.SemaphoreType.DMA((7 * n,))],
        name=name)(*shards)


def _sibling_copies(ins, outs, send, recv, half):
    x, y, c = lax.axis_index("x"), lax.axis_index("y"), lax.axis_index("c")
    return [pltpu.make_async_remote_copy(
        src_ref=ins[k].at[:, 1 - c] if half[k] else ins[k], dst_ref=outs[k], send_sem=send.at[k],
        recv_sem=recv.at[k], device_id=(x, y, 1 - c), device_id_type=MESH) for k in range(len(ins))]


def sibling_exchange(arrs, half, *, name):
    n = len(arrs)
    piece = [(a.shape[0],) + a.shape[2:] if h else a.shape for a, h in zip(arrs, half)]

    def body(*refs):
        copies = _sibling_copies(refs[:n], refs[n:2 * n], refs[2 * n], refs[2 * n + 1], half)
        for cp in copies:
            cp.start()
        for cp in copies:
            cp.wait()

    return pl.pallas_call(
        body, in_specs=[ANY] * n, out_specs=[ANY] * n,
        out_shape=[SDS(tuple(p), a.dtype) for p, a in zip(piece, arrs)],
        scratch_shapes=[pltpu.SemaphoreType.DMA((n,)), pltpu.SemaphoreType.DMA((n,))],
        name=name)(*arrs)


def ssm_discretize(lam_re, lam_im, log_dt, b_re, b_im, c_re, c_im):
    dt = jnp.exp(log_dt)[..., None]
    mag = jnp.exp(lam_re * dt)
    abr = mag * jnp.cos(lam_im * dt)
    abi = mag * jnp.sin(lam_im * dt)
    den = lam_re * lam_re + lam_im * lam_im
    zr = ((abr - 1.0) * lam_re + abi * lam_im) / den
    zi = (abi * lam_re - (abr - 1.0) * lam_im) / den
    bbr = zr[..., None] * b_re - zi[..., None] * b_im
    bbi = zr[..., None] * b_im + zi[..., None] * b_re
    eye = jnp.eye(8, dtype=f32)
    bb = jnp.stack([bbr, bbi], axis=1).reshape(2, 2, SSM_TILES, 8, SSM_STATE, SSM_GROUP)
    bmat = jnp.einsum('dqjgph,gk->djghqkp', bb, eye).reshape(2, SSM_TILES, TILE_CH, 2 * TILE_ST)
    cc = jnp.stack([c_re, -c_im], axis=1).reshape(2, 2, SSM_TILES, 8, SSM_GROUP, SSM_STATE)
    cmat = jnp.einsum('dqjghp,gk->djqkpgh', cc, eye).reshape(2, SSM_TILES, 2 * TILE_ST, TILE_CH)
    n = SSM_GROUPS * SSM_STATE
    return abr.reshape(2, n), abi.reshape(2, n), bmat, cmat


SCAN_REV = (False, True, True, False)


def scan_tables_all(ar, ai, *, nv, name):
    a8 = jnp.stack([ar[0], ai[0], ar[0], -ai[0], ar[1], ai[1], ar[1], -ai[1]])
    n_state = SSM_TILES * TILE_ST

    def body(a_ref, o_ref):
        row = lax.broadcasted_iota(jnp.int32, (8, n_state), 0)

        def put(i, rows, re, im):
            for t in range(SSM_TILES):
                o_ref[i, t, rows, 0:TILE_ST] = re[:, TILE_ST * t:TILE_ST * (t + 1)]
                o_ref[i, t, rows, TILE_ST:] = im[:, TILE_ST * t:TILE_ST * (t + 1)]

        for i, rev in enumerate(SCAN_REV):
            a_r, a_i = a_ref[2 * i:2 * i + 1, :], a_ref[2 * i + 1:2 * i + 2, :]
            pr, pi = a_r, a_i
            for v in range(nv):
                dst = 40 + (nv - 1 - v if rev else v)
                put(i, slice(dst, dst + 1), pr, pi)
                if v + 1 < nv:
                    pr, pi = _cmul(a_r, a_i, pr, pi)
            big = (pr, pi)
            put(i, slice(24, 32), jnp.broadcast_to(big[0], (8, n_state)), jnp.broadcast_to(big[1], (8, n_state)))
            put(i, slice(32, 40), jnp.broadcast_to(a_r, (8, n_state)), jnp.broadcast_to(a_i, (8, n_state)))
            for n, k in enumerate((1, 2, 4)):
                cond = (row <= 7 - k) if rev else (row >= k)
                put(i, slice(8 * n, 8 * n + 8), jnp.where(cond, big[0], 0.0), jnp.where(cond, big[1], 0.0))
                big = _cmul(*big, *big)

    return pl.pallas_call(
        body, out_shape=SDS((4, SSM_TILES, 40 + nv, 2 * TILE_ST), f32),
        compiler_params=pltpu.CompilerParams(vmem_limit_bytes=VMEM_LIMIT), name=name)(a8)


def _tile_a(ga):
    t = ga.sum(axis=0).reshape(SSM_TILES, 2, TILE_ST)
    return t[:, 0].reshape(-1), t[:, 1].reshape(-1)


SMALL = ('norm1', 'q_gain', 'k_gain', 'sink', 'lam_re', 'lam_im', 'log_dt', 'b_re', 'b_im', 'c_re', 'c_im',
         'd_skip', 'norm2')
BIG = ('w_in', 'w_glu', 'w_out', 'w_ff1', 'w_ff2')
WEIGHTS = ('norm1', 'w_in', 'q_gain', 'k_gain', 'sink', 'lam_re', 'lam_im', 'log_dt', 'b_re', 'b_im', 'c_re',
           'c_im', 'd_skip', 'w_glu', 'w_out', 'norm2', 'w_ff1', 'w_ff2')


def _chunk(s):
    return min(512, s)


HOSTS_FIRST = {'attn': (("own", 'w_glu'), ("own", 'w_out'), ("next", 'w_in'), ("next", 'w_glu'), ("next", 'w_out')),
               'ssm0': (("own", 'w_ff1'),), 'ssm1': (("own", 'w_ff2'),),
               'ff1': (("next", 'w_ff1'),), 'ff2': (("next", 'w_ff2'),)}
HOSTS_LATER = {'attn': (("next", 'w_in'), ("next", 'w_glu'), ("next", 'w_out')),
               'ssm0': (("next", 'w_ff1'),), 'ssm1': (("next", 'w_ff2'),), 'ff1': (), 'ff2': ()}


def layer_forward(l, x, p, wb, li, own=None, nxt=None):
    s = x.shape[0]
    tm = min(512, s)
    sv = {}
    wb = dict(wb)
    src = {"own": own or {}, "next": nxt or {}}
    plan = {h: [e for e in es if e[1] in src[e[0]]] for h, es in (HOSTS_FIRST if own else HOSTS_LATER).items()}
    fetched = {}

    def hosted(host):
        es = plan[host]
        return ("chips", [src[w][k] for w, k in es], [True] * len(es)) if es else None

    def landed(host, got):
        for (w, k), g4 in zip(plan[host], got):
            if w == "own":
                wb[k] = layout_one(k, g4[None])
            else:
                fetched[k] = g4

    h1, z, _ = norm_mm(x, p['norm1'], wb['w_in'], li, relu2=False, name=f"l{l}_in", tm=tm)
    eq, ek = head_mean_matrix(ATT_WIDTH), head_mean_matrix(KV_WIDTH)
    qn, kv = qk_prep(z, p['q_gain'], p['k_gain'], eq, ek, name=f"l{l}_qk", tm=tm)
    att, got = attn_fwd(qn, kv, p['sink'], name=f"l{l}_attn", exchange=hosted('attn'))
    landed('attn', got)
    sv.update(qn=qn, kv=kv, eq=eq, ek=ek)
    (ar, ai, bmat, cmat), disc_vjp = jax.vjp(
        ssm_discretize, p['lam_re'], p['lam_im'], p['log_dt'], p['b_re'], p['b_im'], p['c_re'], p['c_im'])
    bmat16, cmat16 = bmat.astype(MX), cmat.astype(MX)
    ys, xbs = [], []
    tabs = scan_tables_all(ar, ai, nv=_chunk(s) // 8, name=f"l{l}_tabs")
    for d, rev in enumerate((False, True)):
        y_d, xb_d, got = ssm_fwd(z, tabs, bmat16[d], cmat16[d], rev=rev, name=f"l{l}_ssm{d}", chunk=_chunk(s),
                                 exchange=hosted(f'ssm{d}'))
        landed(f'ssm{d}', got)
        ys.append(y_d)
        xbs.append(xb_d)
    ypre, gg, mix = glu_fwd(ys[0], ys[1], z, att, p['d_skip'], wb['w_glu'], li, name=f"l{l}_glu", tm=min(256, s))
    x1, _ = mm_res(mix, wb['w_out'], li, x, name=f"l{l}_out", tm=tm)
    h2, a2, got = norm_mm(x1, p['norm2'], wb['w_ff1'], li, relu2=True, name=f"l{l}_ff1", tm=tm, exchange=hosted('ff1'))
    landed('ff1', got)
    x2, got = mm_res(a2, wb['w_ff2'], li, x1, name=f"l{l}_ff2", tm=tm, exchange=hosted('ff2'))
    landed('ff2', got)
    sv.update(x=x, h1=h1, z=z, xbs=xbs, tabs=tabs, bmat16=bmat16, cmat16=cmat16, disc_vjp=disc_vjp,
              ypre=ypre, gg=gg, mix=mix, x1=x1, h2=h2, a2=a2, wb=wb)
    return x2, sv, fetched


def layer_backward(l, gx2, gx2h, p, wb, li, sv, pend=None):
    s = gx2.shape[0]
    tm = min(512, s)
    ts = min(1024, s)
    g = {}
    extra = [pend[2]] if pend and pend[2] is not None else []
    gf, got = mm_nt(gx2h, wb['w_ff2'], li, name=f"l{l}_bff2", tm=tm, a2=sv['a2'],
                    exchange=pend and ("sibling", pend[1] + extra, [True] * len(pend[1]) + [False] * len(extra)))
    sums = pend and [add_own_half(a, b, name=f"l{l}_radd_{k}") for k, a, b in zip(BIG, pend[1], got)]
    if extra:
        sums.append(_elementwise(lambda a, b: (a + b,), [extra[0], got[-1]], 1, name=f"l{l}_radd_small")[0])
    g['w_ff2'] = mm_tn(sv['a2'], gx2h, name=f"l{l}_wff2", tk=1024, tn=1024, ts=ts).reshape(4, D_FF // 4, D_MODEL)
    gx1, gx1h, gn2 = mm_nt_norm(gf, wb['w_ff1'], li, sv['x1'], p['norm2'], gx2, name=f"l{l}_bff1", tm=min(256, s))
    g['norm2'] = gn2.sum(axis=0)
    g['w_ff1'] = mm_tn(sv['h2'], gf, name=f"l{l}_wff1", tk=1024, tn=1024, ts=ts, chip_major=True)
    gmix, _ = mm_nt(gx1h, wb['w_out'], li, name=f"l{l}_bout", tm=tm)
    g['w_out'] = mm_tn(sv['mix'], gx1h, name=f"l{l}_wout", tk=1024, tn=1024, ts=ts).reshape(4, D_MODEL // 4, D_MODEL)
    ggg, yg, gy = glu_bwd(gmix, sv['gg'], sv['ypre'], wb['w_glu'], li, name=f"l{l}_bglu", tm=min(256, s))
    g['w_glu'] = mm_tn(yg, ggg, name=f"l{l}_wglu", tk=512, tn=256, ts=ts, chip_major=True)
    gus, gas, gbs, gcs = [], [], [], []
    for d, rev in enumerate((False, True)):
        gu_d, ga_d, gb_d, gc_d, got = ssm_bwd(
            sv['z'], gy, sv['xbs'][d], sv['tabs'], sv['bmat16'][d], sv['cmat16'][d], rev=rev,
            name=f"l{l}_bssm{d}", chunk=_chunk(s),
            exchange=(pend and d == 0) and ("chips", sums, [False] * len(BIG) + [True] * len(extra)) or None)
        if pend and d == 0:
            pend[0](sums, got)
        gus.append(gu_d)
        gas.append(_tile_a(ga_d))
        gbs.append(gb_d)
        gcs.append(gc_d)
    gar = jnp.stack([gas[0][0], gas[1][0]])
    gai = jnp.stack([gas[0][1], gas[1][1]])
    (g['lam_re'], g['lam_im'], g['log_dt'], g['b_re'], g['b_im'], g['c_re'], g['c_im']) = sv['disc_vjp'](
        (gar, gai, jnp.stack(gbs), jnp.stack(gcs)))
    gqs, dkv, gsk = attn_bwd(sv['qn'], sv['kv'], gmix, p['sink'], name=f"l{l}_battn")
    g['sink'] = gsk[:, 0]
    gz, gqg, gkg, gd = gz_assemble(gqs, dkv, sv['z'], p['q_gain'], p['k_gain'], sv['eq'], sv['ek'], gus[0], gus[1],
                                   gy, p['d_skip'], name=f"l{l}_gz")
    g['q_gain'] = gqg.sum(axis=0).reshape(ATT_HEADS, HEAD_DIM).sum(axis=0)
    g['k_gain'] = gkg.sum(axis=0).reshape(KV_HEADS, HEAD_DIM).sum(axis=0)
    g['d_skip'] = gd.sum(axis=0)
    gx, gxh, gn1 = mm_nt_norm(gz, wb['w_in'], li, sv['x'], p['norm1'], gx1, name=f"l{l}_bin", tm=tm)
    g['norm1'] = gn1.sum(axis=0)
    gw_in = mm_tn(sv['h1'], gz, name=f"l{l}_win", tk=1024, tn=640, ts=ts)
    g['w_in'] = gw_in.reshape(D_MODEL, 4, IN_WIDTH // 4).transpose(1, 0, 2)
    return gx, gxh, g


def layout_one(k, g):
    depth = g.shape[0]
    if k == 'w_in':
        return g.transpose(0, 2, 1, 3).reshape(depth, D_MODEL, IN_WIDTH)
    if k == 'w_out':
        return g.reshape(depth, D_MODEL, D_MODEL)
    if k == 'w_ff2':
        return g.reshape(depth, D_FF, D_MODEL)
    return g


def stack_layouts(gathered):
    return {k: layout_one(k, g) for k, g in gathered.items()}


def local_step(x, target, small, wb):
    depth = wb['w_in'].shape[0]
    saves = []
    for l in range(depth):
        x, sv, _ = layer_forward(l, x, {k: small[k][l] for k in SMALL}, wb, l)
        saves.append(sv)
    gx, gxh, lparts = loss_grad(x, target, name="loss", tm=min(512, x.shape[0]))
    grads = [None] * depth
    for l in reversed(range(depth)):
        gx, gxh, grads[l] = layer_backward(l, gx, gxh, {k: small[k][l] for k in SMALL}, wb, l, saves[l])
    return lparts, gx, grads


def reduce_pieces(g):
    return [g[k].reshape(4, 2, g[k].shape[1] // 2, g[k].shape[2]) for k in BIG]


def reduce_chips(l, sums, got, stacks):
    return {k: sum_pieces(a, b, name=f"l{l}_rsum_{k}", into=stacks[k], layer=l) for k, a, b in zip(BIG, sums, got)}


def gather_first(shards):
    halves = [a.reshape(2, a.shape[0] // 2, a.shape[1]) for a in shards]
    got = gather_weights(halves, name="gather_first")
    return [a.transpose(1, 0, 2, 3).reshape(4, 2 * a.shape[2], a.shape[3]) for a in got]


def reduce_small(packed):
    got = sibling_exchange([packed], [False], name="small_rsib")
    pair = _elementwise(lambda a, b: (a + b,), [packed, got[0]], 1, name="small_radd")[0]
    got = chip_exchange([pair], [True], name="small_rchips")
    return sum4(got[0], name="small_rsum")


def _pack_small(tree):
    parts = []
    for k in SMALL:
        flat = tree[k].reshape(-1)
        parts.append(jnp.pad(flat, (0, (-flat.shape[0]) % 1024)).reshape(-1, 128))
    return jnp.concatenate(parts, axis=0)


def _unpack_small(packed, like):
    out, row = {}, 0
    for k in SMALL:
        n = like[k].size
        rows = -(-n // 1024) * 8
        out[k] = packed[row:row + rows].reshape(-1)[:n].reshape(like[k].shape)
        row += rows
    return out


def kernel(x, norm1, w_in, q_gain, k_gain, sink, lam_re, lam_im, log_dt, b_re, b_im, c_re, c_im, d_skip, w_glu, w_out, norm2, w_ff1, w_ff2, loss_target, m_norm1, m_w_in, m_q_gain, m_k_gain, m_sink, m_lam_re, m_lam_im, m_log_dt, m_b_re, m_b_im, m_c_re, m_c_im, m_d_skip, m_w_glu, m_w_out, m_norm2, m_w_ff1, m_w_ff2, v_norm1, v_w_in, v_q_gain, v_k_gain, v_sink, v_lam_re, v_lam_im, v_log_dt, v_b_re, v_b_im, v_c_re, v_c_im, v_d_skip, v_w_glu, v_w_out, v_norm2, v_w_ff1, v_w_ff2):
    w = dict(norm1=norm1, w_in=w_in, q_gain=q_gain, k_gain=k_gain, sink=sink, lam_re=lam_re, lam_im=lam_im,
             log_dt=log_dt, b_re=b_re, b_im=b_im, c_re=c_re, c_im=c_im, d_skip=d_skip, w_glu=w_glu, w_out=w_out,
             norm2=norm2, w_ff1=w_ff1, w_ff2=w_ff2)
    m = dict(norm1=m_norm1, w_in=m_w_in, q_gain=m_q_gain, k_gain=m_k_gain, sink=m_sink, lam_re=m_lam_re,
             lam_im=m_lam_im, log_dt=m_log_dt, b_re=m_b_re, b_im=m_b_im, c_re=m_c_re, c_im=m_c_im,
             d_skip=m_d_skip, w_glu=m_w_glu, w_out=m_w_out, norm2=m_norm2, w_ff1=m_w_ff1, w_ff2=m_w_ff2)
    v = dict(norm1=v_norm1, w_in=v_w_in, q_gain=v_q_gain, k_gain=v_k_gain, sink=v_sink, lam_re=v_lam_re,
             lam_im=v_lam_im, log_dt=v_log_dt, b_re=v_b_re, b_im=v_b_im, c_re=v_c_re, c_im=v_c_im,
             d_skip=v_d_skip, w_glu=v_w_glu, w_out=v_w_out, norm2=v_norm2, w_ff1=v_w_ff1, w_ff2=v_w_ff2)
    depth = w_in.shape[0]

    shards = {k: w[k].astype(WIRE) for k in BIG}
    small = {k: w[k] for k in SMALL}
    stacks = [{k: jnp.zeros((depth, w[k].shape[1] // 2, w[k].shape[2]), f32) for k in BIG}]

    xs = x[0]
    gathered = {'w_in': gather_first([shards['w_in'][0]])[0]}
    saves = []
    for l in range(depth):
        wb = stack_layouts({k: g[None] for k, g in gathered.items()})
        own = {k: shards[k][l] for k in BIG if k not in gathered}
        nxt = {k: shards[k][l + 1] for k in BIG} if l + 1 < depth else None
        xs, sv, gathered = layer_forward(l, xs, {k: small[k][l] for k in SMALL}, wb, 0, own, nxt)
        saves.append(sv)
    gx, gxh, lparts = loss_grad(xs, loss_target[0], name="loss", tm=min(512, xs.shape[0]))
    loss = lax.psum(0.5 * jnp.sum(lparts) / D_MODEL, ("x", "y", "c"))

    later_small = []

    def finisher(l):
        def finish(sums, got):
            stacks[0] = reduce_chips(l, sums[:len(BIG)], got[:len(BIG)], stacks[0])
            if len(got) > len(BIG):
                later_small.append(sum4(got[-1], name="later_small_rsum"))
        return finish

    grads, pend = [None] * depth, None
    for l in reversed(range(depth)):
        gx, gxh, g = layer_backward(l, gx, gxh, {k: small[k][l] for k in SMALL}, saves[l]['wb'], 0, saves[l], pend)
        grads[l] = {k: g[k] for k in SMALL}
        packed = _pack_small({k: jnp.stack([grads[j][k] for j in range(1, depth)]) for k in SMALL}) if l == 1 else None
        pend = (finisher(l), reduce_pieces(g), packed)
    got = sibling_exchange(pend[1], [True] * len(BIG), name="last_rsib")
    sums = [add_own_half(a, b, name=f"last_radd_{k}") for k, a, b in zip(BIG, pend[1], got)]
    pend[0](sums, chip_exchange(sums, [False] * len(BIG), name="last_rchips"))

    sib = sibling_exchange([stacks[0][k] for k in BIG], [False] * len(BIG), name="reduce_back")
    first_small = reduce_small(_pack_small({k: grads[0][k][None] for k in SMALL}))
    like = {k: w[k] for k in SMALL}
    g_first = _unpack_small(first_small, {k: w[k][:1] for k in SMALL})
    g_later = _unpack_small(later_small[0], {k: w[k][1:] for k in SMALL})
    gfull = {k: jnp.concatenate([g_first[k], g_later[k]], axis=0) for k in SMALL}
    gsmall = _pack_small(gfull)

    delta, new_m, new_v = {}, {}, {}
    for k, sib_k in zip(BIG, sib):
        gfull[k], delta[k], new_m[k], new_v[k] = adamw_halves(w[k], stacks[0][k], sib_k, m[k], v[k],
                                                              name=f"adamw_{k}")
    ds, ms, vs = adamw(_pack_small(like), gsmall, _pack_small({k: m[k] for k in SMALL}),
                       _pack_small({k: v[k] for k in SMALL}), name="adamw_small")
    delta.update(_unpack_small(ds, like))
    new_m.update(_unpack_small(ms, like))
    new_v.update(_unpack_small(vs, like))

    return (loss, gx[None], *[gfull[k] for k in WEIGHTS], *[delta[k] for k in WEIGHTS],
            *[new_m[k] for k in WEIGHTS], *[new_v[k] for k in WEIGHTS])
```

```python
import functools
import math

import jax
import jax.numpy as jnp
from jax import lax
from jax.experimental import pallas as pl
from jax.experimental.pallas import tpu as pltpu

f32 = jnp.float32
MX = jnp.bfloat16
WIRE = jnp.bfloat16
SDS = jax.ShapeDtypeStruct

D_MODEL = 1024
DEPTH = 4
ATT_HEADS = 8
KV_HEADS = 2
GQA = ATT_HEADS // KV_HEADS
HEAD_DIM = 64
ATT_WIDTH = ATT_HEADS * HEAD_DIM
KV_WIDTH = KV_HEADS * HEAD_DIM
BLOCK = 128
SSM_WIDTH = 512
SSM_GROUP = 16
SSM_GROUPS = 32
SSM_STATE = 64
SSM_TILES = 4
TILE_CH = SSM_WIDTH // SSM_TILES
TILE_ST = SSM_GROUPS * SSM_STATE // SSM_TILES
TILES_PER_STEP = 2
IN_WIDTH = ATT_WIDTH + 2 * KV_WIDTH + SSM_WIDTH
U_OFF = ATT_WIDTH + 2 * KV_WIDTH
D_FF = 4096
EPS = 1e-6
NEG = float(jnp.finfo(jnp.float32).min)
SLOPES = tuple(2.0 ** (-8.0 * (h + 1) / ATT_HEADS) for h in range(ATT_HEADS))

ADAM_LR, ADAM_B1, ADAM_B2, ADAM_EPS, ADAM_WD, ADAM_STEP = 0.001, 0.9, 0.999, 1e-08, 0.01, 10

VMEM_LIMIT = 48 * 1024 * 1024
MESH = pl.DeviceIdType.MESH

NT = (((1,), (1,)), ((), ()))
TN = (((0,), (0,)), ((), ()))


def _cp(*sem):
    return pltpu.CompilerParams(dimension_semantics=sem, vmem_limit_bytes=VMEM_LIMIT)


def _dot(a, b, dims=None):
    if dims is None:
        return jnp.dot(a, b, preferred_element_type=f32)
    return lax.dot_general(a, b, dims, preferred_element_type=f32)


def _rows8(v):
    return v.reshape(v.shape[0] // 8, 8, v.shape[1]).sum(axis=0)


def _layer_spec(w, l):
    nd = w.ndim
    return pl.BlockSpec((1,) + tuple(w.shape[1:]), lambda i: (l,) + (0,) * (nd - 1))


def _row_spec(tm, width):
    return pl.BlockSpec((tm, width), lambda i: (i, 0))


def _call(body, *, grid, in_specs, out_specs, out_shape, args, sem, name, scratch=(), exchange=None):
    n_in, n_out, n_scr = len(in_specs), len(out_specs), len(scratch)
    if exchange is None:
        res = pl.pallas_call(body, grid=grid, in_specs=in_specs, out_specs=out_specs, out_shape=out_shape,
                             scratch_shapes=list(scratch), compiler_params=_cp(*sem), name=name)(*args)
        return list(res), []
    kind, arrs, flags = exchange
    nx = len(arrs)
    if kind == "chips":
        make, nsem = _chip_copies, 4 * nx
        got = [SDS((4,) + tuple(a.shape) if b else (3,) + tuple(a.shape[1:]), a.dtype) for a, b in zip(arrs, flags)]
    else:
        make, nsem = _sibling_copies, nx
        got = [SDS((a.shape[0],) + tuple(a.shape[2:]) if h else tuple(a.shape), a.dtype)
               for a, h in zip(arrs, flags)]

    def hosted(*refs):
        ins, xin = refs[:n_in], refs[n_in:n_in + nx]
        outs = refs[n_in + nx:n_in + nx + n_out]
        xout = refs[n_in + nx + n_out:n_in + 2 * nx + n_out]
        scr = refs[n_in + 2 * nx + n_out:]
        copies = make(xin, xout, scr[n_scr], scr[n_scr + 1], flags)
        first = functools.reduce(jnp.logical_and, [pl.program_id(d) == 0 for d in range(len(grid))])
        last = functools.reduce(jnp.logical_and, [pl.program_id(d) == grid[d] - 1 for d in range(len(grid))])

        @pl.when(first)
        def _():
            for cp in copies:
                cp.start()

        body(*ins, *outs, *scr[:n_scr])

        @pl.when(last)
        def _():
            for cp in copies:
                cp.wait()

    res = pl.pallas_call(
        hosted, grid=grid, in_specs=list(in_specs) + [ANY] * nx, out_specs=list(out_specs) + [ANY] * nx,
        out_shape=list(out_shape) + got,
        scratch_shapes=list(scratch) + [pltpu.SemaphoreType.DMA((nsem,)), pltpu.SemaphoreType.DMA((nsem,))],
        compiler_params=_cp(*["arbitrary"] * len(grid)), name=name)(*args, *arrs)
    return list(res[:n_out]), list(res[n_out:])


def norm_mm(x, gain, w, l, *, relu2, name, tm, exchange=None):
    s, d = x.shape
    if relu2:
        nblk, cb = w.shape[1], w.shape[3]
        n = nblk * cb
    else:
        n = w.shape[2]

    def body(x_ref, g_ref, w_ref, h_ref, y_ref):
        xf = x_ref[...]
        r = lax.rsqrt(jnp.mean(xf * xf, axis=-1, keepdims=True) + EPS)
        h = (xf * r * g_ref[...]).astype(MX)
        h_ref[...] = h
        if relu2:
            for b in range(nblk):
                f = jnp.maximum(_dot(h, w_ref[0, b]), 0.0)
                y_ref[:, cb * b:cb * (b + 1)] = (f * f).astype(MX)
        else:
            y_ref[...] = _dot(h, w_ref[0])

    (h, y), got = _call(
        body, grid=(s // tm,),
        in_specs=[_row_spec(tm, d), pl.BlockSpec((1, d), lambda i: (0, 0)), _layer_spec(w, l)],
        out_specs=[_row_spec(tm, d), _row_spec(tm, n)],
        out_shape=[SDS((s, d), MX), SDS((s, n), MX if relu2 else f32)],
        args=(x, gain.reshape(1, d), w), sem=("parallel",), name=name, exchange=exchange)
    return h, y, got


def mm_res(a, w, l, res, *, name, tm, exchange=None):
    s, k = a.shape
    n = w.shape[2]

    def body(a_ref, w_ref, r_ref, o_ref):
        o_ref[...] = r_ref[...] + _dot(a_ref[...], w_ref[0])

    (out,), got = _call(
        body, grid=(s // tm,), in_specs=[_row_spec(tm, k), _layer_spec(w, l), _row_spec(tm, n)],
        out_specs=[_row_spec(tm, n)], out_shape=[SDS((s, n), f32)], args=(a, w, res), sem=("parallel",),
        name=name, exchange=exchange)
    return out, got


def mm_nt(gy, w, l, *, name, tm, a2=None, exchange=None):
    s, n = gy.shape
    k = w.shape[1]
    kb = min(k, 1024)

    def body(*refs):
        g_ref, w_ref, o_ref = refs[0], refs[1], refs[-1]
        g = g_ref[...]
        for b in range(k // kb):
            cols = slice(kb * b, kb * (b + 1))
            acc = _dot(g, w_ref[0, cols, :], NT)
            if a2 is not None:
                acc = acc * (2.0 * jnp.sqrt(refs[2][:, cols].astype(f32)))
            o_ref[:, cols] = acc.astype(o_ref.dtype)

    in_specs = [_row_spec(tm, n), _layer_spec(w, l)]
    args = [gy, w]
    if a2 is not None:
        in_specs.append(_row_spec(tm, k))
        args.append(a2)
    (out,), got = _call(body, grid=(s // tm,), in_specs=in_specs, out_specs=[_row_spec(tm, k)],
                        out_shape=[SDS((s, k), f32 if a2 is None else MX)], args=args, sem=("parallel",),
                        name=name, exchange=exchange)
    return out, got


def mm_nt_norm(gy, w, l, x, gain, res, *, name, tm):
    s, n = gy.shape
    d = x.shape[1]

    def body(g_ref, w_ref, x_ref, gn_ref, r_ref, o_ref, o16_ref, gg_ref):
        @pl.when(pl.program_id(0) == 0)
        def _():
            gg_ref[...] = jnp.zeros_like(gg_ref)

        if w.ndim == 3:
            gh = _dot(g_ref[...], w_ref[0], NT)
        else:
            cb = w.shape[3]
            gh = _dot(g_ref[:, 0:cb], w_ref[0, 0], NT)
            for b in range(1, w.shape[1]):
                gh = gh + _dot(g_ref[:, cb * b:cb * (b + 1)], w_ref[0, b], NT)
        xf = x_ref[...]
        r = lax.rsqrt(jnp.mean(xf * xf, axis=-1, keepdims=True) + EPS)
        xh = xf * r
        t = gh * gn_ref[...]
        gx = r_ref[...] + r * (t - xh * jnp.mean(t * xh, axis=-1, keepdims=True))
        o_ref[...] = gx
        o16_ref[...] = gx.astype(MX)
        gg_ref[...] += _rows8(gh * xh)

    return pl.pallas_call(
        body, grid=(s // tm,),
        in_specs=[_row_spec(tm, n), _layer_spec(w, l), _row_spec(tm, d), pl.BlockSpec((1, d), lambda i: (0, 0)),
                  _row_spec(tm, d)],
        out_specs=[_row_spec(tm, d), _row_spec(tm, d), pl.BlockSpec((8, d), lambda i: (0, 0))],
        out_shape=[SDS((s, d), f32), SDS((s, d), MX), SDS((8, d), f32)],
        compiler_params=_cp("arbitrary"), name=name)(gy, w, x, gain.reshape(1, d), res)


def mm_tn(xa, gy, *, name, tk, tn, ts, chip_major=False):
    s, k = xa.shape
    n = gy.shape[1]

    def body(x_ref, g_ref, o_ref):
        @pl.when(pl.program_id(2) == 0)
        def _():
            o_ref[...] = jnp.zeros_like(o_ref)

        acc = _dot(x_ref[...], g_ref[...], TN)
        if chip_major:
            o_ref[0] += acc
        else:
            o_ref[...] += acc

    if chip_major:
        out_spec = pl.BlockSpec((1, tk, tn), lambda a, b, c: (b, a, 0))
        out_shape = SDS((n // tn, k, tn), f32)
    else:
        out_spec = pl.BlockSpec((tk, tn), lambda a, b, c: (a, b))
        out_shape = SDS((k, n), f32)
    return pl.pallas_call(
        body, grid=(k // tk, n // tn, s // ts),
        in_specs=[pl.BlockSpec((ts, tk), lambda a, b, c: (c, a)), pl.BlockSpec((ts, tn), lambda a, b, c: (c, b))],
        out_specs=out_spec, out_shape=out_shape,
        compiler_params=_cp("parallel", "parallel", "arbitrary"), name=name)(xa, gy)


def head_mean_matrix(width):
    return jnp.kron(jnp.eye(width // HEAD_DIM, dtype=f32), jnp.full((HEAD_DIM, HEAD_DIM), 1.0 / HEAD_DIM, f32)).astype(MX)


def _head_mean(t, e_ref):
    hi = t.astype(MX)
    lo = (t - hi.astype(f32)).astype(MX)
    return _dot(hi, e_ref[...]) + _dot(lo, e_ref[...])


def qk_prep(z, q_gain, k_gain, eq, ek, *, name, tm):
    s = z.shape[0]

    def body(z_ref, qg_ref, kg_ref, eq_ref, ek_ref, q_ref, kv_ref):
        q = z_ref[:, 0:ATT_WIDTH]
        r = lax.rsqrt(_head_mean(q * q, eq_ref) + EPS)
        q_ref[...] = ((q * r * qg_ref[...]) * 0.125).astype(MX)
        k = z_ref[:, ATT_WIDTH:ATT_WIDTH + KV_WIDTH]
        r = lax.rsqrt(_head_mean(k * k, ek_ref) + EPS)
        kv_ref[:, 0:KV_WIDTH] = (k * r * kg_ref[...]).astype(MX)
        kv_ref[:, KV_WIDTH:] = z_ref[:, ATT_WIDTH + KV_WIDTH:U_OFF].astype(MX)

    const = lambda a: pl.BlockSpec(a.shape, lambda i: (0, 0))
    qg = jnp.tile(q_gain.reshape(1, HEAD_DIM), (1, ATT_HEADS))
    kg = jnp.tile(k_gain.reshape(1, HEAD_DIM), (1, KV_HEADS))
    return pl.pallas_call(
        body, grid=(s // tm,), in_specs=[_row_spec(tm, IN_WIDTH), const(qg), const(kg), const(eq), const(ek)],
        out_specs=[_row_spec(tm, ATT_WIDTH), _row_spec(tm, 2 * KV_WIDTH)],
        out_shape=[SDS((s, ATT_WIDTH), MX), SDS((s, 2 * KV_WIDTH), MX)],
        compiler_params=_cp("parallel"), name=name)(z, qg, kg, eq, ek)


def _attn_mask(i, nb):
    row = lax.broadcasted_iota(jnp.int32, (GQA * BLOCK, 3 * BLOCK), 0) & (BLOCK - 1)
    col = lax.broadcasted_iota(jnp.int32, (GQA * BLOCK, 3 * BLOCK), 1)
    dist = jnp.abs(row - col + BLOCK)
    valid = (dist <= BLOCK) & ((col >= BLOCK) | (i >= 1)) & ((col < 2 * BLOCK) | (i <= nb - 2))
    return dist.astype(f32), valid


def _attn_specs(nb):
    return [pl.BlockSpec((BLOCK, ATT_WIDTH), lambda i: (i, 0)),
            pl.BlockSpec((BLOCK, 2 * KV_WIDTH), lambda i: (jnp.maximum(i - 1, 0), 0)),
            pl.BlockSpec((BLOCK, 2 * KV_WIDTH), lambda i: (i, 0)),
            pl.BlockSpec((BLOCK, 2 * KV_WIDTH), lambda i: (jnp.minimum(i + 1, nb - 1), 0)),
            pl.BlockSpec(memory_space=pltpu.SMEM)]


def _attn_probs(sc, kvh, distf, valid, sink_ref):
    row = lax.broadcasted_iota(jnp.int32, (GQA * BLOCK, 1), 0)
    slope = jnp.full((GQA * BLOCK, 1), SLOPES[GQA * kvh], f32)
    sk = jnp.full((GQA * BLOCK, 1), sink_ref[GQA * kvh], f32)
    for j in range(1, GQA):
        slope = jnp.where(row >= BLOCK * j, SLOPES[GQA * kvh + j], slope)
        sk = jnp.where(row >= BLOCK * j, sink_ref[GQA * kvh + j], sk)
    sg = jnp.where(valid, sc - slope * distf, NEG)
    m = jnp.maximum(jnp.max(sg, axis=-1, keepdims=True), sk)
    e = jnp.exp(sg - m)
    es = jnp.exp(sk - m)
    inv = 1.0 / (jnp.sum(e, axis=-1, keepdims=True) + es)
    return e * inv, es * inv


def _stack_heads(ref, kvh):
    return jnp.concatenate([ref[:, HEAD_DIM * (GQA * kvh + g):HEAD_DIM * (GQA * kvh + g + 1)] for g in range(GQA)],
                           axis=0)


def attn_fwd(qn, kv, sink, *, name, exchange=None):
    s = qn.shape[0]
    nb = s // BLOCK

    def body(q_ref, kp_ref, kc_ref, kn_ref, sink_ref, o_ref):
        i = pl.program_id(0)
        distf, valid = _attn_mask(i, nb)
        kv3 = jnp.concatenate([kp_ref[...], kc_ref[...], kn_ref[...]], axis=0)
        for kvh in range(KV_HEADS):
            kn = kv3[:, HEAD_DIM * kvh:HEAD_DIM * (kvh + 1)]
            vh = kv3[:, KV_WIDTH + HEAD_DIM * kvh:KV_WIDTH + HEAD_DIM * (kvh + 1)]
            sc = _dot(_stack_heads(q_ref, kvh), kn, NT)
            p, _ = _attn_probs(sc, kvh, distf, valid, sink_ref)
            o = _dot(p.astype(MX), vh)
            for g in range(GQA):
                h = GQA * kvh + g
                o_ref[:, HEAD_DIM * h:HEAD_DIM * (h + 1)] = o[BLOCK * g:BLOCK * (g + 1)].astype(o_ref.dtype)

    (out,), got = _call(body, grid=(nb,), in_specs=_attn_specs(nb),
                        out_specs=[pl.BlockSpec((BLOCK, ATT_WIDTH), lambda i: (i, 0))],
                        out_shape=[SDS((s, ATT_WIDTH), MX)], args=(qn, kv, kv, kv, sink), sem=("parallel",),
                        name=name, exchange=exchange)
    return out, got


def attn_bwd(qn, kv, gmix, sink, *, name):
    s = qn.shape[0]
    nb = s // BLOCK

    def body(q_ref, kp_ref, kc_ref, kn_ref, sink_ref, go_ref, gq_ref, dkv_ref, gs_ref):
        i = pl.program_id(0)

        @pl.when(i == 0)
        def _():
            gs_ref[...] = jnp.zeros_like(gs_ref)

        distf, valid = _attn_mask(i, nb)
        kv3 = jnp.concatenate([kp_ref[...], kc_ref[...], kn_ref[...]], axis=0)
        for kvh in range(KV_HEADS):
            kn = kv3[:, HEAD_DIM * kvh:HEAD_DIM * (kvh + 1)]
            vh = kv3[:, KV_WIDTH + HEAD_DIM * kvh:KV_WIDTH + HEAD_DIM * (kvh + 1)]
            qs = _stack_heads(q_ref, kvh)
            dos = _stack_heads(go_ref, kvh).astype(MX)
            p, psink = _attn_probs(_dot(qs, kn, NT), kvh, distf, valid, sink_ref)
            dp = _dot(dos, vh, NT)
            delta = jnp.sum(p * dp, axis=-1, keepdims=True)
            gsk = psink * delta
            for g in range(GQA):
                h = GQA * kvh + g
                gs_ref[h:h + 1, :] -= jnp.broadcast_to(
                    jnp.sum(gsk[BLOCK * g:BLOCK * (g + 1)], axis=0, keepdims=True), (1, 128))
            ds = (p * (dp - delta)).astype(MX)
            gv = _dot(p.astype(MX), dos, TN)
            gkn = _dot(ds, qs, TN)
            gqs = _dot(ds, kn)
            for g in range(GQA):
                h = GQA * kvh + g
                gq_ref[:, HEAD_DIM * h:HEAD_DIM * (h + 1)] = gqs[BLOCK * g:BLOCK * (g + 1)]
            for b in range(3):
                dkv_ref[b, :, HEAD_DIM * kvh:HEAD_DIM * (kvh + 1)] = gkn[BLOCK * b:BLOCK * (b + 1)]
                dkv_ref[b, :, KV_WIDTH + HEAD_DIM * kvh:KV_WIDTH + HEAD_DIM * (kvh + 1)] = gv[BLOCK * b:BLOCK * (b + 1)]

    return pl.pallas_call(
        body, grid=(nb,),
        in_specs=_attn_specs(nb) + [pl.BlockSpec((BLOCK, ATT_WIDTH), lambda i: (i, 0))],
        out_specs=[pl.BlockSpec((BLOCK, ATT_WIDTH), lambda i: (i, 0)),
                   pl.BlockSpec((3, BLOCK, 2 * KV_WIDTH), lambda i: (0, i, 0)),
                   pl.BlockSpec((ATT_HEADS, 128), lambda i: (0, 0))],
        out_shape=[SDS((s, ATT_WIDTH), f32), SDS((3, s, 2 * KV_WIDTH), f32), SDS((ATT_HEADS, 128), f32)],
        compiler_params=_cp("arbitrary"), name=name)(qn, kv, kv, kv, sink, gmix)


def gz_assemble(gqs, dkv, z, q_gain, k_gain, eq, ek, gu_f, gu_r, gy, d_skip, *, name):
    s = z.shape[0]
    nb = s // BLOCK

    def norm_bwd(t_in, g_out, gain_ref, e_ref):
        r = lax.rsqrt(_head_mean(t_in * t_in, e_ref) + EPS)
        hat = t_in * r
        t = g_out * gain_ref[...]
        return r * (t - hat * _head_mean(t * hat, e_ref)), g_out * hat

    def body(gq_ref, d0_ref, d1_ref, d2_ref, z_ref, qg_ref, kg_ref, eq_ref, ek_ref, guf_ref, gur_ref, gy_ref, ds_ref,
             gz_ref, gqg_ref, gkg_ref, gd_ref):
        i = pl.program_id(0)

        @pl.when(i == 0)
        def _():
            gqg_ref[...] = jnp.zeros_like(gqg_ref)
            gkg_ref[...] = jnp.zeros_like(gkg_ref)
            gd_ref[...] = jnp.zeros_like(gd_ref)

        gq, gg = norm_bwd(z_ref[:, 0:ATT_WIDTH], gq_ref[...] * 0.125, qg_ref, eq_ref)
        gz_ref[:, 0:ATT_WIDTH] = gq.astype(MX)
        gqg_ref[...] += _rows8(gg)
        gkv = d1_ref[0] + jnp.where(i + 1 < nb, d0_ref[0], 0.0) + jnp.where(i >= 1, d2_ref[0], 0.0)
        gk, gg = norm_bwd(z_ref[:, ATT_WIDTH:ATT_WIDTH + KV_WIDTH], gkv[:, 0:KV_WIDTH], kg_ref, ek_ref)
        gz_ref[:, ATT_WIDTH:ATT_WIDTH + KV_WIDTH] = gk.astype(MX)
        gkg_ref[...] += _rows8(gg)
        gz_ref[:, ATT_WIDTH + KV_WIDTH:U_OFF] = gkv[:, KV_WIDTH:].astype(MX)
        gyv = gy_ref[...]
        gz_ref[:, U_OFF:IN_WIDTH] = (guf_ref[...] + gur_ref[...] + ds_ref[...] * gyv).astype(MX)
        gd_ref[...] += _rows8(gyv * z_ref[:, U_OFF:IN_WIDTH])

    row = lambda w: pl.BlockSpec((BLOCK, w), lambda i: (i, 0))
    const = lambda a: pl.BlockSpec(a.shape, lambda i: (0, 0))
    qg = jnp.tile(q_gain.reshape(1, HEAD_DIM), (1, ATT_HEADS))
    kg = jnp.tile(k_gain.reshape(1, HEAD_DIM), (1, KV_HEADS))
    return pl.pallas_call(
        body, grid=(nb,),
        in_specs=[row(ATT_WIDTH),
                  pl.BlockSpec((1, BLOCK, 2 * KV_WIDTH), lambda i: (0, jnp.minimum(i + 1, nb - 1), 0)),
                  pl.BlockSpec((1, BLOCK, 2 * KV_WIDTH), lambda i: (1, i, 0)),
                  pl.BlockSpec((1, BLOCK, 2 * KV_WIDTH), lambda i: (2, jnp.maximum(i - 1, 0), 0)),
                  row(IN_WIDTH), const(qg), const(kg), const(eq), const(ek),
                  row(SSM_WIDTH), row(SSM_WIDTH), row(SSM_WIDTH), pl.BlockSpec((1, SSM_WIDTH), lambda i: (0, 0))],
        out_specs=[row(IN_WIDTH), pl.BlockSpec((8, ATT_WIDTH), lambda i: (0, 0)),
                   pl.BlockSpec((8, KV_WIDTH), lambda i: (0, 0)), pl.BlockSpec((8, SSM_WIDTH), lambda i: (0, 0))],
        out_shape=[SDS((s, IN_WIDTH), MX), SDS((8, ATT_WIDTH), f32), SDS((8, KV_WIDTH), f32),
                   SDS((8, SSM_WIDTH), f32)],
        compiler_params=_cp("arbitrary"), name=name)(
            gqs, dkv, dkv, dkv, z, qg, kg, eq, ek, gu_f, gu_r, gy, d_skip.reshape(1, SSM_WIDTH))


def _cmul(ar, ai, xr, xi):
    return ar * xr - ai * xi, ar * xi + ai * xr


def _permute_rows(src_ref, dst_ref, nv):
    for v in range(nv):
        dst_ref[8 * v:8 * v + 8, :] = src_ref[pl.ds(v, 8, stride=nv), :]


def _unpermute_rows(val, dst_ref, nv):
    for v in range(nv):
        dst_ref[pl.ds(v, 8, stride=nv), :] = val[8 * v:8 * v + 8, :]


def _scan_chunk(x_ref, tab_ref, carry_ref, nv, rev, acc=None):
    L = TILE_ST
    order = list(range(nv - 1, -1, -1)) if rev else list(range(nv))
    a_r, a_i = tab_ref[32:40, :L], tab_ref[32:40, L:]
    pr = pi = None
    for v in order:
        rows = slice(8 * v, 8 * v + 8)
        xr, xi = x_ref[rows, :L], x_ref[rows, L:]
        if pr is not None:
            mr, mi = _cmul(a_r, a_i, pr, pi)
            xr, xi = xr + mr, xi + mi
            x_ref[rows, :L] = xr
            x_ref[rows, L:] = xi
        pr, pi = xr, xi
    er, ei = pr, pi
    row = lax.broadcasted_iota(jnp.int32, (8, L), 0)
    edge = row == (7 if rev else 0)
    sh = 7 if rev else 1
    fr = jnp.where(edge, carry_ref[:, :L], pltpu.roll(er, sh, 0))
    fi = jnp.where(edge, carry_ref[:, L:], pltpu.roll(ei, sh, 0))
    for n, k in enumerate((1, 2, 4)):
        mr, mi = tab_ref[8 * n:8 * n + 8, :L], tab_ref[8 * n:8 * n + 8, L:]
        sh = (8 - k) if rev else k
        rr, ri = pltpu.roll(fr, sh, 0), pltpu.roll(fi, sh, 0)
        fr, fi = fr + mr * rr - mi * ri, fi + mr * ri + mi * rr
    dr, di = _cmul(tab_ref[24:32, :L], tab_ref[24:32, L:], fr, fi)
    last = 0 if rev else 7
    carry_ref[:, :L] = jnp.broadcast_to((dr + er)[last:last + 1, :], (8, L))
    carry_ref[:, L:] = jnp.broadcast_to((di + ei)[last:last + 1, :], (8, L))
    qr, qi = fr, fi
    if acc is not None:
        sr, si = jnp.zeros((8, L), f32), jnp.zeros((8, L), f32)
    for v in order:
        rows = slice(8 * v, 8 * v + 8)
        trow = slice(40 + v, 41 + v)
        mr, mi = _cmul(tab_ref[trow, :L], tab_ref[trow, L:], fr, fi)
        xr, xi = x_ref[rows, :L] + mr, x_ref[rows, L:] + mi
        x_ref[rows, :L] = xr
        x_ref[rows, L:] = xi
        if acc is not None:
            gr, gi = acc[0][rows, :L], acc[0][rows, L:]
            sr, si = sr + gr * qr + gi * qi, si + gi * qr - gr * qi
            qr, qi = xr, xi
    if acc is not None:
        acc[1][:, :L] += sr
        acc[1][:, L:] += si


def ssm_fwd(z, tabs, bmat, cmat, *, rev, name, chunk, exchange=None):
    var = 2 if rev else 0
    s = z.shape[0]
    nc = s // chunk
    nv = chunk // 8
    ci = (lambda i: nc - 1 - i) if rev else (lambda i: i)

    tp = TILES_PER_STEP

    def body(*refs):
        u_refs = refs[:tp]
        tab_ref, b_ref, c_ref, y_ref, xb_ref, u_scr, x_scr, carry = refs[tp:]

        @pl.when(pl.program_id(1) == 0)
        def _():
            carry[...] = jnp.zeros_like(carry)

        for t in range(tp):
            xb_ref[0, :, 2 * TILE_ST * t:2 * TILE_ST * (t + 1)] = carry[t]
            _permute_rows(u_refs[t], u_scr.at[t], nv)
            x_scr[t] = _dot(u_scr[t].astype(MX), b_ref[t])
        for t in range(tp):
            _scan_chunk(x_scr.at[t], tab_ref.at[0, t], carry.at[t], nv, rev)
        for t in range(tp):
            _unpermute_rows(_dot(x_scr[t].astype(MX), c_ref[t]), u_scr.at[t], nv)
            y_ref[:, TILE_CH * t:TILE_CH * (t + 1)] = u_scr[t]

    u_specs = [pl.BlockSpec((chunk, TILE_CH), lambda j, i, t=t: (ci(i), U_OFF // TILE_CH + tp * j + t))
               for t in range(tp)]
    (y, xb), got = _call(
        body, grid=(SSM_TILES // tp, nc),
        in_specs=u_specs + [pl.BlockSpec((1, tp, 40 + nv, 2 * TILE_ST), lambda j, i: (var, j, 0, 0)),
                            pl.BlockSpec((tp, TILE_CH, 2 * TILE_ST), lambda j, i: (j, 0, 0)),
                            pl.BlockSpec((tp, 2 * TILE_ST, TILE_CH), lambda j, i: (j, 0, 0))],
        out_specs=[pl.BlockSpec((chunk, tp * TILE_CH), lambda j, i: (ci(i), j)),
                   pl.BlockSpec((1, 8, tp * 2 * TILE_ST), lambda j, i: (ci(i), 0, j))],
        out_shape=[SDS((s, SSM_WIDTH), f32), SDS((nc, 8, SSM_TILES * 2 * TILE_ST), f32)],
        scratch=[pltpu.VMEM((tp, chunk, TILE_CH), f32), pltpu.VMEM((tp, chunk, 2 * TILE_ST), f32),
                 pltpu.VMEM((tp, 8, 2 * TILE_ST), f32)],
        args=(*([z] * tp), tabs, bmat, cmat), sem=("parallel", "arbitrary"), name=name, exchange=exchange)
    return y, xb, got


def ssm_bwd(z, gy, xb, tabs, bmat, cmat, *, rev, name, chunk, exchange=None):
    var = 2 if rev else 0
    s = z.shape[0]
    nc = s // chunk
    nv = chunk // 8
    ci = (lambda i: i) if rev else (lambda i: nc - 1 - i)

    tp = TILES_PER_STEP
    w2 = 2 * TILE_ST

    def body(*refs):
        u_refs, gy_refs = refs[:tp], refs[tp:2 * tp]
        (xb_ref, ts_ref, ta_ref, b_ref, c_ref, gu_ref, ga_ref, gb_ref, gc_ref,
         u_scr, gy_scr, x_scr, g_scr, gcarry, xcarry) = refs[2 * tp:]

        @pl.when(pl.program_id(1) == 0)
        def _():
            gcarry[...] = jnp.zeros_like(gcarry)
            ga_ref[...] = jnp.zeros_like(ga_ref)
            gb_ref[...] = jnp.zeros_like(gb_ref)
            gc_ref[...] = jnp.zeros_like(gc_ref)

        ub, gyb = [], []
        for t in range(tp):
            _permute_rows(u_refs[t], u_scr.at[t], nv)
            _permute_rows(gy_refs[t], gy_scr.at[t], nv)
            ub.append(u_scr[t].astype(MX))
            gyb.append(gy_scr[t].astype(MX))
        for t in range(tp):
            g_scr[t] = _dot(gyb[t], c_ref[t], NT)
            x_scr[t] = _dot(ub[t], b_ref[t])
            xcarry[t] = xb_ref[0, :, w2 * t:w2 * (t + 1)]
        for t in range(tp):
            _scan_chunk(g_scr.at[t], ta_ref.at[0, t], gcarry.at[t], nv, not rev)
        for t in range(tp):
            _scan_chunk(x_scr.at[t], ts_ref.at[0, t], xcarry.at[t], nv, rev,
                        acc=(g_scr.at[t], ga_ref.at[:, pl.ds(w2 * t, w2)]))
            gb16 = g_scr[t].astype(MX)
            gb_ref[t] += _dot(ub[t], gb16, TN)
            gc_ref[t] += _dot(x_scr[t].astype(MX), gyb[t], TN)
            _unpermute_rows(_dot(gb16, b_ref[t], NT), u_scr.at[t], nv)
            gu_ref[:, TILE_CH * t:TILE_CH * (t + 1)] = u_scr[t]

    tile3 = lambda a, b: pl.BlockSpec((tp, a, b), lambda j, i: (j, 0, 0))
    u_specs = [pl.BlockSpec((chunk, TILE_CH), lambda j, i, t=t: (ci(i), U_OFF // TILE_CH + tp * j + t))
               for t in range(tp)]
    gy_specs = [pl.BlockSpec((chunk, TILE_CH), lambda j, i, t=t: (ci(i), tp * j + t)) for t in range(tp)]
    outs, got = _call(
        body, grid=(SSM_TILES // tp, nc),
        in_specs=u_specs + gy_specs + [
                  pl.BlockSpec((1, 8, tp * w2), lambda j, i: (ci(i), 0, j)),
                  pl.BlockSpec((1, tp, 40 + nv, w2), lambda j, i: (var, j, 0, 0)),
                  pl.BlockSpec((1, tp, 40 + nv, w2), lambda j, i: (var + 1, j, 0, 0)),
                  tile3(TILE_CH, w2), tile3(w2, TILE_CH)],
        out_specs=[pl.BlockSpec((chunk, tp * TILE_CH), lambda j, i: (ci(i), j)),
                   pl.BlockSpec((8, tp * w2), lambda j, i: (0, j)),
                   tile3(TILE_CH, w2), tile3(w2, TILE_CH)],
        out_shape=[SDS((s, SSM_WIDTH), f32), SDS((8, SSM_TILES * w2), f32),
                   SDS((SSM_TILES, TILE_CH, w2), f32), SDS((SSM_TILES, w2, TILE_CH), f32)],
        scratch=[pltpu.VMEM((tp, chunk, TILE_CH), f32), pltpu.VMEM((tp, chunk, TILE_CH), f32),
                 pltpu.VMEM((tp, chunk, w2), f32), pltpu.VMEM((tp, chunk, w2), f32),
                 pltpu.VMEM((tp, 8, w2), f32), pltpu.VMEM((tp, 8, w2), f32)],
        args=(*([z] * tp), *([gy] * tp), xb, tabs, tabs, bmat, cmat), sem=("parallel", "arbitrary"),
        name=name, exchange=exchange)
    return (*outs, got)


GELU_K = math.sqrt(2.0 / math.pi)


def _gelu(y):
    return 0.5 * y * (1.0 + jnp.tanh(GELU_K * (y + 0.044715 * (y * y * y))))


def _gelu_grad(y):
    t = jnp.tanh(GELU_K * (y + 0.044715 * (y * y * y)))
    return 0.5 * (1.0 + t) + 0.5 * y * (1.0 - t * t) * (GELU_K * (1.0 + 3.0 * 0.044715 * (y * y)))


def glu_fwd(y_f, y_r, z, att, d_skip, w_glu, l, *, name, tm):
    s = z.shape[0]
    nblk, cb = w_glu.shape[1], w_glu.shape[3]

    def body(yf_ref, yr_ref, ua_ref, ub_ref, att_ref, d_ref, w_ref, y_ref, gg_ref, mix_ref):
        u = jnp.concatenate([ua_ref[...], ub_ref[...]], axis=1)
        y = d_ref[...] * u + yf_ref[...] + yr_ref[...]
        y_ref[...] = y
        yg = _gelu(y).astype(MX)
        for b in range(nblk):
            gg_ref[:, cb * b:cb * (b + 1)] = _dot(yg, w_ref[0, b])
        mix_ref[:, 0:ATT_WIDTH] = att_ref[...]
        mix_ref[:, ATT_WIDTH:] = (gg_ref[:, :SSM_WIDTH] * jax.nn.sigmoid(gg_ref[:, SSM_WIDTH:])).astype(MX)

    return pl.pallas_call(
        body, grid=(s // tm,),
        in_specs=[_row_spec(tm, SSM_WIDTH), _row_spec(tm, SSM_WIDTH),
                  pl.BlockSpec((tm, SSM_WIDTH // 2), lambda i: (i, U_OFF // (SSM_WIDTH // 2))),
                  pl.BlockSpec((tm, SSM_WIDTH // 2), lambda i: (i, U_OFF // (SSM_WIDTH // 2) + 1)),
                  _row_spec(tm, ATT_WIDTH), pl.BlockSpec((1, SSM_WIDTH), lambda i: (0, 0)), _layer_spec(w_glu, l)],
        out_specs=[_row_spec(tm, SSM_WIDTH), _row_spec(tm, 2 * SSM_WIDTH), _row_spec(tm, D_MODEL)],
        out_shape=[SDS((s, SSM_WIDTH), f32), SDS((s, 2 * SSM_WIDTH), f32), SDS((s, D_MODEL), MX)],
        compiler_params=_cp("parallel"), name=name)(y_f, y_r, z, z, att, d_skip.reshape(1, SSM_WIDTH), w_glu)


def glu_bwd(gmix, gg, ypre, w_glu, l, *, name, tm):
    s = gg.shape[0]
    nblk, cb = w_glu.shape[1], w_glu.shape[3]

    def body(gm_ref, gg_ref, y_ref, w_ref, ggg_ref, yg_ref, gy_ref):
        gs = gm_ref[...]
        val, gate = gg_ref[:, :SSM_WIDTH], gg_ref[:, SSM_WIDTH:]
        sg = jax.nn.sigmoid(gate)
        ggg_ref[:, :SSM_WIDTH] = (gs * sg).astype(MX)
        ggg_ref[:, SSM_WIDTH:] = (gs * val * sg * (1.0 - sg)).astype(MX)
        y = y_ref[...]
        yg_ref[...] = _gelu(y).astype(MX)
        gyg = _dot(ggg_ref[:, 0:cb], w_ref[0, 0], NT)
        for b in range(1, nblk):
            gyg = gyg + _dot(ggg_ref[:, cb * b:cb * (b + 1)], w_ref[0, b], NT)
        gy_ref[...] = gyg * _gelu_grad(y)

    return pl.pallas_call(
        body, grid=(s // tm,),
        in_specs=[pl.BlockSpec((tm, SSM_WIDTH), lambda i: (i, 1)), _row_spec(tm, 2 * SSM_WIDTH),
                  _row_spec(tm, SSM_WIDTH), _layer_spec(w_glu, l)],
        out_specs=[_row_spec(tm, 2 * SSM_WIDTH), _row_spec(tm, SSM_WIDTH), _row_spec(tm, SSM_WIDTH)],
        out_shape=[SDS((s, 2 * SSM_WIDTH), MX), SDS((s, SSM_WIDTH), MX), SDS((s, SSM_WIDTH), f32)],
        compiler_params=_cp("parallel"), name=name)(gmix, gg, ypre, w_glu)


def loss_grad(y, target, *, name, tm):
    s, d = y.shape

    def body(y_ref, t_ref, g_ref, g16_ref, l_ref):
        @pl.when(pl.program_id(0) == 0)
        def _():
            l_ref[...] = jnp.zeros_like(l_ref)

        e = y_ref[...] - t_ref[...]
        g = e * (1.0 / d)
        g_ref[...] = g
        g16_ref[...] = g.astype(MX)
        l_ref[...] += _rows8(e * e)

    row = pl.BlockSpec((tm, d), lambda i: (i, 0))
    return pl.pallas_call(
        body, grid=(s // tm,), in_specs=[row, row],
        out_specs=[row, row, pl.BlockSpec((8, d), lambda i: (0, 0))],
        out_shape=[SDS((s, d), f32), SDS((s, d), MX), SDS((8, d), f32)],
        compiler_params=_cp("arbitrary"), name=name)(y, target)


def _row_tile(rows, cols):
    tr = rows
    while tr * cols > 256 * 1024 and tr % 16 == 0:
        tr //= 2
    return tr


def _elementwise(fn, ins, n_out, *, name, out_dtype=f32):
    shape = ins[0].shape
    cols = shape[-1]
    ins2 = [a.reshape(-1, cols) for a in ins]
    rows = ins2[0].shape[0]
    tr = _row_tile(rows, cols)

    def body(*refs):
        outs = fn(*[r[...] for r in refs[:len(ins)]])
        for o_ref, o in zip(refs[len(ins):], outs):
            o_ref[...] = o.astype(out_dtype)

    spec = pl.BlockSpec((tr, cols), lambda i: (i, 0))
    outs = pl.pallas_call(
        body, grid=(rows // tr,), in_specs=[spec] * len(ins), out_specs=[spec] * n_out,
        out_shape=[SDS((rows, cols), out_dtype)] * n_out, compiler_params=_cp("parallel"), name=name)(*ins2)
    return [o.reshape(shape) for o in outs]


def _adamw_math(w, g, m, v):
    m = ADAM_B1 * m + (1.0 - ADAM_B1) * g
    v = ADAM_B2 * v + (1.0 - ADAM_B2) * (g * g)
    m_hat = m / (1.0 - ADAM_B1 ** ADAM_STEP)
    v_hat = v / (1.0 - ADAM_B2 ** ADAM_STEP)
    delta = -ADAM_LR * (m_hat / (jnp.sqrt(v_hat) + ADAM_EPS) + ADAM_WD * w)
    return delta, m, v


def adamw(w, g, m, v, *, name):
    return _elementwise(_adamw_math, [w, g, m, v], 3, name=name)


SMEM = pl.BlockSpec(memory_space=pltpu.SMEM)


def _core_index():
    return lax.axis_index("c").astype(jnp.int32).reshape(1)


def adamw_halves(w, own, sib, m, v, *, name):
    depth, r, cols = w.shape
    h = r // 2
    tr = _row_tile(h, cols)
    quad = lambda a: a.reshape(depth, 2, h, cols)

    def body(c_ref, w_ref, own_ref, sib_ref, m_ref, v_ref, g_ref, d_ref, mo_ref, vo_ref):
        g = jnp.where(pl.program_id(1) == c_ref[0], own_ref[0], sib_ref[0])
        g_ref[0, 0] = g
        d_ref[0, 0], mo_ref[0, 0], vo_ref[0, 0] = _adamw_math(w_ref[0, 0], g, m_ref[0, 0], v_ref[0, 0])

    full = pl.BlockSpec((1, 1, tr, cols), lambda l, j, i: (l, j, i, 0))
    part = pl.BlockSpec((1, tr, cols), lambda l, j, i: (l, i, 0))
    outs = pl.pallas_call(
        body, grid=(depth, 2, h // tr), in_specs=[SMEM, full, part, part, full, full], out_specs=[full] * 4,
        out_shape=[SDS((depth, 2, h, cols), f32)] * 4,
        compiler_params=_cp("parallel", "parallel", "parallel"), name=name)(
            _core_index(), quad(w), own, sib, quad(m), quad(v))
    return [o.reshape(depth, r, cols) for o in outs]


def add_own_half(g4, recv, *, name):
    _, _, h, cols = g4.shape
    tr = _row_tile(h, cols)

    def body(c_ref, g_ref, r_ref, o_ref):
        own = jnp.where(c_ref[0] == 0, g_ref[0, 0], g_ref[0, 1])
        o_ref[0] = (own + r_ref[0]).astype(WIRE)

    part = pl.BlockSpec((1, tr, cols), lambda s, i: (s, i, 0))
    return pl.pallas_call(
        body, grid=(4, h // tr),
        in_specs=[SMEM, pl.BlockSpec((1, 2, tr, cols), lambda s, i: (s, 0, i, 0)), part], out_specs=part,
        out_shape=SDS((4, h, cols), WIRE), compiler_params=_cp("parallel", "parallel"), name=name)(
            _core_index(), g4, recv)


def _chip_index():
    return (2 * lax.axis_index("x") + lax.axis_index("y")).astype(jnp.int32).reshape(1)


def sum_pieces(sums, got, *, name, into, layer):
    _, h, cols = sums.shape
    tr = _row_tile(h, cols)

    def body(me_ref, s_ref, g_ref, stack_ref, o_ref):
        del stack_ref
        own = s_ref[0]
        for s in range(1, 4):
            own = jnp.where(me_ref[0] == s, s_ref[s], own)
        o_ref[0] = ((own.astype(f32) + g_ref[0].astype(f32)) + g_ref[1].astype(f32)) + g_ref[2].astype(f32)

    return pl.pallas_call(
        body, grid=(h // tr,),
        in_specs=[SMEM, pl.BlockSpec((4, tr, cols), lambda i: (0, i, 0)),
                  pl.BlockSpec((3, tr, cols), lambda i: (0, i, 0)), ANY],
        out_specs=pl.BlockSpec((1, tr, cols), lambda i: (layer, i, 0)),
        out_shape=SDS(into.shape, f32), input_output_aliases={3: 0},
        compiler_params=_cp("parallel"), name=name)(_chip_index(), sums, got, into)


def sum4(a, *, name, into=None, layer=0):
    shape = a.shape[1:]
    cols = shape[-1]
    a2 = a.reshape(4, -1, cols)
    rows = a2.shape[1]
    tr = _row_tile(rows, cols)

    def body(*refs):
        a_ref, o_ref = refs[0], refs[-1]
        tot = ((a_ref[0].astype(f32) + a_ref[1].astype(f32)) + a_ref[2].astype(f32)) + a_ref[3].astype(f32)
        if into is None:
            o_ref[...] = tot
        else:
            o_ref[0] = tot

    in_spec = pl.BlockSpec((4, tr, cols), lambda i: (0, i, 0))
    if into is None:
        out = pl.pallas_call(
            body, grid=(rows // tr,), in_specs=[in_spec], out_specs=pl.BlockSpec((tr, cols), lambda i: (i, 0)),
            out_shape=SDS((rows, cols), f32), compiler_params=_cp("parallel"), name=name)(a2)
        return out.reshape(shape)
    stack = into.reshape(into.shape[0], rows, cols)
    out = pl.pallas_call(
        body, grid=(rows // tr,), in_specs=[in_spec, ANY],
        out_specs=pl.BlockSpec((1, tr, cols), lambda i: (layer, i, 0)),
        out_shape=SDS(stack.shape, f32), input_output_aliases={1: 0},
        compiler_params=_cp("parallel"), name=name)(a2, stack)
    return out.reshape(into.shape)


ANY = pl.BlockSpec(memory_space=pl.ANY)


def _chip_copies(ins, outs, send, recv, bcast):
    x, y, c = lax.axis_index("x"), lax.axis_index("y"), lax.axis_index("c")
    me = 2 * x + y
    copies = []
    for k in range(len(ins)):
        for j, (px, py) in enumerate(((1 - x, y), (x, 1 - y), (1 - x, 1 - y))):
            copies.append(pltpu.make_async_remote_copy(
                src_ref=ins[k] if bcast[k] else ins[k].at[2 * px + py],
                dst_ref=outs[k].at[me] if bcast[k] else outs[k].at[j],
                send_sem=send.at[4 * k + j], recv_sem=recv.at[4 * k + j],
                device_id=(px, py, c), device_id_type=MESH))
        if bcast[k]:
            copies.append(pltpu.make_async_remote_copy(
                src_ref=ins[k], dst_ref=outs[k].at[me], send_sem=send.at[4 * k + 3], recv_sem=recv.at[4 * k + 3],
                device_id=(x, y, 1 - c), device_id_type=MESH))
    return copies


def chip_exchange(arrs, bcast, *, name):
    n = len(arrs)

    def body(*refs):
        copies = _chip_copies(refs[:n], refs[n:2 * n], refs[2 * n], refs[2 * n + 1], bcast)
        for cp in copies:
            cp.start()
        for cp in copies:
            cp.wait()

    return pl.pallas_call(
        body, in_specs=[ANY] * n, out_specs=[ANY] * n,
        out_shape=[SDS((4,) + tuple(a.shape) if b else (3,) + tuple(a.shape[1:]), a.dtype)
                   for a, b in zip(arrs, bcast)],
        scratch_shapes=[pltpu.SemaphoreType.DMA((4 * n,)), pltpu.SemaphoreType.DMA((4 * n,))],
        name=name)(*arrs)


def gather_weights(shards, *, name):
    n = len(shards)
    hd = shards[0].shape[0] // 2

    def body(*refs):
        ins, outs = refs[:n], refs[n:2 * n]
        send, recv = refs[2 * n:]
        x, y, c = lax.axis_index("x"), lax.axis_index("y"), lax.axis_index("c")
        me = 2 * x + y
        chips = ((1 - x, y), (x, 1 - y), (1 - x, 1 - y))
        mine, theirs = pl.ds(c * hd, hd), pl.ds((1 - c) * hd, hd)

        def ici(k, j, src, dst):
            px, py = chips[j]
            return pltpu.make_async_remote_copy(src_ref=src, dst_ref=dst, send_sem=send.at[7 * k + j],
                                                recv_sem=recv.at[7 * k + j], device_id=(px, py, c),
                                                device_id_type=MESH)

        def d2d(k, j, src, dst):
            return pltpu.make_async_remote_copy(src_ref=src, dst_ref=dst, send_sem=send.at[7 * k + 3 + j],
                                                recv_sem=recv.at[7 * k + 3 + j], device_id=(x, y, 1 - c),
                                                device_id_type=MESH)

        own, sent = [], []
        for k in range(n):
            own.append(d2d(k, 3, ins[k], outs[k].at[:, me]))
            own[-1].start()
            for j in range(3):
                sent.append(ici(k, j, ins[k].at[mine], outs[k].at[mine, me]))
                sent[-1].start()
        for k in range(n):
            for j, (px, py) in enumerate(chips):
                landed = outs[k].at[mine, 2 * px + py]
                ici(k, j, landed, landed).wait_recv()
                sent.append(d2d(k, j, landed, landed))
                sent[-1].start()
        for k in range(n):
            for j, (px, py) in enumerate(chips):
                other = outs[k].at[theirs, 2 * px + py]
                d2d(k, j, other, other).wait_recv()
        for cp in sent:
            cp.wait_send()
        for cp in own:
            cp.wait()

    return pl.pallas_call(
        body, in_specs=[ANY] * n, out_specs=[ANY] * n,
        out_shape=[SDS((a.shape[0], 4) + tuple(a.shape[1:]), a.dtype) for a in shards],
        scratch_shapes=[pltpu.SemaphoreType.DMA((7 * n,)), pltpu.SemaphoreType.DMA((7 * n,))],
        name=name)(*shards)


def _sibling_copies(ins, outs, send, recv, half):
    x, y, c = lax.axis_index("x"), lax.axis_index("y"), lax.axis_index("c")
    return [pltpu.make_async_remote_copy(
        src_ref=ins[k].at[:, 1 - c] if half[k] else ins[k], dst_ref=outs[k], send_sem=send.at[k],
        recv_sem=recv.at[k], device_id=(x, y, 1 - c), device_id_type=MESH) for k in range(len(ins))]


def sibling_exchange(arrs, half, *, name):
    n = len(arrs)
    piece = [(a.shape[0],) + a.shape[2:] if h else a.shape for a, h in zip(arrs, half)]

    def body(*refs):
        copies = _sibling_copies(refs[:n], refs[n:2 * n], refs[2 * n], refs[2 * n + 1], half)
        for cp in copies:
            cp.start()
        for cp in copies:
            cp.wait()

    return pl.pallas_call(
        body, in_specs=[ANY] * n, out_specs=[ANY] * n,
        out_shape=[SDS(tuple(p), a.dtype) for p, a in zip(piece, arrs)],
        scratch_shapes=[pltpu.SemaphoreType.DMA((n,)), pltpu.SemaphoreType.DMA((n,))],
        name=name)(*arrs)


def ssm_discretize(lam_re, lam_im, log_dt, b_re, b_im, c_re, c_im):
    dt = jnp.exp(log_dt)[..., None]
    mag = jnp.exp(lam_re * dt)
    abr = mag * jnp.cos(lam_im * dt)
    abi = mag * jnp.sin(lam_im * dt)
    den = lam_re * lam_re + lam_im * lam_im
    zr = ((abr - 1.0) * lam_re + abi * lam_im) / den
    zi = (abi * lam_re - (abr - 1.0) * lam_im) / den
    bbr = zr[..., None] * b_re - zi[..., None] * b_im
    bbi = zr[..., None] * b_im + zi[..., None] * b_re
    eye = jnp.eye(8, dtype=f32)
    bb = jnp.stack([bbr, bbi], axis=1).reshape(2, 2, SSM_TILES, 8, SSM_STATE, SSM_GROUP)
    bmat = jnp.einsum('dqjgph,gk->djghqkp', bb, eye).reshape(2, SSM_TILES, TILE_CH, 2 * TILE_ST)
    cc = jnp.stack([c_re, -c_im], axis=1).reshape(2, 2, SSM_TILES, 8, SSM_GROUP, SSM_STATE)
    cmat = jnp.einsum('dqjghp,gk->djqkpgh', cc, eye).reshape(2, SSM_TILES, 2 * TILE_ST, TILE_CH)
    n = SSM_GROUPS * SSM_STATE
    return abr.reshape(2, n), abi.reshape(2, n), bmat, cmat


SCAN_REV = (False, True, True, False)


def scan_tables_all(ar, ai, *, nv, name):
    a8 = jnp.stack([ar[0], ai[0], ar[0], -ai[0], ar[1], ai[1], ar[1], -ai[1]])
    n_state = SSM_TILES * TILE_ST

    def body(a_ref, o_ref):
        row = lax.broadcasted_iota(jnp.int32, (8, n_state), 0)

        def put(i, rows, re, im):
            for t in range(SSM_TILES):
                o_ref[i, t, rows, 0:TILE_ST] = re[:, TILE_ST * t:TILE_ST * (t + 1)]
                o_ref[i, t, rows, TILE_ST:] = im[:, TILE_ST * t:TILE_ST * (t + 1)]

        for i, rev in enumerate(SCAN_REV):
            a_r, a_i = a_ref[2 * i:2 * i + 1, :], a_ref[2 * i + 1:2 * i + 2, :]
            pr, pi = a_r, a_i
            for v in range(nv):
                dst = 40 + (nv - 1 - v if rev else v)
                put(i, slice(dst, dst + 1), pr, pi)
                if v + 1 < nv:
                    pr, pi = _cmul(a_r, a_i, pr, pi)
            big = (pr, pi)
            put(i, slice(24, 32), jnp.broadcast_to(big[0], (8, n_state)), jnp.broadcast_to(big[1], (8, n_state)))
            put(i, slice(32, 40), jnp.broadcast_to(a_r, (8, n_state)), jnp.broadcast_to(a_i, (8, n_state)))
            for n, k in enumerate((1, 2, 4)):
                cond = (row <= 7 - k) if rev else (row >= k)
                put(i, slice(8 * n, 8 * n + 8), jnp.where(cond, big[0], 0.0), jnp.where(cond, big[1], 0.0))
                big = _cmul(*big, *big)

    return pl.pallas_call(
        body, out_shape=SDS((4, SSM_TILES, 40 + nv, 2 * TILE_ST), f32),
        compiler_params=pltpu.CompilerParams(vmem_limit_bytes=VMEM_LIMIT), name=name)(a8)


def _tile_a(ga):
    t = ga.sum(axis=0).reshape(SSM_TILES, 2, TILE_ST)
    return t[:, 0].reshape(-1), t[:, 1].reshape(-1)


SMALL = ('norm1', 'q_gain', 'k_gain', 'sink', 'lam_re', 'lam_im', 'log_dt', 'b_re', 'b_im', 'c_re', 'c_im',
         'd_skip', 'norm2')
BIG = ('w_in', 'w_glu', 'w_out', 'w_ff1', 'w_ff2')
WEIGHTS = ('norm1', 'w_in', 'q_gain', 'k_gain', 'sink', 'lam_re', 'lam_im', 'log_dt', 'b_re', 'b_im', 'c_re',
           'c_im', 'd_skip', 'w_glu', 'w_out', 'norm2', 'w_ff1', 'w_ff2')


def _chunk(s):
    return min(512, s)


HOSTS_FIRST = {'attn': (("own", 'w_glu'), ("own", 'w_out'), ("next", 'w_in'), ("next", 'w_glu'), ("next", 'w_out')),
               'ssm0': (("own", 'w_ff1'),), 'ssm1': (("own", 'w_ff2'),),
               'ff1': (("next", 'w_ff1'),), 'ff2': (("next", 'w_ff2'),)}
HOSTS_LATER = {'attn': (("next", 'w_in'), ("next", 'w_glu'), ("next", 'w_out')),
               'ssm0': (("next", 'w_ff1'),), 'ssm1': (("next", 'w_ff2'),), 'ff1': (), 'ff2': ()}


def layer_forward(l, x, p, wb, li, own=None, nxt=None):
    s = x.shape[0]
    tm = min(512, s)
    sv = {}
    wb = dict(wb)
    src = {"own": own or {}, "next": nxt or {}}
    plan = {h: [e for e in es if e[1] in src[e[0]]] for h, es in (HOSTS_FIRST if own else HOSTS_LATER).items()}
    fetched = {}

    def hosted(host):
        es = plan[host]
        return ("chips", [src[w][k] for w, k in es], [True] * len(es)) if es else None

    def landed(host, got):
        for (w, k), g4 in zip(plan[host], got):
            if w == "own":
                wb[k] = layout_one(k, g4[None])
            else:
                fetched[k] = g4

    h1, z, _ = norm_mm(x, p['norm1'], wb['w_in'], li, relu2=False, name=f"l{l}_in", tm=tm)
    eq, ek = head_mean_matrix(ATT_WIDTH), head_mean_matrix(KV_WIDTH)
    qn, kv = qk_prep(z, p['q_gain'], p['k_gain'], eq, ek, name=f"l{l}_qk", tm=tm)
    att, got = attn_fwd(qn, kv, p['sink'], name=f"l{l}_attn", exchange=hosted('attn'))
    landed('attn', got)
    sv.update(qn=qn, kv=kv, eq=eq, ek=ek)
    (ar, ai, bmat, cmat), disc_vjp = jax.vjp(
        ssm_discretize, p['lam_re'], p['lam_im'], p['log_dt'], p['b_re'], p['b_im'], p['c_re'], p['c_im'])
    bmat16, cmat16 = bmat.astype(MX), cmat.astype(MX)
    ys, xbs = [], []
    tabs = scan_tables_all(ar, ai, nv=_chunk(s) // 8, name=f"l{l}_tabs")
    for d, rev in enumerate((False, True)):
        y_d, xb_d, got = ssm_fwd(z, tabs, bmat16[d], cmat16[d], rev=rev, name=f"l{l}_ssm{d}", chunk=_chunk(s),
                                 exchange=hosted(f'ssm{d}'))
        landed(f'ssm{d}', got)
        ys.append(y_d)
        xbs.append(xb_d)
    ypre, gg, mix = glu_fwd(ys[0], ys[1], z, att, p['d_skip'], wb['w_glu'], li, name=f"l{l}_glu", tm=tm)
    x1, _ = mm_res(mix, wb['w_out'], li, x, name=f"l{l}_out", tm=tm)
    h2, a2, got = norm_mm(x1, p['norm2'], wb['w_ff1'], li, relu2=True, name=f"l{l}_ff1", tm=tm, exchange=hosted('ff1'))
    landed('ff1', got)
    x2, got = mm_res(a2, wb['w_ff2'], li, x1, name=f"l{l}_ff2", tm=tm, exchange=hosted('ff2'))
    landed('ff2', got)
    sv.update(x=x, h1=h1, z=z, xbs=xbs, tabs=tabs, bmat16=bmat16, cmat16=cmat16, disc_vjp=disc_vjp,
              ypre=ypre, gg=gg, mix=mix, x1=x1, h2=h2, a2=a2, wb=wb)
    return x2, sv, fetched


def layer_backward(l, gx2, gx2h, p, wb, li, sv, pend=None):
    s = gx2.shape[0]
    tm = min(512, s)
    ts = min(1024, s)
    g = {}
    extra = [pend[2]] if pend and pend[2] is not None else []
    gf, got = mm_nt(gx2h, wb['w_ff2'], li, name=f"l{l}_bff2", tm=tm, a2=sv['a2'],
                    exchange=pend and ("sibling", pend[1] + extra, [True] * len(pend[1]) + [False] * len(extra)))
    sums = pend and [add_own_half(a, b, name=f"l{l}_radd_{k}") for k, a, b in zip(BIG, pend[1], got)]
    if extra:
        sums.append(_elementwise(lambda a, b: (a + b,), [extra[0], got[-1]], 1, name=f"l{l}_radd_small")[0])
    g['w_ff2'] = mm_tn(sv['a2'], gx2h, name=f"l{l}_wff2", tk=1024, tn=1024, ts=ts).reshape(4, D_FF // 4, D_MODEL)
    gx1, gx1h, gn2 = mm_nt_norm(gf, wb['w_ff1'], li, sv['x1'], p['norm2'], gx2, name=f"l{l}_bff1", tm=tm)
    g['norm2'] = gn2.sum(axis=0)
    g['w_ff1'] = mm_tn(sv['h2'], gf, name=f"l{l}_wff1", tk=1024, tn=1024, ts=ts, chip_major=True)
    gmix, _ = mm_nt(gx1h, wb['w_out'], li, name=f"l{l}_bout", tm=tm)
    g['w_out'] = mm_tn(sv['mix'], gx1h, name=f"l{l}_wout", tk=1024, tn=1024, ts=ts).reshape(4, D_MODEL // 4, D_MODEL)
    ggg, yg, gy = glu_bwd(gmix, sv['gg'], sv['ypre'], wb['w_glu'], li, name=f"l{l}_bglu", tm=tm)
    g['w_glu'] = mm_tn(yg, ggg, name=f"l{l}_wglu", tk=512, tn=256, ts=ts, chip_major=True)
    gus, gas, gbs, gcs = [], [], [], []
    for d, rev in enumerate((False, True)):
        gu_d, ga_d, gb_d, gc_d, got = ssm_bwd(
            sv['z'], gy, sv['xbs'][d], sv['tabs'], sv['bmat16'][d], sv['cmat16'][d], rev=rev,
            name=f"l{l}_bssm{d}", chunk=_chunk(s),
            exchange=(pend and d == 0) and ("chips", sums, [False] * len(BIG) + [True] * len(extra)) or None)
        if pend and d == 0:
            pend[0](sums, got)
        gus.append(gu_d)
        gas.append(_tile_a(ga_d))
        gbs.append(gb_d)
        gcs.append(gc_d)
    gar = jnp.stack([gas[0][0], gas[1][0]])
    gai = jnp.stack([gas[0][1], gas[1][1]])
    (g['lam_re'], g['lam_im'], g['log_dt'], g['b_re'], g['b_im'], g['c_re'], g['c_im']) = sv['disc_vjp'](
        (gar, gai, jnp.stack(gbs), jnp.stack(gcs)))
    gqs, dkv, gsk = attn_bwd(sv['qn'], sv['kv'], gmix, p['sink'], name=f"l{l}_battn")
    g['sink'] = gsk[:, 0]
    gz, gqg, gkg, gd = gz_assemble(gqs, dkv, sv['z'], p['q_gain'], p['k_gain'], sv['eq'], sv['ek'], gus[0], gus[1],
                                   gy, p['d_skip'], name=f"l{l}_gz")
    g['q_gain'] = gqg.sum(axis=0).reshape(ATT_HEADS, HEAD_DIM).sum(axis=0)
    g['k_gain'] = gkg.sum(axis=0).reshape(KV_HEADS, HEAD_DIM).sum(axis=0)
    g['d_skip'] = gd.sum(axis=0)
    gx, gxh, gn1 = mm_nt_norm(gz, wb['w_in'], li, sv['x'], p['norm1'], gx1, name=f"l{l}_bin", tm=tm)
    g['norm1'] = gn1.sum(axis=0)
    gw_in = mm_tn(sv['h1'], gz, name=f"l{l}_win", tk=1024, tn=640, ts=ts)
    g['w_in'] = gw_in.reshape(D_MODEL, 4, IN_WIDTH // 4).transpose(1, 0, 2)
    return gx, gxh, g


def layout_one(k, g):
    depth = g.shape[0]
    if k == 'w_in':
        return g.transpose(0, 2, 1, 3).reshape(depth, D_MODEL, IN_WIDTH)
    if k == 'w_out':
        return g.reshape(depth, D_MODEL, D_MODEL)
    if k == 'w_ff2':
        return g.reshape(depth, D_FF, D_MODEL)
    return g


def stack_layouts(gathered):
    return {k: layout_one(k, g) for k, g in gathered.items()}


def local_step(x, target, small, wb):
    depth = wb['w_in'].shape[0]
    saves = []
    for l in range(depth):
        x, sv, _ = layer_forward(l, x, {k: small[k][l] for k in SMALL}, wb, l)
        saves.append(sv)
    gx, gxh, lparts = loss_grad(x, target, name="loss", tm=min(512, x.shape[0]))
    grads = [None] * depth
    for l in reversed(range(depth)):
        gx, gxh, grads[l] = layer_backward(l, gx, gxh, {k: small[k][l] for k in SMALL}, wb, l, saves[l])
    return lparts, gx, grads


def reduce_pieces(g):
    return [g[k].reshape(4, 2, g[k].shape[1] // 2, g[k].shape[2]) for k in BIG]


def reduce_chips(l, sums, got, stacks):
    return {k: sum_pieces(a, b, name=f"l{l}_rsum_{k}", into=stacks[k], layer=l) for k, a, b in zip(BIG, sums, got)}


def gather_first(shards):
    halves = [a.reshape(2, a.shape[0] // 2, a.shape[1]) for a in shards]
    got = gather_weights(halves, name="gather_first")
    return [a.transpose(1, 0, 2, 3).reshape(4, 2 * a.shape[2], a.shape[3]) for a in got]


def reduce_small(packed):
    got = sibling_exchange([packed], [False], name="small_rsib")
    pair = _elementwise(lambda a, b: (a + b,), [packed, got[0]], 1, name="small_radd")[0]
    got = chip_exchange([pair], [True], name="small_rchips")
    return sum4(got[0], name="small_rsum")


def _pack_small(tree):
    parts = []
    for k in SMALL:
        flat = tree[k].reshape(-1)
        parts.append(jnp.pad(flat, (0, (-flat.shape[0]) % 1024)).reshape(-1, 128))
    return jnp.concatenate(parts, axis=0)


def _unpack_small(packed, like):
    out, row = {}, 0
    for k in SMALL:
        n = like[k].size
        rows = -(-n // 1024) * 8
        out[k] = packed[row:row + rows].reshape(-1)[:n].reshape(like[k].shape)
        row += rows
    return out


def kernel(x, norm1, w_in, q_gain, k_gain, sink, lam_re, lam_im, log_dt, b_re, b_im, c_re, c_im, d_skip, w_glu, w_out, norm2, w_ff1, w_ff2, loss_target, m_norm1, m_w_in, m_q_gain, m_k_gain, m_sink, m_lam_re, m_lam_im, m_log_dt, m_b_re, m_b_im, m_c_re, m_c_im, m_d_skip, m_w_glu, m_w_out, m_norm2, m_w_ff1, m_w_ff2, v_norm1, v_w_in, v_q_gain, v_k_gain, v_sink, v_lam_re, v_lam_im, v_log_dt, v_b_re, v_b_im, v_c_re, v_c_im, v_d_skip, v_w_glu, v_w_out, v_norm2, v_w_ff1, v_w_ff2):
    w = dict(norm1=norm1, w_in=w_in, q_gain=q_gain, k_gain=k_gain, sink=sink, lam_re=lam_re, lam_im=lam_im,
             log_dt=log_dt, b_re=b_re, b_im=b_im, c_re=c_re, c_im=c_im, d_skip=d_skip, w_glu=w_glu, w_out=w_out,
             norm2=norm2, w_ff1=w_ff1, w_ff2=w_ff2)
    m = dict(norm1=m_norm1, w_in=m_w_in, q_gain=m_q_gain, k_gain=m_k_gain, sink=m_sink, lam_re=m_lam_re,
             lam_im=m_lam_im, log_dt=m_log_dt, b_re=m_b_re, b_im=m_b_im, c_re=m_c_re, c_im=m_c_im,
             d_skip=m_d_skip, w_glu=m_w_glu, w_out=m_w_out, norm2=m_norm2, w_ff1=m_w_ff1, w_ff2=m_w_ff2)
    v = dict(norm1=v_norm1, w_in=v_w_in, q_gain=v_q_gain, k_gain=v_k_gain, sink=v_sink, lam_re=v_lam_re,
             lam_im=v_lam_im, log_dt=v_log_dt, b_re=v_b_re, b_im=v_b_im, c_re=v_c_re, c_im=v_c_im,
             d_skip=v_d_skip, w_glu=v_w_glu, w_out=v_w_out, norm2=v_norm2, w_ff1=v_w_ff1, w_ff2=v_w_ff2)
    depth = w_in.shape[0]

    shards = {k: w[k].astype(WIRE) for k in BIG}
    small = {k: w[k] for k in SMALL}
    stacks = [{k: jnp.zeros((depth, w[k].shape[1] // 2, w[k].shape[2]), f32) for k in BIG}]

    xs = x[0]
    gathered = {'w_in': gather_first([shards['w_in'][0]])[0]}
    saves = []
    for l in range(depth):
        wb = stack_layouts({k: g[None] for k, g in gathered.items()})
        own = {k: shards[k][l] for k in BIG if k not in gathered}
        nxt = {k: shards[k][l + 1] for k in BIG} if l + 1 < depth else None
        xs, sv, gathered = layer_forward(l, xs, {k: small[k][l] for k in SMALL}, wb, 0, own, nxt)
        saves.append(sv)
    gx, gxh, lparts = loss_grad(xs, loss_target[0], name="loss", tm=min(512, xs.shape[0]))
    loss = lax.psum(0.5 * jnp.sum(lparts) / D_MODEL, ("x", "y", "c"))

    later_small = []

    def finisher(l):
        def finish(sums, got):
            stacks[0] = reduce_chips(l, sums[:len(BIG)], got[:len(BIG)], stacks[0])
            if len(got) > len(BIG):
                later_small.append(sum4(got[-1], name="later_small_rsum"))
        return finish

    grads, pend = [None] * depth, None
    for l in reversed(range(depth)):
        gx, gxh, g = layer_backward(l, gx, gxh, {k: small[k][l] for k in SMALL}, saves[l]['wb'], 0, saves[l], pend)
        grads[l] = {k: g[k] for k in SMALL}
        packed = _pack_small({k: jnp.stack([grads[j][k] for j in range(1, depth)]) for k in SMALL}) if l == 1 else None
        pend = (finisher(l), reduce_pieces(g), packed)
    got = sibling_exchange(pend[1], [True] * len(BIG), name="last_rsib")
    sums = [add_own_half(a, b, name=f"last_radd_{k}") for k, a, b in zip(BIG, pend[1], got)]
    pend[0](sums, chip_exchange(sums, [False] * len(BIG), name="last_rchips"))

    sib = sibling_exchange([stacks[0][k] for k in BIG], [False] * len(BIG), name="reduce_back")
    first_small = reduce_small(_pack_small({k: grads[0][k][None] for k in SMALL}))
    like = {k: w[k] for k in SMALL}
    g_first = _unpack_small(first_small, {k: w[k][:1] for k in SMALL})
    g_later = _unpack_small(later_small[0], {k: w[k][1:] for k in SMALL})
    gfull = {k: jnp.concatenate([g_first[k], g_later[k]], axis=0) for k in SMALL}
    gsmall = _pack_small(gfull)

    delta, new_m, new_v = {}, {}, {}
    for k, sib_k in zip(BIG, sib):
        gfull[k], delta[k], new_m[k], new_v[k] = adamw_halves(w[k], stacks[0][k], sib_k, m[k], v[k],
                                                              name=f"adamw_{k}")
    ds, ms, vs = adamw(_pack_small(like), gsmall, _pack_small({k: m[k] for k in SMALL}),
                       _pack_small({k: v[k] for k in SMALL}), name="adamw_small")
    delta.update(_unpack_small(ds, like))
    new_m.update(_unpack_small(ms, like))
    new_v.update(_unpack_small(vs, like))

    return (loss, gx[None], *[gfull[k] for k in WEIGHTS], *[delta[k] for k in WEIGHTS],
            *[new_m[k] for k in WEIGHTS], *[new_v[k] for k in WEIGHTS])
```

```python
import functools
import math

import jax
import jax.numpy as jnp
from jax import lax
from jax.experimental import pallas as pl
from jax.experimental.pallas import tpu as pltpu

f32 = jnp.float32
MX = jnp.bfloat16
WIRE = jnp.bfloat16
SDS = jax.ShapeDtypeStruct

D_MODEL = 1024
DEPTH = 4
ATT_HEADS = 8
KV_HEADS = 2
GQA = ATT_HEADS // KV_HEADS
HEAD_DIM = 64
ATT_WIDTH = ATT_HEADS * HEAD_DIM
KV_WIDTH = KV_HEADS * HEAD_DIM
BLOCK = 128
SSM_WIDTH = 512
SSM_GROUP = 16
SSM_GROUPS = 32
SSM_STATE = 64
SSM_TILES = 4
TILE_CH = SSM_WIDTH // SSM_TILES
TILE_ST = SSM_GROUPS * SSM_STATE // SSM_TILES
TILES_PER_STEP = 2
IN_WIDTH = ATT_WIDTH + 2 * KV_WIDTH + SSM_WIDTH
U_OFF = ATT_WIDTH + 2 * KV_WIDTH
D_FF = 4096
EPS = 1e-6
NEG = float(jnp.finfo(jnp.float32).min)
SLOPES = tuple(2.0 ** (-8.0 * (h + 1) / ATT_HEADS) for h in range(ATT_HEADS))

ADAM_LR, ADAM_B1, ADAM_B2, ADAM_EPS, ADAM_WD, ADAM_STEP = 0.001, 0.9, 0.999, 1e-08, 0.01, 10

VMEM_LIMIT = 48 * 1024 * 1024
MESH = pl.DeviceIdType.MESH

NT = (((1,), (1,)), ((), ()))
TN = (((0,), (0,)), ((), ()))


def _cp(*sem):
    return pltpu.CompilerParams(dimension_semantics=sem, vmem_limit_bytes=VMEM_LIMIT)


def _dot(a, b, dims=None):
    if dims is None:
        return jnp.dot(a, b, preferred_element_type=f32)
    return lax.dot_general(a, b, dims, preferred_element_type=f32)


def _rows8(v):
    return v.reshape(v.shape[0] // 8, 8, v.shape[1]).sum(axis=0)


def _layer_spec(w, l):
    nd = w.ndim
    return pl.BlockSpec((1,) + tuple(w.shape[1:]), lambda i: (l,) + (0,) * (nd - 1))


def _row_spec(tm, width):
    return pl.BlockSpec((tm, width), lambda i: (i, 0))


def _call(body, *, grid, in_specs, out_specs, out_shape, args, sem, name, scratch=(), exchange=None):
    n_in, n_out, n_scr = len(in_specs), len(out_specs), len(scratch)
    if exchange is None:
        res = pl.pallas_call(body, grid=grid, in_specs=in_specs, out_specs=out_specs, out_shape=out_shape,
                             scratch_shapes=list(scratch), compiler_params=_cp(*sem), name=name)(*args)
        return list(res), []
    kind, arrs, flags = exchange
    nx = len(arrs)
    if kind == "chips":
        make, nsem = _chip_copies, 4 * nx
        got = [SDS((4,) + tuple(a.shape) if b else (3,) + tuple(a.shape[1:]), a.dtype) for a, b in zip(arrs, flags)]
    else:
        make, nsem = _sibling_copies, nx
        got = [SDS((a.shape[0],) + tuple(a.shape[2:]) if h else tuple(a.shape), a.dtype)
               for a, h in zip(arrs, flags)]

    def hosted(*refs):
        ins, xin = refs[:n_in], refs[n_in:n_in + nx]
        outs = refs[n_in + nx:n_in + nx + n_out]
        xout = refs[n_in + nx + n_out:n_in + 2 * nx + n_out]
        scr = refs[n_in + 2 * nx + n_out:]
        copies = make(xin, xout, scr[n_scr], scr[n_scr + 1], flags)
        first = functools.reduce(jnp.logical_and, [pl.program_id(d) == 0 for d in range(len(grid))])
        last = functools.reduce(jnp.logical_and, [pl.program_id(d) == grid[d] - 1 for d in range(len(grid))])

        @pl.when(first)
        def _():
            for cp in copies:
                cp.start()

        body(*ins, *outs, *scr[:n_scr])

        @pl.when(last)
        def _():
            for cp in copies:
                cp.wait()

    res = pl.pallas_call(
        hosted, grid=grid, in_specs=list(in_specs) + [ANY] * nx, out_specs=list(out_specs) + [ANY] * nx,
        out_shape=list(out_shape) + got,
        scratch_shapes=list(scratch) + [pltpu.SemaphoreType.DMA((nsem,)), pltpu.SemaphoreType.DMA((nsem,))],
        compiler_params=_cp(*["arbitrary"] * len(grid)), name=name)(*args, *arrs)
    return list(res[:n_out]), list(res[n_out:])


def norm_mm(x, gain, w, l, *, relu2, name, tm, exchange=None):
    s, d = x.shape
    if relu2:
        nblk, cb = w.shape[1], w.shape[3]
        n = nblk * cb
    else:
        n = w.shape[2]

    def body(x_ref, g_ref, w_ref, h_ref, y_ref):
        xf = x_ref[...]
        r = lax.rsqrt(jnp.mean(xf * xf, axis=-1, keepdims=True) + EPS)
        h = (xf * r * g_ref[...]).astype(MX)
        h_ref[...] = h
        if relu2:
            for b in range(nblk):
                f = jnp.maximum(_dot(h, w_ref[0, b]), 0.0)
                y_ref[:, cb * b:cb * (b + 1)] = (f * f).astype(MX)
        else:
            y_ref[...] = _dot(h, w_ref[0])

    (h, y), got = _call(
        body, grid=(s // tm,),
        in_specs=[_row_spec(tm, d), pl.BlockSpec((1, d), lambda i: (0, 0)), _layer_spec(w, l)],
        out_specs=[_row_spec(tm, d), _row_spec(tm, n)],
        out_shape=[SDS((s, d), MX), SDS((s, n), MX if relu2 else f32)],
        args=(x, gain.reshape(1, d), w), sem=("parallel",), name=name, exchange=exchange)
    return h, y, got


def mm_res(a, w, l, res, *, name, tm, exchange=None):
    s, k = a.shape
    n = w.shape[2]

    def body(a_ref, w_ref, r_ref, o_ref):
        o_ref[...] = r_ref[...] + _dot(a_ref[...], w_ref[0])

    (out,), got = _call(
        body, grid=(s // tm,), in_specs=[_row_spec(tm, k), _layer_spec(w, l), _row_spec(tm, n)],
        out_specs=[_row_spec(tm, n)], out_shape=[SDS((s, n), f32)], args=(a, w, res), sem=("parallel",),
        name=name, exchange=exchange)
    return out, got


def mm_nt(gy, w, l, *, name, tm, a2=None, exchange=None):
    s, n = gy.shape
    k = w.shape[1]
    kb = min(k, 1024)

    def body(*refs):
        g_ref, w_ref, o_ref = refs[0], refs[1], refs[-1]
        g = g_ref[...]
        for b in range(k // kb):
            cols = slice(kb * b, kb * (b + 1))
            acc = _dot(g, w_ref[0, cols, :], NT)
            if a2 is not None:
                acc = acc * (2.0 * jnp.sqrt(refs[2][:, cols].astype(f32)))
            o_ref[:, cols] = acc.astype(o_ref.dtype)

    in_specs = [_row_spec(tm, n), _layer_spec(w, l)]
    args = [gy, w]
    if a2 is not None:
        in_specs.append(_row_spec(tm, k))
        args.append(a2)
    (out,), got = _call(body, grid=(s // tm,), in_specs=in_specs, out_specs=[_row_spec(tm, k)],
                        out_shape=[SDS((s, k), f32 if a2 is None else MX)], args=args, sem=("parallel",),
                        name=name, exchange=exchange)
    return out, got


def mm_nt_norm(gy, w, l, x, gain, res, *, name, tm):
    s, n = gy.shape
    d = x.shape[1]

    def body(g_ref, w_ref, x_ref, gn_ref, r_ref, o_ref, o16_ref, gg_ref):
        @pl.when(pl.program_id(0) == 0)
        def _():
            gg_ref[...] = jnp.zeros_like(gg_ref)

        if w.ndim == 3:
            gh = _dot(g_ref[...], w_ref[0], NT)
        else:
            cb = w.shape[3]
            gh = _dot(g_ref[:, 0:cb], w_ref[0, 0], NT)
            for b in range(1, w.shape[1]):
                gh = gh + _dot(g_ref[:, cb * b:cb * (b + 1)], w_ref[0, b], NT)
        xf = x_ref[...]
        r = lax.rsqrt(jnp.mean(xf * xf, axis=-1, keepdims=True) + EPS)
        xh = xf * r
        t = gh * gn_ref[...]
        gx = r_ref[...] + r * (t - xh * jnp.mean(t * xh, axis=-1, keepdims=True))
        o_ref[...] = gx
        o16_ref[...] = gx.astype(MX)
        gg_ref[...] += _rows8(gh * xh)

    return pl.pallas_call(
        body, grid=(s // tm,),
        in_specs=[_row_spec(tm, n), _layer_spec(w, l), _row_spec(tm, d), pl.BlockSpec((1, d), lambda i: (0, 0)),
                  _row_spec(tm, d)],
        out_specs=[_row_spec(tm, d), _row_spec(tm, d), pl.BlockSpec((8, d), lambda i: (0, 0))],
        out_shape=[SDS((s, d), f32), SDS((s, d), MX), SDS((8, d), f32)],
        compiler_params=_cp("arbitrary"), name=name)(gy, w, x, gain.reshape(1, d), res)


def mm_tn(xa, gy, *, name, tk, tn, ts, chip_major=False):
    s, k = xa.shape
    n = gy.shape[1]

    def body(x_ref, g_ref, o_ref):
        @pl.when(pl.program_id(2) == 0)
        def _():
            o_ref[...] = jnp.zeros_like(o_ref)

        acc = _dot(x_ref[...], g_ref[...], TN)
        if chip_major:
            o_ref[0] += acc
        else:
            o_ref[...] += acc

    if chip_major:
        out_spec = pl.BlockSpec((1, tk, tn), lambda a, b, c: (b, a, 0))
        out_shape = SDS((n // tn, k, tn), f32)
    else:
        out_spec = pl.BlockSpec((tk, tn), lambda a, b, c: (a, b))
        out_shape = SDS((k, n), f32)
    return pl.pallas_call(
        body, grid=(k // tk, n // tn, s // ts),
        in_specs=[pl.BlockSpec((ts, tk), lambda a, b, c: (c, a)), pl.BlockSpec((ts, tn), lambda a, b, c: (c, b))],
        out_specs=out_spec, out_shape=out_shape,
        compiler_params=_cp("parallel", "parallel", "arbitrary"), name=name)(xa, gy)


def head_mean_matrix(width):
    return jnp.kron(jnp.eye(width // HEAD_DIM, dtype=f32), jnp.full((HEAD_DIM, HEAD_DIM), 1.0 / HEAD_DIM, f32)).astype(MX)


def _head_mean(t, e_ref):
    hi = t.astype(MX)
    lo = (t - hi.astype(f32)).astype(MX)
    return _dot(hi, e_ref[...]) + _dot(lo, e_ref[...])


def qk_prep(z, q_gain, k_gain, eq, ek, *, name, tm):
    s = z.shape[0]

    def body(z_ref, qg_ref, kg_ref, eq_ref, ek_ref, q_ref, kv_ref):
        q = z_ref[:, 0:ATT_WIDTH]
        r = lax.rsqrt(_head_mean(q * q, eq_ref) + EPS)
        q_ref[...] = ((q * r * qg_ref[...]) * 0.125).astype(MX)
        k = z_ref[:, ATT_WIDTH:ATT_WIDTH + KV_WIDTH]
        r = lax.rsqrt(_head_mean(k * k, ek_ref) + EPS)
        kv_ref[:, 0:KV_WIDTH] = (k * r * kg_ref[...]).astype(MX)
        kv_ref[:, KV_WIDTH:] = z_ref[:, ATT_WIDTH + KV_WIDTH:U_OFF].astype(MX)

    const = lambda a: pl.BlockSpec(a.shape, lambda i: (0, 0))
    qg = jnp.tile(q_gain.reshape(1, HEAD_DIM), (1, ATT_HEADS))
    kg = jnp.tile(k_gain.reshape(1, HEAD_DIM), (1, KV_HEADS))
    return pl.pallas_call(
        body, grid=(s // tm,), in_specs=[_row_spec(tm, IN_WIDTH), const(qg), const(kg), const(eq), const(ek)],
        out_specs=[_row_spec(tm, ATT_WIDTH), _row_spec(tm, 2 * KV_WIDTH)],
        out_shape=[SDS((s, ATT_WIDTH), MX), SDS((s, 2 * KV_WIDTH), MX)],
        compiler_params=_cp("parallel"), name=name)(z, qg, kg, eq, ek)


def _attn_mask(i, nb):
    row = lax.broadcasted_iota(jnp.int32, (GQA * BLOCK, 3 * BLOCK), 0) & (BLOCK - 1)
    col = lax.broadcasted_iota(jnp.int32, (GQA * BLOCK, 3 * BLOCK), 1)
    dist = jnp.abs(row - col + BLOCK)
    valid = (dist <= BLOCK) & ((col >= BLOCK) | (i >= 1)) & ((col < 2 * BLOCK) | (i <= nb - 2))
    return dist.astype(f32), valid


def _attn_specs(nb):
    return [pl.BlockSpec((BLOCK, ATT_WIDTH), lambda i: (i, 0)),
            pl.BlockSpec((BLOCK, 2 * KV_WIDTH), lambda i: (jnp.maximum(i - 1, 0), 0)),
            pl.BlockSpec((BLOCK, 2 * KV_WIDTH), lambda i: (i, 0)),
            pl.BlockSpec((BLOCK, 2 * KV_WIDTH), lambda i: (jnp.minimum(i + 1, nb - 1), 0)),
            pl.BlockSpec(memory_space=pltpu.SMEM)]


def _attn_probs(sc, kvh, distf, valid, sink_ref):
    row = lax.broadcasted_iota(jnp.int32, (GQA * BLOCK, 1), 0)
    slope = jnp.full((GQA * BLOCK, 1), SLOPES[GQA * kvh], f32)
    sk = jnp.full((GQA * BLOCK, 1), sink_ref[GQA * kvh], f32)
    for j in range(1, GQA):
        slope = jnp.where(row >= BLOCK * j, SLOPES[GQA * kvh + j], slope)
        sk = jnp.where(row >= BLOCK * j, sink_ref[GQA * kvh + j], sk)
    sg = jnp.where(valid, sc - slope * distf, NEG)
    m = jnp.maximum(jnp.max(sg, axis=-1, keepdims=True), sk)
    e = jnp.exp(sg - m)
    es = jnp.exp(sk - m)
    inv = 1.0 / (jnp.sum(e, axis=-1, keepdims=True) + es)
    return e * inv, es * inv


def _stack_heads(ref, kvh):
    return jnp.concatenate([ref[:, HEAD_DIM * (GQA * kvh + g):HEAD_DIM * (GQA * kvh + g + 1)] for g in range(GQA)],
                           axis=0)


def attn_fwd(qn, kv, sink, *, name, exchange=None):
    s = qn.shape[0]
    nb = s // BLOCK

    def body(q_ref, kp_ref, kc_ref, kn_ref, sink_ref, o_ref):
        i = pl.program_id(0)
        distf, valid = _attn_mask(i, nb)
        kv3 = jnp.concatenate([kp_ref[...], kc_ref[...], kn_ref[...]], axis=0)
        for kvh in range(KV_HEADS):
            kn = kv3[:, HEAD_DIM * kvh:HEAD_DIM * (kvh + 1)]
            vh = kv3[:, KV_WIDTH + HEAD_DIM * kvh:KV_WIDTH + HEAD_DIM * (kvh + 1)]
            sc = _dot(_stack_heads(q_ref, kvh), kn, NT)
            p, _ = _attn_probs(sc, kvh, distf, valid, sink_ref)
            o = _dot(p.astype(MX), vh)
            for g in range(GQA):
                h = GQA * kvh + g
                o_ref[:, HEAD_DIM * h:HEAD_DIM * (h + 1)] = o[BLOCK * g:BLOCK * (g + 1)].astype(o_ref.dtype)

    (out,), got = _call(body, grid=(nb,), in_specs=_attn_specs(nb),
                        out_specs=[pl.BlockSpec((BLOCK, ATT_WIDTH), lambda i: (i, 0))],
                        out_shape=[SDS((s, ATT_WIDTH), MX)], args=(qn, kv, kv, kv, sink), sem=("parallel",),
                        name=name, exchange=exchange)
    return out, got


def attn_bwd(qn, kv, gmix, sink, *, name):
    s = qn.shape[0]
    nb = s // BLOCK

    def body(q_ref, kp_ref, kc_ref, kn_ref, sink_ref, go_ref, gq_ref, dkv_ref, gs_ref):
        i = pl.program_id(0)

        @pl.when(i == 0)
        def _():
            gs_ref[...] = jnp.zeros_like(gs_ref)

        distf, valid = _attn_mask(i, nb)
        kv3 = jnp.concatenate([kp_ref[...], kc_ref[...], kn_ref[...]], axis=0)
        for kvh in range(KV_HEADS):
            kn = kv3[:, HEAD_DIM * kvh:HEAD_DIM * (kvh + 1)]
            vh = kv3[:, KV_WIDTH + HEAD_DIM * kvh:KV_WIDTH + HEAD_DIM * (kvh + 1)]
            qs = _stack_heads(q_ref, kvh)
            dos = _stack_heads(go_ref, kvh).astype(MX)
            p, psink = _attn_probs(_dot(qs, kn, NT), kvh, distf, valid, sink_ref)
            dp = _dot(dos, vh, NT)
            delta = jnp.sum(p * dp, axis=-1, keepdims=True)
            gsk = psink * delta
            for g in range(GQA):
                h = GQA * kvh + g
                gs_ref[h:h + 1, :] -= jnp.broadcast_to(
                    jnp.sum(gsk[BLOCK * g:BLOCK * (g + 1)], axis=0, keepdims=True), (1, 128))
            ds = (p * (dp - delta)).astype(MX)
            gv = _dot(p.astype(MX), dos, TN)
            gkn = _dot(ds, qs, TN)
            gqs = _dot(ds, kn)
            for g in range(GQA):
                h = GQA * kvh + g
                gq_ref[:, HEAD_DIM * h:HEAD_DIM * (h + 1)] = gqs[BLOCK * g:BLOCK * (g + 1)]
            for b in range(3):
                dkv_ref[b, :, HEAD_DIM * kvh:HEAD_DIM * (kvh + 1)] = gkn[BLOCK * b:BLOCK * (b + 1)]
                dkv_ref[b, :, KV_WIDTH + HEAD_DIM * kvh:KV_WIDTH + HEAD_DIM * (kvh + 1)] = gv[BLOCK * b:BLOCK * (b + 1)]

    return pl.pallas_call(
        body, grid=(nb,),
        in_specs=_attn_specs(nb) + [pl.BlockSpec((BLOCK, ATT_WIDTH), lambda i: (i, 0))],
        out_specs=[pl.BlockSpec((BLOCK, ATT_WIDTH), lambda i: (i, 0)),
                   pl.BlockSpec((3, BLOCK, 2 * KV_WIDTH), lambda i: (0, i, 0)),
                   pl.BlockSpec((ATT_HEADS, 128), lambda i: (0, 0))],
        out_shape=[SDS((s, ATT_WIDTH), f32), SDS((3, s, 2 * KV_WIDTH), f32), SDS((ATT_HEADS, 128), f32)],
        compiler_params=_cp("arbitrary"), name=name)(qn, kv, kv, kv, sink, gmix)


def gz_assemble(gqs, dkv, z, q_gain, k_gain, eq, ek, gu_f, gu_r, gy, d_skip, *, name):
    s = z.shape[0]
    nb = s // BLOCK

    def norm_bwd(t_in, g_out, gain_ref, e_ref):
        r = lax.rsqrt(_head_mean(t_in * t_in, e_ref) + EPS)
        hat = t_in * r
        t = g_out * gain_ref[...]
        return r * (t - hat * _head_mean(t * hat, e_ref)), g_out * hat

    def body(gq_ref, d0_ref, d1_ref, d2_ref, z_ref, qg_ref, kg_ref, eq_ref, ek_ref, guf_ref, gur_ref, gy_ref, ds_ref,
             gz_ref, gqg_ref, gkg_ref, gd_ref):
        i = pl.program_id(0)

        @pl.when(i == 0)
        def _():
            gqg_ref[...] = jnp.zeros_like(gqg_ref)
            gkg_ref[...] = jnp.zeros_like(gkg_ref)
            gd_ref[...] = jnp.zeros_like(gd_ref)

        gq, gg = norm_bwd(z_ref[:, 0:ATT_WIDTH], gq_ref[...] * 0.125, qg_ref, eq_ref)
        gz_ref[:, 0:ATT_WIDTH] = gq.astype(MX)
        gqg_ref[...] += _rows8(gg)
        gkv = d1_ref[0] + jnp.where(i + 1 < nb, d0_ref[0], 0.0) + jnp.where(i >= 1, d2_ref[0], 0.0)
        gk, gg = norm_bwd(z_ref[:, ATT_WIDTH:ATT_WIDTH + KV_WIDTH], gkv[:, 0:KV_WIDTH], kg_ref, ek_ref)
        gz_ref[:, ATT_WIDTH:ATT_WIDTH + KV_WIDTH] = gk.astype(MX)
        gkg_ref[...] += _rows8(gg)
        gz_ref[:, ATT_WIDTH + KV_WIDTH:U_OFF] = gkv[:, KV_WIDTH:].astype(MX)
        gyv = gy_ref[...]
        gz_ref[:, U_OFF:IN_WIDTH] = (guf_ref[...] + gur_ref[...] + ds_ref[...] * gyv).astype(MX)
        gd_ref[...] += _rows8(gyv * z_ref[:, U_OFF:IN_WIDTH])

    row = lambda w: pl.BlockSpec((BLOCK, w), lambda i: (i, 0))
    const = lambda a: pl.BlockSpec(a.shape, lambda i: (0, 0))
    qg = jnp.tile(q_gain.reshape(1, HEAD_DIM), (1, ATT_HEADS))
    kg = jnp.tile(k_gain.reshape(1, HEAD_DIM), (1, KV_HEADS))
    return pl.pallas_call(
        body, grid=(nb,),
        in_specs=[row(ATT_WIDTH),
                  pl.BlockSpec((1, BLOCK, 2 * KV_WIDTH), lambda i: (0, jnp.minimum(i + 1, nb - 1), 0)),
                  pl.BlockSpec((1, BLOCK, 2 * KV_WIDTH), lambda i: (1, i, 0)),
                  pl.BlockSpec((1, BLOCK, 2 * KV_WIDTH), lambda i: (2, jnp.maximum(i - 1, 0), 0)),
                  row(IN_WIDTH), const(qg), const(kg), const(eq), const(ek),
                  row(SSM_WIDTH), row(SSM_WIDTH), row(SSM_WIDTH), pl.BlockSpec((1, SSM_WIDTH), lambda i: (0, 0))],
        out_specs=[row(IN_WIDTH), pl.BlockSpec((8, ATT_WIDTH), lambda i: (0, 0)),
                   pl.BlockSpec((8, KV_WIDTH), lambda i: (0, 0)), pl.BlockSpec((8, SSM_WIDTH), lambda i: (0, 0))],
        out_shape=[SDS((s, IN_WIDTH), MX), SDS((8, ATT_WIDTH), f32), SDS((8, KV_WIDTH), f32),
                   SDS((8, SSM_WIDTH), f32)],
        compiler_params=_cp("arbitrary"), name=name)(
            gqs, dkv, dkv, dkv, z, qg, kg, eq, ek, gu_f, gu_r, gy, d_skip.reshape(1, SSM_WIDTH))


def _cmul(ar, ai, xr, xi):
    return ar * xr - ai * xi, ar * xi + ai * xr


def _permute_rows(src_ref, dst_ref, nv):
    for v in range(nv):
        dst_ref[8 * v:8 * v + 8, :] = src_ref[pl.ds(v, 8, stride=nv), :]


def _unpermute_rows(val, dst_ref, nv):
    for v in range(nv):
        dst_ref[pl.ds(v, 8, stride=nv), :] = val[8 * v:8 * v + 8, :]


def _scan_chunk(x_ref, tab_ref, carry_ref, nv, rev, acc=None):
    L = TILE_ST
    order = list(range(nv - 1, -1, -1)) if rev else list(range(nv))
    a_r, a_i = tab_ref[32:40, :L], tab_ref[32:40, L:]
    pr = pi = None
    for v in order:
        rows = slice(8 * v, 8 * v + 8)
        xr, xi = x_ref[rows, :L], x_ref[rows, L:]
        if pr is not None:
            mr, mi = _cmul(a_r, a_i, pr, pi)
            xr, xi = xr + mr, xi + mi
            x_ref[rows, :L] = xr
            x_ref[rows, L:] = xi
        pr, pi = xr, xi
    er, ei = pr, pi
    row = lax.broadcasted_iota(jnp.int32, (8, L), 0)
    edge = row == (7 if rev else 0)
    sh = 7 if rev else 1
    fr = jnp.where(edge, carry_ref[:, :L], pltpu.roll(er, sh, 0))
    fi = jnp.where(edge, carry_ref[:, L:], pltpu.roll(ei, sh, 0))
    for n, k in enumerate((1, 2, 4)):
        mr, mi = tab_ref[8 * n:8 * n + 8, :L], tab_ref[8 * n:8 * n + 8, L:]
        sh = (8 - k) if rev else k
        rr, ri = pltpu.roll(fr, sh, 0), pltpu.roll(fi, sh, 0)
        fr, fi = fr + mr * rr - mi * ri, fi + mr * ri + mi * rr
    dr, di = _cmul(tab_ref[24:32, :L], tab_ref[24:32, L:], fr, fi)
    last = 0 if rev else 7
    carry_ref[:, :L] = jnp.broadcast_to((dr + er)[last:last + 1, :], (8, L))
    carry_ref[:, L:] = jnp.broadcast_to((di + ei)[last:last + 1, :], (8, L))
    qr, qi = fr, fi
    if acc is not None:
        sr, si = jnp.zeros((8, L), f32), jnp.zeros((8, L), f32)
    for v in order:
        rows = slice(8 * v, 8 * v + 8)
        trow = slice(40 + v, 41 + v)
        mr, mi = _cmul(tab_ref[trow, :L], tab_ref[trow, L:], fr, fi)
        xr, xi = x_ref[rows, :L] + mr, x_ref[rows, L:] + mi
        x_ref[rows, :L] = xr
        x_ref[rows, L:] = xi
        if acc is not None:
            gr, gi = acc[0][rows, :L], acc[0][rows, L:]
            sr, si = sr + gr * qr + gi * qi, si + gi * qr - gr * qi
            qr, qi = xr, xi
    if acc is not None:
        acc[1][:, :L] += sr
        acc[1][:, L:] += si


def ssm_fwd(z, tabs, bmat, cmat, *, rev, name, chunk, exchange=None):
    var = 2 if rev else 0
    s = z.shape[0]
    nc = s // chunk
    nv = chunk // 8
    ci = (lambda i: nc - 1 - i) if rev else (lambda i: i)

    tp = TILES_PER_STEP

    def body(*refs):
        u_refs = refs[:tp]
        tab_ref, b_ref, c_ref, y_ref, xb_ref, u_scr, x_scr, carry = refs[tp:]

        @pl.when(pl.program_id(1) == 0)
        def _():
            carry[...] = jnp.zeros_like(carry)

        for t in range(tp):
            xb_ref[0, :, 2 * TILE_ST * t:2 * TILE_ST * (t + 1)] = carry[t]
            _permute_rows(u_refs[t], u_scr.at[t], nv)
            x_scr[t] = _dot(u_scr[t].astype(MX), b_ref[t])
        for t in range(tp):
            _scan_chunk(x_scr.at[t], tab_ref.at[0, t], carry.at[t], nv, rev)
        for t in range(tp):
            _unpermute_rows(_dot(x_scr[t].astype(MX), c_ref[t]), u_scr.at[t], nv)
            y_ref[:, TILE_CH * t:TILE_CH * (t + 1)] = u_scr[t]

    u_specs = [pl.BlockSpec((chunk, TILE_CH), lambda j, i, t=t: (ci(i), U_OFF // TILE_CH + tp * j + t))
               for t in range(tp)]
    (y, xb), got = _call(
        body, grid=(SSM_TILES // tp, nc),
        in_specs=u_specs + [pl.BlockSpec((1, tp, 40 + nv, 2 * TILE_ST), lambda j, i: (var, j, 0, 0)),
                            pl.BlockSpec((tp, TILE_CH, 2 * TILE_ST), lambda j, i: (j, 0, 0)),
                            pl.BlockSpec((tp, 2 * TILE_ST, TILE_CH), lambda j, i: (j, 0, 0))],
        out_specs=[pl.BlockSpec((chunk, tp * TILE_CH), lambda j, i: (ci(i), j)),
                   pl.BlockSpec((1, 8, tp * 2 * TILE_ST), lambda j, i: (ci(i), 0, j))],
        out_shape=[SDS((s, SSM_WIDTH), f32), SDS((nc, 8, SSM_TILES * 2 * TILE_ST), f32)],
        scratch=[pltpu.VMEM((tp, chunk, TILE_CH), f32), pltpu.VMEM((tp, chunk, 2 * TILE_ST), f32),
                 pltpu.VMEM((tp, 8, 2 * TILE_ST), f32)],
        args=(*([z] * tp), tabs, bmat, cmat), sem=("parallel", "arbitrary"), name=name, exchange=exchange)
    return y, xb, got


def ssm_bwd(z, gy, xb, tabs, bmat, cmat, *, rev, name, chunk, exchange=None):
    var = 2 if rev else 0
    s = z.shape[0]
    nc = s // chunk
    nv = chunk // 8
    ci = (lambda i: i) if rev else (lambda i: nc - 1 - i)

    tp = TILES_PER_STEP
    w2 = 2 * TILE_ST

    def body(*refs):
        u_refs, gy_refs = refs[:tp], refs[tp:2 * tp]
        (xb_ref, ts_ref, ta_ref, b_ref, c_ref, gu_ref, ga_ref, gb_ref, gc_ref,
         u_scr, gy_scr, x_scr, g_scr, gcarry, xcarry) = refs[2 * tp:]

        @pl.when(pl.program_id(1) == 0)
        def _():
            gcarry[...] = jnp.zeros_like(gcarry)
            ga_ref[...] = jnp.zeros_like(ga_ref)
            gb_ref[...] = jnp.zeros_like(gb_ref)
            gc_ref[...] = jnp.zeros_like(gc_ref)

        ub, gyb = [], []
        for t in range(tp):
            _permute_rows(u_refs[t], u_scr.at[t], nv)
            _permute_rows(gy_refs[t], gy_scr.at[t], nv)
            ub.append(u_scr[t].astype(MX))
            gyb.append(gy_scr[t].astype(MX))
        for t in range(tp):
            g_scr[t] = _dot(gyb[t], c_ref[t], NT)
            x_scr[t] = _dot(ub[t], b_ref[t])
            xcarry[t] = xb_ref[0, :, w2 * t:w2 * (t + 1)]
        for t in range(tp):
            _scan_chunk(g_scr.at[t], ta_ref.at[0, t], gcarry.at[t], nv, not rev)
        for t in range(tp):
            _scan_chunk(x_scr.at[t], ts_ref.at[0, t], xcarry.at[t], nv, rev,
                        acc=(g_scr.at[t], ga_ref.at[:, pl.ds(w2 * t, w2)]))
            gb16 = g_scr[t].astype(MX)
            gb_ref[t] += _dot(ub[t], gb16, TN)
            gc_ref[t] += _dot(x_scr[t].astype(MX), gyb[t], TN)
            _unpermute_rows(_dot(gb16, b_ref[t], NT), u_scr.at[t], nv)
            gu_ref[:, TILE_CH * t:TILE_CH * (t + 1)] = u_scr[t]

    tile3 = lambda a, b: pl.BlockSpec((tp, a, b), lambda j, i: (j, 0, 0))
    u_specs = [pl.BlockSpec((chunk, TILE_CH), lambda j, i, t=t: (ci(i), U_OFF // TILE_CH + tp * j + t))
               for t in range(tp)]
    gy_specs = [pl.BlockSpec((chunk, TILE_CH), lambda j, i, t=t: (ci(i), tp * j + t)) for t in range(tp)]
    outs, got = _call(
        body, grid=(SSM_TILES // tp, nc),
        in_specs=u_specs + gy_specs + [
                  pl.BlockSpec((1, 8, tp * w2), lambda j, i: (ci(i), 0, j)),
                  pl.BlockSpec((1, tp, 40 + nv, w2), lambda j, i: (var, j, 0, 0)),
                  pl.BlockSpec((1, tp, 40 + nv, w2), lambda j, i: (var + 1, j, 0, 0)),
                  tile3(TILE_CH, w2), tile3(w2, TILE_CH)],
        out_specs=[pl.BlockSpec((chunk, tp * TILE_CH), lambda j, i: (ci(i), j)),
                   pl.BlockSpec((8, tp * w2), lambda j, i: (0, j)),
                   tile3(TILE_CH, w2), tile3(w2, TILE_CH)],
        out_shape=[SDS((s, SSM_WIDTH), f32), SDS((8, SSM_TILES * w2), f32),
                   SDS((SSM_TILES, TILE_CH, w2), f32), SDS((SSM_TILES, w2, TILE_CH), f32)],
        scratch=[pltpu.VMEM((tp, chunk, TILE_CH), f32), pltpu.VMEM((tp, chunk, TILE_CH), f32),
                 pltpu.VMEM((tp, chunk, w2), f32), pltpu.VMEM((tp, chunk, w2), f32),
                 pltpu.VMEM((tp, 8, w2), f32), pltpu.VMEM((tp, 8, w2), f32)],
        args=(*([z] * tp), *([gy] * tp), xb, tabs, tabs, bmat, cmat), sem=("parallel", "arbitrary"),
        name=name, exchange=exchange)
    return (*outs, got)


GELU_K = math.sqrt(2.0 / math.pi)


def _gelu(y):
    return 0.5 * y * (1.0 + jnp.tanh(GELU_K * (y + 0.044715 * (y * y * y))))


def _gelu_grad(y):
    t = jnp.tanh(GELU_K * (y + 0.044715 * (y * y * y)))
    return 0.5 * (1.0 + t) + 0.5 * y * (1.0 - t * t) * (GELU_K * (1.0 + 3.0 * 0.044715 * (y * y)))


def glu_fwd(y_f, y_r, z, att, d_skip, w_glu, l, *, name, tm):
    s = z.shape[0]
    nblk, cb = w_glu.shape[1], w_glu.shape[3]

    def body(yf_ref, yr_ref, ua_ref, ub_ref, att_ref, d_ref, w_ref, y_ref, gg_ref, mix_ref):
        u = jnp.concatenate([ua_ref[...], ub_ref[...]], axis=1)
        y = d_ref[...] * u + yf_ref[...] + yr_ref[...]
        y_ref[...] = y
        yg = _gelu(y).astype(MX)
        for b in range(nblk):
            gg_ref[:, cb * b:cb * (b + 1)] = _dot(yg, w_ref[0, b])
        mix_ref[:, 0:ATT_WIDTH] = att_ref[...]
        mix_ref[:, ATT_WIDTH:] = (gg_ref[:, :SSM_WIDTH] * jax.nn.sigmoid(gg_ref[:, SSM_WIDTH:])).astype(MX)

    return pl.pallas_call(
        body, grid=(s // tm,),
        in_specs=[_row_spec(tm, SSM_WIDTH), _row_spec(tm, SSM_WIDTH),
                  pl.BlockSpec((tm, SSM_WIDTH // 2), lambda i: (i, U_OFF // (SSM_WIDTH // 2))),
                  pl.BlockSpec((tm, SSM_WIDTH // 2), lambda i: (i, U_OFF // (SSM_WIDTH // 2) + 1)),
                  _row_spec(tm, ATT_WIDTH), pl.BlockSpec((1, SSM_WIDTH), lambda i: (0, 0)), _layer_spec(w_glu, l)],
        out_specs=[_row_spec(tm, SSM_WIDTH), _row_spec(tm, 2 * SSM_WIDTH), _row_spec(tm, D_MODEL)],
        out_shape=[SDS((s, SSM_WIDTH), f32), SDS((s, 2 * SSM_WIDTH), f32), SDS((s, D_MODEL), MX)],
        compiler_params=_cp("parallel"), name=name)(y_f, y_r, z, z, att, d_skip.reshape(1, SSM_WIDTH), w_glu)


def glu_bwd(gmix, gg, ypre, w_glu, l, *, name, tm):
    s = gg.shape[0]
    nblk, cb = w_glu.shape[1], w_glu.shape[3]

    def body(gm_ref, gg_ref, y_ref, w_ref, ggg_ref, yg_ref, gy_ref):
        gs = gm_ref[...]
        val, gate = gg_ref[:, :SSM_WIDTH], gg_ref[:, SSM_WIDTH:]
        sg = jax.nn.sigmoid(gate)
        ggg_ref[:, :SSM_WIDTH] = (gs * sg).astype(MX)
        ggg_ref[:, SSM_WIDTH:] = (gs * val * sg * (1.0 - sg)).astype(MX)
        y = y_ref[...]
        yg_ref[...] = _gelu(y).astype(MX)
        gyg = _dot(ggg_ref[:, 0:cb], w_ref[0, 0], NT)
        for b in range(1, nblk):
            gyg = gyg + _dot(ggg_ref[:, cb * b:cb * (b + 1)], w_ref[0, b], NT)
        gy_ref[...] = gyg * _gelu_grad(y)

    return pl.pallas_call(
        body, grid=(s // tm,),
        in_specs=[pl.BlockSpec((tm, SSM_WIDTH), lambda i: (i, 1)), _row_spec(tm, 2 * SSM_WIDTH),
                  _row_spec(tm, SSM_WIDTH), _layer_spec(w_glu, l)],
        out_specs=[_row_spec(tm, 2 * SSM_WIDTH), _row_spec(tm, SSM_WIDTH), _row_spec(tm, SSM_WIDTH)],
        out_shape=[SDS((s, 2 * SSM_WIDTH), MX), SDS((s, SSM_WIDTH), MX), SDS((s, SSM_WIDTH), f32)],
        compiler_params=_cp("parallel"), name=name)(gmix, gg, ypre, w_glu)


def loss_grad(y, target, *, name, tm):
    s, d = y.shape

    def body(y_ref, t_ref, g_ref, g16_ref, l_ref):
        @pl.when(pl.program_id(0) == 0)
        def _():
            l_ref[...] = jnp.zeros_like(l_ref)

        e = y_ref[...] - t_ref[...]
        g = e * (1.0 / d)
        g_ref[...] = g
        g16_ref[...] = g.astype(MX)
        l_ref[...] += _rows8(e * e)

    row = pl.BlockSpec((tm, d), lambda i: (i, 0))
    return pl.pallas_call(
        body, grid=(s // tm,), in_specs=[row, row],
        out_specs=[row, row, pl.BlockSpec((8, d), lambda i: (0, 0))],
        out_shape=[SDS((s, d), f32), SDS((s, d), MX), SDS((8, d), f32)],
        compiler_params=_cp("arbitrary"), name=name)(y, target)


def _row_tile(rows, cols):
    tr = rows
    while tr * cols > 256 * 1024 and tr % 16 == 0:
        tr //= 2
    return tr


def _elementwise(fn, ins, n_out, *, name, out_dtype=f32):
    shape = ins[0].shape
    cols = shape[-1]
    ins2 = [a.reshape(-1, cols) for a in ins]
    rows = ins2[0].shape[0]
    tr = _row_tile(rows, cols)

    def body(*refs):
        outs = fn(*[r[...] for r in refs[:len(ins)]])
        for o_ref, o in zip(refs[len(ins):], outs):
            o_ref[...] = o.astype(out_dtype)

    spec = pl.BlockSpec((tr, cols), lambda i: (i, 0))
    outs = pl.pallas_call(
        body, grid=(rows // tr,), in_specs=[spec] * len(ins), out_specs=[spec] * n_out,
        out_shape=[SDS((rows, cols), out_dtype)] * n_out, compiler_params=_cp("parallel"), name=name)(*ins2)
    return [o.reshape(shape) for o in outs]


def _adamw_math(w, g, m, v):
    m = ADAM_B1 * m + (1.0 - ADAM_B1) * g
    v = ADAM_B2 * v + (1.0 - ADAM_B2) * (g * g)
    m_hat = m / (1.0 - ADAM_B1 ** ADAM_STEP)
    v_hat = v / (1.0 - ADAM_B2 ** ADAM_STEP)
    delta = -ADAM_LR * (m_hat / (jnp.sqrt(v_hat) + ADAM_EPS) + ADAM_WD * w)
    return delta, m, v


def adamw(w, g, m, v, *, name):
    return _elementwise(_adamw_math, [w, g, m, v], 3, name=name)


SMEM = pl.BlockSpec(memory_space=pltpu.SMEM)


def _core_index():
    return lax.axis_index("c").astype(jnp.int32).reshape(1)


def adamw_halves(w, own, sib, m, v, *, name):
    depth, r, cols = w.shape
    h = r // 2
    tr = _row_tile(h, cols)
    quad = lambda a: a.reshape(depth, 2, h, cols)

    def body(c_ref, w_ref, own_ref, sib_ref, m_ref, v_ref, g_ref, d_ref, mo_ref, vo_ref):
        g = jnp.where(pl.program_id(1) == c_ref[0], own_ref[0], sib_ref[0])
        g_ref[0, 0] = g
        d_ref[0, 0], mo_ref[0, 0], vo_ref[0, 0] = _adamw_math(w_ref[0, 0], g, m_ref[0, 0], v_ref[0, 0])

    full = pl.BlockSpec((1, 1, tr, cols), lambda l, j, i: (l, j, i, 0))
    part = pl.BlockSpec((1, tr, cols), lambda l, j, i: (l, i, 0))
    outs = pl.pallas_call(
        body, grid=(depth, 2, h // tr), in_specs=[SMEM, full, part, part, full, full], out_specs=[full] * 4,
        out_shape=[SDS((depth, 2, h, cols), f32)] * 4,
        compiler_params=_cp("parallel", "parallel", "parallel"), name=name)(
            _core_index(), quad(w), own, sib, quad(m), quad(v))
    return [o.reshape(depth, r, cols) for o in outs]


def add_own_half(g4, recv, *, name):
    _, _, h, cols = g4.shape
    tr = _row_tile(h, cols)

    def body(c_ref, g_ref, r_ref, o_ref):
        own = jnp.where(c_ref[0] == 0, g_ref[0, 0], g_ref[0, 1])
        o_ref[0] = (own + r_ref[0]).astype(WIRE)

    part = pl.BlockSpec((1, tr, cols), lambda s, i: (s, i, 0))
    return pl.pallas_call(
        body, grid=(4, h // tr),
        in_specs=[SMEM, pl.BlockSpec((1, 2, tr, cols), lambda s, i: (s, 0, i, 0)), part], out_specs=part,
        out_shape=SDS((4, h, cols), WIRE), compiler_params=_cp("parallel", "parallel"), name=name)(
            _core_index(), g4, recv)


def _chip_index():
    return (2 * lax.axis_index("x") + lax.axis_index("y")).astype(jnp.int32).reshape(1)


def sum_pieces(sums, got, *, name, into, layer):
    _, h, cols = sums.shape
    tr = _row_tile(h, cols)

    def body(me_ref, s_ref, g_ref, stack_ref, o_ref):
        del stack_ref
        own = s_ref[0]
        for s in range(1, 4):
            own = jnp.where(me_ref[0] == s, s_ref[s], own)
        o_ref[0] = ((own.astype(f32) + g_ref[0].astype(f32)) + g_ref[1].astype(f32)) + g_ref[2].astype(f32)

    return pl.pallas_call(
        body, grid=(h // tr,),
        in_specs=[SMEM, pl.BlockSpec((4, tr, cols), lambda i: (0, i, 0)),
                  pl.BlockSpec((3, tr, cols), lambda i: (0, i, 0)), ANY],
        out_specs=pl.BlockSpec((1, tr, cols), lambda i: (layer, i, 0)),
        out_shape=SDS(into.shape, f32), input_output_aliases={3: 0},
        compiler_params=_cp("parallel"), name=name)(_chip_index(), sums, got, into)


def sum4(a, *, name, into=None, layer=0):
    shape = a.shape[1:]
    cols = shape[-1]
    a2 = a.reshape(4, -1, cols)
    rows = a2.shape[1]
    tr = _row_tile(rows, cols)

    def body(*refs):
        a_ref, o_ref = refs[0], refs[-1]
        tot = ((a_ref[0].astype(f32) + a_ref[1].astype(f32)) + a_ref[2].astype(f32)) + a_ref[3].astype(f32)
        if into is None:
            o_ref[...] = tot
        else:
            o_ref[0] = tot

    in_spec = pl.BlockSpec((4, tr, cols), lambda i: (0, i, 0))
    if into is None:
        out = pl.pallas_call(
            body, grid=(rows // tr,), in_specs=[in_spec], out_specs=pl.BlockSpec((tr, cols), lambda i: (i, 0)),
            out_shape=SDS((rows, cols), f32), compiler_params=_cp("parallel"), name=name)(a2)
        return out.reshape(shape)
    stack = into.reshape(into.shape[0], rows, cols)
    out = pl.pallas_call(
        body, grid=(rows // tr,), in_specs=[in_spec, ANY],
        out_specs=pl.BlockSpec((1, tr, cols), lambda i: (layer, i, 0)),
        out_shape=SDS(stack.shape, f32), input_output_aliases={1: 0},
        compiler_params=_cp("parallel"), name=name)(a2, stack)
    return out.reshape(into.shape)


ANY = pl.BlockSpec(memory_space=pl.ANY)


def _chip_copies(ins, outs, send, recv, bcast):
    x, y, c = lax.axis_index("x"), lax.axis_index("y"), lax.axis_index("c")
    me = 2 * x + y
    copies = []
    for k in range(len(ins)):
        for j, (px, py) in enumerate(((1 - x, y), (x, 1 - y), (1 - x, 1 - y))):
            copies.append(pltpu.make_async_remote_copy(
                src_ref=ins[k] if bcast[k] else ins[k].at[2 * px + py],
                dst_ref=outs[k].at[me] if bcast[k] else outs[k].at[j],
                send_sem=send.at[4 * k + j], recv_sem=recv.at[4 * k + j],
                device_id=(px, py, c), device_id_type=MESH))
        if bcast[k]:
            copies.append(pltpu.make_async_remote_copy(
                src_ref=ins[k], dst_ref=outs[k].at[me], send_sem=send.at[4 * k + 3], recv_sem=recv.at[4 * k + 3],
                device_id=(x, y, 1 - c), device_id_type=MESH))
    return copies


def chip_exchange(arrs, bcast, *, name):
    n = len(arrs)

    def body(*refs):
        copies = _chip_copies(refs[:n], refs[n:2 * n], refs[2 * n], refs[2 * n + 1], bcast)
        for cp in copies:
            cp.start()
        for cp in copies:
            cp.wait()

    return pl.pallas_call(
        body, in_specs=[ANY] * n, out_specs=[ANY] * n,
        out_shape=[SDS((4,) + tuple(a.shape) if b else (3,) + tuple(a.shape[1:]), a.dtype)
                   for a, b in zip(arrs, bcast)],
        scratch_shapes=[pltpu.SemaphoreType.DMA((4 * n,)), pltpu.SemaphoreType.DMA((4 * n,))],
        name=name)(*arrs)


def gather_weights(shards, *, name):
    n = len(shards)
    hd = shards[0].shape[0] // 2

    def body(*refs):
        ins, outs = refs[:n], refs[n:2 * n]
        send, recv = refs[2 * n:]
        x, y, c = lax.axis_index("x"), lax.axis_index("y"), lax.axis_index("c")
        me = 2 * x + y
        chips = ((1 - x, y), (x, 1 - y), (1 - x, 1 - y))
        mine, theirs = pl.ds(c * hd, hd), pl.ds((1 - c) * hd, hd)

        def ici(k, j, src, dst):
            px, py = chips[j]
            return pltpu.make_async_remote_copy(src_ref=src, dst_ref=dst, send_sem=send.at[7 * k + j],
                                                recv_sem=recv.at[7 * k + j], device_id=(px, py, c),
                                                device_id_type=MESH)

        def d2d(k, j, src, dst):
            return pltpu.make_async_remote_copy(src_ref=src, dst_ref=dst, send_sem=send.at[7 * k + 3 + j],
                                                recv_sem=recv.at[7 * k + 3 + j], device_id=(x, y, 1 - c),
                                                device_id_type=MESH)

        own, sent = [], []
        for k in range(n):
            own.append(d2d(k, 3, ins[k], outs[k].at[:, me]))
            own[-1].start()
            for j in range(3):
                sent.append(ici(k, j, ins[k].at[mine], outs[k].at[mine, me]))
                sent[-1].start()
        for k in range(n):
            for j, (px, py) in enumerate(chips):
                landed = outs[k].at[mine, 2 * px + py]
                ici(k, j, landed, landed).wait_recv()
                sent.append(d2d(k, j, landed, landed))
                sent[-1].start()
        for k in range(n):
            for j, (px, py) in enumerate(chips):
                other = outs[k].at[theirs, 2 * px + py]
                d2d(k, j, other, other).wait_recv()
        for cp in sent:
            cp.wait_send()
        for cp in own:
            cp.wait()

    return pl.pallas_call(
        body, in_specs=[ANY] * n, out_specs=[ANY] * n,
        out_shape=[SDS((a.shape[0], 4) + tuple(a.shape[1:]), a.dtype) for a in shards],
        scratch_shapes=[pltpu.SemaphoreType.DMA((7 * n,)), pltpu.SemaphoreType.DMA((7 * n,))],
        name=name)(*shards)


def _sibling_copies(ins, outs, send, recv, half):
    x, y, c = lax.axis_index("x"), lax.axis_index("y"), lax.axis_index("c")
    return [pltpu.make_async_remote_copy(
        src_ref=ins[k].at[:, 1 - c] if half[k] else ins[k], dst_ref=outs[k], send_sem=send.at[k],
        recv_sem=recv.at[k], device_id=(x, y, 1 - c), device_id_type=MESH) for k in range(len(ins))]


def sibling_exchange(arrs, half, *, name):
    n = len(arrs)
    piece = [(a.shape[0],) + a.shape[2:] if h else a.shape for a, h in zip(arrs, half)]

    def body(*refs):
        copies = _sibling_copies(refs[:n], refs[n:2 * n], refs[2 * n], refs[2 * n + 1], half)
        for cp in copies:
            cp.start()
        for cp in copies:
            cp.wait()

    return pl.pallas_call(
        body, in_specs=[ANY] * n, out_specs=[ANY] * n,
        out_shape=[SDS(tuple(p), a.dtype) for p, a in zip(piece, arrs)],
        scratch_shapes=[pltpu.SemaphoreType.DMA((n,)), pltpu.SemaphoreType.DMA((n,))],
        name=name)(*arrs)


def ssm_discretize(lam_re, lam_im, log_dt, b_re, b_im, c_re, c_im):
    dt = jnp.exp(log_dt)[..., None]
    mag = jnp.exp(lam_re * dt)
    abr = mag * jnp.cos(lam_im * dt)
    abi = mag * jnp.sin(lam_im * dt)
    den = lam_re * lam_re + lam_im * lam_im
    zr = ((abr - 1.0) * lam_re + abi * lam_im) / den
    zi = (abi * lam_re - (abr - 1.0) * lam_im) / den
    bbr = zr[..., None] * b_re - zi[..., None] * b_im
    bbi = zr[..., None] * b_im + zi[..., None] * b_re
    eye = jnp.eye(8, dtype=f32)
    bb = jnp.stack([bbr, bbi], axis=1).reshape(2, 2, SSM_TILES, 8, SSM_STATE, SSM_GROUP)
    bmat = jnp.einsum('dqjgph,gk->djghqkp', bb, eye).reshape(2, SSM_TILES, TILE_CH, 2 * TILE_ST)
    cc = jnp.stack([c_re, -c_im], axis=1).reshape(2, 2, SSM_TILES, 8, SSM_GROUP, SSM_STATE)
    cmat = jnp.einsum('dqjghp,gk->djqkpgh', cc, eye).reshape(2, SSM_TILES, 2 * TILE_ST, TILE_CH)
    n = SSM_GROUPS * SSM_STATE
    return abr.reshape(2, n), abi.reshape(2, n), bmat, cmat


SCAN_REV = (False, True, True, False)


def scan_tables_all(ar, ai, *, nv, name):
    a8 = jnp.stack([ar[0], ai[0], ar[0], -ai[0], ar[1], ai[1], ar[1], -ai[1]])
    n_state = SSM_TILES * TILE_ST

    def body(a_ref, o_ref):
        row = lax.broadcasted_iota(jnp.int32, (8, n_state), 0)

        def put(i, rows, re, im):
            for t in range(SSM_TILES):
                o_ref[i, t, rows, 0:TILE_ST] = re[:, TILE_ST * t:TILE_ST * (t + 1)]
                o_ref[i, t, rows, TILE_ST:] = im[:, TILE_ST * t:TILE_ST * (t + 1)]

        for i, rev in enumerate(SCAN_REV):
            a_r, a_i = a_ref[2 * i:2 * i + 1, :], a_ref[2 * i + 1:2 * i + 2, :]
            pr, pi = a_r, a_i
            for v in range(nv):
                dst = 40 + (nv - 1 - v if rev else v)
                put(i, slice(dst, dst + 1), pr, pi)
                if v + 1 < nv:
                    pr, pi = _cmul(a_r, a_i, pr, pi)
            big = (pr, pi)
            put(i, slice(24, 32), jnp.broadcast_to(big[0], (8, n_state)), jnp.broadcast_to(big[1], (8, n_state)))
            put(i, slice(32, 40), jnp.broadcast_to(a_r, (8, n_state)), jnp.broadcast_to(a_i, (8, n_state)))
            for n, k in enumerate((1, 2, 4)):
                cond = (row <= 7 - k) if rev else (row >= k)
                put(i, slice(8 * n, 8 * n + 8), jnp.where(cond, big[0], 0.0), jnp.where(cond, big[1], 0.0))
                big = _cmul(*big, *big)

    return pl.pallas_call(
        body, out_shape=SDS((4, SSM_TILES, 40 + nv, 2 * TILE_ST), f32),
        compiler_params=pltpu.CompilerParams(vmem_limit_bytes=VMEM_LIMIT), name=name)(a8)


def _tile_a(ga):
    t = ga.sum(axis=0).reshape(SSM_TILES, 2, TILE_ST)
    return t[:, 0].reshape(-1), t[:, 1].reshape(-1)


SMALL = ('norm1', 'q_gain', 'k_gain', 'sink', 'lam_re', 'lam_im', 'log_dt', 'b_re', 'b_im', 'c_re', 'c_im',
         'd_skip', 'norm2')
BIG = ('w_in', 'w_glu', 'w_out', 'w_ff1', 'w_ff2')
WEIGHTS = ('norm1', 'w_in', 'q_gain', 'k_gain', 'sink', 'lam_re', 'lam_im', 'log_dt', 'b_re', 'b_im', 'c_re',
           'c_im', 'd_skip', 'w_glu', 'w_out', 'norm2', 'w_ff1', 'w_ff2')


def _chunk(s):
    return min(512, s)


HOSTS_FIRST = {'attn': (("own", 'w_glu'), ("own", 'w_out'), ("next", 'w_in'), ("next", 'w_glu'), ("next", 'w_out')),
               'ssm0': (("own", 'w_ff1'),), 'ssm1': (("own", 'w_ff2'),),
               'ff1': (("next", 'w_ff1'),), 'ff2': (("next", 'w_ff2'),)}
HOSTS_LATER = {'attn': (("next", 'w_in'), ("next", 'w_glu'), ("next", 'w_out')),
               'ssm0': (("next", 'w_ff1'),), 'ssm1': (("next", 'w_ff2'),), 'ff1': (), 'ff2': ()}


def layer_forward(l, x, p, wb, li, own=None, nxt=None):
    s = x.shape[0]
    tm = min(512, s)
    sv = {}
    wb = dict(wb)
    src = {"own": own or {}, "next": nxt or {}}
    plan = {h: [e for e in es if e[1] in src[e[0]]] for h, es in (HOSTS_FIRST if own else HOSTS_LATER).items()}
    fetched = {}

    def hosted(host):
        es = plan[host]
        return ("chips", [src[w][k] for w, k in es], [True] * len(es)) if es else None

    def landed(host, got):
        for (w, k), g4 in zip(plan[host], got):
            if w == "own":
                wb[k] = layout_one(k, g4[None])
            else:
                fetched[k] = g4

    h1, z, _ = norm_mm(x, p['norm1'], wb['w_in'], li, relu2=False, name=f"l{l}_in", tm=tm)
    eq, ek = head_mean_matrix(ATT_WIDTH), head_mean_matrix(KV_WIDTH)
    qn, kv = qk_prep(z, p['q_gain'], p['k_gain'], eq, ek, name=f"l{l}_qk", tm=tm)
    att, got = attn_fwd(qn, kv, p['sink'], name=f"l{l}_attn", exchange=hosted('attn'))
    landed('attn', got)
    sv.update(qn=qn, kv=kv, eq=eq, ek=ek)
    (ar, ai, bmat, cmat), disc_vjp = jax.vjp(
        ssm_discretize, p['lam_re'], p['lam_im'], p['log_dt'], p['b_re'], p['b_im'], p['c_re'], p['c_im'])
    bmat16, cmat16 = bmat.astype(MX), cmat.astype(MX)
    ys, xbs = [], []
    tabs = scan_tables_all(ar, ai, nv=_chunk(s) // 8, name=f"l{l}_tabs")
    for d, rev in enumerate((False, True)):
        y_d, xb_d, got = ssm_fwd(z, tabs, bmat16[d], cmat16[d], rev=rev, name=f"l{l}_ssm{d}", chunk=_chunk(s),
                                 exchange=hosted(f'ssm{d}'))
        landed(f'ssm{d}', got)
        ys.append(y_d)
        xbs.append(xb_d)
    ypre, gg, mix = glu_fwd(ys[0], ys[1], z, att, p['d_skip'], wb['w_glu'], li, name=f"l{l}_glu", tm=tm)
    x1, _ = mm_res(mix, wb['w_out'], li, x, name=f"l{l}_out", tm=tm)
    h2, a2, got = norm_mm(x1, p['norm2'], wb['w_ff1'], li, relu2=True, name=f"l{l}_ff1", tm=tm, exchange=hosted('ff1'))
    landed('ff1', got)
    x2, got = mm_res(a2, wb['w_ff2'], li, x1, name=f"l{l}_ff2", tm=tm, exchange=hosted('ff2'))
    landed('ff2', got)
    sv.update(x=x, h1=h1, z=z, xbs=xbs, tabs=tabs, bmat16=bmat16, cmat16=cmat16, disc_vjp=disc_vjp,
              ypre=ypre, gg=gg, mix=mix, x1=x1, h2=h2, a2=a2, wb=wb)
    return x2, sv, fetched


def layer_backward(l, gx2, gx2h, p, wb, li, sv, pend=None):
    s = gx2.shape[0]
    tm = min(512, s)
    ts = min(2048, s)
    g = {}
    extra =[pend[2]] if pend and pend[2] is not None else []
    gf, got = mm_nt(gx2h, wb['w_ff2'], li, name=f"l{l}_bff2", tm=tm, a2=sv['a2'],
                    exchange=pend and ("sibling", pend[1] + extra, [True] * len(pend[1]) + [False] * len(extra)))
    sums = pend and [add_own_half(a, b, name=f"l{l}_radd_{k}") for k, a, b in zip(BIG, pend[1], got)]
    if extra:
        sums.append(_elementwise(lambda a, b: (a + b,), [extra[0], got[-1]], 1, name=f"l{l}_radd_small")[0])
    g['w_ff2'] = mm_tn(sv['a2'], gx2h, name=f"l{l}_wff2", tk=1024, tn=1024, ts=ts).reshape(4, D_FF // 4, D_MODEL)
    gx1, gx1h, gn2 = mm_nt_norm(gf, wb['w_ff1'], li, sv['x1'], p['norm2'], gx2, name=f"l{l}_bff1", tm=tm)
    g['norm2'] = gn2.sum(axis=0)
    g['w_ff1'] = mm_tn(sv['h2'], gf, name=f"l{l}_wff1", tk=1024, tn=1024, ts=ts, chip_major=True)
    gmix, _ = mm_nt(gx1h, wb['w_out'], li, name=f"l{l}_bout", tm=tm)
    g['w_out'] = mm_tn(sv['mix'], gx1h, name=f"l{l}_wout", tk=1024, tn=1024, ts=ts).reshape(4, D_MODEL // 4, D_MODEL)
    ggg, yg, gy = glu_bwd(gmix, sv['gg'], sv['ypre'], wb['w_glu'], li, name=f"l{l}_bglu", tm=tm)
    g['w_glu'] = mm_tn(yg, ggg, name=f"l{l}_wglu", tk=512, tn=256, ts=ts, chip_major=True)
    gus, gas, gbs, gcs = [], [], [], []
    for d, rev in enumerate((False, True)):
        gu_d, ga_d, gb_d, gc_d, got = ssm_bwd(
            sv['z'], gy, sv['xbs'][d], sv['tabs'], sv['bmat16'][d], sv['cmat16'][d], rev=rev,
            name=f"l{l}_bssm{d}", chunk=_chunk(s),
            exchange=(pend and d == 0) and ("chips", sums, [False] * len(BIG) + [True] * len(extra)) or None)
        if pend and d == 0:
            pend[0](sums, got)
        gus.append(gu_d)
        gas.append(_tile_a(ga_d))
        gbs.append(gb_d)
        gcs.append(gc_d)
    gar = jnp.stack([gas[0][0], gas[1][0]])
    gai = jnp.stack([gas[0][1], gas[1][1]])
    (g['lam_re'], g['lam_im'], g['log_dt'], g['b_re'], g['b_im'], g['c_re'], g['c_im']) = sv['disc_vjp'](
        (gar, gai, jnp.stack(gbs), jnp.stack(gcs)))
    gqs, dkv, gsk = attn_bwd(sv['qn'], sv['kv'], gmix, p['sink'], name=f"l{l}_battn")
    g['sink'] = gsk[:, 0]
    gz, gqg, gkg, gd = gz_assemble(gqs, dkv, sv['z'], p['q_gain'], p['k_gain'], sv['eq'], sv['ek'], gus[0], gus[1],
                                   gy, p['d_skip'], name=f"l{l}_gz")
    g['q_gain'] = gqg.sum(axis=0).reshape(ATT_HEADS, HEAD_DIM).sum(axis=0)
    g['k_gain'] = gkg.sum(axis=0).reshape(KV_HEADS, HEAD_DIM).sum(axis=0)
    g['d_skip'] = gd.sum(axis=0)
    gx, gxh, gn1 = mm_nt_norm(gz, wb['w_in'], li, sv['x'], p['norm1'], gx1, name=f"l{l}_bin", tm=tm)
    g['norm1'] = gn1.sum(axis=0)
    gw_in = mm_tn(sv['h1'], gz, name=f"l{l}_win", tk=1024, tn=640, ts=ts)
    g['w_in'] = gw_in.reshape(D_MODEL, 4, IN_WIDTH // 4).transpose(1, 0, 2)
    return gx, gxh, g


def layout_one(k, g):
    depth = g.shape[0]
    if k == 'w_in':
        return g.transpose(0, 2, 1, 3).reshape(depth, D_MODEL, IN_WIDTH)
    if k == 'w_out':
        return g.reshape(depth, D_MODEL, D_MODEL)
    if k == 'w_ff2':
        return g.reshape(depth, D_FF, D_MODEL)
    return g


def stack_layouts(gathered):
    return {k: layout_one(k, g) for k, g in gathered.items()}


def local_step(x, target, small, wb):
    depth = wb['w_in'].shape[0]
    saves = []
    for l in range(depth):
        x, sv, _ = layer_forward(l, x, {k: small[k][l] for k in SMALL}, wb, l)
        saves.append(sv)
    gx, gxh, lparts = loss_grad(x, target, name="loss", tm=min(512, x.shape[0]))
    grads = [None] * depth
    for l in reversed(range(depth)):
        gx, gxh, grads[l] = layer_backward(l, gx, gxh, {k: small[k][l] for k in SMALL}, wb, l, saves[l])
    return lparts, gx, grads


def reduce_pieces(g):
    return [g[k].reshape(4, 2, g[k].shape[1] // 2, g[k].shape[2]) for k in BIG]


def reduce_chips(l, sums, got, stacks):
    return {k: sum_pieces(a, b, name=f"l{l}_rsum_{k}", into=stacks[k], layer=l) for k, a, b in zip(BIG, sums, got)}


def gather_first(shards):
    halves = [a.reshape(2, a.shape[0] // 2, a.shape[1]) for a in shards]
    got = gather_weights(halves, name="gather_first")
    return [a.transpose(1, 0, 2, 3).reshape(4, 2 * a.shape[2], a.shape[3]) for a in got]


def reduce_small(packed):
    got = sibling_exchange([packed], [False], name="small_rsib")
    pair = _elementwise(lambda a, b: (a + b,), [packed, got[0]], 1, name="small_radd")[0]
    got = chip_exchange([pair], [True], name="small_rchips")
    return sum4(got[0], name="small_rsum")


def _pack_small(tree):
    parts = []
    for k in SMALL:
        flat = tree[k].reshape(-1)
        parts.append(jnp.pad(flat, (0, (-flat.shape[0]) % 1024)).reshape(-1, 128))
    return jnp.concatenate(parts, axis=0)


def _unpack_small(packed, like):
    out, row = {}, 0
    for k in SMALL:
        n = like[k].size
        rows = -(-n // 1024) * 8
        out[k] = packed[row:row + rows].reshape(-1)[:n].reshape(like[k].shape)
        row += rows
    return out


def kernel(x, norm1, w_in, q_gain, k_gain, sink, lam_re, lam_im, log_dt, b_re, b_im, c_re, c_im, d_skip, w_glu, w_out, norm2, w_ff1, w_ff2, loss_target, m_norm1, m_w_in, m_q_gain, m_k_gain, m_sink, m_lam_re, m_lam_im, m_log_dt, m_b_re, m_b_im, m_c_re, m_c_im, m_d_skip, m_w_glu, m_w_out, m_norm2, m_w_ff1, m_w_ff2, v_norm1, v_w_in, v_q_gain, v_k_gain, v_sink, v_lam_re, v_lam_im, v_log_dt, v_b_re, v_b_im, v_c_re, v_c_im, v_d_skip, v_w_glu, v_w_out, v_norm2, v_w_ff1, v_w_ff2):
    w = dict(norm1=norm1, w_in=w_in, q_gain=q_gain, k_gain=k_gain, sink=sink, lam_re=lam_re, lam_im=lam_im,
             log_dt=log_dt, b_re=b_re, b_im=b_im, c_re=c_re, c_im=c_im, d_skip=d_skip, w_glu=w_glu, w_out=w_out,
             norm2=norm2, w_ff1=w_ff1, w_ff2=w_ff2)
    m = dict(norm1=m_norm1, w_in=m_w_in, q_gain=m_q_gain, k_gain=m_k_gain, sink=m_sink, lam_re=m_lam_re,
             lam_im=m_lam_im, log_dt=m_log_dt, b_re=m_b_re, b_im=m_b_im, c_re=m_c_re, c_im=m_c_im,
             d_skip=m_d_skip, w_glu=m_w_glu, w_out=m_w_out, norm2=m_norm2, w_ff1=m_w_ff1, w_ff2=m_w_ff2)
    v = dict(norm1=v_norm1, w_in=v_w_in, q_gain=v_q_gain, k_gain=v_k_gain, sink=v_sink, lam_re=v_lam_re,
             lam_im=v_lam_im, log_dt=v_log_dt, b_re=v_b_re, b_im=v_b_im, c_re=v_c_re, c_im=v_c_im,
             d_skip=v_d_skip, w_glu=v_w_glu, w_out=v_w_out, norm2=v_norm2, w_ff1=v_w_ff1, w_ff2=v_w_ff2)
    depth = w_in.shape[0]

    shards = {k: w[k].astype(WIRE) for k in BIG}
    small = {k: w[k] for k in SMALL}
    stacks = [{k: jnp.zeros((depth, w[k].shape[1] // 2, w[k].shape[2]), f32) for k in BIG}]

    xs = x[0]
    gathered = {'w_in': gather_first([shards['w_in'][0]])[0]}
    saves = []
    for l in range(depth):
        wb = stack_layouts({k: g[None] for k, g in gathered.items()})
        own = {k: shards[k][l] for k in BIG if k not in gathered}
        nxt = {k: shards[k][l + 1] for k in BIG} if l + 1 < depth else None
        xs, sv, gathered = layer_forward(l, xs, {k: small[k][l] for k in SMALL}, wb, 0, own, nxt)
        saves.append(sv)
    gx, gxh, lparts = loss_grad(xs, loss_target[0], name="loss", tm=min(512, xs.shape[0]))
    loss = lax.psum(0.5 * jnp.sum(lparts) / D_MODEL, ("x", "y", "c"))

    later_small = []

    def finisher(l):
        def finish(sums, got):
            stacks[0] = reduce_chips(l, sums[:len(BIG)], got[:len(BIG)], stacks[0])
            if len(got) > len(BIG):
                later_small.append(sum4(got[-1], name="later_small_rsum"))
        return finish

    grads, pend = [None] * depth, None
    for l in reversed(range(depth)):
        gx, gxh, g = layer_backward(l, gx, gxh, {k: small[k][l] for k in SMALL}, saves[l]['wb'], 0, saves[l], pend)
        grads[l] = {k: g[k] for k in SMALL}
        packed = _pack_small({k: jnp.stack([grads[j][k] for j in range(1, depth)]) for k in SMALL}) if l == 1 else None
        pend = (finisher(l), reduce_pieces(g), packed)
    got = sibling_exchange(pend[1], [True] * len(BIG), name="last_rsib")
    sums = [add_own_half(a, b, name=f"last_radd_{k}") for k, a, b in zip(BIG, pend[1], got)]
    pend[0](sums, chip_exchange(sums, [False] * len(BIG), name="last_rchips"))

    sib = sibling_exchange([stacks[0][k] for k in BIG], [False] * len(BIG), name="reduce_back")
    first_small = reduce_small(_pack_small({k: grads[0][k][None] for k in SMALL}))
    like = {k: w[k] for k in SMALL}
    g_first = _unpack_small(first_small, {k: w[k][:1] for k in SMALL})
    g_later = _unpack_small(later_small[0], {k: w[k][1:] for k in SMALL})
    gfull = {k: jnp.concatenate([g_first[k], g_later[k]], axis=0) for k in SMALL}
    gsmall = _pack_small(gfull)

    delta, new_m, new_v = {}, {}, {}
    for k, sib_k in zip(BIG, sib):
        gfull[k], delta[k], new_m[k], new_v[k] = adamw_halves(w[k], stacks[0][k], sib_k, m[k], v[k],
                                                              name=f"adamw_{k}")
    ds, ms, vs = adamw(_pack_small(like), gsmall, _pack_small({k: m[k] for k in SMALL}),
                       _pack_small({k: v[k] for k in SMALL}), name="adamw_small")
    delta.update(_unpack_small(ds, like))
    new_m.update(_unpack_small(ms, like))
    new_v.update(_unpack_small(vs, like))

    return (loss, gx[None], *[gfull[k] for k in WEIGHTS], *[delta[k] for k in WEIGHTS],
            *[new_m[k] for k in WEIGHTS], *[new_v[k] for k in WEIGHTS])
```
